```python
import math
import jax
import jax.numpy as jnp
from jax import lax
import numpy as np

D_MODEL = 1024
BATCH = 8
SEQ = 4096
DEPTH = 2

CTX_LEN = 256
GRID_W = 64
N_MOD = 6
D_FF = 4 * D_MODEL
RMS_EPS = 1e-6

GDN_HEADS = 4
GDN_DK = 64
GDN_DV = 64
GDN_CONV = 4
GDN_CHUNK = 64
GDN_W = GDN_HEADS * GDN_DV
LRU_W = 256
LRU_BLOCKS = 4
LRU_BLOCK_W = LRU_W // LRU_BLOCKS
LRU_CONV = 4
LRU_C = 8.0
MLA_HEADS = 4
MLA_Q_RANK = 256
MLA_KV_RANK = 128
MLA_NOPE = 64
MLA_ROPE = 32
MLA_V = 64
MLA_W = MLA_HEADS * MLA_V
ROPE_BASE = 10000.0
ATTN_BLOCK = 128
NA_HEADS = 4
NA_DH = 64
NA_W = NA_HEADS * NA_DH
NA_WIN_ROWS = 8
NA_WIN_COLS = 16
NA_QBLOCK = 32
N_BRANCH = 4
BRANCH_W = 256

MIX_SIZES = (3 * GDN_W, GDN_W, 2 * GDN_HEADS, 2 * GDN_HEADS, LRU_W, LRU_W, MLA_Q_RANK, MLA_KV_RANK, MLA_ROPE, 3 * NA_W)
P_GDN_QKV = 0
P_GDN_Z = 1
P_GDN_BETA = 2
P_GDN_A = 3
P_LRU_X = 4
P_LRU_Y = 5
P_MLA_Q = 6
P_MLA_KV = 7
P_MLA_KR = 8
P_NA_QKV = 9
N_MIX_COLS = sum(MIX_SIZES)
N_IN_COLS = N_MIX_COLS + N_BRANCH * D_MODEL

kernel_name = 'hybrid_gdn_rglru_mla_natten_dit_block'


def rms_norm(x, g):
    xf = x.astype(jnp.float32)
    y = xf * lax.rsqrt(jnp.mean(xf * xf, axis=-1, keepdims=True) + RMS_EPS)
    return (y * g.astype(jnp.float32)).astype(x.dtype)


def l2_normalize(x):
    return x * lax.rsqrt(jnp.sum(x * x, axis=-1, keepdims=True) + RMS_EPS)


def modulate(h, shift, scale):
    return h * (1 + scale) + shift


def heads_first(u, n_heads):
    b, t, w = u.shape
    return u.reshape(b, t, n_heads, w // n_heads).transpose(0, 2, 1, 3)


def merge_heads(o):
    b, h, t, d = o.shape
    return o.transpose(0, 2, 1, 3).reshape(b, t, h * d)


def split_mixer_cols(p):
    offs = np.cumsum(MIX_SIZES)[:-1].tolist()
    return jnp.split(p[..., :N_MIX_COLS], offs, axis=-1)


def centred_depthwise_conv(x, w):
    width, ch = w.shape
    left = width // 2
    return lax.conv_general_dilated(
        x, w.astype(x.dtype)[:, None, :], window_strides=(1,),
        padding=[(left, width - 1 - left)],
        dimension_numbers=('NWC', 'WIO', 'NWC'), feature_group_count=ch)


def axial_rope(n_tok, rot_dim):
    t = jnp.arange(n_tok)
    row = (t // GRID_W).astype(jnp.float32)
    col = (t % GRID_W).astype(jnp.float32)
    n_freq = rot_dim // 4
    inv = ROPE_BASE ** (-jnp.arange(n_freq, dtype=jnp.float32) / n_freq)
    ar = row[:, None] * inv
    ac = col[:, None] * inv
    ang = jnp.concatenate([ar, ar, ac, ac], axis=-1)
    return jnp.cos(ang), jnp.sin(ang)


def _rot_half(x):
    x1, x2 = jnp.split(x, 2, axis=-1)
    return jnp.concatenate([-x2, x1], axis=-1)


def apply_axial_rope(x, cos, sin):
    xr, xc = jnp.split(x, 2, axis=-1)
    rot = jnp.concatenate([_rot_half(xr), _rot_half(xc)], axis=-1)
    return x * cos.astype(x.dtype) + rot * sin.astype(x.dtype)


def block_attention(q, k, v, scale):
    b, h, t, dq = q.shape
    dv = v.shape[-1]
    nb = t // ATTN_BLOCK
    qb = jnp.moveaxis(q.reshape(b, h, nb, ATTN_BLOCK, dq), 2, 0)

    def attend(qi):
        s = jnp.einsum('bhqd,bhkd->bhqk', qi, k, preferred_element_type=jnp.float32) * scale
        p = jax.nn.softmax(s, axis=-1).astype(v.dtype)
        return jnp.einsum('bhqk,bhkd->bhqd', p, v)

    o = lax.map(attend, qb)
    return jnp.moveaxis(o, 0, 2).reshape(b, h, t, dv)


def chunk_gated_delta(q, k, v, g, beta, s0):
    b, h, t, dk = q.shape
    dv = v.shape[-1]
    cs = GDN_CHUNK
    n = t // cs
    q, k, v = (u.reshape(b, h, n, cs, -1) for u in (q, k, v))
    g, beta = (u.reshape(b, h, n, cs) for u in (g, beta))
    gc = jnp.cumsum(g, axis=-1)
    incl = np.tril(np.ones((cs, cs), dtype=bool))
    strict = np.tril(np.ones((cs, cs), dtype=bool), -1)
    decay = jnp.where(incl, jnp.exp(jnp.where(incl, gc[..., :, None] - gc[..., None, :], 0.0)), 0.0)
    kb = k * beta[..., None]
    lower = jnp.where(strict, jnp.einsum('bhnid,bhnjd->bhnij', kb, k) * decay, 0.0)
    eye = jnp.eye(cs, dtype=jnp.float32)
    tinv = lax.linalg.triangular_solve(eye + lower, jnp.broadcast_to(eye, lower.shape), left_side=True, lower=True)
    u = tinv @ (v * beta[..., None])
    w = tinv @ (kb * jnp.exp(gc)[..., None])
    a_intra = jnp.where(incl, jnp.einsum('bhnid,bhnjd->bhnij', q, k) * decay, 0.0)
    q_dec = q * jnp.exp(gc)[..., None]
    k_dec = k * jnp.exp(gc[..., -1:] - gc)[..., None]
    g_tot = jnp.exp(gc[..., -1])
    xs = tuple(jnp.moveaxis(z, 2, 0) for z in (q_dec, k_dec, u, w, a_intra, g_tot))

    def step(s, inp):
        q_i, k_i, u_i, w_i, a_i, gt_i = inp
        v_new = u_i - w_i @ s
        o = q_i @ s + a_i @ v_new
        s = s * gt_i[..., None, None] + jnp.einsum('bhck,bhcv->bhkv', k_i, v_new)
        return s, o

    s_final, o = lax.scan(step, s0, xs)
    return jnp.moveaxis(o, 0, 2).reshape(b, h, t, dv), s_final


def gdn_prep(qkv, beta_logit, a_logit, lp):
    b, t, _ = qkv.shape
    qkv = jax.nn.silu(centred_depthwise_conv(qkv, lp['gdn_conv_w'])).astype(jnp.float32)
    q, k, v = (heads_first(u, GDN_HEADS) for u in jnp.split(qkv, 3, axis=-1))
    q = l2_normalize(q) * (GDN_DK ** -0.5)
    k = l2_normalize(k)

    def per_dir(z):
        return z.astype(jnp.float32).reshape(b, t, 2, GDN_HEADS).transpose(2, 0, 3, 1)

    beta = jax.nn.sigmoid(per_dir(beta_logit))
    a_log = lp['gdn_a_log'].astype(jnp.float32)[:, None, :, None]
    dt_bias = lp['gdn_dt_bias'].astype(jnp.float32)[:, None, :, None]
    g = -jnp.exp(a_log) * jax.nn.softplus(per_dir(a_logit) + dt_bias)
    return q, k, v, beta, g


def gdn_bidir(q, k, v, beta, g, s0_f, s0_b):
    o_f, s_f = chunk_gated_delta(q, k, v, g[0], beta[0], s0_f)
    flip = lambda z: jnp.flip(z, axis=2)
    o_b, s_b = chunk_gated_delta(flip(q), flip(k), flip(v), flip(g[1]), flip(beta[1]), s0_b)
    return o_f + flip(o_b), s_f, s_b


def gdn_output(o, z, lp):
    b, h, t, dv = o.shape
    o = rms_norm(jnp.swapaxes(o, 1, 2), lp['gdn_norm_g'])
    o = o * jax.nn.silu(z.astype(jnp.float32)).reshape(b, t, h, dv)
    return o.reshape(b, t, h * dv).astype(z.dtype)


def gdn_mixer(pc, pl, lp, need_ctx):
    qc, kc, vc, bc, gc = gdn_prep(pc[P_GDN_QKV], pc[P_GDN_BETA], pc[P_GDN_A], lp)
    zero = jnp.zeros((qc.shape[0], GDN_HEADS, GDN_DK, GDN_DV), jnp.float32)
    oc, s_f, s_b = gdn_bidir(qc, kc, vc, bc, gc, zero, zero)
    ql, kl, vl, bl, gl = gdn_prep(pl[P_GDN_QKV], pl[P_GDN_BETA], pl[P_GDN_A], lp)
    ol, _, _ = gdn_bidir(ql, kl, vl, bl, gl, s_f, s_b)
    uc = gdn_output(oc, pc[P_GDN_Z], lp) if need_ctx else None
    return uc, gdn_output(ol, pl[P_GDN_Z], lp)


def _lin_combine(left, right):
    a_l, b_l = left
    a_r, b_r = right
    return a_l * a_r, a_r * b_l + b_r


def linear_scan(a, b, h0, reverse):
    if reverse:
        a, b = jnp.flip(a, axis=1), jnp.flip(b, axis=1)
    b = b.at[:, 0].add(a[:, 0] * h0)
    _, h = lax.associative_scan(_lin_combine, (a, b), axis=1)
    return jnp.flip(h, axis=1) if reverse else h


def lru_gate(xblk, w, bias):
    b, t = xblk.shape[:2]
    y = jnp.einsum('btni,dnij->dbtnj', xblk, w.astype(jnp.float32)).reshape(2, b, t, LRU_W)
    return jax.nn.sigmoid(y + bias.astype(jnp.float32)[:, None, None, :])


def lru_prep(xb, lp):
    xb = centred_depthwise_conv(xb, lp['lru_conv_w']) + lp['lru_conv_b']
    xf = xb.astype(jnp.float32)
    b, t, _ = xf.shape
    xblk = xf.reshape(b, t, LRU_BLOCKS, LRU_BLOCK_W)
    r = lru_gate(xblk, lp['lru_w_r'], lp['lru_b_r'])
    i = lru_gate(xblk, lp['lru_w_i'], lp['lru_b_i'])
    log_a = -LRU_C * jax.nn.softplus(-lp['lru_lambda'].astype(jnp.float32))[:, None, None, :] * r
    a = jnp.exp(log_a)
    inp = jnp.sqrt(-jnp.expm1(2.0 * log_a)) * i * xf[None]
    return a, inp


def lru_mixer(pc, pl, lp, need_ctx):
    a_c, b_c = lru_prep(pc[P_LRU_X], lp)
    zero = jnp.zeros((a_c.shape[1], LRU_W), jnp.float32)
    hc_f = linear_scan(a_c[0], b_c[0], zero, False)
    hc_b = linear_scan(a_c[1], b_c[1], zero, True)
    a_l, b_l = lru_prep(pl[P_LRU_X], lp)
    hl_f = linear_scan(a_l[0], b_l[0], hc_f[:, -1], False)
    hl_b = linear_scan(a_l[1], b_l[1], hc_b[:, 0], True)

    def out(h, y):
        return (h * jax.nn.gelu(y.astype(jnp.float32))).astype(y.dtype)

    uc = out(hc_f + hc_b, pc[P_LRU_Y]) if need_ctx else None
    return uc, out(hl_f + hl_b, pl[P_LRU_Y])


def mla_queries(q_lat, lp, rope):
    b, t, _ = q_lat.shape
    q = (rms_norm(q_lat, lp['mla_q_norm_g']) @ lp['mla_w_uq']).reshape(b, t, MLA_HEADS, MLA_NOPE + MLA_ROPE)
    q_nope, q_pe = q[..., :MLA_NOPE], q[..., MLA_NOPE:]
    if rope is not None:
        q_pe = apply_axial_rope(q_pe, rope[0][:, None, :], rope[1][:, None, :])
    return jnp.concatenate([q_nope, q_pe], axis=-1).transpose(0, 2, 1, 3)


def mla_keys_values(kv_lat, k_rope, lp, rope):
    b, t, _ = kv_lat.shape
    kv = (rms_norm(kv_lat, lp['mla_kv_norm_g']) @ lp['mla_w_ukv']).reshape(b, t, MLA_HEADS, MLA_NOPE + MLA_V)
    k_nope, v = kv[..., :MLA_NOPE], kv[..., MLA_NOPE:]
    k_pe = k_rope[:, :, None, :]
    if rope is not None:
        k_pe = apply_axial_rope(k_pe, rope[0][:, None, :], rope[1][:, None, :])
    k_pe = jnp.broadcast_to(k_pe, (b, t, MLA_HEADS, MLA_ROPE))
    k = jnp.concatenate([k_nope, k_pe], axis=-1)
    return k.transpose(0, 2, 1, 3), v.transpose(0, 2, 1, 3)


def mla_mixer(pc, pl, lp, rope, need_ctx):
    scale = (MLA_NOPE + MLA_ROPE) ** -0.5
    kc, vc = mla_keys_values(pc[P_MLA_KV], pc[P_MLA_KR], lp, None)
    kl, vl = mla_keys_values(pl[P_MLA_KV], pl[P_MLA_KR], lp, rope)
    ql = mla_queries(pl[P_MLA_Q], lp, rope)
    ol = block_attention(ql, jnp.concatenate([kc, kl], axis=2), jnp.concatenate([vc, vl], axis=2), scale)
    uc = merge_heads(block_attention(mla_queries(pc[P_MLA_Q], lp, None), kc, vc, scale)) if need_ctx else None
    return uc, merge_heads(ol)


def na_layout(rows):
    kr = min(NA_WIN_ROWS, rows)
    kc = NA_WIN_COLS
    rc = min(NA_QBLOCK + kc, GRID_W)
    n_tok = rows * GRID_W
    t = np.arange(n_tok)
    r, c = t // GRID_W, t % GRID_W
    rs = np.clip(r - kr // 2, 0, rows - kr)
    cs = np.clip(c - kc // 2, 0, GRID_W - kc)
    nblk = n_tok // NA_QBLOCK
    qr, qc, qrs, qcs = (z.reshape(nblk, NA_QBLOCK) for z in (r, c, rs, cs))
    c0 = np.minimum(qcs[:, 0], GRID_W - rc)
    key_r = qrs[:, :1] + np.repeat(np.arange(kr), rc)[None]
    key_c = c0[:, None] + np.tile(np.arange(rc), kr)[None]
    idx = (key_r * GRID_W + key_c).astype(np.int32)
    kcb = key_c[:, None, :]
    valid = (kcb >= qcs[:, :, None]) & (kcb < qcs[:, :, None] + kc)
    rel_r = np.clip(key_r[:, None, :] - qr[:, :, None] + NA_WIN_ROWS - 1, 0, 2 * NA_WIN_ROWS - 2)
    rel_c = np.clip(kcb - qc[:, :, None] + NA_WIN_COLS - 1, 0, 2 * NA_WIN_COLS - 2)
    return idx, valid, rel_r, rel_c


def neighbourhood_attention(q, k, v, ck, cv, rpb, layout):
    idx, valid, rel_r, rel_c = layout
    b, h, t, dh = q.shape
    nblk, rk = idx.shape
    scale = dh ** -0.5
    qb = q.reshape(b, h, nblk, NA_QBLOCK, dh)
    kb = k[:, :, idx]
    vb = v[:, :, idx]
    bias = rpb.astype(jnp.float32)[:, rel_r, rel_c]
    s_win = jnp.einsum('bhnqd,bhnkd->bhnqk', qb, kb, preferred_element_type=jnp.float32) * scale + bias
    s_win = jnp.where(valid, s_win, -jnp.inf)
    s_ctx = jnp.einsum('bhnqd,bhld->bhnql', qb, ck, preferred_element_type=jnp.float32) * scale
    p = jax.nn.softmax(jnp.concatenate([s_win, s_ctx], axis=-1), axis=-1).astype(v.dtype)
    o = jnp.einsum('bhnqk,bhnkd->bhnqd', p[..., :rk], vb) + jnp.einsum('bhnql,bhld->bhnqd', p[..., rk:], cv)
    return o.reshape(b, h, t, dh)


def na_mixer(pc, pl, lp, layout, need_ctx):
    qc, kc, vc = (heads_first(u, NA_HEADS) for u in jnp.split(pc[P_NA_QKV], 3, axis=-1))
    ql, kl, vl = (heads_first(u, NA_HEADS) for u in jnp.split(pl[P_NA_QKV], 3, axis=-1))
    ol = neighbourhood_attention(ql, kl, vl, kc, vc, lp['na_rpb'], layout)
    uc = merge_heads(block_attention(qc, kc, vc, NA_DH ** -0.5)) if need_ctx else None
    return uc, merge_heads(ol)


def merge_branches(branches, gate_logits, lp):
    b, t, _ = gate_logits.shape
    gates = jax.nn.sigmoid(gate_logits.reshape(b, t, N_BRANCH, D_MODEL) + lp['b_gate'])
    merged = gates[:, :, 0] * (branches[0] @ lp['w_branch'][0])
    for n in range(1, N_BRANCH):
        merged = merged + gates[:, :, n] * (branches[n] @ lp['w_branch'][n])
    return merged @ lp['w_out']


def sq_relu_mlp(h, lp):
    return jnp.square(jax.nn.relu(h @ lp['mlp_w1'])) @ lp['mlp_w2']


def hybrid_layer(xl, xc, c, c_ctx, lp, rope, layout, need_ctx):
    mod_l = jax.nn.silu(c) @ lp['mod_w'] + lp['mod_b']
    mod_c = jax.nn.silu(c_ctx) @ lp['mod_w'] + lp['mod_b']
    sh1, sc1, g1, sh2, sc2, g2 = jnp.split(mod_l[:, None, :], N_MOD, axis=-1)
    csh1, csc1, cg1, csh2, csc2, cg2 = jnp.split(mod_c, N_MOD, axis=-1)

    hl = modulate(rms_norm(xl, lp['norm1_g']), sh1, sc1)
    hc = modulate(rms_norm(xc, lp['norm1_g']), csh1, csc1)
    w_in = lp['w_in']
    pl = hl @ w_in
    pc = hc @ (w_in if need_ctx else w_in[:, :N_MIX_COLS])
    parts_l = split_mixer_cols(pl)
    parts_c = split_mixer_cols(pc)

    ua_c, ua_l = gdn_mixer(parts_c, parts_l, lp, need_ctx)
    ub_c, ub_l = lru_mixer(parts_c, parts_l, lp, need_ctx)
    uc_c, uc_l = mla_mixer(parts_c, parts_l, lp, rope, need_ctx)
    ud_c, ud_l = na_mixer(parts_c, parts_l, lp, layout, need_ctx)

    xl = xl + g1 * merge_branches((ua_l, ub_l, uc_l, ud_l), pl[..., N_MIX_COLS:], lp)
    xl = xl + g2 * sq_relu_mlp(modulate(rms_norm(xl, lp['norm2_g']), sh2, sc2), lp)
    if not need_ctx:
        return xl, None
    xc = xc + cg1 * merge_branches((ua_c, ub_c, uc_c, ud_c), pc[..., N_MIX_COLS:], lp)
    xc = xc + cg2 * sq_relu_mlp(modulate(rms_norm(xc, lp['norm2_g']), csh2, csc2), lp)
    return xl, xc


def setup_inputs(seed: int = 0) -> dict:
    keys = list(jax.random.split(jax.random.key(seed), 40))

    def nrm(shape, std):
        return std * jax.random.normal(keys.pop(), shape, jnp.float32)

    def uni(shape, lo, hi):
        return jax.random.uniform(keys.pop(), shape, jnp.float32, lo, hi)

    L = DEPTH
    dt = jnp.exp(uni((L, 2, GDN_HEADS), math.log(1e-3), math.log(1e-1)))
    lam_s = uni((L, 2, LRU_W), 0.9, 0.999) ** (1.0 / LRU_C)
    return {
        'x': nrm((BATCH, SEQ, D_MODEL), 1.0),
        'c': nrm((BATCH, D_MODEL), 1.0),
        'ctx': nrm((BATCH, CTX_LEN, D_MODEL), 1.0),
        'c_ctx': nrm((D_MODEL,), 1.0),
        'mod_w': nrm((L, D_MODEL, N_MOD * D_MODEL), 0.5 * D_MODEL ** -0.5),
        'mod_b': nrm((L, N_MOD * D_MODEL), 0.02),
        'norm1_g': 1.0 + nrm((L, D_MODEL), 0.02),
        'norm2_g': 1.0 + nrm((L, D_MODEL), 0.02),
        'w_in': nrm((L, D_MODEL, N_IN_COLS), D_MODEL ** -0.5),
        'b_gate': nrm((L, N_BRANCH, D_MODEL), 0.02),
        'gdn_conv_w': nrm((L, GDN_CONV, 3 * GDN_W), GDN_CONV ** -0.5),
        'gdn_a_log': jnp.log(uni((L, 2, GDN_HEADS), 1.0, 16.0)),
        'gdn_dt_bias': dt + jnp.log(-jnp.expm1(-dt)),
        'gdn_norm_g': 1.0 + nrm((L, GDN_DV), 0.02),
        'lru_conv_w': nrm((L, LRU_CONV, LRU_W), LRU_CONV ** -0.5),
        'lru_conv_b': nrm((L, LRU_W), 0.02),
        'lru_w_r': nrm((L, 2, LRU_BLOCKS, LRU_BLOCK_W, LRU_BLOCK_W), LRU_BLOCK_W ** -0.5),
        'lru_b_r': nrm((L, 2, LRU_W), 0.02),
        'lru_w_i': nrm((L, 2, LRU_BLOCKS, LRU_BLOCK_W, LRU_BLOCK_W), LRU_BLOCK_W ** -0.5),
        'lru_b_i': nrm((L, 2, LRU_W), 0.02),
        'lru_lambda': jnp.log(lam_s) - jnp.log1p(-lam_s),
        'mla_q_norm_g': 1.0 + nrm((L, MLA_Q_RANK), 0.02),
        'mla_w_uq': nrm((L, MLA_Q_RANK, MLA_HEADS * (MLA_NOPE + MLA_ROPE)), MLA_Q_RANK ** -0.5),
        'mla_kv_norm_g': 1.0 + nrm((L, MLA_KV_RANK), 0.02),
        'mla_w_ukv': nrm((L, MLA_KV_RANK, MLA_HEADS * (MLA_NOPE + MLA_V)), MLA_KV_RANK ** -0.5),
        'na_rpb': nrm((L, NA_HEADS, 2 * NA_WIN_ROWS - 1, 2 * NA_WIN_COLS - 1), 0.1),
        'w_branch': nrm((L, N_BRANCH, BRANCH_W, D_MODEL), BRANCH_W ** -0.5),
        'w_out': nrm((L, D_MODEL, D_MODEL), D_MODEL ** -0.5),
        'mlp_w1': nrm((L, D_MODEL, D_FF), D_MODEL ** -0.5),
        'mlp_w2': nrm((L, D_FF, D_MODEL), D_FF ** -0.5),
        'final_norm_g': 1.0 + nrm((D_MODEL,), 0.02),
    }


def reference(x, c, ctx, c_ctx, mod_w, mod_b, norm1_g, norm2_g, w_in, b_gate, gdn_conv_w, gdn_a_log,
              gdn_dt_bias, gdn_norm_g, lru_conv_w, lru_conv_b, lru_w_r, lru_b_r, lru_w_i, lru_b_i,
              lru_lambda, mla_q_norm_g, mla_w_uq, mla_kv_norm_g, mla_w_ukv, na_rpb, w_branch, w_out,
              mlp_w1, mlp_w2, final_norm_g):
    n_tok = x.shape[1]
    rows = n_tok // GRID_W
    rope = axial_rope(n_tok, MLA_ROPE)
    layout = na_layout(rows)
    xl, xc = x, ctx
    for l in range(DEPTH):
        lp = {
            'mod_w': mod_w[l], 'mod_b': mod_b[l], 'norm1_g': norm1_g[l], 'norm2_g': norm2_g[l],
            'w_in': w_in[l], 'b_gate': b_gate[l],
            'gdn_conv_w': gdn_conv_w[l], 'gdn_a_log': gdn_a_log[l], 'gdn_dt_bias': gdn_dt_bias[l],
            'gdn_norm_g': gdn_norm_g[l],
            'lru_conv_w': lru_conv_w[l], 'lru_conv_b': lru_conv_b[l], 'lru_w_r': lru_w_r[l],
            'lru_b_r': lru_b_r[l], 'lru_w_i': lru_w_i[l], 'lru_b_i': lru_b_i[l], 'lru_lambda': lru_lambda[l],
            'mla_q_norm_g': mla_q_norm_g[l], 'mla_w_uq': mla_w_uq[l], 'mla_kv_norm_g': mla_kv_norm_g[l],
            'mla_w_ukv': mla_w_ukv[l], 'na_rpb': na_rpb[l],
            'w_branch': w_branch[l], 'w_out': w_out[l], 'mlp_w1': mlp_w1[l], 'mlp_w2': mlp_w2[l],
        }
        xl, xc = hybrid_layer(xl, xc, c, c_ctx, lp, rope, layout, need_ctx=(l < DEPTH - 1))
    return rms_norm(xl, final_norm_g)
```

```python
import functools
import math

import jax
import jax.numpy as jnp
import numpy as np
from jax import lax
from jax.experimental import pallas as pl
from jax.experimental.pallas import tpu as pltpu

F32 = jnp.float32
BF16 = jnp.bfloat16

GRID_W = 64
N_MOD = 6
RMS_EPS = 1e-6
GDN_HEADS = 4
GDN_DK = 64
GDN_DV = 64
GDN_CHUNK = 64
GDN_W = GDN_HEADS * GDN_DV
LRU_W = 256
LRU_BLOCKS = 4
LRU_BLOCK_W = LRU_W // LRU_BLOCKS
LRU_C = 8.0
MLA_HEADS = 4
MLA_Q_RANK = 256
MLA_KV_RANK = 128
MLA_NOPE = 64
MLA_ROPE = 32
MLA_V = 64
MLA_SLOT = 128
ROPE_BASE = 10000.0
ATTN_BLOCK = 128
NA_HEADS = 4
NA_DH = 64
NA_W = NA_HEADS * NA_DH
NA_WIN_ROWS = 8
NA_WIN_COLS = 16
NA_QBLOCK = 32
N_BRANCH = 4
BRANCH_W = 256

_REF_COLS = {}
_off = 0
for _name, _w in (('gdn_qkv', 3 * GDN_W), ('gdn_z', GDN_W), ('gdn_beta', 2 * GDN_HEADS), ('gdn_a', 2 * GDN_HEADS),
                  ('lru_x', LRU_W), ('lru_y', LRU_W), ('mla_q', MLA_Q_RANK), ('mla_kv', MLA_KV_RANK),
                  ('mla_kr', MLA_ROPE), ('na_qkv', 3 * NA_W)):
    _REF_COLS[_name] = (_off, _w)
    _off += _w
N_MIX_COLS = _off

P_GDN_QKV = 0
P_GDN_Z = 768
P_LRU_X = 1024
P_LRU_Y = 1280
P_MLA_Q = 1536
P_NA_QKV = 1792
P_MLA_KV = 2560
P_MLA_KR = 2688
P_GDN_BA = 2816
P_MIX_END = 3072
P_GATES = 3072
P_COLS = P_GATES + N_BRANCH * 1024

VMEM_LIMIT = 48 * 1024 * 1024


def _cparams(sem):
    return pltpu.CompilerParams(dimension_semantics=sem, vmem_limit_bytes=VMEM_LIMIT)


def _arrange_w_in(w_in):
    d = w_in.shape[0]
    out = jnp.zeros((d, P_COLS), F32)
    for name, dst in (('gdn_qkv', P_GDN_QKV), ('gdn_z', P_GDN_Z), ('lru_x', P_LRU_X), ('lru_y', P_LRU_Y),
                      ('mla_q', P_MLA_Q), ('na_qkv', P_NA_QKV), ('mla_kv', P_MLA_KV), ('mla_kr', P_MLA_KR),
                      ('gdn_beta', P_GDN_BA), ('gdn_a', P_GDN_BA + 2 * GDN_HEADS)):
        o, w = _REF_COLS[name]
        out = out.at[:, dst:dst + w].set(w_in[:, o:o + w])
    out = out.at[:, P_GATES:].set(w_in[:, N_MIX_COLS:])
    return out.astype(BF16)


def _rope_perm():
    q = MLA_ROPE // 4
    src = np.zeros(MLA_ROPE, np.int32)
    sign = np.zeros(MLA_ROPE, np.float32)
    for base in (0, 2 * q):
        for d in range(q):
            src[base + d] = base + d + q
            sign[base + d] = -1.0
            src[base + q + d] = base + d
            sign[base + q + d] = 1.0
    return src, sign


def _arrange_mla(w_uq, w_ukv):
    src, sign = _rope_perm()
    hq = MLA_NOPE + MLA_ROPE
    wq = jnp.zeros((MLA_Q_RANK, 2 * MLA_HEADS * MLA_SLOT), F32)
    wk = jnp.zeros((MLA_KV_RANK, MLA_HEADS * MLA_SLOT), F32)
    wv = jnp.zeros((MLA_KV_RANK, MLA_HEADS * MLA_V), F32)
    place = np.zeros((2 * MLA_SLOT, 2 * MLA_HEADS * MLA_SLOT), np.float32)
    rot_off = MLA_HEADS * MLA_SLOT
    for h in range(MLA_HEADS):
        nope = w_uq[:, h * hq:h * hq + MLA_NOPE]
        pe = w_uq[:, h * hq + MLA_NOPE:(h + 1) * hq]
        s = h * MLA_SLOT
        wq = wq.at[:, s:s + MLA_NOPE].set(nope)
        wq = wq.at[:, s + MLA_NOPE:s + MLA_NOPE + MLA_ROPE].set(pe)
        wq = wq.at[:, rot_off + s + MLA_NOPE:rot_off + s + MLA_NOPE + MLA_ROPE].set(pe[:, src] * sign)
        wk = wk.at[:, s:s + MLA_NOPE].set(w_ukv[:, h * 128:h * 128 + MLA_NOPE])
        wv = wv.at[:, h * MLA_V:(h + 1) * MLA_V].set(w_ukv[:, h * 128 + MLA_NOPE:(h + 1) * 128])
        for d in range(MLA_ROPE):
            for half in (0, MLA_SLOT):
                place[half + d, s + MLA_NOPE + d] = 1.0
                place[half + src[d], rot_off + s + MLA_NOPE + d] = sign[d]
    return wq.astype(BF16), wk.astype(BF16), wv.astype(BF16), jnp.asarray(place, BF16)


def _rope_tables(n_tok, use_rope):
    cos = np.ones((n_tok, MLA_SLOT), np.float32)
    sin = np.zeros((n_tok, MLA_SLOT), np.float32)
    if use_rope:
        t = np.arange(n_tok)
        row = (t // GRID_W).astype(np.float32)
        col = (t % GRID_W).astype(np.float32)
        n_freq = MLA_ROPE // 4
        inv = (ROPE_BASE ** (-np.arange(n_freq, dtype=np.float32) / n_freq)).astype(np.float32)
        ar = row[:, None] * inv
        ac = col[:, None] * inv
        ang = np.concatenate([ar, ar, ac, ac], axis=-1).astype(np.float32)
        cos[:, MLA_NOPE:MLA_NOPE + MLA_ROPE] = np.cos(ang)
        sin[:, MLA_NOPE:MLA_NOPE + MLA_ROPE] = np.sin(ang)
    return jnp.asarray(cos), jnp.asarray(sin)


def _mod_kernel(c_ref, w_ref, b_ref, o_ref):
    c = c_ref[...]
    s = c * (1.0 / (1.0 + jnp.exp(-c)))
    o_ref[...] = jnp.dot(s, w_ref[...], preferred_element_type=F32) + b_ref[...]


def _modulation(cc, mod_w, mod_b):
    r, d = cc.shape
    n = mod_w.shape[1]
    tn = 1024
    return pl.pallas_call(
        _mod_kernel,
        grid=(n // tn,),
        in_specs=[pl.BlockSpec((r, d), lambda j: (0, 0)),
                  pl.BlockSpec((d, tn), lambda j: (0, j)),
                  pl.BlockSpec((1, tn), lambda j: (0, j))],
        out_specs=pl.BlockSpec((r, tn), lambda j: (0, j)),
        out_shape=jax.ShapeDtypeStruct((r, n), F32),
        compiler_params=_cparams(("arbitrary",)),
        name="modulation",
    )(cc, mod_w, mod_b.reshape(1, n))


def _inproj_kernel(x_ref, sh_ref, sc_ref, g_ref, w_ref, o_ref, h_ref):
    @pl.when(pl.program_id(2) == 0)
    def _():
        x = x_ref[...]
        y = x * lax.rsqrt(jnp.mean(x * x, axis=-1, keepdims=True) + RMS_EPS)
        y = y * g_ref[...]
        h_ref[...] = (y * (1.0 + sc_ref[...]) + sh_ref[...]).astype(BF16)

    o_ref[...] = jnp.dot(h_ref[...], w_ref[...], preferred_element_type=F32)


def _inproj(x, shift, scale, gain, w, n_cols):
    b, t, d = x.shape
    tm = min(1024, t)
    tn = 1024
    return pl.pallas_call(
        _inproj_kernel,
        grid=(b, t // tm, n_cols // tn),
        in_specs=[pl.BlockSpec((None, tm, d), lambda bi, i, j: (bi, i, 0)),
                  pl.BlockSpec((None, 1, d), lambda bi, i, j: (bi, 0, 0)),
                  pl.BlockSpec((None, 1, d), lambda bi, i, j: (bi, 0, 0)),
                  pl.BlockSpec((1, d), lambda bi, i, j: (0, 0)),
                  pl.BlockSpec((d, tn), lambda bi, i, j: (0, j))],
        out_specs=pl.BlockSpec((None, tm, tn), lambda bi, i, j: (bi, i, j)),
        out_shape=jax.ShapeDtypeStruct((b, t, n_cols), F32),
        scratch_shapes=[pltpu.VMEM((tm, d), BF16)],
        compiler_params=_cparams(("parallel", "parallel", "arbitrary")),
        name="inproj",
    )(x, shift, scale, gain, w)


def _mla_prep_kernel(ql_ref, kv_ref, cos_ref, sin_ref, gq_ref, gkv_ref, wq_ref, wk_ref, wv_ref, pl_ref,
                     q_ref, k_ref, v_ref, *, scale):
    nslot = MLA_HEADS * MLA_SLOT
    cos = jnp.concatenate([cos_ref[...]] * MLA_HEADS, axis=-1)
    sin = jnp.concatenate([sin_ref[...]] * MLA_HEADS, axis=-1)

    ql = ql_ref[...]
    qn = ql * lax.rsqrt(jnp.mean(ql * ql, axis=-1, keepdims=True) + RMS_EPS) * gq_ref[...]
    q2 = jnp.dot(qn.astype(BF16), wq_ref[...], preferred_element_type=F32)
    q = (q2[:, :nslot] * cos + q2[:, nslot:] * sin) * scale
    q_ref[...] = q.astype(BF16)

    kvkr = kv_ref[...]
    kvl = kvkr[:, :MLA_KV_RANK]
    kr = kvkr[:, MLA_KV_RANK:]
    kvn = (kvl * lax.rsqrt(jnp.mean(kvl * kvl, axis=-1, keepdims=True) + RMS_EPS) * gkv_ref[...]).astype(BF16)
    kn = jnp.dot(kvn, wk_ref[...], preferred_element_type=F32)
    v_ref[...] = jnp.dot(kvn, wv_ref[...], preferred_element_type=F32).astype(BF16)
    kr_hi = kr.astype(BF16)
    kr_lo = (kr - kr_hi.astype(F32)).astype(BF16)
    kr2 = jnp.dot(jnp.concatenate([kr_hi, kr_lo], axis=-1), pl_ref[...], preferred_element_type=F32)
    k_ref[...] = (kn + kr2[:, :nslot] * cos + kr2[:, nslot:] * sin).astype(BF16)


def _mla_prep(p, cos, sin, gq, gkv, wq, wk, wv, place):
    b, t, _ = p.shape
    tm = min(512, t)
    nslot = MLA_HEADS * MLA_SLOT
    scale = (MLA_NOPE + MLA_ROPE) ** -0.5
    full = lambda a: pl.BlockSpec(a.shape, lambda bi, i: (0,) * a.ndim)
    return pl.pallas_call(
        functools.partial(_mla_prep_kernel, scale=scale),
        grid=(b, t // tm),
        in_specs=[pl.BlockSpec((None, tm, MLA_Q_RANK), lambda bi, i: (bi, i, P_MLA_Q // MLA_Q_RANK)),
                  pl.BlockSpec((None, tm, 2 * MLA_SLOT), lambda bi, i: (bi, i, P_MLA_KV // (2 * MLA_SLOT))),
                  pl.BlockSpec((tm, MLA_SLOT), lambda bi, i: (i, 0)),
                  pl.BlockSpec((tm, MLA_SLOT), lambda bi, i: (i, 0)),
                  full(gq), full(gkv), full(wq), full(wk), full(wv), full(place)],
        out_specs=[pl.BlockSpec((None, tm, nslot), lambda bi, i: (bi, i, 0)),
                   pl.BlockSpec((None, tm, nslot), lambda bi, i: (bi, i, 0)),
                   pl.BlockSpec((None, tm, MLA_HEADS * MLA_V), lambda bi, i: (bi, i, 0))],
        out_shape=[jax.ShapeDtypeStruct((b, t, nslot), BF16),
                   jax.ShapeDtypeStruct((b, t, nslot), BF16),
                   jax.ShapeDtypeStruct((b, t, MLA_HEADS * MLA_V), BF16)],
        compiler_params=_cparams(("parallel", "parallel")),
        name="mla_prep",
    )(p, p, cos, sin, gq, gkv, wq, wk, wv, place)


def _flash_update(h, q, k, v, m_ref, l_ref, acc_ref):
    s = lax.dot_general(q, k, (((1,), (1,)), ((), ())), preferred_element_type=F32)
    m_prev = m_ref[h]
    m_new = jnp.maximum(m_prev, jnp.max(s, axis=-1, keepdims=True))
    alpha = jnp.exp(m_prev - m_new)
    p = jnp.exp(s - jnp.concatenate([m_new] * (s.shape[1] // 128), axis=-1))
    l_ref[h] = alpha * l_ref[h] + jnp.sum(p, axis=-1, keepdims=True)
    acc_ref[h] = alpha * acc_ref[h] + jnp.dot(p.astype(BF16), v, preferred_element_type=F32)
    m_ref[h] = m_new


def _mla_flash_kernel(*refs, tk, has_lat):
    if has_lat:
        q_ref, kc_ref, vc_ref, kl_ref, vl_ref, o_ref, m_ref, l_ref, acc_ref = refs
    else:
        q_ref, kc_ref, vc_ref, o_ref, m_ref, l_ref, acc_ref = refs
    m_ref[...] = jnp.full(m_ref.shape, -jnp.inf, F32)
    l_ref[...] = jnp.zeros(l_ref.shape, F32)
    acc_ref[...] = jnp.zeros(acc_ref.shape, F32)
    n_ctx = kc_ref.shape[0]
    ctk = min(tk, n_ctx)
    for h in range(MLA_HEADS):
        hs = slice(h * MLA_SLOT, (h + 1) * MLA_SLOT)
        vs = slice((h // 2) * 128, (h // 2) * 128 + 128)
        q = q_ref[:, hs]
        for c in range(n_ctx // ctk):
            _flash_update(h, q, kc_ref[c * ctk:(c + 1) * ctk, hs], vc_ref[c * ctk:(c + 1) * ctk, vs],
                          m_ref, l_ref, acc_ref)
        if has_lat:
            def body(j, carry):
                rows = pl.ds(pl.multiple_of(j * tk, tk), tk)
                _flash_update(h, q, kl_ref[rows, hs], vl_ref[rows, vs], m_ref, l_ref, acc_ref)
                return carry
            lax.fori_loop(0, kl_ref.shape[0] // tk, body, 0)
    lane = lax.broadcasted_iota(jnp.int32, (q_ref.shape[0], 128), 1)
    outs = []
    for pair in range(MLA_HEADS // 2):
        o0 = acc_ref[2 * pair] / l_ref[2 * pair]
        o1 = acc_ref[2 * pair + 1] / l_ref[2 * pair + 1]
        outs.append(jnp.where(lane < MLA_V, o0, o1))
    o_ref[...] = jnp.concatenate(outs, axis=-1).astype(o_ref.dtype)


def _mla_flash(q, kc, vc, kl=None, vl=None):
    b, t, nslot = q.shape
    n_ctx = kc.shape[1]
    tq = min(512, t)
    tk = 512
    has_lat = kl is not None
    in_specs = [pl.BlockSpec((None, tq, nslot), lambda bi, i: (bi, i, 0)),
                pl.BlockSpec((None, n_ctx, nslot), lambda bi, i: (bi, 0, 0)),
                pl.BlockSpec((None, n_ctx, MLA_HEADS * MLA_V), lambda bi, i: (bi, 0, 0))]
    args = [q, kc, vc]
    if has_lat:
        n_lat = kl.shape[1]
        in_specs += [pl.BlockSpec((None, n_lat, nslot), lambda bi, i: (bi, 0, 0)),
                     pl.BlockSpec((None, n_lat, MLA_HEADS * MLA_V), lambda bi, i: (bi, 0, 0))]
        args += [kl, vl]
    return pl.pallas_call(
        functools.partial(_mla_flash_kernel, tk=tk, has_lat=has_lat),
        grid=(b, t // tq),
        in_specs=in_specs,
        out_specs=pl.BlockSpec((None, tq, MLA_HEADS * MLA_V), lambda bi, i: (bi, i, 0)),
        out_shape=jax.ShapeDtypeStruct((b, t, MLA_HEADS * MLA_V), BF16),
        scratch_shapes=[pltpu.VMEM((MLA_HEADS, tq, 128), F32),
                        pltpu.VMEM((MLA_HEADS, tq, 128), F32),
                        pltpu.VMEM((MLA_HEADS, tq, 128), F32)],
        compiler_params=_cparams(("parallel", "arbitrary")),
        name="mla_flash",
    )(*args)


def _merge_kernel(x_ref, g1_ref, u0_ref, u1_ref, u2_ref, u3_ref, t0_ref, t1_ref, t2_ref, t3_ref,
                  bg_ref, wb_ref, wo_ref, o_ref):
    merged = None
    for n, (u_ref, t_ref) in enumerate(((u0_ref, t0_ref), (u1_ref, t1_ref), (u2_ref, t2_ref), (u3_ref, t3_ref))):
        logits = t_ref[...] + bg_ref[n:n + 1, :]
        gate = 1.0 / (1.0 + jnp.exp(-logits))
        term = gate * jnp.dot(u_ref[...].astype(BF16), wb_ref[n], preferred_element_type=F32)
        merged = term if merged is None else merged + term
    out = jnp.dot(merged.astype(BF16), wo_ref[...], preferred_element_type=F32)
    o_ref[...] = x_ref[...] + g1_ref[...] * out


def _merge(x, g1, branches, p, b_gate, w_branch, w_out):
    b, t, d = x.shape
    tm = min(512, t)
    gate_blk = P_GATES // d
    tok = lambda w: pl.BlockSpec((None, tm, w), lambda bi, i: (bi, i, 0))
    in_specs = [tok(d), pl.BlockSpec((None, 1, d), lambda bi, i: (bi, 0, 0))]
    in_specs += [tok(BRANCH_W)] * N_BRANCH
    in_specs += [pl.BlockSpec((None, tm, d), functools.partial(lambda bi, i, n: (bi, i, gate_blk + n), n=n))
                 for n in range(N_BRANCH)]
    in_specs += [pl.BlockSpec(b_gate.shape, lambda bi, i: (0, 0)),
                 pl.BlockSpec(w_branch.shape, lambda bi, i: (0, 0, 0)),
                 pl.BlockSpec(w_out.shape, lambda bi, i: (0, 0))]
    return pl.pallas_call(
        _merge_kernel,
        grid=(b, t // tm),
        in_specs=in_specs,
        out_specs=tok(d),
        out_shape=jax.ShapeDtypeStruct((b, t, d), F32),
        compiler_params=_cparams(("parallel", "parallel")),
        name="merge",
    )(x, g1, *branches, p, p, p, p, b_gate, w_branch, w_out)


def _mlp_kernel(x_ref, sh_ref, sc_ref, g2_ref, gn_ref, gf_ref, w1_ref, w2_ref, o_ref, h_ref, acc_ref, *, final_norm):
    f = pl.program_id(2)

    @pl.when(f == 0)
    def _():
        x = x_ref[...]
        y = x * lax.rsqrt(jnp.mean(x * x, axis=-1, keepdims=True) + RMS_EPS)
        y = y * gn_ref[...]
        h_ref[...] = (y * (1.0 + sc_ref[...]) + sh_ref[...]).astype(BF16)

    a = jnp.maximum(jnp.dot(h_ref[...], w1_ref[...], preferred_element_type=F32), 0.0)
    part = jnp.dot((a * a).astype(BF16), w2_ref[...], preferred_element_type=F32)

    @pl.when(f == 0)
    def _():
        acc_ref[...] = part

    @pl.when(f > 0)
    def _():
        acc_ref[...] += part

    @pl.when(f == pl.num_programs(2) - 1)
    def _():
        y = x_ref[...] + g2_ref[...] * acc_ref[...]
        if final_norm:
            y = y * lax.rsqrt(jnp.mean(y * y, axis=-1, keepdims=True) + RMS_EPS) * gf_ref[...]
        o_ref[...] = y


def _mlp(x, shift, scale, gate, gain, w1, w2, final_gain, final_norm):
    b, t, d = x.shape
    dff = w1.shape[1]
    tm = min(1024, t)
    tf = 1024
    vec = pl.BlockSpec((None, 1, d), lambda bi, i, f: (bi, 0, 0))
    row = pl.BlockSpec((1, d), lambda bi, i, f: (0, 0))
    return pl.pallas_call(
        functools.partial(_mlp_kernel, final_norm=final_norm),
        grid=(b, t // tm, dff // tf),
        in_specs=[pl.BlockSpec((None, tm, d), lambda bi, i, f: (bi, i, 0)), vec, vec, vec, row, row,
                  pl.BlockSpec((d, tf), lambda bi, i, f: (0, f)),
                  pl.BlockSpec((tf, d), lambda bi, i, f: (f, 0))],
        out_specs=pl.BlockSpec((None, tm, d), lambda bi, i, f: (bi, i, 0)),
        out_shape=jax.ShapeDtypeStruct((b, t, d), F32),
        scratch_shapes=[pltpu.VMEM((tm, d), BF16), pltpu.VMEM((tm, d), F32)],
        compiler_params=_cparams(("parallel", "parallel", "arbitrary")),
        name="mlp",
    )(x, shift, scale, gate, gain, final_gain, w1, w2)


def _rms_norm(x, g):
    xf = x.astype(F32)
    y = xf * lax.rsqrt(jnp.mean(xf * xf, axis=-1, keepdims=True) + RMS_EPS)
    return (y * g.astype(F32)).astype(x.dtype)


def _l2n(x):
    return x * lax.rsqrt(jnp.sum(x * x, axis=-1, keepdims=True) + RMS_EPS)


def _heads_first(u, n_heads):
    b, t, w = u.shape
    return u.reshape(b, t, n_heads, w // n_heads).transpose(0, 2, 1, 3)


def _merge_heads(o):
    b, h, t, d = o.shape
    return o.transpose(0, 2, 1, 3).reshape(b, t, h * d)


def _cconv(x, w):
    width, ch = w.shape
    left = width // 2
    return lax.conv_general_dilated(
        x, w.astype(x.dtype)[:, None, :], window_strides=(1,),
        padding=[(left, width - 1 - left)],
        dimension_numbers=('NWC', 'WIO', 'NWC'), feature_group_count=ch)


def _block_attention(q, k, v, scale):
    b, h, t, dq = q.shape
    dv = v.shape[-1]
    nb = t // ATTN_BLOCK
    qb = jnp.moveaxis(q.reshape(b, h, nb, ATTN_BLOCK, dq), 2, 0)

    def attend(qi):
        s = jnp.einsum('bhqd,bhkd->bhqk', qi, k, preferred_element_type=F32) * scale
        p = jax.nn.softmax(s, axis=-1).astype(v.dtype)
        return jnp.einsum('bhqk,bhkd->bhqd', p, v)

    o = lax.map(attend, qb)
    return jnp.moveaxis(o, 0, 2).reshape(b, h, t, dv)


def _chunk_gated_delta(q, k, v, g, beta, s0):
    b, h, t, dk = q.shape
    dv = v.shape[-1]
    cs = GDN_CHUNK
    n = t // cs
    q, k, v = (u.reshape(b, h, n, cs, -1) for u in (q, k, v))
    g, beta = (u.reshape(b, h, n, cs) for u in (g, beta))
    gc = jnp.cumsum(g, axis=-1)
    incl = np.tril(np.ones((cs, cs), dtype=bool))
    strict = np.tril(np.ones((cs, cs), dtype=bool), -1)
    decay = jnp.where(incl, jnp.exp(jnp.where(incl, gc[..., :, None] - gc[..., None, :], 0.0)), 0.0)
    kb = k * beta[..., None]
    lower = jnp.where(strict, jnp.einsum('bhnid,bhnjd->bhnij', kb, k) * decay, 0.0)
    eye = jnp.eye(cs, dtype=F32)
    tinv = lax.linalg.triangular_solve(eye + lower, jnp.broadcast_to(eye, lower.shape), left_side=True, lower=True)
    u = tinv @ (v * beta[..., None])
    w = tinv @ (kb * jnp.exp(gc)[..., None])
    a_intra = jnp.where(incl, jnp.einsum('bhnid,bhnjd->bhnij', q, k) * decay, 0.0)
    q_dec = q * jnp.exp(gc)[..., None]
    k_dec = k * jnp.exp(gc[..., -1:] - gc)[..., None]
    g_tot = jnp.exp(gc[..., -1])
    xs = tuple(jnp.moveaxis(z, 2, 0) for z in (q_dec, k_dec, u, w, a_intra, g_tot))

    def step(s, inp):
        q_i, k_i, u_i, w_i, a_i, gt_i = inp
        v_new = u_i - w_i @ s
        o = q_i @ s + a_i @ v_new
        s = s * gt_i[..., None, None] + jnp.einsum('bhck,bhcv->bhkv', k_i, v_new)
        return s, o

    s_final, o = lax.scan(step, s0, xs)
    return jnp.moveaxis(o, 0, 2).reshape(b, h, t, dv), s_final


def _gdn_prep(qkv, beta_logit, a_logit, lp):
    b, t, _ = qkv.shape
    qkv = jax.nn.silu(_cconv(qkv, lp['gdn_conv_w'])).astype(F32)
    q, k, v = (_heads_first(u, GDN_HEADS) for u in jnp.split(qkv, 3, axis=-1))
    q = _l2n(q) * (GDN_DK ** -0.5)
    k = _l2n(k)

    def per_dir(z):
        return z.astype(F32).reshape(b, t, 2, GDN_HEADS).transpose(2, 0, 3, 1)

    beta = jax.nn.sigmoid(per_dir(beta_logit))
    a_log = lp['gdn_a_log'].astype(F32)[:, None, :, None]
    dt_bias = lp['gdn_dt_bias'].astype(F32)[:, None, :, None]
    g = -jnp.exp(a_log) * jax.nn.softplus(per_dir(a_logit) + dt_bias)
    return q, k, v, beta, g


def _gdn_bidir(q, k, v, beta, g, s0_f, s0_b):
    o_f, s_f = _chunk_gated_delta(q, k, v, g[0], beta[0], s0_f)
    flip = lambda z: jnp.flip(z, axis=2)
    o_b, s_b = _chunk_gated_delta(flip(q), flip(k), flip(v), flip(g[1]), flip(beta[1]), s0_b)
    return o_f + flip(o_b), s_f, s_b


def _gdn_output(o, z, lp):
    b, h, t, dv = o.shape
    o = _rms_norm(jnp.swapaxes(o, 1, 2), lp['gdn_norm_g'])
    o = o * jax.nn.silu(z.astype(F32)).reshape(b, t, h, dv)
    return o.reshape(b, t, h * dv).astype(z.dtype)


def _gdn_mixer(pc, pt, lp, need_ctx):
    def parts(p):
        return (p[..., P_GDN_QKV:P_GDN_QKV + 3 * GDN_W], p[..., P_GDN_Z:P_GDN_Z + GDN_W],
                p[..., P_GDN_BA:P_GDN_BA + 2 * GDN_HEADS], p[..., P_GDN_BA + 2 * GDN_HEADS:P_GDN_BA + 4 * GDN_HEADS])
    qkv_c, z_c, b_c, a_c = parts(pc)
    qkv_l, z_l, b_l, a_l = parts(pt)
    qc, kc, vc, bc, gc = _gdn_prep(qkv_c, b_c, a_c, lp)
    zero = jnp.zeros((qc.shape[0], GDN_HEADS, GDN_DK, GDN_DV), F32)
    oc, s_f, s_b = _gdn_bidir(qc, kc, vc, bc, gc, zero, zero)
    ql, kl, vl, bl, gl = _gdn_prep(qkv_l, b_l, a_l, lp)
    ol, _, _ = _gdn_bidir(ql, kl, vl, bl, gl, s_f, s_b)
    uc = _gdn_output(oc, z_c, lp) if need_ctx else None
    return uc, _gdn_output(ol, z_l, lp)


def _lin_combine(left, right):
    a_l, b_l = left
    a_r, b_r = right
    return a_l * a_r, a_r * b_l + b_r


def _linear_scan(a, b, h0, reverse):
    if reverse:
        a, b = jnp.flip(a, axis=1), jnp.flip(b, axis=1)
    b = b.at[:, 0].add(a[:, 0] * h0)
    _, h = lax.associative_scan(_lin_combine, (a, b), axis=1)
    return jnp.flip(h, axis=1) if reverse else h


def _lru_gate(xblk, w, bias):
    b, t = xblk.shape[:2]
    y = jnp.einsum('btni,dnij->dbtnj', xblk, w.astype(F32)).reshape(2, b, t, LRU_W)
    return jax.nn.sigmoid(y + bias.astype(F32)[:, None, None, :])


def _lru_prep(xb, lp):
    xb = _cconv(xb, lp['lru_conv_w']) + lp['lru_conv_b']
    xf = xb.astype(F32)
    b, t, _ = xf.shape
    xblk = xf.reshape(b, t, LRU_BLOCKS, LRU_BLOCK_W)
    r = _lru_gate(xblk, lp['lru_w_r'], lp['lru_b_r'])
    i = _lru_gate(xblk, lp['lru_w_i'], lp['lru_b_i'])
    log_a = -LRU_C * jax.nn.softplus(-lp['lru_lambda'].astype(F32))[:, None, None, :] * r
    a = jnp.exp(log_a)
    inp = jnp.sqrt(-jnp.expm1(2.0 * log_a)) * i * xf[None]
    return a, inp


def _lru_mixer(pc, pt, lp, need_ctx):
    xs = lambda p: p[..., P_LRU_X:P_LRU_X + LRU_W]
    ys = lambda p: p[..., P_LRU_Y:P_LRU_Y + LRU_W]
    a_c, b_c = _lru_prep(xs(pc), lp)
    zero = jnp.zeros((a_c.shape[1], LRU_W), F32)
    hc_f = _linear_scan(a_c[0], b_c[0], zero, False)
    hc_b = _linear_scan(a_c[1], b_c[1], zero, True)
    a_l, b_l = _lru_prep(xs(pt), lp)
    hl_f = _linear_scan(a_l[0], b_l[0], hc_f[:, -1], False)
    hl_b = _linear_scan(a_l[1], b_l[1], hc_b[:, 0], True)

    def out(h, y):
        return (h * jax.nn.gelu(y.astype(F32))).astype(y.dtype)

    uc = out(hc_f + hc_b, ys(pc)) if need_ctx else None
    return uc, out(hl_f + hl_b, ys(pt))


def _na_layout(rows):
    kr = min(NA_WIN_ROWS, rows)
    kc = NA_WIN_COLS
    rc = min(NA_QBLOCK + kc, GRID_W)
    n_tok = rows * GRID_W
    t = np.arange(n_tok)
    r, c = t // GRID_W, t % GRID_W
    rs = np.clip(r - kr // 2, 0, rows - kr)
    cs = np.clip(c - kc // 2, 0, GRID_W - kc)
    nblk = n_tok // NA_QBLOCK
    qr, qc, qrs, qcs = (z.reshape(nblk, NA_QBLOCK) for z in (r, c, rs, cs))
    c0 = np.minimum(qcs[:, 0], GRID_W - rc)
    key_r = qrs[:, :1] + np.repeat(np.arange(kr), rc)[None]
    key_c = c0[:, None] + np.tile(np.arange(rc), kr)[None]
    idx = (key_r * GRID_W + key_c).astype(np.int32)
    kcb = key_c[:, None, :]
    valid = (kcb >= qcs[:, :, None]) & (kcb < qcs[:, :, None] + kc)
    rel_r = np.clip(key_r[:, None, :] - qr[:, :, None] + NA_WIN_ROWS - 1, 0, 2 * NA_WIN_ROWS - 2)
    rel_c = np.clip(kcb - qc[:, :, None] + NA_WIN_COLS - 1, 0, 2 * NA_WIN_COLS - 2)
    return idx, valid, rel_r, rel_c


def _neighbourhood_attention(q, k, v, ck, cv, rpb, layout):
    idx, valid, rel_r, rel_c = layout
    b, h, t, dh = q.shape
    nblk, rk = idx.shape
    scale = dh ** -0.5
    qb = q.reshape(b, h, nblk, NA_QBLOCK, dh)
    kb = k[:, :, idx]
    vb = v[:, :, idx]
    bias = rpb.astype(F32)[:, rel_r, rel_c]
    s_win = jnp.einsum('bhnqd,bhnkd->bhnqk', qb, kb, preferred_element_type=F32) * scale + bias
    s_win = jnp.where(valid, s_win, -jnp.inf)
    s_ctx = jnp.einsum('bhnqd,bhld->bhnql', qb, ck, preferred_element_type=F32) * scale
    p = jax.nn.softmax(jnp.concatenate([s_win, s_ctx], axis=-1), axis=-1).astype(v.dtype)
    o = jnp.einsum('bhnqk,bhnkd->bhnqd', p[..., :rk], vb) + jnp.einsum('bhnql,bhld->bhnqd', p[..., rk:], cv)
    return o.reshape(b, h, t, dh)


def _na_mixer(pc, pt, lp, layout, need_ctx):
    qkv = lambda p: p[..., P_NA_QKV:P_NA_QKV + 3 * NA_W]
    qc, kc, vc = (_heads_first(u, NA_HEADS) for u in jnp.split(qkv(pc), 3, axis=-1))
    ql, kl, vl = (_heads_first(u, NA_HEADS) for u in jnp.split(qkv(pt), 3, axis=-1))
    ol = _neighbourhood_attention(ql, kl, vl, kc, vc, lp['na_rpb'], layout)
    uc = _merge_heads(_block_attention(qc, kc, vc, NA_DH ** -0.5)) if need_ctx else None
    return uc, _merge_heads(ol)


def kernel(x, c, ctx, c_ctx, mod_w, mod_b, norm1_g, norm2_g, w_in, b_gate, gdn_conv_w, gdn_a_log, gdn_dt_bias,
           gdn_norm_g, lru_conv_w, lru_conv_b, lru_w_r, lru_b_r, lru_w_i, lru_b_i, lru_lambda, mla_q_norm_g,
           mla_w_uq, mla_kv_norm_g, mla_w_ukv, na_rpb, w_branch, w_out, mlp_w1, mlp_w2, final_norm_g):
    bsz, n_tok, d = x.shape
    n_ctx = ctx.shape[1]
    depth = w_in.shape[0]
    rows = n_tok // GRID_W
    layout = _na_layout(rows)
    cos_l, sin_l = _rope_tables(n_tok, True)
    cos_c, sin_c = _rope_tables(n_ctx, False)

    n_rows = -(-(bsz + 1) // 8) * 8
    cc = jnp.zeros((n_rows, d), F32).at[:bsz].set(c).at[bsz].set(c_ctx)
    final_gain = final_norm_g.reshape(1, d)

    xl, xc = x, ctx
    for l in range(depth):
        need_ctx = l < depth - 1
        lp = {'gdn_conv_w': gdn_conv_w[l], 'gdn_a_log': gdn_a_log[l], 'gdn_dt_bias': gdn_dt_bias[l],
              'gdn_norm_g': gdn_norm_g[l], 'lru_conv_w': lru_conv_w[l], 'lru_conv_b': lru_conv_b[l],
              'lru_w_r': lru_w_r[l], 'lru_b_r': lru_b_r[l], 'lru_w_i': lru_w_i[l], 'lru_b_i': lru_b_i[l],
              'lru_lambda': lru_lambda[l], 'na_rpb': na_rpb[l]}
        mod = _modulation(cc, mod_w[l], mod_b[l])
        mod_l = [mod[:bsz, n * d:(n + 1) * d].reshape(bsz, 1, d) for n in range(N_MOD)]
        mod_c = [jnp.broadcast_to(mod[bsz, n * d:(n + 1) * d].reshape(1, 1, d), (bsz, 1, d)) for n in range(N_MOD)]

        w_in_l = _arrange_w_in(w_in[l])
        wq, wk, wv, place = _arrange_mla(mla_w_uq[l], mla_w_ukv[l])
        gq = mla_q_norm_g[l].reshape(1, -1)
        gkv = mla_kv_norm_g[l].reshape(1, -1)
        g1n = norm1_g[l].reshape(1, d)
        g2n = norm2_g[l].reshape(1, d)
        wb = w_branch[l].astype(BF16)
        wo = w_out[l].astype(BF16)
        w1 = mlp_w1[l].astype(BF16)
        w2 = mlp_w2[l].astype(BF16)

        pt = _inproj(xl, mod_l[0], mod_l[1], g1n, w_in_l, P_COLS)
        pc = _inproj(xc, mod_c[0], mod_c[1], g1n, w_in_l, P_COLS if need_ctx else P_MIX_END)

        ua_c, ua_l = _gdn_mixer(pc, pt, lp, need_ctx)
        ub_c, ub_l = _lru_mixer(pc, pt, lp, need_ctx)
        qc_m, kc_m, vc_m = _mla_prep(pc, cos_c, sin_c, gq, gkv, wq, wk, wv, place)
        ql_m, kl_m, vl_m = _mla_prep(pt, cos_l, sin_l, gq, gkv, wq, wk, wv, place)
        uc_l = _mla_flash(ql_m, kc_m, vc_m, kl_m, vl_m)
        ud_c, ud_l = _na_mixer(pc, pt, lp, layout, need_ctx)

        last = l == depth - 1
        xl = _merge(xl, mod_l[2], (ua_l, ub_l, uc_l, ud_l), pt, b_gate[l], wb, wo)
        xl = _mlp(xl, mod_l[3], mod_l[4], mod_l[5], g2n, w1, w2, final_gain, last)
        if need_ctx:
            uc_c = _mla_flash(qc_m, kc_m, vc_m)
            xc = _merge(xc, mod_c[2], (ua_c, ub_c, uc_c, ud_c), pc, b_gate[l], wb, wo)
            xc = _mlp(xc, mod_c[3], mod_c[4], mod_c[5], g2n, w1, w2, final_gain, False)
    return xl
```

```python
import functools
import math

import jax
import jax.numpy as jnp
import numpy as np
from jax import lax
from jax.experimental import pallas as pl
from jax.experimental.pallas import tpu as pltpu

F32 = jnp.float32
BF16 = jnp.bfloat16

GRID_W = 64
N_MOD = 6
RMS_EPS = 1e-6
GDN_HEADS = 4
GDN_DK = 64
GDN_DV = 64
GDN_CHUNK = 64
GDN_W = GDN_HEADS * GDN_DV
LRU_W = 256
LRU_BLOCKS = 4
LRU_BLOCK_W = LRU_W // LRU_BLOCKS
LRU_C = 8.0
MLA_HEADS = 4
MLA_Q_RANK = 256
MLA_KV_RANK = 128
MLA_NOPE = 64
MLA_ROPE = 32
MLA_V = 64
MLA_SLOT = 128
ROPE_BASE = 10000.0
ATTN_BLOCK = 128
NA_HEADS = 4
NA_DH = 64
NA_W = NA_HEADS * NA_DH
NA_WIN_ROWS = 8
NA_WIN_COLS = 16
NA_QBLOCK = 32
N_BRANCH = 4
BRANCH_W = 256

_REF_COLS = {}
_off = 0
for _name, _w in (('gdn_qkv', 3 * GDN_W), ('gdn_z', GDN_W), ('gdn_beta', 2 * GDN_HEADS), ('gdn_a', 2 * GDN_HEADS),
                  ('lru_x', LRU_W), ('lru_y', LRU_W), ('mla_q', MLA_Q_RANK), ('mla_kv', MLA_KV_RANK),
                  ('mla_kr', MLA_ROPE), ('na_qkv', 3 * NA_W)):
    _REF_COLS[_name] = (_off, _w)
    _off += _w
N_MIX_COLS = _off

P_GDN_QKV = 0
P_GDN_Z = 768
P_LRU_X = 1024
P_LRU_Y = 1280
P_MLA_Q = 1536
P_NA_QKV = 1792
P_MLA_KV = 2560
P_MLA_KR = 2688
P_GDN_BA = 2816
P_MIX_END = 3072
P_GATES = 3072
P_COLS = P_GATES + N_BRANCH * 1024

VMEM_LIMIT = 48 * 1024 * 1024


def _cparams(sem):
    return pltpu.CompilerParams(dimension_semantics=sem, vmem_limit_bytes=VMEM_LIMIT)


def _arrange_w_in(w_in):
    d = w_in.shape[0]
    out = jnp.zeros((d, P_COLS), F32)
    for name, dst in (('gdn_qkv', P_GDN_QKV), ('gdn_z', P_GDN_Z), ('lru_x', P_LRU_X), ('lru_y', P_LRU_Y),
                      ('mla_q', P_MLA_Q), ('na_qkv', P_NA_QKV), ('mla_kv', P_MLA_KV), ('mla_kr', P_MLA_KR),
                      ('gdn_beta', P_GDN_BA), ('gdn_a', P_GDN_BA + 2 * GDN_HEADS)):
        o, w = _REF_COLS[name]
        out = out.at[:, dst:dst + w].set(w_in[:, o:o + w])
    out = out.at[:, P_GATES:].set(w_in[:, N_MIX_COLS:])
    return out.astype(BF16)


def _rope_perm():
    q = MLA_ROPE // 4
    src = np.zeros(MLA_ROPE, np.int32)
    sign = np.zeros(MLA_ROPE, np.float32)
    for base in (0, 2 * q):
        for d in range(q):
            src[base + d] = base + d + q
            sign[base + d] = -1.0
            src[base + q + d] = base + d
            sign[base + q + d] = 1.0
    return src, sign


def _arrange_mla(w_uq, w_ukv):
    src, sign = _rope_perm()
    hq = MLA_NOPE + MLA_ROPE
    wq = jnp.zeros((MLA_Q_RANK, 2 * MLA_HEADS * MLA_SLOT), F32)
    wk = jnp.zeros((MLA_KV_RANK, MLA_HEADS * MLA_SLOT), F32)
    wv = jnp.zeros((MLA_KV_RANK, MLA_HEADS * MLA_V), F32)
    place = np.zeros((2 * MLA_SLOT, 2 * MLA_HEADS * MLA_SLOT), np.float32)
    rot_off = MLA_HEADS * MLA_SLOT
    for h in range(MLA_HEADS):
        nope = w_uq[:, h * hq:h * hq + MLA_NOPE]
        pe = w_uq[:, h * hq + MLA_NOPE:(h + 1) * hq]
        s = h * MLA_SLOT
        wq = wq.at[:, s:s + MLA_NOPE].set(nope)
        wq = wq.at[:, s + MLA_NOPE:s + MLA_NOPE + MLA_ROPE].set(pe)
        wq = wq.at[:, rot_off + s + MLA_NOPE:rot_off + s + MLA_NOPE + MLA_ROPE].set(pe[:, src] * sign)
        wk = wk.at[:, s:s + MLA_NOPE].set(w_ukv[:, h * 128:h * 128 + MLA_NOPE])
        wv = wv.at[:, h * MLA_V:(h + 1) * MLA_V].set(w_ukv[:, h * 128 + MLA_NOPE:(h + 1) * 128])
        for d in range(MLA_ROPE):
            for half in (0, MLA_SLOT):
                place[half + d, s + MLA_NOPE + d] = 1.0
                place[half + src[d], rot_off + s + MLA_NOPE + d] = sign[d]
    return wq.astype(BF16), wk.astype(BF16), wv.astype(BF16), jnp.asarray(place, BF16)


def _rope_tables(n_tok, use_rope):
    cos = np.ones((n_tok, MLA_SLOT), np.float32)
    sin = np.zeros((n_tok, MLA_SLOT), np.float32)
    if use_rope:
        t = np.arange(n_tok)
        row = (t // GRID_W).astype(np.float32)
        col = (t % GRID_W).astype(np.float32)
        n_freq = MLA_ROPE // 4
        inv = (ROPE_BASE ** (-np.arange(n_freq, dtype=np.float32) / n_freq)).astype(np.float32)
        ar = row[:, None] * inv
        ac = col[:, None] * inv
        ang = np.concatenate([ar, ar, ac, ac], axis=-1).astype(np.float32)
        cos[:, MLA_NOPE:MLA_NOPE + MLA_ROPE] = np.cos(ang)
        sin[:, MLA_NOPE:MLA_NOPE + MLA_ROPE] = np.sin(ang)
    return jnp.asarray(cos), jnp.asarray(sin)


def _mod_kernel(c_ref, w_ref, b_ref, o_ref):
    c = c_ref[...]
    s = c * (1.0 / (1.0 + jnp.exp(-c)))
    o_ref[...] = jnp.dot(s, w_ref[...], preferred_element_type=F32) + b_ref[...]


def _modulation(cc, mod_w, mod_b):
    r, d = cc.shape
    n = mod_w.shape[1]
    tn = 1024
    return pl.pallas_call(
        _mod_kernel,
        grid=(n // tn,),
        in_specs=[pl.BlockSpec((r, d), lambda j: (0, 0)),
                  pl.BlockSpec((d, tn), lambda j: (0, j)),
                  pl.BlockSpec((1, tn), lambda j: (0, j))],
        out_specs=pl.BlockSpec((r, tn), lambda j: (0, j)),
        out_shape=jax.ShapeDtypeStruct((r, n), F32),
        compiler_params=_cparams(("arbitrary",)),
        name="modulation",
    )(cc, mod_w, mod_b.reshape(1, n))


def _inproj_kernel(x_ref, sh_ref, sc_ref, g_ref, w_ref, o_ref, h_ref):
    @pl.when(pl.program_id(2) == 0)
    def _():
        x = x_ref[...]
        y = x * lax.rsqrt(jnp.mean(x * x, axis=-1, keepdims=True) + RMS_EPS)
        y = y * g_ref[...]
        h_ref[...] = (y * (1.0 + sc_ref[...]) + sh_ref[...]).astype(BF16)

    o_ref[...] = jnp.dot(h_ref[...], w_ref[...], preferred_element_type=F32)


def _inproj(x, shift, scale, gain, w, n_cols):
    b, t, d = x.shape
    tm = min(1024, t)
    tn = 1024
    return pl.pallas_call(
        _inproj_kernel,
        grid=(b, t // tm, n_cols // tn),
        in_specs=[pl.BlockSpec((None, tm, d), lambda bi, i, j: (bi, i, 0)),
                  pl.BlockSpec((None, 1, d), lambda bi, i, j: (bi, 0, 0)),
                  pl.BlockSpec((None, 1, d), lambda bi, i, j: (bi, 0, 0)),
                  pl.BlockSpec((1, d), lambda bi, i, j: (0, 0)),
                  pl.BlockSpec((d, tn), lambda bi, i, j: (0, j))],
        out_specs=pl.BlockSpec((None, tm, tn), lambda bi, i, j: (bi, i, j)),
        out_shape=jax.ShapeDtypeStruct((b, t, n_cols), F32),
        scratch_shapes=[pltpu.VMEM((tm, d), BF16)],
        compiler_params=_cparams(("parallel", "parallel", "arbitrary")),
        name="inproj",
    )(x, shift, scale, gain, w)


def _mla_prep_kernel(ql_ref, kv_ref, cos_ref, sin_ref, gq_ref, gkv_ref, wq_ref, wk_ref, wv_ref, pl_ref,
                     q_ref, k_ref, v_ref, *, scale):
    nslot = MLA_HEADS * MLA_SLOT
    cos = jnp.concatenate([cos_ref[...]] * MLA_HEADS, axis=-1)
    sin = jnp.concatenate([sin_ref[...]] * MLA_HEADS, axis=-1)

    ql = ql_ref[...]
    qn = ql * lax.rsqrt(jnp.mean(ql * ql, axis=-1, keepdims=True) + RMS_EPS) * gq_ref[...]
    q2 = jnp.dot(qn.astype(BF16), wq_ref[...], preferred_element_type=F32)
    q = (q2[:, :nslot] * cos + q2[:, nslot:] * sin) * scale
    q_ref[...] = q.astype(BF16)

    kvkr = kv_ref[...]
    kvl = kvkr[:, :MLA_KV_RANK]
    kr = kvkr[:, MLA_KV_RANK:]
    kvn = (kvl * lax.rsqrt(jnp.mean(kvl * kvl, axis=-1, keepdims=True) + RMS_EPS) * gkv_ref[...]).astype(BF16)
    kn = jnp.dot(kvn, wk_ref[...], preferred_element_type=F32)
    v_ref[...] = jnp.dot(kvn, wv_ref[...], preferred_element_type=F32).astype(BF16)
    kr_hi = kr.astype(BF16)
    kr_lo = (kr - kr_hi.astype(F32)).astype(BF16)
    kr2 = jnp.dot(jnp.concatenate([kr_hi, kr_lo], axis=-1), pl_ref[...], preferred_element_type=F32)
    k_ref[...] = (kn + kr2[:, :nslot] * cos + kr2[:, nslot:] * sin).astype(BF16)


def _mla_prep(p, cos, sin, gq, gkv, wq, wk, wv, place):
    b, t, _ = p.shape
    tm = min(512, t)
    nslot = MLA_HEADS * MLA_SLOT
    scale = (MLA_NOPE + MLA_ROPE) ** -0.5
    full = lambda a: pl.BlockSpec(a.shape, lambda bi, i: (0,) * a.ndim)
    return pl.pallas_call(
        functools.partial(_mla_prep_kernel, scale=scale),
        grid=(b, t // tm),
        in_specs=[pl.BlockSpec((None, tm, MLA_Q_RANK), lambda bi, i: (bi, i, P_MLA_Q // MLA_Q_RANK)),
                  pl.BlockSpec((None, tm, 2 * MLA_SLOT), lambda bi, i: (bi, i, P_MLA_KV // (2 * MLA_SLOT))),
                  pl.BlockSpec((tm, MLA_SLOT), lambda bi, i: (i, 0)),
                  pl.BlockSpec((tm, MLA_SLOT), lambda bi, i: (i, 0)),
                  full(gq), full(gkv), full(wq), full(wk), full(wv), full(place)],
        out_specs=[pl.BlockSpec((None, tm, nslot), lambda bi, i: (bi, i, 0)),
                   pl.BlockSpec((None, tm, nslot), lambda bi, i: (bi, i, 0)),
                   pl.BlockSpec((None, tm, MLA_HEADS * MLA_V), lambda bi, i: (bi, i, 0))],
        out_shape=[jax.ShapeDtypeStruct((b, t, nslot), BF16),
                   jax.ShapeDtypeStruct((b, t, nslot), BF16),
                   jax.ShapeDtypeStruct((b, t, MLA_HEADS * MLA_V), BF16)],
        compiler_params=_cparams(("parallel", "parallel")),
        name="mla_prep",
    )(p, p, cos, sin, gq, gkv, wq, wk, wv, place)


def _flash_update(h, q, k, v, m_ref, l_ref, acc_ref):
    s = lax.dot_general(q, k, (((1,), (1,)), ((), ())), preferred_element_type=F32)
    m_prev = m_ref[h]
    m_new = jnp.maximum(m_prev, jnp.max(s, axis=-1, keepdims=True))
    alpha = jnp.exp(m_prev - m_new)
    p = jnp.exp(s - jnp.concatenate([m_new] * (s.shape[1] // 128), axis=-1))
    l_ref[h] = alpha * l_ref[h] + jnp.sum(p, axis=-1, keepdims=True)
    acc_ref[h] = alpha * acc_ref[h] + jnp.dot(p.astype(BF16), v, preferred_element_type=F32)
    m_ref[h] = m_new


def _mla_flash_kernel(*refs, tk, has_lat):
    if has_lat:
        q_ref, kc_ref, vc_ref, kl_ref, vl_ref, o_ref, m_ref, l_ref, acc_ref = refs
    else:
        q_ref, kc_ref, vc_ref, o_ref, m_ref, l_ref, acc_ref = refs
    m_ref[...] = jnp.full(m_ref.shape, -jnp.inf, F32)
    l_ref[...] = jnp.zeros(l_ref.shape, F32)
    acc_ref[...] = jnp.zeros(acc_ref.shape, F32)
    n_ctx = kc_ref.shape[0]
    ctk = min(tk, n_ctx)
    for h in range(MLA_HEADS):
        hs = slice(h * MLA_SLOT, (h + 1) * MLA_SLOT)
        vs = slice((h // 2) * 128, (h // 2) * 128 + 128)
        q = q_ref[:, hs]
        for c in range(n_ctx // ctk):
            _flash_update(h, q, kc_ref[c * ctk:(c + 1) * ctk, hs], vc_ref[c * ctk:(c + 1) * ctk, vs],
                          m_ref, l_ref, acc_ref)
        if has_lat:
            def body(j, carry):
                rows = pl.ds(pl.multiple_of(j * tk, tk), tk)
                _flash_update(h, q, kl_ref[rows, hs], vl_ref[rows, vs], m_ref, l_ref, acc_ref)
                return carry
            lax.fori_loop(0, kl_ref.shape[0] // tk, body, 0)
    lane = lax.broadcasted_iota(jnp.int32, (q_ref.shape[0], 128), 1)
    outs = []
    for pair in range(MLA_HEADS // 2):
        o0 = acc_ref[2 * pair] / l_ref[2 * pair]
        o1 = acc_ref[2 * pair + 1] / l_ref[2 * pair + 1]
        outs.append(jnp.where(lane < MLA_V, o0, o1))
    o_ref[...] = jnp.concatenate(outs, axis=-1).astype(o_ref.dtype)


def _mla_flash(q, kc, vc, kl=None, vl=None):
    b, t, nslot = q.shape
    n_ctx = kc.shape[1]
    tq = min(512, t)
    tk = 512
    has_lat = kl is not None
    in_specs = [pl.BlockSpec((None, tq, nslot), lambda bi, i: (bi, i, 0)),
                pl.BlockSpec((None, n_ctx, nslot), lambda bi, i: (bi, 0, 0)),
                pl.BlockSpec((None, n_ctx, MLA_HEADS * MLA_V), lambda bi, i: (bi, 0, 0))]
    args = [q, kc, vc]
    if has_lat:
        n_lat = kl.shape[1]
        in_specs += [pl.BlockSpec((None, n_lat, nslot), lambda bi, i: (bi, 0, 0)),
                     pl.BlockSpec((None, n_lat, MLA_HEADS * MLA_V), lambda bi, i: (bi, 0, 0))]
        args += [kl, vl]
    return pl.pallas_call(
        functools.partial(_mla_flash_kernel, tk=tk, has_lat=has_lat),
        grid=(b, t // tq),
        in_specs=in_specs,
        out_specs=pl.BlockSpec((None, tq, MLA_HEADS * MLA_V), lambda bi, i: (bi, i, 0)),
        out_shape=jax.ShapeDtypeStruct((b, t, MLA_HEADS * MLA_V), BF16),
        scratch_shapes=[pltpu.VMEM((MLA_HEADS, tq, 128), F32),
                        pltpu.VMEM((MLA_HEADS, tq, 128), F32),
                        pltpu.VMEM((MLA_HEADS, tq, 128), F32)],
        compiler_params=_cparams(("parallel", "arbitrary")),
        name="mla_flash",
    )(*args)


NA_QROWS = 4
NA_SLAB = NA_QROWS + NA_WIN_ROWS


def _na_geometry(rows):
    assert rows % NA_QROWS == 0 and rows >= NA_SLAB
    nblk = rows // NA_QROWS
    qi = np.arange(NA_QROWS * GRID_W)
    ki = np.arange(NA_SLAB * GRID_W)
    qc, kc = qi % GRID_W, ki % GRID_W
    cs = np.clip(qc - NA_WIN_COLS // 2, 0, GRID_W - NA_WIN_COLS)
    col_valid = (kc[None, :] >= cs[:, None]) & (kc[None, :] < cs[:, None] + NA_WIN_COLS)
    rel_c = np.clip(kc[None, :] - qc[:, None] + NA_WIN_COLS - 1, 0, 2 * NA_WIN_COLS - 2)
    patterns, cls, starts = {}, [], []
    for i in range(nblk):
        r0 = i * NA_QROWS
        start = int(np.clip(r0 - NA_WIN_ROWS // 2, 0, rows - NA_SLAB))
        qr = r0 + qi // GRID_W
        rs = np.clip(qr - NA_WIN_ROWS // 2, 0, rows - NA_WIN_ROWS)
        key = (start - r0,) + tuple((rs[::GRID_W] - r0).tolist())
        if key not in patterns:
            kr = start + ki // GRID_W
            valid = col_valid & (kr[None, :] >= rs[:, None]) & (kr[None, :] < rs[:, None] + NA_WIN_ROWS)
            rel_r = np.clip(kr[None, :] - qr[:, None] + NA_WIN_ROWS - 1, 0, 2 * NA_WIN_ROWS - 2)
            patterns[key] = (len(patterns), valid, rel_r)
        cls.append(patterns[key][0])
        starts.append(start)
    ordered = sorted(patterns.values(), key=lambda z: z[0])
    valid = np.stack([z[1] for z in ordered])
    rel_r = np.stack([z[2] for z in ordered])
    meta = np.stack([np.asarray(cls, np.int32), np.asarray(starts, np.int32)])
    return meta, valid, rel_r, np.broadcast_to(rel_c, rel_r.shape)


def _na_bias_table(rpb, valid, rel_r, rel_c):
    bias = rpb.astype(F32)[:, rel_r, rel_c]
    return jnp.where(valid[None], bias, -jnp.inf).transpose(1, 0, 2, 3)


def _na_attend(q_pair, lane, h, k_parts, v_parts, bias):
    in_head = (lane < NA_DH) if h % 2 == 0 else (lane >= NA_DH)
    qm = jnp.where(in_head, q_pair * (NA_DH ** -0.5), 0.0).astype(BF16)
    scores = [lax.dot_general(qm, k, (((1,), (1,)), ((), ())), preferred_element_type=F32) for k in k_parts]
    if bias is not None:
        scores[0] = scores[0] + bias
    m = scores[0].max(axis=-1, keepdims=True)
    for s in scores[1:]:
        m = jnp.maximum(m, s.max(axis=-1, keepdims=True))
    den = None
    out = None
    for s, v in zip(scores, v_parts):
        p = jnp.exp(s - m)
        ps = p.sum(axis=-1, keepdims=True)
        po = jnp.dot(p.astype(BF16), v, preferred_element_type=F32)
        den = ps if den is None else den + ps
        out = po if out is None else out + po
    return out / den


def _na_win_kernel(meta_ref, q_ref, k_ref, v_ref, kc_ref, vc_ref, bias_ref, o_ref):
    i = pl.program_id(1)
    start = pl.multiple_of(meta_ref[1, i] * GRID_W, GRID_W)
    nq = q_ref.shape[0]
    lane = lax.broadcasted_iota(jnp.int32, (nq, 128), 1)
    outs = []
    for pair in range(NA_HEADS // 2):
        ls = slice(pair * 128, (pair + 1) * 128)
        q_pair = q_ref[:, ls]
        k_win = k_ref[pl.ds(start, NA_SLAB * GRID_W), ls].astype(BF16)
        v_win = v_ref[pl.ds(start, NA_SLAB * GRID_W), ls].astype(BF16)
        k_ctx = kc_ref[:, ls].astype(BF16)
        v_ctx = vc_ref[:, ls].astype(BF16)
        o = [_na_attend(q_pair, lane, 2 * pair + e, (k_win, k_ctx), (v_win, v_ctx), bias_ref[2 * pair + e])
             for e in range(2)]
        outs.append(jnp.where(lane < NA_DH, o[0], o[1]))
    o_ref[...] = jnp.concatenate(outs, axis=-1).astype(o_ref.dtype)


def _na_window(pt, pc, meta, table):
    b, t, _ = pt.shape
    n_ctx = pc.shape[1]
    nq = NA_QROWS * GRID_W
    nk = NA_SLAB * GRID_W
    qb = P_NA_QKV // NA_W
    grid_spec = pltpu.PrefetchScalarGridSpec(
        num_scalar_prefetch=1,
        grid=(b, t // nq),
        in_specs=[pl.BlockSpec((None, nq, NA_W), lambda bi, i, m: (bi, i, qb)),
                  pl.BlockSpec((None, t, NA_W), lambda bi, i, m: (bi, 0, qb + 1)),
                  pl.BlockSpec((None, t, NA_W), lambda bi, i, m: (bi, 0, qb + 2)),
                  pl.BlockSpec((None, n_ctx, NA_W), lambda bi, i, m: (bi, 0, qb + 1)),
                  pl.BlockSpec((None, n_ctx, NA_W), lambda bi, i, m: (bi, 0, qb + 2)),
                  pl.BlockSpec((None, NA_HEADS, nq, nk), lambda bi, i, m: (m[0, i], 0, 0, 0))],
        out_specs=pl.BlockSpec((None, nq, NA_W), lambda bi, i, m: (bi, i, 0)))
    return pl.pallas_call(
        _na_win_kernel,
        grid_spec=grid_spec,
        out_shape=jax.ShapeDtypeStruct((b, t, NA_W), BF16),
        compiler_params=_cparams(("parallel", "arbitrary")),
        name="na_window",
    )(meta, pt, pt, pt, pc, pc, table)


def _na_ctx_kernel(q_ref, k_ref, v_ref, o_ref):
    nq = q_ref.shape[0]
    lane = lax.broadcasted_iota(jnp.int32, (nq, 128), 1)
    outs = []
    for pair in range(NA_HEADS // 2):
        ls = slice(pair * 128, (pair + 1) * 128)
        q_pair = q_ref[:, ls]
        k = k_ref[:, ls].astype(BF16)
        v = v_ref[:, ls].astype(BF16)
        o = [_na_attend(q_pair, lane, 2 * pair + e, (k,), (v,), None) for e in range(2)]
        outs.append(jnp.where(lane < NA_DH, o[0], o[1]))
    o_ref[...] = jnp.concatenate(outs, axis=-1).astype(o_ref.dtype)


def _na_context(pc):
    b, n_ctx, _ = pc.shape
    tq = min(256, n_ctx)
    qb = P_NA_QKV // NA_W
    return pl.pallas_call(
        _na_ctx_kernel,
        grid=(b, n_ctx // tq),
        in_specs=[pl.BlockSpec((None, tq, NA_W), lambda bi, i: (bi, i, qb)),
                  pl.BlockSpec((None, n_ctx, NA_W), lambda bi, i: (bi, 0, qb + 1)),
                  pl.BlockSpec((None, n_ctx, NA_W), lambda bi, i: (bi, 0, qb + 2))],
        out_specs=pl.BlockSpec((None, tq, NA_W), lambda bi, i: (bi, i, 0)),
        out_shape=jax.ShapeDtypeStruct((b, n_ctx, NA_W), BF16),
        compiler_params=_cparams(("parallel", "parallel")),
        name="na_context",
    )(pc, pc, pc)


def _merge_kernel(x_ref, g1_ref, u0_ref, u1_ref, u2_ref, u3_ref, t0_ref, t1_ref, t2_ref, t3_ref,
                  bg_ref, wb_ref, wo_ref, o_ref):
    merged = None
    for n, (u_ref, t_ref) in enumerate(((u0_ref, t0_ref), (u1_ref, t1_ref), (u2_ref, t2_ref), (u3_ref, t3_ref))):
        logits = t_ref[...] + bg_ref[n:n + 1, :]
        gate = 1.0 / (1.0 + jnp.exp(-logits))
        term = gate * jnp.dot(u_ref[...].astype(BF16), wb_ref[n], preferred_element_type=F32)
        merged = term if merged is None else merged + term
    out = jnp.dot(merged.astype(BF16), wo_ref[...], preferred_element_type=F32)
    o_ref[...] = x_ref[...] + g1_ref[...] * out


def _merge(x, g1, branches, p, b_gate, w_branch, w_out):
    b, t, d = x.shape
    tm = min(512, t)
    gate_blk = P_GATES // d
    tok = lambda w: pl.BlockSpec((None, tm, w), lambda bi, i: (bi, i, 0))
    in_specs = [tok(d), pl.BlockSpec((None, 1, d), lambda bi, i: (bi, 0, 0))]
    in_specs += [tok(BRANCH_W)] * N_BRANCH
    in_specs += [pl.BlockSpec((None, tm, d), functools.partial(lambda bi, i, n: (bi, i, gate_blk + n), n=n))
                 for n in range(N_BRANCH)]
    in_specs += [pl.BlockSpec(b_gate.shape, lambda bi, i: (0, 0)),
                 pl.BlockSpec(w_branch.shape, lambda bi, i: (0, 0, 0)),
                 pl.BlockSpec(w_out.shape, lambda bi, i: (0, 0))]
    return pl.pallas_call(
        _merge_kernel,
        grid=(b, t // tm),
        in_specs=in_specs,
        out_specs=tok(d),
        out_shape=jax.ShapeDtypeStruct((b, t, d), F32),
        compiler_params=_cparams(("parallel", "parallel")),
        name="merge",
    )(x, g1, *branches, p, p, p, p, b_gate, w_branch, w_out)


def _mlp_kernel(x_ref, sh_ref, sc_ref, g2_ref, gn_ref, gf_ref, w1_ref, w2_ref, o_ref, h_ref, acc_ref, *, final_norm):
    f = pl.program_id(2)

    @pl.when(f == 0)
    def _():
        x = x_ref[...]
        y = x * lax.rsqrt(jnp.mean(x * x, axis=-1, keepdims=True) + RMS_EPS)
        y = y * gn_ref[...]
        h_ref[...] = (y * (1.0 + sc_ref[...]) + sh_ref[...]).astype(BF16)

    a = jnp.maximum(jnp.dot(h_ref[...], w1_ref[...], preferred_element_type=F32), 0.0)
    part = jnp.dot((a * a).astype(BF16), w2_ref[...], preferred_element_type=F32)

    @pl.when(f == 0)
    def _():
        acc_ref[...] = part

    @pl.when(f > 0)
    def _():
        acc_ref[...] += part

    @pl.when(f == pl.num_programs(2) - 1)
    def _():
        y = x_ref[...] + g2_ref[...] * acc_ref[...]
        if final_norm:
            y = y * lax.rsqrt(jnp.mean(y * y, axis=-1, keepdims=True) + RMS_EPS) * gf_ref[...]
        o_ref[...] = y


def _mlp(x, shift, scale, gate, gain, w1, w2, final_gain, final_norm):
    b, t, d = x.shape
    dff = w1.shape[1]
    tm = min(1024, t)
    tf = 1024
    vec = pl.BlockSpec((None, 1, d), lambda bi, i, f: (bi, 0, 0))
    row = pl.BlockSpec((1, d), lambda bi, i, f: (0, 0))
    return pl.pallas_call(
        functools.partial(_mlp_kernel, final_norm=final_norm),
        grid=(b, t // tm, dff // tf),
        in_specs=[pl.BlockSpec((None, tm, d), lambda bi, i, f: (bi, i, 0)), vec, vec, vec, row, row,
                  pl.BlockSpec((d, tf), lambda bi, i, f: (0, f)),
                  pl.BlockSpec((tf, d), lambda bi, i, f: (f, 0))],
        out_specs=pl.BlockSpec((None, tm, d), lambda bi, i, f: (bi, i, 0)),
        out_shape=jax.ShapeDtypeStruct((b, t, d), F32),
        scratch_shapes=[pltpu.VMEM((tm, d), BF16), pltpu.VMEM((tm, d), F32)],
        compiler_params=_cparams(("parallel", "parallel", "arbitrary")),
        name="mlp",
    )(x, shift, scale, gate, gain, final_gain, w1, w2)


def _rms_norm(x, g):
    xf = x.astype(F32)
    y = xf * lax.rsqrt(jnp.mean(xf * xf, axis=-1, keepdims=True) + RMS_EPS)
    return (y * g.astype(F32)).astype(x.dtype)


def _l2n(x):
    return x * lax.rsqrt(jnp.sum(x * x, axis=-1, keepdims=True) + RMS_EPS)


def _heads_first(u, n_heads):
    b, t, w = u.shape
    return u.reshape(b, t, n_heads, w // n_heads).transpose(0, 2, 1, 3)


def _merge_heads(o):
    b, h, t, d = o.shape
    return o.transpose(0, 2, 1, 3).reshape(b, t, h * d)


def _cconv(x, w):
    width, ch = w.shape
    left = width // 2
    return lax.conv_general_dilated(
        x, w.astype(x.dtype)[:, None, :], window_strides=(1,),
        padding=[(left, width - 1 - left)],
        dimension_numbers=('NWC', 'WIO', 'NWC'), feature_group_count=ch)


def _block_attention(q, k, v, scale):
    b, h, t, dq = q.shape
    dv = v.shape[-1]
    nb = t // ATTN_BLOCK
    qb = jnp.moveaxis(q.reshape(b, h, nb, ATTN_BLOCK, dq), 2, 0)

    def attend(qi):
        s = jnp.einsum('bhqd,bhkd->bhqk', qi, k, preferred_element_type=F32) * scale
        p = jax.nn.softmax(s, axis=-1).astype(v.dtype)
        return jnp.einsum('bhqk,bhkd->bhqd', p, v)

    o = lax.map(attend, qb)
    return jnp.moveaxis(o, 0, 2).reshape(b, h, t, dv)


def _chunk_gated_delta(q, k, v, g, beta, s0):
    b, h, t, dk = q.shape
    dv = v.shape[-1]
    cs = GDN_CHUNK
    n = t // cs
    q, k, v = (u.reshape(b, h, n, cs, -1) for u in (q, k, v))
    g, beta = (u.reshape(b, h, n, cs) for u in (g, beta))
    gc = jnp.cumsum(g, axis=-1)
    incl = np.tril(np.ones((cs, cs), dtype=bool))
    strict = np.tril(np.ones((cs, cs), dtype=bool), -1)
    decay = jnp.where(incl, jnp.exp(jnp.where(incl, gc[..., :, None] - gc[..., None, :], 0.0)), 0.0)
    kb = k * beta[..., None]
    lower = jnp.where(strict, jnp.einsum('bhnid,bhnjd->bhnij', kb, k) * decay, 0.0)
    eye = jnp.eye(cs, dtype=F32)
    tinv = lax.linalg.triangular_solve(eye + lower, jnp.broadcast_to(eye, lower.shape), left_side=True, lower=True)
    u = tinv @ (v * beta[..., None])
    w = tinv @ (kb * jnp.exp(gc)[..., None])
    a_intra = jnp.where(incl, jnp.einsum('bhnid,bhnjd->bhnij', q, k) * decay, 0.0)
    q_dec = q * jnp.exp(gc)[..., None]
    k_dec = k * jnp.exp(gc[..., -1:] - gc)[..., None]
    g_tot = jnp.exp(gc[..., -1])
    xs = tuple(jnp.moveaxis(z, 2, 0) for z in (q_dec, k_dec, u, w, a_intra, g_tot))

    def step(s, inp):
        q_i, k_i, u_i, w_i, a_i, gt_i = inp
        v_new = u_i - w_i @ s
        o = q_i @ s + a_i @ v_new
        s = s * gt_i[..., None, None] + jnp.einsum('bhck,bhcv->bhkv', k_i, v_new)
        return s, o

    s_final, o = lax.scan(step, s0, xs)
    return jnp.moveaxis(o, 0, 2).reshape(b, h, t, dv), s_final


def _gdn_prep(qkv, beta_logit, a_logit, lp):
    b, t, _ = qkv.shape
    qkv = jax.nn.silu(_cconv(qkv, lp['gdn_conv_w'])).astype(F32)
    q, k, v = (_heads_first(u, GDN_HEADS) for u in jnp.split(qkv, 3, axis=-1))
    q = _l2n(q) * (GDN_DK ** -0.5)
    k = _l2n(k)

    def per_dir(z):
        return z.astype(F32).reshape(b, t, 2, GDN_HEADS).transpose(2, 0, 3, 1)

    beta = jax.nn.sigmoid(per_dir(beta_logit))
    a_log = lp['gdn_a_log'].astype(F32)[:, None, :, None]
    dt_bias = lp['gdn_dt_bias'].astype(F32)[:, None, :, None]
    g = -jnp.exp(a_log) * jax.nn.softplus(per_dir(a_logit) + dt_bias)
    return q, k, v, beta, g


def _gdn_bidir(q, k, v, beta, g, s0_f, s0_b):
    o_f, s_f = _chunk_gated_delta(q, k, v, g[0], beta[0], s0_f)
    flip = lambda z: jnp.flip(z, axis=2)
    o_b, s_b = _chunk_gated_delta(flip(q), flip(k), flip(v), flip(g[1]), flip(beta[1]), s0_b)
    return o_f + flip(o_b), s_f, s_b


def _gdn_output(o, z, lp):
    b, h, t, dv = o.shape
    o = _rms_norm(jnp.swapaxes(o, 1, 2), lp['gdn_norm_g'])
    o = o * jax.nn.silu(z.astype(F32)).reshape(b, t, h, dv)
    return o.reshape(b, t, h * dv).astype(z.dtype)


def _gdn_mixer(pc, pt, lp, need_ctx):
    def parts(p):
        return (p[..., P_GDN_QKV:P_GDN_QKV + 3 * GDN_W], p[..., P_GDN_Z:P_GDN_Z + GDN_W],
                p[..., P_GDN_BA:P_GDN_BA + 2 * GDN_HEADS], p[..., P_GDN_BA + 2 * GDN_HEADS:P_GDN_BA + 4 * GDN_HEADS])
    qkv_c, z_c, b_c, a_c = parts(pc)
    qkv_l, z_l, b_l, a_l = parts(pt)
    qc, kc, vc, bc, gc = _gdn_prep(qkv_c, b_c, a_c, lp)
    zero = jnp.zeros((qc.shape[0], GDN_HEADS, GDN_DK, GDN_DV), F32)
    oc, s_f, s_b = _gdn_bidir(qc, kc, vc, bc, gc, zero, zero)
    ql, kl, vl, bl, gl = _gdn_prep(qkv_l, b_l, a_l, lp)
    ol, _, _ = _gdn_bidir(ql, kl, vl, bl, gl, s_f, s_b)
    uc = _gdn_output(oc, z_c, lp) if need_ctx else None
    return uc, _gdn_output(ol, z_l, lp)


def _lin_combine(left, right):
    a_l, b_l = left
    a_r, b_r = right
    return a_l * a_r, a_r * b_l + b_r


def _linear_scan(a, b, h0, reverse):
    if reverse:
        a, b = jnp.flip(a, axis=1), jnp.flip(b, axis=1)
    b = b.at[:, 0].add(a[:, 0] * h0)
    _, h = lax.associative_scan(_lin_combine, (a, b), axis=1)
    return jnp.flip(h, axis=1) if reverse else h


def _lru_gate(xblk, w, bias):
    b, t = xblk.shape[:2]
    y = jnp.einsum('btni,dnij->dbtnj', xblk, w.astype(F32)).reshape(2, b, t, LRU_W)
    return jax.nn.sigmoid(y + bias.astype(F32)[:, None, None, :])


def _lru_prep(xb, lp):
    xb = _cconv(xb, lp['lru_conv_w']) + lp['lru_conv_b']
    xf = xb.astype(F32)
    b, t, _ = xf.shape
    xblk = xf.reshape(b, t, LRU_BLOCKS, LRU_BLOCK_W)
    r = _lru_gate(xblk, lp['lru_w_r'], lp['lru_b_r'])
    i = _lru_gate(xblk, lp['lru_w_i'], lp['lru_b_i'])
    log_a = -LRU_C * jax.nn.softplus(-lp['lru_lambda'].astype(F32))[:, None, None, :] * r
    a = jnp.exp(log_a)
    inp = jnp.sqrt(-jnp.expm1(2.0 * log_a)) * i * xf[None]
    return a, inp


def _lru_mixer(pc, pt, lp, need_ctx):
    xs = lambda p: p[..., P_LRU_X:P_LRU_X + LRU_W]
    ys = lambda p: p[..., P_LRU_Y:P_LRU_Y + LRU_W]
    a_c, b_c = _lru_prep(xs(pc), lp)
    zero = jnp.zeros((a_c.shape[1], LRU_W), F32)
    hc_f = _linear_scan(a_c[0], b_c[0], zero, False)
    hc_b = _linear_scan(a_c[1], b_c[1], zero, True)
    a_l, b_l = _lru_prep(xs(pt), lp)
    hl_f = _linear_scan(a_l[0], b_l[0], hc_f[:, -1], False)
    hl_b = _linear_scan(a_l[1], b_l[1], hc_b[:, 0], True)

    def out(h, y):
        return (h * jax.nn.gelu(y.astype(F32))).astype(y.dtype)

    uc = out(hc_f + hc_b, ys(pc)) if need_ctx else None
    return uc, out(hl_f + hl_b, ys(pt))


def _na_layout(rows):
    kr = min(NA_WIN_ROWS, rows)
    kc = NA_WIN_COLS
    rc = min(NA_QBLOCK + kc, GRID_W)
    n_tok = rows * GRID_W
    t = np.arange(n_tok)
    r, c = t // GRID_W, t % GRID_W
    rs = np.clip(r - kr // 2, 0, rows - kr)
    cs = np.clip(c - kc // 2, 0, GRID_W - kc)
    nblk = n_tok // NA_QBLOCK
    qr, qc, qrs, qcs = (z.reshape(nblk, NA_QBLOCK) for z in (r, c, rs, cs))
    c0 = np.minimum(qcs[:, 0], GRID_W - rc)
    key_r = qrs[:, :1] + np.repeat(np.arange(kr), rc)[None]
    key_c = c0[:, None] + np.tile(np.arange(rc), kr)[None]
    idx = (key_r * GRID_W + key_c).astype(np.int32)
    kcb = key_c[:, None, :]
    valid = (kcb >= qcs[:, :, None]) & (kcb < qcs[:, :, None] + kc)
    rel_r = np.clip(key_r[:, None, :] - qr[:, :, None] + NA_WIN_ROWS - 1, 0, 2 * NA_WIN_ROWS - 2)
    rel_c = np.clip(kcb - qc[:, :, None] + NA_WIN_COLS - 1, 0, 2 * NA_WIN_COLS - 2)
    return idx, valid, rel_r, rel_c


def _neighbourhood_attention(q, k, v, ck, cv, rpb, layout):
    idx, valid, rel_r, rel_c = layout
    b, h, t, dh = q.shape
    nblk, rk = idx.shape
    scale = dh ** -0.5
    qb = q.reshape(b, h, nblk, NA_QBLOCK, dh)
    kb = k[:, :, idx]
    vb = v[:, :, idx]
    bias = rpb.astype(F32)[:, rel_r, rel_c]
    s_win = jnp.einsum('bhnqd,bhnkd->bhnqk', qb, kb, preferred_element_type=F32) * scale + bias
    s_win = jnp.where(valid, s_win, -jnp.inf)
    s_ctx = jnp.einsum('bhnqd,bhld->bhnql', qb, ck, preferred_element_type=F32) * scale
    p = jax.nn.softmax(jnp.concatenate([s_win, s_ctx], axis=-1), axis=-1).astype(v.dtype)
    o = jnp.einsum('bhnqk,bhnkd->bhnqd', p[..., :rk], vb) + jnp.einsum('bhnql,bhld->bhnqd', p[..., rk:], cv)
    return o.reshape(b, h, t, dh)


def _na_mixer(pc, pt, lp, layout, need_ctx):
    qkv = lambda p: p[..., P_NA_QKV:P_NA_QKV + 3 * NA_W]
    qc, kc, vc = (_heads_first(u, NA_HEADS) for u in jnp.split(qkv(pc), 3, axis=-1))
    ql, kl, vl = (_heads_first(u, NA_HEADS) for u in jnp.split(qkv(pt), 3, axis=-1))
    ol = _neighbourhood_attention(ql, kl, vl, kc, vc, lp['na_rpb'], layout)
    uc = _merge_heads(_block_attention(qc, kc, vc, NA_DH ** -0.5)) if need_ctx else None
    return uc, _merge_heads(ol)


def kernel(x, c, ctx, c_ctx, mod_w, mod_b, norm1_g, norm2_g, w_in, b_gate, gdn_conv_w, gdn_a_log, gdn_dt_bias,
           gdn_norm_g, lru_conv_w, lru_conv_b, lru_w_r, lru_b_r, lru_w_i, lru_b_i, lru_lambda, mla_q_norm_g,
           mla_w_uq, mla_kv_norm_g, mla_w_ukv, na_rpb, w_branch, w_out, mlp_w1, mlp_w2, final_norm_g):
    bsz, n_tok, d = x.shape
    n_ctx = ctx.shape[1]
    depth = w_in.shape[0]
    rows = n_tok // GRID_W
    na_meta, na_valid, na_rel_r, na_rel_c = _na_geometry(rows)
    na_meta = jnp.asarray(na_meta)
    cos_l, sin_l = _rope_tables(n_tok, True)
    cos_c, sin_c = _rope_tables(n_ctx, False)

    n_rows = -(-(bsz + 1) // 8) * 8
    cc = jnp.zeros((n_rows, d), F32).at[:bsz].set(c).at[bsz].set(c_ctx)
    final_gain = final_norm_g.reshape(1, d)

    xl, xc = x, ctx
    for l in range(depth):
        need_ctx = l < depth - 1
        lp = {'gdn_conv_w': gdn_conv_w[l], 'gdn_a_log': gdn_a_log[l], 'gdn_dt_bias': gdn_dt_bias[l],
              'gdn_norm_g': gdn_norm_g[l], 'lru_conv_w': lru_conv_w[l], 'lru_conv_b': lru_conv_b[l],
              'lru_w_r': lru_w_r[l], 'lru_b_r': lru_b_r[l], 'lru_w_i': lru_w_i[l], 'lru_b_i': lru_b_i[l],
              'lru_lambda': lru_lambda[l], 'na_rpb': na_rpb[l]}
        mod = _modulation(cc, mod_w[l], mod_b[l])
        mod_l = [mod[:bsz, n * d:(n + 1) * d].reshape(bsz, 1, d) for n in range(N_MOD)]
        mod_c = [jnp.broadcast_to(mod[bsz, n * d:(n + 1) * d].reshape(1, 1, d), (bsz, 1, d)) for n in range(N_MOD)]

        w_in_l = _arrange_w_in(w_in[l])
        wq, wk, wv, place = _arrange_mla(mla_w_uq[l], mla_w_ukv[l])
        gq = mla_q_norm_g[l].reshape(1, -1)
        gkv = mla_kv_norm_g[l].reshape(1, -1)
        g1n = norm1_g[l].reshape(1, d)
        g2n = norm2_g[l].reshape(1, d)
        wb = w_branch[l].astype(BF16)
        wo = w_out[l].astype(BF16)
        w1 = mlp_w1[l].astype(BF16)
        w2 = mlp_w2[l].astype(BF16)

        pt = _inproj(xl, mod_l[0], mod_l[1], g1n, w_in_l, P_COLS)
        pc = _inproj(xc, mod_c[0], mod_c[1], g1n, w_in_l, P_COLS if need_ctx else P_MIX_END)

        ua_c, ua_l = _gdn_mixer(pc, pt, lp, need_ctx)
        ub_c, ub_l = _lru_mixer(pc, pt, lp, need_ctx)
        qc_m, kc_m, vc_m = _mla_prep(pc, cos_c, sin_c, gq, gkv, wq, wk, wv, place)
        ql_m, kl_m, vl_m = _mla_prep(pt, cos_l, sin_l, gq, gkv, wq, wk, wv, place)
        uc_l = _mla_flash(ql_m, kc_m, vc_m, kl_m, vl_m)
        ud_l = _na_window(pt, pc, na_meta, _na_bias_table(na_rpb[l], na_valid, na_rel_r, na_rel_c))
        ud_c = _na_context(pc) if need_ctx else None

        last = l == depth - 1
        xl = _merge(xl, mod_l[2], (ua_l, ub_l, uc_l, ud_l), pt, b_gate[l], wb, wo)
        xl = _mlp(xl, mod_l[3], mod_l[4], mod_l[5], g2n, w1, w2, final_gain, last)
        if need_ctx:
            uc_c = _mla_flash(qc_m, kc_m, vc_m)
            xc = _merge(xc, mod_c[2], (ua_c, ub_c, uc_c, ud_c), pc, b_gate[l], wb, wo)
            xc = _mlp(xc, mod_c[3], mod_c[4], mod_c[5], g2n, w1, w2, final_gain, False)
    return xl
```

```python
import functools

import jax
import jax.numpy as jnp
import numpy as np
from jax import lax
from jax.experimental import pallas as pl
from jax.experimental.pallas import tpu as pltpu

F32 = jnp.float32
BF16 = jnp.bfloat16

GRID_W = 64
N_MOD = 6
RMS_EPS = 1e-6
GDN_HEADS = 4
GDN_DK = 64
GDN_DV = 64
GDN_CHUNK = 64
GDN_W = GDN_HEADS * GDN_DV
GDN_CONV = 4
LRU_W = 256
LRU_BLOCKS = 4
LRU_BLOCK_W = LRU_W // LRU_BLOCKS
LRU_CONV = 4
LRU_C = 8.0
MLA_HEADS = 4
MLA_Q_RANK = 256
MLA_KV_RANK = 128
MLA_NOPE = 64
MLA_ROPE = 32
MLA_V = 64
MLA_SLOT = 128
ROPE_BASE = 10000.0
NA_HEADS = 4
NA_DH = 64
NA_W = NA_HEADS * NA_DH
NA_WIN_ROWS = 8
NA_WIN_COLS = 16
N_BRANCH = 4
BRANCH_W = 256

SEQ_TILE = 256
HALO = 8

_REF_COLS = {}
_off = 0
for _name, _w in (('gdn_qkv', 3 * GDN_W), ('gdn_z', GDN_W), ('gdn_beta', 2 * GDN_HEADS), ('gdn_a', 2 * GDN_HEADS),
                  ('lru_x', LRU_W), ('lru_y', LRU_W), ('mla_q', MLA_Q_RANK), ('mla_kv', MLA_KV_RANK),
                  ('mla_kr', MLA_ROPE), ('na_qkv', 3 * NA_W)):
    _REF_COLS[_name] = (_off, _w)
    _off += _w
N_MIX_COLS = _off

P_GDN_QKV = 0
P_GDN_Z = 768
P_LRU_X = 1024
P_LRU_Y = 1280
P_MLA_Q = 1536
P_NA_QKV = 1792
P_MLA_KV = 2560
P_MLA_KR = 2688
P_GDN_BA = 2816
P_MIX_END = 3072
P_GATES = 3072
P_COLS = P_GATES + N_BRANCH * 1024

VMEM_LIMIT = 52 * 1024 * 1024


def _cparams(sem):
    return pltpu.CompilerParams(dimension_semantics=sem, vmem_limit_bytes=VMEM_LIMIT)


def _pick_tile(n, cap):
    best = 8
    for t in range(8, min(n, cap) + 1, 8):
        if n % t == 0:
            best = t
    return best


def _full(a):
    return pl.BlockSpec(a.shape, lambda *_: (0,) * a.ndim)


def _split3(x):
    hi = x.astype(BF16)
    r = x - hi.astype(F32)
    mid = r.astype(BF16)
    lo = (r - mid.astype(F32)).astype(BF16)
    return hi, mid, lo


def _sigmoid(x):
    return 1.0 / (1.0 + jnp.exp(-x))


def _arrange_w_in(w_in):
    d = w_in.shape[0]
    out = jnp.zeros((d, P_COLS), F32)
    for name, dst in (('gdn_qkv', P_GDN_QKV), ('gdn_z', P_GDN_Z), ('lru_x', P_LRU_X), ('lru_y', P_LRU_Y),
                      ('mla_q', P_MLA_Q), ('na_qkv', P_NA_QKV), ('mla_kv', P_MLA_KV), ('mla_kr', P_MLA_KR),
                      ('gdn_beta', P_GDN_BA), ('gdn_a', P_GDN_BA + 2 * GDN_HEADS)):
        o, w = _REF_COLS[name]
        out = out.at[:, dst:dst + w].set(w_in[:, o:o + w])
    out = out.at[:, P_GATES:].set(w_in[:, N_MIX_COLS:])
    return out.astype(BF16)


def _rope_perm():
    q = MLA_ROPE // 4
    src = np.zeros(MLA_ROPE, np.int32)
    sign = np.zeros(MLA_ROPE, np.float32)
    for base in (0, 2 * q):
        for d in range(q):
            src[base + d] = base + d + q
            sign[base + d] = -1.0
            src[base + q + d] = base + d
            sign[base + q + d] = 1.0
    return src, sign


def _arrange_mla(w_uq, w_ukv):
    src, sign = _rope_perm()
    hq = MLA_NOPE + MLA_ROPE
    wq = jnp.zeros((MLA_Q_RANK, 2 * MLA_HEADS * MLA_SLOT), F32)
    wk = jnp.zeros((MLA_KV_RANK, MLA_HEADS * MLA_SLOT), F32)
    wv = jnp.zeros((MLA_KV_RANK, MLA_HEADS * MLA_V), F32)
    place = np.zeros((2 * MLA_SLOT, 2 * MLA_HEADS * MLA_SLOT), np.float32)
    rot_off = MLA_HEADS * MLA_SLOT
    for h in range(MLA_HEADS):
        nope = w_uq[:, h * hq:h * hq + MLA_NOPE]
        pe = w_uq[:, h * hq + MLA_NOPE:(h + 1) * hq]
        s = h * MLA_SLOT
        wq = wq.at[:, s:s + MLA_NOPE].set(nope)
        wq = wq.at[:, s + MLA_NOPE:s + MLA_NOPE + MLA_ROPE].set(pe)
        wq = wq.at[:, rot_off + s + MLA_NOPE:rot_off + s + MLA_NOPE + MLA_ROPE].set(pe[:, src] * sign)
        wk = wk.at[:, s:s + MLA_NOPE].set(w_ukv[:, h * 128:h * 128 + MLA_NOPE])
        wv = wv.at[:, h * MLA_V:(h + 1) * MLA_V].set(w_ukv[:, h * 128 + MLA_NOPE:(h + 1) * 128])
        for d in range(MLA_ROPE):
            for half in (0, MLA_SLOT):
                place[half + d, s + MLA_NOPE + d] = 1.0
                place[half + src[d], rot_off + s + MLA_NOPE + d] = sign[d]
    return wq.astype(BF16), wk.astype(BF16), wv.astype(BF16), jnp.asarray(place, BF16)


def _rope_tables(n_ctx, n_tok):
    cos = np.ones((n_ctx + n_tok, MLA_SLOT), np.float32)
    sin = np.zeros((n_ctx + n_tok, MLA_SLOT), np.float32)
    t = np.arange(n_tok)
    row = (t // GRID_W).astype(np.float32)
    col = (t % GRID_W).astype(np.float32)
    n_freq = MLA_ROPE // 4
    inv = (ROPE_BASE ** (-np.arange(n_freq, dtype=np.float32) / n_freq)).astype(np.float32)
    ar = row[:, None] * inv
    ac = col[:, None] * inv
    ang = np.concatenate([ar, ar, ac, ac], axis=-1).astype(np.float32)
    cos[n_ctx:, MLA_NOPE:MLA_NOPE + MLA_ROPE] = np.cos(ang)
    sin[n_ctx:, MLA_NOPE:MLA_NOPE + MLA_ROPE] = np.sin(ang)
    return jnp.asarray(cos), jnp.asarray(sin)


def _head_block_ones(n_heads, width):
    m = np.kron(np.eye(n_heads, dtype=np.float32), np.ones((width, width), np.float32))
    return jnp.asarray(m, BF16)


def _mod_kernel(c_ref, w_ref, b_ref, o_ref):
    c = c_ref[...]
    s = c * _sigmoid(c)
    o_ref[...] = jnp.dot(s, w_ref[...], preferred_element_type=F32) + b_ref[...]


def _modulation(cc, mod_w, mod_b):
    r, d = cc.shape
    n = mod_w.shape[1]
    tn = 1024
    return pl.pallas_call(
        _mod_kernel,
        grid=(n // tn,),
        in_specs=[pl.BlockSpec((r, d), lambda j: (0, 0)),
                  pl.BlockSpec((d, tn), lambda j: (0, j)),
                  pl.BlockSpec((1, tn), lambda j: (0, j))],
        out_specs=pl.BlockSpec((r, tn), lambda j: (0, j)),
        out_shape=jax.ShapeDtypeStruct((r, n), F32),
        compiler_params=_cparams(("arbitrary",)),
        name="modulation",
    )(cc, mod_w, mod_b.reshape(1, n))


def _row_select(tile_rows, tile_idx, n_ctx, ctx_vec, lat_vec):
    row = tile_idx * tile_rows + lax.broadcasted_iota(jnp.int32, (tile_rows, 1), 0)
    return jnp.where(row < n_ctx, ctx_vec, lat_vec)


def _inproj_kernel(x_ref, mc_ref, ml_ref, g_ref, w_ref, o_ref, h_ref, *, n_ctx):
    @pl.when(pl.program_id(2) == 0)
    def _():
        tm = x_ref.shape[0]
        i = pl.program_id(1)
        x = x_ref[...]
        y = x * lax.rsqrt(jnp.mean(x * x, axis=-1, keepdims=True) + RMS_EPS)
        y = y * g_ref[...]
        shift = _row_select(tm, i, n_ctx, mc_ref[0:1, :], ml_ref[0:1, :])
        scale = _row_select(tm, i, n_ctx, mc_ref[1:2, :], ml_ref[1:2, :])
        h_ref[...] = (y * (1.0 + scale) + shift).astype(BF16)

    o_ref[...] = jnp.dot(h_ref[...], w_ref[...], preferred_element_type=F32)


def _inproj(xs, mod_c, mod_l, gain, w, n_ctx):
    b, s, d = xs.shape
    tm = _pick_tile(s, 1088)
    tn = 1024
    n_cols = w.shape[1]
    return pl.pallas_call(
        functools.partial(_inproj_kernel, n_ctx=n_ctx),
        grid=(b, s // tm, n_cols // tn),
        in_specs=[pl.BlockSpec((None, tm, d), lambda bi, i, j: (bi, i, 0)),
                  pl.BlockSpec((8, d), lambda bi, i, j: (0, 0)),
                  pl.BlockSpec((None, 8, d), lambda bi, i, j: (bi, 0, 0)),
                  pl.BlockSpec((1, d), lambda bi, i, j: (0, 0)),
                  pl.BlockSpec((d, tn), lambda bi, i, j: (0, j))],
        out_specs=pl.BlockSpec((None, tm, tn), lambda bi, i, j: (bi, i, j)),
        out_shape=jax.ShapeDtypeStruct((b, s, n_cols), F32),
        scratch_shapes=[pltpu.VMEM((tm, d), BF16)],
        compiler_params=_cparams(("parallel", "parallel", "arbitrary")),
        name="inproj",
    )(xs, mod_c, mod_l, gain, w)


def _mla_prep_kernel(ql_ref, kv_ref, cos_ref, sin_ref, gq_ref, gkv_ref, wq_ref, wk_ref, wv_ref, pl_ref,
                     q_ref, k_ref, v_ref, *, scale):
    nslot = MLA_HEADS * MLA_SLOT
    cos = jnp.concatenate([cos_ref[...]] * MLA_HEADS, axis=-1)
    sin = jnp.concatenate([sin_ref[...]] * MLA_HEADS, axis=-1)

    ql = ql_ref[...]
    qn = ql * lax.rsqrt(jnp.mean(ql * ql, axis=-1, keepdims=True) + RMS_EPS) * gq_ref[...]
    q2 = jnp.dot(qn.astype(BF16), wq_ref[...], preferred_element_type=F32)
    q = (q2[:, :nslot] * cos + q2[:, nslot:] * sin) * scale
    q_ref[...] = q.astype(BF16)

    kvkr = kv_ref[...]
    kvl = kvkr[:, :MLA_KV_RANK]
    kr = kvkr[:, MLA_KV_RANK:]
    kvn = (kvl * lax.rsqrt(jnp.mean(kvl * kvl, axis=-1, keepdims=True) + RMS_EPS) * gkv_ref[...]).astype(BF16)
    kn = jnp.dot(kvn, wk_ref[...], preferred_element_type=F32)
    v_ref[...] = jnp.dot(kvn, wv_ref[...], preferred_element_type=F32).astype(BF16)
    kr_hi = kr.astype(BF16)
    kr_lo = (kr - kr_hi.astype(F32)).astype(BF16)
    kr2 = jnp.dot(jnp.concatenate([kr_hi, kr_lo], axis=-1), pl_ref[...], preferred_element_type=F32)
    k_ref[...] = (kn + kr2[:, :nslot] * cos + kr2[:, nslot:] * sin).astype(BF16)


def _mla_prep(p, cos, sin, gq, gkv, wq, wk, wv, place):
    b, s, _ = p.shape
    tm = _pick_tile(s, 544)
    nslot = MLA_HEADS * MLA_SLOT
    scale = (MLA_NOPE + MLA_ROPE) ** -0.5
    return pl.pallas_call(
        functools.partial(_mla_prep_kernel, scale=scale),
        grid=(b, s // tm),
        in_specs=[pl.BlockSpec((None, tm, MLA_Q_RANK), lambda bi, i: (bi, i, P_MLA_Q // MLA_Q_RANK)),
                  pl.BlockSpec((None, tm, 2 * MLA_SLOT), lambda bi, i: (bi, i, P_MLA_KV // (2 * MLA_SLOT))),
                  pl.BlockSpec((tm, MLA_SLOT), lambda bi, i: (i, 0)),
                  pl.BlockSpec((tm, MLA_SLOT), lambda bi, i: (i, 0)),
                  _full(gq), _full(gkv), _full(wq), _full(wk), _full(wv), _full(place)],
        out_specs=[pl.BlockSpec((None, tm, nslot), lambda bi, i: (bi, i, 0)),
                   pl.BlockSpec((None, tm, nslot), lambda bi, i: (bi, i, 0)),
                   pl.BlockSpec((None, tm, MLA_HEADS * MLA_V), lambda bi, i: (bi, i, 0))],
        out_shape=[jax.ShapeDtypeStruct((b, s, nslot), BF16),
                   jax.ShapeDtypeStruct((b, s, nslot), BF16),
                   jax.ShapeDtypeStruct((b, s, MLA_HEADS * MLA_V), BF16)],
        compiler_params=_cparams(("parallel", "parallel")),
        name="mla_prep",
    )(p, p, cos, sin, gq, gkv, wq, wk, wv, place)


def _flash_update(h, q, k, v, m_ref, l_ref, acc_ref):
    s = lax.dot_general(q, k, (((1,), (1,)), ((), ())), preferred_element_type=F32)
    m_prev = m_ref[h]
    m_new = jnp.maximum(m_prev, jnp.max(s, axis=-1, keepdims=True))
    alpha = jnp.exp(m_prev - m_new)
    p = jnp.exp(s - jnp.concatenate([m_new] * (s.shape[1] // 128), axis=-1))
    l_ref[h] = alpha * l_ref[h] + jnp.sum(p, axis=-1, keepdims=True)
    acc_ref[h] = alpha * acc_ref[h] + jnp.dot(p.astype(BF16), v, preferred_element_type=F32)
    m_ref[h] = m_new


def _mla_flash_kernel(q_ref, k_ref, v_ref, o_ref, m_ref, l_ref, acc_ref, *, tk, n_ctx, tile_off):
    i = pl.program_id(1) + tile_off
    tq = q_ref.shape[0]
    n_lat = k_ref.shape[0] - n_ctx
    m_ref[...] = jnp.full(m_ref.shape, -jnp.inf, F32)
    l_ref[...] = jnp.zeros(l_ref.shape, F32)
    acc_ref[...] = jnp.zeros(acc_ref.shape, F32)
    ctk = min(tk, n_ctx)
    for h in range(MLA_HEADS):
        hs = slice(h * MLA_SLOT, (h + 1) * MLA_SLOT)
        vs = slice((h // 2) * 128, (h // 2) * 128 + 128)
        q = q_ref[:, hs]
        for c in range(n_ctx // ctk):
            _flash_update(h, q, k_ref[c * ctk:(c + 1) * ctk, hs], v_ref[c * ctk:(c + 1) * ctk, vs],
                          m_ref, l_ref, acc_ref)

        @pl.when(i * tq >= n_ctx)
        def _():
            def body(j, carry):
                rows = pl.ds(pl.multiple_of(n_ctx + j * tk, 8), tk)
                _flash_update(h, q, k_ref[rows, hs], v_ref[rows, vs], m_ref, l_ref, acc_ref)
                return carry
            lax.fori_loop(0, n_lat // tk, body, 0)

    lane = lax.broadcasted_iota(jnp.int32, (tq, 128), 1)
    outs = []
    for pair in range(MLA_HEADS // 2):
        o0 = acc_ref[2 * pair] / l_ref[2 * pair]
        o1 = acc_ref[2 * pair + 1] / l_ref[2 * pair + 1]
        outs.append(jnp.where(lane < MLA_V, o0, o1))
    o_ref[...] = jnp.concatenate(outs, axis=-1).astype(o_ref.dtype)


def _mla_flash(q, k, v, n_ctx, with_ctx_queries):
    b, s, nslot = q.shape
    tq = SEQ_TILE
    tk = 512
    assert n_ctx % tq == 0 and (s - n_ctx) % tk == 0
    off = 0 if with_ctx_queries else n_ctx // tq
    return pl.pallas_call(
        functools.partial(_mla_flash_kernel, tk=tk, n_ctx=n_ctx, tile_off=off),
        grid=(b, s // tq - off),
        in_specs=[pl.BlockSpec((None, tq, nslot), lambda bi, i: (bi, i + off, 0)),
                  pl.BlockSpec((None, s, nslot), lambda bi, i: (bi, 0, 0)),
                  pl.BlockSpec((None, s, MLA_HEADS * MLA_V), lambda bi, i: (bi, 0, 0))],
        out_specs=pl.BlockSpec((None, tq, MLA_HEADS * MLA_V), lambda bi, i: (bi, i + off, 0)),
        out_shape=jax.ShapeDtypeStruct((b, s, MLA_HEADS * MLA_V), BF16),
        scratch_shapes=[pltpu.VMEM((MLA_HEADS, tq, 128), F32),
                        pltpu.VMEM((MLA_HEADS, tq, 128), F32),
                        pltpu.VMEM((MLA_HEADS, tq, 128), F32)],
        compiler_params=_cparams(("parallel", "arbitrary")),
        name="mla_flash",
    )(q, k, v)


NA_QROWS = 4
NA_SLAB = NA_QROWS + NA_WIN_ROWS


def _na_geometry(rows):
    assert rows % NA_QROWS == 0 and rows >= NA_SLAB
    nblk = rows // NA_QROWS
    qc = np.arange(GRID_W)
    cs = np.clip(qc - NA_WIN_COLS // 2, 0, GRID_W - NA_WIN_COLS)
    col_valid = (qc[None, :] >= cs[:, None]) & (qc[None, :] < cs[:, None] + NA_WIN_COLS)
    rel_c = np.clip(qc[None, :] - qc[:, None] + NA_WIN_COLS - 1, 0, 2 * NA_WIN_COLS - 2)
    onehot_c = (rel_c[None] == np.arange(2 * NA_WIN_COLS - 1)[:, None, None]) & col_valid[None]
    patterns, cls, starts = {}, [], []
    for i in range(nblk):
        r0 = i * NA_QROWS
        start = int(np.clip(r0 - NA_WIN_ROWS // 2, 0, rows - NA_SLAB))
        qr = r0 + np.arange(NA_QROWS)
        rs = np.clip(qr - NA_WIN_ROWS // 2, 0, rows - NA_WIN_ROWS)
        key = (start - r0,) + tuple((rs - r0).tolist())
        if key not in patterns:
            kr = start + np.arange(NA_SLAB)
            row_valid = (kr[None, :] >= rs[:, None]) & (kr[None, :] < rs[:, None] + NA_WIN_ROWS)
            rel_r = np.clip(kr[None, :] - qr[:, None] + NA_WIN_ROWS - 1, 0, 2 * NA_WIN_ROWS - 2)
            onehot_r = (rel_r[..., None] == np.arange(2 * NA_WIN_ROWS - 1)) & row_valid[..., None]
            patterns[key] = (len(patterns), onehot_r)
        cls.append(patterns[key][0])
        starts.append(start)
    ordered = sorted(patterns.values(), key=lambda z: z[0])
    onehot_r = np.stack([z[1] for z in ordered]).astype(np.float32)
    meta = np.stack([np.asarray(cls, np.int32), np.asarray(starts, np.int32)])
    return meta, onehot_r, onehot_c.astype(np.float32)


def _na_bias_table(rpb, onehot_r, onehot_c):
    hi = lax.Precision.HIGHEST
    bc = jnp.einsum('hrc,cqk->hrqk', rpb.astype(F32), onehot_c, precision=hi)
    tab = jnp.einsum('pabr,hrqk->phaqbk', onehot_r, bc, precision=hi)
    valid = jnp.einsum('pabr,cqk->paqbk', onehot_r, onehot_c, precision=hi) > 0.5
    tab = jnp.where(valid[:, None], tab, -jnp.inf)
    p, h = tab.shape[:2]
    return tab.reshape(p, h, NA_QROWS * GRID_W, NA_SLAB * GRID_W)


def _na_attend(q_pair, lane, h, k_parts, v_parts, bias):
    in_head = (lane < NA_DH) if h % 2 == 0 else (lane >= NA_DH)
    qm = jnp.where(in_head, q_pair * (NA_DH ** -0.5), 0.0).astype(BF16)
    scores = [lax.dot_general(qm, k, (((1,), (1,)), ((), ())), preferred_element_type=F32) for k in k_parts]
    if bias is not None:
        scores[0] = scores[0] + bias
    m = scores[0].max(axis=-1, keepdims=True)
    for s in scores[1:]:
        m = jnp.maximum(m, s.max(axis=-1, keepdims=True))
    den = None
    out = None
    for s, v in zip(scores, v_parts):
        p = jnp.exp(s - m)
        ps = p.sum(axis=-1, keepdims=True)
        po = jnp.dot(p.astype(BF16), v, preferred_element_type=F32)
        den = ps if den is None else den + ps
        out = po if out is None else out + po
    return out / den


def _na_kernel(meta_ref, q_ref, k_ref, v_ref, bias_ref, o_ref, *, n_ctx, tile_off):
    i = pl.program_id(1) + tile_off
    nq = q_ref.shape[0]
    n_ctx_tiles = n_ctx // nq
    lane = lax.broadcasted_iota(jnp.int32, (nq, 128), 1)

    def run(windowed):
        outs = []
        for pair in range(NA_HEADS // 2):
            ls = slice(pair * 128, (pair + 1) * 128)
            q_pair = q_ref[:, ls]
            k_ctx = k_ref[0:n_ctx, ls].astype(BF16)
            v_ctx = v_ref[0:n_ctx, ls].astype(BF16)
            if windowed:
                start = pl.multiple_of(n_ctx + meta_ref[1, i - n_ctx_tiles] * GRID_W, GRID_W)
                rows = pl.ds(start, NA_SLAB * GRID_W)
                ks = (k_ref[rows, ls].astype(BF16), k_ctx)
                vs = (v_ref[rows, ls].astype(BF16), v_ctx)
            else:
                ks, vs = (k_ctx,), (v_ctx,)
            o = [_na_attend(q_pair, lane, 2 * pair + e, ks, vs, bias_ref[2 * pair + e] if windowed else None)
                 for e in range(2)]
            outs.append(jnp.where(lane < NA_DH, o[0], o[1]))
        o_ref[...] = jnp.concatenate(outs, axis=-1).astype(o_ref.dtype)

    if tile_off == 0:
        pl.when(i < n_ctx_tiles)(lambda: run(False))
        pl.when(i >= n_ctx_tiles)(lambda: run(True))
    else:
        run(True)


def _na_attention(p, meta, table, n_ctx, with_ctx_queries):
    b, s, _ = p.shape
    nq = NA_QROWS * GRID_W
    nk = NA_SLAB * GRID_W
    assert n_ctx % nq == 0
    nct = n_ctx // nq
    off = 0 if with_ctx_queries else nct
    qb = P_NA_QKV // NA_W
    grid_spec = pltpu.PrefetchScalarGridSpec(
        num_scalar_prefetch=1,
        grid=(b, s // nq - off),
        in_specs=[pl.BlockSpec((None, nq, NA_W), lambda bi, i, m: (bi, i + off, qb)),
                  pl.BlockSpec((None, s, NA_W), lambda bi, i, m: (bi, 0, qb + 1)),
                  pl.BlockSpec((None, s, NA_W), lambda bi, i, m: (bi, 0, qb + 2)),
                  pl.BlockSpec((None, NA_HEADS, nq, nk),
                               lambda bi, i, m: (m[0, jnp.maximum(i + off - nct, 0)], 0, 0, 0))],
        out_specs=pl.BlockSpec((None, nq, NA_W), lambda bi, i, m: (bi, i + off, 0)))
    return pl.pallas_call(
        functools.partial(_na_kernel, n_ctx=n_ctx, tile_off=off),
        grid_spec=grid_spec,
        out_shape=jax.ShapeDtypeStruct((b, s, NA_W), BF16),
        compiler_params=_cparams(("parallel", "arbitrary")),
        name="na_attention",
    )(meta, p, p, p, table)


def _tile_conv(x, prev, nxt, w_ref, tile_idx, n_tiles, n_ctx_tiles):
    r = x.shape[0]
    width = w_ref.shape[0]
    left = width // 2
    has_prev = jnp.logical_and(tile_idx != 0, tile_idx != n_ctx_tiles)
    has_next = jnp.logical_and(tile_idx != n_tiles - 1, tile_idx != n_ctx_tiles - 1)
    prev = jnp.where(has_prev, prev, 0.0)
    nxt = jnp.where(has_next, nxt, 0.0)
    xe = jnp.concatenate([prev, x, nxt], axis=0)
    acc = None
    for j in range(width):
        o = HALO - left + j
        term = xe[o:o + r, :] * w_ref[j:j + 1, :]
        acc = term if acc is None else acc + term
    return acc


def _halo_specs(width, col_block, tile_of):
    per = SEQ_TILE // HALO

    def main(bi, s, *_):
        return (bi, tile_of(s), col_block)

    def prev(bi, s, *_):
        return (bi, jnp.maximum(tile_of(s) * per - 1, 0), col_block)

    def make_next(n_tiles):
        def nxt(bi, s, *_):
            return (bi, jnp.minimum((tile_of(s) + 1) * per, n_tiles * per - 1), col_block)
        return nxt

    return main, prev, make_next


def _gdn_prep_kernel(x_ref, xp_ref, xn_ref, ba_ref, cw_ref, ones_ref, exp_ref, alog_ref, dtb_ref,
                     q_ref, k_ref, v_ref, beta_ref, g_ref, *, n_ctx_tiles):
    i = pl.program_id(1)
    y = _tile_conv(x_ref[...], xp_ref[...], xn_ref[...], cw_ref, i, pl.num_programs(1), n_ctx_tiles)
    y = y * _sigmoid(y)
    q = y[:, :GDN_W]
    k = y[:, GDN_W:2 * GDN_W]
    v_ref[...] = y[:, 2 * GDN_W:]

    def head_norm(u):
        parts = jnp.concatenate(_split3(u * u), axis=-1)
        ss = jnp.dot(parts, ones_ref[...], preferred_element_type=F32)
        return u * lax.rsqrt(ss + RMS_EPS)

    q_ref[...] = head_norm(q) * (GDN_DK ** -0.5)
    k_ref[...] = head_norm(k)

    logits = jnp.dot(jnp.concatenate(_split3(ba_ref[...]), axis=-1), exp_ref[...], preferred_element_type=F32)
    half = 2 * GDN_W
    beta_ref[...] = _sigmoid(logits[:, :half])
    a = logits[:, half:] + dtb_ref[...]
    softplus = jnp.maximum(a, 0.0) + jnp.log1p(jnp.exp(-jnp.abs(a)))
    g_ref[...] = -jnp.exp(alog_ref[...]) * softplus


def _gdn_prep(p, conv_w, a_log, dt_bias, n_ctx):
    b, s, _ = p.shape
    n_tiles = s // SEQ_TILE
    main, prev, make_next = _halo_specs(3 * GDN_W, 0, lambda t: t)
    ones3 = jnp.concatenate([_head_block_ones(GDN_HEADS, GDN_DK)] * 3, axis=0)
    expand = np.zeros((128, 4 * GDN_W), np.float32)
    for kind in range(2):
        for d in range(2):
            for h in range(GDN_HEADS):
                c0 = kind * 2 * GDN_W + d * GDN_W + h * GDN_DV
                expand[kind * 2 * GDN_HEADS + d * GDN_HEADS + h, c0:c0 + GDN_DV] = 1.0
    expand3 = jnp.asarray(np.concatenate([expand] * 3, axis=0), BF16)
    alog_e = jnp.repeat(a_log.astype(F32).reshape(-1), GDN_DV).reshape(1, 2 * GDN_W)
    dtb_e = jnp.repeat(dt_bias.astype(F32).reshape(-1), GDN_DV).reshape(1, 2 * GDN_W)
    tok = lambda w: pl.BlockSpec((None, SEQ_TILE, w), lambda bi, i: (bi, i, 0))
    return pl.pallas_call(
        functools.partial(_gdn_prep_kernel, n_ctx_tiles=n_ctx // SEQ_TILE),
        grid=(b, n_tiles),
        in_specs=[pl.BlockSpec((None, SEQ_TILE, 3 * GDN_W), main),
                  pl.BlockSpec((None, HALO, 3 * GDN_W), prev),
                  pl.BlockSpec((None, HALO, 3 * GDN_W), make_next(n_tiles)),
                  pl.BlockSpec((None, SEQ_TILE, 128), lambda bi, i: (bi, i, P_GDN_BA // 128)),
                  _full(conv_w), _full(ones3), _full(expand3), _full(alog_e), _full(dtb_e)],
        out_specs=[tok(GDN_W), tok(GDN_W), tok(GDN_W), tok(2 * GDN_W), tok(2 * GDN_W)],
        out_shape=[jax.ShapeDtypeStruct((b, s, GDN_W), F32)] * 3 + [jax.ShapeDtypeStruct((b, s, 2 * GDN_W), F32)] * 2,
        compiler_params=_cparams(("parallel", "parallel")),
        name="gdn_prep",
    )(p, p, p, p, conv_w, ones3, expand3, alog_e, dtb_e)


def _blockdiag(y, bd_mask):
    yb = y.astype(BF16)
    return jnp.where(bd_mask, jnp.concatenate([yb] * GDN_HEADS, axis=0), jnp.zeros((), BF16))


def _mm(a, b):
    return jnp.dot(a.astype(BF16), b, preferred_element_type=F32)


def _gdn_chunk(q, k, v, beta, g, s_ref, consts, reverse):
    bd_mask, tri3, eye4, incl, strict, level = consts
    c = GDN_CHUNK
    gparts = jnp.concatenate(_split3(g), axis=0)
    gc = jnp.dot(tri3, gparts, preferred_element_type=F32)
    g_last = gc[0:1, :] if reverse else gc[c - 1:c, :]
    gc_row = jnp.sum(jnp.where(eye4, gc, 0.0), axis=0, keepdims=True)
    decay = jnp.where(incl, jnp.exp(jnp.where(incl, gc - gc_row, 0.0)), 0.0)
    kexp = _blockdiag(k, bd_mask)
    gram = lax.dot_general(jnp.concatenate([k, q], axis=0).astype(BF16), kexp, (((1,), (1,)), ((), ())),
                           preferred_element_type=F32)
    kk = gram[:c]
    qk = gram[c:]
    lower = jnp.where(strict, beta * kk * decay, 0.0)
    a_intra = jnp.where(incl, qk * decay, 0.0)
    t = jnp.where(eye4, 1.0, 0.0) - jnp.where(level == 1, lower, 0.0)
    for lev in range(2, 7):
        off = jnp.where(level == lev, lower, 0.0)
        t = t - _mm(_mm(t, _blockdiag(off, bd_mask)), _blockdiag(t, bd_mask))
    e_gc = jnp.exp(gc)
    u = _mm(t, _blockdiag(v * beta, bd_mask))
    w = _mm(t, _blockdiag(k * beta * e_gc, bd_mask))
    q_dec = q * e_gc
    k_dec = k * jnp.exp(g_last - gc)
    s_bd = s_ref[...]
    ws_qs = _mm(jnp.concatenate([w, q_dec], axis=0), s_bd.astype(BF16))
    v_new = u - ws_qs[:c]
    o = ws_qs[c:] + _mm(a_intra, _blockdiag(v_new, bd_mask))
    kv = lax.dot_general(k_dec.astype(BF16), v_new.astype(BF16), (((0,), (0,)), ((), ())),
                         preferred_element_type=F32)
    s_ref[...] = s_bd * jnp.exp(g_last) + jnp.where(bd_mask, kv, 0.0)
    return o


def _gdn_consts(reverse):
    c = GDN_CHUNK
    w = GDN_W
    r2 = lax.broadcasted_iota(jnp.int32, (w, w), 0)
    c2 = lax.broadcasted_iota(jnp.int32, (w, w), 1)
    bd_mask = (r2 // c) == (c2 // c)
    i = lax.broadcasted_iota(jnp.int32, (c, w), 0)
    j = lax.broadcasted_iota(jnp.int32, (c, w), 1) % c
    eye4 = i == j
    incl = (j >= i) if reverse else (j <= i)
    strict = (j > i) if reverse else (j < i)
    x = i ^ j
    level = jnp.zeros((c, w), jnp.int32)
    for bit in range(6):
        level = level + (x >= (1 << bit)).astype(jnp.int32)
    ti = lax.broadcasted_iota(jnp.int32, (c, 3 * c), 0)
    tj = lax.broadcasted_iota(jnp.int32, (c, 3 * c), 1) % c
    tri3 = jnp.where((tj >= ti) if reverse else (tj <= ti), 1.0, 0.0).astype(BF16)
    return bd_mask, tri3, eye4, incl, strict, level


def _gdn_scan_kernel(qf, kf, vf, bf, gf, qb, kb, vb, bb, gb, of_ref, ob_ref, sf_ref, sb_ref):
    @pl.when(pl.program_id(1) == 0)
    def _():
        sf_ref[...] = jnp.zeros(sf_ref.shape, F32)
        sb_ref[...] = jnp.zeros(sb_ref.shape, F32)

    cf = _gdn_consts(False)
    cb = _gdn_consts(True)
    n = SEQ_TILE // GDN_CHUNK
    for c in range(n):
        rf = slice(c * GDN_CHUNK, (c + 1) * GDN_CHUNK)
        rb = slice((n - 1 - c) * GDN_CHUNK, (n - c) * GDN_CHUNK)
        of_ref[rf, :] = _gdn_chunk(qf[rf, :], kf[rf, :], vf[rf, :], bf[rf, :], gf[rf, :], sf_ref, cf, False)
        ob_ref[rb, :] = _gdn_chunk(qb[rb, :], kb[rb, :], vb[rb, :], bb[rb, :], gb[rb, :], sb_ref, cb, True)


def _seq_tile_maps(n_ctx_tiles, n_tiles):
    fwd = lambda s: s
    bwd = lambda s: jnp.where(s < n_ctx_tiles, n_ctx_tiles - 1 - s, n_ctx_tiles + n_tiles - 1 - s)
    return fwd, bwd


def _gdn_scan(q, k, v, beta, g, n_ctx):
    b, s, _ = q.shape
    n_tiles = s // SEQ_TILE
    fwd, bwd = _seq_tile_maps(n_ctx // SEQ_TILE, n_tiles)
    spec = lambda tile_of, col: pl.BlockSpec((None, SEQ_TILE, GDN_W), lambda bi, t: (bi, tile_of(t), col))
    in_specs = [spec(fwd, 0)] * 5 + [spec(bwd, 0)] * 3 + [spec(bwd, 1)] * 2
    return pl.pallas_call(
        _gdn_scan_kernel,
        grid=(b, n_tiles),
        in_specs=in_specs,
        out_specs=[spec(fwd, 0), spec(bwd, 0)],
        out_shape=[jax.ShapeDtypeStruct((b, s, GDN_W), F32)] * 2,
        scratch_shapes=[pltpu.VMEM((GDN_W, GDN_W), F32), pltpu.VMEM((GDN_W, GDN_W), F32)],
        compiler_params=_cparams(("parallel", "arbitrary")),
        name="gdn_scan",
    )(q, k, v, beta, g, q, k, v, beta, g)


def _lru_tile_prep(x_ref, xp_ref, xn_ref, cw_ref, cb_ref, wg_ref, bg_ref, nla_ref, a_ref, b_ref,
                   tile_idx, n_tiles, n_ctx_tiles):
    xb = _tile_conv(x_ref[...], xp_ref[...], xn_ref[...], cw_ref, tile_idx, n_tiles, n_ctx_tiles) + cb_ref[...]
    gates = _sigmoid(jnp.dot(xb.astype(BF16), wg_ref[...], preferred_element_type=F32) + bg_ref[...])
    log_a = nla_ref[...] * gates[:, :LRU_W]
    a_ref[...] = jnp.exp(log_a)
    th = jnp.tanh(log_a)
    b_ref[...] = jnp.sqrt(-2.0 * th / (1.0 - th)) * gates[:, LRU_W:] * xb


def _scan_group(a, b, h, row, reverse):
    for d in (1, 2, 4):
        if reverse:
            keep = row < 8 - d
            shift = 8 - d
        else:
            keep = row >= d
            shift = d
        a_s = jnp.where(keep, pltpu.roll(a, shift, 0), 1.0)
        b_s = jnp.where(keep, pltpu.roll(b, shift, 0), 0.0)
        b = a * b_s + b
        a = a * a_s
    return a * h + b


def _lru_scan_kernel(xf, xfp, xfn, xb, xbp, xbn, cw_ref, cb_ref, wgf, bgf, nlaf, wgb, bgb, nlab,
                     hf_ref, hb_ref, af_ref, bf_ref, ab_ref, bb_ref, cf_ref, cbk_ref, *, n_ctx_tiles):
    s = pl.program_id(1)
    n_tiles = pl.num_programs(1)

    @pl.when(s == 0)
    def _():
        cf_ref[...] = jnp.zeros(cf_ref.shape, F32)
        cbk_ref[...] = jnp.zeros(cbk_ref.shape, F32)

    t_b = jnp.where(s < n_ctx_tiles, n_ctx_tiles - 1 - s, n_ctx_tiles + n_tiles - 1 - s)
    _lru_tile_prep(xf, xfp, xfn, cw_ref, cb_ref, wgf, bgf, nlaf, af_ref, bf_ref, s, n_tiles, n_ctx_tiles)
    _lru_tile_prep(xb, xbp, xbn, cw_ref, cb_ref, wgb, bgb, nlab, ab_ref, bb_ref, t_b, n_tiles, n_ctx_tiles)

    n_groups = SEQ_TILE // 8
    row = lax.broadcasted_iota(jnp.int32, (8, LRU_W), 0)

    def body(gi, carry):
        h_f, h_b = carry
        rf = pl.ds(pl.multiple_of(gi * 8, 8), 8)
        rb = pl.ds(pl.multiple_of((n_groups - 1 - gi) * 8, 8), 8)
        out_f = _scan_group(af_ref[rf, :], bf_ref[rf, :], h_f, row, False)
        out_b = _scan_group(ab_ref[rb, :], bb_ref[rb, :], h_b, row, True)
        hf_ref[rf, :] = out_f
        hb_ref[rb, :] = out_b
        return (jnp.broadcast_to(out_f[7:8, :], (8, LRU_W)), jnp.broadcast_to(out_b[0:1, :], (8, LRU_W)))

    h_f, h_b = lax.fori_loop(0, n_groups, body, (cf_ref[...], cbk_ref[...]))
    cf_ref[...] = h_f
    cbk_ref[...] = h_b


def _lru_scan(p, conv_w, conv_b, w_r, b_r, w_i, b_i, lam, n_ctx):
    b, s, _ = p.shape
    n_tiles = s // SEQ_TILE
    nct = n_ctx // SEQ_TILE
    fwd, bwd = _seq_tile_maps(nct, n_tiles)
    col = P_LRU_X // LRU_W
    specs = []
    for tile_of in (fwd, bwd):
        main, prev, make_next = _halo_specs(LRU_W, col, tile_of)
        specs += [pl.BlockSpec((None, SEQ_TILE, LRU_W), main), pl.BlockSpec((None, HALO, LRU_W), prev),
                  pl.BlockSpec((None, HALO, LRU_W), make_next(n_tiles))]

    def blockdiag(w):
        return jax.scipy.linalg.block_diag(*[w[n] for n in range(LRU_BLOCKS)])

    dir_args = []
    for d in range(2):
        wg = jnp.concatenate([blockdiag(w_r[d]), blockdiag(w_i[d])], axis=1).astype(BF16)
        bg = jnp.concatenate([b_r[d], b_i[d]]).astype(F32).reshape(1, 2 * LRU_W)
        nla = (-LRU_C * jax.nn.softplus(-lam[d].astype(F32))).reshape(1, LRU_W)
        dir_args += [wg, bg, nla]
    cb2 = conv_b.reshape(1, LRU_W)
    out_spec = lambda tile_of: pl.BlockSpec((None, SEQ_TILE, LRU_W), lambda bi, t: (bi, tile_of(t), 0))
    return pl.pallas_call(
        functools.partial(_lru_scan_kernel, n_ctx_tiles=nct),
        grid=(b, n_tiles),
        in_specs=specs + [_full(conv_w), _full(cb2)] + [_full(a) for a in dir_args],
        out_specs=[out_spec(fwd), out_spec(bwd)],
        out_shape=[jax.ShapeDtypeStruct((b, s, LRU_W), F32)] * 2,
        scratch_shapes=[pltpu.VMEM((SEQ_TILE, LRU_W), F32)] * 4 + [pltpu.VMEM((8, LRU_W), F32)] * 2,
        compiler_params=_cparams(("parallel", "arbitrary")),
        name="lru_scan",
    )(p, p, p, p, p, p, conv_w, cb2, *dir_args)


def _post_kernel(of_ref, ob_ref, z_ref, gn_ref, ones_ref, hf_ref, hb_ref, y_ref, ua_ref, ub_ref):
    o = of_ref[...] + ob_ref[...]
    ms = jnp.dot(jnp.concatenate(_split3(o * o), axis=-1), ones_ref[...], preferred_element_type=F32) * (1.0 / GDN_DV)
    z = z_ref[...]
    ua_ref[...] = (o * lax.rsqrt(ms + RMS_EPS) * gn_ref[...] * (z * _sigmoid(z))).astype(ua_ref.dtype)
    y = y_ref[...]
    gelu = 0.5 * y * (1.0 + jnp.tanh(0.7978845608028654 * (y + 0.044715 * (y * y * y))))
    ub_ref[...] = ((hf_ref[...] + hb_ref[...]) * gelu).astype(ub_ref.dtype)


def _post(o_f, o_b, h_f, h_b, p, gdn_norm_g):
    b, s, _ = p.shape
    tm = _pick_tile(s, 544)
    ones3 = jnp.concatenate([_head_block_ones(GDN_HEADS, GDN_DV)] * 3, axis=0)
    gn = jnp.tile(gdn_norm_g.astype(F32), GDN_HEADS).reshape(1, GDN_W)
    tok = pl.BlockSpec((None, tm, GDN_W), lambda bi, i: (bi, i, 0))
    pcol = lambda c: pl.BlockSpec((None, tm, GDN_W), lambda bi, i: (bi, i, c))
    return pl.pallas_call(
        _post_kernel,
        grid=(b, s // tm),
        in_specs=[tok, tok, pcol(P_GDN_Z // GDN_W), _full(gn), _full(ones3), tok, tok, pcol(P_LRU_Y // LRU_W)],
        out_specs=[tok, tok],
        out_shape=[jax.ShapeDtypeStruct((b, s, GDN_W), BF16)] * 2,
        compiler_params=_cparams(("parallel", "parallel")),
        name="gdn_lru_post",
    )(o_f, o_b, p, gn, ones3, h_f, h_b, p)


def _merge_kernel(x_ref, mc_ref, ml_ref, u0_ref, u1_ref, u2_ref, u3_ref, t0_ref, t1_ref, t2_ref, t3_ref,
                  bg_ref, wb_ref, wo_ref, o_ref, *, n_ctx, tile_off):
    merged = None
    for n, (u_ref, t_ref) in enumerate(((u0_ref, t0_ref), (u1_ref, t1_ref), (u2_ref, t2_ref), (u3_ref, t3_ref))):
        gate = _sigmoid(t_ref[...] + bg_ref[n:n + 1, :])
        term = gate * jnp.dot(u_ref[...], wb_ref[n], preferred_element_type=F32)
        merged = term if merged is None else merged + term
    out = jnp.dot(merged.astype(BF16), wo_ref[...], preferred_element_type=F32)
    g1 = _row_select(x_ref.shape[0], pl.program_id(1) + tile_off, n_ctx, mc_ref[2:3, :], ml_ref[2:3, :])
    o_ref[...] = x_ref[...] + g1 * out


def _merge(xs, mod_c, mod_l, branches, p, b_gate, w_branch, w_out, n_ctx, with_ctx):
    b, s, d = xs.shape
    tm = SEQ_TILE
    off = 0 if with_ctx else n_ctx // tm
    gate_blk = P_GATES // d
    tok = lambda w: pl.BlockSpec((None, tm, w), lambda bi, i: (bi, i + off, 0))
    in_specs = [tok(d), pl.BlockSpec((8, d), lambda bi, i: (0, 0)), pl.BlockSpec((None, 8, d), lambda bi, i: (bi, 0, 0))]
    in_specs += [tok(BRANCH_W)] * N_BRANCH
    in_specs += [pl.BlockSpec((None, tm, d), functools.partial(lambda bi, i, n: (bi, i + off, gate_blk + n), n=n))
                 for n in range(N_BRANCH)]
    in_specs += [_full(b_gate), _full(w_branch), _full(w_out)]
    return pl.pallas_call(
        functools.partial(_merge_kernel, n_ctx=n_ctx, tile_off=off),
        grid=(b, s // tm - off),
        in_specs=in_specs,
        out_specs=tok(d),
        out_shape=jax.ShapeDtypeStruct((b, s, d), F32),
        input_output_aliases={0: 0},
        compiler_params=_cparams(("parallel", "parallel")),
        name="merge",
    )(xs, mod_c, mod_l, *branches, p, p, p, p, b_gate, w_branch, w_out)


def _mlp_kernel(x_ref, mc_ref, ml_ref, gn_ref, gf_ref, w1_ref, w2_ref, o_ref, h_ref, acc_ref, *, n_ctx, final_norm):
    f = pl.program_id(2)
    tm = x_ref.shape[0]
    i = pl.program_id(1)

    @pl.when(f == 0)
    def _():
        x = x_ref[...]
        y = x * lax.rsqrt(jnp.mean(x * x, axis=-1, keepdims=True) + RMS_EPS)
        y = y * gn_ref[...]
        shift = _row_select(tm, i, n_ctx, mc_ref[3:4, :], ml_ref[3:4, :])
        scale = _row_select(tm, i, n_ctx, mc_ref[4:5, :], ml_ref[4:5, :])
        h_ref[...] = (y * (1.0 + scale) + shift).astype(BF16)

    a = jnp.maximum(jnp.dot(h_ref[...], w1_ref[...], preferred_element_type=F32), 0.0)
    part = jnp.dot((a * a).astype(BF16), w2_ref[...], preferred_element_type=F32)

    @pl.when(f == 0)
    def _():
        acc_ref[...] = part

    @pl.when(f > 0)
    def _():
        acc_ref[...] += part

    @pl.when(f == pl.num_programs(2) - 1)
    def _():
        g2 = _row_select(tm, i, n_ctx, mc_ref[5:6, :], ml_ref[5:6, :])
        y = x_ref[...] + g2 * acc_ref[...]
        if final_norm:
            y = y * lax.rsqrt(jnp.mean(y * y, axis=-1, keepdims=True) + RMS_EPS) * gf_ref[...]
        o_ref[...] = y


def _mlp(xs, mod_c, mod_l, gain, w1, w2, final_gain, n_ctx, final_norm):
    b, s, d = xs.shape
    dff = w1.shape[1]
    tm = _pick_tile(s, 1088)
    tf = 1024
    row = pl.BlockSpec((1, d), lambda bi, i, f: (0, 0))
    return pl.pallas_call(
        functools.partial(_mlp_kernel, n_ctx=n_ctx, final_norm=final_norm),
        grid=(b, s // tm, dff // tf),
        in_specs=[pl.BlockSpec((None, tm, d), lambda bi, i, f: (bi, i, 0)),
                  pl.BlockSpec((8, d), lambda bi, i, f: (0, 0)),
                  pl.BlockSpec((None, 8, d), lambda bi, i, f: (bi, 0, 0)),
                  row, row,
                  pl.BlockSpec((d, tf), lambda bi, i, f: (0, f)),
                  pl.BlockSpec((tf, d), lambda bi, i, f: (f, 0))],
        out_specs=pl.BlockSpec((None, tm, d), lambda bi, i, f: (bi, i, 0)),
        out_shape=jax.ShapeDtypeStruct((b, s, d), F32),
        scratch_shapes=[pltpu.VMEM((tm, d), BF16), pltpu.VMEM((tm, d), F32)],
        compiler_params=_cparams(("parallel", "parallel", "arbitrary")),
        name="mlp",
    )(xs, mod_c, mod_l, gain, final_gain, w1, w2)


def kernel(x, c, ctx, c_ctx, mod_w, mod_b, norm1_g, norm2_g, w_in, b_gate, gdn_conv_w, gdn_a_log, gdn_dt_bias,
           gdn_norm_g, lru_conv_w, lru_conv_b, lru_w_r, lru_b_r, lru_w_i, lru_b_i, lru_lambda, mla_q_norm_g,
           mla_w_uq, mla_kv_norm_g, mla_w_ukv, na_rpb, w_branch, w_out, mlp_w1, mlp_w2, final_norm_g):
    bsz, n_tok, d = x.shape
    n_ctx = ctx.shape[1]
    depth = w_in.shape[0]
    assert n_ctx % SEQ_TILE == 0 and n_tok % SEQ_TILE == 0 and n_tok % GRID_W == 0
    na_meta, onehot_r, onehot_c = _na_geometry(n_tok // GRID_W)
    na_meta = jnp.asarray(na_meta)
    cos, sin = _rope_tables(n_ctx, n_tok)

    n_rows = -(-(bsz + 1) // 8) * 8
    cc = jnp.zeros((n_rows, d), F32).at[:bsz].set(c).at[bsz].set(c_ctx)
    final_gain = final_norm_g.reshape(1, d)

    xs = jnp.concatenate([ctx, x], axis=1)
    for l in range(depth):
        need_ctx = l < depth - 1
        mod = _modulation(cc, mod_w[l], mod_b[l]).reshape(n_rows, N_MOD, d)
        pad = jnp.zeros((8 - N_MOD, d), F32)
        mod_c = jnp.concatenate([mod[bsz], pad], axis=0)
        mod_l = jnp.concatenate([mod[:bsz], jnp.broadcast_to(pad, (bsz, 8 - N_MOD, d))], axis=1)

        w_in_l = _arrange_w_in(w_in[l])
        wq, wk, wv, place = _arrange_mla(mla_w_uq[l], mla_w_ukv[l])
        gq = mla_q_norm_g[l].reshape(1, -1)
        gkv = mla_kv_norm_g[l].reshape(1, -1)
        g1n = norm1_g[l].reshape(1, d)
        g2n = norm2_g[l].reshape(1, d)
        wb = w_branch[l].astype(BF16)
        wo = w_out[l].astype(BF16)
        w1 = mlp_w1[l].astype(BF16)
        w2 = mlp_w2[l].astype(BF16)

        p = _inproj(xs, mod_c, mod_l, g1n, w_in_l, n_ctx)

        gq_, gk_, gv_, gbeta, gg = _gdn_prep(p, gdn_conv_w[l], gdn_a_log[l], gdn_dt_bias[l], n_ctx)
        o_f, o_b = _gdn_scan(gq_, gk_, gv_, gbeta, gg, n_ctx)
        h_f, h_b = _lru_scan(p, lru_conv_w[l], lru_conv_b[l], lru_w_r[l], lru_b_r[l], lru_w_i[l], lru_b_i[l],
                             lru_lambda[l], n_ctx)
        ua, ub = _post(o_f, o_b, h_f, h_b, p, gdn_norm_g[l])

        mq, mk, mv = _mla_prep(p, cos, sin, gq, gkv, wq, wk, wv, place)
        uc = _mla_flash(mq, mk, mv, n_ctx, need_ctx)
        ud = _na_attention(p, na_meta, _na_bias_table(na_rpb[l], onehot_r, onehot_c), n_ctx, need_ctx)

        xs = _merge(xs, mod_c, mod_l, (ua, ub, uc, ud), p, b_gate[l], wb, wo, n_ctx, need_ctx)
        xs = _mlp(xs, mod_c, mod_l, g2n, w1, w2, final_gain, n_ctx, l == depth - 1)
    return xs[:, n_ctx:]
```

```python
import functools

import jax
import jax.numpy as jnp
import numpy as np
from jax import lax
from jax.experimental import pallas as pl
from jax.experimental.pallas import tpu as pltpu

F32 = jnp.float32
BF16 = jnp.bfloat16

GRID_W = 64
N_MOD = 6
RMS_EPS = 1e-6
GDN_HEADS = 4
GDN_DK = 64
GDN_DV = 64
GDN_CHUNK = 64
GDN_W = GDN_HEADS * GDN_DV
GDN_CONV = 4
LRU_W = 256
LRU_BLOCKS = 4
LRU_BLOCK_W = LRU_W // LRU_BLOCKS
LRU_CONV = 4
LRU_C = 8.0
MLA_HEADS = 4
MLA_Q_RANK = 256
MLA_KV_RANK = 128
MLA_NOPE = 64
MLA_ROPE = 32
MLA_V = 64
MLA_SLOT = 128
ROPE_BASE = 10000.0
LOG2_E = 1.4426950408889634
NA_HEADS = 4
NA_DH = 64
NA_W = NA_HEADS * NA_DH
NA_WIN_ROWS = 8
NA_WIN_COLS = 16
N_BRANCH = 4
BRANCH_W = 256

SEQ_TILE = 256
HALO = 8

_REF_COLS = {}
_off = 0
for _name, _w in (('gdn_qkv', 3 * GDN_W), ('gdn_z', GDN_W), ('gdn_beta', 2 * GDN_HEADS), ('gdn_a', 2 * GDN_HEADS),
                  ('lru_x', LRU_W), ('lru_y', LRU_W), ('mla_q', MLA_Q_RANK), ('mla_kv', MLA_KV_RANK),
                  ('mla_kr', MLA_ROPE), ('na_qkv', 3 * NA_W)):
    _REF_COLS[_name] = (_off, _w)
    _off += _w
N_MIX_COLS = _off

P_GDN_QKV = 0
P_GDN_Z = 768
P_LRU_X = 1024
P_LRU_Y = 1280
P_MLA_Q = 1536
P_NA_QKV = 1792
P_MLA_KV = 2560
P_MLA_KR = 2688
P_GDN_BA = 2816
P_MIX_END = 3072
P_GATES = 3072
P_COLS = P_GATES + N_BRANCH * 1024

VMEM_LIMIT = 52 * 1024 * 1024


def _cparams(sem):
    return pltpu.CompilerParams(dimension_semantics=sem, vmem_limit_bytes=VMEM_LIMIT)


def _pick_tile(n, cap):
    best = 8
    for t in range(8, min(n, cap) + 1, 8):
        if n % t == 0:
            best = t
    return best


def _full(a):
    return pl.BlockSpec(a.shape, lambda *_: (0,) * a.ndim)


def _split3(x):
    hi = x.astype(BF16)
    r = x - hi.astype(F32)
    mid = r.astype(BF16)
    lo = (r - mid.astype(F32)).astype(BF16)
    return hi, mid, lo


def _sigmoid(x):
    return 1.0 / (1.0 + jnp.exp(-x))


def _arrange_w_in(w_in):
    d = w_in.shape[0]
    out = jnp.zeros((d, P_COLS), F32)
    for name, dst in (('gdn_qkv', P_GDN_QKV), ('gdn_z', P_GDN_Z), ('lru_x', P_LRU_X), ('lru_y', P_LRU_Y),
                      ('mla_q', P_MLA_Q), ('na_qkv', P_NA_QKV), ('mla_kv', P_MLA_KV), ('mla_kr', P_MLA_KR),
                      ('gdn_beta', P_GDN_BA), ('gdn_a', P_GDN_BA + 2 * GDN_HEADS)):
        o, w = _REF_COLS[name]
        out = out.at[:, dst:dst + w].set(w_in[:, o:o + w])
    out = out.at[:, P_GATES:].set(w_in[:, N_MIX_COLS:])
    return out.astype(BF16)


def _rope_perm():
    q = MLA_ROPE // 4
    src = np.zeros(MLA_ROPE, np.int32)
    sign = np.zeros(MLA_ROPE, np.float32)
    for base in (0, 2 * q):
        for d in range(q):
            src[base + d] = base + d + q
            sign[base + d] = -1.0
            src[base + q + d] = base + d
            sign[base + q + d] = 1.0
    return src, sign


def _arrange_mla(w_uq, w_ukv):
    src, sign = _rope_perm()
    hq = MLA_NOPE + MLA_ROPE
    wq = jnp.zeros((MLA_Q_RANK, 2 * MLA_HEADS * MLA_SLOT), F32)
    wk = jnp.zeros((MLA_KV_RANK, MLA_HEADS * MLA_SLOT), F32)
    wv = jnp.zeros((MLA_KV_RANK, MLA_HEADS * MLA_V), F32)
    place = np.zeros((2 * MLA_SLOT, 2 * MLA_HEADS * MLA_SLOT), np.float32)
    rot_off = MLA_HEADS * MLA_SLOT
    for h in range(MLA_HEADS):
        nope = w_uq[:, h * hq:h * hq + MLA_NOPE]
        pe = w_uq[:, h * hq + MLA_NOPE:(h + 1) * hq]
        s = h * MLA_SLOT
        wq = wq.at[:, s:s + MLA_NOPE].set(nope)
        wq = wq.at[:, s + MLA_NOPE:s + MLA_NOPE + MLA_ROPE].set(pe)
        wq = wq.at[:, rot_off + s + MLA_NOPE:rot_off + s + MLA_NOPE + MLA_ROPE].set(pe[:, src] * sign)
        wk = wk.at[:, s:s + MLA_NOPE].set(w_ukv[:, h * 128:h * 128 + MLA_NOPE])
        wv = wv.at[:, h * MLA_V:(h + 1) * MLA_V].set(w_ukv[:, h * 128 + MLA_NOPE:(h + 1) * 128])
        for d in range(MLA_ROPE):
            for half in (0, MLA_SLOT):
                place[half + d, s + MLA_NOPE + d] = 1.0
                place[half + src[d], rot_off + s + MLA_NOPE + d] = sign[d]
    return wq.astype(BF16), wk.astype(BF16), wv.astype(BF16), jnp.asarray(place, BF16)


def _rope_tables(n_tok, n_ctx):
    cos = np.ones((n_tok + n_ctx, MLA_SLOT), np.float32)
    sin = np.zeros((n_tok + n_ctx, MLA_SLOT), np.float32)
    t = np.arange(n_tok)
    row = (t // GRID_W).astype(np.float32)
    col = (t % GRID_W).astype(np.float32)
    n_freq = MLA_ROPE // 4
    inv = (ROPE_BASE ** (-np.arange(n_freq, dtype=np.float32) / n_freq)).astype(np.float32)
    ar = row[:, None] * inv
    ac = col[:, None] * inv
    ang = np.concatenate([ar, ar, ac, ac], axis=-1).astype(np.float32)
    cos[:n_tok, MLA_NOPE:MLA_NOPE + MLA_ROPE] = np.cos(ang)
    sin[:n_tok, MLA_NOPE:MLA_NOPE + MLA_ROPE] = np.sin(ang)
    return jnp.asarray(cos), jnp.asarray(sin)


def _head_block_ones(n_heads, width):
    m = np.kron(np.eye(n_heads, dtype=np.float32), np.ones((width, width), np.float32))
    return jnp.asarray(m, BF16)


def _mod_kernel(c_ref, w_ref, b_ref, o_ref):
    c = c_ref[...]
    s = c * _sigmoid(c)
    o_ref[...] = jnp.dot(s, w_ref[...], preferred_element_type=F32) + b_ref[...]


def _modulation(cc, mod_w, mod_b):
    r, d = cc.shape
    n = mod_w.shape[1]
    tn = 1024
    return pl.pallas_call(
        _mod_kernel,
        grid=(n // tn,),
        in_specs=[pl.BlockSpec((r, d), lambda j: (0, 0)),
                  pl.BlockSpec((d, tn), lambda j: (0, j)),
                  pl.BlockSpec((1, tn), lambda j: (0, j))],
        out_specs=pl.BlockSpec((r, tn), lambda j: (0, j)),
        out_shape=jax.ShapeDtypeStruct((r, n), F32),
        compiler_params=_cparams(("arbitrary",)),
        name="modulation",
    )(cc, mod_w, mod_b.reshape(1, n))


def _row_select(tile_rows, tile_idx, n_lat, ctx_vec, lat_vec):
    row = tile_idx * tile_rows + lax.broadcasted_iota(jnp.int32, (tile_rows, 1), 0)
    return jnp.where(row < n_lat, lat_vec, ctx_vec)


def _inproj_kernel(x_ref, mc_ref, ml_ref, g_ref, w_ref, o_ref, h_ref, *, n_lat):
    @pl.when(pl.program_id(2) == 0)
    def _():
        tm = x_ref.shape[0]
        i = pl.program_id(1)
        x = x_ref[...]
        y = x * lax.rsqrt(jnp.mean(x * x, axis=-1, keepdims=True) + RMS_EPS)
        y = y * g_ref[...]
        shift = _row_select(tm, i, n_lat, mc_ref[0:1, :], ml_ref[0:1, :])
        scale = _row_select(tm, i, n_lat, mc_ref[1:2, :], ml_ref[1:2, :])
        h_ref[...] = (y * (1.0 + scale) + shift).astype(BF16)

    o_ref[...] = jnp.dot(h_ref[...], w_ref[...], preferred_element_type=F32)


def _inproj(xs, mod_c, mod_l, gain, w, n_lat):
    b, s, d = xs.shape
    tm = _pick_tile(s, 1088)
    tn = 1024
    n_cols = w.shape[1]
    return pl.pallas_call(
        functools.partial(_inproj_kernel, n_lat=n_lat),
        grid=(b, s // tm, n_cols // tn),
        in_specs=[pl.BlockSpec((None, tm, d), lambda bi, i, j: (bi, i, 0)),
                  pl.BlockSpec((8, d), lambda bi, i, j: (0, 0)),
                  pl.BlockSpec((None, 8, d), lambda bi, i, j: (bi, 0, 0)),
                  pl.BlockSpec((1, d), lambda bi, i, j: (0, 0)),
                  pl.BlockSpec((d, tn), lambda bi, i, j: (0, j))],
        out_specs=pl.BlockSpec((None, tm, tn), lambda bi, i, j: (bi, i, j)),
        out_shape=jax.ShapeDtypeStruct((b, s, n_cols), F32),
        scratch_shapes=[pltpu.VMEM((tm, d), BF16)],
        compiler_params=_cparams(("parallel", "parallel", "arbitrary")),
        name="inproj",
    )(xs, mod_c, mod_l, gain, w)


def _mla_prep_kernel(ql_ref, kv_ref, cos_ref, sin_ref, gq_ref, gkv_ref, wq_ref, wk_ref, wv_ref, pl_ref,
                     q_ref, k_ref, v_ref, *, scale):
    nslot = MLA_HEADS * MLA_SLOT
    cos = jnp.concatenate([cos_ref[...]] * MLA_HEADS, axis=-1)
    sin = jnp.concatenate([sin_ref[...]] * MLA_HEADS, axis=-1)

    ql = ql_ref[...]
    qn = ql * lax.rsqrt(jnp.mean(ql * ql, axis=-1, keepdims=True) + RMS_EPS) * gq_ref[...]
    q2 = jnp.dot(qn.astype(BF16), wq_ref[...], preferred_element_type=F32)
    q = (q2[:, :nslot] * cos + q2[:, nslot:] * sin) * scale
    q_ref[...] = q.astype(BF16)

    kvkr = kv_ref[...]
    kvl = kvkr[:, :MLA_KV_RANK]
    kr = kvkr[:, MLA_KV_RANK:]
    kvn = (kvl * lax.rsqrt(jnp.mean(kvl * kvl, axis=-1, keepdims=True) + RMS_EPS) * gkv_ref[...]).astype(BF16)
    kn = jnp.dot(kvn, wk_ref[...], preferred_element_type=F32)
    v_ref[...] = jnp.dot(kvn, wv_ref[...], preferred_element_type=F32).astype(BF16)
    kr_hi = kr.astype(BF16)
    kr_lo = (kr - kr_hi.astype(F32)).astype(BF16)
    kr2 = jnp.dot(jnp.concatenate([kr_hi, kr_lo], axis=-1), pl_ref[...], preferred_element_type=F32)
    k_ref[...] = (kn + kr2[:, :nslot] * cos + kr2[:, nslot:] * sin).astype(BF16)


def _mla_prep(p, cos, sin, gq, gkv, wq, wk, wv, place):
    b, s, _ = p.shape
    tm = _pick_tile(s, 544)
    nslot = MLA_HEADS * MLA_SLOT
    scale = (MLA_NOPE + MLA_ROPE) ** -0.5 * LOG2_E
    return pl.pallas_call(
        functools.partial(_mla_prep_kernel, scale=scale),
        grid=(b, s // tm),
        in_specs=[pl.BlockSpec((None, tm, MLA_Q_RANK), lambda bi, i: (bi, i, P_MLA_Q // MLA_Q_RANK)),
                  pl.BlockSpec((None, tm, 2 * MLA_SLOT), lambda bi, i: (bi, i, P_MLA_KV // (2 * MLA_SLOT))),
                  pl.BlockSpec((tm, MLA_SLOT), lambda bi, i: (i, 0)),
                  pl.BlockSpec((tm, MLA_SLOT), lambda bi, i: (i, 0)),
                  _full(gq), _full(gkv), _full(wq), _full(wk), _full(wv), _full(place)],
        out_specs=[pl.BlockSpec((None, tm, nslot), lambda bi, i: (bi, i, 0)),
                   pl.BlockSpec((None, tm, nslot), lambda bi, i: (bi, i, 0)),
                   pl.BlockSpec((None, tm, MLA_HEADS * MLA_V), lambda bi, i: (bi, i, 0))],
        out_shape=[jax.ShapeDtypeStruct((b, s, nslot), BF16),
                   jax.ShapeDtypeStruct((b, s, nslot), BF16),
                   jax.ShapeDtypeStruct((b, s, MLA_HEADS * MLA_V), BF16)],
        compiler_params=_cparams(("parallel", "parallel")),
        name="mla_prep",
    )(p, p, cos, sin, gq, gkv, wq, wk, wv, place)


def _flash_update(h, q, k, v, m_ref, l_ref, acc_ref):
    s = lax.dot_general(q, k, (((1,), (1,)), ((), ())), preferred_element_type=F32)
    m_prev = m_ref[h]
    m_new = jnp.maximum(m_prev, jnp.max(s, axis=-1, keepdims=True))
    alpha = jnp.exp2(m_prev - m_new)
    p = jnp.exp2(s - jnp.concatenate([m_new] * (s.shape[1] // 128), axis=-1))
    l_ref[h] = alpha * l_ref[h] + jnp.sum(p, axis=-1, keepdims=True)
    acc_ref[h] = alpha * acc_ref[h] + jnp.dot(p.astype(BF16), v, preferred_element_type=F32)
    m_ref[h] = m_new


def _mla_flash_kernel(*refs, tk, aliased):
    if aliased:
        q_ref, k_ref, v_ref, _, o_ref, m_ref, l_ref, acc_ref = refs
    else:
        q_ref, k_ref, v_ref, o_ref, m_ref, l_ref, acc_ref = refs
    tq = q_ref.shape[0]
    nk = k_ref.shape[0]
    n_loop = nk // tk
    m_ref[...] = jnp.full(m_ref.shape, -jnp.inf, F32)
    l_ref[...] = jnp.zeros(l_ref.shape, F32)
    acc_ref[...] = jnp.zeros(acc_ref.shape, F32)

    def chunk(rows):
        for h in range(MLA_HEADS):
            hs = slice(h * MLA_SLOT, (h + 1) * MLA_SLOT)
            vs = slice((h // 2) * 128, (h // 2) * 128 + 128)
            _flash_update(h, q_ref[:, hs], k_ref[rows, hs], v_ref[rows, vs], m_ref, l_ref, acc_ref)

    if n_loop:
        def body(j, carry):
            chunk(pl.ds(pl.multiple_of(j * tk, tk), tk))
            return carry
        lax.fori_loop(0, n_loop, body, 0)
    if nk > n_loop * tk:
        chunk(slice(n_loop * tk, nk))

    lane = lax.broadcasted_iota(jnp.int32, (tq, 128), 1)
    outs = []
    for pair in range(MLA_HEADS // 2):
        o0 = acc_ref[2 * pair] / l_ref[2 * pair]
        o1 = acc_ref[2 * pair + 1] / l_ref[2 * pair + 1]
        outs.append(jnp.where(lane < MLA_V, o0, o1))
    o_ref[...] = jnp.concatenate(outs, axis=-1).astype(o_ref.dtype)


def _mla_flash(q, k, v, n_lat, prev=None):
    b, s, nslot = q.shape
    n_ctx = s - n_lat
    nv = MLA_HEADS * MLA_V
    tk = 512
    if prev is None:
        tq = _pick_tile(n_lat, 512)
        q_off, n_q, kv_rows, kv_blk = 0, n_lat // tq, s, 0
    else:
        assert n_lat % n_ctx == 0
        tq = _pick_tile(n_ctx, 256)
        q_off, n_q, kv_rows, kv_blk = n_lat // tq, n_ctx // tq, n_ctx, n_lat // n_ctx
    in_specs = [pl.BlockSpec((None, tq, nslot), lambda bi, i: (bi, i + q_off, 0)),
                pl.BlockSpec((None, kv_rows, nslot), lambda bi, i: (bi, kv_blk, 0)),
                pl.BlockSpec((None, kv_rows, nv), lambda bi, i: (bi, kv_blk, 0))]
    args = [q, k, v]
    aliases = {}
    if prev is not None:
        in_specs.append(pl.BlockSpec(memory_space=pl.ANY))
        args.append(prev)
        aliases = {3: 0}
    return pl.pallas_call(
        functools.partial(_mla_flash_kernel, tk=tk, aliased=prev is not None),
        grid=(b, n_q),
        in_specs=in_specs,
        out_specs=pl.BlockSpec((None, tq, nv), lambda bi, i: (bi, i + q_off, 0)),
        out_shape=jax.ShapeDtypeStruct((b, s, nv), BF16),
        scratch_shapes=[pltpu.VMEM((MLA_HEADS, tq, 128), F32),
                        pltpu.VMEM((MLA_HEADS, tq, 128), F32),
                        pltpu.VMEM((MLA_HEADS, tq, 128), F32)],
        input_output_aliases=aliases,
        compiler_params=_cparams(("parallel", "arbitrary")),
        name="mla_flash",
    )(*args)


NA_QROWS = 4
NA_SLAB = NA_QROWS + NA_WIN_ROWS


def _na_geometry(rows):
    assert rows % NA_QROWS == 0 and rows >= NA_SLAB
    nblk = rows // NA_QROWS
    qc = np.arange(GRID_W)
    cs = np.clip(qc - NA_WIN_COLS // 2, 0, GRID_W - NA_WIN_COLS)
    col_valid = (qc[None, :] >= cs[:, None]) & (qc[None, :] < cs[:, None] + NA_WIN_COLS)
    rel_c = np.clip(qc[None, :] - qc[:, None] + NA_WIN_COLS - 1, 0, 2 * NA_WIN_COLS - 2)
    onehot_c = (rel_c[None] == np.arange(2 * NA_WIN_COLS - 1)[:, None, None]) & col_valid[None]
    patterns, cls, starts = {}, [], []
    for i in range(nblk):
        r0 = i * NA_QROWS
        start = int(np.clip(r0 - NA_WIN_ROWS // 2, 0, rows - NA_SLAB))
        qr = r0 + np.arange(NA_QROWS)
        rs = np.clip(qr - NA_WIN_ROWS // 2, 0, rows - NA_WIN_ROWS)
        key = (start - r0,) + tuple((rs - r0).tolist())
        if key not in patterns:
            kr = start + np.arange(NA_SLAB)
            row_valid = (kr[None, :] >= rs[:, None]) & (kr[None, :] < rs[:, None] + NA_WIN_ROWS)
            rel_r = np.clip(kr[None, :] - qr[:, None] + NA_WIN_ROWS - 1, 0, 2 * NA_WIN_ROWS - 2)
            onehot_r = (rel_r[..., None] == np.arange(2 * NA_WIN_ROWS - 1)) & row_valid[..., None]
            patterns[key] = (len(patterns), onehot_r)
        cls.append(patterns[key][0])
        starts.append(start)
    ordered = sorted(patterns.values(), key=lambda z: z[0])
    onehot_r = np.stack([z[1] for z in ordered]).astype(np.float32)
    meta = np.stack([np.asarray(cls, np.int32), np.asarray(starts, np.int32)])
    return meta, onehot_r, onehot_c.astype(np.float32)


def _na_bias_table(rpb, onehot_r, onehot_c):
    hi = lax.Precision.HIGHEST
    bc = jnp.einsum('hrc,cqk->hrqk', rpb.astype(F32), onehot_c, precision=hi)
    tab = jnp.einsum('pabr,hrqk->phaqbk', onehot_r, bc, precision=hi)
    valid = jnp.einsum('pabr,cqk->paqbk', onehot_r, onehot_c, precision=hi) > 0.5
    tab = jnp.where(valid[:, None], tab, -jnp.inf)
    p, h = tab.shape[:2]
    return tab.reshape(p, h, NA_QROWS * GRID_W, NA_SLAB * GRID_W)


def _na_attend(q_pair, lane, h, k_parts, v_parts, bias):
    in_head = (lane < NA_DH) if h % 2 == 0 else (lane >= NA_DH)
    qm = jnp.where(in_head, q_pair * (NA_DH ** -0.5), 0.0).astype(BF16)
    scores = [lax.dot_general(qm, k, (((1,), (1,)), ((), ())), preferred_element_type=F32) for k in k_parts]
    if bias is not None:
        scores[0] = scores[0] + bias
    m = scores[0].max(axis=-1, keepdims=True)
    for s in scores[1:]:
        m = jnp.maximum(m, s.max(axis=-1, keepdims=True))
    den = None
    out = None
    for s, v in zip(scores, v_parts):
        p = jnp.exp(s - m)
        ps = p.sum(axis=-1, keepdims=True)
        po = jnp.dot(p.astype(BF16), v, preferred_element_type=F32)
        den = ps if den is None else den + ps
        out = po if out is None else out + po
    return out / den


def _na_kernel(meta_ref, q_ref, k_ref, v_ref, bias_ref, o_ref, *, n_lat, with_ctx):
    i = pl.program_id(1)
    nq = q_ref.shape[0]
    n_lat_tiles = n_lat // nq
    n_all = k_ref.shape[0]
    lane = lax.broadcasted_iota(jnp.int32, (nq, 128), 1)

    def run(windowed):
        outs = []
        for pair in range(NA_HEADS // 2):
            ls = slice(pair * 128, (pair + 1) * 128)
            q_pair = q_ref[:, ls]
            k_ctx = k_ref[n_lat:n_all, ls].astype(BF16)
            v_ctx = v_ref[n_lat:n_all, ls].astype(BF16)
            if windowed:
                start = pl.multiple_of(meta_ref[1, jnp.minimum(i, n_lat_tiles - 1)] * GRID_W, GRID_W)
                rows = pl.ds(start, NA_SLAB * GRID_W)
                ks = (k_ref[rows, ls].astype(BF16), k_ctx)
                vs = (v_ref[rows, ls].astype(BF16), v_ctx)
            else:
                ks, vs = (k_ctx,), (v_ctx,)
            o = [_na_attend(q_pair, lane, 2 * pair + e, ks, vs, bias_ref[2 * pair + e] if windowed else None)
                 for e in range(2)]
            outs.append(jnp.where(lane < NA_DH, o[0], o[1]))
        o_ref[...] = jnp.concatenate(outs, axis=-1).astype(o_ref.dtype)

    if with_ctx:
        pl.when(i < n_lat_tiles)(lambda: run(True))
        pl.when(i >= n_lat_tiles)(lambda: run(False))
    else:
        run(True)


def _na_attention(p, meta, table, n_lat, with_ctx_queries):
    b, s, _ = p.shape
    nq = NA_QROWS * GRID_W
    nk = NA_SLAB * GRID_W
    assert (s - n_lat) % nq == 0
    nlt = n_lat // nq
    qb = P_NA_QKV // NA_W
    grid_spec = pltpu.PrefetchScalarGridSpec(
        num_scalar_prefetch=1,
        grid=(b, s // nq if with_ctx_queries else nlt),
        in_specs=[pl.BlockSpec((None, nq, NA_W), lambda bi, i, m: (bi, i, qb)),
                  pl.BlockSpec((None, s, NA_W), lambda bi, i, m: (bi, 0, qb + 1)),
                  pl.BlockSpec((None, s, NA_W), lambda bi, i, m: (bi, 0, qb + 2)),
                  pl.BlockSpec((None, NA_HEADS, nq, nk),
                               lambda bi, i, m: (m[0, jnp.minimum(i, nlt - 1)], 0, 0, 0))],
        out_specs=pl.BlockSpec((None, nq, NA_W), lambda bi, i, m: (bi, i, 0)))
    return pl.pallas_call(
        functools.partial(_na_kernel, n_lat=n_lat, with_ctx=with_ctx_queries),
        grid_spec=grid_spec,
        out_shape=jax.ShapeDtypeStruct((b, s, NA_W), BF16),
        compiler_params=_cparams(("parallel", "arbitrary")),
        name="na_attention",
    )(meta, p, p, p, table)


def _tile_conv(x, prev, nxt, w_ref, tile_idx, n_tiles, n_lat_tiles):
    r = x.shape[0]
    width = w_ref.shape[0]
    left = width // 2
    has_prev = jnp.logical_and(tile_idx != 0, tile_idx != n_lat_tiles)
    has_next = jnp.logical_and(tile_idx != n_tiles - 1, tile_idx != n_lat_tiles - 1)
    prev = jnp.where(has_prev, prev, 0.0)
    nxt = jnp.where(has_next, nxt, 0.0)
    xe = jnp.concatenate([prev, x, nxt], axis=0)
    acc = None
    for j in range(width):
        o = HALO - left + j
        term = xe[o:o + r, :] * w_ref[j:j + 1, :]
        acc = term if acc is None else acc + term
    return acc


def _halo_specs(width, col_block, tile_of):
    per = SEQ_TILE // HALO

    def main(bi, s, *_):
        return (bi, tile_of(s), col_block)

    def prev(bi, s, *_):
        return (bi, jnp.maximum(tile_of(s) * per - 1, 0), col_block)

    def make_next(n_tiles):
        def nxt(bi, s, *_):
            return (bi, jnp.minimum((tile_of(s) + 1) * per, n_tiles * per - 1), col_block)
        return nxt

    return main, prev, make_next


def _gdn_prep_kernel(x_ref, xp_ref, xn_ref, ba_ref, cw_ref, ones_ref, exp_ref, alog_ref, dtb_ref,
                     q_ref, k_ref, v_ref, beta_ref, g_ref, *, n_lat_tiles):
    i = pl.program_id(1)
    y = _tile_conv(x_ref[...], xp_ref[...], xn_ref[...], cw_ref, i, pl.num_programs(1), n_lat_tiles)
    y = y * _sigmoid(y)
    q = y[:, :GDN_W]
    k = y[:, GDN_W:2 * GDN_W]
    v_ref[...] = y[:, 2 * GDN_W:]

    def head_norm(u):
        parts = jnp.concatenate(_split3(u * u), axis=-1)
        ss = jnp.dot(parts, ones_ref[...], preferred_element_type=F32)
        return u * lax.rsqrt(ss + RMS_EPS)

    q_ref[...] = head_norm(q) * (GDN_DK ** -0.5)
    k_ref[...] = head_norm(k)

    logits = jnp.dot(jnp.concatenate(_split3(ba_ref[...]), axis=-1), exp_ref[...], preferred_element_type=F32)
    half = 2 * GDN_W
    beta_ref[...] = _sigmoid(logits[:, :half])
    a = logits[:, half:] + dtb_ref[...]
    softplus = jnp.maximum(a, 0.0) + jnp.log1p(jnp.exp(-jnp.abs(a)))
    g_ref[...] = -jnp.exp(alog_ref[...]) * softplus


def _gdn_prep(p, conv_w, a_log, dt_bias, n_lat):
    b, s, _ = p.shape
    n_tiles = s // SEQ_TILE
    main, prev, make_next = _halo_specs(3 * GDN_W, 0, lambda t: t)
    ones3 = jnp.concatenate([_head_block_ones(GDN_HEADS, GDN_DK)] * 3, axis=0)
    expand = np.zeros((128, 4 * GDN_W), np.float32)
    for kind in range(2):
        for d in range(2):
            for h in range(GDN_HEADS):
                c0 = kind * 2 * GDN_W + d * GDN_W + h * GDN_DV
                expand[kind * 2 * GDN_HEADS + d * GDN_HEADS + h, c0:c0 + GDN_DV] = 1.0
    expand3 = jnp.asarray(np.concatenate([expand] * 3, axis=0), BF16)
    alog_e = jnp.repeat(a_log.astype(F32).reshape(-1), GDN_DV).reshape(1, 2 * GDN_W)
    dtb_e = jnp.repeat(dt_bias.astype(F32).reshape(-1), GDN_DV).reshape(1, 2 * GDN_W)
    tok = lambda w: pl.BlockSpec((None, SEQ_TILE, w), lambda bi, i: (bi, i, 0))
    return pl.pallas_call(
        functools.partial(_gdn_prep_kernel, n_lat_tiles=n_lat // SEQ_TILE),
        grid=(b, n_tiles),
        in_specs=[pl.BlockSpec((None, SEQ_TILE, 3 * GDN_W), main),
                  pl.BlockSpec((None, HALO, 3 * GDN_W), prev),
                  pl.BlockSpec((None, HALO, 3 * GDN_W), make_next(n_tiles)),
                  pl.BlockSpec((None, SEQ_TILE, 128), lambda bi, i: (bi, i, P_GDN_BA // 128)),
                  _full(conv_w), _full(ones3), _full(expand3), _full(alog_e), _full(dtb_e)],
        out_specs=[tok(GDN_W), tok(GDN_W), tok(GDN_W), tok(2 * GDN_W), tok(2 * GDN_W)],
        out_shape=[jax.ShapeDtypeStruct((b, s, GDN_W), F32)] * 3 + [jax.ShapeDtypeStruct((b, s, 2 * GDN_W), F32)] * 2,
        compiler_params=_cparams(("parallel", "parallel")),
        name="gdn_prep",
    )(p, p, p, p, conv_w, ones3, expand3, alog_e, dtb_e)


def _gdn_masks():
    c, w = GDN_CHUNK, GDN_W
    r2, c2 = np.arange(w)[:, None], np.arange(w)[None, :]
    bd = ((r2 // c) == (c2 // c)).astype(np.float32)
    i = np.arange(c)[:, None]
    j = (np.arange(w) % c)[None, :]
    level = np.zeros((c, w), np.int32)
    for bit in range(6):
        level += ((i ^ j) >= (1 << bit)).astype(np.int32)
    lvl = np.stack([(level == m).astype(np.float32) for m in range(7)])
    dirm = np.stack([np.stack([(j <= i), (j < i)]), np.stack([(j >= i), (j > i)])]).astype(np.float32)
    tj = (np.arange(3 * c) % c)[None, :]
    tri = np.stack([(tj <= i), (tj >= i)]).astype(np.float32)
    return jnp.asarray(bd, BF16), jnp.asarray(lvl), jnp.asarray(dirm), jnp.asarray(tri, BF16)


def _blockdiag(y, bd):
    yb = y.astype(BF16)
    return jnp.concatenate([yb] * GDN_HEADS, axis=0) * bd


def _mm(a, b):
    return jnp.dot(a.astype(BF16), b, preferred_element_type=F32)


def _gdn_intra(probs, bd, lvl_ref, dirm_ref, tri_ref):
    c = GDN_CHUNK
    n = len(probs)
    eye = lvl_ref[0]
    gc, g_last, decay, gram = [], [], [], []
    for q, k, v, beta, g, rev in probs:
        d = 1 if rev else 0
        gcp = jnp.dot(tri_ref[d], jnp.concatenate(_split3(g), axis=0), preferred_element_type=F32)
        gc.append(gcp)
        g_last.append(gcp[0:1, :] if rev else gcp[c - 1:c, :])
        gc_row = jnp.sum(gcp * eye, axis=0, keepdims=True)
        decay.append(dirm_ref[d, 0] * jnp.exp(jnp.minimum(gcp - gc_row, 0.0)))
        gram.append(lax.dot_general(jnp.concatenate([k, q], axis=0).astype(BF16), _blockdiag(k, bd),
                                    (((1,), (1,)), ((), ())), preferred_element_type=F32))
    lower = [dirm_ref[1 if p[5] else 0, 1] * p[3] * gram[x][:c] * decay[x] for x, p in enumerate(probs)]
    a_intra = [gram[x][c:] * decay[x] for x in range(n)]
    t = [eye - lower[x] * lvl_ref[1] for x in range(n)]
    for lev in range(2, 7):
        y = [_mm(t[x], _blockdiag(lower[x] * lvl_ref[lev], bd)) for x in range(n)]
        z = [_mm(y[x], _blockdiag(t[x], bd)) for x in range(n)]
        t = [t[x] - z[x] for x in range(n)]
    e_gc = [jnp.exp(gc[x]) for x in range(n)]
    u = [_mm(t[x], _blockdiag(p[2] * p[3], bd)) for x, p in enumerate(probs)]
    w = [_mm(t[x], _blockdiag(p[1] * p[3] * e_gc[x], bd)) for x, p in enumerate(probs)]
    wq = [jnp.concatenate([w[x], p[0] * e_gc[x]], axis=0).astype(BF16) for x, p in enumerate(probs)]
    k_dec = [(p[1] * jnp.exp(g_last[x] - gc[x])).astype(BF16) for x, p in enumerate(probs)]
    g_tot = [jnp.exp(g_last[x]) for x in range(n)]
    return u, wq, k_dec, a_intra, g_tot


def _gdn_state_step(u, wq, k_dec, a_intra, g_tot, s_ref, bd):
    c = GDN_CHUNK
    s_bd = s_ref[...]
    ws_qs = jnp.dot(wq, s_bd.astype(BF16), preferred_element_type=F32)
    v_new = u - ws_qs[:c]
    o = ws_qs[c:] + _mm(a_intra, _blockdiag(v_new, bd))
    kv = lax.dot_general(k_dec, v_new.astype(BF16), (((0,), (0,)), ((), ())), preferred_element_type=F32)
    s_ref[...] = s_bd * g_tot + kv * bd.astype(F32)
    return o


def _gdn_scan_kernel(qf, kf, vf, bf, gf, qb, kb, vb, bb, gb, bd_ref, lvl_ref, dirm_ref, tri_ref,
                     of_ref, ob_ref, sf_ref, sb_ref):
    @pl.when(pl.program_id(1) == 0)
    def _():
        sf_ref[...] = jnp.zeros(sf_ref.shape, F32)
        sb_ref[...] = jnp.zeros(sb_ref.shape, F32)

    bd = bd_ref[...]
    n = SEQ_TILE // GDN_CHUNK
    probs, rows = [], []
    for c in range(n):
        rf = slice(c * GDN_CHUNK, (c + 1) * GDN_CHUNK)
        rb = slice((n - 1 - c) * GDN_CHUNK, (n - c) * GDN_CHUNK)
        probs.append((qf[rf, :], kf[rf, :], vf[rf, :], bf[rf, :], gf[rf, :], False))
        probs.append((qb[rb, :], kb[rb, :], vb[rb, :], bb[rb, :], gb[rb, :], True))
        rows += [rf, rb]
    u, wq, k_dec, a_intra, g_tot = _gdn_intra(probs, bd, lvl_ref, dirm_ref, tri_ref)
    for x in range(2 * n):
        rev = probs[x][5]
        o = _gdn_state_step(u[x], wq[x], k_dec[x], a_intra[x], g_tot[x], sb_ref if rev else sf_ref, bd)
        (ob_ref if rev else of_ref)[rows[x], :] = o


def _seq_tile_maps(n_lat_tiles, n_tiles):
    fwd = lambda s: lax.rem(s + n_lat_tiles, n_tiles)
    bwd = lambda s: n_tiles - 1 - s
    return fwd, bwd


def _gdn_scan(q, k, v, beta, g, n_lat):
    b, s, _ = q.shape
    n_tiles = s // SEQ_TILE
    fwd, bwd = _seq_tile_maps(n_lat // SEQ_TILE, n_tiles)
    spec = lambda tile_of, col: pl.BlockSpec((None, SEQ_TILE, GDN_W), lambda bi, t: (bi, tile_of(t), col))
    masks = _gdn_masks()
    in_specs = [spec(fwd, 0)] * 5 + [spec(bwd, 0)] * 3 + [spec(bwd, 1)] * 2 + [_full(m) for m in masks]
    return pl.pallas_call(
        _gdn_scan_kernel,
        grid=(b, n_tiles),
        in_specs=in_specs,
        out_specs=[spec(fwd, 0), spec(bwd, 0)],
        out_shape=[jax.ShapeDtypeStruct((b, s, GDN_W), F32)] * 2,
        scratch_shapes=[pltpu.VMEM((GDN_W, GDN_W), F32), pltpu.VMEM((GDN_W, GDN_W), F32)],
        compiler_params=_cparams(("parallel", "arbitrary")),
        name="gdn_scan",
    )(q, k, v, beta, g, q, k, v, beta, g, *masks)


def _lru_tile_prep(x_ref, xp_ref, xn_ref, cw_ref, cb_ref, wg_ref, bg_ref, nla_ref, a_ref, b_ref,
                   tile_idx, n_tiles, n_lat_tiles):
    xb = _tile_conv(x_ref[...], xp_ref[...], xn_ref[...], cw_ref, tile_idx, n_tiles, n_lat_tiles) + cb_ref[...]
    gates = _sigmoid(jnp.dot(xb.astype(BF16), wg_ref[...], preferred_element_type=F32) + bg_ref[...])
    log_a = nla_ref[...] * gates[:, :LRU_W]
    a_ref[...] = jnp.exp(log_a)
    th = jnp.tanh(log_a)
    b_ref[...] = jnp.sqrt(-2.0 * th / (1.0 - th)) * gates[:, LRU_W:] * xb


def _scan_group(a, b, h, row, reverse):
    for d in (1, 2, 4):
        if reverse:
            keep = row < 8 - d
            shift = 8 - d
        else:
            keep = row >= d
            shift = d
        a_s = jnp.where(keep, pltpu.roll(a, shift, 0), 1.0)
        b_s = jnp.where(keep, pltpu.roll(b, shift, 0), 0.0)
        b = a * b_s + b
        a = a * a_s
    return a * h + b


def _lru_scan_kernel(xf, xfp, xfn, xb, xbp, xbn, cw_ref, cb_ref, wgf, bgf, nlaf, wgb, bgb, nlab,
                     hf_ref, hb_ref, af_ref, bf_ref, ab_ref, bb_ref, cf_ref, cbk_ref, *, n_lat_tiles):
    s = pl.program_id(1)
    n_tiles = pl.num_programs(1)

    @pl.when(s == 0)
    def _():
        cf_ref[...] = jnp.zeros(cf_ref.shape, F32)
        cbk_ref[...] = jnp.zeros(cbk_ref.shape, F32)

    t_f, t_b = (m(s) for m in _seq_tile_maps(n_lat_tiles, n_tiles))
    _lru_tile_prep(xf, xfp, xfn, cw_ref, cb_ref, wgf, bgf, nlaf, af_ref, bf_ref, t_f, n_tiles, n_lat_tiles)
    _lru_tile_prep(xb, xbp, xbn, cw_ref, cb_ref, wgb, bgb, nlab, ab_ref, bb_ref, t_b, n_tiles, n_lat_tiles)

    n_groups = SEQ_TILE // 8
    row = lax.broadcasted_iota(jnp.int32, (8, LRU_W), 0)

    def body(gi, carry):
        h_f, h_b = carry
        rf = pl.ds(pl.multiple_of(gi * 8, 8), 8)
        rb = pl.ds(pl.multiple_of((n_groups - 1 - gi) * 8, 8), 8)
        out_f = _scan_group(af_ref[rf, :], bf_ref[rf, :], h_f, row, False)
        out_b = _scan_group(ab_ref[rb, :], bb_ref[rb, :], h_b, row, True)
        hf_ref[rf, :] = out_f
        hb_ref[rb, :] = out_b
        return (jnp.broadcast_to(out_f[7:8, :], (8, LRU_W)), jnp.broadcast_to(out_b[0:1, :], (8, LRU_W)))

    h_f, h_b = lax.fori_loop(0, n_groups, body, (cf_ref[...], cbk_ref[...]))
    cf_ref[...] = h_f
    cbk_ref[...] = h_b


def _lru_scan(p, conv_w, conv_b, w_r, b_r, w_i, b_i, lam, n_lat):
    b, s, _ = p.shape
    n_tiles = s // SEQ_TILE
    nlt = n_lat // SEQ_TILE
    fwd, bwd = _seq_tile_maps(nlt, n_tiles)
    col = P_LRU_X // LRU_W
    specs = []
    for tile_of in (fwd, bwd):
        main, prev, make_next = _halo_specs(LRU_W, col, tile_of)
        specs += [pl.BlockSpec((None, SEQ_TILE, LRU_W), main), pl.BlockSpec((None, HALO, LRU_W), prev),
                  pl.BlockSpec((None, HALO, LRU_W), make_next(n_tiles))]

    def blockdiag(w):
        return jax.scipy.linalg.block_diag(*[w[n] for n in range(LRU_BLOCKS)])

    dir_args = []
    for d in range(2):
        wg = jnp.concatenate([blockdiag(w_r[d]), blockdiag(w_i[d])], axis=1).astype(BF16)
        bg = jnp.concatenate([b_r[d], b_i[d]]).astype(F32).reshape(1, 2 * LRU_W)
        nla = (-LRU_C * jax.nn.softplus(-lam[d].astype(F32))).reshape(1, LRU_W)
        dir_args += [wg, bg, nla]
    cb2 = conv_b.reshape(1, LRU_W)
    out_spec = lambda tile_of: pl.BlockSpec((None, SEQ_TILE, LRU_W), lambda bi, t: (bi, tile_of(t), 0))
    return pl.pallas_call(
        functools.partial(_lru_scan_kernel, n_lat_tiles=nlt),
        grid=(b, n_tiles),
        in_specs=specs + [_full(conv_w), _full(cb2)] + [_full(a) for a in dir_args],
        out_specs=[out_spec(fwd), out_spec(bwd)],
        out_shape=[jax.ShapeDtypeStruct((b, s, LRU_W), F32)] * 2,
        scratch_shapes=[pltpu.VMEM((SEQ_TILE, LRU_W), F32)] * 4 + [pltpu.VMEM((8, LRU_W), F32)] * 2,
        compiler_params=_cparams(("parallel", "arbitrary")),
        name="lru_scan",
    )(p, p, p, p, p, p, conv_w, cb2, *dir_args)


def _post_kernel(of_ref, ob_ref, z_ref, gn_ref, ones_ref, hf_ref, hb_ref, y_ref, ua_ref, ub_ref):
    o = of_ref[...] + ob_ref[...]
    ms = jnp.dot(jnp.concatenate(_split3(o * o), axis=-1), ones_ref[...], preferred_element_type=F32) * (1.0 / GDN_DV)
    z = z_ref[...]
    ua_ref[...] = (o * lax.rsqrt(ms + RMS_EPS) * gn_ref[...] * (z * _sigmoid(z))).astype(ua_ref.dtype)
    y = y_ref[...]
    gelu = 0.5 * y * (1.0 + jnp.tanh(0.7978845608028654 * (y + 0.044715 * (y * y * y))))
    ub_ref[...] = ((hf_ref[...] + hb_ref[...]) * gelu).astype(ub_ref.dtype)


def _post(o_f, o_b, h_f, h_b, p, gdn_norm_g):
    b, s, _ = p.shape
    tm = _pick_tile(s, 544)
    ones3 = jnp.concatenate([_head_block_ones(GDN_HEADS, GDN_DV)] * 3, axis=0)
    gn = jnp.tile(gdn_norm_g.astype(F32), GDN_HEADS).reshape(1, GDN_W)
    tok = pl.BlockSpec((None, tm, GDN_W), lambda bi, i: (bi, i, 0))
    pcol = lambda c: pl.BlockSpec((None, tm, GDN_W), lambda bi, i: (bi, i, c))
    return pl.pallas_call(
        _post_kernel,
        grid=(b, s // tm),
        in_specs=[tok, tok, pcol(P_GDN_Z // GDN_W), _full(gn), _full(ones3), tok, tok, pcol(P_LRU_Y // LRU_W)],
        out_specs=[tok, tok],
        out_shape=[jax.ShapeDtypeStruct((b, s, GDN_W), BF16)] * 2,
        compiler_params=_cparams(("parallel", "parallel")),
        name="gdn_lru_post",
    )(o_f, o_b, p, gn, ones3, h_f, h_b, p)


def _merge_kernel(x_ref, mc_ref, ml_ref, u0_ref, u1_ref, u2_ref, u3_ref, t0_ref, t1_ref, t2_ref, t3_ref,
                  bg_ref, wb_ref, wo_ref, o_ref, *, n_lat):
    merged = None
    for n, (u_ref, t_ref) in enumerate(((u0_ref, t0_ref), (u1_ref, t1_ref), (u2_ref, t2_ref), (u3_ref, t3_ref))):
        gate = _sigmoid(t_ref[...] + bg_ref[n:n + 1, :])
        term = gate * jnp.dot(u_ref[...], wb_ref[n], preferred_element_type=F32)
        merged = term if merged is None else merged + term
    out = jnp.dot(merged.astype(BF16), wo_ref[...], preferred_element_type=F32)
    g1 = _row_select(x_ref.shape[0], pl.program_id(1), n_lat, mc_ref[2:3, :], ml_ref[2:3, :])
    o_ref[...] = x_ref[...] + g1 * out


def _merge(xs, mod_c, mod_l, branches, p, b_gate, w_branch, w_out, n_lat, with_ctx):
    b, s, d = xs.shape
    rows = s if with_ctx else n_lat
    tm = _pick_tile(rows, 544) if with_ctx else SEQ_TILE
    gate_blk = P_GATES // d
    tok = lambda w: pl.BlockSpec((None, tm, w), lambda bi, i: (bi, i, 0))
    in_specs = [tok(d), pl.BlockSpec((8, d), lambda bi, i: (0, 0)), pl.BlockSpec((None, 8, d), lambda bi, i: (bi, 0, 0))]
    in_specs += [tok(BRANCH_W)] * N_BRANCH
    in_specs += [pl.BlockSpec((None, tm, d), functools.partial(lambda bi, i, n: (bi, i, gate_blk + n), n=n))
                 for n in range(N_BRANCH)]
    in_specs += [_full(b_gate), _full(w_branch), _full(w_out)]
    return pl.pallas_call(
        functools.partial(_merge_kernel, n_lat=n_lat),
        grid=(b, rows // tm),
        in_specs=in_specs,
        out_specs=tok(d),
        out_shape=jax.ShapeDtypeStruct((b, rows, d), F32),
        input_output_aliases={0: 0} if with_ctx else {},
        compiler_params=_cparams(("parallel", "parallel")),
        name="merge",
    )(xs, mod_c, mod_l, *branches, p, p, p, p, b_gate, w_branch, w_out)


def _mlp_kernel(x_ref, mc_ref, ml_ref, gn_ref, gf_ref, w1_ref, w2_ref, o_ref, h_ref, acc_ref, *, n_lat, final_norm):
    f = pl.program_id(2)
    tm = x_ref.shape[0]
    i = pl.program_id(1)

    @pl.when(f == 0)
    def _():
        x = x_ref[...]
        y = x * lax.rsqrt(jnp.mean(x * x, axis=-1, keepdims=True) + RMS_EPS)
        y = y * gn_ref[...]
        shift = _row_select(tm, i, n_lat, mc_ref[3:4, :], ml_ref[3:4, :])
        scale = _row_select(tm, i, n_lat, mc_ref[4:5, :], ml_ref[4:5, :])
        h_ref[...] = (y * (1.0 + scale) + shift).astype(BF16)

    a = jnp.maximum(jnp.dot(h_ref[...], w1_ref[...], preferred_element_type=F32), 0.0)
    part = jnp.dot((a * a).astype(BF16), w2_ref[...], preferred_element_type=F32)

    @pl.when(f == 0)
    def _():
        acc_ref[...] = part

    @pl.when(f > 0)
    def _():
        acc_ref[...] += part

    @pl.when(f == pl.num_programs(2) - 1)
    def _():
        g2 = _row_select(tm, i, n_lat, mc_ref[5:6, :], ml_ref[5:6, :])
        y = x_ref[...] + g2 * acc_ref[...]
        if final_norm:
            y = y * lax.rsqrt(jnp.mean(y * y, axis=-1, keepdims=True) + RMS_EPS) * gf_ref[...]
        o_ref[...] = y


def _mlp(xs, mod_c, mod_l, gain, w1, w2, final_gain, n_lat, final_norm):
    b, rows, d = xs.shape
    dff = w1.shape[1]
    tm = _pick_tile(rows, 1088)
    tf = 1024
    row = pl.BlockSpec((1, d), lambda bi, i, f: (0, 0))
    return pl.pallas_call(
        functools.partial(_mlp_kernel, n_lat=n_lat, final_norm=final_norm),
        grid=(b, rows // tm, dff // tf),
        in_specs=[pl.BlockSpec((None, tm, d), lambda bi, i, f: (bi, i, 0)),
                  pl.BlockSpec((8, d), lambda bi, i, f: (0, 0)),
                  pl.BlockSpec((None, 8, d), lambda bi, i, f: (bi, 0, 0)),
                  row, row,
                  pl.BlockSpec((d, tf), lambda bi, i, f: (0, f)),
                  pl.BlockSpec((tf, d), lambda bi, i, f: (f, 0))],
        out_specs=pl.BlockSpec((None, tm, d), lambda bi, i, f: (bi, i, 0)),
        out_shape=jax.ShapeDtypeStruct((b, rows, d), F32),
        scratch_shapes=[pltpu.VMEM((tm, d), BF16), pltpu.VMEM((tm, d), F32)],
        compiler_params=_cparams(("parallel", "parallel", "arbitrary")),
        name="mlp",
    )(xs, mod_c, mod_l, gain, final_gain, w1, w2)


def kernel(x, c, ctx, c_ctx, mod_w, mod_b, norm1_g, norm2_g, w_in, b_gate, gdn_conv_w, gdn_a_log, gdn_dt_bias,
           gdn_norm_g, lru_conv_w, lru_conv_b, lru_w_r, lru_b_r, lru_w_i, lru_b_i, lru_lambda, mla_q_norm_g,
           mla_w_uq, mla_kv_norm_g, mla_w_ukv, na_rpb, w_branch, w_out, mlp_w1, mlp_w2, final_norm_g):
    bsz, n_tok, d = x.shape
    n_ctx = ctx.shape[1]
    depth = w_in.shape[0]
    assert n_ctx % SEQ_TILE == 0 and n_tok % SEQ_TILE == 0 and n_tok % GRID_W == 0
    na_meta, onehot_r, onehot_c = _na_geometry(n_tok // GRID_W)
    na_meta = jnp.asarray(na_meta)
    cos, sin = _rope_tables(n_tok, n_ctx)

    n_rows = -(-(bsz + 1) // 8) * 8
    cc = jnp.zeros((n_rows, d), F32).at[:bsz].set(c).at[bsz].set(c_ctx)
    final_gain = final_norm_g.reshape(1, d)

    xs = jnp.concatenate([x, ctx], axis=1)
    for l in range(depth):
        need_ctx = l < depth - 1
        mod = _modulation(cc, mod_w[l], mod_b[l]).reshape(n_rows, N_MOD, d)
        pad = jnp.zeros((8 - N_MOD, d), F32)
        mod_c = jnp.concatenate([mod[bsz], pad], axis=0)
        mod_l = jnp.concatenate([mod[:bsz], jnp.broadcast_to(pad, (bsz, 8 - N_MOD, d))], axis=1)

        w_in_l = _arrange_w_in(w_in[l])
        wq, wk, wv, place = _arrange_mla(mla_w_uq[l], mla_w_ukv[l])
        gq = mla_q_norm_g[l].reshape(1, -1)
        gkv = mla_kv_norm_g[l].reshape(1, -1)
        g1n = norm1_g[l].reshape(1, d)
        g2n = norm2_g[l].reshape(1, d)
        wb = w_branch[l].astype(BF16)
        wo = w_out[l].astype(BF16)
        w1 = mlp_w1[l].astype(BF16)
        w2 = mlp_w2[l].astype(BF16)

        p = _inproj(xs, mod_c, mod_l, g1n, w_in_l, n_tok)

        gq_, gk_, gv_, gbeta, gg = _gdn_prep(p, gdn_conv_w[l], gdn_a_log[l], gdn_dt_bias[l], n_tok)
        o_f, o_b = _gdn_scan(gq_, gk_, gv_, gbeta, gg, n_tok)
        h_f, h_b = _lru_scan(p, lru_conv_w[l], lru_conv_b[l], lru_w_r[l], lru_b_r[l], lru_w_i[l], lru_b_i[l],
                             lru_lambda[l], n_tok)
        ua, ub = _post(o_f, o_b, h_f, h_b, p, gdn_norm_g[l])

        mq, mk, mv = _mla_prep(p, cos, sin, gq, gkv, wq, wk, wv, place)
        uc = _mla_flash(mq, mk, mv, n_tok)
        if need_ctx:
            uc = _mla_flash(mq, mk, mv, n_tok, prev=uc)
        ud = _na_attention(p, na_meta, _na_bias_table(na_rpb[l], onehot_r, onehot_c), n_tok, need_ctx)

        xs = _merge(xs, mod_c, mod_l, (ua, ub, uc, ud), p, b_gate[l], wb, wo, n_tok, need_ctx)
        xs = _mlp(xs, mod_c, mod_l, g2n, w1, w2, final_gain, n_tok, l == depth - 1)
    return xs
```

```python
import functools

import jax
import jax.numpy as jnp
import numpy as np
from jax import lax
from jax.experimental import pallas as pl
from jax.experimental.pallas import tpu as pltpu

F32 = jnp.float32
BF16 = jnp.bfloat16

GRID_W = 64
N_MOD = 6
RMS_EPS = 1e-6
GDN_HEADS = 4
GDN_DK = 64
GDN_DV = 64
GDN_CHUNK = 64
GDN_W = GDN_HEADS * GDN_DV
GDN_CONV = 4
LRU_W = 256
LRU_BLOCKS = 4
LRU_BLOCK_W = LRU_W // LRU_BLOCKS
LRU_CONV = 4
LRU_C = 8.0
MLA_HEADS = 4
MLA_Q_RANK = 256
MLA_KV_RANK = 128
MLA_NOPE = 64
MLA_ROPE = 32
MLA_V = 64
MLA_SLOT = 128
ROPE_BASE = 10000.0
LOG2_E = 1.4426950408889634
NA_HEADS = 4
NA_DH = 64
NA_W = NA_HEADS * NA_DH
NA_WIN_ROWS = 8
NA_WIN_COLS = 16
N_BRANCH = 4
BRANCH_W = 256

SEQ_TILE = 256
HALO = 8

_REF_COLS = {}
_off = 0
for _name, _w in (('gdn_qkv', 3 * GDN_W), ('gdn_z', GDN_W), ('gdn_beta', 2 * GDN_HEADS), ('gdn_a', 2 * GDN_HEADS),
                  ('lru_x', LRU_W), ('lru_y', LRU_W), ('mla_q', MLA_Q_RANK), ('mla_kv', MLA_KV_RANK),
                  ('mla_kr', MLA_ROPE), ('na_qkv', 3 * NA_W)):
    _REF_COLS[_name] = (_off, _w)
    _off += _w
N_MIX_COLS = _off

P_GDN_QKV = 0
P_GDN_Z = 768
P_LRU_X = 1024
P_LRU_Y = 1280
P_MLA_KV = 1536
P_MLA_KR = 1664
P_GDN_BA = 1792
P32_COLS = 2048
P_MLA_Q = 0
P_NA_QKV = 256
P_GATES = 1024
P16_COLS = P_GATES + N_BRANCH * 1024
PROJ_TILE = 1024

VMEM_LIMIT = 52 * 1024 * 1024


def _cparams(sem):
    return pltpu.CompilerParams(dimension_semantics=sem, vmem_limit_bytes=VMEM_LIMIT)


def _pick_tile(n, cap):
    best = 8
    for t in range(8, min(n, cap) + 1, 8):
        if n % t == 0:
            best = t
    return best


def _full(a):
    return pl.BlockSpec(a.shape, lambda *_: (0,) * a.ndim)


def _split3(x):
    hi = x.astype(BF16)
    r = x - hi.astype(F32)
    mid = r.astype(BF16)
    lo = (r - mid.astype(F32)).astype(BF16)
    return hi, mid, lo


def _sigmoid(x):
    return 0.5 * jnp.tanh(0.5 * x) + 0.5


def _arrange_w_in(w_in):
    d = w_in.shape[0]
    pieces, pos = [], 0

    def put(dst, block):
        nonlocal pos
        if dst > pos:
            pieces.append(jnp.zeros((d, dst - pos), w_in.dtype))
        pieces.append(block)
        pos = dst + block.shape[1]

    ref = lambda name: w_in[:, _REF_COLS[name][0]:_REF_COLS[name][0] + _REF_COLS[name][1]]
    for name, dst in (('gdn_qkv', P_GDN_QKV), ('gdn_z', P_GDN_Z), ('lru_x', P_LRU_X), ('lru_y', P_LRU_Y),
                      ('mla_kv', P_MLA_KV), ('mla_kr', P_MLA_KR), ('gdn_beta', P_GDN_BA),
                      ('gdn_a', P_GDN_BA + 2 * GDN_HEADS)):
        put(dst, ref(name))
    put(P32_COLS + P_MLA_Q, ref('mla_q'))
    put(P32_COLS + P_NA_QKV, ref('na_qkv'))
    put(P32_COLS + P_GATES, w_in[:, N_MIX_COLS:])
    assert pos == P32_COLS + P16_COLS
    return jnp.concatenate(pieces, axis=1).astype(BF16)


def _rope_perm():
    q = MLA_ROPE // 4
    src = np.zeros(MLA_ROPE, np.int32)
    sign = np.zeros(MLA_ROPE, np.float32)
    for base in (0, 2 * q):
        for d in range(q):
            src[base + d] = base + d + q
            sign[base + d] = -1.0
            src[base + q + d] = base + d
            sign[base + q + d] = 1.0
    return src, sign


def _arrange_mla(w_uq, w_ukv):
    src, sign = _rope_perm()
    hq = MLA_NOPE + MLA_ROPE
    wq = jnp.zeros((MLA_Q_RANK, 2 * MLA_HEADS * MLA_SLOT), F32)
    wk = jnp.zeros((MLA_KV_RANK, MLA_HEADS * MLA_SLOT), F32)
    wv = jnp.zeros((MLA_KV_RANK, MLA_HEADS * MLA_V), F32)
    place = np.zeros((2 * MLA_SLOT, 2 * MLA_HEADS * MLA_SLOT), np.float32)
    rot_off = MLA_HEADS * MLA_SLOT
    for h in range(MLA_HEADS):
        nope = w_uq[:, h * hq:h * hq + MLA_NOPE]
        pe = w_uq[:, h * hq + MLA_NOPE:(h + 1) * hq]
        s = h * MLA_SLOT
        wq = wq.at[:, s:s + MLA_NOPE].set(nope)
        wq = wq.at[:, s + MLA_NOPE:s + MLA_NOPE + MLA_ROPE].set(pe)
        wq = wq.at[:, rot_off + s + MLA_NOPE:rot_off + s + MLA_NOPE + MLA_ROPE].set(pe[:, src] * sign)
        wk = wk.at[:, s:s + MLA_NOPE].set(w_ukv[:, h * 128:h * 128 + MLA_NOPE])
        wv = wv.at[:, h * MLA_V:(h + 1) * MLA_V].set(w_ukv[:, h * 128 + MLA_NOPE:(h + 1) * 128])
        for d in range(MLA_ROPE):
            for half in (0, MLA_SLOT):
                place[half + d, s + MLA_NOPE + d] = 1.0
                place[half + src[d], rot_off + s + MLA_NOPE + d] = sign[d]
    return wq.astype(BF16), wk.astype(BF16), wv.astype(BF16), jnp.asarray(place, BF16)


def _rope_tables(n_tok, n_ctx):
    cos = np.ones((n_tok + n_ctx, MLA_SLOT), np.float32)
    sin = np.zeros((n_tok + n_ctx, MLA_SLOT), np.float32)
    t = np.arange(n_tok)
    row = (t // GRID_W).astype(np.float32)
    col = (t % GRID_W).astype(np.float32)
    n_freq = MLA_ROPE // 4
    inv = (ROPE_BASE ** (-np.arange(n_freq, dtype=np.float32) / n_freq)).astype(np.float32)
    ar = row[:, None] * inv
    ac = col[:, None] * inv
    ang = np.concatenate([ar, ar, ac, ac], axis=-1).astype(np.float32)
    cos[:n_tok, MLA_NOPE:MLA_NOPE + MLA_ROPE] = np.cos(ang)
    sin[:n_tok, MLA_NOPE:MLA_NOPE + MLA_ROPE] = np.sin(ang)
    return jnp.asarray(cos), jnp.asarray(sin)


def _head_block_ones(n_heads, width):
    m = np.kron(np.eye(n_heads, dtype=np.float32), np.ones((width, width), np.float32))
    return jnp.asarray(m, BF16)


def _mod_kernel(c_ref, w_ref, b_ref, o_ref):
    c = c_ref[...]
    s = c * _sigmoid(c)
    o_ref[...] = jnp.dot(s, w_ref[...], preferred_element_type=F32) + b_ref[...]


def _modulation(cc, mod_w, mod_b):
    r, d = cc.shape
    n = mod_w.shape[1]
    tn = 1024
    return pl.pallas_call(
        _mod_kernel,
        grid=(n // tn,),
        in_specs=[pl.BlockSpec((r, d), lambda j: (0, 0)),
                  pl.BlockSpec((d, tn), lambda j: (0, j)),
                  pl.BlockSpec((1, tn), lambda j: (0, j))],
        out_specs=pl.BlockSpec((r, tn), lambda j: (0, j)),
        out_shape=jax.ShapeDtypeStruct((r, n), F32),
        compiler_params=_cparams(("arbitrary",)),
        name="modulation",
    )(cc, mod_w, mod_b.reshape(1, n))


def _row_select(tile_rows, tile_idx, n_lat, ctx_vec, lat_vec):
    row = tile_idx * tile_rows + lax.broadcasted_iota(jnp.int32, (tile_rows, 1), 0)
    return jnp.where(row < n_lat, lat_vec, ctx_vec)


def _inproj_kernel(x_ref, mc_ref, ml_ref, g_ref, w_ref, o32_ref, o16_ref, h_ref, *, n_lat):
    j = pl.program_id(2)
    n32 = P32_COLS // PROJ_TILE

    @pl.when(j == 0)
    def _():
        tm = x_ref.shape[0]
        i = pl.program_id(1)
        x = x_ref[...]
        y = x * lax.rsqrt(jnp.mean(x * x, axis=-1, keepdims=True) + RMS_EPS)
        y = y * g_ref[...]
        shift = _row_select(tm, i, n_lat, mc_ref[0:1, :], ml_ref[0:1, :])
        scale = _row_select(tm, i, n_lat, mc_ref[1:2, :], ml_ref[1:2, :])
        h_ref[...] = (y * (1.0 + scale) + shift).astype(BF16)

    acc = jnp.dot(h_ref[...], w_ref[...], preferred_element_type=F32)

    @pl.when(j < n32)
    def _():
        o32_ref[...] = acc

    @pl.when(j >= n32)
    def _():
        o16_ref[...] = acc.astype(BF16)


def _inproj(xs, mod_c, mod_l, gain, w, n_lat):
    b, s, d = xs.shape
    tm = _pick_tile(s, 1088)
    tn = PROJ_TILE
    n32 = P32_COLS // tn
    return pl.pallas_call(
        functools.partial(_inproj_kernel, n_lat=n_lat),
        grid=(b, s // tm, (P32_COLS + P16_COLS) // tn),
        in_specs=[pl.BlockSpec((None, tm, d), lambda bi, i, j: (bi, i, 0)),
                  pl.BlockSpec((8, d), lambda bi, i, j: (0, 0)),
                  pl.BlockSpec((None, 8, d), lambda bi, i, j: (bi, 0, 0)),
                  pl.BlockSpec((1, d), lambda bi, i, j: (0, 0)),
                  pl.BlockSpec((d, tn), lambda bi, i, j: (0, j))],
        out_specs=[pl.BlockSpec((None, tm, tn), lambda bi, i, j: (bi, i, jnp.minimum(j, n32 - 1))),
                   pl.BlockSpec((None, tm, tn), lambda bi, i, j: (bi, i, jnp.maximum(j - n32, 0)))],
        out_shape=[jax.ShapeDtypeStruct((b, s, P32_COLS), F32), jax.ShapeDtypeStruct((b, s, P16_COLS), BF16)],
        scratch_shapes=[pltpu.VMEM((tm, d), BF16)],
        compiler_params=_cparams(("parallel", "parallel", "arbitrary")),
        name="inproj",
    )(xs, mod_c, mod_l, gain, w)


def _mla_prep_kernel(ql_ref, kv_ref, cos_ref, sin_ref, gq_ref, gkv_ref, wq_ref, wk_ref, wv_ref, pl_ref,
                     q_ref, k_ref, v_ref, *, scale):
    nslot = MLA_HEADS * MLA_SLOT
    cos = jnp.concatenate([cos_ref[...]] * MLA_HEADS, axis=-1)
    sin = jnp.concatenate([sin_ref[...]] * MLA_HEADS, axis=-1)

    ql = ql_ref[...].astype(F32)
    qn = ql * lax.rsqrt(jnp.mean(ql * ql, axis=-1, keepdims=True) + RMS_EPS) * gq_ref[...]
    q2 = jnp.dot(qn.astype(BF16), wq_ref[...], preferred_element_type=F32)
    q = (q2[:, :nslot] * cos + q2[:, nslot:] * sin) * scale
    q_ref[...] = q.astype(BF16)

    kvkr = kv_ref[...]
    kvl = kvkr[:, :MLA_KV_RANK]
    kr = kvkr[:, MLA_KV_RANK:]
    kvn = (kvl * lax.rsqrt(jnp.mean(kvl * kvl, axis=-1, keepdims=True) + RMS_EPS) * gkv_ref[...]).astype(BF16)
    kn = jnp.dot(kvn, wk_ref[...], preferred_element_type=F32)
    v_ref[...] = jnp.dot(kvn, wv_ref[...], preferred_element_type=F32).astype(BF16)
    kr_hi = kr.astype(BF16)
    kr_lo = (kr - kr_hi.astype(F32)).astype(BF16)
    kr2 = jnp.dot(jnp.concatenate([kr_hi, kr_lo], axis=-1), pl_ref[...], preferred_element_type=F32)
    k_ref[...] = (kn + kr2[:, :nslot] * cos + kr2[:, nslot:] * sin).astype(BF16)


def _mla_prep(p32, p16, cos, sin, gq, gkv, wq, wk, wv, place):
    b, s, _ = p32.shape
    tm = _pick_tile(s, 544)
    nslot = MLA_HEADS * MLA_SLOT
    scale = (MLA_NOPE + MLA_ROPE) ** -0.5 * LOG2_E
    return pl.pallas_call(
        functools.partial(_mla_prep_kernel, scale=scale),
        grid=(b, s // tm),
        in_specs=[pl.BlockSpec((None, tm, MLA_Q_RANK), lambda bi, i: (bi, i, P_MLA_Q // MLA_Q_RANK)),
                  pl.BlockSpec((None, tm, 2 * MLA_SLOT), lambda bi, i: (bi, i, P_MLA_KV // (2 * MLA_SLOT))),
                  pl.BlockSpec((tm, MLA_SLOT), lambda bi, i: (i, 0)),
                  pl.BlockSpec((tm, MLA_SLOT), lambda bi, i: (i, 0)),
                  _full(gq), _full(gkv), _full(wq), _full(wk), _full(wv), _full(place)],
        out_specs=[pl.BlockSpec((None, tm, nslot), lambda bi, i: (bi, i, 0)),
                   pl.BlockSpec((None, tm, nslot), lambda bi, i: (bi, i, 0)),
                   pl.BlockSpec((None, tm, MLA_HEADS * MLA_V), lambda bi, i: (bi, i, 0))],
        out_shape=[jax.ShapeDtypeStruct((b, s, nslot), BF16),
                   jax.ShapeDtypeStruct((b, s, nslot), BF16),
                   jax.ShapeDtypeStruct((b, s, MLA_HEADS * MLA_V), BF16)],
        compiler_params=_cparams(("parallel", "parallel")),
        name="mla_prep",
    )(p16, p32, cos, sin, gq, gkv, wq, wk, wv, place)


def _flash_update(h, q, k, v, m_ref, l_ref, acc_ref):
    s = lax.dot_general(q, k, (((1,), (1,)), ((), ())), preferred_element_type=F32)
    m_prev = m_ref[h]
    m_new = jnp.maximum(m_prev, jnp.max(s, axis=-1, keepdims=True))
    alpha = jnp.exp2(m_prev - m_new)
    p = jnp.exp2(s - jnp.concatenate([m_new] * (s.shape[1] // 128), axis=-1))
    l_ref[h] = alpha * l_ref[h] + jnp.sum(p, axis=-1, keepdims=True)
    acc_ref[h] = alpha * acc_ref[h] + jnp.dot(p.astype(BF16), v, preferred_element_type=F32)
    m_ref[h] = m_new


def _mla_flash_kernel(*refs, tk, aliased):
    if aliased:
        q_ref, k_ref, v_ref, _, o_ref, m_ref, l_ref, acc_ref = refs
    else:
        q_ref, k_ref, v_ref, o_ref, m_ref, l_ref, acc_ref = refs
    tq = q_ref.shape[0]
    nk = k_ref.shape[0]
    n_loop = nk // tk
    m_ref[...] = jnp.full(m_ref.shape, -jnp.inf, F32)
    l_ref[...] = jnp.zeros(l_ref.shape, F32)
    acc_ref[...] = jnp.zeros(acc_ref.shape, F32)

    def chunk(rows):
        for h in range(MLA_HEADS):
            hs = slice(h * MLA_SLOT, (h + 1) * MLA_SLOT)
            vs = slice((h // 2) * 128, (h // 2) * 128 + 128)
            _flash_update(h, q_ref[:, hs], k_ref[rows, hs], v_ref[rows, vs], m_ref, l_ref, acc_ref)

    if n_loop:
        def body(j, carry):
            chunk(pl.ds(pl.multiple_of(j * tk, tk), tk))
            return carry
        lax.fori_loop(0, n_loop, body, 0)
    if nk > n_loop * tk:
        chunk(slice(n_loop * tk, nk))

    lane = lax.broadcasted_iota(jnp.int32, (tq, 128), 1)
    outs = []
    for pair in range(MLA_HEADS // 2):
        o0 = acc_ref[2 * pair] / l_ref[2 * pair]
        o1 = acc_ref[2 * pair + 1] / l_ref[2 * pair + 1]
        outs.append(jnp.where(lane < MLA_V, o0, o1))
    o_ref[...] = jnp.concatenate(outs, axis=-1).astype(o_ref.dtype)


def _mla_flash(q, k, v, n_lat, prev=None):
    b, s, nslot = q.shape
    n_ctx = s - n_lat
    nv = MLA_HEADS * MLA_V
    tk = 512
    if prev is None:
        tq = _pick_tile(n_lat, 512)
        q_off, n_q, kv_rows, kv_blk = 0, n_lat // tq, s, 0
    else:
        assert n_lat % n_ctx == 0
        tq = _pick_tile(n_ctx, 256)
        q_off, n_q, kv_rows, kv_blk = n_lat // tq, n_ctx // tq, n_ctx, n_lat // n_ctx
    in_specs = [pl.BlockSpec((None, tq, nslot), lambda bi, i: (bi, i + q_off, 0)),
                pl.BlockSpec((None, kv_rows, nslot), lambda bi, i: (bi, kv_blk, 0)),
                pl.BlockSpec((None, kv_rows, nv), lambda bi, i: (bi, kv_blk, 0))]
    args = [q, k, v]
    aliases = {}
    if prev is not None:
        in_specs.append(pl.BlockSpec(memory_space=pl.ANY))
        args.append(prev)
        aliases = {3: 0}
    return pl.pallas_call(
        functools.partial(_mla_flash_kernel, tk=tk, aliased=prev is not None),
        grid=(b, n_q),
        in_specs=in_specs,
        out_specs=pl.BlockSpec((None, tq, nv), lambda bi, i: (bi, i + q_off, 0)),
        out_shape=jax.ShapeDtypeStruct((b, s, nv), BF16),
        scratch_shapes=[pltpu.VMEM((MLA_HEADS, tq, 128), F32),
                        pltpu.VMEM((MLA_HEADS, tq, 128), F32),
                        pltpu.VMEM((MLA_HEADS, tq, 128), F32)],
        input_output_aliases=aliases,
        compiler_params=_cparams(("parallel", "arbitrary")),
        name="mla_flash",
    )(*args)


NA_QROWS = 4
NA_SLAB = NA_QROWS + NA_WIN_ROWS


def _na_geometry(rows):
    assert rows % NA_QROWS == 0 and rows >= NA_SLAB
    nblk = rows // NA_QROWS
    qc = np.arange(GRID_W)
    cs = np.clip(qc - NA_WIN_COLS // 2, 0, GRID_W - NA_WIN_COLS)
    col_valid = (qc[None, :] >= cs[:, None]) & (qc[None, :] < cs[:, None] + NA_WIN_COLS)
    rel_c = np.clip(qc[None, :] - qc[:, None] + NA_WIN_COLS - 1, 0, 2 * NA_WIN_COLS - 2)
    onehot_c = (rel_c[None] == np.arange(2 * NA_WIN_COLS - 1)[:, None, None]) & col_valid[None]
    patterns, cls, starts = {}, [], []
    for i in range(nblk):
        r0 = i * NA_QROWS
        start = int(np.clip(r0 - NA_WIN_ROWS // 2, 0, rows - NA_SLAB))
        qr = r0 + np.arange(NA_QROWS)
        rs = np.clip(qr - NA_WIN_ROWS // 2, 0, rows - NA_WIN_ROWS)
        key = (start - r0,) + tuple((rs - r0).tolist())
        if key not in patterns:
            kr = start + np.arange(NA_SLAB)
            row_valid = (kr[None, :] >= rs[:, None]) & (kr[None, :] < rs[:, None] + NA_WIN_ROWS)
            rel_r = np.clip(kr[None, :] - qr[:, None] + NA_WIN_ROWS - 1, 0, 2 * NA_WIN_ROWS - 2)
            onehot_r = (rel_r[..., None] == np.arange(2 * NA_WIN_ROWS - 1)) & row_valid[..., None]
            patterns[key] = (len(patterns), onehot_r)
        cls.append(patterns[key][0])
        starts.append(start)
    ordered = sorted(patterns.values(), key=lambda z: z[0])
    onehot_r = np.stack([z[1] for z in ordered]).astype(np.float32)
    meta = np.stack([np.asarray(cls, np.int32), np.asarray(starts, np.int32)])
    return meta, onehot_r, onehot_c.astype(np.float32)


def _na_bias_table(rpb, onehot_r, onehot_c):
    hi = lax.Precision.HIGHEST
    bc = jnp.einsum('hrc,cqk->hrqk', rpb.astype(F32), onehot_c, precision=hi)
    tab = jnp.einsum('pabr,hrqk->phaqbk', onehot_r, bc, precision=hi)
    valid = jnp.einsum('pabr,cqk->paqbk', onehot_r, onehot_c, precision=hi) > 0.5
    tab = jnp.where(valid[:, None], tab, -jnp.inf)
    p, h = tab.shape[:2]
    return tab.reshape(p, h, NA_QROWS * GRID_W, NA_SLAB * GRID_W)


def _na_attend(q_pair, lane, h, k_parts, v_parts, bias):
    in_head = (lane < NA_DH) if h % 2 == 0 else (lane >= NA_DH)
    qm = jnp.where(in_head, q_pair * (NA_DH ** -0.5), 0.0).astype(BF16)
    scores = [lax.dot_general(qm, k, (((1,), (1,)), ((), ())), preferred_element_type=F32) for k in k_parts]
    if bias is not None:
        scores[0] = scores[0] + bias
    m = scores[0].max(axis=-1, keepdims=True)
    for s in scores[1:]:
        m = jnp.maximum(m, s.max(axis=-1, keepdims=True))
    den = None
    out = None
    for s, v in zip(scores, v_parts):
        p = jnp.exp(s - m)
        ps = p.sum(axis=-1, keepdims=True)
        po = jnp.dot(p.astype(BF16), v, preferred_element_type=F32)
        den = ps if den is None else den + ps
        out = po if out is None else out + po
    return out / den


def _na_kernel(meta_ref, q_ref, k_ref, v_ref, bias_ref, o_ref, *, n_lat, with_ctx):
    i = pl.program_id(1)
    nq = q_ref.shape[0]
    n_lat_tiles = n_lat // nq
    n_all = k_ref.shape[0]
    lane = lax.broadcasted_iota(jnp.int32, (nq, 128), 1)

    def run(windowed):
        outs = []
        for pair in range(NA_HEADS // 2):
            ls = slice(pair * 128, (pair + 1) * 128)
            q_pair = q_ref[:, ls]
            k_ctx = k_ref[n_lat:n_all, ls].astype(BF16)
            v_ctx = v_ref[n_lat:n_all, ls].astype(BF16)
            if windowed:
                start = pl.multiple_of(meta_ref[1, jnp.minimum(i, n_lat_tiles - 1)] * GRID_W, GRID_W)
                rows = pl.ds(start, NA_SLAB * GRID_W)
                ks = (k_ref[rows, ls].astype(BF16), k_ctx)
                vs = (v_ref[rows, ls].astype(BF16), v_ctx)
            else:
                ks, vs = (k_ctx,), (v_ctx,)
            o = [_na_attend(q_pair, lane, 2 * pair + e, ks, vs, bias_ref[2 * pair + e] if windowed else None)
                 for e in range(2)]
            outs.append(jnp.where(lane < NA_DH, o[0], o[1]))
        o_ref[...] = jnp.concatenate(outs, axis=-1).astype(o_ref.dtype)

    if with_ctx:
        pl.when(i < n_lat_tiles)(lambda: run(True))
        pl.when(i >= n_lat_tiles)(lambda: run(False))
    else:
        run(True)


def _na_attention(p, meta, table, n_lat, with_ctx_queries):
    b, s, _ = p.shape
    nq = NA_QROWS * GRID_W
    nk = NA_SLAB * GRID_W
    assert (s - n_lat) % nq == 0
    nlt = n_lat // nq
    qb = P_NA_QKV // NA_W
    grid_spec = pltpu.PrefetchScalarGridSpec(
        num_scalar_prefetch=1,
        grid=(b, s // nq if with_ctx_queries else nlt),
        in_specs=[pl.BlockSpec((None, nq, NA_W), lambda bi, i, m: (bi, i, qb)),
                  pl.BlockSpec((None, s, NA_W), lambda bi, i, m: (bi, 0, qb + 1)),
                  pl.BlockSpec((None, s, NA_W), lambda bi, i, m: (bi, 0, qb + 2)),
                  pl.BlockSpec((None, NA_HEADS, nq, nk),
                               lambda bi, i, m: (m[0, jnp.minimum(i, nlt - 1)], 0, 0, 0))],
        out_specs=pl.BlockSpec((None, nq, NA_W), lambda bi, i, m: (bi, i, 0)))
    return pl.pallas_call(
        functools.partial(_na_kernel, n_lat=n_lat, with_ctx=with_ctx_queries),
        grid_spec=grid_spec,
        out_shape=jax.ShapeDtypeStruct((b, s, NA_W), BF16),
        compiler_params=_cparams(("parallel", "arbitrary")),
        name="na_attention",
    )(meta, p, p, p, table)


def _tile_conv(x, prev, nxt, w_ref, tile_idx, n_tiles, n_lat_tiles):
    r = x.shape[0]
    width = w_ref.shape[0]
    left = width // 2
    has_prev = jnp.logical_and(tile_idx != 0, tile_idx != n_lat_tiles)
    has_next = jnp.logical_and(tile_idx != n_tiles - 1, tile_idx != n_lat_tiles - 1)
    prev = jnp.where(has_prev, prev, 0.0)
    nxt = jnp.where(has_next, nxt, 0.0)
    xe = jnp.concatenate([prev, x, nxt], axis=0)
    acc = None
    for j in range(width):
        o = HALO - left + j
        term = xe[o:o + r, :] * w_ref[j:j + 1, :]
        acc = term if acc is None else acc + term
    return acc


def _halo_specs(width, col_block, tile_of):
    per = SEQ_TILE // HALO

    def main(bi, s, *_):
        return (bi, tile_of(s), col_block)

    def prev(bi, s, *_):
        return (bi, jnp.maximum(tile_of(s) * per - 1, 0), col_block)

    def make_next(n_tiles):
        def nxt(bi, s, *_):
            return (bi, jnp.minimum((tile_of(s) + 1) * per, n_tiles * per - 1), col_block)
        return nxt

    return main, prev, make_next


def _gdn_prep_kernel(x_ref, xp_ref, xn_ref, ba_ref, cw_ref, ones_ref, exp_ref, alog_ref, dtb_ref,
                     q_ref, k_ref, v_ref, beta_ref, g_ref, *, n_lat_tiles):
    i = pl.program_id(1)
    y = _tile_conv(x_ref[...], xp_ref[...], xn_ref[...], cw_ref, i, pl.num_programs(1), n_lat_tiles)
    y = y * _sigmoid(y)
    q = y[:, :GDN_W]
    k = y[:, GDN_W:2 * GDN_W]
    v_ref[...] = y[:, 2 * GDN_W:]

    def head_norm(u):
        parts = jnp.concatenate(_split3(u * u), axis=-1)
        ss = jnp.dot(parts, ones_ref[...], preferred_element_type=F32)
        return u * lax.rsqrt(ss + RMS_EPS)

    q_ref[...] = head_norm(q) * (GDN_DK ** -0.5)
    k_ref[...] = head_norm(k)

    logits = jnp.dot(jnp.concatenate(_split3(ba_ref[...]), axis=-1), exp_ref[...], preferred_element_type=F32)
    half = 2 * GDN_W
    beta_ref[...] = _sigmoid(logits[:, :half])
    a = logits[:, half:] + dtb_ref[...]
    softplus = jnp.maximum(a, 0.0) + jnp.log1p(jnp.exp(-jnp.abs(a)))
    g_ref[...] = -jnp.exp(alog_ref[...]) * softplus


def _gdn_prep(p, conv_w, a_log, dt_bias, n_lat):
    b, s, _ = p.shape
    n_tiles = s // SEQ_TILE
    main, prev, make_next = _halo_specs(3 * GDN_W, 0, lambda t: t)
    ones3 = jnp.concatenate([_head_block_ones(GDN_HEADS, GDN_DK)] * 3, axis=0)
    expand = np.zeros((128, 4 * GDN_W), np.float32)
    for kind in range(2):
        for d in range(2):
            for h in range(GDN_HEADS):
                c0 = kind * 2 * GDN_W + d * GDN_W + h * GDN_DV
                expand[kind * 2 * GDN_HEADS + d * GDN_HEADS + h, c0:c0 + GDN_DV] = 1.0
    expand3 = jnp.asarray(np.concatenate([expand] * 3, axis=0), BF16)
    alog_e = jnp.repeat(a_log.astype(F32).reshape(-1), GDN_DV).reshape(1, 2 * GDN_W)
    dtb_e = jnp.repeat(dt_bias.astype(F32).reshape(-1), GDN_DV).reshape(1, 2 * GDN_W)
    tok = lambda w: pl.BlockSpec((None, SEQ_TILE, w), lambda bi, i: (bi, i, 0))
    return pl.pallas_call(
        functools.partial(_gdn_prep_kernel, n_lat_tiles=n_lat // SEQ_TILE),
        grid=(b, n_tiles),
        in_specs=[pl.BlockSpec((None, SEQ_TILE, 3 * GDN_W), main),
                  pl.BlockSpec((None, HALO, 3 * GDN_W), prev),
                  pl.BlockSpec((None, HALO, 3 * GDN_W), make_next(n_tiles)),
                  pl.BlockSpec((None, SEQ_TILE, 128), lambda bi, i: (bi, i, P_GDN_BA // 128)),
                  _full(conv_w), _full(ones3), _full(expand3), _full(alog_e), _full(dtb_e)],
        out_specs=[tok(GDN_W), tok(GDN_W), tok(GDN_W), tok(2 * GDN_W), tok(2 * GDN_W)],
        out_shape=[jax.ShapeDtypeStruct((b, s, GDN_W), F32)] * 3 + [jax.ShapeDtypeStruct((b, s, 2 * GDN_W), F32)] * 2,
        compiler_params=_cparams(("parallel", "parallel")),
        name="gdn_prep",
    )(p, p, p, p, conv_w, ones3, expand3, alog_e, dtb_e)


def _gdn_masks():
    c, w = GDN_CHUNK, GDN_W
    r2, c2 = np.arange(w)[:, None], np.arange(w)[None, :]
    bd = ((r2 // c) == (c2 // c)).astype(np.float32)
    i = np.arange(c)[:, None]
    j = (np.arange(w) % c)[None, :]
    level = np.zeros((c, w), np.int32)
    for bit in range(6):
        level += ((i ^ j) >= (1 << bit)).astype(np.int32)
    lvl = np.stack([(level == m).astype(np.float32) for m in range(7)])
    dirm = np.stack([np.stack([(j <= i), (j < i)]), np.stack([(j >= i), (j > i)])]).astype(np.float32)
    tj = (np.arange(3 * c) % c)[None, :]
    tri = np.stack([(tj <= i), (tj >= i)]).astype(np.float32)
    return jnp.asarray(bd, BF16), jnp.asarray(lvl), jnp.asarray(dirm), jnp.asarray(tri, BF16)


def _blockdiag(y, bd):
    yb = y.astype(BF16)
    return jnp.concatenate([yb] * GDN_HEADS, axis=0) * bd


def _mm(a, b):
    return jnp.dot(a.astype(BF16), b, preferred_element_type=F32)


def _gdn_intra(probs, bd, lvl_ref, dirm_ref, tri_ref):
    c = GDN_CHUNK
    n = len(probs)
    eye = lvl_ref[0]
    gc, g_last, decay, gram = [], [], [], []
    for q, k, v, beta, g, rev in probs:
        d = 1 if rev else 0
        gcp = jnp.dot(tri_ref[d], jnp.concatenate(_split3(g), axis=0), preferred_element_type=F32)
        gc.append(gcp)
        g_last.append(gcp[0:1, :] if rev else gcp[c - 1:c, :])
        gc_row = jnp.sum(gcp * eye, axis=0, keepdims=True)
        decay.append(dirm_ref[d, 0] * jnp.exp(jnp.minimum(gcp - gc_row, 0.0)))
        gram.append(lax.dot_general(jnp.concatenate([k, q], axis=0).astype(BF16), _blockdiag(k, bd),
                                    (((1,), (1,)), ((), ())), preferred_element_type=F32))
    lower = [dirm_ref[1 if p[5] else 0, 1] * p[3] * gram[x][:c] * decay[x] for x, p in enumerate(probs)]
    a_intra = [gram[x][c:] * decay[x] for x in range(n)]
    t = [eye - lower[x] * lvl_ref[1] for x in range(n)]
    for lev in range(2, 7):
        y = [_mm(t[x], _blockdiag(lower[x] * lvl_ref[lev], bd)) for x in range(n)]
        z = [_mm(y[x], _blockdiag(t[x], bd)) for x in range(n)]
        t = [t[x] - z[x] for x in range(n)]
    e_gc = [jnp.exp(gc[x]) for x in range(n)]
    u = [_mm(t[x], _blockdiag(p[2] * p[3], bd)) for x, p in enumerate(probs)]
    w = [_mm(t[x], _blockdiag(p[1] * p[3] * e_gc[x], bd)) for x, p in enumerate(probs)]
    wq = [jnp.concatenate([w[x], p[0] * e_gc[x]], axis=0).astype(BF16) for x, p in enumerate(probs)]
    k_dec = [(p[1] * jnp.exp(g_last[x] - gc[x])).astype(BF16) for x, p in enumerate(probs)]
    g_tot = [jnp.exp(g_last[x]) for x in range(n)]
    return u, wq, k_dec, a_intra, g_tot


def _gdn_state_step(u, wq, k_dec, a_intra, g_tot, s_ref, bd):
    c = GDN_CHUNK
    s_bd = s_ref[...]
    ws_qs = jnp.dot(wq, s_bd.astype(BF16), preferred_element_type=F32)
    v_new = u - ws_qs[:c]
    o = ws_qs[c:] + _mm(a_intra, _blockdiag(v_new, bd))
    kv = lax.dot_general(k_dec, v_new.astype(BF16), (((0,), (0,)), ((), ())), preferred_element_type=F32)
    s_ref[...] = s_bd * g_tot + kv * bd.astype(F32)
    return o


def _gdn_scan_kernel(qf, kf, vf, bf, gf, qb, kb, vb, bb, gb, bd_ref, lvl_ref, dirm_ref, tri_ref,
                     of_ref, ob_ref, sf_ref, sb_ref):
    @pl.when(pl.program_id(1) == 0)
    def _():
        sf_ref[...] = jnp.zeros(sf_ref.shape, F32)
        sb_ref[...] = jnp.zeros(sb_ref.shape, F32)

    bd = bd_ref[...]
    n = SEQ_TILE // GDN_CHUNK
    probs, rows = [], []
    for c in range(n):
        rf = slice(c * GDN_CHUNK, (c + 1) * GDN_CHUNK)
        rb = slice((n - 1 - c) * GDN_CHUNK, (n - c) * GDN_CHUNK)
        probs.append((qf[rf, :], kf[rf, :], vf[rf, :], bf[rf, :], gf[rf, :], False))
        probs.append((qb[rb, :], kb[rb, :], vb[rb, :], bb[rb, :], gb[rb, :], True))
        rows += [rf, rb]
    u, wq, k_dec, a_intra, g_tot = _gdn_intra(probs, bd, lvl_ref, dirm_ref, tri_ref)
    for x in range(2 * n):
        rev = probs[x][5]
        o = _gdn_state_step(u[x], wq[x], k_dec[x], a_intra[x], g_tot[x], sb_ref if rev else sf_ref, bd)
        (ob_ref if rev else of_ref)[rows[x], :] = o


def _seq_tile_maps(n_lat_tiles, n_tiles):
    fwd = lambda s: lax.rem(s + n_lat_tiles, n_tiles)
    bwd = lambda s: n_tiles - 1 - s
    return fwd, bwd


def _gdn_scan(q, k, v, beta, g, n_lat):
    b, s, _ = q.shape
    n_tiles = s // SEQ_TILE
    fwd, bwd = _seq_tile_maps(n_lat // SEQ_TILE, n_tiles)
    spec = lambda tile_of, col: pl.BlockSpec((None, SEQ_TILE, GDN_W), lambda bi, t: (bi, tile_of(t), col))
    masks = _gdn_masks()
    in_specs = [spec(fwd, 0)] * 5 + [spec(bwd, 0)] * 3 + [spec(bwd, 1)] * 2 + [_full(m) for m in masks]
    return pl.pallas_call(
        _gdn_scan_kernel,
        grid=(b, n_tiles),
        in_specs=in_specs,
        out_specs=[spec(fwd, 0), spec(bwd, 0)],
        out_shape=[jax.ShapeDtypeStruct((b, s, GDN_W), F32)] * 2,
        scratch_shapes=[pltpu.VMEM((GDN_W, GDN_W), F32), pltpu.VMEM((GDN_W, GDN_W), F32)],
        compiler_params=_cparams(("parallel", "arbitrary")),
        name="gdn_scan",
    )(q, k, v, beta, g, q, k, v, beta, g, *masks)


def _lru_tile_prep(x_ref, xp_ref, xn_ref, cw_ref, cb_ref, wg_ref, bg_ref, nla_ref, a_ref, b_ref,
                   tile_idx, n_tiles, n_lat_tiles):
    xb = _tile_conv(x_ref[...], xp_ref[...], xn_ref[...], cw_ref, tile_idx, n_tiles, n_lat_tiles) + cb_ref[...]
    gates = _sigmoid(jnp.dot(xb.astype(BF16), wg_ref[...], preferred_element_type=F32) + bg_ref[...])
    log_a = nla_ref[...] * gates[:, :LRU_W]
    a_ref[...] = jnp.exp(log_a)
    th = jnp.tanh(log_a)
    b_ref[...] = jnp.sqrt(-2.0 * th / (1.0 - th)) * gates[:, LRU_W:] * xb


def _scan_group(a, b, h, row, reverse):
    for d in (1, 2, 4):
        if reverse:
            keep = row < 8 - d
            shift = 8 - d
        else:
            keep = row >= d
            shift = d
        a_s = jnp.where(keep, pltpu.roll(a, shift, 0), 1.0)
        b_s = jnp.where(keep, pltpu.roll(b, shift, 0), 0.0)
        b = a * b_s + b
        a = a * a_s
    return a * h + b


def _lru_scan_kernel(xf, xfp, xfn, xb, xbp, xbn, cw_ref, cb_ref, wgf, bgf, nlaf, wgb, bgb, nlab,
                     hf_ref, hb_ref, af_ref, bf_ref, ab_ref, bb_ref, cf_ref, cbk_ref, *, n_lat_tiles):
    s = pl.program_id(1)
    n_tiles = pl.num_programs(1)

    @pl.when(s == 0)
    def _():
        cf_ref[...] = jnp.zeros(cf_ref.shape, F32)
        cbk_ref[...] = jnp.zeros(cbk_ref.shape, F32)

    t_f, t_b = (m(s) for m in _seq_tile_maps(n_lat_tiles, n_tiles))
    _lru_tile_prep(xf, xfp, xfn, cw_ref, cb_ref, wgf, bgf, nlaf, af_ref, bf_ref, t_f, n_tiles, n_lat_tiles)
    _lru_tile_prep(xb, xbp, xbn, cw_ref, cb_ref, wgb, bgb, nlab, ab_ref, bb_ref, t_b, n_tiles, n_lat_tiles)

    n_groups = SEQ_TILE // 8
    row = lax.broadcasted_iota(jnp.int32, (8, LRU_W), 0)

    def body(gi, carry):
        h_f, h_b = carry
        rf = pl.ds(pl.multiple_of(gi * 8, 8), 8)
        rb = pl.ds(pl.multiple_of((n_groups - 1 - gi) * 8, 8), 8)
        out_f = _scan_group(af_ref[rf, :], bf_ref[rf, :], h_f, row, False)
        out_b = _scan_group(ab_ref[rb, :], bb_ref[rb, :], h_b, row, True)
        hf_ref[rf, :] = out_f
        hb_ref[rb, :] = out_b
        return (jnp.broadcast_to(out_f[7:8, :], (8, LRU_W)), jnp.broadcast_to(out_b[0:1, :], (8, LRU_W)))

    h_f, h_b = lax.fori_loop(0, n_groups, body, (cf_ref[...], cbk_ref[...]))
    cf_ref[...] = h_f
    cbk_ref[...] = h_b


def _lru_scan(p, conv_w, conv_b, w_r, b_r, w_i, b_i, lam, n_lat):
    b, s, _ = p.shape
    n_tiles = s // SEQ_TILE
    nlt = n_lat // SEQ_TILE
    fwd, bwd = _seq_tile_maps(nlt, n_tiles)
    col = P_LRU_X // LRU_W
    specs = []
    for tile_of in (fwd, bwd):
        main, prev, make_next = _halo_specs(LRU_W, col, tile_of)
        specs += [pl.BlockSpec((None, SEQ_TILE, LRU_W), main), pl.BlockSpec((None, HALO, LRU_W), prev),
                  pl.BlockSpec((None, HALO, LRU_W), make_next(n_tiles))]

    def blockdiag(w):
        return jax.scipy.linalg.block_diag(*[w[n] for n in range(LRU_BLOCKS)])

    dir_args = []
    for d in range(2):
        wg = jnp.concatenate([blockdiag(w_r[d]), blockdiag(w_i[d])], axis=1).astype(BF16)
        bg = jnp.concatenate([b_r[d], b_i[d]]).astype(F32).reshape(1, 2 * LRU_W)
        nla = (-LRU_C * jax.nn.softplus(-lam[d].astype(F32))).reshape(1, LRU_W)
        dir_args += [wg, bg, nla]
    cb2 = conv_b.reshape(1, LRU_W)
    out_spec = lambda tile_of: pl.BlockSpec((None, SEQ_TILE, LRU_W), lambda bi, t: (bi, tile_of(t), 0))
    return pl.pallas_call(
        functools.partial(_lru_scan_kernel, n_lat_tiles=nlt),
        grid=(b, n_tiles),
        in_specs=specs + [_full(conv_w), _full(cb2)] + [_full(a) for a in dir_args],
        out_specs=[out_spec(fwd), out_spec(bwd)],
        out_shape=[jax.ShapeDtypeStruct((b, s, LRU_W), F32)] * 2,
        scratch_shapes=[pltpu.VMEM((SEQ_TILE, LRU_W), F32)] * 4 + [pltpu.VMEM((8, LRU_W), F32)] * 2,
        compiler_params=_cparams(("parallel", "arbitrary")),
        name="lru_scan",
    )(p, p, p, p, p, p, conv_w, cb2, *dir_args)


def _post_kernel(of_ref, ob_ref, z_ref, gn_ref, ones_ref, hf_ref, hb_ref, y_ref, ua_ref, ub_ref):
    o = of_ref[...] + ob_ref[...]
    ms = jnp.dot(jnp.concatenate(_split3(o * o), axis=-1), ones_ref[...], preferred_element_type=F32) * (1.0 / GDN_DV)
    z = z_ref[...]
    ua_ref[...] = (o * lax.rsqrt(ms + RMS_EPS) * gn_ref[...] * (z * _sigmoid(z))).astype(ua_ref.dtype)
    y = y_ref[...]
    gelu = 0.5 * y * (1.0 + jnp.tanh(0.7978845608028654 * (y + 0.044715 * (y * y * y))))
    ub_ref[...] = ((hf_ref[...] + hb_ref[...]) * gelu).astype(ub_ref.dtype)


def _post(o_f, o_b, h_f, h_b, p, gdn_norm_g):
    b, s, _ = p.shape
    tm = _pick_tile(s, 544)
    ones3 = jnp.concatenate([_head_block_ones(GDN_HEADS, GDN_DV)] * 3, axis=0)
    gn = jnp.tile(gdn_norm_g.astype(F32), GDN_HEADS).reshape(1, GDN_W)
    tok = pl.BlockSpec((None, tm, GDN_W), lambda bi, i: (bi, i, 0))
    pcol = lambda c: pl.BlockSpec((None, tm, GDN_W), lambda bi, i: (bi, i, c))
    return pl.pallas_call(
        _post_kernel,
        grid=(b, s // tm),
        in_specs=[tok, tok, pcol(P_GDN_Z // GDN_W), _full(gn), _full(ones3), tok, tok, pcol(P_LRU_Y // LRU_W)],
        out_specs=[tok, tok],
        out_shape=[jax.ShapeDtypeStruct((b, s, GDN_W), BF16)] * 2,
        compiler_params=_cparams(("parallel", "parallel")),
        name="gdn_lru_post",
    )(o_f, o_b, p, gn, ones3, h_f, h_b, p)


def _merge_kernel(x_ref, mc_ref, ml_ref, u0_ref, u1_ref, u2_ref, u3_ref, t0_ref, t1_ref, t2_ref, t3_ref,
                  bg_ref, wb_ref, wo_ref, o_ref, *, n_lat):
    merged = None
    for n, (u_ref, t_ref) in enumerate(((u0_ref, t0_ref), (u1_ref, t1_ref), (u2_ref, t2_ref), (u3_ref, t3_ref))):
        gate = _sigmoid(t_ref[...] + bg_ref[n:n + 1, :])
        term = gate * jnp.dot(u_ref[...], wb_ref[n], preferred_element_type=F32)
        merged = term if merged is None else merged + term
    out = jnp.dot(merged.astype(BF16), wo_ref[...], preferred_element_type=F32)
    g1 = _row_select(x_ref.shape[0], pl.program_id(1), n_lat, mc_ref[2:3, :], ml_ref[2:3, :])
    o_ref[...] = x_ref[...] + g1 * out


def _merge(xs, mod_c, mod_l, branches, p, b_gate, w_branch, w_out, n_lat, with_ctx):
    b, s, d = xs.shape
    rows = s if with_ctx else n_lat
    tm = _pick_tile(rows, 544) if with_ctx else SEQ_TILE
    gate_blk = P_GATES // d
    tok = lambda w: pl.BlockSpec((None, tm, w), lambda bi, i: (bi, i, 0))
    in_specs = [tok(d), pl.BlockSpec((8, d), lambda bi, i: (0, 0)), pl.BlockSpec((None, 8, d), lambda bi, i: (bi, 0, 0))]
    in_specs += [tok(BRANCH_W)] * N_BRANCH
    in_specs += [pl.BlockSpec((None, tm, d), functools.partial(lambda bi, i, n: (bi, i, gate_blk + n), n=n))
                 for n in range(N_BRANCH)]
    in_specs += [_full(b_gate), _full(w_branch), _full(w_out)]
    return pl.pallas_call(
        functools.partial(_merge_kernel, n_lat=n_lat),
        grid=(b, rows // tm),
        in_specs=in_specs,
        out_specs=tok(d),
        out_shape=jax.ShapeDtypeStruct((b, rows, d), F32),
        input_output_aliases={0: 0} if with_ctx else {},
        compiler_params=_cparams(("parallel", "parallel")),
        name="merge",
    )(xs, mod_c, mod_l, *branches, p, p, p, p, b_gate, w_branch, w_out)


def _mlp_kernel(x_ref, mc_ref, ml_ref, gn_ref, gf_ref, w1_ref, w2_ref, o_ref, h_ref, acc_ref, *, n_lat, final_norm):
    f = pl.program_id(2)
    tm = x_ref.shape[0]
    i = pl.program_id(1)

    @pl.when(f == 0)
    def _():
        x = x_ref[...]
        y = x * lax.rsqrt(jnp.mean(x * x, axis=-1, keepdims=True) + RMS_EPS)
        y = y * gn_ref[...]
        shift = _row_select(tm, i, n_lat, mc_ref[3:4, :], ml_ref[3:4, :])
        scale = _row_select(tm, i, n_lat, mc_ref[4:5, :], ml_ref[4:5, :])
        h_ref[...] = (y * (1.0 + scale) + shift).astype(BF16)

    a = jnp.maximum(jnp.dot(h_ref[...], w1_ref[...], preferred_element_type=F32), 0.0)
    part = jnp.dot((a * a).astype(BF16), w2_ref[...], preferred_element_type=F32)

    @pl.when(f == 0)
    def _():
        acc_ref[...] = part

    @pl.when(f > 0)
    def _():
        acc_ref[...] += part

    @pl.when(f == pl.num_programs(2) - 1)
    def _():
        g2 = _row_select(tm, i, n_lat, mc_ref[5:6, :], ml_ref[5:6, :])
        y = x_ref[...] + g2 * acc_ref[...]
        if final_norm:
            y = y * lax.rsqrt(jnp.mean(y * y, axis=-1, keepdims=True) + RMS_EPS) * gf_ref[...]
        o_ref[...] = y


def _mlp(xs, mod_c, mod_l, gain, w1, w2, final_gain, n_lat, final_norm):
    b, rows, d = xs.shape
    dff = w1.shape[1]
    tm = _pick_tile(rows, 1088)
    tf = 1024
    row = pl.BlockSpec((1, d), lambda bi, i, f: (0, 0))
    return pl.pallas_call(
        functools.partial(_mlp_kernel, n_lat=n_lat, final_norm=final_norm),
        grid=(b, rows // tm, dff // tf),
        in_specs=[pl.BlockSpec((None, tm, d), lambda bi, i, f: (bi, i, 0)),
                  pl.BlockSpec((8, d), lambda bi, i, f: (0, 0)),
                  pl.BlockSpec((None, 8, d), lambda bi, i, f: (bi, 0, 0)),
                  row, row,
                  pl.BlockSpec((d, tf), lambda bi, i, f: (0, f)),
                  pl.BlockSpec((tf, d), lambda bi, i, f: (f, 0))],
        out_specs=pl.BlockSpec((None, tm, d), lambda bi, i, f: (bi, i, 0)),
        out_shape=jax.ShapeDtypeStruct((b, rows, d), F32),
        scratch_shapes=[pltpu.VMEM((tm, d), BF16), pltpu.VMEM((tm, d), F32)],
        compiler_params=_cparams(("parallel", "parallel", "arbitrary")),
        name="mlp",
    )(xs, mod_c, mod_l, gain, final_gain, w1, w2)


def kernel(x, c, ctx, c_ctx, mod_w, mod_b, norm1_g, norm2_g, w_in, b_gate, gdn_conv_w, gdn_a_log, gdn_dt_bias,
           gdn_norm_g, lru_conv_w, lru_conv_b, lru_w_r, lru_b_r, lru_w_i, lru_b_i, lru_lambda, mla_q_norm_g,
           mla_w_uq, mla_kv_norm_g, mla_w_ukv, na_rpb, w_branch, w_out, mlp_w1, mlp_w2, final_norm_g):
    bsz, n_tok, d = x.shape
    n_ctx = ctx.shape[1]
    depth = w_in.shape[0]
    assert n_ctx % SEQ_TILE == 0 and n_tok % SEQ_TILE == 0 and n_tok % GRID_W == 0
    na_meta, onehot_r, onehot_c = _na_geometry(n_tok // GRID_W)
    na_meta = jnp.asarray(na_meta)
    cos, sin = _rope_tables(n_tok, n_ctx)

    n_rows = -(-(bsz + 1) // 8) * 8
    cc = jnp.zeros((n_rows, d), F32).at[:bsz].set(c).at[bsz].set(c_ctx)
    final_gain = final_norm_g.reshape(1, d)

    xs = jnp.concatenate([x, ctx], axis=1)
    for l in range(depth):
        need_ctx = l < depth - 1
        mod = _modulation(cc, mod_w[l], mod_b[l]).reshape(n_rows, N_MOD, d)
        pad = jnp.zeros((8 - N_MOD, d), F32)
        mod_c = jnp.concatenate([mod[bsz], pad], axis=0)
        mod_l = jnp.concatenate([mod[:bsz], jnp.broadcast_to(pad, (bsz, 8 - N_MOD, d))], axis=1)

        w_in_l = _arrange_w_in(w_in[l])
        wq, wk, wv, place = _arrange_mla(mla_w_uq[l], mla_w_ukv[l])
        gq = mla_q_norm_g[l].reshape(1, -1)
        gkv = mla_kv_norm_g[l].reshape(1, -1)
        g1n = norm1_g[l].reshape(1, d)
        g2n = norm2_g[l].reshape(1, d)
        wb = w_branch[l].astype(BF16)
        wo = w_out[l].astype(BF16)
        w1 = mlp_w1[l].astype(BF16)
        w2 = mlp_w2[l].astype(BF16)

        p32, p16 = _inproj(xs, mod_c, mod_l, g1n, w_in_l, n_tok)

        gq_, gk_, gv_, gbeta, gg = _gdn_prep(p32, gdn_conv_w[l], gdn_a_log[l], gdn_dt_bias[l], n_tok)
        o_f, o_b = _gdn_scan(gq_, gk_, gv_, gbeta, gg, n_tok)
        h_f, h_b = _lru_scan(p32, lru_conv_w[l], lru_conv_b[l], lru_w_r[l], lru_b_r[l], lru_w_i[l], lru_b_i[l],
                             lru_lambda[l], n_tok)
        ua, ub = _post(o_f, o_b, h_f, h_b, p32, gdn_norm_g[l])

        mq, mk, mv = _mla_prep(p32, p16, cos, sin, gq, gkv, wq, wk, wv, place)
        uc = _mla_flash(mq, mk, mv, n_tok)
        if need_ctx:
            uc = _mla_flash(mq, mk, mv, n_tok, prev=uc)
        ud = _na_attention(p16, na_meta, _na_bias_table(na_rpb[l], onehot_r, onehot_c), n_tok, need_ctx)

        xs = _merge(xs, mod_c, mod_l, (ua, ub, uc, ud), p16, b_gate[l], wb, wo, n_tok, need_ctx)
        xs = _mlp(xs, mod_c, mod_l, g2n, w1, w2, final_gain, n_tok, l == depth - 1)
    return xs
```

```python
import functools

import jax
import jax.numpy as jnp
import numpy as np
from jax import lax
from jax.experimental import pallas as pl
from jax.experimental.pallas import tpu as pltpu

F32 = jnp.float32
BF16 = jnp.bfloat16

GRID_W = 64
N_MOD = 6
RMS_EPS = 1e-6
GDN_HEADS = 4
GDN_DK = 64
GDN_DV = 64
GDN_CHUNK = 64
GDN_W = GDN_HEADS * GDN_DV
GDN_CONV = 4
LRU_W = 256
LRU_BLOCKS = 4
LRU_BLOCK_W = LRU_W // LRU_BLOCKS
LRU_CONV = 4
LRU_C = 8.0
MLA_HEADS = 4
MLA_Q_RANK = 256
MLA_KV_RANK = 128
MLA_NOPE = 64
MLA_ROPE = 32
MLA_V = 64
MLA_SLOT = 128
ROPE_BASE = 10000.0
LOG2_E = 1.4426950408889634
NA_HEADS = 4
NA_DH = 64
NA_W = NA_HEADS * NA_DH
NA_WIN_ROWS = 8
NA_WIN_COLS = 16
N_BRANCH = 4
BRANCH_W = 256

SEQ_TILE = 256
HALO = 8

_REF_COLS = {}
_off = 0
for _name, _w in (('gdn_qkv', 3 * GDN_W), ('gdn_z', GDN_W), ('gdn_beta', 2 * GDN_HEADS), ('gdn_a', 2 * GDN_HEADS),
                  ('lru_x', LRU_W), ('lru_y', LRU_W), ('mla_q', MLA_Q_RANK), ('mla_kv', MLA_KV_RANK),
                  ('mla_kr', MLA_ROPE), ('na_qkv', 3 * NA_W)):
    _REF_COLS[_name] = (_off, _w)
    _off += _w
N_MIX_COLS = _off

P_GDN_QKV = 0
P_GDN_Z = 768
P_LRU_X = 1024
P_LRU_Y = 1280
P_MLA_KV = 1536
P_MLA_KR = 1664
P_GDN_BA = 1792
P32_COLS = 2048
P_MLA_Q = 0
P_NA_QKV = 256
P_GATES = 1024
P16_COLS = P_GATES + N_BRANCH * 1024
PROJ_TILE = 1024

VMEM_LIMIT = 52 * 1024 * 1024


def _cparams(sem):
    return pltpu.CompilerParams(dimension_semantics=sem, vmem_limit_bytes=VMEM_LIMIT)


def _pick_tile(n, cap):
    best = 8
    for t in range(8, min(n, cap) + 1, 8):
        if n % t == 0:
            best = t
    return best


def _full(a):
    return pl.BlockSpec(a.shape, lambda *_: (0,) * a.ndim)


def _split3(x):
    hi = x.astype(BF16)
    r = x - hi.astype(F32)
    mid = r.astype(BF16)
    lo = (r - mid.astype(F32)).astype(BF16)
    return hi, mid, lo


def _sigmoid(x):
    return 0.5 * jnp.tanh(0.5 * x) + 0.5


def _arrange_w_in(w_in):
    d = w_in.shape[0]
    pieces, pos = [], 0

    def put(dst, block):
        nonlocal pos
        if dst > pos:
            pieces.append(jnp.zeros((d, dst - pos), w_in.dtype))
        pieces.append(block)
        pos = dst + block.shape[1]

    ref = lambda name: w_in[:, _REF_COLS[name][0]:_REF_COLS[name][0] + _REF_COLS[name][1]]
    for name, dst in (('gdn_qkv', P_GDN_QKV), ('gdn_z', P_GDN_Z), ('lru_x', P_LRU_X), ('lru_y', P_LRU_Y),
                      ('mla_kv', P_MLA_KV), ('mla_kr', P_MLA_KR), ('gdn_beta', P_GDN_BA),
                      ('gdn_a', P_GDN_BA + 2 * GDN_HEADS)):
        put(dst, ref(name))
    put(P32_COLS + P_MLA_Q, ref('mla_q'))
    put(P32_COLS + P_NA_QKV, ref('na_qkv'))
    put(P32_COLS + P_GATES, w_in[:, N_MIX_COLS:])
    assert pos == P32_COLS + P16_COLS
    return jnp.concatenate(pieces, axis=1).astype(BF16)


def _rope_perm():
    q = MLA_ROPE // 4
    src = np.zeros(MLA_ROPE, np.int32)
    sign = np.zeros(MLA_ROPE, np.float32)
    for base in (0, 2 * q):
        for d in range(q):
            src[base + d] = base + d + q
            sign[base + d] = -1.0
            src[base + q + d] = base + d
            sign[base + q + d] = 1.0
    return src, sign


def _arrange_mla(w_uq, w_ukv):
    src, sign = _rope_perm()
    hq = MLA_NOPE + MLA_ROPE
    wq = jnp.zeros((MLA_Q_RANK, 2 * MLA_HEADS * MLA_SLOT), F32)
    wk = jnp.zeros((MLA_KV_RANK, MLA_HEADS * MLA_SLOT), F32)
    wv = jnp.zeros((MLA_KV_RANK, MLA_HEADS * MLA_V), F32)
    place = np.zeros((2 * MLA_SLOT, 2 * MLA_HEADS * MLA_SLOT), np.float32)
    rot_off = MLA_HEADS * MLA_SLOT
    for h in range(MLA_HEADS):
        nope = w_uq[:, h * hq:h * hq + MLA_NOPE]
        pe = w_uq[:, h * hq + MLA_NOPE:(h + 1) * hq]
        s = h * MLA_SLOT
        wq = wq.at[:, s:s + MLA_NOPE].set(nope)
        wq = wq.at[:, s + MLA_NOPE:s + MLA_NOPE + MLA_ROPE].set(pe)
        wq = wq.at[:, rot_off + s + MLA_NOPE:rot_off + s + MLA_NOPE + MLA_ROPE].set(pe[:, src] * sign)
        wk = wk.at[:, s:s + MLA_NOPE].set(w_ukv[:, h * 128:h * 128 + MLA_NOPE])
        wv = wv.at[:, h * MLA_V:(h + 1) * MLA_V].set(w_ukv[:, h * 128 + MLA_NOPE:(h + 1) * 128])
        for d in range(MLA_ROPE):
            for half in (0, MLA_SLOT):
                place[half + d, s + MLA_NOPE + d] = 1.0
                place[half + src[d], rot_off + s + MLA_NOPE + d] = sign[d]
    return wq.astype(BF16), wk.astype(BF16), wv.astype(BF16), jnp.asarray(place, BF16)


def _rope_tables(n_tok, n_ctx):
    cos = np.ones((n_tok + n_ctx, MLA_SLOT), np.float32)
    sin = np.zeros((n_tok + n_ctx, MLA_SLOT), np.float32)
    t = np.arange(n_tok)
    row = (t // GRID_W).astype(np.float32)
    col = (t % GRID_W).astype(np.float32)
    n_freq = MLA_ROPE // 4
    inv = (ROPE_BASE ** (-np.arange(n_freq, dtype=np.float32) / n_freq)).astype(np.float32)
    ar = row[:, None] * inv
    ac = col[:, None] * inv
    ang = np.concatenate([ar, ar, ac, ac], axis=-1).astype(np.float32)
    cos[:n_tok, MLA_NOPE:MLA_NOPE + MLA_ROPE] = np.cos(ang)
    sin[:n_tok, MLA_NOPE:MLA_NOPE + MLA_ROPE] = np.sin(ang)
    return jnp.asarray(cos), jnp.asarray(sin)


def _head_block_ones(n_heads, width):
    m = np.kron(np.eye(n_heads, dtype=np.float32), np.ones((width, width), np.float32))
    return jnp.asarray(m, BF16)


def _mod_kernel(c_ref, w_ref, b_ref, o_ref):
    c = c_ref[...]
    s = c * _sigmoid(c)
    o_ref[...] = jnp.dot(s, w_ref[...], preferred_element_type=F32) + b_ref[...]


def _modulation(cc, mod_w, mod_b):
    r, d = cc.shape
    n = mod_w.shape[1]
    tn = 1024
    return pl.pallas_call(
        _mod_kernel,
        grid=(n // tn,),
        in_specs=[pl.BlockSpec((r, d), lambda j: (0, 0)),
                  pl.BlockSpec((d, tn), lambda j: (0, j)),
                  pl.BlockSpec((1, tn), lambda j: (0, j))],
        out_specs=pl.BlockSpec((r, tn), lambda j: (0, j)),
        out_shape=jax.ShapeDtypeStruct((r, n), F32),
        compiler_params=_cparams(("arbitrary",)),
        name="modulation",
    )(cc, mod_w, mod_b.reshape(1, n))


def _row_select(tile_rows, tile_idx, n_lat, ctx_vec, lat_vec):
    row = tile_idx * tile_rows + lax.broadcasted_iota(jnp.int32, (tile_rows, 1), 0)
    return jnp.where(row < n_lat, lat_vec, ctx_vec)


def _inproj_kernel(x_ref, mc_ref, ml_ref, g_ref, w_ref, o32_ref, o16_ref, h_ref, *, n_lat):
    j = pl.program_id(2)
    n32 = P32_COLS // PROJ_TILE

    @pl.when(j == 0)
    def _():
        tm = x_ref.shape[0]
        i = pl.program_id(1)
        x = x_ref[...]
        y = x * lax.rsqrt(jnp.mean(x * x, axis=-1, keepdims=True) + RMS_EPS)
        y = y * g_ref[...]
        shift = _row_select(tm, i, n_lat, mc_ref[0:1, :], ml_ref[0:1, :])
        scale = _row_select(tm, i, n_lat, mc_ref[1:2, :], ml_ref[1:2, :])
        h_ref[...] = (y * (1.0 + scale) + shift).astype(BF16)

    acc = jnp.dot(h_ref[...], w_ref[...], preferred_element_type=F32)

    @pl.when(j < n32)
    def _():
        o32_ref[...] = acc

    @pl.when(j >= n32)
    def _():
        o16_ref[...] = acc.astype(BF16)


def _inproj(xs, mod_c, mod_l, gain, w, n_lat):
    b, s, d = xs.shape
    tm = _pick_tile(s, 1088)
    tn = PROJ_TILE
    n32 = P32_COLS // tn
    return pl.pallas_call(
        functools.partial(_inproj_kernel, n_lat=n_lat),
        grid=(b, s // tm, (P32_COLS + P16_COLS) // tn),
        in_specs=[pl.BlockSpec((None, tm, d), lambda bi, i, j: (bi, i, 0)),
                  pl.BlockSpec((8, d), lambda bi, i, j: (0, 0)),
                  pl.BlockSpec((None, 8, d), lambda bi, i, j: (bi, 0, 0)),
                  pl.BlockSpec((1, d), lambda bi, i, j: (0, 0)),
                  pl.BlockSpec((d, tn), lambda bi, i, j: (0, j))],
        out_specs=[pl.BlockSpec((None, tm, tn), lambda bi, i, j: (bi, i, jnp.minimum(j, n32 - 1))),
                   pl.BlockSpec((None, tm, tn), lambda bi, i, j: (bi, i, jnp.maximum(j - n32, 0)))],
        out_shape=[jax.ShapeDtypeStruct((b, s, P32_COLS), F32), jax.ShapeDtypeStruct((b, s, P16_COLS), BF16)],
        scratch_shapes=[pltpu.VMEM((tm, d), BF16)],
        compiler_params=_cparams(("parallel", "parallel", "arbitrary")),
        name="inproj",
    )(xs, mod_c, mod_l, gain, w)


def _mla_prep_kernel(ql_ref, kv_ref, cos_ref, sin_ref, gq_ref, gkv_ref, wq_ref, wk_ref, wv_ref, pl_ref,
                     q_ref, k_ref, v_ref, *, scale):
    nslot = MLA_HEADS * MLA_SLOT
    cos = jnp.concatenate([cos_ref[...]] * MLA_HEADS, axis=-1)
    sin = jnp.concatenate([sin_ref[...]] * MLA_HEADS, axis=-1)

    ql = ql_ref[...].astype(F32)
    qn = ql * lax.rsqrt(jnp.mean(ql * ql, axis=-1, keepdims=True) + RMS_EPS) * gq_ref[...]
    q2 = jnp.dot(qn.astype(BF16), wq_ref[...], preferred_element_type=F32)
    q = (q2[:, :nslot] * cos + q2[:, nslot:] * sin) * scale
    q_ref[...] = q.astype(BF16)

    kvkr = kv_ref[...]
    kvl = kvkr[:, :MLA_KV_RANK]
    kr = kvkr[:, MLA_KV_RANK:]
    kvn = (kvl * lax.rsqrt(jnp.mean(kvl * kvl, axis=-1, keepdims=True) + RMS_EPS) * gkv_ref[...]).astype(BF16)
    kn = jnp.dot(kvn, wk_ref[...], preferred_element_type=F32)
    v_ref[...] = jnp.dot(kvn, wv_ref[...], preferred_element_type=F32).astype(BF16)
    kr_hi = kr.astype(BF16)
    kr_lo = (kr - kr_hi.astype(F32)).astype(BF16)
    kr2 = jnp.dot(jnp.concatenate([kr_hi, kr_lo], axis=-1), pl_ref[...], preferred_element_type=F32)
    k_ref[...] = (kn + kr2[:, :nslot] * cos + kr2[:, nslot:] * sin).astype(BF16)


def _mla_prep(p32, p16, cos, sin, gq, gkv, wq, wk, wv, place):
    b, s, _ = p32.shape
    tm = _pick_tile(s, 544)
    nslot = MLA_HEADS * MLA_SLOT
    scale = (MLA_NOPE + MLA_ROPE) ** -0.5 * LOG2_E
    return pl.pallas_call(
        functools.partial(_mla_prep_kernel, scale=scale),
        grid=(b, s // tm),
        in_specs=[pl.BlockSpec((None, tm, MLA_Q_RANK), lambda bi, i: (bi, i, P_MLA_Q // MLA_Q_RANK)),
                  pl.BlockSpec((None, tm, 2 * MLA_SLOT), lambda bi, i: (bi, i, P_MLA_KV // (2 * MLA_SLOT))),
                  pl.BlockSpec((tm, MLA_SLOT), lambda bi, i: (i, 0)),
                  pl.BlockSpec((tm, MLA_SLOT), lambda bi, i: (i, 0)),
                  _full(gq), _full(gkv), _full(wq), _full(wk), _full(wv), _full(place)],
        out_specs=[pl.BlockSpec((None, tm, nslot), lambda bi, i: (bi, i, 0)),
                   pl.BlockSpec((None, tm, nslot), lambda bi, i: (bi, i, 0)),
                   pl.BlockSpec((None, tm, MLA_HEADS * MLA_V), lambda bi, i: (bi, i, 0))],
        out_shape=[jax.ShapeDtypeStruct((b, s, nslot), BF16),
                   jax.ShapeDtypeStruct((b, s, nslot), BF16),
                   jax.ShapeDtypeStruct((b, s, MLA_HEADS * MLA_V), BF16)],
        compiler_params=_cparams(("parallel", "parallel")),
        name="mla_prep",
    )(p16, p32, cos, sin, gq, gkv, wq, wk, wv, place)


def _flash_update(h, q, k, v, m_ref, l_ref, acc_ref):
    s = lax.dot_general(q, k, (((1,), (1,)), ((), ())), preferred_element_type=F32)
    m_prev = m_ref[h]
    m_new = jnp.maximum(m_prev, jnp.max(s, axis=-1, keepdims=True))
    alpha = jnp.exp2(m_prev - m_new)
    p = jnp.exp2(s - jnp.concatenate([m_new] * (s.shape[1] // 128), axis=-1))
    l_ref[h] = alpha * l_ref[h] + jnp.sum(p, axis=-1, keepdims=True)
    acc_ref[h] = alpha * acc_ref[h] + jnp.dot(p.astype(BF16), v, preferred_element_type=F32)
    m_ref[h] = m_new


def _mla_flash_kernel(*refs, tk, aliased):
    if aliased:
        q_ref, k_ref, v_ref, _, o_ref, m_ref, l_ref, acc_ref = refs
    else:
        q_ref, k_ref, v_ref, o_ref, m_ref, l_ref, acc_ref = refs
    tq = q_ref.shape[0]
    nk = k_ref.shape[0]
    n_loop = nk // tk
    m_ref[...] = jnp.full(m_ref.shape, -jnp.inf, F32)
    l_ref[...] = jnp.zeros(l_ref.shape, F32)
    acc_ref[...] = jnp.zeros(acc_ref.shape, F32)

    def chunk(rows):
        for h in range(MLA_HEADS):
            hs = slice(h * MLA_SLOT, (h + 1) * MLA_SLOT)
            vs = slice((h // 2) * 128, (h // 2) * 128 + 128)
            _flash_update(h, q_ref[:, hs], k_ref[rows, hs], v_ref[rows, vs], m_ref, l_ref, acc_ref)

    if n_loop:
        def body(j, carry):
            chunk(pl.ds(pl.multiple_of(j * tk, tk), tk))
            return carry
        lax.fori_loop(0, n_loop, body, 0)
    if nk > n_loop * tk:
        chunk(slice(n_loop * tk, nk))

    lane = lax.broadcasted_iota(jnp.int32, (tq, 128), 1)
    outs = []
    for pair in range(MLA_HEADS // 2):
        o0 = acc_ref[2 * pair] / l_ref[2 * pair]
        o1 = acc_ref[2 * pair + 1] / l_ref[2 * pair + 1]
        outs.append(jnp.where(lane < MLA_V, o0, o1))
    o_ref[...] = jnp.concatenate(outs, axis=-1).astype(o_ref.dtype)


def _mla_flash(q, k, v, n_lat, prev=None):
    b, s, nslot = q.shape
    n_ctx = s - n_lat
    nv = MLA_HEADS * MLA_V
    tk = 512
    if prev is None:
        tq = _pick_tile(n_lat, 512)
        q_off, n_q, kv_rows, kv_blk = 0, n_lat // tq, s, 0
    else:
        assert n_lat % n_ctx == 0
        tq = _pick_tile(n_ctx, 256)
        q_off, n_q, kv_rows, kv_blk = n_lat // tq, n_ctx // tq, n_ctx, n_lat // n_ctx
    in_specs = [pl.BlockSpec((None, tq, nslot), lambda bi, i: (bi, i + q_off, 0)),
                pl.BlockSpec((None, kv_rows, nslot), lambda bi, i: (bi, kv_blk, 0)),
                pl.BlockSpec((None, kv_rows, nv), lambda bi, i: (bi, kv_blk, 0))]
    args = [q, k, v]
    aliases = {}
    if prev is not None:
        in_specs.append(pl.BlockSpec(memory_space=pl.ANY))
        args.append(prev)
        aliases = {3: 0}
    return pl.pallas_call(
        functools.partial(_mla_flash_kernel, tk=tk, aliased=prev is not None),
        grid=(b, n_q),
        in_specs=in_specs,
        out_specs=pl.BlockSpec((None, tq, nv), lambda bi, i: (bi, i + q_off, 0)),
        out_shape=jax.ShapeDtypeStruct((b, s, nv), BF16),
        scratch_shapes=[pltpu.VMEM((MLA_HEADS, tq, 128), F32),
                        pltpu.VMEM((MLA_HEADS, tq, 128), F32),
                        pltpu.VMEM((MLA_HEADS, tq, 128), F32)],
        input_output_aliases=aliases,
        compiler_params=_cparams(("parallel", "arbitrary")),
        name="mla_flash",
    )(*args)


NA_QROWS = 4
NA_SLAB = NA_QROWS + NA_WIN_ROWS


def _na_geometry(rows):
    assert rows % NA_QROWS == 0 and rows >= NA_SLAB
    nblk = rows // NA_QROWS
    qc = np.arange(GRID_W)
    cs = np.clip(qc - NA_WIN_COLS // 2, 0, GRID_W - NA_WIN_COLS)
    col_valid = (qc[None, :] >= cs[:, None]) & (qc[None, :] < cs[:, None] + NA_WIN_COLS)
    rel_c = np.clip(qc[None, :] - qc[:, None] + NA_WIN_COLS - 1, 0, 2 * NA_WIN_COLS - 2)
    onehot_c = (rel_c[None] == np.arange(2 * NA_WIN_COLS - 1)[:, None, None]) & col_valid[None]
    patterns, cls, starts = {}, [], []
    for i in range(nblk):
        r0 = i * NA_QROWS
        start = int(np.clip(r0 - NA_WIN_ROWS // 2, 0, rows - NA_SLAB))
        qr = r0 + np.arange(NA_QROWS)
        rs = np.clip(qr - NA_WIN_ROWS // 2, 0, rows - NA_WIN_ROWS)
        key = (start - r0,) + tuple((rs - r0).tolist())
        if key not in patterns:
            kr = start + np.arange(NA_SLAB)
            row_valid = (kr[None, :] >= rs[:, None]) & (kr[None, :] < rs[:, None] + NA_WIN_ROWS)
            rel_r = np.clip(kr[None, :] - qr[:, None] + NA_WIN_ROWS - 1, 0, 2 * NA_WIN_ROWS - 2)
            onehot_r = (rel_r[..., None] == np.arange(2 * NA_WIN_ROWS - 1)) & row_valid[..., None]
            patterns[key] = (len(patterns), onehot_r)
        cls.append(patterns[key][0])
        starts.append(start)
    ordered = sorted(patterns.values(), key=lambda z: z[0])
    onehot_r = np.stack([z[1] for z in ordered]).astype(np.float32)
    meta = np.stack([np.asarray(cls, np.int32), np.asarray(starts, np.int32)])
    return meta, onehot_r, onehot_c.astype(np.float32)


def _na_bias_table(rpb, onehot_r, onehot_c):
    hi = lax.Precision.HIGHEST
    bc = jnp.einsum('hrc,cqk->hrqk', rpb.astype(F32), onehot_c, precision=hi)
    tab = jnp.einsum('pabr,hrqk->phaqbk', onehot_r, bc, precision=hi)
    valid = jnp.einsum('pabr,cqk->paqbk', onehot_r, onehot_c, precision=hi) > 0.5
    tab = jnp.where(valid[:, None], tab, -jnp.inf)
    p, h = tab.shape[:2]
    return tab.reshape(p, h, NA_QROWS * GRID_W, NA_SLAB * GRID_W)


def _na_attend(q_pair, lane, h, k_parts, v_parts, bias):
    in_head = (lane < NA_DH) if h % 2 == 0 else (lane >= NA_DH)
    qm = jnp.where(in_head, q_pair * (NA_DH ** -0.5), 0.0).astype(BF16)
    scores = [lax.dot_general(qm, k, (((1,), (1,)), ((), ())), preferred_element_type=F32) for k in k_parts]
    if bias is not None:
        scores[0] = scores[0] + bias
    m = scores[0].max(axis=-1, keepdims=True)
    for s in scores[1:]:
        m = jnp.maximum(m, s.max(axis=-1, keepdims=True))
    den = None
    out = None
    for s, v in zip(scores, v_parts):
        p = jnp.exp(s - m)
        ps = p.sum(axis=-1, keepdims=True)
        po = jnp.dot(p.astype(BF16), v, preferred_element_type=F32)
        den = ps if den is None else den + ps
        out = po if out is None else out + po
    return out / den


def _na_kernel(meta_ref, q_ref, k_ref, v_ref, bias_ref, o_ref, *, n_lat, with_ctx):
    i = pl.program_id(1)
    nq = q_ref.shape[0]
    n_lat_tiles = n_lat // nq
    n_all = k_ref.shape[0]
    lane = lax.broadcasted_iota(jnp.int32, (nq, 128), 1)

    def run(windowed):
        outs = []
        for pair in range(NA_HEADS // 2):
            ls = slice(pair * 128, (pair + 1) * 128)
            q_pair = q_ref[:, ls]
            k_ctx = k_ref[n_lat:n_all, ls].astype(BF16)
            v_ctx = v_ref[n_lat:n_all, ls].astype(BF16)
            if windowed:
                start = pl.multiple_of(meta_ref[1, jnp.minimum(i, n_lat_tiles - 1)] * GRID_W, GRID_W)
                rows = pl.ds(start, NA_SLAB * GRID_W)
                ks = (k_ref[rows, ls].astype(BF16), k_ctx)
                vs = (v_ref[rows, ls].astype(BF16), v_ctx)
            else:
                ks, vs = (k_ctx,), (v_ctx,)
            o = [_na_attend(q_pair, lane, 2 * pair + e, ks, vs, bias_ref[2 * pair + e] if windowed else None)
                 for e in range(2)]
            outs.append(jnp.where(lane < NA_DH, o[0], o[1]))
        o_ref[...] = jnp.concatenate(outs, axis=-1).astype(o_ref.dtype)

    if with_ctx:
        pl.when(i < n_lat_tiles)(lambda: run(True))
        pl.when(i >= n_lat_tiles)(lambda: run(False))
    else:
        run(True)


def _na_attention(p, meta, table, n_lat, with_ctx_queries):
    b, s, _ = p.shape
    nq = NA_QROWS * GRID_W
    nk = NA_SLAB * GRID_W
    assert (s - n_lat) % nq == 0
    nlt = n_lat // nq
    qb = P_NA_QKV // NA_W
    grid_spec = pltpu.PrefetchScalarGridSpec(
        num_scalar_prefetch=1,
        grid=(b, s // nq if with_ctx_queries else nlt),
        in_specs=[pl.BlockSpec((None, nq, NA_W), lambda bi, i, m: (bi, i, qb)),
                  pl.BlockSpec((None, s, NA_W), lambda bi, i, m: (bi, 0, qb + 1)),
                  pl.BlockSpec((None, s, NA_W), lambda bi, i, m: (bi, 0, qb + 2)),
                  pl.BlockSpec((None, NA_HEADS, nq, nk),
                               lambda bi, i, m: (m[0, jnp.minimum(i, nlt - 1)], 0, 0, 0))],
        out_specs=pl.BlockSpec((None, nq, NA_W), lambda bi, i, m: (bi, i, 0)))
    return pl.pallas_call(
        functools.partial(_na_kernel, n_lat=n_lat, with_ctx=with_ctx_queries),
        grid_spec=grid_spec,
        out_shape=jax.ShapeDtypeStruct((b, s, NA_W), BF16),
        compiler_params=_cparams(("parallel", "arbitrary")),
        name="na_attention",
    )(meta, p, p, p, table)


def _tile_conv(x, prev, nxt, w_ref, tile_idx, n_tiles, n_lat_tiles):
    r = x.shape[0]
    width = w_ref.shape[0]
    left = width // 2
    has_prev = jnp.logical_and(tile_idx != 0, tile_idx != n_lat_tiles)
    has_next = jnp.logical_and(tile_idx != n_tiles - 1, tile_idx != n_lat_tiles - 1)
    prev = jnp.where(has_prev, prev, 0.0)
    nxt = jnp.where(has_next, nxt, 0.0)
    xe = jnp.concatenate([prev, x, nxt], axis=0)
    acc = None
    for j in range(width):
        o = HALO - left + j
        term = xe[o:o + r, :] * w_ref[j:j + 1, :]
        acc = term if acc is None else acc + term
    return acc


def _halo_specs(width, col_block, tile_of):
    per = SEQ_TILE // HALO

    def main(bi, s, *_):
        return (bi, tile_of(s), col_block)

    def prev(bi, s, *_):
        return (bi, jnp.maximum(tile_of(s) * per - 1, 0), col_block)

    def make_next(n_tiles):
        def nxt(bi, s, *_):
            return (bi, jnp.minimum((tile_of(s) + 1) * per, n_tiles * per - 1), col_block)
        return nxt

    return main, prev, make_next


def _gdn_prep_kernel(x_ref, xp_ref, xn_ref, ba_ref, cw_ref, ones_ref, exp_ref, alog_ref, dtb_ref,
                     q_ref, k_ref, v_ref, beta_ref, g_ref, *, n_lat_tiles):
    i = pl.program_id(1)
    y = _tile_conv(x_ref[...], xp_ref[...], xn_ref[...], cw_ref, i, pl.num_programs(1), n_lat_tiles)
    y = y * _sigmoid(y)
    q = y[:, :GDN_W]
    k = y[:, GDN_W:2 * GDN_W]
    v_ref[...] = y[:, 2 * GDN_W:]

    def head_norm(u):
        parts = jnp.concatenate(_split3(u * u), axis=-1)
        ss = jnp.dot(parts, ones_ref[...], preferred_element_type=F32)
        return u * lax.rsqrt(ss + RMS_EPS)

    q_ref[...] = head_norm(q) * (GDN_DK ** -0.5)
    k_ref[...] = head_norm(k)

    logits = jnp.dot(jnp.concatenate(_split3(ba_ref[...]), axis=-1), exp_ref[...], preferred_element_type=F32)
    half = 2 * GDN_W
    beta_ref[...] = _sigmoid(logits[:, :half])
    a = logits[:, half:] + dtb_ref[...]
    softplus = jnp.maximum(a, 0.0) + jnp.log1p(jnp.exp(-jnp.abs(a)))
    g_ref[...] = -jnp.exp(alog_ref[...]) * softplus


def _gdn_prep(p, conv_w, a_log, dt_bias, n_lat):
    b, s, _ = p.shape
    n_tiles = s // SEQ_TILE
    main, prev, make_next = _halo_specs(3 * GDN_W, 0, lambda t: t)
    ones3 = jnp.concatenate([_head_block_ones(GDN_HEADS, GDN_DK)] * 3, axis=0)
    expand = np.zeros((128, 4 * GDN_W), np.float32)
    for kind in range(2):
        for d in range(2):
            for h in range(GDN_HEADS):
                c0 = kind * 2 * GDN_W + d * GDN_W + h * GDN_DV
                expand[kind * 2 * GDN_HEADS + d * GDN_HEADS + h, c0:c0 + GDN_DV] = 1.0
    expand3 = jnp.asarray(np.concatenate([expand] * 3, axis=0), BF16)
    alog_e = jnp.repeat(a_log.astype(F32).reshape(-1), GDN_DV).reshape(1, 2 * GDN_W)
    dtb_e = jnp.repeat(dt_bias.astype(F32).reshape(-1), GDN_DV).reshape(1, 2 * GDN_W)
    tok = lambda w: pl.BlockSpec((None, SEQ_TILE, w), lambda bi, i: (bi, i, 0))
    return pl.pallas_call(
        functools.partial(_gdn_prep_kernel, n_lat_tiles=n_lat // SEQ_TILE),
        grid=(b, n_tiles),
        in_specs=[pl.BlockSpec((None, SEQ_TILE, 3 * GDN_W), main),
                  pl.BlockSpec((None, HALO, 3 * GDN_W), prev),
                  pl.BlockSpec((None, HALO, 3 * GDN_W), make_next(n_tiles)),
                  pl.BlockSpec((None, SEQ_TILE, 128), lambda bi, i: (bi, i, P_GDN_BA // 128)),
                  _full(conv_w), _full(ones3), _full(expand3), _full(alog_e), _full(dtb_e)],
        out_specs=[tok(GDN_W), tok(GDN_W), tok(GDN_W), tok(2 * GDN_W), tok(2 * GDN_W)],
        out_shape=[jax.ShapeDtypeStruct((b, s, GDN_W), F32)] * 3 + [jax.ShapeDtypeStruct((b, s, 2 * GDN_W), F32)] * 2,
        compiler_params=_cparams(("parallel", "parallel")),
        name="gdn_prep",
    )(p, p, p, p, conv_w, ones3, expand3, alog_e, dtb_e)


GDN_PAIR = 2 * GDN_DK


def _gdn_masks():
    c, w = GDN_CHUNK, GDN_W
    r2, c2 = np.arange(GDN_PAIR)[:, None], np.arange(GDN_PAIR)[None, :]
    bd = ((r2 // c) == (c2 // c)).astype(np.float32)
    i = np.arange(c)[:, None]
    j = (np.arange(w) % c)[None, :]
    level = np.zeros((c, w), np.int32)
    for bit in range(6):
        level += ((i ^ j) >= (1 << bit)).astype(np.int32)
    lvl = np.stack([(level == m).astype(np.float32) for m in range(7)])
    dirm = np.stack([np.stack([(j <= i), (j < i)]), np.stack([(j >= i), (j > i)])]).astype(np.float32)
    tj = (np.arange(3 * c) % c)[None, :]
    tri = np.stack([(tj <= i), (tj >= i)]).astype(np.float32)
    return jnp.asarray(bd, BF16), jnp.asarray(lvl), jnp.asarray(dirm), jnp.asarray(tri, BF16)


def _heads_mm(x, y, bd, transpose_rhs=False):
    xb = x.astype(BF16)
    yb = y.astype(BF16)
    outs = []
    for pair in range(GDN_W // GDN_PAIR):
        ls = slice(pair * GDN_PAIR, (pair + 1) * GDN_PAIR)
        w = jnp.concatenate([yb[:, ls], yb[:, ls]], axis=0) * bd
        dims = (((1,), (1,)), ((), ())) if transpose_rhs else (((1,), (0,)), ((), ()))
        outs.append(lax.dot_general(xb[:, ls], w, dims, preferred_element_type=F32))
    return jnp.concatenate(outs, axis=1)


def _gdn_intra(probs, bd, lvl_ref, dirm_ref, tri_ref):
    c = GDN_CHUNK
    n = len(probs)
    eye = lvl_ref[0]
    gc, g_last, decay, gram = [], [], [], []
    for q, k, v, beta, g, rev in probs:
        d = 1 if rev else 0
        gcp = jnp.dot(tri_ref[d], jnp.concatenate(_split3(g), axis=0), preferred_element_type=F32)
        gc.append(gcp)
        g_last.append(gcp[0:1, :] if rev else gcp[c - 1:c, :])
        gc_row = jnp.sum(gcp * eye, axis=0, keepdims=True)
        decay.append(dirm_ref[d, 0] * jnp.exp(jnp.minimum(gcp - gc_row, 0.0)))
        gram.append(_heads_mm(jnp.concatenate([k, q], axis=0), k, bd, transpose_rhs=True))
    lower = [dirm_ref[1 if p[5] else 0, 1] * p[3] * gram[x][:c] * decay[x] for x, p in enumerate(probs)]
    a_intra = [gram[x][c:] * decay[x] for x in range(n)]
    t = [eye - lower[x] * lvl_ref[1] for x in range(n)]
    for lev in range(2, 7):
        y = [_heads_mm(t[x], lower[x] * lvl_ref[lev], bd) for x in range(n)]
        z = [_heads_mm(y[x], t[x], bd) for x in range(n)]
        t = [t[x] - z[x] for x in range(n)]
    e_gc = [jnp.exp(gc[x]) for x in range(n)]
    u = [_heads_mm(t[x], p[2] * p[3], bd) for x, p in enumerate(probs)]
    w = [_heads_mm(t[x], p[1] * p[3] * e_gc[x], bd) for x, p in enumerate(probs)]
    wq = [jnp.concatenate([w[x], p[0] * e_gc[x]], axis=0).astype(BF16) for x, p in enumerate(probs)]
    k_dec = [(p[1] * jnp.exp(g_last[x] - gc[x])).astype(BF16) for x, p in enumerate(probs)]
    g_tot = [jnp.exp(g_last[x]) for x in range(n)]
    return u, wq, k_dec, a_intra, g_tot


def _gdn_state_step(u, wq, k_dec, a_intra, g_tot, s_ref, bd):
    c = GDN_CHUNK
    pairs = [slice(p * GDN_PAIR, (p + 1) * GDN_PAIR) for p in range(GDN_W // GDN_PAIR)]
    ws_qs = jnp.concatenate([jnp.dot(wq[:, ls], s_ref[p].astype(BF16), preferred_element_type=F32)
                             for p, ls in enumerate(pairs)], axis=1)
    v_new = u - ws_qs[:c]
    o = ws_qs[c:] + _heads_mm(a_intra, v_new, bd)
    vb = v_new.astype(BF16)
    bdf = bd.astype(F32)
    for p, ls in enumerate(pairs):
        kv = lax.dot_general(k_dec[:, ls], vb[:, ls], (((0,), (0,)), ((), ())), preferred_element_type=F32)
        s_ref[p] = s_ref[p] * g_tot[:, ls] + kv * bdf
    return o


def _gdn_scan_kernel(qf, kf, vf, bf, gf, qb, kb, vb, bb, gb, bd_ref, lvl_ref, dirm_ref, tri_ref,
                     of_ref, ob_ref, sf_ref, sb_ref):
    @pl.when(pl.program_id(1) == 0)
    def _():
        sf_ref[...] = jnp.zeros(sf_ref.shape, F32)
        sb_ref[...] = jnp.zeros(sb_ref.shape, F32)

    bd = bd_ref[...]
    n = SEQ_TILE // GDN_CHUNK
    probs, rows = [], []
    for c in range(n):
        rf = slice(c * GDN_CHUNK, (c + 1) * GDN_CHUNK)
        rb = slice((n - 1 - c) * GDN_CHUNK, (n - c) * GDN_CHUNK)
        probs.append((qf[rf, :], kf[rf, :], vf[rf, :], bf[rf, :], gf[rf, :], False))
        probs.append((qb[rb, :], kb[rb, :], vb[rb, :], bb[rb, :], gb[rb, :], True))
        rows += [rf, rb]
    u, wq, k_dec, a_intra, g_tot = _gdn_intra(probs, bd, lvl_ref, dirm_ref, tri_ref)
    for x in range(2 * n):
        rev = probs[x][5]
        o = _gdn_state_step(u[x], wq[x], k_dec[x], a_intra[x], g_tot[x], sb_ref if rev else sf_ref, bd)
        (ob_ref if rev else of_ref)[rows[x], :] = o


def _seq_tile_maps(n_lat_tiles, n_tiles):
    fwd = lambda s: lax.rem(s + n_lat_tiles, n_tiles)
    bwd = lambda s: n_tiles - 1 - s
    return fwd, bwd


def _gdn_scan(q, k, v, beta, g, n_lat):
    b, s, _ = q.shape
    n_tiles = s // SEQ_TILE
    fwd, bwd = _seq_tile_maps(n_lat // SEQ_TILE, n_tiles)
    spec = lambda tile_of, col: pl.BlockSpec((None, SEQ_TILE, GDN_W), lambda bi, t: (bi, tile_of(t), col))
    masks = _gdn_masks()
    in_specs = [spec(fwd, 0)] * 5 + [spec(bwd, 0)] * 3 + [spec(bwd, 1)] * 2 + [_full(m) for m in masks]
    return pl.pallas_call(
        _gdn_scan_kernel,
        grid=(b, n_tiles),
        in_specs=in_specs,
        out_specs=[spec(fwd, 0), spec(bwd, 0)],
        out_shape=[jax.ShapeDtypeStruct((b, s, GDN_W), F32)] * 2,
        scratch_shapes=[pltpu.VMEM((GDN_W // GDN_PAIR, GDN_PAIR, GDN_PAIR), F32)] * 2,
        compiler_params=_cparams(("parallel", "arbitrary")),
        name="gdn_scan",
    )(q, k, v, beta, g, q, k, v, beta, g, *masks)


def _lru_tile_prep(x_ref, xp_ref, xn_ref, cw_ref, cb_ref, wg_ref, bg_ref, nla_ref, a_ref, b_ref,
                   tile_idx, n_tiles, n_lat_tiles):
    xb = _tile_conv(x_ref[...], xp_ref[...], xn_ref[...], cw_ref, tile_idx, n_tiles, n_lat_tiles) + cb_ref[...]
    gates = _sigmoid(jnp.dot(xb.astype(BF16), wg_ref[...], preferred_element_type=F32) + bg_ref[...])
    log_a = nla_ref[...] * gates[:, :LRU_W]
    a_ref[...] = jnp.exp(log_a)
    th = jnp.tanh(log_a)
    b_ref[...] = jnp.sqrt(-2.0 * th / (1.0 - th)) * gates[:, LRU_W:] * xb


def _scan_group(a, b, h, row, reverse):
    for d in (1, 2, 4):
        if reverse:
            keep = row < 8 - d
            shift = 8 - d
        else:
            keep = row >= d
            shift = d
        a_s = jnp.where(keep, pltpu.roll(a, shift, 0), 1.0)
        b_s = jnp.where(keep, pltpu.roll(b, shift, 0), 0.0)
        b = a * b_s + b
        a = a * a_s
    return a * h + b


def _lru_scan_kernel(xf, xfp, xfn, xb, xbp, xbn, cw_ref, cb_ref, wgf, bgf, nlaf, wgb, bgb, nlab,
                     hf_ref, hb_ref, af_ref, bf_ref, ab_ref, bb_ref, cf_ref, cbk_ref, *, n_lat_tiles):
    s = pl.program_id(1)
    n_tiles = pl.num_programs(1)

    @pl.when(s == 0)
    def _():
        cf_ref[...] = jnp.zeros(cf_ref.shape, F32)
        cbk_ref[...] = jnp.zeros(cbk_ref.shape, F32)

    t_f, t_b = (m(s) for m in _seq_tile_maps(n_lat_tiles, n_tiles))
    _lru_tile_prep(xf, xfp, xfn, cw_ref, cb_ref, wgf, bgf, nlaf, af_ref, bf_ref, t_f, n_tiles, n_lat_tiles)
    _lru_tile_prep(xb, xbp, xbn, cw_ref, cb_ref, wgb, bgb, nlab, ab_ref, bb_ref, t_b, n_tiles, n_lat_tiles)

    n_groups = SEQ_TILE // 8
    row = lax.broadcasted_iota(jnp.int32, (8, LRU_W), 0)

    def body(gi, carry):
        h_f, h_b = carry
        rf = pl.ds(pl.multiple_of(gi * 8, 8), 8)
        rb = pl.ds(pl.multiple_of((n_groups - 1 - gi) * 8, 8), 8)
        out_f = _scan_group(af_ref[rf, :], bf_ref[rf, :], h_f, row, False)
        out_b = _scan_group(ab_ref[rb, :], bb_ref[rb, :], h_b, row, True)
        hf_ref[rf, :] = out_f
        hb_ref[rb, :] = out_b
        return (jnp.broadcast_to(out_f[7:8, :], (8, LRU_W)), jnp.broadcast_to(out_b[0:1, :], (8, LRU_W)))

    h_f, h_b = lax.fori_loop(0, n_groups, body, (cf_ref[...], cbk_ref[...]))
    cf_ref[...] = h_f
    cbk_ref[...] = h_b


def _lru_scan(p, conv_w, conv_b, w_r, b_r, w_i, b_i, lam, n_lat):
    b, s, _ = p.shape
    n_tiles = s // SEQ_TILE
    nlt = n_lat // SEQ_TILE
    fwd, bwd = _seq_tile_maps(nlt, n_tiles)
    col = P_LRU_X // LRU_W
    specs = []
    for tile_of in (fwd, bwd):
        main, prev, make_next = _halo_specs(LRU_W, col, tile_of)
        specs += [pl.BlockSpec((None, SEQ_TILE, LRU_W), main), pl.BlockSpec((None, HALO, LRU_W), prev),
                  pl.BlockSpec((None, HALO, LRU_W), make_next(n_tiles))]

    def blockdiag(w):
        return jax.scipy.linalg.block_diag(*[w[n] for n in range(LRU_BLOCKS)])

    dir_args = []
    for d in range(2):
        wg = jnp.concatenate([blockdiag(w_r[d]), blockdiag(w_i[d])], axis=1).astype(BF16)
        bg = jnp.concatenate([b_r[d], b_i[d]]).astype(F32).reshape(1, 2 * LRU_W)
        nla = (-LRU_C * jax.nn.softplus(-lam[d].astype(F32))).reshape(1, LRU_W)
        dir_args += [wg, bg, nla]
    cb2 = conv_b.reshape(1, LRU_W)
    out_spec = lambda tile_of: pl.BlockSpec((None, SEQ_TILE, LRU_W), lambda bi, t: (bi, tile_of(t), 0))
    return pl.pallas_call(
        functools.partial(_lru_scan_kernel, n_lat_tiles=nlt),
        grid=(b, n_tiles),
        in_specs=specs + [_full(conv_w), _full(cb2)] + [_full(a) for a in dir_args],
        out_specs=[out_spec(fwd), out_spec(bwd)],
        out_shape=[jax.ShapeDtypeStruct((b, s, LRU_W), F32)] * 2,
        scratch_shapes=[pltpu.VMEM((SEQ_TILE, LRU_W), F32)] * 4 + [pltpu.VMEM((8, LRU_W), F32)] * 2,
        compiler_params=_cparams(("parallel", "arbitrary")),
        name="lru_scan",
    )(p, p, p, p, p, p, conv_w, cb2, *dir_args)


def _post_kernel(of_ref, ob_ref, z_ref, gn_ref, ones_ref, hf_ref, hb_ref, y_ref, ua_ref, ub_ref):
    o = of_ref[...] + ob_ref[...]
    ms = jnp.dot(jnp.concatenate(_split3(o * o), axis=-1), ones_ref[...], preferred_element_type=F32) * (1.0 / GDN_DV)
    z = z_ref[...]
    ua_ref[...] = (o * lax.rsqrt(ms + RMS_EPS) * gn_ref[...] * (z * _sigmoid(z))).astype(ua_ref.dtype)
    y = y_ref[...]
    gelu = 0.5 * y * (1.0 + jnp.tanh(0.7978845608028654 * (y + 0.044715 * (y * y * y))))
    ub_ref[...] = ((hf_ref[...] + hb_ref[...]) * gelu).astype(ub_ref.dtype)


def _post(o_f, o_b, h_f, h_b, p, gdn_norm_g):
    b, s, _ = p.shape
    tm = _pick_tile(s, 544)
    ones3 = jnp.concatenate([_head_block_ones(GDN_HEADS, GDN_DV)] * 3, axis=0)
    gn = jnp.tile(gdn_norm_g.astype(F32), GDN_HEADS).reshape(1, GDN_W)
    tok = pl.BlockSpec((None, tm, GDN_W), lambda bi, i: (bi, i, 0))
    pcol = lambda c: pl.BlockSpec((None, tm, GDN_W), lambda bi, i: (bi, i, c))
    return pl.pallas_call(
        _post_kernel,
        grid=(b, s // tm),
        in_specs=[tok, tok, pcol(P_GDN_Z // GDN_W), _full(gn), _full(ones3), tok, tok, pcol(P_LRU_Y // LRU_W)],
        out_specs=[tok, tok],
        out_shape=[jax.ShapeDtypeStruct((b, s, GDN_W), BF16)] * 2,
        compiler_params=_cparams(("parallel", "parallel")),
        name="gdn_lru_post",
    )(o_f, o_b, p, gn, ones3, h_f, h_b, p)


def _merge_kernel(x_ref, mc_ref, ml_ref, u0_ref, u1_ref, u2_ref, u3_ref, t0_ref, t1_ref, t2_ref, t3_ref,
                  bg_ref, wb_ref, wo_ref, o_ref, *, n_lat):
    merged = None
    for n, (u_ref, t_ref) in enumerate(((u0_ref, t0_ref), (u1_ref, t1_ref), (u2_ref, t2_ref), (u3_ref, t3_ref))):
        gate = _sigmoid(t_ref[...] + bg_ref[n:n + 1, :])
        term = gate * jnp.dot(u_ref[...], wb_ref[n], preferred_element_type=F32)
        merged = term if merged is None else merged + term
    out = jnp.dot(merged.astype(BF16), wo_ref[...], preferred_element_type=F32)
    g1 = _row_select(x_ref.shape[0], pl.program_id(1), n_lat, mc_ref[2:3, :], ml_ref[2:3, :])
    o_ref[...] = x_ref[...] + g1 * out


def _merge(xs, mod_c, mod_l, branches, p, b_gate, w_branch, w_out, n_lat, with_ctx):
    b, s, d = xs.shape
    rows = s if with_ctx else n_lat
    tm = _pick_tile(rows, 544) if with_ctx else SEQ_TILE
    gate_blk = P_GATES // d
    tok = lambda w: pl.BlockSpec((None, tm, w), lambda bi, i: (bi, i, 0))
    in_specs = [tok(d), pl.BlockSpec((8, d), lambda bi, i: (0, 0)), pl.BlockSpec((None, 8, d), lambda bi, i: (bi, 0, 0))]
    in_specs += [tok(BRANCH_W)] * N_BRANCH
    in_specs += [pl.BlockSpec((None, tm, d), functools.partial(lambda bi, i, n: (bi, i, gate_blk + n), n=n))
                 for n in range(N_BRANCH)]
    in_specs += [_full(b_gate), _full(w_branch), _full(w_out)]
    return pl.pallas_call(
        functools.partial(_merge_kernel, n_lat=n_lat),
        grid=(b, rows // tm),
        in_specs=in_specs,
        out_specs=tok(d),
        out_shape=jax.ShapeDtypeStruct((b, rows, d), F32),
        input_output_aliases={0: 0} if with_ctx else {},
        compiler_params=_cparams(("parallel", "parallel")),
        name="merge",
    )(xs, mod_c, mod_l, *branches, p, p, p, p, b_gate, w_branch, w_out)


def _mlp_kernel(x_ref, mc_ref, ml_ref, gn_ref, gf_ref, w1_ref, w2_ref, o_ref, h_ref, acc_ref, *, n_lat, final_norm):
    f = pl.program_id(2)
    tm = x_ref.shape[0]
    i = pl.program_id(1)

    @pl.when(f == 0)
    def _():
        x = x_ref[...]
        y = x * lax.rsqrt(jnp.mean(x * x, axis=-1, keepdims=True) + RMS_EPS)
        y = y * gn_ref[...]
        shift = _row_select(tm, i, n_lat, mc_ref[3:4, :], ml_ref[3:4, :])
        scale = _row_select(tm, i, n_lat, mc_ref[4:5, :], ml_ref[4:5, :])
        h_ref[...] = (y * (1.0 + scale) + shift).astype(BF16)

    a = jnp.maximum(jnp.dot(h_ref[...], w1_ref[...], preferred_element_type=F32), 0.0)
    part = jnp.dot((a * a).astype(BF16), w2_ref[...], preferred_element_type=F32)

    @pl.when(f == 0)
    def _():
        acc_ref[...] = part

    @pl.when(f > 0)
    def _():
        acc_ref[...] += part

    @pl.when(f == pl.num_programs(2) - 1)
    def _():
        g2 = _row_select(tm, i, n_lat, mc_ref[5:6, :], ml_ref[5:6, :])
        y = x_ref[...] + g2 * acc_ref[...]
        if final_norm:
            y = y * lax.rsqrt(jnp.mean(y * y, axis=-1, keepdims=True) + RMS_EPS) * gf_ref[...]
        o_ref[...] = y


def _mlp(xs, mod_c, mod_l, gain, w1, w2, final_gain, n_lat, final_norm):
    b, rows, d = xs.shape
    dff = w1.shape[1]
    tm = _pick_tile(rows, 1088)
    tf = 1024
    row = pl.BlockSpec((1, d), lambda bi, i, f: (0, 0))
    return pl.pallas_call(
        functools.partial(_mlp_kernel, n_lat=n_lat, final_norm=final_norm),
        grid=(b, rows // tm, dff // tf),
        in_specs=[pl.BlockSpec((None, tm, d), lambda bi, i, f: (bi, i, 0)),
                  pl.BlockSpec((8, d), lambda bi, i, f: (0, 0)),
                  pl.BlockSpec((None, 8, d), lambda bi, i, f: (bi, 0, 0)),
                  row, row,
                  pl.BlockSpec((d, tf), lambda bi, i, f: (0, f)),
                  pl.BlockSpec((tf, d), lambda bi, i, f: (f, 0))],
        out_specs=pl.BlockSpec((None, tm, d), lambda bi, i, f: (bi, i, 0)),
        out_shape=jax.ShapeDtypeStruct((b, rows, d), F32),
        scratch_shapes=[pltpu.VMEM((tm, d), BF16), pltpu.VMEM((tm, d), F32)],
        compiler_params=_cparams(("parallel", "parallel", "arbitrary")),
        name="mlp",
    )(xs, mod_c, mod_l, gain, final_gain, w1, w2)


def kernel(x, c, ctx, c_ctx, mod_w, mod_b, norm1_g, norm2_g, w_in, b_gate, gdn_conv_w, gdn_a_log, gdn_dt_bias,
           gdn_norm_g, lru_conv_w, lru_conv_b, lru_w_r, lru_b_r, lru_w_i, lru_b_i, lru_lambda, mla_q_norm_g,
           mla_w_uq, mla_kv_norm_g, mla_w_ukv, na_rpb, w_branch, w_out, mlp_w1, mlp_w2, final_norm_g):
    bsz, n_tok, d = x.shape
    n_ctx = ctx.shape[1]
    depth = w_in.shape[0]
    assert n_ctx % SEQ_TILE == 0 and n_tok % SEQ_TILE == 0 and n_tok % GRID_W == 0
    na_meta, onehot_r, onehot_c = _na_geometry(n_tok // GRID_W)
    na_meta = jnp.asarray(na_meta)
    cos, sin = _rope_tables(n_tok, n_ctx)

    n_rows = -(-(bsz + 1) // 8) * 8
    cc = jnp.zeros((n_rows, d), F32).at[:bsz].set(c).at[bsz].set(c_ctx)
    final_gain = final_norm_g.reshape(1, d)

    xs = jnp.concatenate([x, ctx], axis=1)
    for l in range(depth):
        need_ctx = l < depth - 1
        mod = _modulation(cc, mod_w[l], mod_b[l]).reshape(n_rows, N_MOD, d)
        pad = jnp.zeros((8 - N_MOD, d), F32)
        mod_c = jnp.concatenate([mod[bsz], pad], axis=0)
        mod_l = jnp.concatenate([mod[:bsz], jnp.broadcast_to(pad, (bsz, 8 - N_MOD, d))], axis=1)

        w_in_l = _arrange_w_in(w_in[l])
        wq, wk, wv, place = _arrange_mla(mla_w_uq[l], mla_w_ukv[l])
        gq = mla_q_norm_g[l].reshape(1, -1)
        gkv = mla_kv_norm_g[l].reshape(1, -1)
        g1n = norm1_g[l].reshape(1, d)
        g2n = norm2_g[l].reshape(1, d)
        wb = w_branch[l].astype(BF16)
        wo = w_out[l].astype(BF16)
        w1 = mlp_w1[l].astype(BF16)
        w2 = mlp_w2[l].astype(BF16)

        p32, p16 = _inproj(xs, mod_c, mod_l, g1n, w_in_l, n_tok)

        gq_, gk_, gv_, gbeta, gg = _gdn_prep(p32, gdn_conv_w[l], gdn_a_log[l], gdn_dt_bias[l], n_tok)
        o_f, o_b = _gdn_scan(gq_, gk_, gv_, gbeta, gg, n_tok)
        h_f, h_b = _lru_scan(p32, lru_conv_w[l], lru_conv_b[l], lru_w_r[l], lru_b_r[l], lru_w_i[l], lru_b_i[l],
                             lru_lambda[l], n_tok)
        ua, ub = _post(o_f, o_b, h_f, h_b, p32, gdn_norm_g[l])

        mq, mk, mv = _mla_prep(p32, p16, cos, sin, gq, gkv, wq, wk, wv, place)
        uc = _mla_flash(mq, mk, mv, n_tok)
        if need_ctx:
            uc = _mla_flash(mq, mk, mv, n_tok, prev=uc)
        ud = _na_attention(p16, na_meta, _na_bias_table(na_rpb[l], onehot_r, onehot_c), n_tok, need_ctx)

        xs = _merge(xs, mod_c, mod_l, (ua, ub, uc, ud), p16, b_gate[l], wb, wo, n_tok, need_ctx)
        xs = _mlp(xs, mod_c, mod_l, g2n, w1, w2, final_gain, n_tok, l == depth - 1)
    return xs
```

```python
import functools

import jax
import jax.numpy as jnp
import numpy as np
from jax import lax
from jax.experimental import pallas as pl
from jax.experimental.pallas import tpu as pltpu

F32 = jnp.float32
BF16 = jnp.bfloat16

GRID_W = 64
N_MOD = 6
RMS_EPS = 1e-6
GDN_HEADS = 4
GDN_DK = 64
GDN_DV = 64
GDN_CHUNK = 64
GDN_W = GDN_HEADS * GDN_DV
GDN_CONV = 4
LRU_W = 256
LRU_BLOCKS = 4
LRU_BLOCK_W = LRU_W // LRU_BLOCKS
LRU_CONV = 4
LRU_C = 8.0
MLA_HEADS = 4
MLA_Q_RANK = 256
MLA_KV_RANK = 128
MLA_NOPE = 64
MLA_ROPE = 32
MLA_V = 64
MLA_SLOT = 128
ROPE_BASE = 10000.0
LOG2_E = 1.4426950408889634
NA_HEADS = 4
NA_DH = 64
NA_W = NA_HEADS * NA_DH
NA_WIN_ROWS = 8
NA_WIN_COLS = 16
N_BRANCH = 4
BRANCH_W = 256

SEQ_TILE = 256
HALO = 8

_REF_COLS = {}
_off = 0
for _name, _w in (('gdn_qkv', 3 * GDN_W), ('gdn_z', GDN_W), ('gdn_beta', 2 * GDN_HEADS), ('gdn_a', 2 * GDN_HEADS),
                  ('lru_x', LRU_W), ('lru_y', LRU_W), ('mla_q', MLA_Q_RANK), ('mla_kv', MLA_KV_RANK),
                  ('mla_kr', MLA_ROPE), ('na_qkv', 3 * NA_W)):
    _REF_COLS[_name] = (_off, _w)
    _off += _w
N_MIX_COLS = _off

P_GDN_QKV = 0
P_GDN_Z = 768
P_LRU_X = 1024
P_LRU_Y = 1280
P_MLA_KV = 1536
P_MLA_KR = 1664
P_GDN_BA = 1792
P32_COLS = 2048
P_MLA_Q = 0
P_NA_QKV = 256
P_GATES = 1024
P16_COLS = P_GATES + N_BRANCH * 1024
PROJ_TILE = 1024

VMEM_LIMIT = 52 * 1024 * 1024


def _cparams(sem):
    return pltpu.CompilerParams(dimension_semantics=sem, vmem_limit_bytes=VMEM_LIMIT)


def _pick_tile(n, cap):
    best = 8
    for t in range(8, min(n, cap) + 1, 8):
        if n % t == 0:
            best = t
    return best


def _full(a):
    return pl.BlockSpec(a.shape, lambda *_: (0,) * a.ndim)


def _split3(x):
    hi = x.astype(BF16)
    r = x - hi.astype(F32)
    mid = r.astype(BF16)
    lo = (r - mid.astype(F32)).astype(BF16)
    return hi, mid, lo


def _sigmoid(x):
    return 0.5 * jnp.tanh(0.5 * x) + 0.5


def _arrange_w_in(w_in):
    d = w_in.shape[0]
    pieces, pos = [], 0

    def put(dst, block):
        nonlocal pos
        if dst > pos:
            pieces.append(jnp.zeros((d, dst - pos), w_in.dtype))
        pieces.append(block)
        pos = dst + block.shape[1]

    ref = lambda name: w_in[:, _REF_COLS[name][0]:_REF_COLS[name][0] + _REF_COLS[name][1]]
    for name, dst in (('gdn_qkv', P_GDN_QKV), ('gdn_z', P_GDN_Z), ('lru_x', P_LRU_X), ('lru_y', P_LRU_Y),
                      ('mla_kv', P_MLA_KV), ('mla_kr', P_MLA_KR), ('gdn_beta', P_GDN_BA),
                      ('gdn_a', P_GDN_BA + 2 * GDN_HEADS)):
        put(dst, ref(name))
    put(P32_COLS + P_MLA_Q, ref('mla_q'))
    put(P32_COLS + P_NA_QKV, ref('na_qkv'))
    put(P32_COLS + P_GATES, w_in[:, N_MIX_COLS:])
    assert pos == P32_COLS + P16_COLS
    return jnp.concatenate(pieces, axis=1).astype(BF16)


def _rope_perm():
    q = MLA_ROPE // 4
    src = np.zeros(MLA_ROPE, np.int32)
    sign = np.zeros(MLA_ROPE, np.float32)
    for base in (0, 2 * q):
        for d in range(q):
            src[base + d] = base + d + q
            sign[base + d] = -1.0
            src[base + q + d] = base + d
            sign[base + q + d] = 1.0
    return src, sign


def _arrange_mla(w_uq, w_ukv):
    src, sign = _rope_perm()
    hq = MLA_NOPE + MLA_ROPE
    wq = jnp.zeros((MLA_Q_RANK, 2 * MLA_HEADS * MLA_SLOT), F32)
    wk = jnp.zeros((MLA_KV_RANK, MLA_HEADS * MLA_SLOT), F32)
    wv = jnp.zeros((MLA_KV_RANK, MLA_HEADS * MLA_V), F32)
    place = np.zeros((2 * MLA_SLOT, 2 * MLA_HEADS * MLA_SLOT), np.float32)
    rot_off = MLA_HEADS * MLA_SLOT
    for h in range(MLA_HEADS):
        nope = w_uq[:, h * hq:h * hq + MLA_NOPE]
        pe = w_uq[:, h * hq + MLA_NOPE:(h + 1) * hq]
        s = h * MLA_SLOT
        wq = wq.at[:, s:s + MLA_NOPE].set(nope)
        wq = wq.at[:, s + MLA_NOPE:s + MLA_NOPE + MLA_ROPE].set(pe)
        wq = wq.at[:, rot_off + s + MLA_NOPE:rot_off + s + MLA_NOPE + MLA_ROPE].set(pe[:, src] * sign)
        wk = wk.at[:, s:s + MLA_NOPE].set(w_ukv[:, h * 128:h * 128 + MLA_NOPE])
        wv = wv.at[:, h * MLA_V:(h + 1) * MLA_V].set(w_ukv[:, h * 128 + MLA_NOPE:(h + 1) * 128])
        for d in range(MLA_ROPE):
            for half in (0, MLA_SLOT):
                place[half + d, s + MLA_NOPE + d] = 1.0
                place[half + src[d], rot_off + s + MLA_NOPE + d] = sign[d]
    return wq.astype(BF16), wk.astype(BF16), wv.astype(BF16), jnp.asarray(place, BF16)


def _rope_tables(n_tok, n_ctx):
    cos = np.ones((n_tok + n_ctx, MLA_SLOT), np.float32)
    sin = np.zeros((n_tok + n_ctx, MLA_SLOT), np.float32)
    t = np.arange(n_tok)
    row = (t // GRID_W).astype(np.float32)
    col = (t % GRID_W).astype(np.float32)
    n_freq = MLA_ROPE // 4
    inv = (ROPE_BASE ** (-np.arange(n_freq, dtype=np.float32) / n_freq)).astype(np.float32)
    ar = row[:, None] * inv
    ac = col[:, None] * inv
    ang = np.concatenate([ar, ar, ac, ac], axis=-1).astype(np.float32)
    cos[:n_tok, MLA_NOPE:MLA_NOPE + MLA_ROPE] = np.cos(ang)
    sin[:n_tok, MLA_NOPE:MLA_NOPE + MLA_ROPE] = np.sin(ang)
    return jnp.asarray(cos), jnp.asarray(sin)


def _head_block_ones(n_heads, width):
    m = np.kron(np.eye(n_heads, dtype=np.float32), np.ones((width, width), np.float32))
    return jnp.asarray(m, BF16)


def _mod_kernel(c_ref, w_ref, b_ref, o_ref):
    c = c_ref[...]
    s = c * _sigmoid(c)
    o_ref[...] = jnp.dot(s, w_ref[...], preferred_element_type=F32) + b_ref[...]


def _modulation(cc, mod_w, mod_b):
    r, d = cc.shape
    n = mod_w.shape[1]
    tn = 1024
    return pl.pallas_call(
        _mod_kernel,
        grid=(n // tn,),
        in_specs=[pl.BlockSpec((r, d), lambda j: (0, 0)),
                  pl.BlockSpec((d, tn), lambda j: (0, j)),
                  pl.BlockSpec((1, tn), lambda j: (0, j))],
        out_specs=pl.BlockSpec((r, tn), lambda j: (0, j)),
        out_shape=jax.ShapeDtypeStruct((r, n), F32),
        compiler_params=_cparams(("arbitrary",)),
        name="modulation",
    )(cc, mod_w, mod_b.reshape(1, n))


def _row_select(tile_rows, tile_idx, n_lat, ctx_vec, lat_vec):
    row = tile_idx * tile_rows + lax.broadcasted_iota(jnp.int32, (tile_rows, 1), 0)
    return jnp.where(row < n_lat, lat_vec, ctx_vec)


def _inproj_kernel(x_ref, mc_ref, ml_ref, g_ref, w_ref, o32_ref, o16_ref, h_ref, *, n_lat):
    j = pl.program_id(2)
    n32 = P32_COLS // PROJ_TILE

    @pl.when(j == 0)
    def _():
        tm = x_ref.shape[0]
        i = pl.program_id(1)
        x = x_ref[...]
        y = x * lax.rsqrt(jnp.mean(x * x, axis=-1, keepdims=True) + RMS_EPS)
        y = y * g_ref[...]
        shift = _row_select(tm, i, n_lat, mc_ref[0:1, :], ml_ref[0:1, :])
        scale = _row_select(tm, i, n_lat, mc_ref[1:2, :], ml_ref[1:2, :])
        h_ref[...] = (y * (1.0 + scale) + shift).astype(BF16)

    acc = jnp.dot(h_ref[...], w_ref[...], preferred_element_type=F32)

    @pl.when(j < n32)
    def _():
        o32_ref[...] = acc

    @pl.when(j >= n32)
    def _():
        o16_ref[...] = acc.astype(BF16)


def _inproj(xs, mod_c, mod_l, gain, w, n_lat):
    b, s, d = xs.shape
    tm = _pick_tile(s, 1088)
    tn = PROJ_TILE
    n32 = P32_COLS // tn
    return pl.pallas_call(
        functools.partial(_inproj_kernel, n_lat=n_lat),
        grid=(b, s // tm, (P32_COLS + P16_COLS) // tn),
        in_specs=[pl.BlockSpec((None, tm, d), lambda bi, i, j: (bi, i, 0)),
                  pl.BlockSpec((8, d), lambda bi, i, j: (0, 0)),
                  pl.BlockSpec((None, 8, d), lambda bi, i, j: (bi, 0, 0)),
                  pl.BlockSpec((1, d), lambda bi, i, j: (0, 0)),
                  pl.BlockSpec((d, tn), lambda bi, i, j: (0, j))],
        out_specs=[pl.BlockSpec((None, tm, tn), lambda bi, i, j: (bi, i, jnp.minimum(j, n32 - 1))),
                   pl.BlockSpec((None, tm, tn), lambda bi, i, j: (bi, i, jnp.maximum(j - n32, 0)))],
        out_shape=[jax.ShapeDtypeStruct((b, s, P32_COLS), F32), jax.ShapeDtypeStruct((b, s, P16_COLS), BF16)],
        scratch_shapes=[pltpu.VMEM((tm, d), BF16)],
        compiler_params=_cparams(("parallel", "parallel", "arbitrary")),
        name="inproj",
    )(xs, mod_c, mod_l, gain, w)


def _mla_prep_kernel(ql_ref, kv_ref, cos_ref, sin_ref, gq_ref, gkv_ref, wq_ref, wk_ref, wv_ref, pl_ref,
                     q_ref, k_ref, v_ref, *, scale):
    nslot = MLA_HEADS * MLA_SLOT
    cos = jnp.concatenate([cos_ref[...]] * MLA_HEADS, axis=-1)
    sin = jnp.concatenate([sin_ref[...]] * MLA_HEADS, axis=-1)

    ql = ql_ref[...].astype(F32)
    qn = ql * lax.rsqrt(jnp.mean(ql * ql, axis=-1, keepdims=True) + RMS_EPS) * gq_ref[...]
    q2 = jnp.dot(qn.astype(BF16), wq_ref[...], preferred_element_type=F32)
    q = (q2[:, :nslot] * cos + q2[:, nslot:] * sin) * scale
    q_ref[...] = q.astype(BF16)

    kvkr = kv_ref[...]
    kvl = kvkr[:, :MLA_KV_RANK]
    kr = kvkr[:, MLA_KV_RANK:]
    kvn = (kvl * lax.rsqrt(jnp.mean(kvl * kvl, axis=-1, keepdims=True) + RMS_EPS) * gkv_ref[...]).astype(BF16)
    kn = jnp.dot(kvn, wk_ref[...], preferred_element_type=F32)
    v_ref[...] = jnp.dot(kvn, wv_ref[...], preferred_element_type=F32).astype(BF16)
    kr_hi = kr.astype(BF16)
    kr_lo = (kr - kr_hi.astype(F32)).astype(BF16)
    kr2 = jnp.dot(jnp.concatenate([kr_hi, kr_lo], axis=-1), pl_ref[...], preferred_element_type=F32)
    k_ref[...] = (kn + kr2[:, :nslot] * cos + kr2[:, nslot:] * sin).astype(BF16)


def _mla_prep(p32, p16, cos, sin, gq, gkv, wq, wk, wv, place):
    b, s, _ = p32.shape
    tm = _pick_tile(s, 544)
    nslot = MLA_HEADS * MLA_SLOT
    scale = (MLA_NOPE + MLA_ROPE) ** -0.5 * LOG2_E
    return pl.pallas_call(
        functools.partial(_mla_prep_kernel, scale=scale),
        grid=(b, s // tm),
        in_specs=[pl.BlockSpec((None, tm, MLA_Q_RANK), lambda bi, i: (bi, i, P_MLA_Q // MLA_Q_RANK)),
                  pl.BlockSpec((None, tm, 2 * MLA_SLOT), lambda bi, i: (bi, i, P_MLA_KV // (2 * MLA_SLOT))),
                  pl.BlockSpec((tm, MLA_SLOT), lambda bi, i: (i, 0)),
                  pl.BlockSpec((tm, MLA_SLOT), lambda bi, i: (i, 0)),
                  _full(gq), _full(gkv), _full(wq), _full(wk), _full(wv), _full(place)],
        out_specs=[pl.BlockSpec((None, tm, nslot), lambda bi, i: (bi, i, 0)),
                   pl.BlockSpec((None, tm, nslot), lambda bi, i: (bi, i, 0)),
                   pl.BlockSpec((None, tm, MLA_HEADS * MLA_V), lambda bi, i: (bi, i, 0))],
        out_shape=[jax.ShapeDtypeStruct((b, s, nslot), BF16),
                   jax.ShapeDtypeStruct((b, s, nslot), BF16),
                   jax.ShapeDtypeStruct((b, s, MLA_HEADS * MLA_V), BF16)],
        compiler_params=_cparams(("parallel", "parallel")),
        name="mla_prep",
    )(p16, p32, cos, sin, gq, gkv, wq, wk, wv, place)


def _flash_softmax(h, s, m_ref, l_ref):
    m_prev = m_ref[h]
    m_new = jnp.maximum(m_prev, jnp.max(s, axis=-1, keepdims=True))
    alpha = jnp.exp2(m_prev - m_new)
    p = jnp.exp2(s - jnp.concatenate([m_new] * (s.shape[1] // 128), axis=-1))
    l_ref[h] = alpha * l_ref[h] + jnp.sum(p, axis=-1, keepdims=True)
    m_ref[h] = m_new
    return alpha, p.astype(BF16)


def _mla_flash_kernel(*refs, tk, aliased):
    if aliased:
        q_ref, k_ref, v_ref, _, o_ref, m_ref, l_ref, acc_ref = refs
    else:
        q_ref, k_ref, v_ref, o_ref, m_ref, l_ref, acc_ref = refs
    tq = q_ref.shape[0]
    nk = k_ref.shape[0]
    n_loop = nk // tk
    m_ref[...] = jnp.full(m_ref.shape, -jnp.inf, F32)
    l_ref[...] = jnp.zeros(l_ref.shape, F32)
    acc_ref[...] = jnp.zeros(acc_ref.shape, F32)

    def chunk(rows):
        heads = range(MLA_HEADS)
        hs = [slice(h * MLA_SLOT, (h + 1) * MLA_SLOT) for h in heads]
        vs = [slice((h // 2) * 128, (h // 2) * 128 + 128) for h in heads]
        s = [lax.dot_general(q_ref[:, hs[h]], k_ref[rows, hs[h]], (((1,), (1,)), ((), ())),
                             preferred_element_type=F32) for h in heads]
        ap = [_flash_softmax(h, s[h], m_ref, l_ref) for h in heads]
        for h in heads:
            alpha, p = ap[h]
            acc_ref[h] = alpha * acc_ref[h] + jnp.dot(p, v_ref[rows, vs[h]], preferred_element_type=F32)

    if n_loop:
        def body(j, carry):
            chunk(pl.ds(pl.multiple_of(j * tk, tk), tk))
            return carry
        lax.fori_loop(0, n_loop, body, 0)
    if nk > n_loop * tk:
        chunk(slice(n_loop * tk, nk))

    lane = lax.broadcasted_iota(jnp.int32, (tq, 128), 1)
    outs = []
    for pair in range(MLA_HEADS // 2):
        o0 = acc_ref[2 * pair] / l_ref[2 * pair]
        o1 = acc_ref[2 * pair + 1] / l_ref[2 * pair + 1]
        outs.append(jnp.where(lane < MLA_V, o0, o1))
    o_ref[...] = jnp.concatenate(outs, axis=-1).astype(o_ref.dtype)


def _mla_flash(q, k, v, n_lat, prev=None):
    b, s, nslot = q.shape
    n_ctx = s - n_lat
    nv = MLA_HEADS * MLA_V
    tk = 512
    if prev is None:
        tq = _pick_tile(n_lat, 512)
        q_off, n_q, kv_rows, kv_blk = 0, n_lat // tq, s, 0
    else:
        assert n_lat % n_ctx == 0
        tq = _pick_tile(n_ctx, 256)
        q_off, n_q, kv_rows, kv_blk = n_lat // tq, n_ctx // tq, n_ctx, n_lat // n_ctx
    in_specs = [pl.BlockSpec((None, tq, nslot), lambda bi, i: (bi, i + q_off, 0)),
                pl.BlockSpec((None, kv_rows, nslot), lambda bi, i: (bi, kv_blk, 0)),
                pl.BlockSpec((None, kv_rows, nv), lambda bi, i: (bi, kv_blk, 0))]
    args = [q, k, v]
    aliases = {}
    if prev is not None:
        in_specs.append(pl.BlockSpec(memory_space=pl.ANY))
        args.append(prev)
        aliases = {3: 0}
    return pl.pallas_call(
        functools.partial(_mla_flash_kernel, tk=tk, aliased=prev is not None),
        grid=(b, n_q),
        in_specs=in_specs,
        out_specs=pl.BlockSpec((None, tq, nv), lambda bi, i: (bi, i + q_off, 0)),
        out_shape=jax.ShapeDtypeStruct((b, s, nv), BF16),
        scratch_shapes=[pltpu.VMEM((MLA_HEADS, tq, 128), F32),
                        pltpu.VMEM((MLA_HEADS, tq, 128), F32),
                        pltpu.VMEM((MLA_HEADS, tq, 128), F32)],
        input_output_aliases=aliases,
        compiler_params=_cparams(("parallel", "arbitrary")),
        name="mla_flash",
    )(*args)


NA_QROWS = 4
NA_SLAB = NA_QROWS + NA_WIN_ROWS


def _na_geometry(rows):
    assert rows % NA_QROWS == 0 and rows >= NA_SLAB
    nblk = rows // NA_QROWS
    qc = np.arange(GRID_W)
    cs = np.clip(qc - NA_WIN_COLS // 2, 0, GRID_W - NA_WIN_COLS)
    col_valid = (qc[None, :] >= cs[:, None]) & (qc[None, :] < cs[:, None] + NA_WIN_COLS)
    rel_c = np.clip(qc[None, :] - qc[:, None] + NA_WIN_COLS - 1, 0, 2 * NA_WIN_COLS - 2)
    onehot_c = (rel_c[None] == np.arange(2 * NA_WIN_COLS - 1)[:, None, None]) & col_valid[None]
    patterns, cls, starts = {}, [], []
    for i in range(nblk):
        r0 = i * NA_QROWS
        start = int(np.clip(r0 - NA_WIN_ROWS // 2, 0, rows - NA_SLAB))
        qr = r0 + np.arange(NA_QROWS)
        rs = np.clip(qr - NA_WIN_ROWS // 2, 0, rows - NA_WIN_ROWS)
        key = (start - r0,) + tuple((rs - r0).tolist())
        if key not in patterns:
            kr = start + np.arange(NA_SLAB)
            row_valid = (kr[None, :] >= rs[:, None]) & (kr[None, :] < rs[:, None] + NA_WIN_ROWS)
            rel_r = np.clip(kr[None, :] - qr[:, None] + NA_WIN_ROWS - 1, 0, 2 * NA_WIN_ROWS - 2)
            onehot_r = (rel_r[..., None] == np.arange(2 * NA_WIN_ROWS - 1)) & row_valid[..., None]
            patterns[key] = (len(patterns), onehot_r)
        cls.append(patterns[key][0])
        starts.append(start)
    ordered = sorted(patterns.values(), key=lambda z: z[0])
    onehot_r = np.stack([z[1] for z in ordered]).astype(np.float32)
    meta = np.stack([np.asarray(cls, np.int32), np.asarray(starts, np.int32)])
    return meta, onehot_r, onehot_c.astype(np.float32)


def _na_bias_table(rpb, onehot_r, onehot_c):
    hi = lax.Precision.HIGHEST
    bc = jnp.einsum('hrc,cqk->hrqk', rpb.astype(F32), onehot_c, precision=hi)
    tab = jnp.einsum('pabr,hrqk->phaqbk', onehot_r, bc, precision=hi)
    valid = jnp.einsum('pabr,cqk->paqbk', onehot_r, onehot_c, precision=hi) > 0.5
    tab = jnp.where(valid[:, None], tab, -jnp.inf)
    p, h = tab.shape[:2]
    return tab.reshape(p, h, NA_QROWS * GRID_W, NA_SLAB * GRID_W)


def _na_scores(q_pair, lane, h, k_parts):
    in_head = (lane < NA_DH) if h % 2 == 0 else (lane >= NA_DH)
    qm = jnp.where(in_head, q_pair * (NA_DH ** -0.5), 0.0).astype(BF16)
    return [lax.dot_general(qm, k, (((1,), (1,)), ((), ())), preferred_element_type=F32) for k in k_parts]


def _na_softmax(scores, bias):
    if bias is not None:
        scores = [scores[0] + bias] + scores[1:]
    m = scores[0].max(axis=-1, keepdims=True)
    for s in scores[1:]:
        m = jnp.maximum(m, s.max(axis=-1, keepdims=True))
    probs = [jnp.exp(s - m) for s in scores]
    den = probs[0].sum(axis=-1, keepdims=True)
    for p in probs[1:]:
        den = den + p.sum(axis=-1, keepdims=True)
    return [p.astype(BF16) for p in probs], den


def _na_kernel(meta_ref, q_ref, k_ref, v_ref, bias_ref, o_ref, *, n_lat, with_ctx):
    i = pl.program_id(1)
    nq = q_ref.shape[0]
    n_lat_tiles = n_lat // nq
    n_all = k_ref.shape[0]
    lane = lax.broadcasted_iota(jnp.int32, (nq, 128), 1)

    def run(windowed):
        key_rows = [pl.ds(n_lat, n_all - n_lat)]
        if windowed:
            start = pl.multiple_of(meta_ref[1, jnp.minimum(i, n_lat_tiles - 1)] * GRID_W, GRID_W)
            key_rows = [pl.ds(start, NA_SLAB * GRID_W)] + key_rows
        pair_lanes = [slice((h // 2) * 128, (h // 2 + 1) * 128) for h in range(NA_HEADS)]
        scores = [_na_scores(q_ref[:, pair_lanes[h]], lane, h, [k_ref[r, pair_lanes[h]] for r in key_rows])
                  for h in range(NA_HEADS)]
        soft = [_na_softmax(scores[h], bias_ref[h] if windowed else None) for h in range(NA_HEADS)]
        heads = []
        for h in range(NA_HEADS):
            probs, den = soft[h]
            out = None
            for p, r in zip(probs, key_rows):
                po = jnp.dot(p, v_ref[r, pair_lanes[h]], preferred_element_type=F32)
                out = po if out is None else out + po
            heads.append(out / den)
        outs = [jnp.where(lane < NA_DH, heads[2 * pair], heads[2 * pair + 1]) for pair in range(NA_HEADS // 2)]
        o_ref[...] = jnp.concatenate(outs, axis=-1).astype(o_ref.dtype)

    if with_ctx:
        pl.when(i < n_lat_tiles)(lambda: run(True))
        pl.when(i >= n_lat_tiles)(lambda: run(False))
    else:
        run(True)


def _na_attention(p, meta, table, n_lat, with_ctx_queries):
    b, s, _ = p.shape
    nq = NA_QROWS * GRID_W
    nk = NA_SLAB * GRID_W
    assert (s - n_lat) % nq == 0
    nlt = n_lat // nq
    qb = P_NA_QKV // NA_W
    grid_spec = pltpu.PrefetchScalarGridSpec(
        num_scalar_prefetch=1,
        grid=(b, s // nq if with_ctx_queries else nlt),
        in_specs=[pl.BlockSpec((None, nq, NA_W), lambda bi, i, m: (bi, i, qb)),
                  pl.BlockSpec((None, s, NA_W), lambda bi, i, m: (bi, 0, qb + 1)),
                  pl.BlockSpec((None, s, NA_W), lambda bi, i, m: (bi, 0, qb + 2)),
                  pl.BlockSpec((None, NA_HEADS, nq, nk),
                               lambda bi, i, m: (m[0, jnp.minimum(i, nlt - 1)], 0, 0, 0))],
        out_specs=pl.BlockSpec((None, nq, NA_W), lambda bi, i, m: (bi, i, 0)))
    return pl.pallas_call(
        functools.partial(_na_kernel, n_lat=n_lat, with_ctx=with_ctx_queries),
        grid_spec=grid_spec,
        out_shape=jax.ShapeDtypeStruct((b, s, NA_W), BF16),
        compiler_params=_cparams(("parallel", "arbitrary")),
        name="na_attention",
    )(meta, p, p, p, table)


def _tile_conv(x, prev, nxt, w_ref, tile_idx, n_tiles, n_lat_tiles):
    r = x.shape[0]
    width = w_ref.shape[0]
    left = width // 2
    has_prev = jnp.logical_and(tile_idx != 0, tile_idx != n_lat_tiles)
    has_next = jnp.logical_and(tile_idx != n_tiles - 1, tile_idx != n_lat_tiles - 1)
    prev = jnp.where(has_prev, prev, 0.0)
    nxt = jnp.where(has_next, nxt, 0.0)
    xe = jnp.concatenate([prev, x, nxt], axis=0)
    acc = None
    for j in range(width):
        o = HALO - left + j
        term = xe[o:o + r, :] * w_ref[j:j + 1, :]
        acc = term if acc is None else acc + term
    return acc


def _halo_specs(width, col_block, tile_of):
    per = SEQ_TILE // HALO

    def main(bi, s, *_):
        return (bi, tile_of(s), col_block)

    def prev(bi, s, *_):
        return (bi, jnp.maximum(tile_of(s) * per - 1, 0), col_block)

    def make_next(n_tiles):
        def nxt(bi, s, *_):
            return (bi, jnp.minimum((tile_of(s) + 1) * per, n_tiles * per - 1), col_block)
        return nxt

    return main, prev, make_next


def _gdn_prep_kernel(x_ref, xp_ref, xn_ref, ba_ref, cw_ref, ones_ref, exp_ref, alog_ref, dtb_ref,
                     q_ref, k_ref, v_ref, beta_ref, g_ref, *, n_lat_tiles):
    i = pl.program_id(1)
    y = _tile_conv(x_ref[...], xp_ref[...], xn_ref[...], cw_ref, i, pl.num_programs(1), n_lat_tiles)
    y = y * _sigmoid(y)
    q = y[:, :GDN_W]
    k = y[:, GDN_W:2 * GDN_W]
    v_ref[...] = y[:, 2 * GDN_W:]

    def head_norm(u):
        parts = jnp.concatenate(_split3(u * u), axis=-1)
        ss = jnp.dot(parts, ones_ref[...], preferred_element_type=F32)
        return u * lax.rsqrt(ss + RMS_EPS)

    q_ref[...] = head_norm(q) * (GDN_DK ** -0.5)
    k_ref[...] = head_norm(k)

    logits = jnp.dot(jnp.concatenate(_split3(ba_ref[...]), axis=-1), exp_ref[...], preferred_element_type=F32)
    half = 2 * GDN_W
    beta_ref[...] = _sigmoid(logits[:, :half])
    a = logits[:, half:] + dtb_ref[...]
    softplus = jnp.maximum(a, 0.0) + jnp.log1p(jnp.exp(-jnp.abs(a)))
    g_ref[...] = -jnp.exp(alog_ref[...]) * softplus


def _gdn_prep(p, conv_w, a_log, dt_bias, n_lat):
    b, s, _ = p.shape
    n_tiles = s // SEQ_TILE
    main, prev, make_next = _halo_specs(3 * GDN_W, 0, lambda t: t)
    ones3 = jnp.concatenate([_head_block_ones(GDN_HEADS, GDN_DK)] * 3, axis=0)
    expand = np.zeros((128, 4 * GDN_W), np.float32)
    for kind in range(2):
        for d in range(2):
            for h in range(GDN_HEADS):
                c0 = kind * 2 * GDN_W + d * GDN_W + h * GDN_DV
                expand[kind * 2 * GDN_HEADS + d * GDN_HEADS + h, c0:c0 + GDN_DV] = 1.0
    expand3 = jnp.asarray(np.concatenate([expand] * 3, axis=0), BF16)
    alog_e = jnp.repeat(a_log.astype(F32).reshape(-1), GDN_DV).reshape(1, 2 * GDN_W)
    dtb_e = jnp.repeat(dt_bias.astype(F32).reshape(-1), GDN_DV).reshape(1, 2 * GDN_W)
    tok = lambda w: pl.BlockSpec((None, SEQ_TILE, w), lambda bi, i: (bi, i, 0))
    return pl.pallas_call(
        functools.partial(_gdn_prep_kernel, n_lat_tiles=n_lat // SEQ_TILE),
        grid=(b, n_tiles),
        in_specs=[pl.BlockSpec((None, SEQ_TILE, 3 * GDN_W), main),
                  pl.BlockSpec((None, HALO, 3 * GDN_W), prev),
                  pl.BlockSpec((None, HALO, 3 * GDN_W), make_next(n_tiles)),
                  pl.BlockSpec((None, SEQ_TILE, 128), lambda bi, i: (bi, i, P_GDN_BA // 128)),
                  _full(conv_w), _full(ones3), _full(expand3), _full(alog_e), _full(dtb_e)],
        out_specs=[tok(GDN_W), tok(GDN_W), tok(GDN_W), tok(2 * GDN_W), tok(2 * GDN_W)],
        out_shape=[jax.ShapeDtypeStruct((b, s, GDN_W), F32)] * 3 + [jax.ShapeDtypeStruct((b, s, 2 * GDN_W), F32)] * 2,
        compiler_params=_cparams(("parallel", "parallel")),
        name="gdn_prep",
    )(p, p, p, p, conv_w, ones3, expand3, alog_e, dtb_e)


GDN_PAIR = 2 * GDN_DK


def _gdn_masks():
    c, w = GDN_CHUNK, GDN_W
    r2, c2 = np.arange(GDN_PAIR)[:, None], np.arange(GDN_PAIR)[None, :]
    bd = ((r2 // c) == (c2 // c)).astype(np.float32)
    i = np.arange(c)[:, None]
    j = (np.arange(w) % c)[None, :]
    level = np.zeros((c, w), np.int32)
    for bit in range(6):
        level += ((i ^ j) >= (1 << bit)).astype(np.int32)
    lvl = np.stack([(level == m).astype(np.float32) for m in range(7)])
    dirm = np.stack([np.stack([(j <= i), (j < i)]), np.stack([(j >= i), (j > i)])]).astype(np.float32)
    tj = (np.arange(3 * c) % c)[None, :]
    tri = np.stack([(tj <= i), (tj >= i)]).astype(np.float32)
    return jnp.asarray(bd, BF16), jnp.asarray(lvl), jnp.asarray(dirm), jnp.asarray(tri, BF16)


def _heads_mm(x, y, bd, transpose_rhs=False):
    xb = x.astype(BF16)
    yb = y.astype(BF16)
    outs = []
    for pair in range(GDN_W // GDN_PAIR):
        ls = slice(pair * GDN_PAIR, (pair + 1) * GDN_PAIR)
        w = jnp.concatenate([yb[:, ls], yb[:, ls]], axis=0) * bd
        dims = (((1,), (1,)), ((), ())) if transpose_rhs else (((1,), (0,)), ((), ()))
        outs.append(lax.dot_general(xb[:, ls], w, dims, preferred_element_type=F32))
    return jnp.concatenate(outs, axis=1)


def _gdn_intra(probs, bd, lvl_ref, dirm_ref, tri_ref):
    c = GDN_CHUNK
    n = len(probs)
    eye = lvl_ref[0]
    gc, g_last, decay, gram = [], [], [], []
    for q, k, v, beta, g, rev in probs:
        d = 1 if rev else 0
        gcp = jnp.dot(tri_ref[d], jnp.concatenate(_split3(g), axis=0), preferred_element_type=F32)
        gc.append(gcp)
        g_last.append(gcp[0:1, :] if rev else gcp[c - 1:c, :])
        gc_row = jnp.sum(gcp * eye, axis=0, keepdims=True)
        decay.append(dirm_ref[d, 0] * jnp.exp(jnp.minimum(gcp - gc_row, 0.0)))
        gram.append(_heads_mm(jnp.concatenate([k, q], axis=0), k, bd, transpose_rhs=True))
    lower = [dirm_ref[1 if p[5] else 0, 1] * p[3] * gram[x][:c] * decay[x] for x, p in enumerate(probs)]
    a_intra = [gram[x][c:] * decay[x] for x in range(n)]
    t = [eye - lower[x] * lvl_ref[1] for x in range(n)]
    for lev in range(2, 7):
        y = [_heads_mm(t[x], lower[x] * lvl_ref[lev], bd) for x in range(n)]
        z = [_heads_mm(y[x], t[x], bd) for x in range(n)]
        t = [t[x] - z[x] for x in range(n)]
    e_gc = [jnp.exp(gc[x]) for x in range(n)]
    u = [_heads_mm(t[x], p[2] * p[3], bd) for x, p in enumerate(probs)]
    w = [_heads_mm(t[x], p[1] * p[3] * e_gc[x], bd) for x, p in enumerate(probs)]
    wq = [jnp.concatenate([w[x], p[0] * e_gc[x]], axis=0).astype(BF16) for x, p in enumerate(probs)]
    k_dec = [(p[1] * jnp.exp(g_last[x] - gc[x])).astype(BF16) for x, p in enumerate(probs)]
    g_tot = [jnp.exp(g_last[x]) for x in range(n)]
    return u, wq, k_dec, a_intra, g_tot


def _gdn_state_step(u, wq, k_dec, a_intra, g_tot, s_ref, bd):
    c = GDN_CHUNK
    pairs = [slice(p * GDN_PAIR, (p + 1) * GDN_PAIR) for p in range(GDN_W // GDN_PAIR)]
    ws_qs = jnp.concatenate([jnp.dot(wq[:, ls], s_ref[p].astype(BF16), preferred_element_type=F32)
                             for p, ls in enumerate(pairs)], axis=1)
    v_new = u - ws_qs[:c]
    o = ws_qs[c:] + _heads_mm(a_intra, v_new, bd)
    vb = v_new.astype(BF16)
    bdf = bd.astype(F32)
    for p, ls in enumerate(pairs):
        kv = lax.dot_general(k_dec[:, ls], vb[:, ls], (((0,), (0,)), ((), ())), preferred_element_type=F32)
        s_ref[p] = s_ref[p] * g_tot[:, ls] + kv * bdf
    return o


def _gdn_scan_kernel(qf, kf, vf, bf, gf, qb, kb, vb, bb, gb, bd_ref, lvl_ref, dirm_ref, tri_ref,
                     of_ref, ob_ref, sf_ref, sb_ref):
    @pl.when(pl.program_id(1) == 0)
    def _():
        sf_ref[...] = jnp.zeros(sf_ref.shape, F32)
        sb_ref[...] = jnp.zeros(sb_ref.shape, F32)

    bd = bd_ref[...]
    n = SEQ_TILE // GDN_CHUNK
    probs, rows = [], []
    for c in range(n):
        rf = slice(c * GDN_CHUNK, (c + 1) * GDN_CHUNK)
        rb = slice((n - 1 - c) * GDN_CHUNK, (n - c) * GDN_CHUNK)
        probs.append((qf[rf, :], kf[rf, :], vf[rf, :], bf[rf, :], gf[rf, :], False))
        probs.append((qb[rb, :], kb[rb, :], vb[rb, :], bb[rb, :], gb[rb, :], True))
        rows += [rf, rb]
    u, wq, k_dec, a_intra, g_tot = _gdn_intra(probs, bd, lvl_ref, dirm_ref, tri_ref)
    for x in range(2 * n):
        rev = probs[x][5]
        o = _gdn_state_step(u[x], wq[x], k_dec[x], a_intra[x], g_tot[x], sb_ref if rev else sf_ref, bd)
        (ob_ref if rev else of_ref)[rows[x], :] = o


def _seq_tile_maps(n_lat_tiles, n_tiles):
    fwd = lambda s: lax.rem(s + n_lat_tiles, n_tiles)
    bwd = lambda s: n_tiles - 1 - s
    return fwd, bwd


def _gdn_scan(q, k, v, beta, g, n_lat):
    b, s, _ = q.shape
    n_tiles = s // SEQ_TILE
    fwd, bwd = _seq_tile_maps(n_lat // SEQ_TILE, n_tiles)
    spec = lambda tile_of, col: pl.BlockSpec((None, SEQ_TILE, GDN_W), lambda bi, t: (bi, tile_of(t), col))
    masks = _gdn_masks()
    in_specs = [spec(fwd, 0)] * 5 + [spec(bwd, 0)] * 3 + [spec(bwd, 1)] * 2 + [_full(m) for m in masks]
    return pl.pallas_call(
        _gdn_scan_kernel,
        grid=(b, n_tiles),
        in_specs=in_specs,
        out_specs=[spec(fwd, 0), spec(bwd, 0)],
        out_shape=[jax.ShapeDtypeStruct((b, s, GDN_W), F32)] * 2,
        scratch_shapes=[pltpu.VMEM((GDN_W // GDN_PAIR, GDN_PAIR, GDN_PAIR), F32)] * 2,
        compiler_params=_cparams(("parallel", "arbitrary")),
        name="gdn_scan",
    )(q, k, v, beta, g, q, k, v, beta, g, *masks)


def _lru_tile_prep(x_ref, xp_ref, xn_ref, cw_ref, cb_ref, wg_ref, bg_ref, nla_ref, a_ref, b_ref,
                   tile_idx, n_tiles, n_lat_tiles):
    xb = _tile_conv(x_ref[...], xp_ref[...], xn_ref[...], cw_ref, tile_idx, n_tiles, n_lat_tiles) + cb_ref[...]
    gates = _sigmoid(jnp.dot(xb.astype(BF16), wg_ref[...], preferred_element_type=F32) + bg_ref[...])
    log_a = nla_ref[...] * gates[:, :LRU_W]
    a_ref[...] = jnp.exp(log_a)
    th = jnp.tanh(log_a)
    b_ref[...] = jnp.sqrt(-2.0 * th / (1.0 - th)) * gates[:, LRU_W:] * xb


def _scan_group(a, b, h, row, reverse):
    for d in (1, 2, 4):
        if reverse:
            keep = row < 8 - d
            shift = 8 - d
        else:
            keep = row >= d
            shift = d
        a_s = jnp.where(keep, pltpu.roll(a, shift, 0), 1.0)
        b_s = jnp.where(keep, pltpu.roll(b, shift, 0), 0.0)
        b = a * b_s + b
        a = a * a_s
    return a * h + b


def _lru_scan_kernel(xf, xfp, xfn, xb, xbp, xbn, cw_ref, cb_ref, wgf, bgf, nlaf, wgb, bgb, nlab,
                     hf_ref, hb_ref, af_ref, bf_ref, ab_ref, bb_ref, cf_ref, cbk_ref, *, n_lat_tiles):
    s = pl.program_id(1)
    n_tiles = pl.num_programs(1)

    @pl.when(s == 0)
    def _():
        cf_ref[...] = jnp.zeros(cf_ref.shape, F32)
        cbk_ref[...] = jnp.zeros(cbk_ref.shape, F32)

    t_f, t_b = (m(s) for m in _seq_tile_maps(n_lat_tiles, n_tiles))
    _lru_tile_prep(xf, xfp, xfn, cw_ref, cb_ref, wgf, bgf, nlaf, af_ref, bf_ref, t_f, n_tiles, n_lat_tiles)
    _lru_tile_prep(xb, xbp, xbn, cw_ref, cb_ref, wgb, bgb, nlab, ab_ref, bb_ref, t_b, n_tiles, n_lat_tiles)

    n_groups = SEQ_TILE // 8
    row = lax.broadcasted_iota(jnp.int32, (8, LRU_W), 0)

    def body(gi, carry):
        h_f, h_b = carry
        rf = pl.ds(pl.multiple_of(gi * 8, 8), 8)
        rb = pl.ds(pl.multiple_of((n_groups - 1 - gi) * 8, 8), 8)
        out_f = _scan_group(af_ref[rf, :], bf_ref[rf, :], h_f, row, False)
        out_b = _scan_group(ab_ref[rb, :], bb_ref[rb, :], h_b, row, True)
        hf_ref[rf, :] = out_f
        hb_ref[rb, :] = out_b
        return (jnp.broadcast_to(out_f[7:8, :], (8, LRU_W)), jnp.broadcast_to(out_b[0:1, :], (8, LRU_W)))

    h_f, h_b = lax.fori_loop(0, n_groups, body, (cf_ref[...], cbk_ref[...]))
    cf_ref[...] = h_f
    cbk_ref[...] = h_b


def _lru_scan(p, conv_w, conv_b, w_r, b_r, w_i, b_i, lam, n_lat):
    b, s, _ = p.shape
    n_tiles = s // SEQ_TILE
    nlt = n_lat // SEQ_TILE
    fwd, bwd = _seq_tile_maps(nlt, n_tiles)
    col = P_LRU_X // LRU_W
    specs = []
    for tile_of in (fwd, bwd):
        main, prev, make_next = _halo_specs(LRU_W, col, tile_of)
        specs += [pl.BlockSpec((None, SEQ_TILE, LRU_W), main), pl.BlockSpec((None, HALO, LRU_W), prev),
                  pl.BlockSpec((None, HALO, LRU_W), make_next(n_tiles))]

    def blockdiag(w):
        return jax.scipy.linalg.block_diag(*[w[n] for n in range(LRU_BLOCKS)])

    dir_args = []
    for d in range(2):
        wg = jnp.concatenate([blockdiag(w_r[d]), blockdiag(w_i[d])], axis=1).astype(BF16)
        bg = jnp.concatenate([b_r[d], b_i[d]]).astype(F32).reshape(1, 2 * LRU_W)
        nla = (-LRU_C * jax.nn.softplus(-lam[d].astype(F32))).reshape(1, LRU_W)
        dir_args += [wg, bg, nla]
    cb2 = conv_b.reshape(1, LRU_W)
    out_spec = lambda tile_of: pl.BlockSpec((None, SEQ_TILE, LRU_W), lambda bi, t: (bi, tile_of(t), 0))
    return pl.pallas_call(
        functools.partial(_lru_scan_kernel, n_lat_tiles=nlt),
        grid=(b, n_tiles),
        in_specs=specs + [_full(conv_w), _full(cb2)] + [_full(a) for a in dir_args],
        out_specs=[out_spec(fwd), out_spec(bwd)],
        out_shape=[jax.ShapeDtypeStruct((b, s, LRU_W), F32)] * 2,
        scratch_shapes=[pltpu.VMEM((SEQ_TILE, LRU_W), F32)] * 4 + [pltpu.VMEM((8, LRU_W), F32)] * 2,
        compiler_params=_cparams(("parallel", "arbitrary")),
        name="lru_scan",
    )(p, p, p, p, p, p, conv_w, cb2, *dir_args)


def _post_kernel(of_ref, ob_ref, z_ref, gn_ref, ones_ref, hf_ref, hb_ref, y_ref, ua_ref, ub_ref):
    o = of_ref[...] + ob_ref[...]
    ms = jnp.dot(jnp.concatenate(_split3(o * o), axis=-1), ones_ref[...], preferred_element_type=F32) * (1.0 / GDN_DV)
    z = z_ref[...]
    ua_ref[...] = (o * lax.rsqrt(ms + RMS_EPS) * gn_ref[...] * (z * _sigmoid(z))).astype(ua_ref.dtype)
    y = y_ref[...]
    gelu = 0.5 * y * (1.0 + jnp.tanh(0.7978845608028654 * (y + 0.044715 * (y * y * y))))
    ub_ref[...] = ((hf_ref[...] + hb_ref[...]) * gelu).astype(ub_ref.dtype)


def _post(o_f, o_b, h_f, h_b, p, gdn_norm_g):
    b, s, _ = p.shape
    tm = _pick_tile(s, 544)
    ones3 = jnp.concatenate([_head_block_ones(GDN_HEADS, GDN_DV)] * 3, axis=0)
    gn = jnp.tile(gdn_norm_g.astype(F32), GDN_HEADS).reshape(1, GDN_W)
    tok = pl.BlockSpec((None, tm, GDN_W), lambda bi, i: (bi, i, 0))
    pcol = lambda c: pl.BlockSpec((None, tm, GDN_W), lambda bi, i: (bi, i, c))
    return pl.pallas_call(
        _post_kernel,
        grid=(b, s // tm),
        in_specs=[tok, tok, pcol(P_GDN_Z // GDN_W), _full(gn), _full(ones3), tok, tok, pcol(P_LRU_Y // LRU_W)],
        out_specs=[tok, tok],
        out_shape=[jax.ShapeDtypeStruct((b, s, GDN_W), BF16)] * 2,
        compiler_params=_cparams(("parallel", "parallel")),
        name="gdn_lru_post",
    )(o_f, o_b, p, gn, ones3, h_f, h_b, p)


def _merge_kernel(x_ref, mc_ref, ml_ref, u0_ref, u1_ref, u2_ref, u3_ref, t0_ref, t1_ref, t2_ref, t3_ref,
                  bg_ref, wb_ref, wo_ref, o_ref, *, n_lat):
    merged = None
    for n, (u_ref, t_ref) in enumerate(((u0_ref, t0_ref), (u1_ref, t1_ref), (u2_ref, t2_ref), (u3_ref, t3_ref))):
        gate = _sigmoid(t_ref[...] + bg_ref[n:n + 1, :])
        term = gate * jnp.dot(u_ref[...], wb_ref[n], preferred_element_type=F32)
        merged = term if merged is None else merged + term
    out = jnp.dot(merged.astype(BF16), wo_ref[...], preferred_element_type=F32)
    g1 = _row_select(x_ref.shape[0], pl.program_id(1), n_lat, mc_ref[2:3, :], ml_ref[2:3, :])
    o_ref[...] = x_ref[...] + g1 * out


def _merge(xs, mod_c, mod_l, branches, p, b_gate, w_branch, w_out, n_lat, with_ctx):
    b, s, d = xs.shape
    rows = s if with_ctx else n_lat
    tm = _pick_tile(rows, 544) if with_ctx else SEQ_TILE
    gate_blk = P_GATES // d
    tok = lambda w: pl.BlockSpec((None, tm, w), lambda bi, i: (bi, i, 0))
    in_specs = [tok(d), pl.BlockSpec((8, d), lambda bi, i: (0, 0)), pl.BlockSpec((None, 8, d), lambda bi, i: (bi, 0, 0))]
    in_specs += [tok(BRANCH_W)] * N_BRANCH
    in_specs += [pl.BlockSpec((None, tm, d), functools.partial(lambda bi, i, n: (bi, i, gate_blk + n), n=n))
                 for n in range(N_BRANCH)]
    in_specs += [_full(b_gate), _full(w_branch), _full(w_out)]
    return pl.pallas_call(
        functools.partial(_merge_kernel, n_lat=n_lat),
        grid=(b, rows // tm),
        in_specs=in_specs,
        out_specs=tok(d),
        out_shape=jax.ShapeDtypeStruct((b, rows, d), F32),
        input_output_aliases={0: 0} if with_ctx else {},
        compiler_params=_cparams(("parallel", "parallel")),
        name="merge",
    )(xs, mod_c, mod_l, *branches, p, p, p, p, b_gate, w_branch, w_out)


def _mlp_kernel(x_ref, mc_ref, ml_ref, gn_ref, gf_ref, w1_ref, w2_ref, o_ref, h_ref, acc_ref, *, n_lat, final_norm):
    f = pl.program_id(2)
    tm = x_ref.shape[0]
    i = pl.program_id(1)

    @pl.when(f == 0)
    def _():
        x = x_ref[...]
        y = x * lax.rsqrt(jnp.mean(x * x, axis=-1, keepdims=True) + RMS_EPS)
        y = y * gn_ref[...]
        shift = _row_select(tm, i, n_lat, mc_ref[3:4, :], ml_ref[3:4, :])
        scale = _row_select(tm, i, n_lat, mc_ref[4:5, :], ml_ref[4:5, :])
        h_ref[...] = (y * (1.0 + scale) + shift).astype(BF16)

    a = jnp.maximum(jnp.dot(h_ref[...], w1_ref[...], preferred_element_type=F32), 0.0)
    part = jnp.dot((a * a).astype(BF16), w2_ref[...], preferred_element_type=F32)

    @pl.when(f == 0)
    def _():
        acc_ref[...] = part

    @pl.when(f > 0)
    def _():
        acc_ref[...] += part

    @pl.when(f == pl.num_programs(2) - 1)
    def _():
        g2 = _row_select(tm, i, n_lat, mc_ref[5:6, :], ml_ref[5:6, :])
        y = x_ref[...] + g2 * acc_ref[...]
        if final_norm:
            y = y * lax.rsqrt(jnp.mean(y * y, axis=-1, keepdims=True) + RMS_EPS) * gf_ref[...]
        o_ref[...] = y


def _mlp(xs, mod_c, mod_l, gain, w1, w2, final_gain, n_lat, final_norm):
    b, rows, d = xs.shape
    dff = w1.shape[1]
    tm = _pick_tile(rows, 1088)
    tf = 1024
    row = pl.BlockSpec((1, d), lambda bi, i, f: (0, 0))
    return pl.pallas_call(
        functools.partial(_mlp_kernel, n_lat=n_lat, final_norm=final_norm),
        grid=(b, rows // tm, dff // tf),
        in_specs=[pl.BlockSpec((None, tm, d), lambda bi, i, f: (bi, i, 0)),
                  pl.BlockSpec((8, d), lambda bi, i, f: (0, 0)),
                  pl.BlockSpec((None, 8, d), lambda bi, i, f: (bi, 0, 0)),
                  row, row,
                  pl.BlockSpec((d, tf), lambda bi, i, f: (0, f)),
                  pl.BlockSpec((tf, d), lambda bi, i, f: (f, 0))],
        out_specs=pl.BlockSpec((None, tm, d), lambda bi, i, f: (bi, i, 0)),
        out_shape=jax.ShapeDtypeStruct((b, rows, d), F32),
        scratch_shapes=[pltpu.VMEM((tm, d), BF16), pltpu.VMEM((tm, d), F32)],
        compiler_params=_cparams(("parallel", "parallel", "arbitrary")),
        name="mlp",
    )(xs, mod_c, mod_l, gain, final_gain, w1, w2)


def kernel(x, c, ctx, c_ctx, mod_w, mod_b, norm1_g, norm2_g, w_in, b_gate, gdn_conv_w, gdn_a_log, gdn_dt_bias,
           gdn_norm_g, lru_conv_w, lru_conv_b, lru_w_r, lru_b_r, lru_w_i, lru_b_i, lru_lambda, mla_q_norm_g,
           mla_w_uq, mla_kv_norm_g, mla_w_ukv, na_rpb, w_branch, w_out, mlp_w1, mlp_w2, final_norm_g):
    bsz, n_tok, d = x.shape
    n_ctx = ctx.shape[1]
    depth = w_in.shape[0]
    assert n_ctx % SEQ_TILE == 0 and n_tok % SEQ_TILE == 0 and n_tok % GRID_W == 0
    na_meta, onehot_r, onehot_c = _na_geometry(n_tok // GRID_W)
    na_meta = jnp.asarray(na_meta)
    cos, sin = _rope_tables(n_tok, n_ctx)

    n_rows = -(-(bsz + 1) // 8) * 8
    cc = jnp.zeros((n_rows, d), F32).at[:bsz].set(c).at[bsz].set(c_ctx)
    final_gain = final_norm_g.reshape(1, d)

    xs = jnp.concatenate([x, ctx], axis=1)
    for l in range(depth):
        need_ctx = l < depth - 1
        mod = _modulation(cc, mod_w[l], mod_b[l]).reshape(n_rows, N_MOD, d)
        pad = jnp.zeros((8 - N_MOD, d), F32)
        mod_c = jnp.concatenate([mod[bsz], pad], axis=0)
        mod_l = jnp.concatenate([mod[:bsz], jnp.broadcast_to(pad, (bsz, 8 - N_MOD, d))], axis=1)

        w_in_l = _arrange_w_in(w_in[l])
        wq, wk, wv, place = _arrange_mla(mla_w_uq[l], mla_w_ukv[l])
        gq = mla_q_norm_g[l].reshape(1, -1)
        gkv = mla_kv_norm_g[l].reshape(1, -1)
        g1n = norm1_g[l].reshape(1, d)
        g2n = norm2_g[l].reshape(1, d)
        wb = w_branch[l].astype(BF16)
        wo = w_out[l].astype(BF16)
        w1 = mlp_w1[l].astype(BF16)
        w2 = mlp_w2[l].astype(BF16)

        p32, p16 = _inproj(xs, mod_c, mod_l, g1n, w_in_l, n_tok)

        gq_, gk_, gv_, gbeta, gg = _gdn_prep(p32, gdn_conv_w[l], gdn_a_log[l], gdn_dt_bias[l], n_tok)
        o_f, o_b = _gdn_scan(gq_, gk_, gv_, gbeta, gg, n_tok)
        h_f, h_b = _lru_scan(p32, lru_conv_w[l], lru_conv_b[l], lru_w_r[l], lru_b_r[l], lru_w_i[l], lru_b_i[l],
                             lru_lambda[l], n_tok)
        ua, ub = _post(o_f, o_b, h_f, h_b, p32, gdn_norm_g[l])

        mq, mk, mv = _mla_prep(p32, p16, cos, sin, gq, gkv, wq, wk, wv, place)
        uc = _mla_flash(mq, mk, mv, n_tok)
        if need_ctx:
            uc = _mla_flash(mq, mk, mv, n_tok, prev=uc)
        ud = _na_attention(p16, na_meta, _na_bias_table(na_rpb[l], onehot_r, onehot_c), n_tok, need_ctx)

        xs = _merge(xs, mod_c, mod_l, (ua, ub, uc, ud), p16, b_gate[l], wb, wo, n_tok, need_ctx)
        xs = _mlp(xs, mod_c, mod_l, g2n, w1, w2, final_gain, n_tok, l == depth - 1)
    return xs
```

```python
import functools

import jax
import jax.numpy as jnp
import numpy as np
from jax import lax
from jax.experimental import pallas as pl
from jax.experimental.pallas import tpu as pltpu

F32 = jnp.float32
BF16 = jnp.bfloat16

GRID_W = 64
N_MOD = 6
RMS_EPS = 1e-6
GDN_HEADS = 4
GDN_DK = 64
GDN_DV = 64
GDN_CHUNK = 64
GDN_W = GDN_HEADS * GDN_DV
GDN_CONV = 4
LRU_W = 256
LRU_BLOCKS = 4
LRU_BLOCK_W = LRU_W // LRU_BLOCKS
LRU_CONV = 4
LRU_C = 8.0
MLA_HEADS = 4
MLA_Q_RANK = 256
MLA_KV_RANK = 128
MLA_NOPE = 64
MLA_ROPE = 32
MLA_V = 64
MLA_SLOT = 128
ROPE_BASE = 10000.0
LOG2_E = 1.4426950408889634
NA_HEADS = 4
NA_DH = 64
NA_W = NA_HEADS * NA_DH
NA_WIN_ROWS = 8
NA_WIN_COLS = 16
N_BRANCH = 4
BRANCH_W = 256

SEQ_TILE = 256
HALO = 8

_REF_COLS = {}
_off = 0
for _name, _w in (('gdn_qkv', 3 * GDN_W), ('gdn_z', GDN_W), ('gdn_beta', 2 * GDN_HEADS), ('gdn_a', 2 * GDN_HEADS),
                  ('lru_x', LRU_W), ('lru_y', LRU_W), ('mla_q', MLA_Q_RANK), ('mla_kv', MLA_KV_RANK),
                  ('mla_kr', MLA_ROPE), ('na_qkv', 3 * NA_W)):
    _REF_COLS[_name] = (_off, _w)
    _off += _w
N_MIX_COLS = _off

P_GDN_QKV = 0
P_GDN_Z = 768
P_LRU_X = 1024
P_LRU_Y = 1280
P_MLA_KV = 1536
P_MLA_KR = 1664
P_GDN_BA = 1792
P32_COLS = 2048
P_MLA_Q = 0
P_NA_QKV = 256
P_GATES = 1024
P16_COLS = P_GATES + N_BRANCH * 1024
PROJ_TILE = 1024

VMEM_LIMIT = 52 * 1024 * 1024


def _cparams(sem):
    return pltpu.CompilerParams(dimension_semantics=sem, vmem_limit_bytes=VMEM_LIMIT)


def _pick_tile(n, cap):
    best = 8
    for t in range(8, min(n, cap) + 1, 8):
        if n % t == 0:
            best = t
    return best


def _full(a):
    return pl.BlockSpec(a.shape, lambda *_: (0,) * a.ndim)


def _split3(x):
    hi = x.astype(BF16)
    r = x - hi.astype(F32)
    mid = r.astype(BF16)
    lo = (r - mid.astype(F32)).astype(BF16)
    return hi, mid, lo


def _sigmoid(x):
    return 0.5 * jnp.tanh(0.5 * x) + 0.5


def _arrange_w_in(w_in):
    d = w_in.shape[0]
    pieces, pos = [], 0

    def put(dst, block):
        nonlocal pos
        if dst > pos:
            pieces.append(jnp.zeros((d, dst - pos), w_in.dtype))
        pieces.append(block)
        pos = dst + block.shape[1]

    ref = lambda name: w_in[:, _REF_COLS[name][0]:_REF_COLS[name][0] + _REF_COLS[name][1]]
    for name, dst in (('gdn_qkv', P_GDN_QKV), ('gdn_z', P_GDN_Z), ('lru_x', P_LRU_X), ('lru_y', P_LRU_Y),
                      ('mla_kv', P_MLA_KV), ('mla_kr', P_MLA_KR), ('gdn_beta', P_GDN_BA),
                      ('gdn_a', P_GDN_BA + 2 * GDN_HEADS)):
        put(dst, ref(name))
    put(P32_COLS + P_MLA_Q, ref('mla_q'))
    put(P32_COLS + P_NA_QKV, ref('na_qkv'))
    put(P32_COLS + P_GATES, w_in[:, N_MIX_COLS:])
    assert pos == P32_COLS + P16_COLS
    return jnp.concatenate(pieces, axis=1).astype(BF16)


def _rope_perm():
    q = MLA_ROPE // 4
    src = np.zeros(MLA_ROPE, np.int32)
    sign = np.zeros(MLA_ROPE, np.float32)
    for base in (0, 2 * q):
        for d in range(q):
            src[base + d] = base + d + q
            sign[base + d] = -1.0
            src[base + q + d] = base + d
            sign[base + q + d] = 1.0
    return src, sign


def _arrange_mla(w_uq, w_ukv):
    src, sign = _rope_perm()
    hq = MLA_NOPE + MLA_ROPE
    wq = jnp.zeros((MLA_Q_RANK, 2 * MLA_HEADS * MLA_SLOT), F32)
    wk = jnp.zeros((MLA_KV_RANK, MLA_HEADS * MLA_SLOT), F32)
    wv = jnp.zeros((MLA_KV_RANK, MLA_HEADS * MLA_V), F32)
    place = np.zeros((2 * MLA_SLOT, 2 * MLA_HEADS * MLA_SLOT), np.float32)
    rot_off = MLA_HEADS * MLA_SLOT
    for h in range(MLA_HEADS):
        nope = w_uq[:, h * hq:h * hq + MLA_NOPE]
        pe = w_uq[:, h * hq + MLA_NOPE:(h + 1) * hq]
        s = h * MLA_SLOT
        wq = wq.at[:, s:s + MLA_NOPE].set(nope)
        wq = wq.at[:, s + MLA_NOPE:s + MLA_NOPE + MLA_ROPE].set(pe)
        wq = wq.at[:, rot_off + s + MLA_NOPE:rot_off + s + MLA_NOPE + MLA_ROPE].set(pe[:, src] * sign)
        wk = wk.at[:, s:s + MLA_NOPE].set(w_ukv[:, h * 128:h * 128 + MLA_NOPE])
        wv = wv.at[:, h * MLA_V:(h + 1) * MLA_V].set(w_ukv[:, h * 128 + MLA_NOPE:(h + 1) * 128])
        for d in range(MLA_ROPE):
            for half in (0, MLA_SLOT):
                place[half + d, s + MLA_NOPE + d] = 1.0
                place[half + src[d], rot_off + s + MLA_NOPE + d] = sign[d]
    return wq.astype(BF16), wk.astype(BF16), wv.astype(BF16), jnp.asarray(place, BF16)


def _rope_tables(n_tok, n_ctx):
    cos = np.ones((n_tok + n_ctx, MLA_SLOT), np.float32)
    sin = np.zeros((n_tok + n_ctx, MLA_SLOT), np.float32)
    t = np.arange(n_tok)
    row = (t // GRID_W).astype(np.float32)
    col = (t % GRID_W).astype(np.float32)
    n_freq = MLA_ROPE // 4
    inv = (ROPE_BASE ** (-np.arange(n_freq, dtype=np.float32) / n_freq)).astype(np.float32)
    ar = row[:, None] * inv
    ac = col[:, None] * inv
    ang = np.concatenate([ar, ar, ac, ac], axis=-1).astype(np.float32)
    cos[:n_tok, MLA_NOPE:MLA_NOPE + MLA_ROPE] = np.cos(ang)
    sin[:n_tok, MLA_NOPE:MLA_NOPE + MLA_ROPE] = np.sin(ang)
    return jnp.asarray(cos), jnp.asarray(sin)


def _head_block_ones(n_heads, width):
    m = np.kron(np.eye(n_heads, dtype=np.float32), np.ones((width, width), np.float32))
    return jnp.asarray(m, BF16)


def _mod_kernel(c_ref, w_ref, b_ref, o_ref):
    c = c_ref[...]
    s = c * _sigmoid(c)
    o_ref[...] = jnp.dot(s, w_ref[...], preferred_element_type=F32) + b_ref[...]


def _modulation(cc, mod_w, mod_b):
    r, d = cc.shape
    n = mod_w.shape[1]
    tn = 1024
    return pl.pallas_call(
        _mod_kernel,
        grid=(n // tn,),
        in_specs=[pl.BlockSpec((r, d), lambda j: (0, 0)),
                  pl.BlockSpec((d, tn), lambda j: (0, j)),
                  pl.BlockSpec((1, tn), lambda j: (0, j))],
        out_specs=pl.BlockSpec((r, tn), lambda j: (0, j)),
        out_shape=jax.ShapeDtypeStruct((r, n), F32),
        compiler_params=_cparams(("arbitrary",)),
        name="modulation",
    )(cc, mod_w, mod_b.reshape(1, n))


def _row_select(tile_rows, tile_idx, n_lat, ctx_vec, lat_vec):
    row = tile_idx * tile_rows + lax.broadcasted_iota(jnp.int32, (tile_rows, 1), 0)
    return jnp.where(row < n_lat, lat_vec, ctx_vec)


def _inproj_kernel(x_ref, mc_ref, ml_ref, g_ref, w_ref, o32_ref, o16_ref, h_ref, *, n_lat):
    j = pl.program_id(2)
    n32 = P32_COLS // PROJ_TILE

    @pl.when(j == 0)
    def _():
        tm = x_ref.shape[0]
        i = pl.program_id(1)
        x = x_ref[...]
        y = x * lax.rsqrt(jnp.mean(x * x, axis=-1, keepdims=True) + RMS_EPS)
        y = y * g_ref[...]
        shift = _row_select(tm, i, n_lat, mc_ref[0:1, :], ml_ref[0:1, :])
        scale = _row_select(tm, i, n_lat, mc_ref[1:2, :], ml_ref[1:2, :])
        h_ref[...] = (y * (1.0 + scale) + shift).astype(BF16)

    acc = jnp.dot(h_ref[...], w_ref[...], preferred_element_type=F32)

    @pl.when(j < n32)
    def _():
        o32_ref[...] = acc

    @pl.when(j >= n32)
    def _():
        o16_ref[...] = acc.astype(BF16)


def _inproj(xs, mod_c, mod_l, gain, w, n_lat):
    b, s, d = xs.shape
    tm = _pick_tile(s, 1088)
    tn = PROJ_TILE
    n32 = P32_COLS // tn
    return pl.pallas_call(
        functools.partial(_inproj_kernel, n_lat=n_lat),
        grid=(b, s // tm, (P32_COLS + P16_COLS) // tn),
        in_specs=[pl.BlockSpec((None, tm, d), lambda bi, i, j: (bi, i, 0)),
                  pl.BlockSpec((8, d), lambda bi, i, j: (0, 0)),
                  pl.BlockSpec((None, 8, d), lambda bi, i, j: (bi, 0, 0)),
                  pl.BlockSpec((1, d), lambda bi, i, j: (0, 0)),
                  pl.BlockSpec((d, tn), lambda bi, i, j: (0, j))],
        out_specs=[pl.BlockSpec((None, tm, tn), lambda bi, i, j: (bi, i, jnp.minimum(j, n32 - 1))),
                   pl.BlockSpec((None, tm, tn), lambda bi, i, j: (bi, i, jnp.maximum(j - n32, 0)))],
        out_shape=[jax.ShapeDtypeStruct((b, s, P32_COLS), F32), jax.ShapeDtypeStruct((b, s, P16_COLS), BF16)],
        scratch_shapes=[pltpu.VMEM((tm, d), BF16)],
        compiler_params=_cparams(("parallel", "parallel", "arbitrary")),
        name="inproj",
    )(xs, mod_c, mod_l, gain, w)


def _mla_prep_kernel(ql_ref, kv_ref, cos_ref, sin_ref, gq_ref, gkv_ref, wq_ref, wk_ref, wv_ref, pl_ref,
                     q_ref, k_ref, v_ref, *, scale):
    nslot = MLA_HEADS * MLA_SLOT
    cos = jnp.concatenate([cos_ref[...]] * MLA_HEADS, axis=-1)
    sin = jnp.concatenate([sin_ref[...]] * MLA_HEADS, axis=-1)

    ql = ql_ref[...].astype(F32)
    qn = ql * lax.rsqrt(jnp.mean(ql * ql, axis=-1, keepdims=True) + RMS_EPS) * gq_ref[...]
    q2 = jnp.dot(qn.astype(BF16), wq_ref[...], preferred_element_type=F32)
    q = (q2[:, :nslot] * cos + q2[:, nslot:] * sin) * scale
    q_ref[...] = q.astype(BF16)

    kvkr = kv_ref[...]
    kvl = kvkr[:, :MLA_KV_RANK]
    kr = kvkr[:, MLA_KV_RANK:]
    kvn = (kvl * lax.rsqrt(jnp.mean(kvl * kvl, axis=-1, keepdims=True) + RMS_EPS) * gkv_ref[...]).astype(BF16)
    kn = jnp.dot(kvn, wk_ref[...], preferred_element_type=F32)
    v_ref[...] = jnp.dot(kvn, wv_ref[...], preferred_element_type=F32).astype(BF16)
    kr_hi = kr.astype(BF16)
    kr_lo = (kr - kr_hi.astype(F32)).astype(BF16)
    kr2 = jnp.dot(jnp.concatenate([kr_hi, kr_lo], axis=-1), pl_ref[...], preferred_element_type=F32)
    k_ref[...] = (kn + kr2[:, :nslot] * cos + kr2[:, nslot:] * sin).astype(BF16)


def _mla_prep(p32, p16, cos, sin, gq, gkv, wq, wk, wv, place):
    b, s, _ = p32.shape
    tm = _pick_tile(s, 544)
    nslot = MLA_HEADS * MLA_SLOT
    scale = (MLA_NOPE + MLA_ROPE) ** -0.5 * LOG2_E
    return pl.pallas_call(
        functools.partial(_mla_prep_kernel, scale=scale),
        grid=(b, s // tm),
        in_specs=[pl.BlockSpec((None, tm, MLA_Q_RANK), lambda bi, i: (bi, i, P_MLA_Q // MLA_Q_RANK)),
                  pl.BlockSpec((None, tm, 2 * MLA_SLOT), lambda bi, i: (bi, i, P_MLA_KV // (2 * MLA_SLOT))),
                  pl.BlockSpec((tm, MLA_SLOT), lambda bi, i: (i, 0)),
                  pl.BlockSpec((tm, MLA_SLOT), lambda bi, i: (i, 0)),
                  _full(gq), _full(gkv), _full(wq), _full(wk), _full(wv), _full(place)],
        out_specs=[pl.BlockSpec((None, tm, nslot), lambda bi, i: (bi, i, 0)),
                   pl.BlockSpec((None, tm, nslot), lambda bi, i: (bi, i, 0)),
                   pl.BlockSpec((None, tm, MLA_HEADS * MLA_V), lambda bi, i: (bi, i, 0))],
        out_shape=[jax.ShapeDtypeStruct((b, s, nslot), BF16),
                   jax.ShapeDtypeStruct((b, s, nslot), BF16),
                   jax.ShapeDtypeStruct((b, s, MLA_HEADS * MLA_V), BF16)],
        compiler_params=_cparams(("parallel", "parallel")),
        name="mla_prep",
    )(p16, p32, cos, sin, gq, gkv, wq, wk, wv, place)


def _flash_softmax(h, s, m_ref, l_ref):
    m_prev = m_ref[h]
    m_new = jnp.maximum(m_prev, jnp.max(s, axis=-1, keepdims=True))
    alpha = jnp.exp2(m_prev - m_new)
    p = jnp.exp2(s - jnp.concatenate([m_new] * (s.shape[1] // 128), axis=-1))
    l_ref[h] = alpha * l_ref[h] + jnp.sum(p, axis=-1, keepdims=True)
    m_ref[h] = m_new
    return alpha, p.astype(BF16)


def _mla_flash_kernel(*refs, tk, aliased):
    if aliased:
        q_ref, k_ref, v_ref, _, o_ref, m_ref, l_ref, acc_ref = refs
    else:
        q_ref, k_ref, v_ref, o_ref, m_ref, l_ref, acc_ref = refs
    tq = q_ref.shape[0]
    nk = k_ref.shape[0]
    n_loop = nk // tk
    m_ref[...] = jnp.full(m_ref.shape, -jnp.inf, F32)
    l_ref[...] = jnp.zeros(l_ref.shape, F32)
    acc_ref[...] = jnp.zeros(acc_ref.shape, F32)

    def chunk(rows):
        heads = range(MLA_HEADS)
        hs = [slice(h * MLA_SLOT, (h + 1) * MLA_SLOT) for h in heads]
        vs = [slice((h // 2) * 128, (h // 2) * 128 + 128) for h in heads]
        s = [lax.dot_general(q_ref[:, hs[h]], k_ref[rows, hs[h]], (((1,), (1,)), ((), ())),
                             preferred_element_type=F32) for h in heads]
        ap = [_flash_softmax(h, s[h], m_ref, l_ref) for h in heads]
        for h in heads:
            alpha, p = ap[h]
            acc_ref[h] = alpha * acc_ref[h] + jnp.dot(p, v_ref[rows, vs[h]], preferred_element_type=F32)

    if n_loop:
        def body(j, carry):
            chunk(pl.ds(pl.multiple_of(j * tk, tk), tk))
            return carry
        lax.fori_loop(0, n_loop, body, 0)
    if nk > n_loop * tk:
        chunk(slice(n_loop * tk, nk))

    lane = lax.broadcasted_iota(jnp.int32, (tq, 128), 1)
    outs = []
    for pair in range(MLA_HEADS // 2):
        o0 = acc_ref[2 * pair] / l_ref[2 * pair]
        o1 = acc_ref[2 * pair + 1] / l_ref[2 * pair + 1]
        outs.append(jnp.where(lane < MLA_V, o0, o1))
    o_ref[...] = jnp.concatenate(outs, axis=-1).astype(o_ref.dtype)


def _mla_flash(q, k, v, n_lat, prev=None):
    b, s, nslot = q.shape
    n_ctx = s - n_lat
    nv = MLA_HEADS * MLA_V
    tk = 512
    if prev is None:
        tq = _pick_tile(n_lat, 512)
        q_off, n_q, kv_rows, kv_blk = 0, n_lat // tq, s, 0
    else:
        assert n_lat % n_ctx == 0
        tq = _pick_tile(n_ctx, 256)
        q_off, n_q, kv_rows, kv_blk = n_lat // tq, n_ctx // tq, n_ctx, n_lat // n_ctx
    in_specs = [pl.BlockSpec((None, tq, nslot), lambda bi, i: (bi, i + q_off, 0)),
                pl.BlockSpec((None, kv_rows, nslot), lambda bi, i: (bi, kv_blk, 0)),
                pl.BlockSpec((None, kv_rows, nv), lambda bi, i: (bi, kv_blk, 0))]
    args = [q, k, v]
    aliases = {}
    if prev is not None:
        in_specs.append(pl.BlockSpec(memory_space=pl.ANY))
        args.append(prev)
        aliases = {3: 0}
    return pl.pallas_call(
        functools.partial(_mla_flash_kernel, tk=tk, aliased=prev is not None),
        grid=(b, n_q),
        in_specs=in_specs,
        out_specs=pl.BlockSpec((None, tq, nv), lambda bi, i: (bi, i + q_off, 0)),
        out_shape=jax.ShapeDtypeStruct((b, s, nv), BF16),
        scratch_shapes=[pltpu.VMEM((MLA_HEADS, tq, 128), F32),
                        pltpu.VMEM((MLA_HEADS, tq, 128), F32),
                        pltpu.VMEM((MLA_HEADS, tq, 128), F32)],
        input_output_aliases=aliases,
        compiler_params=_cparams(("parallel", "arbitrary")),
        name="mla_flash",
    )(*args)


NA_QROWS = 4
NA_SLAB = NA_QROWS + NA_WIN_ROWS


def _na_geometry(rows):
    assert rows % NA_QROWS == 0 and rows >= NA_SLAB
    nblk = rows // NA_QROWS
    qc = np.arange(GRID_W)
    cs = np.clip(qc - NA_WIN_COLS // 2, 0, GRID_W - NA_WIN_COLS)
    col_valid = (qc[None, :] >= cs[:, None]) & (qc[None, :] < cs[:, None] + NA_WIN_COLS)
    rel_c = np.clip(qc[None, :] - qc[:, None] + NA_WIN_COLS - 1, 0, 2 * NA_WIN_COLS - 2)
    onehot_c = (rel_c[None] == np.arange(2 * NA_WIN_COLS - 1)[:, None, None]) & col_valid[None]
    patterns, cls, starts = {}, [], []
    for i in range(nblk):
        r0 = i * NA_QROWS
        start = int(np.clip(r0 - NA_WIN_ROWS // 2, 0, rows - NA_SLAB))
        qr = r0 + np.arange(NA_QROWS)
        rs = np.clip(qr - NA_WIN_ROWS // 2, 0, rows - NA_WIN_ROWS)
        key = (start - r0,) + tuple((rs - r0).tolist())
        if key not in patterns:
            kr = start + np.arange(NA_SLAB)
            row_valid = (kr[None, :] >= rs[:, None]) & (kr[None, :] < rs[:, None] + NA_WIN_ROWS)
            rel_r = np.clip(kr[None, :] - qr[:, None] + NA_WIN_ROWS - 1, 0, 2 * NA_WIN_ROWS - 2)
            onehot_r = (rel_r[..., None] == np.arange(2 * NA_WIN_ROWS - 1)) & row_valid[..., None]
            patterns[key] = (len(patterns), onehot_r)
        cls.append(patterns[key][0])
        starts.append(start)
    ordered = sorted(patterns.values(), key=lambda z: z[0])
    onehot_r = np.stack([z[1] for z in ordered]).astype(np.float32)
    meta = np.stack([np.asarray(cls, np.int32), np.asarray(starts, np.int32)])
    return meta, onehot_r, onehot_c.astype(np.float32)


def _na_bias_table(rpb, onehot_r, onehot_c):
    hi = lax.Precision.HIGHEST
    bc = jnp.einsum('hrc,cqk->hrqk', rpb.astype(F32), onehot_c, precision=hi)
    tab = jnp.einsum('pabr,hrqk->phaqbk', onehot_r, bc, precision=hi)
    valid = jnp.einsum('pabr,cqk->paqbk', onehot_r, onehot_c, precision=hi) > 0.5
    tab = jnp.where(valid[:, None], tab, -jnp.inf)
    p, h = tab.shape[:2]
    return tab.reshape(p, h, NA_QROWS * GRID_W, NA_SLAB * GRID_W)


def _na_scores(q_pair, lane, h, k_parts):
    in_head = (lane < NA_DH) if h % 2 == 0 else (lane >= NA_DH)
    qm = jnp.where(in_head, q_pair * (NA_DH ** -0.5), 0.0).astype(BF16)
    return [lax.dot_general(qm, k, (((1,), (1,)), ((), ())), preferred_element_type=F32) for k in k_parts]


def _na_softmax(scores, bias):
    if bias is not None:
        scores = [scores[0] + bias] + scores[1:]
    m = scores[0].max(axis=-1, keepdims=True)
    for s in scores[1:]:
        m = jnp.maximum(m, s.max(axis=-1, keepdims=True))
    probs = [jnp.exp(s - m) for s in scores]
    den = probs[0].sum(axis=-1, keepdims=True)
    for p in probs[1:]:
        den = den + p.sum(axis=-1, keepdims=True)
    return [p.astype(BF16) for p in probs], den


def _na_kernel(meta_ref, q_ref, k_ref, v_ref, bias_ref, o_ref, *, n_lat, with_ctx):
    i = pl.program_id(1)
    nq = q_ref.shape[0]
    n_lat_tiles = n_lat // nq
    n_all = k_ref.shape[0]
    lane = lax.broadcasted_iota(jnp.int32, (nq, 128), 1)

    def run(windowed):
        key_rows = [pl.ds(n_lat, n_all - n_lat)]
        if windowed:
            start = pl.multiple_of(meta_ref[1, jnp.minimum(i, n_lat_tiles - 1)] * GRID_W, GRID_W)
            key_rows = [pl.ds(start, NA_SLAB * GRID_W)] + key_rows
        pair_lanes = [slice((h // 2) * 128, (h // 2 + 1) * 128) for h in range(NA_HEADS)]
        scores = [_na_scores(q_ref[:, pair_lanes[h]], lane, h, [k_ref[r, pair_lanes[h]] for r in key_rows])
                  for h in range(NA_HEADS)]
        soft = [_na_softmax(scores[h], bias_ref[h] if windowed else None) for h in range(NA_HEADS)]
        heads = []
        for h in range(NA_HEADS):
            probs, den = soft[h]
            out = None
            for p, r in zip(probs, key_rows):
                po = jnp.dot(p, v_ref[r, pair_lanes[h]], preferred_element_type=F32)
                out = po if out is None else out + po
            heads.append(out / den)
        outs = [jnp.where(lane < NA_DH, heads[2 * pair], heads[2 * pair + 1]) for pair in range(NA_HEADS // 2)]
        o_ref[...] = jnp.concatenate(outs, axis=-1).astype(o_ref.dtype)

    if with_ctx:
        pl.when(i < n_lat_tiles)(lambda: run(True))
        pl.when(i >= n_lat_tiles)(lambda: run(False))
    else:
        run(True)


def _na_attention(p, meta, table, n_lat, with_ctx_queries):
    b, s, _ = p.shape
    nq = NA_QROWS * GRID_W
    nk = NA_SLAB * GRID_W
    assert (s - n_lat) % nq == 0
    nlt = n_lat // nq
    qb = P_NA_QKV // NA_W
    grid_spec = pltpu.PrefetchScalarGridSpec(
        num_scalar_prefetch=1,
        grid=(b, s // nq if with_ctx_queries else nlt),
        in_specs=[pl.BlockSpec((None, nq, NA_W), lambda bi, i, m: (bi, i, qb)),
                  pl.BlockSpec((None, s, NA_W), lambda bi, i, m: (bi, 0, qb + 1)),
                  pl.BlockSpec((None, s, NA_W), lambda bi, i, m: (bi, 0, qb + 2)),
                  pl.BlockSpec((None, NA_HEADS, nq, nk),
                               lambda bi, i, m: (m[0, jnp.minimum(i, nlt - 1)], 0, 0, 0))],
        out_specs=pl.BlockSpec((None, nq, NA_W), lambda bi, i, m: (bi, i, 0)))
    return pl.pallas_call(
        functools.partial(_na_kernel, n_lat=n_lat, with_ctx=with_ctx_queries),
        grid_spec=grid_spec,
        out_shape=jax.ShapeDtypeStruct((b, s, NA_W), BF16),
        compiler_params=_cparams(("parallel", "arbitrary")),
        name="na_attention",
    )(meta, p, p, p, table)


def _tile_conv(x, prev, nxt, w_ref, tile_idx, n_tiles, n_lat_tiles):
    r = x.shape[0]
    width = w_ref.shape[0]
    left = width // 2
    has_prev = jnp.logical_and(tile_idx != 0, tile_idx != n_lat_tiles)
    has_next = jnp.logical_and(tile_idx != n_tiles - 1, tile_idx != n_lat_tiles - 1)
    prev = jnp.where(has_prev, prev, 0.0)
    nxt = jnp.where(has_next, nxt, 0.0)
    xe = jnp.concatenate([prev, x, nxt], axis=0)
    acc = None
    for j in range(width):
        o = HALO - left + j
        term = xe[o:o + r, :] * w_ref[j:j + 1, :]
        acc = term if acc is None else acc + term
    return acc


def _halo_specs(width, col_block, tile_of):
    per = SEQ_TILE // HALO

    def main(bi, s, *_):
        return (bi, tile_of(s), col_block)

    def prev(bi, s, *_):
        return (bi, jnp.maximum(tile_of(s) * per - 1, 0), col_block)

    def make_next(n_tiles):
        def nxt(bi, s, *_):
            return (bi, jnp.minimum((tile_of(s) + 1) * per, n_tiles * per - 1), col_block)
        return nxt

    return main, prev, make_next


def _gdn_prep_kernel(x_ref, xp_ref, xn_ref, ba_ref, cw_ref, ones_ref, exp_ref, alog_ref, dtb_ref,
                     q_ref, k_ref, v_ref, beta_ref, g_ref, *, n_lat_tiles):
    i = pl.program_id(1)
    y = _tile_conv(x_ref[...], xp_ref[...], xn_ref[...], cw_ref, i, pl.num_programs(1), n_lat_tiles)
    y = y * _sigmoid(y)
    q = y[:, :GDN_W]
    k = y[:, GDN_W:2 * GDN_W]
    v_ref[...] = y[:, 2 * GDN_W:].astype(v_ref.dtype)

    def head_norm(u):
        parts = jnp.concatenate(_split3(u * u), axis=-1)
        ss = jnp.dot(parts, ones_ref[...], preferred_element_type=F32)
        return u * lax.rsqrt(ss + RMS_EPS)

    q_ref[...] = (head_norm(q) * (GDN_DK ** -0.5)).astype(q_ref.dtype)
    k_ref[...] = head_norm(k).astype(k_ref.dtype)

    logits = jnp.dot(jnp.concatenate(_split3(ba_ref[...]), axis=-1), exp_ref[...], preferred_element_type=F32)
    half = 2 * GDN_W
    beta_ref[...] = _sigmoid(logits[:, :half]).astype(beta_ref.dtype)
    a = logits[:, half:] + dtb_ref[...]
    softplus = jnp.maximum(a, 0.0) + jnp.log1p(jnp.exp(-jnp.abs(a)))
    g_ref[...] = -jnp.exp(alog_ref[...]) * softplus


def _gdn_prep(p, conv_w, a_log, dt_bias, n_lat):
    b, s, _ = p.shape
    n_tiles = s // SEQ_TILE
    main, prev, make_next = _halo_specs(3 * GDN_W, 0, lambda t: t)
    ones3 = jnp.concatenate([_head_block_ones(GDN_HEADS, GDN_DK)] * 3, axis=0)
    expand = np.zeros((128, 4 * GDN_W), np.float32)
    for kind in range(2):
        for d in range(2):
            for h in range(GDN_HEADS):
                c0 = kind * 2 * GDN_W + d * GDN_W + h * GDN_DV
                expand[kind * 2 * GDN_HEADS + d * GDN_HEADS + h, c0:c0 + GDN_DV] = 1.0
    expand3 = jnp.asarray(np.concatenate([expand] * 3, axis=0), BF16)
    alog_e = jnp.repeat(a_log.astype(F32).reshape(-1), GDN_DV).reshape(1, 2 * GDN_W)
    dtb_e = jnp.repeat(dt_bias.astype(F32).reshape(-1), GDN_DV).reshape(1, 2 * GDN_W)
    tok = lambda w: pl.BlockSpec((None, SEQ_TILE, w), lambda bi, i: (bi, i, 0))
    return pl.pallas_call(
        functools.partial(_gdn_prep_kernel, n_lat_tiles=n_lat // SEQ_TILE),
        grid=(b, n_tiles),
        in_specs=[pl.BlockSpec((None, SEQ_TILE, 3 * GDN_W), main),
                  pl.BlockSpec((None, HALO, 3 * GDN_W), prev),
                  pl.BlockSpec((None, HALO, 3 * GDN_W), make_next(n_tiles)),
                  pl.BlockSpec((None, SEQ_TILE, 128), lambda bi, i: (bi, i, P_GDN_BA // 128)),
                  _full(conv_w), _full(ones3), _full(expand3), _full(alog_e), _full(dtb_e)],
        out_specs=[tok(GDN_W), tok(GDN_W), tok(GDN_W), tok(2 * GDN_W), tok(2 * GDN_W)],
        out_shape=[jax.ShapeDtypeStruct((b, s, GDN_W), BF16)] * 3 + [jax.ShapeDtypeStruct((b, s, 2 * GDN_W), BF16),
                                                                    jax.ShapeDtypeStruct((b, s, 2 * GDN_W), F32)],
        compiler_params=_cparams(("parallel", "parallel")),
        name="gdn_prep",
    )(p, p, p, p, conv_w, ones3, expand3, alog_e, dtb_e)


GDN_PAIR = 2 * GDN_DK


def _gdn_masks():
    c, w = GDN_CHUNK, GDN_W
    r2, c2 = np.arange(GDN_PAIR)[:, None], np.arange(GDN_PAIR)[None, :]
    bd = ((r2 // c) == (c2 // c)).astype(np.float32)
    i = np.arange(c)[:, None]
    j = (np.arange(w) % c)[None, :]
    level = np.zeros((c, w), np.int32)
    for bit in range(6):
        level += ((i ^ j) >= (1 << bit)).astype(np.int32)
    lvl = np.stack([(level == m).astype(np.float32) for m in range(7)])
    dirm = np.stack([np.stack([(j <= i), (j < i)]), np.stack([(j >= i), (j > i)])]).astype(np.float32)
    tj = (np.arange(3 * c) % c)[None, :]
    tri = np.stack([(tj <= i), (tj >= i)]).astype(np.float32)
    return jnp.asarray(bd, BF16), jnp.asarray(lvl), jnp.asarray(dirm), jnp.asarray(tri, BF16)


def _heads_mm(x, y, bd, transpose_rhs=False):
    xb = x.astype(BF16)
    yb = y.astype(BF16)
    outs = []
    for pair in range(GDN_W // GDN_PAIR):
        ls = slice(pair * GDN_PAIR, (pair + 1) * GDN_PAIR)
        w = jnp.concatenate([yb[:, ls], yb[:, ls]], axis=0) * bd
        dims = (((1,), (1,)), ((), ())) if transpose_rhs else (((1,), (0,)), ((), ()))
        outs.append(lax.dot_general(xb[:, ls], w, dims, preferred_element_type=F32))
    return jnp.concatenate(outs, axis=1)


def _gdn_intra(probs, bd, lvl_ref, dirm_ref, tri_ref):
    c = GDN_CHUNK
    n = len(probs)
    eye = lvl_ref[0]
    gc, g_last, decay, gram = [], [], [], []
    for q, k, v, beta, g, rev in probs:
        d = 1 if rev else 0
        gcp = jnp.dot(tri_ref[d], jnp.concatenate(_split3(g), axis=0), preferred_element_type=F32)
        gc.append(gcp)
        g_last.append(gcp[0:1, :] if rev else gcp[c - 1:c, :])
        gc_row = jnp.sum(gcp * eye, axis=0, keepdims=True)
        decay.append(dirm_ref[d, 0] * jnp.exp(jnp.minimum(gcp - gc_row, 0.0)))
        gram.append(_heads_mm(jnp.concatenate([k, q], axis=0), k, bd, transpose_rhs=True))
    lower = [dirm_ref[1 if p[5] else 0, 1] * p[3] * gram[x][:c] * decay[x] for x, p in enumerate(probs)]
    a_intra = [gram[x][c:] * decay[x] for x in range(n)]
    t = [eye - lower[x] * lvl_ref[1] for x in range(n)]
    for lev in range(2, 7):
        y = [_heads_mm(t[x], lower[x] * lvl_ref[lev], bd) for x in range(n)]
        z = [_heads_mm(y[x], t[x], bd) for x in range(n)]
        t = [t[x] - z[x] for x in range(n)]
    e_gc = [jnp.exp(gc[x]) for x in range(n)]
    u = [_heads_mm(t[x], p[2] * p[3], bd) for x, p in enumerate(probs)]
    w = [_heads_mm(t[x], p[1] * p[3] * e_gc[x], bd) for x, p in enumerate(probs)]
    wq = [jnp.concatenate([w[x], p[0] * e_gc[x]], axis=0).astype(BF16) for x, p in enumerate(probs)]
    k_dec = [(p[1] * jnp.exp(g_last[x] - gc[x])).astype(BF16) for x, p in enumerate(probs)]
    g_tot = [jnp.exp(g_last[x]) for x in range(n)]
    return u, wq, k_dec, a_intra, g_tot


def _gdn_state_steps(items, bd):
    c = GDN_CHUNK
    pairs = [slice(p * GDN_PAIR, (p + 1) * GDN_PAIR) for p in range(GDN_W // GDN_PAIR)]
    bdf = bd.astype(F32)
    ws_qs = [jnp.concatenate([jnp.dot(wq[:, ls], s_ref[p].astype(BF16), preferred_element_type=F32)
                              for p, ls in enumerate(pairs)], axis=1)
             for _, wq, _, _, _, s_ref in items]
    v_new = [it[0] - ws[:c] for it, ws in zip(items, ws_qs)]
    outs = [ws[c:] + _heads_mm(it[3], vn, bd) for it, ws, vn in zip(items, ws_qs, v_new)]
    for (_, _, k_dec, _, g_tot, s_ref), vn in zip(items, v_new):
        vb = vn.astype(BF16)
        for p, ls in enumerate(pairs):
            kv = lax.dot_general(k_dec[:, ls], vb[:, ls], (((0,), (0,)), ((), ())), preferred_element_type=F32)
            s_ref[p] = s_ref[p] * g_tot[:, ls] + kv * bdf
    return outs


def _gdn_scan_kernel(qf, kf, vf, bf, gf, qb, kb, vb, bb, gb, bd_ref, lvl_ref, dirm_ref, tri_ref,
                     of_ref, ob_ref, sf_ref, sb_ref):
    @pl.when(pl.program_id(1) == 0)
    def _():
        sf_ref[...] = jnp.zeros(sf_ref.shape, F32)
        sb_ref[...] = jnp.zeros(sb_ref.shape, F32)

    bd = bd_ref[...]
    n = SEQ_TILE // GDN_CHUNK
    probs, rows = [], []
    for c in range(n):
        rf = slice(c * GDN_CHUNK, (c + 1) * GDN_CHUNK)
        rb = slice((n - 1 - c) * GDN_CHUNK, (n - c) * GDN_CHUNK)
        probs.append(tuple(r[rf, :].astype(F32) for r in (qf, kf, vf, bf, gf)) + (False,))
        probs.append(tuple(r[rb, :].astype(F32) for r in (qb, kb, vb, bb, gb)) + (True,))
        rows += [rf, rb]
    u, wq, k_dec, a_intra, g_tot = _gdn_intra(probs, bd, lvl_ref, dirm_ref, tri_ref)
    for c in range(n):
        xs = (2 * c, 2 * c + 1)
        outs = _gdn_state_steps([(u[x], wq[x], k_dec[x], a_intra[x], g_tot[x], sb_ref if probs[x][5] else sf_ref)
                                 for x in xs], bd)
        for x, o in zip(xs, outs):
            (ob_ref if probs[x][5] else of_ref)[rows[x], :] = o


def _seq_tile_maps(n_lat_tiles, n_tiles):
    fwd = lambda s: lax.rem(s + n_lat_tiles, n_tiles)
    bwd = lambda s: n_tiles - 1 - s
    return fwd, bwd


def _gdn_scan(q, k, v, beta, g, n_lat):
    b, s, _ = q.shape
    n_tiles = s // SEQ_TILE
    fwd, bwd = _seq_tile_maps(n_lat // SEQ_TILE, n_tiles)
    spec = lambda tile_of, col: pl.BlockSpec((None, SEQ_TILE, GDN_W), lambda bi, t: (bi, tile_of(t), col))
    masks = _gdn_masks()
    in_specs = [spec(fwd, 0)] * 5 + [spec(bwd, 0)] * 3 + [spec(bwd, 1)] * 2 + [_full(m) for m in masks]
    return pl.pallas_call(
        _gdn_scan_kernel,
        grid=(b, n_tiles),
        in_specs=in_specs,
        out_specs=[spec(fwd, 0), spec(bwd, 0)],
        out_shape=[jax.ShapeDtypeStruct((b, s, GDN_W), F32)] * 2,
        scratch_shapes=[pltpu.VMEM((GDN_W // GDN_PAIR, GDN_PAIR, GDN_PAIR), F32)] * 2,
        compiler_params=_cparams(("parallel", "arbitrary")),
        name="gdn_scan",
    )(q, k, v, beta, g, q, k, v, beta, g, *masks)


def _lru_tile_prep(x_ref, xp_ref, xn_ref, cw_ref, cb_ref, wg_ref, bg_ref, nla_ref, a_ref, b_ref,
                   tile_idx, n_tiles, n_lat_tiles):
    xb = _tile_conv(x_ref[...], xp_ref[...], xn_ref[...], cw_ref, tile_idx, n_tiles, n_lat_tiles) + cb_ref[...]
    gates = _sigmoid(jnp.dot(xb.astype(BF16), wg_ref[...], preferred_element_type=F32) + bg_ref[...])
    log_a = nla_ref[...] * gates[:, :LRU_W]
    a_ref[...] = jnp.exp(log_a)
    th = jnp.tanh(log_a)
    b_ref[...] = jnp.sqrt(-2.0 * th / (1.0 - th)) * gates[:, LRU_W:] * xb


def _scan_group(a, b, h, row, reverse):
    for d in (1, 2, 4):
        if reverse:
            keep = row < 8 - d
            shift = 8 - d
        else:
            keep = row >= d
            shift = d
        a_s = jnp.where(keep, pltpu.roll(a, shift, 0), 1.0)
        b_s = jnp.where(keep, pltpu.roll(b, shift, 0), 0.0)
        b = a * b_s + b
        a = a * a_s
    return a * h + b


def _lru_scan_kernel(xf, xfp, xfn, xb, xbp, xbn, cw_ref, cb_ref, wgf, bgf, nlaf, wgb, bgb, nlab,
                     hf_ref, hb_ref, af_ref, bf_ref, ab_ref, bb_ref, cf_ref, cbk_ref, *, n_lat_tiles):
    s = pl.program_id(1)
    n_tiles = pl.num_programs(1)

    @pl.when(s == 0)
    def _():
        cf_ref[...] = jnp.zeros(cf_ref.shape, F32)
        cbk_ref[...] = jnp.zeros(cbk_ref.shape, F32)

    t_f, t_b = (m(s) for m in _seq_tile_maps(n_lat_tiles, n_tiles))
    _lru_tile_prep(xf, xfp, xfn, cw_ref, cb_ref, wgf, bgf, nlaf, af_ref, bf_ref, t_f, n_tiles, n_lat_tiles)
    _lru_tile_prep(xb, xbp, xbn, cw_ref, cb_ref, wgb, bgb, nlab, ab_ref, bb_ref, t_b, n_tiles, n_lat_tiles)

    n_groups = SEQ_TILE // 8
    row = lax.broadcasted_iota(jnp.int32, (8, LRU_W), 0)

    def body(gi, carry):
        h_f, h_b = carry
        rf = pl.ds(pl.multiple_of(gi * 8, 8), 8)
        rb = pl.ds(pl.multiple_of((n_groups - 1 - gi) * 8, 8), 8)
        out_f = _scan_group(af_ref[rf, :], bf_ref[rf, :], h_f, row, False)
        out_b = _scan_group(ab_ref[rb, :], bb_ref[rb, :], h_b, row, True)
        hf_ref[rf, :] = out_f
        hb_ref[rb, :] = out_b
        return (jnp.broadcast_to(out_f[7:8, :], (8, LRU_W)), jnp.broadcast_to(out_b[0:1, :], (8, LRU_W)))

    h_f, h_b = lax.fori_loop(0, n_groups, body, (cf_ref[...], cbk_ref[...]))
    cf_ref[...] = h_f
    cbk_ref[...] = h_b


def _lru_scan(p, conv_w, conv_b, w_r, b_r, w_i, b_i, lam, n_lat):
    b, s, _ = p.shape
    n_tiles = s // SEQ_TILE
    nlt = n_lat // SEQ_TILE
    fwd, bwd = _seq_tile_maps(nlt, n_tiles)
    col = P_LRU_X // LRU_W
    specs = []
    for tile_of in (fwd, bwd):
        main, prev, make_next = _halo_specs(LRU_W, col, tile_of)
        specs += [pl.BlockSpec((None, SEQ_TILE, LRU_W), main), pl.BlockSpec((None, HALO, LRU_W), prev),
                  pl.BlockSpec((None, HALO, LRU_W), make_next(n_tiles))]

    def blockdiag(w):
        return jax.scipy.linalg.block_diag(*[w[n] for n in range(LRU_BLOCKS)])

    dir_args = []
    for d in range(2):
        wg = jnp.concatenate([blockdiag(w_r[d]), blockdiag(w_i[d])], axis=1).astype(BF16)
        bg = jnp.concatenate([b_r[d], b_i[d]]).astype(F32).reshape(1, 2 * LRU_W)
        nla = (-LRU_C * jax.nn.softplus(-lam[d].astype(F32))).reshape(1, LRU_W)
        dir_args += [wg, bg, nla]
    cb2 = conv_b.reshape(1, LRU_W)
    out_spec = lambda tile_of: pl.BlockSpec((None, SEQ_TILE, LRU_W), lambda bi, t: (bi, tile_of(t), 0))
    return pl.pallas_call(
        functools.partial(_lru_scan_kernel, n_lat_tiles=nlt),
        grid=(b, n_tiles),
        in_specs=specs + [_full(conv_w), _full(cb2)] + [_full(a) for a in dir_args],
        out_specs=[out_spec(fwd), out_spec(bwd)],
        out_shape=[jax.ShapeDtypeStruct((b, s, LRU_W), F32)] * 2,
        scratch_shapes=[pltpu.VMEM((SEQ_TILE, LRU_W), F32)] * 4 + [pltpu.VMEM((8, LRU_W), F32)] * 2,
        compiler_params=_cparams(("parallel", "arbitrary")),
        name="lru_scan",
    )(p, p, p, p, p, p, conv_w, cb2, *dir_args)


def _post_kernel(of_ref, ob_ref, z_ref, gn_ref, ones_ref, hf_ref, hb_ref, y_ref, ua_ref, ub_ref):
    o = of_ref[...] + ob_ref[...]
    ms = jnp.dot(jnp.concatenate(_split3(o * o), axis=-1), ones_ref[...], preferred_element_type=F32) * (1.0 / GDN_DV)
    z = z_ref[...]
    ua_ref[...] = (o * lax.rsqrt(ms + RMS_EPS) * gn_ref[...] * (z * _sigmoid(z))).astype(ua_ref.dtype)
    y = y_ref[...]
    gelu = 0.5 * y * (1.0 + jnp.tanh(0.7978845608028654 * (y + 0.044715 * (y * y * y))))
    ub_ref[...] = ((hf_ref[...] + hb_ref[...]) * gelu).astype(ub_ref.dtype)


def _post(o_f, o_b, h_f, h_b, p, gdn_norm_g):
    b, s, _ = p.shape
    tm = _pick_tile(s, 544)
    ones3 = jnp.concatenate([_head_block_ones(GDN_HEADS, GDN_DV)] * 3, axis=0)
    gn = jnp.tile(gdn_norm_g.astype(F32), GDN_HEADS).reshape(1, GDN_W)
    tok = pl.BlockSpec((None, tm, GDN_W), lambda bi, i: (bi, i, 0))
    pcol = lambda c: pl.BlockSpec((None, tm, GDN_W), lambda bi, i: (bi, i, c))
    return pl.pallas_call(
        _post_kernel,
        grid=(b, s // tm),
        in_specs=[tok, tok, pcol(P_GDN_Z // GDN_W), _full(gn), _full(ones3), tok, tok, pcol(P_LRU_Y // LRU_W)],
        out_specs=[tok, tok],
        out_shape=[jax.ShapeDtypeStruct((b, s, GDN_W), BF16)] * 2,
        compiler_params=_cparams(("parallel", "parallel")),
        name="gdn_lru_post",
    )(o_f, o_b, p, gn, ones3, h_f, h_b, p)


def _merge_kernel(x_ref, mc_ref, ml_ref, u0_ref, u1_ref, u2_ref, u3_ref, t0_ref, t1_ref, t2_ref, t3_ref,
                  bg_ref, wb_ref, wo_ref, o_ref, *, n_lat):
    merged = None
    for n, (u_ref, t_ref) in enumerate(((u0_ref, t0_ref), (u1_ref, t1_ref), (u2_ref, t2_ref), (u3_ref, t3_ref))):
        gate = _sigmoid(t_ref[...] + bg_ref[n:n + 1, :])
        term = gate * jnp.dot(u_ref[...], wb_ref[n], preferred_element_type=F32)
        merged = term if merged is None else merged + term
    out = jnp.dot(merged.astype(BF16), wo_ref[...], preferred_element_type=F32)
    g1 = _row_select(x_ref.shape[0], pl.program_id(1), n_lat, mc_ref[2:3, :], ml_ref[2:3, :])
    o_ref[...] = x_ref[...] + g1 * out


def _merge(xs, mod_c, mod_l, branches, p, b_gate, w_branch, w_out, n_lat, with_ctx):
    b, s, d = xs.shape
    rows = s if with_ctx else n_lat
    tm = _pick_tile(rows, 544) if with_ctx else SEQ_TILE
    gate_blk = P_GATES // d
    tok = lambda w: pl.BlockSpec((None, tm, w), lambda bi, i: (bi, i, 0))
    in_specs = [tok(d), pl.BlockSpec((8, d), lambda bi, i: (0, 0)), pl.BlockSpec((None, 8, d), lambda bi, i: (bi, 0, 0))]
    in_specs += [tok(BRANCH_W)] * N_BRANCH
    in_specs += [pl.BlockSpec((None, tm, d), functools.partial(lambda bi, i, n: (bi, i, gate_blk + n), n=n))
                 for n in range(N_BRANCH)]
    in_specs += [_full(b_gate), _full(w_branch), _full(w_out)]
    return pl.pallas_call(
        functools.partial(_merge_kernel, n_lat=n_lat),
        grid=(b, rows // tm),
        in_specs=in_specs,
        out_specs=tok(d),
        out_shape=jax.ShapeDtypeStruct((b, rows, d), F32),
        input_output_aliases={0: 0} if with_ctx else {},
        compiler_params=_cparams(("parallel", "parallel")),
        name="merge",
    )(xs, mod_c, mod_l, *branches, p, p, p, p, b_gate, w_branch, w_out)


def _mlp_kernel(x_ref, mc_ref, ml_ref, gn_ref, gf_ref, w1_ref, w2_ref, o_ref, h_ref, acc_ref, *, n_lat, final_norm):
    f = pl.program_id(2)
    tm = x_ref.shape[0]
    i = pl.program_id(1)

    @pl.when(f == 0)
    def _():
        x = x_ref[...]
        y = x * lax.rsqrt(jnp.mean(x * x, axis=-1, keepdims=True) + RMS_EPS)
        y = y * gn_ref[...]
        shift = _row_select(tm, i, n_lat, mc_ref[3:4, :], ml_ref[3:4, :])
        scale = _row_select(tm, i, n_lat, mc_ref[4:5, :], ml_ref[4:5, :])
        h_ref[...] = (y * (1.0 + scale) + shift).astype(BF16)

    a = jnp.maximum(jnp.dot(h_ref[...], w1_ref[...], preferred_element_type=F32), 0.0)
    part = jnp.dot((a * a).astype(BF16), w2_ref[...], preferred_element_type=F32)

    @pl.when(f == 0)
    def _():
        acc_ref[...] = part

    @pl.when(f > 0)
    def _():
        acc_ref[...] += part

    @pl.when(f == pl.num_programs(2) - 1)
    def _():
        g2 = _row_select(tm, i, n_lat, mc_ref[5:6, :], ml_ref[5:6, :])
        y = x_ref[...] + g2 * acc_ref[...]
        if final_norm:
            y = y * lax.rsqrt(jnp.mean(y * y, axis=-1, keepdims=True) + RMS_EPS) * gf_ref[...]
        o_ref[...] = y


def _mlp(xs, mod_c, mod_l, gain, w1, w2, final_gain, n_lat, final_norm):
    b, rows, d = xs.shape
    dff = w1.shape[1]
    tm = _pick_tile(rows, 1088)
    tf = 1024
    row = pl.BlockSpec((1, d), lambda bi, i, f: (0, 0))
    return pl.pallas_call(
        functools.partial(_mlp_kernel, n_lat=n_lat, final_norm=final_norm),
        grid=(b, rows // tm, dff // tf),
        in_specs=[pl.BlockSpec((None, tm, d), lambda bi, i, f: (bi, i, 0)),
                  pl.BlockSpec((8, d), lambda bi, i, f: (0, 0)),
                  pl.BlockSpec((None, 8, d), lambda bi, i, f: (bi, 0, 0)),
                  row, row,
                  pl.BlockSpec((d, tf), lambda bi, i, f: (0, f)),
                  pl.BlockSpec((tf, d), lambda bi, i, f: (f, 0))],
        out_specs=pl.BlockSpec((None, tm, d), lambda bi, i, f: (bi, i, 0)),
        out_shape=jax.ShapeDtypeStruct((b, rows, d), F32),
        scratch_shapes=[pltpu.VMEM((tm, d), BF16), pltpu.VMEM((tm, d), F32)],
        compiler_params=_cparams(("parallel", "parallel", "arbitrary")),
        name="mlp",
    )(xs, mod_c, mod_l, gain, final_gain, w1, w2)


def kernel(x, c, ctx, c_ctx, mod_w, mod_b, norm1_g, norm2_g, w_in, b_gate, gdn_conv_w, gdn_a_log, gdn_dt_bias,
           gdn_norm_g, lru_conv_w, lru_conv_b, lru_w_r, lru_b_r, lru_w_i, lru_b_i, lru_lambda, mla_q_norm_g,
           mla_w_uq, mla_kv_norm_g, mla_w_ukv, na_rpb, w_branch, w_out, mlp_w1, mlp_w2, final_norm_g):
    bsz, n_tok, d = x.shape
    n_ctx = ctx.shape[1]
    depth = w_in.shape[0]
    assert n_ctx % SEQ_TILE == 0 and n_tok % SEQ_TILE == 0 and n_tok % GRID_W == 0
    na_meta, onehot_r, onehot_c = _na_geometry(n_tok // GRID_W)
    na_meta = jnp.asarray(na_meta)
    cos, sin = _rope_tables(n_tok, n_ctx)

    n_rows = -(-(bsz + 1) // 8) * 8
    cc = jnp.zeros((n_rows, d), F32).at[:bsz].set(c).at[bsz].set(c_ctx)
    final_gain = final_norm_g.reshape(1, d)

    xs = jnp.concatenate([x, ctx], axis=1)
    for l in range(depth):
        need_ctx = l < depth - 1
        mod = _modulation(cc, mod_w[l], mod_b[l]).reshape(n_rows, N_MOD, d)
        pad = jnp.zeros((8 - N_MOD, d), F32)
        mod_c = jnp.concatenate([mod[bsz], pad], axis=0)
        mod_l = jnp.concatenate([mod[:bsz], jnp.broadcast_to(pad, (bsz, 8 - N_MOD, d))], axis=1)

        w_in_l = _arrange_w_in(w_in[l])
        wq, wk, wv, place = _arrange_mla(mla_w_uq[l], mla_w_ukv[l])
        gq = mla_q_norm_g[l].reshape(1, -1)
        gkv = mla_kv_norm_g[l].reshape(1, -1)
        g1n = norm1_g[l].reshape(1, d)
        g2n = norm2_g[l].reshape(1, d)
        wb = w_branch[l].astype(BF16)
        wo = w_out[l].astype(BF16)
        w1 = mlp_w1[l].astype(BF16)
        w2 = mlp_w2[l].astype(BF16)

        p32, p16 = _inproj(xs, mod_c, mod_l, g1n, w_in_l, n_tok)

        gq_, gk_, gv_, gbeta, gg = _gdn_prep(p32, gdn_conv_w[l], gdn_a_log[l], gdn_dt_bias[l], n_tok)
        o_f, o_b = _gdn_scan(gq_, gk_, gv_, gbeta, gg, n_tok)
        h_f, h_b = _lru_scan(p32, lru_conv_w[l], lru_conv_b[l], lru_w_r[l], lru_b_r[l], lru_w_i[l], lru_b_i[l],
                             lru_lambda[l], n_tok)
        ua, ub = _post(o_f, o_b, h_f, h_b, p32, gdn_norm_g[l])

        mq, mk, mv = _mla_prep(p32, p16, cos, sin, gq, gkv, wq, wk, wv, place)
        uc = _mla_flash(mq, mk, mv, n_tok)
        if need_ctx:
            uc = _mla_flash(mq, mk, mv, n_tok, prev=uc)
        ud = _na_attention(p16, na_meta, _na_bias_table(na_rpb[l], onehot_r, onehot_c), n_tok, need_ctx)

        xs = _merge(xs, mod_c, mod_l, (ua, ub, uc, ud), p16, b_gate[l], wb, wo, n_tok, need_ctx)
        xs = _mlp(xs, mod_c, mod_l, g2n, w1, w2, final_gain, n_tok, l == depth - 1)
    return xs
```

```python
import functools

import jax
import jax.numpy as jnp
import numpy as np
from jax import lax
from jax.experimental import pallas as pl
from jax.experimental.pallas import tpu as pltpu

F32 = jnp.float32
BF16 = jnp.bfloat16

GRID_W = 64
N_MOD = 6
RMS_EPS = 1e-6
GDN_HEADS = 4
GDN_DK = 64
GDN_DV = 64
GDN_CHUNK = 64
GDN_W = GDN_HEADS * GDN_DV
GDN_CONV = 4
LRU_W = 256
LRU_BLOCKS = 4
LRU_BLOCK_W = LRU_W // LRU_BLOCKS
LRU_CONV = 4
LRU_C = 8.0
MLA_HEADS = 4
MLA_Q_RANK = 256
MLA_KV_RANK = 128
MLA_NOPE = 64
MLA_ROPE = 32
MLA_V = 64
MLA_SLOT = 128
ROPE_BASE = 10000.0
LOG2_E = 1.4426950408889634
NA_HEADS = 4
NA_DH = 64
NA_W = NA_HEADS * NA_DH
NA_WIN_ROWS = 8
NA_WIN_COLS = 16
N_BRANCH = 4
BRANCH_W = 256

SEQ_TILE = 256
HALO = 8

_REF_COLS = {}
_off = 0
for _name, _w in (('gdn_qkv', 3 * GDN_W), ('gdn_z', GDN_W), ('gdn_beta', 2 * GDN_HEADS), ('gdn_a', 2 * GDN_HEADS),
                  ('lru_x', LRU_W), ('lru_y', LRU_W), ('mla_q', MLA_Q_RANK), ('mla_kv', MLA_KV_RANK),
                  ('mla_kr', MLA_ROPE), ('na_qkv', 3 * NA_W)):
    _REF_COLS[_name] = (_off, _w)
    _off += _w
N_MIX_COLS = _off

P_GDN_QKV = 0
P_GDN_Z = 768
P_LRU_X = 1024
P_LRU_Y = 1280
P_MLA_KV = 1536
P_MLA_KR = 1664
P_GDN_BA = 1792
P32_COLS = 2048
P_MLA_Q = 0
P_NA_QKV = 256
P_GATES = 1024
P16_COLS = P_GATES + N_BRANCH * 1024
PROJ_TILE = 1024

VMEM_LIMIT = 52 * 1024 * 1024


def _cparams(sem):
    return pltpu.CompilerParams(dimension_semantics=sem, vmem_limit_bytes=VMEM_LIMIT)


def _pick_tile(n, cap):
    best = 8
    for t in range(8, min(n, cap) + 1, 8):
        if n % t == 0:
            best = t
    return best


def _full(a):
    return pl.BlockSpec(a.shape, lambda *_: (0,) * a.ndim)


def _split3(x):
    hi = x.astype(BF16)
    r = x - hi.astype(F32)
    mid = r.astype(BF16)
    lo = (r - mid.astype(F32)).astype(BF16)
    return hi, mid, lo


def _sigmoid(x):
    return 0.5 * jnp.tanh(0.5 * x) + 0.5


def _arrange_w_in(w_in):
    pieces, pos = [], 0

    def put(dst, block):
        nonlocal pos
        if dst > pos:
            pieces.append(jnp.zeros(w_in.shape[:-1] + (dst - pos,), w_in.dtype))
        pieces.append(block)
        pos = dst + block.shape[-1]

    ref = lambda name: w_in[..., _REF_COLS[name][0]:_REF_COLS[name][0] + _REF_COLS[name][1]]
    for name, dst in (('gdn_qkv', P_GDN_QKV), ('gdn_z', P_GDN_Z), ('lru_x', P_LRU_X), ('lru_y', P_LRU_Y),
                      ('mla_kv', P_MLA_KV), ('mla_kr', P_MLA_KR), ('gdn_beta', P_GDN_BA),
                      ('gdn_a', P_GDN_BA + 2 * GDN_HEADS)):
        put(dst, ref(name))
    put(P32_COLS + P_MLA_Q, ref('mla_q'))
    put(P32_COLS + P_NA_QKV, ref('na_qkv'))
    put(P32_COLS + P_GATES, w_in[..., N_MIX_COLS:])
    assert pos == P32_COLS + P16_COLS
    return jnp.concatenate(pieces, axis=-1).astype(BF16)


def _rope_perm():
    q = MLA_ROPE // 4
    src = np.zeros(MLA_ROPE, np.int32)
    sign = np.zeros(MLA_ROPE, np.float32)
    for base in (0, 2 * q):
        for d in range(q):
            src[base + d] = base + d + q
            sign[base + d] = -1.0
            src[base + q + d] = base + d
            sign[base + q + d] = 1.0
    return src, sign


def _arrange_mla(w_uq, w_ukv):
    src, sign = _rope_perm()
    hq = MLA_NOPE + MLA_ROPE
    wq = jnp.zeros((MLA_Q_RANK, 2 * MLA_HEADS * MLA_SLOT), F32)
    wk = jnp.zeros((MLA_KV_RANK, MLA_HEADS * MLA_SLOT), F32)
    wv = jnp.zeros((MLA_KV_RANK, MLA_HEADS * MLA_V), F32)
    place = np.zeros((2 * MLA_SLOT, 2 * MLA_HEADS * MLA_SLOT), np.float32)
    rot_off = MLA_HEADS * MLA_SLOT
    for h in range(MLA_HEADS):
        nope = w_uq[:, h * hq:h * hq + MLA_NOPE]
        pe = w_uq[:, h * hq + MLA_NOPE:(h + 1) * hq]
        s = h * MLA_SLOT
        wq = wq.at[:, s:s + MLA_NOPE].set(nope)
        wq = wq.at[:, s + MLA_NOPE:s + MLA_NOPE + MLA_ROPE].set(pe)
        wq = wq.at[:, rot_off + s + MLA_NOPE:rot_off + s + MLA_NOPE + MLA_ROPE].set(pe[:, src] * sign)
        wk = wk.at[:, s:s + MLA_NOPE].set(w_ukv[:, h * 128:h * 128 + MLA_NOPE])
        wv = wv.at[:, h * MLA_V:(h + 1) * MLA_V].set(w_ukv[:, h * 128 + MLA_NOPE:(h + 1) * 128])
        for d in range(MLA_ROPE):
            for half in (0, MLA_SLOT):
                place[half + d, s + MLA_NOPE + d] = 1.0
                place[half + src[d], rot_off + s + MLA_NOPE + d] = sign[d]
    return wq.astype(BF16), wk.astype(BF16), wv.astype(BF16), jnp.asarray(place, BF16)


def _rope_tables(n_tok, n_ctx):
    cos = np.ones((n_tok + n_ctx, MLA_SLOT), np.float32)
    sin = np.zeros((n_tok + n_ctx, MLA_SLOT), np.float32)
    t = np.arange(n_tok)
    row = (t // GRID_W).astype(np.float32)
    col = (t % GRID_W).astype(np.float32)
    n_freq = MLA_ROPE // 4
    inv = (ROPE_BASE ** (-np.arange(n_freq, dtype=np.float32) / n_freq)).astype(np.float32)
    ar = row[:, None] * inv
    ac = col[:, None] * inv
    ang = np.concatenate([ar, ar, ac, ac], axis=-1).astype(np.float32)
    cos[:n_tok, MLA_NOPE:MLA_NOPE + MLA_ROPE] = np.cos(ang)
    sin[:n_tok, MLA_NOPE:MLA_NOPE + MLA_ROPE] = np.sin(ang)
    return jnp.asarray(cos), jnp.asarray(sin)


def _head_block_ones(n_heads, width):
    m = np.kron(np.eye(n_heads, dtype=np.float32), np.ones((width, width), np.float32))
    return jnp.asarray(m, BF16)


def _mod_kernel(c_ref, w_ref, b_ref, o_ref):
    c = c_ref[...]
    s = c * _sigmoid(c)
    o_ref[...] = jnp.dot(s, w_ref[...], preferred_element_type=F32) + b_ref[...]


def _modulation(cc, mod_w, mod_b):
    r, d = cc.shape
    n = mod_w.shape[1]
    tn = 1024
    return pl.pallas_call(
        _mod_kernel,
        grid=(n // tn,),
        in_specs=[pl.BlockSpec((r, d), lambda j: (0, 0)),
                  pl.BlockSpec((d, tn), lambda j: (0, j)),
                  pl.BlockSpec((1, tn), lambda j: (0, j))],
        out_specs=pl.BlockSpec((r, tn), lambda j: (0, j)),
        out_shape=jax.ShapeDtypeStruct((r, n), F32),
        compiler_params=_cparams(("arbitrary",)),
        name="modulation",
    )(cc, mod_w, mod_b.reshape(1, n))


def _row_select(tile_rows, tile_idx, n_lat, ctx_vec, lat_vec):
    row = tile_idx * tile_rows + lax.broadcasted_iota(jnp.int32, (tile_rows, 1), 0)
    return jnp.where(row < n_lat, lat_vec, ctx_vec)


def _inproj_kernel(x_ref, mc_ref, ml_ref, g_ref, w_ref, o32_ref, o16_ref, h_ref, *, n_lat):
    j = pl.program_id(2)
    n32 = P32_COLS // PROJ_TILE

    @pl.when(j == 0)
    def _():
        tm = x_ref.shape[0]
        i = pl.program_id(1)
        x = x_ref[...]
        y = x * lax.rsqrt(jnp.mean(x * x, axis=-1, keepdims=True) + RMS_EPS)
        y = y * g_ref[...]
        shift = _row_select(tm, i, n_lat, mc_ref[0:1, :], ml_ref[0:1, :])
        scale = _row_select(tm, i, n_lat, mc_ref[1:2, :], ml_ref[1:2, :])
        h_ref[...] = (y * (1.0 + scale) + shift).astype(BF16)

    acc = jnp.dot(h_ref[...], w_ref[...], preferred_element_type=F32)

    @pl.when(j < n32)
    def _():
        o32_ref[...] = acc

    @pl.when(j >= n32)
    def _():
        o16_ref[...] = acc.astype(BF16)


def _inproj(xs, mod_c, mod_l, gain, w, layer, n_lat):
    b, s, d = xs.shape
    tm = _pick_tile(s, 1088)
    tn = PROJ_TILE
    n32 = P32_COLS // tn
    return pl.pallas_call(
        functools.partial(_inproj_kernel, n_lat=n_lat),
        grid=(b, s // tm, (P32_COLS + P16_COLS) // tn),
        in_specs=[pl.BlockSpec((None, tm, d), lambda bi, i, j: (bi, i, 0)),
                  pl.BlockSpec((8, d), lambda bi, i, j: (0, 0)),
                  pl.BlockSpec((None, 8, d), lambda bi, i, j: (bi, 0, 0)),
                  pl.BlockSpec((1, d), lambda bi, i, j: (0, 0)),
                  pl.BlockSpec((None, d, tn), lambda bi, i, j: (layer, 0, j))],
        out_specs=[pl.BlockSpec((None, tm, tn), lambda bi, i, j: (bi, i, jnp.minimum(j, n32 - 1))),
                   pl.BlockSpec((None, tm, tn), lambda bi, i, j: (bi, i, jnp.maximum(j - n32, 0)))],
        out_shape=[jax.ShapeDtypeStruct((b, s, P32_COLS), F32), jax.ShapeDtypeStruct((b, s, P16_COLS), BF16)],
        scratch_shapes=[pltpu.VMEM((tm, d), BF16)],
        compiler_params=_cparams(("parallel", "parallel", "arbitrary")),
        name="inproj",
    )(xs, mod_c, mod_l, gain, w)


def _mla_prep_kernel(ql_ref, kv_ref, cos_ref, sin_ref, gq_ref, gkv_ref, wq_ref, wk_ref, wv_ref, pl_ref,
                     q_ref, k_ref, v_ref, *, scale):
    nslot = MLA_HEADS * MLA_SLOT
    cos = jnp.concatenate([cos_ref[...]] * MLA_HEADS, axis=-1)
    sin = jnp.concatenate([sin_ref[...]] * MLA_HEADS, axis=-1)

    ql = ql_ref[...].astype(F32)
    qn = ql * lax.rsqrt(jnp.mean(ql * ql, axis=-1, keepdims=True) + RMS_EPS) * gq_ref[...]
    q2 = jnp.dot(qn.astype(BF16), wq_ref[...], preferred_element_type=F32)
    q = (q2[:, :nslot] * cos + q2[:, nslot:] * sin) * scale
    q_ref[...] = q.astype(BF16)

    kvkr = kv_ref[...]
    kvl = kvkr[:, :MLA_KV_RANK]
    kr = kvkr[:, MLA_KV_RANK:]
    kvn = (kvl * lax.rsqrt(jnp.mean(kvl * kvl, axis=-1, keepdims=True) + RMS_EPS) * gkv_ref[...]).astype(BF16)
    kn = jnp.dot(kvn, wk_ref[...], preferred_element_type=F32)
    v_ref[...] = jnp.dot(kvn, wv_ref[...], preferred_element_type=F32).astype(BF16)
    kr_hi = kr.astype(BF16)
    kr_lo = (kr - kr_hi.astype(F32)).astype(BF16)
    kr2 = jnp.dot(jnp.concatenate([kr_hi, kr_lo], axis=-1), pl_ref[...], preferred_element_type=F32)
    k_ref[...] = (kn + kr2[:, :nslot] * cos + kr2[:, nslot:] * sin).astype(BF16)


def _mla_prep(p32, p16, cos, sin, gq, gkv, wq, wk, wv, place):
    b, s, _ = p32.shape
    tm = _pick_tile(s, 544)
    nslot = MLA_HEADS * MLA_SLOT
    scale = (MLA_NOPE + MLA_ROPE) ** -0.5 * LOG2_E
    return pl.pallas_call(
        functools.partial(_mla_prep_kernel, scale=scale),
        grid=(b, s // tm),
        in_specs=[pl.BlockSpec((None, tm, MLA_Q_RANK), lambda bi, i: (bi, i, P_MLA_Q // MLA_Q_RANK)),
                  pl.BlockSpec((None, tm, 2 * MLA_SLOT), lambda bi, i: (bi, i, P_MLA_KV // (2 * MLA_SLOT))),
                  pl.BlockSpec((tm, MLA_SLOT), lambda bi, i: (i, 0)),
                  pl.BlockSpec((tm, MLA_SLOT), lambda bi, i: (i, 0)),
                  _full(gq), _full(gkv), _full(wq), _full(wk), _full(wv), _full(place)],
        out_specs=[pl.BlockSpec((None, tm, nslot), lambda bi, i: (bi, i, 0)),
                   pl.BlockSpec((None, tm, nslot), lambda bi, i: (bi, i, 0)),
                   pl.BlockSpec((None, tm, MLA_HEADS * MLA_V), lambda bi, i: (bi, i, 0))],
        out_shape=[jax.ShapeDtypeStruct((b, s, nslot), BF16),
                   jax.ShapeDtypeStruct((b, s, nslot), BF16),
                   jax.ShapeDtypeStruct((b, s, MLA_HEADS * MLA_V), BF16)],
        compiler_params=_cparams(("parallel", "parallel")),
        name="mla_prep",
    )(p16, p32, cos, sin, gq, gkv, wq, wk, wv, place)


def _flash_softmax(h, s, m_ref, l_ref):
    m_prev = m_ref[h]
    m_new = jnp.maximum(m_prev, jnp.max(s, axis=-1, keepdims=True))
    alpha = jnp.exp2(m_prev - m_new)
    p = jnp.exp2(s - jnp.concatenate([m_new] * (s.shape[1] // 128), axis=-1))
    l_ref[h] = alpha * l_ref[h] + jnp.sum(p, axis=-1, keepdims=True)
    m_ref[h] = m_new
    return alpha, p.astype(BF16)


def _mla_flash_kernel(*refs, tk, aliased):
    if aliased:
        q_ref, k_ref, v_ref, _, o_ref, m_ref, l_ref, acc_ref = refs
    else:
        q_ref, k_ref, v_ref, o_ref, m_ref, l_ref, acc_ref = refs
    tq = q_ref.shape[0]
    nk = k_ref.shape[0]
    n_loop = nk // tk
    m_ref[...] = jnp.full(m_ref.shape, -jnp.inf, F32)
    l_ref[...] = jnp.zeros(l_ref.shape, F32)
    acc_ref[...] = jnp.zeros(acc_ref.shape, F32)

    def chunk(rows):
        heads = range(MLA_HEADS)
        hs = [slice(h * MLA_SLOT, (h + 1) * MLA_SLOT) for h in heads]
        vs = [slice((h // 2) * 128, (h // 2) * 128 + 128) for h in heads]
        s = [lax.dot_general(q_ref[:, hs[h]], k_ref[rows, hs[h]], (((1,), (1,)), ((), ())),
                             preferred_element_type=F32) for h in heads]
        ap = [_flash_softmax(h, s[h], m_ref, l_ref) for h in heads]
        for h in heads:
            alpha, p = ap[h]
            acc_ref[h] = alpha * acc_ref[h] + jnp.dot(p, v_ref[rows, vs[h]], preferred_element_type=F32)

    if n_loop:
        def body(j, carry):
            chunk(pl.ds(pl.multiple_of(j * tk, tk), tk))
            return carry
        lax.fori_loop(0, n_loop, body, 0)
    if nk > n_loop * tk:
        chunk(slice(n_loop * tk, nk))

    lane = lax.broadcasted_iota(jnp.int32, (tq, 128), 1)
    outs = []
    for pair in range(MLA_HEADS // 2):
        o0 = acc_ref[2 * pair] / l_ref[2 * pair]
        o1 = acc_ref[2 * pair + 1] / l_ref[2 * pair + 1]
        outs.append(jnp.where(lane < MLA_V, o0, o1))
    o_ref[...] = jnp.concatenate(outs, axis=-1).astype(o_ref.dtype)


def _mla_flash(q, k, v, n_lat, prev=None):
    b, s, nslot = q.shape
    n_ctx = s - n_lat
    nv = MLA_HEADS * MLA_V
    tk = 512
    if prev is None:
        tq = _pick_tile(n_lat, 512)
        q_off, n_q, kv_rows, kv_blk = 0, n_lat // tq, s, 0
    else:
        assert n_lat % n_ctx == 0
        tq = _pick_tile(n_ctx, 256)
        q_off, n_q, kv_rows, kv_blk = n_lat // tq, n_ctx // tq, n_ctx, n_lat // n_ctx
    in_specs = [pl.BlockSpec((None, tq, nslot), lambda bi, i: (bi, i + q_off, 0)),
                pl.BlockSpec((None, kv_rows, nslot), lambda bi, i: (bi, kv_blk, 0)),
                pl.BlockSpec((None, kv_rows, nv), lambda bi, i: (bi, kv_blk, 0))]
    args = [q, k, v]
    aliases = {}
    if prev is not None:
        in_specs.append(pl.BlockSpec(memory_space=pl.ANY))
        args.append(prev)
        aliases = {3: 0}
    return pl.pallas_call(
        functools.partial(_mla_flash_kernel, tk=tk, aliased=prev is not None),
        grid=(b, n_q),
        in_specs=in_specs,
        out_specs=pl.BlockSpec((None, tq, nv), lambda bi, i: (bi, i + q_off, 0)),
        out_shape=jax.ShapeDtypeStruct((b, s, nv), BF16),
        scratch_shapes=[pltpu.VMEM((MLA_HEADS, tq, 128), F32),
                        pltpu.VMEM((MLA_HEADS, tq, 128), F32),
                        pltpu.VMEM((MLA_HEADS, tq, 128), F32)],
        input_output_aliases=aliases,
        compiler_params=_cparams(("parallel", "arbitrary")),
        name="mla_flash",
    )(*args)


NA_QROWS = 4
NA_SLAB = NA_QROWS + NA_WIN_ROWS


def _na_geometry(rows):
    assert rows % NA_QROWS == 0 and rows >= NA_SLAB
    nblk = rows // NA_QROWS
    qc = np.arange(GRID_W)
    cs = np.clip(qc - NA_WIN_COLS // 2, 0, GRID_W - NA_WIN_COLS)
    col_valid = (qc[None, :] >= cs[:, None]) & (qc[None, :] < cs[:, None] + NA_WIN_COLS)
    rel_c = np.clip(qc[None, :] - qc[:, None] + NA_WIN_COLS - 1, 0, 2 * NA_WIN_COLS - 2)
    onehot_c = (rel_c[None] == np.arange(2 * NA_WIN_COLS - 1)[:, None, None]) & col_valid[None]
    patterns, cls, starts = {}, [], []
    for i in range(nblk):
        r0 = i * NA_QROWS
        start = int(np.clip(r0 - NA_WIN_ROWS // 2, 0, rows - NA_SLAB))
        qr = r0 + np.arange(NA_QROWS)
        rs = np.clip(qr - NA_WIN_ROWS // 2, 0, rows - NA_WIN_ROWS)
        key = (start - r0,) + tuple((rs - r0).tolist())
        if key not in patterns:
            kr = start + np.arange(NA_SLAB)
            row_valid = (kr[None, :] >= rs[:, None]) & (kr[None, :] < rs[:, None] + NA_WIN_ROWS)
            rel_r = np.clip(kr[None, :] - qr[:, None] + NA_WIN_ROWS - 1, 0, 2 * NA_WIN_ROWS - 2)
            patterns[key] = (len(patterns), np.where(row_valid, rel_r, -1))
        cls.append(patterns[key][0])
        starts.append(start)
    ordered = sorted(patterns.values(), key=lambda z: z[0])
    rel_rows = np.stack([z[1] for z in ordered])
    meta = np.stack([np.asarray(cls, np.int32), np.asarray(starts, np.int32)])
    return meta, rel_rows, (onehot_c.astype(np.float32), col_valid)


def _na_bias_table(rpb, rel_rows, col_tables):
    onehot_c, col_valid = col_tables
    bc = jnp.einsum('hrc,cqk->hrqk', rpb.astype(F32), onehot_c, precision=lax.Precision.HIGHEST)
    bc = jnp.where(col_valid, bc, -jnp.inf)
    outside = jnp.full(bc.shape[:1] + bc.shape[2:], -jnp.inf, F32)
    pats = []
    for pat in rel_rows:
        qrows = [jnp.concatenate([bc[:, r] if r >= 0 else outside for r in row], axis=-1) for row in pat]
        pats.append(jnp.concatenate(qrows, axis=1))
    return jnp.stack(pats)


def _na_scores(q_pair, lane, h, k_parts):
    in_head = (lane < NA_DH) if h % 2 == 0 else (lane >= NA_DH)
    qm = jnp.where(in_head, q_pair * (NA_DH ** -0.5), 0.0).astype(BF16)
    return [lax.dot_general(qm, k, (((1,), (1,)), ((), ())), preferred_element_type=F32) for k in k_parts]


def _na_softmax(scores, bias):
    if bias is not None:
        scores = [scores[0] + bias] + scores[1:]
    m = scores[0].max(axis=-1, keepdims=True)
    for s in scores[1:]:
        m = jnp.maximum(m, s.max(axis=-1, keepdims=True))
    probs = [jnp.exp(s - m) for s in scores]
    den = probs[0].sum(axis=-1, keepdims=True)
    for p in probs[1:]:
        den = den + p.sum(axis=-1, keepdims=True)
    return [p.astype(BF16) for p in probs], den


def _na_kernel(meta_ref, q_ref, k_ref, v_ref, bias_ref, o_ref, *, n_lat, with_ctx):
    i = pl.program_id(1)
    nq = q_ref.shape[0]
    n_lat_tiles = n_lat // nq
    n_all = k_ref.shape[0]
    lane = lax.broadcasted_iota(jnp.int32, (nq, 128), 1)

    def run(windowed):
        key_rows = [pl.ds(n_lat, n_all - n_lat)]
        if windowed:
            start = pl.multiple_of(meta_ref[1, jnp.minimum(i, n_lat_tiles - 1)] * GRID_W, GRID_W)
            key_rows = [pl.ds(start, NA_SLAB * GRID_W)] + key_rows
        pair_lanes = [slice((h // 2) * 128, (h // 2 + 1) * 128) for h in range(NA_HEADS)]
        scores = [_na_scores(q_ref[:, pair_lanes[h]], lane, h, [k_ref[r, pair_lanes[h]] for r in key_rows])
                  for h in range(NA_HEADS)]
        soft = [_na_softmax(scores[h], bias_ref[h] if windowed else None) for h in range(NA_HEADS)]
        heads = []
        for h in range(NA_HEADS):
            probs, den = soft[h]
            out = None
            for p, r in zip(probs, key_rows):
                po = jnp.dot(p, v_ref[r, pair_lanes[h]], preferred_element_type=F32)
                out = po if out is None else out + po
            heads.append(out / den)
        outs = [jnp.where(lane < NA_DH, heads[2 * pair], heads[2 * pair + 1]) for pair in range(NA_HEADS // 2)]
        o_ref[...] = jnp.concatenate(outs, axis=-1).astype(o_ref.dtype)

    if with_ctx:
        pl.when(i < n_lat_tiles)(lambda: run(True))
        pl.when(i >= n_lat_tiles)(lambda: run(False))
    else:
        run(True)


def _na_attention(p, meta, table, n_lat, with_ctx_queries):
    b, s, _ = p.shape
    nq = NA_QROWS * GRID_W
    nk = NA_SLAB * GRID_W
    assert (s - n_lat) % nq == 0
    nlt = n_lat // nq
    qb = P_NA_QKV // NA_W
    grid_spec = pltpu.PrefetchScalarGridSpec(
        num_scalar_prefetch=1,
        grid=(b, s // nq if with_ctx_queries else nlt),
        in_specs=[pl.BlockSpec((None, nq, NA_W), lambda bi, i, m: (bi, i, qb)),
                  pl.BlockSpec((None, s, NA_W), lambda bi, i, m: (bi, 0, qb + 1)),
                  pl.BlockSpec((None, s, NA_W), lambda bi, i, m: (bi, 0, qb + 2)),
                  pl.BlockSpec((None, NA_HEADS, nq, nk),
                               lambda bi, i, m: (m[0, jnp.minimum(i, nlt - 1)], 0, 0, 0))],
        out_specs=pl.BlockSpec((None, nq, NA_W), lambda bi, i, m: (bi, i, 0)))
    return pl.pallas_call(
        functools.partial(_na_kernel, n_lat=n_lat, with_ctx=with_ctx_queries),
        grid_spec=grid_spec,
        out_shape=jax.ShapeDtypeStruct((b, s, NA_W), BF16),
        compiler_params=_cparams(("parallel", "arbitrary")),
        name="na_attention",
    )(meta, p, p, p, table)


def _tile_conv(x, prev, nxt, w_ref, tile_idx, n_tiles, n_lat_tiles):
    r = x.shape[0]
    width = w_ref.shape[0]
    left = width // 2
    has_prev = jnp.logical_and(tile_idx != 0, tile_idx != n_lat_tiles)
    has_next = jnp.logical_and(tile_idx != n_tiles - 1, tile_idx != n_lat_tiles - 1)
    prev = jnp.where(has_prev, prev, 0.0)
    nxt = jnp.where(has_next, nxt, 0.0)
    xe = jnp.concatenate([prev, x, nxt], axis=0)
    acc = None
    for j in range(width):
        o = HALO - left + j
        term = xe[o:o + r, :] * w_ref[j:j + 1, :]
        acc = term if acc is None else acc + term
    return acc


def _halo_specs(width, col_block, tile_of):
    per = SEQ_TILE // HALO

    def main(bi, s, *_):
        return (bi, tile_of(s), col_block)

    def prev(bi, s, *_):
        return (bi, jnp.maximum(tile_of(s) * per - 1, 0), col_block)

    def make_next(n_tiles):
        def nxt(bi, s, *_):
            return (bi, jnp.minimum((tile_of(s) + 1) * per, n_tiles * per - 1), col_block)
        return nxt

    return main, prev, make_next


def _gdn_prep_kernel(x_ref, xp_ref, xn_ref, ba_ref, cw_ref, ones_ref, exp_ref, alog_ref, dtb_ref,
                     q_ref, k_ref, v_ref, beta_ref, g_ref, *, n_lat_tiles):
    i = pl.program_id(1)
    y = _tile_conv(x_ref[...], xp_ref[...], xn_ref[...], cw_ref, i, pl.num_programs(1), n_lat_tiles)
    y = y * _sigmoid(y)
    q = y[:, :GDN_W]
    k = y[:, GDN_W:2 * GDN_W]
    v_ref[...] = y[:, 2 * GDN_W:].astype(v_ref.dtype)

    def head_norm(u):
        parts = jnp.concatenate(_split3(u * u), axis=-1)
        ss = jnp.dot(parts, ones_ref[...], preferred_element_type=F32)
        return u * lax.rsqrt(ss + RMS_EPS)

    q_ref[...] = (head_norm(q) * (GDN_DK ** -0.5)).astype(q_ref.dtype)
    k_ref[...] = head_norm(k).astype(k_ref.dtype)

    ba = ba_ref[...]
    a = ba + dtb_ref[...]
    softplus = jnp.maximum(a, 0.0) + jnp.log1p(jnp.exp(-jnp.abs(a)))
    lane = lax.broadcasted_iota(jnp.int32, ba.shape, 1)
    compact = jnp.where(lane < 2 * GDN_HEADS, _sigmoid(ba), -jnp.exp(alog_ref[...]) * softplus)
    wide = jnp.dot(jnp.concatenate(_split3(compact), axis=-1), exp_ref[...], preferred_element_type=F32)
    half = 2 * GDN_W
    beta_ref[...] = wide[:, :half].astype(beta_ref.dtype)
    g_ref[...] = wide[:, half:]


def _gdn_prep(p, conv_w, a_log, dt_bias, n_lat):
    b, s, _ = p.shape
    n_tiles = s // SEQ_TILE
    main, prev, make_next = _halo_specs(3 * GDN_W, 0, lambda t: t)
    ones3 = jnp.concatenate([_head_block_ones(GDN_HEADS, GDN_DK)] * 3, axis=0)
    expand = np.zeros((128, 4 * GDN_W), np.float32)
    for kind in range(2):
        for d in range(2):
            for h in range(GDN_HEADS):
                c0 = kind * 2 * GDN_W + d * GDN_W + h * GDN_DV
                expand[kind * 2 * GDN_HEADS + d * GDN_HEADS + h, c0:c0 + GDN_DV] = 1.0
    expand3 = jnp.asarray(np.concatenate([expand] * 3, axis=0), BF16)
    lanes = jnp.zeros((1, 128), F32)
    alog_e = lanes.at[0, 2 * GDN_HEADS:4 * GDN_HEADS].set(a_log.astype(F32).reshape(-1))
    dtb_e = lanes.at[0, 2 * GDN_HEADS:4 * GDN_HEADS].set(dt_bias.astype(F32).reshape(-1))
    tok = lambda w: pl.BlockSpec((None, SEQ_TILE, w), lambda bi, i: (bi, i, 0))
    return pl.pallas_call(
        functools.partial(_gdn_prep_kernel, n_lat_tiles=n_lat // SEQ_TILE),
        grid=(b, n_tiles),
        in_specs=[pl.BlockSpec((None, SEQ_TILE, 3 * GDN_W), main),
                  pl.BlockSpec((None, HALO, 3 * GDN_W), prev),
                  pl.BlockSpec((None, HALO, 3 * GDN_W), make_next(n_tiles)),
                  pl.BlockSpec((None, SEQ_TILE, 128), lambda bi, i: (bi, i, P_GDN_BA // 128)),
                  _full(conv_w), _full(ones3), _full(expand3), _full(alog_e), _full(dtb_e)],
        out_specs=[tok(GDN_W), tok(GDN_W), tok(GDN_W), tok(2 * GDN_W), tok(2 * GDN_W)],
        out_shape=[jax.ShapeDtypeStruct((b, s, GDN_W), BF16)] * 3 + [jax.ShapeDtypeStruct((b, s, 2 * GDN_W), BF16),
                                                                    jax.ShapeDtypeStruct((b, s, 2 * GDN_W), F32)],
        compiler_params=_cparams(("parallel", "parallel")),
        name="gdn_prep",
    )(p, p, p, p, conv_w, ones3, expand3, alog_e, dtb_e)


GDN_PAIR = 2 * GDN_DK


def _gdn_masks():
    c, w = GDN_CHUNK, GDN_W
    r2, c2 = np.arange(GDN_PAIR)[:, None], np.arange(GDN_PAIR)[None, :]
    bd = ((r2 // c) == (c2 // c)).astype(np.float32)
    i = np.arange(c)[:, None]
    j = (np.arange(w) % c)[None, :]
    level = np.zeros((c, w), np.int32)
    for bit in range(6):
        level += ((i ^ j) >= (1 << bit)).astype(np.int32)
    lvl = np.stack([(level == m).astype(np.float32) for m in range(7)])
    dirm = np.stack([np.stack([(j <= i), (j < i)]), np.stack([(j >= i), (j > i)])]).astype(np.float32)
    tj = (np.arange(3 * c) % c)[None, :]
    tri = np.stack([(tj <= i), (tj >= i)]).astype(np.float32)
    return jnp.asarray(bd, BF16), jnp.asarray(lvl), jnp.asarray(dirm), jnp.asarray(tri, BF16)


def _heads_mm(x, y, bd, transpose_rhs=False):
    xb = x.astype(BF16)
    yb = y.astype(BF16)
    outs = []
    for pair in range(GDN_W // GDN_PAIR):
        ls = slice(pair * GDN_PAIR, (pair + 1) * GDN_PAIR)
        w = jnp.concatenate([yb[:, ls], yb[:, ls]], axis=0) * bd
        dims = (((1,), (1,)), ((), ())) if transpose_rhs else (((1,), (0,)), ((), ()))
        outs.append(lax.dot_general(xb[:, ls], w, dims, preferred_element_type=F32))
    return jnp.concatenate(outs, axis=1)


def _gdn_intra(probs, bd, lvl_ref, dirm_ref, tri_ref):
    c = GDN_CHUNK
    n = len(probs)
    eye = lvl_ref[0]
    gc, g_last, decay, gram = [], [], [], []
    for q, k, v, beta, g, rev in probs:
        d = 1 if rev else 0
        gcp = jnp.dot(tri_ref[d], jnp.concatenate(_split3(g), axis=0), preferred_element_type=F32)
        gc.append(gcp)
        g_last.append(gcp[0:1, :] if rev else gcp[c - 1:c, :])
        gc_row = jnp.sum(gcp * eye, axis=0, keepdims=True)
        decay.append(dirm_ref[d, 0] * jnp.exp(jnp.minimum(gcp - gc_row, 0.0)))
        gram.append(_heads_mm(jnp.concatenate([k, q], axis=0), k, bd, transpose_rhs=True))
    lower = [dirm_ref[1 if p[5] else 0, 1] * p[3] * gram[x][:c] * decay[x] for x, p in enumerate(probs)]
    a_intra = [gram[x][c:] * decay[x] for x in range(n)]
    t = [eye - lower[x] * lvl_ref[1] for x in range(n)]
    for lev in range(2, 7):
        y = [_heads_mm(t[x], lower[x] * lvl_ref[lev], bd) for x in range(n)]
        z = [_heads_mm(y[x], t[x], bd) for x in range(n)]
        t = [t[x] - z[x] for x in range(n)]
    e_gc = [jnp.exp(gc[x]) for x in range(n)]
    u = [_heads_mm(t[x], p[2] * p[3], bd) for x, p in enumerate(probs)]
    w = [_heads_mm(t[x], p[1] * p[3] * e_gc[x], bd) for x, p in enumerate(probs)]
    wq = [jnp.concatenate([w[x], p[0] * e_gc[x]], axis=0).astype(BF16) for x, p in enumerate(probs)]
    k_dec = [(p[1] * jnp.exp(g_last[x] - gc[x])).astype(BF16) for x, p in enumerate(probs)]
    g_tot = [jnp.exp(g_last[x]) for x in range(n)]
    return u, wq, k_dec, a_intra, g_tot


def _gdn_state_steps(items, bd):
    c = GDN_CHUNK
    pairs = [slice(p * GDN_PAIR, (p + 1) * GDN_PAIR) for p in range(GDN_W // GDN_PAIR)]
    bdf = bd.astype(F32)
    ws_qs = [jnp.concatenate([jnp.dot(wq[:, ls], s_ref[p].astype(BF16), preferred_element_type=F32)
                              for p, ls in enumerate(pairs)], axis=1)
             for _, wq, _, _, _, s_ref in items]
    v_new = [it[0] - ws[:c] for it, ws in zip(items, ws_qs)]
    outs = [ws[c:] + _heads_mm(it[3], vn, bd) for it, ws, vn in zip(items, ws_qs, v_new)]
    for (_, _, k_dec, _, g_tot, s_ref), vn in zip(items, v_new):
        vb = vn.astype(BF16)
        for p, ls in enumerate(pairs):
            kv = lax.dot_general(k_dec[:, ls], vb[:, ls], (((0,), (0,)), ((), ())), preferred_element_type=F32)
            s_ref[p] = s_ref[p] * g_tot[:, ls] + kv * bdf
    return outs


def _gdn_scan_kernel(qf, kf, vf, bf, gf, qb, kb, vb, bb, gb, bd_ref, lvl_ref, dirm_ref, tri_ref,
                     of_ref, ob_ref, sf_ref, sb_ref):
    @pl.when(pl.program_id(1) == 0)
    def _():
        sf_ref[...] = jnp.zeros(sf_ref.shape, F32)
        sb_ref[...] = jnp.zeros(sb_ref.shape, F32)

    bd = bd_ref[...]
    n = SEQ_TILE // GDN_CHUNK
    probs, rows = [], []
    for c in range(n):
        rf = slice(c * GDN_CHUNK, (c + 1) * GDN_CHUNK)
        rb = slice((n - 1 - c) * GDN_CHUNK, (n - c) * GDN_CHUNK)
        probs.append(tuple(r[rf, :].astype(F32) for r in (qf, kf, vf, bf, gf)) + (False,))
        probs.append(tuple(r[rb, :].astype(F32) for r in (qb, kb, vb, bb, gb)) + (True,))
        rows += [rf, rb]
    u, wq, k_dec, a_intra, g_tot = _gdn_intra(probs, bd, lvl_ref, dirm_ref, tri_ref)
    for c in range(n):
        xs = (2 * c, 2 * c + 1)
        outs = _gdn_state_steps([(u[x], wq[x], k_dec[x], a_intra[x], g_tot[x], sb_ref if probs[x][5] else sf_ref)
                                 for x in xs], bd)
        for x, o in zip(xs, outs):
            (ob_ref if probs[x][5] else of_ref)[rows[x], :] = o


def _seq_tile_maps(n_lat_tiles, n_tiles):
    fwd = lambda s: lax.rem(s + n_lat_tiles, n_tiles)
    bwd = lambda s: n_tiles - 1 - s
    return fwd, bwd


def _gdn_scan(q, k, v, beta, g, n_lat):
    b, s, _ = q.shape
    n_tiles = s // SEQ_TILE
    fwd, bwd = _seq_tile_maps(n_lat // SEQ_TILE, n_tiles)
    spec = lambda tile_of, col: pl.BlockSpec((None, SEQ_TILE, GDN_W), lambda bi, t: (bi, tile_of(t), col))
    masks = _gdn_masks()
    in_specs = [spec(fwd, 0)] * 5 + [spec(bwd, 0)] * 3 + [spec(bwd, 1)] * 2 + [_full(m) for m in masks]
    return pl.pallas_call(
        _gdn_scan_kernel,
        grid=(b, n_tiles),
        in_specs=in_specs,
        out_specs=[spec(fwd, 0), spec(bwd, 0)],
        out_shape=[jax.ShapeDtypeStruct((b, s, GDN_W), F32)] * 2,
        scratch_shapes=[pltpu.VMEM((GDN_W // GDN_PAIR, GDN_PAIR, GDN_PAIR), F32)] * 2,
        compiler_params=_cparams(("parallel", "arbitrary")),
        name="gdn_scan",
    )(q, k, v, beta, g, q, k, v, beta, g, *masks)


def _lru_tile_prep(x_ref, xp_ref, xn_ref, cw_ref, cb_ref, wg_ref, bg_ref, nla_ref, a_ref, b_ref,
                   tile_idx, n_tiles, n_lat_tiles):
    xb = _tile_conv(x_ref[...], xp_ref[...], xn_ref[...], cw_ref, tile_idx, n_tiles, n_lat_tiles) + cb_ref[...]
    gates = _sigmoid(jnp.dot(xb.astype(BF16), wg_ref[...], preferred_element_type=F32) + bg_ref[...])
    log_a = nla_ref[...] * gates[:, :LRU_W]
    a_ref[...] = jnp.exp(log_a)
    th = jnp.tanh(log_a)
    b_ref[...] = jnp.sqrt(-2.0 * th / (1.0 - th)) * gates[:, LRU_W:] * xb


def _scan_group(a, b, h, row, reverse):
    for d in (1, 2, 4):
        if reverse:
            keep = row < 8 - d
            shift = 8 - d
        else:
            keep = row >= d
            shift = d
        a_s = jnp.where(keep, pltpu.roll(a, shift, 0), 1.0)
        b_s = jnp.where(keep, pltpu.roll(b, shift, 0), 0.0)
        b = a * b_s + b
        a = a * a_s
    return a * h + b


def _lru_scan_kernel(xf, xfp, xfn, xb, xbp, xbn, cw_ref, cb_ref, wgf, bgf, nlaf, wgb, bgb, nlab,
                     hf_ref, hb_ref, af_ref, bf_ref, ab_ref, bb_ref, cf_ref, cbk_ref, *, n_lat_tiles):
    s = pl.program_id(1)
    n_tiles = pl.num_programs(1)

    @pl.when(s == 0)
    def _():
        cf_ref[...] = jnp.zeros(cf_ref.shape, F32)
        cbk_ref[...] = jnp.zeros(cbk_ref.shape, F32)

    t_f, t_b = (m(s) for m in _seq_tile_maps(n_lat_tiles, n_tiles))
    _lru_tile_prep(xf, xfp, xfn, cw_ref, cb_ref, wgf, bgf, nlaf, af_ref, bf_ref, t_f, n_tiles, n_lat_tiles)
    _lru_tile_prep(xb, xbp, xbn, cw_ref, cb_ref, wgb, bgb, nlab, ab_ref, bb_ref, t_b, n_tiles, n_lat_tiles)

    n_groups = SEQ_TILE // 8
    row = lax.broadcasted_iota(jnp.int32, (8, LRU_W), 0)

    def body(gi, carry):
        h_f, h_b = carry
        rf = pl.ds(pl.multiple_of(gi * 8, 8), 8)
        rb = pl.ds(pl.multiple_of((n_groups - 1 - gi) * 8, 8), 8)
        out_f = _scan_group(af_ref[rf, :], bf_ref[rf, :], h_f, row, False)
        out_b = _scan_group(ab_ref[rb, :], bb_ref[rb, :], h_b, row, True)
        hf_ref[rf, :] = out_f
        hb_ref[rb, :] = out_b
        return (jnp.broadcast_to(out_f[7:8, :], (8, LRU_W)), jnp.broadcast_to(out_b[0:1, :], (8, LRU_W)))

    h_f, h_b = lax.fori_loop(0, n_groups, body, (cf_ref[...], cbk_ref[...]))
    cf_ref[...] = h_f
    cbk_ref[...] = h_b


def _lru_scan(p, conv_w, conv_b, w_r, b_r, w_i, b_i, lam, n_lat):
    b, s, _ = p.shape
    n_tiles = s // SEQ_TILE
    nlt = n_lat // SEQ_TILE
    fwd, bwd = _seq_tile_maps(nlt, n_tiles)
    col = P_LRU_X // LRU_W
    specs = []
    for tile_of in (fwd, bwd):
        main, prev, make_next = _halo_specs(LRU_W, col, tile_of)
        specs += [pl.BlockSpec((None, SEQ_TILE, LRU_W), main), pl.BlockSpec((None, HALO, LRU_W), prev),
                  pl.BlockSpec((None, HALO, LRU_W), make_next(n_tiles))]

    def blockdiag(w):
        return jax.scipy.linalg.block_diag(*[w[n] for n in range(LRU_BLOCKS)])

    dir_args = []
    for d in range(2):
        wg = jnp.concatenate([blockdiag(w_r[d]), blockdiag(w_i[d])], axis=1).astype(BF16)
        bg = jnp.concatenate([b_r[d], b_i[d]]).astype(F32).reshape(1, 2 * LRU_W)
        nla = (-LRU_C * jax.nn.softplus(-lam[d].astype(F32))).reshape(1, LRU_W)
        dir_args += [wg, bg, nla]
    cb2 = conv_b.reshape(1, LRU_W)
    out_spec = lambda tile_of: pl.BlockSpec((None, SEQ_TILE, LRU_W), lambda bi, t: (bi, tile_of(t), 0))
    return pl.pallas_call(
        functools.partial(_lru_scan_kernel, n_lat_tiles=nlt),
        grid=(b, n_tiles),
        in_specs=specs + [_full(conv_w), _full(cb2)] + [_full(a) for a in dir_args],
        out_specs=[out_spec(fwd), out_spec(bwd)],
        out_shape=[jax.ShapeDtypeStruct((b, s, LRU_W), F32)] * 2,
        scratch_shapes=[pltpu.VMEM((SEQ_TILE, LRU_W), F32)] * 4 + [pltpu.VMEM((8, LRU_W), F32)] * 2,
        compiler_params=_cparams(("parallel", "arbitrary")),
        name="lru_scan",
    )(p, p, p, p, p, p, conv_w, cb2, *dir_args)


def _post_kernel(of_ref, ob_ref, z_ref, gn_ref, ones_ref, hf_ref, hb_ref, y_ref, ua_ref, ub_ref):
    o = of_ref[...] + ob_ref[...]
    ms = jnp.dot(jnp.concatenate(_split3(o * o), axis=-1), ones_ref[...], preferred_element_type=F32) * (1.0 / GDN_DV)
    z = z_ref[...]
    ua_ref[...] = (o * lax.rsqrt(ms + RMS_EPS) * gn_ref[...] * (z * _sigmoid(z))).astype(ua_ref.dtype)
    y = y_ref[...]
    gelu = 0.5 * y * (1.0 + jnp.tanh(0.7978845608028654 * (y + 0.044715 * (y * y * y))))
    ub_ref[...] = ((hf_ref[...] + hb_ref[...]) * gelu).astype(ub_ref.dtype)


def _post(o_f, o_b, h_f, h_b, p, gdn_norm_g):
    b, s, _ = p.shape
    tm = _pick_tile(s, 544)
    ones3 = jnp.concatenate([_head_block_ones(GDN_HEADS, GDN_DV)] * 3, axis=0)
    gn = jnp.tile(gdn_norm_g.astype(F32), GDN_HEADS).reshape(1, GDN_W)
    tok = pl.BlockSpec((None, tm, GDN_W), lambda bi, i: (bi, i, 0))
    pcol = lambda c: pl.BlockSpec((None, tm, GDN_W), lambda bi, i: (bi, i, c))
    return pl.pallas_call(
        _post_kernel,
        grid=(b, s // tm),
        in_specs=[tok, tok, pcol(P_GDN_Z // GDN_W), _full(gn), _full(ones3), tok, tok, pcol(P_LRU_Y // LRU_W)],
        out_specs=[tok, tok],
        out_shape=[jax.ShapeDtypeStruct((b, s, GDN_W), BF16)] * 2,
        compiler_params=_cparams(("parallel", "parallel")),
        name="gdn_lru_post",
    )(o_f, o_b, p, gn, ones3, h_f, h_b, p)


def _merge_kernel(x_ref, mc_ref, ml_ref, u0_ref, u1_ref, u2_ref, u3_ref, t0_ref, t1_ref, t2_ref, t3_ref,
                  bg_ref, wb_ref, wo_ref, o_ref, *, n_lat):
    merged = None
    for n, (u_ref, t_ref) in enumerate(((u0_ref, t0_ref), (u1_ref, t1_ref), (u2_ref, t2_ref), (u3_ref, t3_ref))):
        gate = _sigmoid(t_ref[...] + bg_ref[n:n + 1, :])
        term = gate * jnp.dot(u_ref[...], wb_ref[n], preferred_element_type=F32)
        merged = term if merged is None else merged + term
    out = jnp.dot(merged.astype(BF16), wo_ref[...], preferred_element_type=F32)
    g1 = _row_select(x_ref.shape[0], pl.program_id(1), n_lat, mc_ref[2:3, :], ml_ref[2:3, :])
    o_ref[...] = x_ref[...] + g1 * out


def _merge(xs, mod_c, mod_l, branches, p, b_gate, w_branch, w_out, layer, n_lat, with_ctx):
    b, s, d = xs.shape
    rows = s if with_ctx else n_lat
    tm = _pick_tile(rows, 544) if with_ctx else SEQ_TILE
    gate_blk = P_GATES // d
    tok = lambda w: pl.BlockSpec((None, tm, w), lambda bi, i: (bi, i, 0))
    in_specs = [tok(d), pl.BlockSpec((8, d), lambda bi, i: (0, 0)), pl.BlockSpec((None, 8, d), lambda bi, i: (bi, 0, 0))]
    in_specs += [tok(BRANCH_W)] * N_BRANCH
    in_specs += [pl.BlockSpec((None, tm, d), functools.partial(lambda bi, i, n: (bi, i, gate_blk + n), n=n))
                 for n in range(N_BRANCH)]
    in_specs += [_full(b_gate),
                 pl.BlockSpec((None,) + w_branch.shape[1:], lambda bi, i: (layer, 0, 0, 0)),
                 pl.BlockSpec((None,) + w_out.shape[1:], lambda bi, i: (layer, 0, 0))]
    return pl.pallas_call(
        functools.partial(_merge_kernel, n_lat=n_lat),
        grid=(b, rows // tm),
        in_specs=in_specs,
        out_specs=tok(d),
        out_shape=jax.ShapeDtypeStruct((b, rows, d), F32),
        input_output_aliases={0: 0} if with_ctx else {},
        compiler_params=_cparams(("parallel", "parallel")),
        name="merge",
    )(xs, mod_c, mod_l, *branches, p, p, p, p, b_gate, w_branch, w_out)


def _mlp_kernel(x_ref, mc_ref, ml_ref, gn_ref, gf_ref, w1_ref, w2_ref, o_ref, h_ref, acc_ref, *, n_lat, final_norm):
    f = pl.program_id(2)
    tm = x_ref.shape[0]
    i = pl.program_id(1)

    @pl.when(f == 0)
    def _():
        x = x_ref[...]
        y = x * lax.rsqrt(jnp.mean(x * x, axis=-1, keepdims=True) + RMS_EPS)
        y = y * gn_ref[...]
        shift = _row_select(tm, i, n_lat, mc_ref[3:4, :], ml_ref[3:4, :])
        scale = _row_select(tm, i, n_lat, mc_ref[4:5, :], ml_ref[4:5, :])
        h_ref[...] = (y * (1.0 + scale) + shift).astype(BF16)

    a = jnp.maximum(jnp.dot(h_ref[...], w1_ref[...], preferred_element_type=F32), 0.0)
    part = jnp.dot((a * a).astype(BF16), w2_ref[...], preferred_element_type=F32)

    @pl.when(f == 0)
    def _():
        acc_ref[...] = part

    @pl.when(f > 0)
    def _():
        acc_ref[...] += part

    @pl.when(f == pl.num_programs(2) - 1)
    def _():
        g2 = _row_select(tm, i, n_lat, mc_ref[5:6, :], ml_ref[5:6, :])
        y = x_ref[...] + g2 * acc_ref[...]
        if final_norm:
            y = y * lax.rsqrt(jnp.mean(y * y, axis=-1, keepdims=True) + RMS_EPS) * gf_ref[...]
        o_ref[...] = y


def _mlp(xs, mod_c, mod_l, gain, w1, w2, layer, final_gain, n_lat, final_norm):
    b, rows, d = xs.shape
    dff = w1.shape[-1]
    tm = _pick_tile(rows, 1088)
    tf = 1024
    row = pl.BlockSpec((1, d), lambda bi, i, f: (0, 0))
    return pl.pallas_call(
        functools.partial(_mlp_kernel, n_lat=n_lat, final_norm=final_norm),
        grid=(b, rows // tm, dff // tf),
        in_specs=[pl.BlockSpec((None, tm, d), lambda bi, i, f: (bi, i, 0)),
                  pl.BlockSpec((8, d), lambda bi, i, f: (0, 0)),
                  pl.BlockSpec((None, 8, d), lambda bi, i, f: (bi, 0, 0)),
                  row, row,
                  pl.BlockSpec((None, d, tf), lambda bi, i, f: (layer, 0, f)),
                  pl.BlockSpec((None, tf, d), lambda bi, i, f: (layer, f, 0))],
        out_specs=pl.BlockSpec((None, tm, d), lambda bi, i, f: (bi, i, 0)),
        out_shape=jax.ShapeDtypeStruct((b, rows, d), F32),
        scratch_shapes=[pltpu.VMEM((tm, d), BF16), pltpu.VMEM((tm, d), F32)],
        compiler_params=_cparams(("parallel", "parallel", "arbitrary")),
        name="mlp",
    )(xs, mod_c, mod_l, gain, final_gain, w1, w2)


def kernel(x, c, ctx, c_ctx, mod_w, mod_b, norm1_g, norm2_g, w_in, b_gate, gdn_conv_w, gdn_a_log, gdn_dt_bias,
           gdn_norm_g, lru_conv_w, lru_conv_b, lru_w_r, lru_b_r, lru_w_i, lru_b_i, lru_lambda, mla_q_norm_g,
           mla_w_uq, mla_kv_norm_g, mla_w_ukv, na_rpb, w_branch, w_out, mlp_w1, mlp_w2, final_norm_g):
    bsz, n_tok, d = x.shape
    n_ctx = ctx.shape[1]
    depth = w_in.shape[0]
    assert n_ctx % SEQ_TILE == 0 and n_tok % SEQ_TILE == 0 and n_tok % GRID_W == 0
    na_meta, na_rel_rows, na_cols = _na_geometry(n_tok // GRID_W)
    na_meta = jnp.asarray(na_meta)
    cos, sin = _rope_tables(n_tok, n_ctx)

    n_rows = -(-(bsz + 1) // 8) * 8
    cc = jnp.zeros((n_rows, d), F32).at[:bsz].set(c).at[bsz].set(c_ctx)
    final_gain = final_norm_g.reshape(1, d)

    w_in_all = _arrange_w_in(w_in)
    wb_all = w_branch.astype(BF16)
    wo_all = w_out.astype(BF16)
    w1_all = mlp_w1.astype(BF16)
    w2_all = mlp_w2.astype(BF16)

    xs = jnp.concatenate([x, ctx], axis=1)
    for l in range(depth):
        need_ctx = l < depth - 1
        mod = _modulation(cc, mod_w[l], mod_b[l]).reshape(n_rows, N_MOD, d)
        pad = jnp.zeros((8 - N_MOD, d), F32)
        mod_c = jnp.concatenate([mod[bsz], pad], axis=0)
        mod_l = jnp.concatenate([mod[:bsz], jnp.broadcast_to(pad, (bsz, 8 - N_MOD, d))], axis=1)

        wq, wk, wv, place = _arrange_mla(mla_w_uq[l], mla_w_ukv[l])
        gq = mla_q_norm_g[l].reshape(1, -1)
        gkv = mla_kv_norm_g[l].reshape(1, -1)
        g1n = norm1_g[l].reshape(1, d)
        g2n = norm2_g[l].reshape(1, d)

        p32, p16 = _inproj(xs, mod_c, mod_l, g1n, w_in_all, l, n_tok)

        gq_, gk_, gv_, gbeta, gg = _gdn_prep(p32, gdn_conv_w[l], gdn_a_log[l], gdn_dt_bias[l], n_tok)
        o_f, o_b = _gdn_scan(gq_, gk_, gv_, gbeta, gg, n_tok)
        h_f, h_b = _lru_scan(p32, lru_conv_w[l], lru_conv_b[l], lru_w_r[l], lru_b_r[l], lru_w_i[l], lru_b_i[l],
                             lru_lambda[l], n_tok)
        ua, ub = _post(o_f, o_b, h_f, h_b, p32, gdn_norm_g[l])

        mq, mk, mv = _mla_prep(p32, p16, cos, sin, gq, gkv, wq, wk, wv, place)
        uc = _mla_flash(mq, mk, mv, n_tok)
        if need_ctx:
            uc = _mla_flash(mq, mk, mv, n_tok, prev=uc)
        ud = _na_attention(p16, na_meta, _na_bias_table(na_rpb[l], na_rel_rows, na_cols), n_tok, need_ctx)

        xs = _merge(xs, mod_c, mod_l, (ua, ub, uc, ud), p16, b_gate[l], wb_all, wo_all, l, n_tok, need_ctx)
        xs = _mlp(xs, mod_c, mod_l, g2n, w1_all, w2_all, l, final_gain, n_tok, l == depth - 1)
    return xs
```

```python
import functools

import jax
import jax.numpy as jnp
import numpy as np
from jax import lax
from jax.experimental import pallas as pl
from jax.experimental.pallas import tpu as pltpu

F32 = jnp.float32
BF16 = jnp.bfloat16

GRID_W = 64
N_MOD = 6
RMS_EPS = 1e-6
GDN_HEADS = 4
GDN_DK = 64
GDN_DV = 64
GDN_CHUNK = 64
GDN_W = GDN_HEADS * GDN_DV
GDN_CONV = 4
LRU_W = 256
LRU_BLOCKS = 4
LRU_BLOCK_W = LRU_W // LRU_BLOCKS
LRU_CONV = 4
LRU_C = 8.0
MLA_HEADS = 4
MLA_Q_RANK = 256
MLA_KV_RANK = 128
MLA_NOPE = 64
MLA_ROPE = 32
MLA_V = 64
MLA_SLOT = 128
ROPE_BASE = 10000.0
LOG2_E = 1.4426950408889634
NA_HEADS = 4
NA_DH = 64
NA_W = NA_HEADS * NA_DH
NA_WIN_ROWS = 8
NA_WIN_COLS = 16
N_BRANCH = 4
BRANCH_W = 256

SEQ_TILE = 256
HALO = 8

_REF_COLS = {}
_off = 0
for _name, _w in (('gdn_qkv', 3 * GDN_W), ('gdn_z', GDN_W), ('gdn_beta', 2 * GDN_HEADS), ('gdn_a', 2 * GDN_HEADS),
                  ('lru_x', LRU_W), ('lru_y', LRU_W), ('mla_q', MLA_Q_RANK), ('mla_kv', MLA_KV_RANK),
                  ('mla_kr', MLA_ROPE), ('na_qkv', 3 * NA_W)):
    _REF_COLS[_name] = (_off, _w)
    _off += _w
N_MIX_COLS = _off

P_GDN_QKV = 0
P_GDN_Z = 768
P_LRU_X = 1024
P_LRU_Y = 1280
P_MLA_KV = 1536
P_MLA_KR = 1664
P_GDN_BA = 1792
P32_COLS = 2048
P_MLA_Q = 0
P_NA_QKV = 256
P_GATES = 1024
P16_COLS = P_GATES + N_BRANCH * 1024
PROJ_TILE = 1024

VMEM_LIMIT = 52 * 1024 * 1024


def _cparams(sem):
    return pltpu.CompilerParams(dimension_semantics=sem, vmem_limit_bytes=VMEM_LIMIT)


def _pick_tile(n, cap):
    best = 8
    for t in range(8, min(n, cap) + 1, 8):
        if n % t == 0:
            best = t
    return best


def _full(a):
    return pl.BlockSpec(a.shape, lambda *_: (0,) * a.ndim)


def _split3(x):
    hi = x.astype(BF16)
    r = x - hi.astype(F32)
    mid = r.astype(BF16)
    lo = (r - mid.astype(F32)).astype(BF16)
    return hi, mid, lo


def _sigmoid(x):
    return 0.5 * jnp.tanh(0.5 * x) + 0.5


def _arrange_w_in(w_in):
    pieces, pos = [], 0

    def put(dst, block):
        nonlocal pos
        if dst > pos:
            pieces.append(jnp.zeros(w_in.shape[:-1] + (dst - pos,), w_in.dtype))
        pieces.append(block)
        pos = dst + block.shape[-1]

    ref = lambda name: w_in[..., _REF_COLS[name][0]:_REF_COLS[name][0] + _REF_COLS[name][1]]
    for name, dst in (('gdn_qkv', P_GDN_QKV), ('gdn_z', P_GDN_Z), ('lru_x', P_LRU_X), ('lru_y', P_LRU_Y),
                      ('mla_kv', P_MLA_KV), ('mla_kr', P_MLA_KR), ('gdn_beta', P_GDN_BA),
                      ('gdn_a', P_GDN_BA + 2 * GDN_HEADS)):
        put(dst, ref(name))
    put(P32_COLS + P_MLA_Q, ref('mla_q'))
    put(P32_COLS + P_NA_QKV, ref('na_qkv'))
    put(P32_COLS + P_GATES, w_in[..., N_MIX_COLS:])
    assert pos == P32_COLS + P16_COLS
    return jnp.concatenate(pieces, axis=-1).astype(BF16)


def _rope_perm():
    q = MLA_ROPE // 4
    src = np.zeros(MLA_ROPE, np.int32)
    sign = np.zeros(MLA_ROPE, np.float32)
    for base in (0, 2 * q):
        for d in range(q):
            src[base + d] = base + d + q
            sign[base + d] = -1.0
            src[base + q + d] = base + d
            sign[base + q + d] = 1.0
    return src, sign


def _arrange_mla(w_uq, w_ukv):
    src, sign = _rope_perm()
    hq = MLA_NOPE + MLA_ROPE
    wq = jnp.zeros((MLA_Q_RANK, 2 * MLA_HEADS * MLA_SLOT), F32)
    wk = jnp.zeros((MLA_KV_RANK, MLA_HEADS * MLA_SLOT), F32)
    wv = jnp.zeros((MLA_KV_RANK, MLA_HEADS * MLA_V), F32)
    place = np.zeros((2 * MLA_SLOT, 2 * MLA_HEADS * MLA_SLOT), np.float32)
    rot_off = MLA_HEADS * MLA_SLOT
    for h in range(MLA_HEADS):
        nope = w_uq[:, h * hq:h * hq + MLA_NOPE]
        pe = w_uq[:, h * hq + MLA_NOPE:(h + 1) * hq]
        s = h * MLA_SLOT
        wq = wq.at[:, s:s + MLA_NOPE].set(nope)
        wq = wq.at[:, s + MLA_NOPE:s + MLA_NOPE + MLA_ROPE].set(pe)
        wq = wq.at[:, rot_off + s + MLA_NOPE:rot_off + s + MLA_NOPE + MLA_ROPE].set(pe[:, src] * sign)
        wk = wk.at[:, s:s + MLA_NOPE].set(w_ukv[:, h * 128:h * 128 + MLA_NOPE])
        wv = wv.at[:, h * MLA_V:(h + 1) * MLA_V].set(w_ukv[:, h * 128 + MLA_NOPE:(h + 1) * 128])
        for d in range(MLA_ROPE):
            for half in (0, MLA_SLOT):
                place[half + d, s + MLA_NOPE + d] = 1.0
                place[half + src[d], rot_off + s + MLA_NOPE + d] = sign[d]
    return wq.astype(BF16), wk.astype(BF16), wv.astype(BF16), jnp.asarray(place, BF16)


def _rope_tables(n_tok, n_ctx):
    cos = np.ones((n_tok + n_ctx, MLA_SLOT), np.float32)
    sin = np.zeros((n_tok + n_ctx, MLA_SLOT), np.float32)
    t = np.arange(n_tok)
    row = (t // GRID_W).astype(np.float32)
    col = (t % GRID_W).astype(np.float32)
    n_freq = MLA_ROPE // 4
    inv = (ROPE_BASE ** (-np.arange(n_freq, dtype=np.float32) / n_freq)).astype(np.float32)
    ar = row[:, None] * inv
    ac = col[:, None] * inv
    ang = np.concatenate([ar, ar, ac, ac], axis=-1).astype(np.float32)
    cos[:n_tok, MLA_NOPE:MLA_NOPE + MLA_ROPE] = np.cos(ang)
    sin[:n_tok, MLA_NOPE:MLA_NOPE + MLA_ROPE] = np.sin(ang)
    return jnp.asarray(cos), jnp.asarray(sin)


def _head_block_ones(n_heads, width):
    m = np.kron(np.eye(n_heads, dtype=np.float32), np.ones((width, width), np.float32))
    return jnp.asarray(m, BF16)


def _mod_kernel(c_ref, w_ref, b_ref, o_ref):
    c = c_ref[...]
    s = c * _sigmoid(c)
    o_ref[...] = jnp.dot(s, w_ref[...], preferred_element_type=F32) + b_ref[...]


def _modulation(cc, mod_w, mod_b):
    r, d = cc.shape
    n = mod_w.shape[1]
    tn = 1024
    return pl.pallas_call(
        _mod_kernel,
        grid=(n // tn,),
        in_specs=[pl.BlockSpec((r, d), lambda j: (0, 0)),
                  pl.BlockSpec((d, tn), lambda j: (0, j)),
                  pl.BlockSpec((1, tn), lambda j: (0, j))],
        out_specs=pl.BlockSpec((r, tn), lambda j: (0, j)),
        out_shape=jax.ShapeDtypeStruct((r, n), F32),
        compiler_params=_cparams(("arbitrary",)),
        name="modulation",
    )(cc, mod_w, mod_b.reshape(1, n))


def _row_select(tile_rows, tile_idx, n_lat, ctx_vec, lat_vec):
    row = tile_idx * tile_rows + lax.broadcasted_iota(jnp.int32, (tile_rows, 1), 0)
    return jnp.where(row < n_lat, lat_vec, ctx_vec)


def _inproj_kernel(x_ref, mc_ref, ml_ref, g_ref, w_ref, o32_ref, o16_ref, h_ref, *, n_lat):
    j = pl.program_id(2)
    n32 = P32_COLS // PROJ_TILE

    @pl.when(j == 0)
    def _():
        tm = x_ref.shape[0]
        i = pl.program_id(1)
        x = x_ref[...]
        y = x * lax.rsqrt(jnp.mean(x * x, axis=-1, keepdims=True) + RMS_EPS)
        y = y * g_ref[...]
        shift = _row_select(tm, i, n_lat, mc_ref[0:1, :], ml_ref[0:1, :])
        scale = _row_select(tm, i, n_lat, mc_ref[1:2, :], ml_ref[1:2, :])
        h_ref[...] = (y * (1.0 + scale) + shift).astype(BF16)

    acc = jnp.dot(h_ref[...], w_ref[...], preferred_element_type=F32)

    @pl.when(j < n32)
    def _():
        o32_ref[...] = acc

    @pl.when(j >= n32)
    def _():
        o16_ref[...] = acc.astype(BF16)


def _inproj(xs, mod_c, mod_l, gain, w, layer, n_lat):
    b, s, d = xs.shape
    tm = _pick_tile(s, 1088)
    tn = PROJ_TILE
    n32 = P32_COLS // tn
    return pl.pallas_call(
        functools.partial(_inproj_kernel, n_lat=n_lat),
        grid=(b, s // tm, (P32_COLS + P16_COLS) // tn),
        in_specs=[pl.BlockSpec((None, tm, d), lambda bi, i, j: (bi, i, 0)),
                  pl.BlockSpec((8, d), lambda bi, i, j: (0, 0)),
                  pl.BlockSpec((None, 8, d), lambda bi, i, j: (bi, 0, 0)),
                  pl.BlockSpec((1, d), lambda bi, i, j: (0, 0)),
                  pl.BlockSpec((None, d, tn), lambda bi, i, j: (layer, 0, j))],
        out_specs=[pl.BlockSpec((None, tm, tn), lambda bi, i, j: (bi, i, jnp.minimum(j, n32 - 1))),
                   pl.BlockSpec((None, tm, tn), lambda bi, i, j: (bi, i, jnp.maximum(j - n32, 0)))],
        out_shape=[jax.ShapeDtypeStruct((b, s, P32_COLS), F32), jax.ShapeDtypeStruct((b, s, P16_COLS), BF16)],
        scratch_shapes=[pltpu.VMEM((tm, d), BF16)],
        compiler_params=_cparams(("parallel", "parallel", "arbitrary")),
        name="inproj",
    )(xs, mod_c, mod_l, gain, w)


def _mla_prep_kernel(ql_ref, kv_ref, cos_ref, sin_ref, gq_ref, gkv_ref, wq_ref, wk_ref, wv_ref, pl_ref,
                     q_ref, k_ref, v_ref, *, scale):
    nslot = MLA_HEADS * MLA_SLOT
    cos = jnp.concatenate([cos_ref[...]] * MLA_HEADS, axis=-1)
    sin = jnp.concatenate([sin_ref[...]] * MLA_HEADS, axis=-1)

    ql = ql_ref[...].astype(F32)
    qn = ql * lax.rsqrt(jnp.mean(ql * ql, axis=-1, keepdims=True) + RMS_EPS) * gq_ref[...]
    q2 = jnp.dot(qn.astype(BF16), wq_ref[...], preferred_element_type=F32)
    q = (q2[:, :nslot] * cos + q2[:, nslot:] * sin) * scale
    q_ref[...] = q.astype(BF16)

    kvkr = kv_ref[...]
    kvl = kvkr[:, :MLA_KV_RANK]
    kr = kvkr[:, MLA_KV_RANK:]
    kvn = (kvl * lax.rsqrt(jnp.mean(kvl * kvl, axis=-1, keepdims=True) + RMS_EPS) * gkv_ref[...]).astype(BF16)
    kn = jnp.dot(kvn, wk_ref[...], preferred_element_type=F32)
    v_ref[...] = jnp.dot(kvn, wv_ref[...], preferred_element_type=F32).astype(BF16)
    kr_hi = kr.astype(BF16)
    kr_lo = (kr - kr_hi.astype(F32)).astype(BF16)
    kr2 = jnp.dot(jnp.concatenate([kr_hi, kr_lo], axis=-1), pl_ref[...], preferred_element_type=F32)
    k_ref[...] = (kn + kr2[:, :nslot] * cos + kr2[:, nslot:] * sin).astype(BF16)


def _mla_prep(p32, p16, cos, sin, gq, gkv, wq, wk, wv, place):
    b, s, _ = p32.shape
    tm = _pick_tile(s, 544)
    nslot = MLA_HEADS * MLA_SLOT
    scale = (MLA_NOPE + MLA_ROPE) ** -0.5 * LOG2_E
    return pl.pallas_call(
        functools.partial(_mla_prep_kernel, scale=scale),
        grid=(b, s // tm),
        in_specs=[pl.BlockSpec((None, tm, MLA_Q_RANK), lambda bi, i: (bi, i, P_MLA_Q // MLA_Q_RANK)),
                  pl.BlockSpec((None, tm, 2 * MLA_SLOT), lambda bi, i: (bi, i, P_MLA_KV // (2 * MLA_SLOT))),
                  pl.BlockSpec((tm, MLA_SLOT), lambda bi, i: (i, 0)),
                  pl.BlockSpec((tm, MLA_SLOT), lambda bi, i: (i, 0)),
                  _full(gq), _full(gkv), _full(wq), _full(wk), _full(wv), _full(place)],
        out_specs=[pl.BlockSpec((None, tm, nslot), lambda bi, i: (bi, i, 0)),
                   pl.BlockSpec((None, tm, nslot), lambda bi, i: (bi, i, 0)),
                   pl.BlockSpec((None, tm, MLA_HEADS * MLA_V), lambda bi, i: (bi, i, 0))],
        out_shape=[jax.ShapeDtypeStruct((b, s, nslot), BF16),
                   jax.ShapeDtypeStruct((b, s, nslot), BF16),
                   jax.ShapeDtypeStruct((b, s, MLA_HEADS * MLA_V), BF16)],
        compiler_params=_cparams(("parallel", "parallel")),
        name="mla_prep",
    )(p16, p32, cos, sin, gq, gkv, wq, wk, wv, place)


def _flash_softmax(h, s, m_ref, l_ref):
    m_prev = m_ref[h]
    m_new = jnp.maximum(m_prev, jnp.max(s, axis=-1, keepdims=True))
    alpha = jnp.exp2(m_prev - m_new)
    p = jnp.exp2(s - jnp.concatenate([m_new] * (s.shape[1] // 128), axis=-1))
    l_ref[h] = alpha * l_ref[h] + jnp.sum(p, axis=-1, keepdims=True)
    m_ref[h] = m_new
    return alpha, p.astype(BF16)


def _mla_flash_kernel(*refs, tk, aliased):
    if aliased:
        q_ref, k_ref, v_ref, _, o_ref, m_ref, l_ref, acc_ref = refs
    else:
        q_ref, k_ref, v_ref, o_ref, m_ref, l_ref, acc_ref = refs
    tq = q_ref.shape[0]
    nk = k_ref.shape[0]
    n_loop = nk // tk
    m_ref[...] = jnp.full(m_ref.shape, -jnp.inf, F32)
    l_ref[...] = jnp.zeros(l_ref.shape, F32)
    acc_ref[...] = jnp.zeros(acc_ref.shape, F32)

    def chunk(rows):
        heads = range(MLA_HEADS)
        hs = [slice(h * MLA_SLOT, (h + 1) * MLA_SLOT) for h in heads]
        vs = [slice((h // 2) * 128, (h // 2) * 128 + 128) for h in heads]
        s = [lax.dot_general(q_ref[:, hs[h]], k_ref[rows, hs[h]], (((1,), (1,)), ((), ())),
                             preferred_element_type=F32) for h in heads]
        ap = [_flash_softmax(h, s[h], m_ref, l_ref) for h in heads]
        for h in heads:
            alpha, p = ap[h]
            acc_ref[h] = alpha * acc_ref[h] + jnp.dot(p, v_ref[rows, vs[h]], preferred_element_type=F32)

    if n_loop:
        def body(j, carry):
            chunk(pl.ds(pl.multiple_of(j * tk, tk), tk))
            return carry
        lax.fori_loop(0, n_loop, body, 0)
    if nk > n_loop * tk:
        chunk(slice(n_loop * tk, nk))

    lane = lax.broadcasted_iota(jnp.int32, (tq, 128), 1)
    outs = []
    for pair in range(MLA_HEADS // 2):
        o0 = acc_ref[2 * pair] / l_ref[2 * pair]
        o1 = acc_ref[2 * pair + 1] / l_ref[2 * pair + 1]
        outs.append(jnp.where(lane < MLA_V, o0, o1))
    o_ref[...] = jnp.concatenate(outs, axis=-1).astype(o_ref.dtype)


def _mla_flash(q, k, v, n_lat, prev=None):
    b, s, nslot = q.shape
    n_ctx = s - n_lat
    nv = MLA_HEADS * MLA_V
    tk = 512
    if prev is None:
        tq = _pick_tile(n_lat, 1024)
        q_off, n_q, kv_rows, kv_blk = 0, n_lat // tq, s, 0
    else:
        assert n_lat % n_ctx == 0
        tq = _pick_tile(n_ctx, 256)
        q_off, n_q, kv_rows, kv_blk = n_lat // tq, n_ctx // tq, n_ctx, n_lat // n_ctx
    in_specs = [pl.BlockSpec((None, tq, nslot), lambda bi, i: (bi, i + q_off, 0)),
                pl.BlockSpec((None, kv_rows, nslot), lambda bi, i: (bi, kv_blk, 0)),
                pl.BlockSpec((None, kv_rows, nv), lambda bi, i: (bi, kv_blk, 0))]
    args = [q, k, v]
    aliases = {}
    if prev is not None:
        in_specs.append(pl.BlockSpec(memory_space=pl.ANY))
        args.append(prev)
        aliases = {3: 0}
    return pl.pallas_call(
        functools.partial(_mla_flash_kernel, tk=tk, aliased=prev is not None),
        grid=(b, n_q),
        in_specs=in_specs,
        out_specs=pl.BlockSpec((None, tq, nv), lambda bi, i: (bi, i + q_off, 0)),
        out_shape=jax.ShapeDtypeStruct((b, s, nv), BF16),
        scratch_shapes=[pltpu.VMEM((MLA_HEADS, tq, 128), F32),
                        pltpu.VMEM((MLA_HEADS, tq, 128), F32),
                        pltpu.VMEM((MLA_HEADS, tq, 128), F32)],
        input_output_aliases=aliases,
        compiler_params=_cparams(("parallel", "arbitrary")),
        name="mla_flash",
    )(*args)


NA_QROWS = 4
NA_SLAB = NA_QROWS + NA_WIN_ROWS


def _na_geometry(rows):
    assert rows % NA_QROWS == 0 and rows >= NA_SLAB
    nblk = rows // NA_QROWS
    qc = np.arange(GRID_W)
    cs = np.clip(qc - NA_WIN_COLS // 2, 0, GRID_W - NA_WIN_COLS)
    col_valid = (qc[None, :] >= cs[:, None]) & (qc[None, :] < cs[:, None] + NA_WIN_COLS)
    rel_c = np.clip(qc[None, :] - qc[:, None] + NA_WIN_COLS - 1, 0, 2 * NA_WIN_COLS - 2)
    onehot_c = (rel_c[None] == np.arange(2 * NA_WIN_COLS - 1)[:, None, None]) & col_valid[None]
    patterns, cls, starts = {}, [], []
    for i in range(nblk):
        r0 = i * NA_QROWS
        start = int(np.clip(r0 - NA_WIN_ROWS // 2, 0, rows - NA_SLAB))
        qr = r0 + np.arange(NA_QROWS)
        rs = np.clip(qr - NA_WIN_ROWS // 2, 0, rows - NA_WIN_ROWS)
        key = (start - r0,) + tuple((rs - r0).tolist())
        if key not in patterns:
            kr = start + np.arange(NA_SLAB)
            row_valid = (kr[None, :] >= rs[:, None]) & (kr[None, :] < rs[:, None] + NA_WIN_ROWS)
            rel_r = np.clip(kr[None, :] - qr[:, None] + NA_WIN_ROWS - 1, 0, 2 * NA_WIN_ROWS - 2)
            patterns[key] = (len(patterns), np.where(row_valid, rel_r, -1))
        cls.append(patterns[key][0])
        starts.append(start)
    ordered = sorted(patterns.values(), key=lambda z: z[0])
    rel_rows = np.stack([z[1] for z in ordered])
    meta = np.stack([np.asarray(cls, np.int32), np.asarray(starts, np.int32)])
    return meta, rel_rows, (onehot_c.astype(np.float32), col_valid)


def _na_bias_table(rpb, rel_rows, col_tables):
    onehot_c, col_valid = col_tables
    bc = jnp.einsum('hrc,cqk->hrqk', rpb.astype(F32), onehot_c, precision=lax.Precision.HIGHEST)
    bc = jnp.where(col_valid, bc, -jnp.inf)
    outside = jnp.full(bc.shape[:1] + bc.shape[2:], -jnp.inf, F32)
    pats = []
    for pat in rel_rows:
        qrows = [jnp.concatenate([bc[:, r] if r >= 0 else outside for r in row], axis=-1) for row in pat]
        pats.append(jnp.concatenate(qrows, axis=1))
    return jnp.stack(pats)


def _na_scores(q_pair, lane, h, k_parts):
    in_head = (lane < NA_DH) if h % 2 == 0 else (lane >= NA_DH)
    qm = jnp.where(in_head, q_pair * (NA_DH ** -0.5), 0.0).astype(BF16)
    return [lax.dot_general(qm, k, (((1,), (1,)), ((), ())), preferred_element_type=F32) for k in k_parts]


def _na_softmax(scores, bias):
    if bias is not None:
        scores = [scores[0] + bias] + scores[1:]
    m = scores[0].max(axis=-1, keepdims=True)
    for s in scores[1:]:
        m = jnp.maximum(m, s.max(axis=-1, keepdims=True))
    probs = [jnp.exp(s - m) for s in scores]
    den = probs[0].sum(axis=-1, keepdims=True)
    for p in probs[1:]:
        den = den + p.sum(axis=-1, keepdims=True)
    return [p.astype(BF16) for p in probs], den


def _na_kernel(meta_ref, q_ref, k_ref, v_ref, bias_ref, o_ref, *, n_lat, with_ctx):
    i = pl.program_id(1)
    nq = q_ref.shape[0]
    n_lat_tiles = n_lat // nq
    n_all = k_ref.shape[0]
    lane = lax.broadcasted_iota(jnp.int32, (nq, 128), 1)

    def run(windowed):
        key_rows = [pl.ds(n_lat, n_all - n_lat)]
        if windowed:
            start = pl.multiple_of(meta_ref[1, jnp.minimum(i, n_lat_tiles - 1)] * GRID_W, GRID_W)
            key_rows = [pl.ds(start, NA_SLAB * GRID_W)] + key_rows
        pair_lanes = [slice((h // 2) * 128, (h // 2 + 1) * 128) for h in range(NA_HEADS)]
        scores = [_na_scores(q_ref[:, pair_lanes[h]], lane, h, [k_ref[r, pair_lanes[h]] for r in key_rows])
                  for h in range(NA_HEADS)]
        soft = [_na_softmax(scores[h], bias_ref[h] if windowed else None) for h in range(NA_HEADS)]
        heads = []
        for h in range(NA_HEADS):
            probs, den = soft[h]
            out = None
            for p, r in zip(probs, key_rows):
                po = jnp.dot(p, v_ref[r, pair_lanes[h]], preferred_element_type=F32)
                out = po if out is None else out + po
            heads.append(out / den)
        outs = [jnp.where(lane < NA_DH, heads[2 * pair], heads[2 * pair + 1]) for pair in range(NA_HEADS // 2)]
        o_ref[...] = jnp.concatenate(outs, axis=-1).astype(o_ref.dtype)

    if with_ctx:
        pl.when(i < n_lat_tiles)(lambda: run(True))
        pl.when(i >= n_lat_tiles)(lambda: run(False))
    else:
        run(True)


def _na_attention(p, meta, table, n_lat, with_ctx_queries):
    b, s, _ = p.shape
    nq = NA_QROWS * GRID_W
    nk = NA_SLAB * GRID_W
    assert (s - n_lat) % nq == 0
    nlt = n_lat // nq
    qb = P_NA_QKV // NA_W
    grid_spec = pltpu.PrefetchScalarGridSpec(
        num_scalar_prefetch=1,
        grid=(b, s // nq if with_ctx_queries else nlt),
        in_specs=[pl.BlockSpec((None, nq, NA_W), lambda bi, i, m: (bi, i, qb)),
                  pl.BlockSpec((None, s, NA_W), lambda bi, i, m: (bi, 0, qb + 1)),
                  pl.BlockSpec((None, s, NA_W), lambda bi, i, m: (bi, 0, qb + 2)),
                  pl.BlockSpec((None, NA_HEADS, nq, nk),
                               lambda bi, i, m: (m[0, jnp.minimum(i, nlt - 1)], 0, 0, 0))],
        out_specs=pl.BlockSpec((None, nq, NA_W), lambda bi, i, m: (bi, i, 0)))
    return pl.pallas_call(
        functools.partial(_na_kernel, n_lat=n_lat, with_ctx=with_ctx_queries),
        grid_spec=grid_spec,
        out_shape=jax.ShapeDtypeStruct((b, s, NA_W), BF16),
        compiler_params=_cparams(("parallel", "arbitrary")),
        name="na_attention",
    )(meta, p, p, p, table)


def _tile_conv(x, prev, nxt, w_ref, tile_idx, n_tiles, n_lat_tiles):
    r = x.shape[0]
    width = w_ref.shape[0]
    left = width // 2
    has_prev = jnp.logical_and(tile_idx != 0, tile_idx != n_lat_tiles)
    has_next = jnp.logical_and(tile_idx != n_tiles - 1, tile_idx != n_lat_tiles - 1)
    prev = jnp.where(has_prev, prev, 0.0)
    nxt = jnp.where(has_next, nxt, 0.0)
    xe = jnp.concatenate([prev, x, nxt], axis=0)
    acc = None
    for j in range(width):
        o = HALO - left + j
        term = xe[o:o + r, :] * w_ref[j:j + 1, :]
        acc = term if acc is None else acc + term
    return acc


def _halo_specs(width, col_block, tile_of):
    per = SEQ_TILE // HALO

    def main(bi, s, *_):
        return (bi, tile_of(s), col_block)

    def prev(bi, s, *_):
        return (bi, jnp.maximum(tile_of(s) * per - 1, 0), col_block)

    def make_next(n_tiles):
        def nxt(bi, s, *_):
            return (bi, jnp.minimum((tile_of(s) + 1) * per, n_tiles * per - 1), col_block)
        return nxt

    return main, prev, make_next


def _gdn_prep_kernel(x_ref, xp_ref, xn_ref, ba_ref, cw_ref, ones_ref, exp_ref, alog_ref, dtb_ref,
                     q_ref, k_ref, v_ref, beta_ref, g_ref, *, n_lat_tiles):
    i = pl.program_id(1)
    y = _tile_conv(x_ref[...], xp_ref[...], xn_ref[...], cw_ref, i, pl.num_programs(1), n_lat_tiles)
    y = y * _sigmoid(y)
    q = y[:, :GDN_W]
    k = y[:, GDN_W:2 * GDN_W]
    v_ref[...] = y[:, 2 * GDN_W:].astype(v_ref.dtype)

    def head_norm(u):
        parts = jnp.concatenate(_split3(u * u), axis=-1)
        ss = jnp.dot(parts, ones_ref[...], preferred_element_type=F32)
        return u * lax.rsqrt(ss + RMS_EPS)

    q_ref[...] = (head_norm(q) * (GDN_DK ** -0.5)).astype(q_ref.dtype)
    k_ref[...] = head_norm(k).astype(k_ref.dtype)

    ba = ba_ref[...]
    a = ba + dtb_ref[...]
    softplus = jnp.maximum(a, 0.0) + jnp.log1p(jnp.exp(-jnp.abs(a)))
    lane = lax.broadcasted_iota(jnp.int32, ba.shape, 1)
    compact = jnp.where(lane < 2 * GDN_HEADS, _sigmoid(ba), -jnp.exp(alog_ref[...]) * softplus)
    wide = jnp.dot(jnp.concatenate(_split3(compact), axis=-1), exp_ref[...], preferred_element_type=F32)
    half = 2 * GDN_W
    beta_ref[...] = wide[:, :half].astype(beta_ref.dtype)
    g_ref[...] = wide[:, half:]


def _gdn_prep(p, conv_w, a_log, dt_bias, n_lat):
    b, s, _ = p.shape
    n_tiles = s // SEQ_TILE
    main, prev, make_next = _halo_specs(3 * GDN_W, 0, lambda t: t)
    ones3 = jnp.concatenate([_head_block_ones(GDN_HEADS, GDN_DK)] * 3, axis=0)
    expand = np.zeros((128, 4 * GDN_W), np.float32)
    for kind in range(2):
        for d in range(2):
            for h in range(GDN_HEADS):
                c0 = kind * 2 * GDN_W + d * GDN_W + h * GDN_DV
                expand[kind * 2 * GDN_HEADS + d * GDN_HEADS + h, c0:c0 + GDN_DV] = 1.0
    expand3 = jnp.asarray(np.concatenate([expand] * 3, axis=0), BF16)
    lanes = jnp.zeros((1, 128), F32)
    alog_e = lanes.at[0, 2 * GDN_HEADS:4 * GDN_HEADS].set(a_log.astype(F32).reshape(-1))
    dtb_e = lanes.at[0, 2 * GDN_HEADS:4 * GDN_HEADS].set(dt_bias.astype(F32).reshape(-1))
    tok = lambda w: pl.BlockSpec((None, SEQ_TILE, w), lambda bi, i: (bi, i, 0))
    return pl.pallas_call(
        functools.partial(_gdn_prep_kernel, n_lat_tiles=n_lat // SEQ_TILE),
        grid=(b, n_tiles),
        in_specs=[pl.BlockSpec((None, SEQ_TILE, 3 * GDN_W), main),
                  pl.BlockSpec((None, HALO, 3 * GDN_W), prev),
                  pl.BlockSpec((None, HALO, 3 * GDN_W), make_next(n_tiles)),
                  pl.BlockSpec((None, SEQ_TILE, 128), lambda bi, i: (bi, i, P_GDN_BA // 128)),
                  _full(conv_w), _full(ones3), _full(expand3), _full(alog_e), _full(dtb_e)],
        out_specs=[tok(GDN_W), tok(GDN_W), tok(GDN_W), tok(2 * GDN_W), tok(2 * GDN_W)],
        out_shape=[jax.ShapeDtypeStruct((b, s, GDN_W), BF16)] * 3 + [jax.ShapeDtypeStruct((b, s, 2 * GDN_W), BF16),
                                                                    jax.ShapeDtypeStruct((b, s, 2 * GDN_W), F32)],
        compiler_params=_cparams(("parallel", "parallel")),
        name="gdn_prep",
    )(p, p, p, p, conv_w, ones3, expand3, alog_e, dtb_e)


GDN_PAIR = 2 * GDN_DK


def _gdn_masks():
    c, w = GDN_CHUNK, GDN_W
    r2, c2 = np.arange(GDN_PAIR)[:, None], np.arange(GDN_PAIR)[None, :]
    bd = ((r2 // c) == (c2 // c)).astype(np.float32)
    i = np.arange(c)[:, None]
    j = (np.arange(w) % c)[None, :]
    level = np.zeros((c, w), np.int32)
    for bit in range(6):
        level += ((i ^ j) >= (1 << bit)).astype(np.int32)
    lvl = np.stack([(level == m).astype(np.float32) for m in range(7)])
    dirm = np.stack([np.stack([(j <= i), (j < i)]), np.stack([(j >= i), (j > i)])]).astype(np.float32)
    tj = (np.arange(3 * c) % c)[None, :]
    tri = np.stack([(tj <= i), (tj >= i)]).astype(np.float32)
    return jnp.asarray(bd, BF16), jnp.asarray(lvl), jnp.asarray(dirm), jnp.asarray(tri, BF16)


def _heads_mm(x, y, bd, transpose_rhs=False):
    xb = x.astype(BF16)
    yb = y.astype(BF16)
    outs = []
    for pair in range(GDN_W // GDN_PAIR):
        ls = slice(pair * GDN_PAIR, (pair + 1) * GDN_PAIR)
        w = jnp.concatenate([yb[:, ls], yb[:, ls]], axis=0) * bd
        dims = (((1,), (1,)), ((), ())) if transpose_rhs else (((1,), (0,)), ((), ()))
        outs.append(lax.dot_general(xb[:, ls], w, dims, preferred_element_type=F32))
    return jnp.concatenate(outs, axis=1)


def _gdn_intra(probs, bd, lvl_ref, dirm_ref, tri_ref):
    c = GDN_CHUNK
    n = len(probs)
    eye = lvl_ref[0]
    gc, g_last, decay, gram = [], [], [], []
    for q, k, v, beta, g, rev in probs:
        d = 1 if rev else 0
        gcp = jnp.dot(tri_ref[d], jnp.concatenate(_split3(g), axis=0), preferred_element_type=F32)
        gc.append(gcp)
        g_last.append(gcp[0:1, :] if rev else gcp[c - 1:c, :])
        gc_row = jnp.sum(gcp * eye, axis=0, keepdims=True)
        decay.append(dirm_ref[d, 0] * jnp.exp(jnp.minimum(gcp - gc_row, 0.0)))
        gram.append(_heads_mm(jnp.concatenate([k, q], axis=0), k, bd, transpose_rhs=True))
    lower = [dirm_ref[1 if p[5] else 0, 1] * p[3] * gram[x][:c] * decay[x] for x, p in enumerate(probs)]
    a_intra = [gram[x][c:] * decay[x] for x in range(n)]
    t = [eye - lower[x] * lvl_ref[1] for x in range(n)]
    for lev in range(2, 7):
        y = [_heads_mm(t[x], lower[x] * lvl_ref[lev], bd) for x in range(n)]
        z = [_heads_mm(y[x], t[x], bd) for x in range(n)]
        t = [t[x] - z[x] for x in range(n)]
    e_gc = [jnp.exp(gc[x]) for x in range(n)]
    u = [_heads_mm(t[x], p[2] * p[3], bd) for x, p in enumerate(probs)]
    w = [_heads_mm(t[x], p[1] * p[3] * e_gc[x], bd) for x, p in enumerate(probs)]
    wq = [jnp.concatenate([w[x], p[0] * e_gc[x]], axis=0).astype(BF16) for x, p in enumerate(probs)]
    k_dec = [(p[1] * jnp.exp(g_last[x] - gc[x])).astype(BF16) for x, p in enumerate(probs)]
    g_tot = [jnp.exp(g_last[x]) for x in range(n)]
    return u, wq, k_dec, a_intra, g_tot


def _gdn_state_steps(items, bd):
    c = GDN_CHUNK
    pairs = [slice(p * GDN_PAIR, (p + 1) * GDN_PAIR) for p in range(GDN_W // GDN_PAIR)]
    bdf = bd.astype(F32)
    ws_qs = [jnp.concatenate([jnp.dot(wq[:, ls], s_ref[p].astype(BF16), preferred_element_type=F32)
                              for p, ls in enumerate(pairs)], axis=1)
             for _, wq, _, _, _, s_ref in items]
    v_new = [it[0] - ws[:c] for it, ws in zip(items, ws_qs)]
    outs = [ws[c:] + _heads_mm(it[3], vn, bd) for it, ws, vn in zip(items, ws_qs, v_new)]
    for (_, _, k_dec, _, g_tot, s_ref), vn in zip(items, v_new):
        vb = vn.astype(BF16)
        for p, ls in enumerate(pairs):
            kv = lax.dot_general(k_dec[:, ls], vb[:, ls], (((0,), (0,)), ((), ())), preferred_element_type=F32)
            s_ref[p] = s_ref[p] * g_tot[:, ls] + kv * bdf
    return outs


def _gdn_scan_kernel(qf, kf, vf, bf, gf, qb, kb, vb, bb, gb, bd_ref, lvl_ref, dirm_ref, tri_ref,
                     of_ref, ob_ref, sf_ref, sb_ref):
    @pl.when(pl.program_id(1) == 0)
    def _():
        sf_ref[...] = jnp.zeros(sf_ref.shape, F32)
        sb_ref[...] = jnp.zeros(sb_ref.shape, F32)

    bd = bd_ref[...]
    n = SEQ_TILE // GDN_CHUNK
    probs, rows = [], []
    for c in range(n):
        rf = slice(c * GDN_CHUNK, (c + 1) * GDN_CHUNK)
        rb = slice((n - 1 - c) * GDN_CHUNK, (n - c) * GDN_CHUNK)
        probs.append(tuple(r[rf, :].astype(F32) for r in (qf, kf, vf, bf, gf)) + (False,))
        probs.append(tuple(r[rb, :].astype(F32) for r in (qb, kb, vb, bb, gb)) + (True,))
        rows += [rf, rb]
    u, wq, k_dec, a_intra, g_tot = _gdn_intra(probs, bd, lvl_ref, dirm_ref, tri_ref)
    for c in range(n):
        xs = (2 * c, 2 * c + 1)
        outs = _gdn_state_steps([(u[x], wq[x], k_dec[x], a_intra[x], g_tot[x], sb_ref if probs[x][5] else sf_ref)
                                 for x in xs], bd)
        for x, o in zip(xs, outs):
            (ob_ref if probs[x][5] else of_ref)[rows[x], :] = o


def _seq_tile_maps(n_lat_tiles, n_tiles):
    fwd = lambda s: lax.rem(s + n_lat_tiles, n_tiles)
    bwd = lambda s: n_tiles - 1 - s
    return fwd, bwd


def _gdn_scan(q, k, v, beta, g, n_lat):
    b, s, _ = q.shape
    n_tiles = s // SEQ_TILE
    fwd, bwd = _seq_tile_maps(n_lat // SEQ_TILE, n_tiles)
    spec = lambda tile_of, col: pl.BlockSpec((None, SEQ_TILE, GDN_W), lambda bi, t: (bi, tile_of(t), col))
    masks = _gdn_masks()
    in_specs = [spec(fwd, 0)] * 5 + [spec(bwd, 0)] * 3 + [spec(bwd, 1)] * 2 + [_full(m) for m in masks]
    return pl.pallas_call(
        _gdn_scan_kernel,
        grid=(b, n_tiles),
        in_specs=in_specs,
        out_specs=[spec(fwd, 0), spec(bwd, 0)],
        out_shape=[jax.ShapeDtypeStruct((b, s, GDN_W), F32)] * 2,
        scratch_shapes=[pltpu.VMEM((GDN_W // GDN_PAIR, GDN_PAIR, GDN_PAIR), F32)] * 2,
        compiler_params=_cparams(("parallel", "arbitrary")),
        name="gdn_scan",
    )(q, k, v, beta, g, q, k, v, beta, g, *masks)


def _lru_tile_prep(x_ref, xp_ref, xn_ref, cw_ref, cb_ref, wg_ref, bg_ref, nla_ref, a_ref, b_ref,
                   tile_idx, n_tiles, n_lat_tiles):
    xb = _tile_conv(x_ref[...], xp_ref[...], xn_ref[...], cw_ref, tile_idx, n_tiles, n_lat_tiles) + cb_ref[...]
    gates = _sigmoid(jnp.dot(xb.astype(BF16), wg_ref[...], preferred_element_type=F32) + bg_ref[...])
    log_a = nla_ref[...] * gates[:, :LRU_W]
    a_ref[...] = jnp.exp(log_a)
    th = jnp.tanh(log_a)
    b_ref[...] = jnp.sqrt(-2.0 * th / (1.0 - th)) * gates[:, LRU_W:] * xb


def _scan_group(a, b, h, row, reverse):
    for d in (1, 2, 4):
        if reverse:
            keep = row < 8 - d
            shift = 8 - d
        else:
            keep = row >= d
            shift = d
        a_s = jnp.where(keep, pltpu.roll(a, shift, 0), 1.0)
        b_s = jnp.where(keep, pltpu.roll(b, shift, 0), 0.0)
        b = a * b_s + b
        a = a * a_s
    return a * h + b


def _lru_scan_kernel(xf, xfp, xfn, xb, xbp, xbn, cw_ref, cb_ref, wgf, bgf, nlaf, wgb, bgb, nlab,
                     hf_ref, hb_ref, af_ref, bf_ref, ab_ref, bb_ref, cf_ref, cbk_ref, *, n_lat_tiles):
    s = pl.program_id(1)
    n_tiles = pl.num_programs(1)

    @pl.when(s == 0)
    def _():
        cf_ref[...] = jnp.zeros(cf_ref.shape, F32)
        cbk_ref[...] = jnp.zeros(cbk_ref.shape, F32)

    t_f, t_b = (m(s) for m in _seq_tile_maps(n_lat_tiles, n_tiles))
    _lru_tile_prep(xf, xfp, xfn, cw_ref, cb_ref, wgf, bgf, nlaf, af_ref, bf_ref, t_f, n_tiles, n_lat_tiles)
    _lru_tile_prep(xb, xbp, xbn, cw_ref, cb_ref, wgb, bgb, nlab, ab_ref, bb_ref, t_b, n_tiles, n_lat_tiles)

    n_groups = SEQ_TILE // 8
    row = lax.broadcasted_iota(jnp.int32, (8, LRU_W), 0)

    def body(gi, carry):
        h_f, h_b = carry
        rf = pl.ds(pl.multiple_of(gi * 8, 8), 8)
        rb = pl.ds(pl.multiple_of((n_groups - 1 - gi) * 8, 8), 8)
        out_f = _scan_group(af_ref[rf, :], bf_ref[rf, :], h_f, row, False)
        out_b = _scan_group(ab_ref[rb, :], bb_ref[rb, :], h_b, row, True)
        hf_ref[rf, :] = out_f
        hb_ref[rb, :] = out_b
        return (jnp.broadcast_to(out_f[7:8, :], (8, LRU_W)), jnp.broadcast_to(out_b[0:1, :], (8, LRU_W)))

    h_f, h_b = lax.fori_loop(0, n_groups, body, (cf_ref[...], cbk_ref[...]), unroll=4)
    cf_ref[...] = h_f
    cbk_ref[...] = h_b


def _lru_scan(p, conv_w, conv_b, w_r, b_r, w_i, b_i, lam, n_lat):
    b, s, _ = p.shape
    n_tiles = s // SEQ_TILE
    nlt = n_lat // SEQ_TILE
    fwd, bwd = _seq_tile_maps(nlt, n_tiles)
    col = P_LRU_X // LRU_W
    specs = []
    for tile_of in (fwd, bwd):
        main, prev, make_next = _halo_specs(LRU_W, col, tile_of)
        specs += [pl.BlockSpec((None, SEQ_TILE, LRU_W), main), pl.BlockSpec((None, HALO, LRU_W), prev),
                  pl.BlockSpec((None, HALO, LRU_W), make_next(n_tiles))]

    def blockdiag(w):
        return jax.scipy.linalg.block_diag(*[w[n] for n in range(LRU_BLOCKS)])

    dir_args = []
    for d in range(2):
        wg = jnp.concatenate([blockdiag(w_r[d]), blockdiag(w_i[d])], axis=1).astype(BF16)
        bg = jnp.concatenate([b_r[d], b_i[d]]).astype(F32).reshape(1, 2 * LRU_W)
        nla = (-LRU_C * jax.nn.softplus(-lam[d].astype(F32))).reshape(1, LRU_W)
        dir_args += [wg, bg, nla]
    cb2 = conv_b.reshape(1, LRU_W)
    out_spec = lambda tile_of: pl.BlockSpec((None, SEQ_TILE, LRU_W), lambda bi, t: (bi, tile_of(t), 0))
    return pl.pallas_call(
        functools.partial(_lru_scan_kernel, n_lat_tiles=nlt),
        grid=(b, n_tiles),
        in_specs=specs + [_full(conv_w), _full(cb2)] + [_full(a) for a in dir_args],
        out_specs=[out_spec(fwd), out_spec(bwd)],
        out_shape=[jax.ShapeDtypeStruct((b, s, LRU_W), F32)] * 2,
        scratch_shapes=[pltpu.VMEM((SEQ_TILE, LRU_W), F32)] * 4 + [pltpu.VMEM((8, LRU_W), F32)] * 2,
        compiler_params=_cparams(("parallel", "arbitrary")),
        name="lru_scan",
    )(p, p, p, p, p, p, conv_w, cb2, *dir_args)


def _post_kernel(of_ref, ob_ref, z_ref, gn_ref, ones_ref, hf_ref, hb_ref, y_ref, ua_ref, ub_ref):
    o = of_ref[...] + ob_ref[...]
    ms = jnp.dot(jnp.concatenate(_split3(o * o), axis=-1), ones_ref[...], preferred_element_type=F32) * (1.0 / GDN_DV)
    z = z_ref[...]
    ua_ref[...] = (o * lax.rsqrt(ms + RMS_EPS) * gn_ref[...] * (z * _sigmoid(z))).astype(ua_ref.dtype)
    y = y_ref[...]
    gelu = 0.5 * y * (1.0 + jnp.tanh(0.7978845608028654 * (y + 0.044715 * (y * y * y))))
    ub_ref[...] = ((hf_ref[...] + hb_ref[...]) * gelu).astype(ub_ref.dtype)


def _post(o_f, o_b, h_f, h_b, p, gdn_norm_g):
    b, s, _ = p.shape
    tm = _pick_tile(s, 544)
    ones3 = jnp.concatenate([_head_block_ones(GDN_HEADS, GDN_DV)] * 3, axis=0)
    gn = jnp.tile(gdn_norm_g.astype(F32), GDN_HEADS).reshape(1, GDN_W)
    tok = pl.BlockSpec((None, tm, GDN_W), lambda bi, i: (bi, i, 0))
    pcol = lambda c: pl.BlockSpec((None, tm, GDN_W), lambda bi, i: (bi, i, c))
    return pl.pallas_call(
        _post_kernel,
        grid=(b, s // tm),
        in_specs=[tok, tok, pcol(P_GDN_Z // GDN_W), _full(gn), _full(ones3), tok, tok, pcol(P_LRU_Y // LRU_W)],
        out_specs=[tok, tok],
        out_shape=[jax.ShapeDtypeStruct((b, s, GDN_W), BF16)] * 2,
        compiler_params=_cparams(("parallel", "parallel")),
        name="gdn_lru_post",
    )(o_f, o_b, p, gn, ones3, h_f, h_b, p)


def _merge_kernel(x_ref, mc_ref, ml_ref, u0_ref, u1_ref, u2_ref, u3_ref, t0_ref, t1_ref, t2_ref, t3_ref,
                  bg_ref, wb_ref, wo_ref, o_ref, *, n_lat):
    merged = None
    for n, (u_ref, t_ref) in enumerate(((u0_ref, t0_ref), (u1_ref, t1_ref), (u2_ref, t2_ref), (u3_ref, t3_ref))):
        gate = _sigmoid(t_ref[...] + bg_ref[n:n + 1, :].astype(BF16))
        term = gate.astype(F32) * jnp.dot(u_ref[...], wb_ref[n], preferred_element_type=F32)
        merged = term if merged is None else merged + term
    out = jnp.dot(merged.astype(BF16), wo_ref[...], preferred_element_type=F32)
    g1 = _row_select(x_ref.shape[0], pl.program_id(1), n_lat, mc_ref[2:3, :], ml_ref[2:3, :])
    o_ref[...] = x_ref[...] + g1 * out


def _merge(xs, mod_c, mod_l, branches, p, b_gate, w_branch, w_out, layer, n_lat, with_ctx):
    b, s, d = xs.shape
    rows = s if with_ctx else n_lat
    tm = _pick_tile(rows, 544) if with_ctx else SEQ_TILE
    gate_blk = P_GATES // d
    tok = lambda w: pl.BlockSpec((None, tm, w), lambda bi, i: (bi, i, 0))
    in_specs = [tok(d), pl.BlockSpec((8, d), lambda bi, i: (0, 0)), pl.BlockSpec((None, 8, d), lambda bi, i: (bi, 0, 0))]
    in_specs += [tok(BRANCH_W)] * N_BRANCH
    in_specs += [pl.BlockSpec((None, tm, d), functools.partial(lambda bi, i, n: (bi, i, gate_blk + n), n=n))
                 for n in range(N_BRANCH)]
    in_specs += [_full(b_gate),
                 pl.BlockSpec((None,) + w_branch.shape[1:], lambda bi, i: (layer, 0, 0, 0)),
                 pl.BlockSpec((None,) + w_out.shape[1:], lambda bi, i: (layer, 0, 0))]
    return pl.pallas_call(
        functools.partial(_merge_kernel, n_lat=n_lat),
        grid=(b, rows // tm),
        in_specs=in_specs,
        out_specs=tok(d),
        out_shape=jax.ShapeDtypeStruct((b, rows, d), F32),
        input_output_aliases={0: 0} if with_ctx else {},
        compiler_params=_cparams(("parallel", "parallel")),
        name="merge",
    )(xs, mod_c, mod_l, *branches, p, p, p, p, b_gate, w_branch, w_out)


def _mlp_kernel(x_ref, mc_ref, ml_ref, gn_ref, gf_ref, w1_ref, w2_ref, o_ref, h_ref, acc_ref, *, n_lat, final_norm):
    f = pl.program_id(2)
    tm = x_ref.shape[0]
    i = pl.program_id(1)

    @pl.when(f == 0)
    def _():
        x = x_ref[...]
        y = x * lax.rsqrt(jnp.mean(x * x, axis=-1, keepdims=True) + RMS_EPS)
        y = y * gn_ref[...]
        shift = _row_select(tm, i, n_lat, mc_ref[3:4, :], ml_ref[3:4, :])
        scale = _row_select(tm, i, n_lat, mc_ref[4:5, :], ml_ref[4:5, :])
        h_ref[...] = (y * (1.0 + scale) + shift).astype(BF16)

    a = jnp.maximum(jnp.dot(h_ref[...], w1_ref[...], preferred_element_type=F32), 0.0)
    part = jnp.dot((a * a).astype(BF16), w2_ref[...], preferred_element_type=F32)

    @pl.when(f == 0)
    def _():
        acc_ref[...] = part

    @pl.when(f > 0)
    def _():
        acc_ref[...] += part

    @pl.when(f == pl.num_programs(2) - 1)
    def _():
        g2 = _row_select(tm, i, n_lat, mc_ref[5:6, :], ml_ref[5:6, :])
        y = x_ref[...] + g2 * acc_ref[...]
        if final_norm:
            y = y * lax.rsqrt(jnp.mean(y * y, axis=-1, keepdims=True) + RMS_EPS) * gf_ref[...]
        o_ref[...] = y


def _mlp(xs, mod_c, mod_l, gain, w1, w2, layer, final_gain, n_lat, final_norm):
    b, rows, d = xs.shape
    dff = w1.shape[-1]
    tm = _pick_tile(rows, 1088)
    tf = 1024
    row = pl.BlockSpec((1, d), lambda bi, i, f: (0, 0))
    return pl.pallas_call(
        functools.partial(_mlp_kernel, n_lat=n_lat, final_norm=final_norm),
        grid=(b, rows // tm, dff // tf),
        in_specs=[pl.BlockSpec((None, tm, d), lambda bi, i, f: (bi, i, 0)),
                  pl.BlockSpec((8, d), lambda bi, i, f: (0, 0)),
                  pl.BlockSpec((None, 8, d), lambda bi, i, f: (bi, 0, 0)),
                  row, row,
                  pl.BlockSpec((None, d, tf), lambda bi, i, f: (layer, 0, f)),
                  pl.BlockSpec((None, tf, d), lambda bi, i, f: (layer, f, 0))],
        out_specs=pl.BlockSpec((None, tm, d), lambda bi, i, f: (bi, i, 0)),
        out_shape=jax.ShapeDtypeStruct((b, rows, d), F32),
        scratch_shapes=[pltpu.VMEM((tm, d), BF16), pltpu.VMEM((tm, d), F32)],
        compiler_params=_cparams(("parallel", "parallel", "arbitrary")),
        name="mlp",
    )(xs, mod_c, mod_l, gain, final_gain, w1, w2)


def kernel(x, c, ctx, c_ctx, mod_w, mod_b, norm1_g, norm2_g, w_in, b_gate, gdn_conv_w, gdn_a_log, gdn_dt_bias,
           gdn_norm_g, lru_conv_w, lru_conv_b, lru_w_r, lru_b_r, lru_w_i, lru_b_i, lru_lambda, mla_q_norm_g,
           mla_w_uq, mla_kv_norm_g, mla_w_ukv, na_rpb, w_branch, w_out, mlp_w1, mlp_w2, final_norm_g):
    bsz, n_tok, d = x.shape
    n_ctx = ctx.shape[1]
    depth = w_in.shape[0]
    assert n_ctx % SEQ_TILE == 0 and n_tok % SEQ_TILE == 0 and n_tok % GRID_W == 0
    na_meta, na_rel_rows, na_cols = _na_geometry(n_tok // GRID_W)
    na_meta = jnp.asarray(na_meta)
    cos, sin = _rope_tables(n_tok, n_ctx)

    n_rows = -(-(bsz + 1) // 8) * 8
    cc = jnp.zeros((n_rows, d), F32).at[:bsz].set(c).at[bsz].set(c_ctx)
    final_gain = final_norm_g.reshape(1, d)

    w_in_all = _arrange_w_in(w_in)
    wb_all = w_branch.astype(BF16)
    wo_all = w_out.astype(BF16)
    w1_all = mlp_w1.astype(BF16)
    w2_all = mlp_w2.astype(BF16)

    xs = jnp.concatenate([x, ctx], axis=1)
    for l in range(depth):
        need_ctx = l < depth - 1
        mod = _modulation(cc, mod_w[l], mod_b[l]).reshape(n_rows, N_MOD, d)
        pad = jnp.zeros((8 - N_MOD, d), F32)
        mod_c = jnp.concatenate([mod[bsz], pad], axis=0)
        mod_l = jnp.concatenate([mod[:bsz], jnp.broadcast_to(pad, (bsz, 8 - N_MOD, d))], axis=1)

        wq, wk, wv, place = _arrange_mla(mla_w_uq[l], mla_w_ukv[l])
        gq = mla_q_norm_g[l].reshape(1, -1)
        gkv = mla_kv_norm_g[l].reshape(1, -1)
        g1n = norm1_g[l].reshape(1, d)
        g2n = norm2_g[l].reshape(1, d)

        p32, p16 = _inproj(xs, mod_c, mod_l, g1n, w_in_all, l, n_tok)

        gq_, gk_, gv_, gbeta, gg = _gdn_prep(p32, gdn_conv_w[l], gdn_a_log[l], gdn_dt_bias[l], n_tok)
        o_f, o_b = _gdn_scan(gq_, gk_, gv_, gbeta, gg, n_tok)
        h_f, h_b = _lru_scan(p32, lru_conv_w[l], lru_conv_b[l], lru_w_r[l], lru_b_r[l], lru_w_i[l], lru_b_i[l],
                             lru_lambda[l], n_tok)
        ua, ub = _post(o_f, o_b, h_f, h_b, p32, gdn_norm_g[l])

        mq, mk, mv = _mla_prep(p32, p16, cos, sin, gq, gkv, wq, wk, wv, place)
        uc = _mla_flash(mq, mk, mv, n_tok)
        if need_ctx:
            uc = _mla_flash(mq, mk, mv, n_tok, prev=uc)
        ud = _na_attention(p16, na_meta, _na_bias_table(na_rpb[l], na_rel_rows, na_cols), n_tok, need_ctx)

        xs = _merge(xs, mod_c, mod_l, (ua, ub, uc, ud), p16, b_gate[l], wb_all, wo_all, l, n_tok, need_ctx)
        xs = _mlp(xs, mod_c, mod_l, g2n, w1_all, w2_all, l, final_gain, n_tok, l == depth - 1)
    return xs
```

```python
import functools

import jax
import jax.numpy as jnp
import numpy as np
from jax import lax
from jax.experimental import pallas as pl
from jax.experimental.pallas import tpu as pltpu

F32 = jnp.float32
BF16 = jnp.bfloat16

GRID_W = 64
N_MOD = 6
RMS_EPS = 1e-6
GDN_HEADS = 4
GDN_DK = 64
GDN_DV = 64
GDN_CHUNK = 64
GDN_W = GDN_HEADS * GDN_DV
GDN_CONV = 4
LRU_W = 256
LRU_BLOCKS = 4
LRU_BLOCK_W = LRU_W // LRU_BLOCKS
LRU_CONV = 4
LRU_C = 8.0
MLA_HEADS = 4
MLA_Q_RANK = 256
MLA_KV_RANK = 128
MLA_NOPE = 64
MLA_ROPE = 32
MLA_V = 64
MLA_SLOT = 128
ROPE_BASE = 10000.0
LOG2_E = 1.4426950408889634
NA_HEADS = 4
NA_DH = 64
NA_W = NA_HEADS * NA_DH
NA_WIN_ROWS = 8
NA_WIN_COLS = 16
N_BRANCH = 4
BRANCH_W = 256

SEQ_TILE = 256
HALO = 8

_REF_COLS = {}
_off = 0
for _name, _w in (('gdn_qkv', 3 * GDN_W), ('gdn_z', GDN_W), ('gdn_beta', 2 * GDN_HEADS), ('gdn_a', 2 * GDN_HEADS),
                  ('lru_x', LRU_W), ('lru_y', LRU_W), ('mla_q', MLA_Q_RANK), ('mla_kv', MLA_KV_RANK),
                  ('mla_kr', MLA_ROPE), ('na_qkv', 3 * NA_W)):
    _REF_COLS[_name] = (_off, _w)
    _off += _w
N_MIX_COLS = _off

P_GDN_QKV = 0
P_GDN_Z = 768
P_LRU_X = 1024
P_LRU_Y = 1280
P_MLA_KV = 1536
P_MLA_KR = 1664
P_GDN_BA = 1792
P32_COLS = 2048
P_MLA_Q = 0
P_NA_QKV = 256
P_GATES = 1024
P16_COLS = P_GATES + N_BRANCH * 1024
PROJ_TILE = 1024

VMEM_LIMIT = 52 * 1024 * 1024


def _cparams(sem):
    return pltpu.CompilerParams(dimension_semantics=sem, vmem_limit_bytes=VMEM_LIMIT)


def _pick_tile(n, cap):
    best = 8
    for t in range(8, min(n, cap) + 1, 8):
        if n % t == 0:
            best = t
    return best


def _full(a):
    return pl.BlockSpec(a.shape, lambda *_: (0,) * a.ndim)


def _split3(x):
    hi = x.astype(BF16)
    r = x - hi.astype(F32)
    mid = r.astype(BF16)
    lo = (r - mid.astype(F32)).astype(BF16)
    return hi, mid, lo


def _sigmoid(x):
    return 0.5 * jnp.tanh(0.5 * x) + 0.5


def _arrange_w_in(w_in):
    pieces, pos = [], 0

    def put(dst, block):
        nonlocal pos
        if dst > pos:
            pieces.append(jnp.zeros(w_in.shape[:-1] + (dst - pos,), w_in.dtype))
        pieces.append(block)
        pos = dst + block.shape[-1]

    ref = lambda name: w_in[..., _REF_COLS[name][0]:_REF_COLS[name][0] + _REF_COLS[name][1]]
    for name, dst in (('gdn_qkv', P_GDN_QKV), ('gdn_z', P_GDN_Z), ('lru_x', P_LRU_X), ('lru_y', P_LRU_Y),
                      ('mla_kv', P_MLA_KV), ('mla_kr', P_MLA_KR), ('gdn_beta', P_GDN_BA),
                      ('gdn_a', P_GDN_BA + 2 * GDN_HEADS)):
        put(dst, ref(name))
    put(P32_COLS + P_MLA_Q, ref('mla_q'))
    put(P32_COLS + P_NA_QKV, ref('na_qkv'))
    put(P32_COLS + P_GATES, w_in[..., N_MIX_COLS:])
    assert pos == P32_COLS + P16_COLS
    return jnp.concatenate(pieces, axis=-1).astype(BF16)


def _rope_perm():
    q = MLA_ROPE // 4
    src = np.zeros(MLA_ROPE, np.int32)
    sign = np.zeros(MLA_ROPE, np.float32)
    for base in (0, 2 * q):
        for d in range(q):
            src[base + d] = base + d + q
            sign[base + d] = -1.0
            src[base + q + d] = base + d
            sign[base + q + d] = 1.0
    return src, sign


def _arrange_mla(w_uq, w_ukv):
    src, sign = _rope_perm()
    hq = MLA_NOPE + MLA_ROPE
    wq = jnp.zeros((MLA_Q_RANK, 2 * MLA_HEADS * MLA_SLOT), F32)
    wk = jnp.zeros((MLA_KV_RANK, MLA_HEADS * MLA_SLOT), F32)
    wv = jnp.zeros((MLA_KV_RANK, MLA_HEADS * MLA_V), F32)
    place = np.zeros((2 * MLA_SLOT, 2 * MLA_HEADS * MLA_SLOT), np.float32)
    rot_off = MLA_HEADS * MLA_SLOT
    for h in range(MLA_HEADS):
        nope = w_uq[:, h * hq:h * hq + MLA_NOPE]
        pe = w_uq[:, h * hq + MLA_NOPE:(h + 1) * hq]
        s = h * MLA_SLOT
        wq = wq.at[:, s:s + MLA_NOPE].set(nope)
        wq = wq.at[:, s + MLA_NOPE:s + MLA_NOPE + MLA_ROPE].set(pe)
        wq = wq.at[:, rot_off + s + MLA_NOPE:rot_off + s + MLA_NOPE + MLA_ROPE].set(pe[:, src] * sign)
        wk = wk.at[:, s:s + MLA_NOPE].set(w_ukv[:, h * 128:h * 128 + MLA_NOPE])
        wv = wv.at[:, h * MLA_V:(h + 1) * MLA_V].set(w_ukv[:, h * 128 + MLA_NOPE:(h + 1) * 128])
        for d in range(MLA_ROPE):
            for half in (0, MLA_SLOT):
                place[half + d, s + MLA_NOPE + d] = 1.0
                place[half + src[d], rot_off + s + MLA_NOPE + d] = sign[d]
    return wq.astype(BF16), wk.astype(BF16), wv.astype(BF16), jnp.asarray(place, BF16)


def _rope_tables(n_tok, n_ctx):
    cos = np.ones((n_tok + n_ctx, MLA_SLOT), np.float32)
    sin = np.zeros((n_tok + n_ctx, MLA_SLOT), np.float32)
    t = np.arange(n_tok)
    row = (t // GRID_W).astype(np.float32)
    col = (t % GRID_W).astype(np.float32)
    n_freq = MLA_ROPE // 4
    inv = (ROPE_BASE ** (-np.arange(n_freq, dtype=np.float32) / n_freq)).astype(np.float32)
    ar = row[:, None] * inv
    ac = col[:, None] * inv
    ang = np.concatenate([ar, ar, ac, ac], axis=-1).astype(np.float32)
    cos[:n_tok, MLA_NOPE:MLA_NOPE + MLA_ROPE] = np.cos(ang)
    sin[:n_tok, MLA_NOPE:MLA_NOPE + MLA_ROPE] = np.sin(ang)
    return jnp.asarray(cos), jnp.asarray(sin)


def _head_block_ones(n_heads, width):
    m = np.kron(np.eye(n_heads, dtype=np.float32), np.ones((width, width), np.float32))
    return jnp.asarray(m, BF16)


def _mod_kernel(c_ref, w_ref, b_ref, o_ref):
    c = c_ref[...]
    s = c * _sigmoid(c)
    o_ref[...] = jnp.dot(s, w_ref[...], preferred_element_type=F32) + b_ref[...]


def _modulation(cc, mod_w, mod_b):
    r, d = cc.shape
    n = mod_w.shape[1]
    tn = 1024
    return pl.pallas_call(
        _mod_kernel,
        grid=(n // tn,),
        in_specs=[pl.BlockSpec((r, d), lambda j: (0, 0)),
                  pl.BlockSpec((d, tn), lambda j: (0, j)),
                  pl.BlockSpec((1, tn), lambda j: (0, j))],
        out_specs=pl.BlockSpec((r, tn), lambda j: (0, j)),
        out_shape=jax.ShapeDtypeStruct((r, n), F32),
        compiler_params=_cparams(("arbitrary",)),
        name="modulation",
    )(cc, mod_w, mod_b.reshape(1, n))


def _row_select(tile_rows, tile_idx, n_lat, ctx_vec, lat_vec):
    row = tile_idx * tile_rows + lax.broadcasted_iota(jnp.int32, (tile_rows, 1), 0)
    return jnp.where(row < n_lat, lat_vec, ctx_vec)


def _norm_modulate(x_ref, h_ref, gain_ref, mc_ref, ml_ref, shift_row, tile_idx, n_lat):
    tm = x_ref.shape[0]
    x = x_ref[...]
    xn = x * lax.rsqrt(jnp.mean(x * x, axis=-1, keepdims=True) + RMS_EPS)
    sh_l, sh_c = ml_ref[shift_row:shift_row + 1, :], mc_ref[shift_row:shift_row + 1, :]
    amp_l = gain_ref[...] * (1.0 + ml_ref[shift_row + 1:shift_row + 2, :])
    amp_c = gain_ref[...] * (1.0 + mc_ref[shift_row + 1:shift_row + 2, :])
    all_latent = (tile_idx + 1) * tm <= n_lat

    @pl.when(all_latent)
    def _():
        h_ref[...] = (xn * amp_l + sh_l).astype(h_ref.dtype)

    @pl.when(jnp.logical_not(all_latent))
    def _():
        amp = _row_select(tm, tile_idx, n_lat, amp_c, amp_l)
        shift = _row_select(tm, tile_idx, n_lat, sh_c, sh_l)
        h_ref[...] = (xn * amp + shift).astype(h_ref.dtype)


def _inproj_kernel(x_ref, mc_ref, ml_ref, g_ref, w_ref, o32_ref, o16_ref, h_ref, *, n_lat):
    j = pl.program_id(2)
    n32 = P32_COLS // PROJ_TILE

    @pl.when(j == 0)
    def _():
        _norm_modulate(x_ref, h_ref, g_ref, mc_ref, ml_ref, 0, pl.program_id(1), n_lat)

    acc = jnp.dot(h_ref[...], w_ref[...], preferred_element_type=F32)

    @pl.when(j < n32)
    def _():
        o32_ref[...] = acc

    @pl.when(j >= n32)
    def _():
        o16_ref[...] = acc.astype(BF16)


def _inproj(xs, mod_c, mod_l, gain, w, layer, n_lat):
    b, s, d = xs.shape
    tm = _pick_tile(s, 1088)
    tn = PROJ_TILE
    n32 = P32_COLS // tn
    return pl.pallas_call(
        functools.partial(_inproj_kernel, n_lat=n_lat),
        grid=(b, s // tm, (P32_COLS + P16_COLS) // tn),
        in_specs=[pl.BlockSpec((None, tm, d), lambda bi, i, j: (bi, i, 0)),
                  pl.BlockSpec((8, d), lambda bi, i, j: (0, 0)),
                  pl.BlockSpec((None, 8, d), lambda bi, i, j: (bi, 0, 0)),
                  pl.BlockSpec((1, d), lambda bi, i, j: (0, 0)),
                  pl.BlockSpec((None, d, tn), lambda bi, i, j: (layer, 0, j))],
        out_specs=[pl.BlockSpec((None, tm, tn), lambda bi, i, j: (bi, i, jnp.minimum(j, n32 - 1))),
                   pl.BlockSpec((None, tm, tn), lambda bi, i, j: (bi, i, jnp.maximum(j - n32, 0)))],
        out_shape=[jax.ShapeDtypeStruct((b, s, P32_COLS), F32), jax.ShapeDtypeStruct((b, s, P16_COLS), BF16)],
        scratch_shapes=[pltpu.VMEM((tm, d), BF16)],
        compiler_params=_cparams(("parallel", "parallel", "arbitrary")),
        name="inproj",
    )(xs, mod_c, mod_l, gain, w)


def _mla_prep_kernel(ql_ref, kv_ref, cos_ref, sin_ref, gq_ref, gkv_ref, wq_ref, wk_ref, wv_ref, pl_ref,
                     q_ref, k_ref, v_ref, *, scale):
    nslot = MLA_HEADS * MLA_SLOT
    cos = jnp.concatenate([cos_ref[...]] * MLA_HEADS, axis=-1)
    sin = jnp.concatenate([sin_ref[...]] * MLA_HEADS, axis=-1)

    ql = ql_ref[...].astype(F32)
    qn = ql * lax.rsqrt(jnp.mean(ql * ql, axis=-1, keepdims=True) + RMS_EPS) * gq_ref[...]
    q2 = jnp.dot(qn.astype(BF16), wq_ref[...], preferred_element_type=F32)
    q = (q2[:, :nslot] * cos + q2[:, nslot:] * sin) * scale
    q_ref[...] = q.astype(BF16)

    kvkr = kv_ref[...]
    kvl = kvkr[:, :MLA_KV_RANK]
    kr = kvkr[:, MLA_KV_RANK:]
    kvn = (kvl * lax.rsqrt(jnp.mean(kvl * kvl, axis=-1, keepdims=True) + RMS_EPS) * gkv_ref[...]).astype(BF16)
    kn = jnp.dot(kvn, wk_ref[...], preferred_element_type=F32)
    v_ref[...] = jnp.dot(kvn, wv_ref[...], preferred_element_type=F32).astype(BF16)
    kr_hi = kr.astype(BF16)
    kr_lo = (kr - kr_hi.astype(F32)).astype(BF16)
    kr2 = jnp.dot(jnp.concatenate([kr_hi, kr_lo], axis=-1), pl_ref[...], preferred_element_type=F32)
    k_ref[...] = (kn + kr2[:, :nslot] * cos + kr2[:, nslot:] * sin).astype(BF16)


def _mla_prep(p32, p16, cos, sin, gq, gkv, wq, wk, wv, place):
    b, s, _ = p32.shape
    tm = _pick_tile(s, 544)
    nslot = MLA_HEADS * MLA_SLOT
    scale = (MLA_NOPE + MLA_ROPE) ** -0.5 * LOG2_E
    return pl.pallas_call(
        functools.partial(_mla_prep_kernel, scale=scale),
        grid=(b, s // tm),
        in_specs=[pl.BlockSpec((None, tm, MLA_Q_RANK), lambda bi, i: (bi, i, P_MLA_Q // MLA_Q_RANK)),
                  pl.BlockSpec((None, tm, 2 * MLA_SLOT), lambda bi, i: (bi, i, P_MLA_KV // (2 * MLA_SLOT))),
                  pl.BlockSpec((tm, MLA_SLOT), lambda bi, i: (i, 0)),
                  pl.BlockSpec((tm, MLA_SLOT), lambda bi, i: (i, 0)),
                  _full(gq), _full(gkv), _full(wq), _full(wk), _full(wv), _full(place)],
        out_specs=[pl.BlockSpec((None, tm, nslot), lambda bi, i: (bi, i, 0)),
                   pl.BlockSpec((None, tm, nslot), lambda bi, i: (bi, i, 0)),
                   pl.BlockSpec((None, tm, MLA_HEADS * MLA_V), lambda bi, i: (bi, i, 0))],
        out_shape=[jax.ShapeDtypeStruct((b, s, nslot), BF16),
                   jax.ShapeDtypeStruct((b, s, nslot), BF16),
                   jax.ShapeDtypeStruct((b, s, MLA_HEADS * MLA_V), BF16)],
        compiler_params=_cparams(("parallel", "parallel")),
        name="mla_prep",
    )(p16, p32, cos, sin, gq, gkv, wq, wk, wv, place)


def _flash_softmax(h, s, m_ref, l_ref):
    m_prev = m_ref[h]
    m_new = jnp.maximum(m_prev, jnp.max(s, axis=-1, keepdims=True))
    alpha = jnp.exp2(m_prev - m_new)
    p = jnp.exp2(s - jnp.concatenate([m_new] * (s.shape[1] // 128), axis=-1))
    l_ref[h] = alpha * l_ref[h] + jnp.sum(p, axis=-1, keepdims=True)
    m_ref[h] = m_new
    return alpha, p.astype(BF16)


def _mla_flash_kernel(*refs, tk, aliased):
    if aliased:
        q_ref, k_ref, v_ref, _, o_ref, m_ref, l_ref, acc_ref = refs
    else:
        q_ref, k_ref, v_ref, o_ref, m_ref, l_ref, acc_ref = refs
    tq = q_ref.shape[0]
    nk = k_ref.shape[0]
    n_loop = nk // tk
    m_ref[...] = jnp.full(m_ref.shape, -jnp.inf, F32)
    l_ref[...] = jnp.zeros(l_ref.shape, F32)
    acc_ref[...] = jnp.zeros(acc_ref.shape, F32)

    def chunk(rows):
        heads = range(MLA_HEADS)
        hs = [slice(h * MLA_SLOT, (h + 1) * MLA_SLOT) for h in heads]
        vs = [slice((h // 2) * 128, (h // 2) * 128 + 128) for h in heads]
        s = [lax.dot_general(q_ref[:, hs[h]], k_ref[rows, hs[h]], (((1,), (1,)), ((), ())),
                             preferred_element_type=F32) for h in heads]
        ap = [_flash_softmax(h, s[h], m_ref, l_ref) for h in heads]
        for h in heads:
            alpha, p = ap[h]
            acc_ref[h] = alpha * acc_ref[h] + jnp.dot(p, v_ref[rows, vs[h]], preferred_element_type=F32)

    if n_loop:
        def body(j, carry):
            chunk(pl.ds(pl.multiple_of(j * tk, tk), tk))
            return carry
        lax.fori_loop(0, n_loop, body, 0)
    if nk > n_loop * tk:
        chunk(slice(n_loop * tk, nk))

    lane = lax.broadcasted_iota(jnp.int32, (tq, 128), 1)
    outs = []
    for pair in range(MLA_HEADS // 2):
        o0 = acc_ref[2 * pair] / l_ref[2 * pair]
        o1 = acc_ref[2 * pair + 1] / l_ref[2 * pair + 1]
        outs.append(jnp.where(lane < MLA_V, o0, o1))
    o_ref[...] = jnp.concatenate(outs, axis=-1).astype(o_ref.dtype)


def _mla_flash(q, k, v, n_lat, prev=None):
    b, s, nslot = q.shape
    n_ctx = s - n_lat
    nv = MLA_HEADS * MLA_V
    tk = 512
    if prev is None:
        tq = _pick_tile(n_lat, 1024)
        q_off, n_q, kv_rows, kv_blk = 0, n_lat // tq, s, 0
    else:
        assert n_lat % n_ctx == 0
        tq = _pick_tile(n_ctx, 256)
        q_off, n_q, kv_rows, kv_blk = n_lat // tq, n_ctx // tq, n_ctx, n_lat // n_ctx
    in_specs = [pl.BlockSpec((None, tq, nslot), lambda bi, i: (bi, i + q_off, 0)),
                pl.BlockSpec((None, kv_rows, nslot), lambda bi, i: (bi, kv_blk, 0)),
                pl.BlockSpec((None, kv_rows, nv), lambda bi, i: (bi, kv_blk, 0))]
    args = [q, k, v]
    aliases = {}
    if prev is not None:
        in_specs.append(pl.BlockSpec(memory_space=pl.ANY))
        args.append(prev)
        aliases = {3: 0}
    return pl.pallas_call(
        functools.partial(_mla_flash_kernel, tk=tk, aliased=prev is not None),
        grid=(b, n_q),
        in_specs=in_specs,
        out_specs=pl.BlockSpec((None, tq, nv), lambda bi, i: (bi, i + q_off, 0)),
        out_shape=jax.ShapeDtypeStruct((b, s, nv), BF16),
        scratch_shapes=[pltpu.VMEM((MLA_HEADS, tq, 128), F32),
                        pltpu.VMEM((MLA_HEADS, tq, 128), F32),
                        pltpu.VMEM((MLA_HEADS, tq, 128), F32)],
        input_output_aliases=aliases,
        compiler_params=_cparams(("parallel", "arbitrary")),
        name="mla_flash",
    )(*args)


NA_QROWS = 4
NA_SLAB = NA_QROWS + NA_WIN_ROWS


def _na_geometry(rows):
    assert rows % NA_QROWS == 0 and rows >= NA_SLAB
    nblk = rows // NA_QROWS
    qc = np.arange(GRID_W)
    cs = np.clip(qc - NA_WIN_COLS // 2, 0, GRID_W - NA_WIN_COLS)
    col_valid = (qc[None, :] >= cs[:, None]) & (qc[None, :] < cs[:, None] + NA_WIN_COLS)
    rel_c = np.clip(qc[None, :] - qc[:, None] + NA_WIN_COLS - 1, 0, 2 * NA_WIN_COLS - 2)
    onehot_c = (rel_c[None] == np.arange(2 * NA_WIN_COLS - 1)[:, None, None]) & col_valid[None]
    patterns, cls, starts = {}, [], []
    for i in range(nblk):
        r0 = i * NA_QROWS
        start = int(np.clip(r0 - NA_WIN_ROWS // 2, 0, rows - NA_SLAB))
        qr = r0 + np.arange(NA_QROWS)
        rs = np.clip(qr - NA_WIN_ROWS // 2, 0, rows - NA_WIN_ROWS)
        key = (start - r0,) + tuple((rs - r0).tolist())
        if key not in patterns:
            kr = start + np.arange(NA_SLAB)
            row_valid = (kr[None, :] >= rs[:, None]) & (kr[None, :] < rs[:, None] + NA_WIN_ROWS)
            rel_r = np.clip(kr[None, :] - qr[:, None] + NA_WIN_ROWS - 1, 0, 2 * NA_WIN_ROWS - 2)
            patterns[key] = (len(patterns), np.where(row_valid, rel_r, -1))
        cls.append(patterns[key][0])
        starts.append(start)
    ordered = sorted(patterns.values(), key=lambda z: z[0])
    rel_rows = np.stack([z[1] for z in ordered])
    meta = np.stack([np.asarray(cls, np.int32), np.asarray(starts, np.int32)])
    return meta, rel_rows, (onehot_c.astype(np.float32), col_valid)


def _na_bias_table(rpb, rel_rows, col_tables):
    onehot_c, col_valid = col_tables
    bc = jnp.einsum('hrc,cqk->hrqk', rpb.astype(F32), onehot_c, precision=lax.Precision.HIGHEST)
    bc = jnp.where(col_valid, bc, -jnp.inf)
    outside = jnp.full(bc.shape[:1] + bc.shape[2:], -jnp.inf, F32)
    pats = []
    for pat in rel_rows:
        qrows = [jnp.concatenate([bc[:, r] if r >= 0 else outside for r in row], axis=-1) for row in pat]
        pats.append(jnp.concatenate(qrows, axis=1))
    return jnp.stack(pats)


def _na_scores(q_pair, lane, h, k_parts):
    in_head = (lane < NA_DH) if h % 2 == 0 else (lane >= NA_DH)
    qm = jnp.where(in_head, q_pair * (NA_DH ** -0.5), 0.0).astype(BF16)
    return [lax.dot_general(qm, k, (((1,), (1,)), ((), ())), preferred_element_type=F32) for k in k_parts]


def _na_softmax(scores, bias):
    if bias is not None:
        scores = [scores[0] + bias] + scores[1:]
    m = scores[0].max(axis=-1, keepdims=True)
    for s in scores[1:]:
        m = jnp.maximum(m, s.max(axis=-1, keepdims=True))
    probs = [jnp.exp(s - m) for s in scores]
    den = probs[0].sum(axis=-1, keepdims=True)
    for p in probs[1:]:
        den = den + p.sum(axis=-1, keepdims=True)
    return [p.astype(BF16) for p in probs], den


def _na_kernel(meta_ref, q_ref, k_ref, v_ref, bias_ref, o_ref, *, n_lat, with_ctx):
    i = pl.program_id(1)
    nq = q_ref.shape[0]
    n_lat_tiles = n_lat // nq
    n_all = k_ref.shape[0]
    lane = lax.broadcasted_iota(jnp.int32, (nq, 128), 1)

    def run(windowed):
        key_rows = [pl.ds(n_lat, n_all - n_lat)]
        if windowed:
            start = pl.multiple_of(meta_ref[1, jnp.minimum(i, n_lat_tiles - 1)] * GRID_W, GRID_W)
            key_rows = [pl.ds(start, NA_SLAB * GRID_W)] + key_rows
        pair_lanes = [slice((h // 2) * 128, (h // 2 + 1) * 128) for h in range(NA_HEADS)]
        scores = [_na_scores(q_ref[:, pair_lanes[h]], lane, h, [k_ref[r, pair_lanes[h]] for r in key_rows])
                  for h in range(NA_HEADS)]
        soft = [_na_softmax(scores[h], bias_ref[h] if windowed else None) for h in range(NA_HEADS)]
        heads = []
        for h in range(NA_HEADS):
            probs, den = soft[h]
            out = None
            for p, r in zip(probs, key_rows):
                po = jnp.dot(p, v_ref[r, pair_lanes[h]], preferred_element_type=F32)
                out = po if out is None else out + po
            heads.append(out / den)
        outs = [jnp.where(lane < NA_DH, heads[2 * pair], heads[2 * pair + 1]) for pair in range(NA_HEADS // 2)]
        o_ref[...] = jnp.concatenate(outs, axis=-1).astype(o_ref.dtype)

    if with_ctx:
        pl.when(i < n_lat_tiles)(lambda: run(True))
        pl.when(i >= n_lat_tiles)(lambda: run(False))
    else:
        run(True)


def _na_attention(p, meta, table, n_lat, with_ctx_queries):
    b, s, _ = p.shape
    nq = NA_QROWS * GRID_W
    nk = NA_SLAB * GRID_W
    assert (s - n_lat) % nq == 0
    nlt = n_lat // nq
    qb = P_NA_QKV // NA_W
    grid_spec = pltpu.PrefetchScalarGridSpec(
        num_scalar_prefetch=1,
        grid=(b, s // nq if with_ctx_queries else nlt),
        in_specs=[pl.BlockSpec((None, nq, NA_W), lambda bi, i, m: (bi, i, qb)),
                  pl.BlockSpec((None, s, NA_W), lambda bi, i, m: (bi, 0, qb + 1)),
                  pl.BlockSpec((None, s, NA_W), lambda bi, i, m: (bi, 0, qb + 2)),
                  pl.BlockSpec((None, NA_HEADS, nq, nk),
                               lambda bi, i, m: (m[0, jnp.minimum(i, nlt - 1)], 0, 0, 0))],
        out_specs=pl.BlockSpec((None, nq, NA_W), lambda bi, i, m: (bi, i, 0)))
    return pl.pallas_call(
        functools.partial(_na_kernel, n_lat=n_lat, with_ctx=with_ctx_queries),
        grid_spec=grid_spec,
        out_shape=jax.ShapeDtypeStruct((b, s, NA_W), BF16),
        compiler_params=_cparams(("parallel", "arbitrary")),
        name="na_attention",
    )(meta, p, p, p, table)


def _tile_conv(x, prev, nxt, w_ref, tile_idx, n_tiles, n_lat_tiles):
    r = x.shape[0]
    width = w_ref.shape[0]
    left = width // 2
    has_prev = jnp.logical_and(tile_idx != 0, tile_idx != n_lat_tiles)
    has_next = jnp.logical_and(tile_idx != n_tiles - 1, tile_idx != n_lat_tiles - 1)
    prev = jnp.where(has_prev, prev, 0.0)
    nxt = jnp.where(has_next, nxt, 0.0)
    xe = jnp.concatenate([prev, x, nxt], axis=0)
    acc = None
    for j in range(width):
        o = HALO - left + j
        term = xe[o:o + r, :] * w_ref[j:j + 1, :]
        acc = term if acc is None else acc + term
    return acc


def _halo_specs(width, col_block, tile_of):
    per = SEQ_TILE // HALO

    def main(bi, s, *_):
        return (bi, tile_of(s), col_block)

    def prev(bi, s, *_):
        return (bi, jnp.maximum(tile_of(s) * per - 1, 0), col_block)

    def make_next(n_tiles):
        def nxt(bi, s, *_):
            return (bi, jnp.minimum((tile_of(s) + 1) * per, n_tiles * per - 1), col_block)
        return nxt

    return main, prev, make_next


def _gdn_prep_kernel(x_ref, xp_ref, xn_ref, ba_ref, cw_ref, ones_ref, exp_ref, alog_ref, dtb_ref,
                     q_ref, k_ref, v_ref, beta_ref, g_ref, *, n_lat_tiles):
    i = pl.program_id(1)
    y = _tile_conv(x_ref[...], xp_ref[...], xn_ref[...], cw_ref, i, pl.num_programs(1), n_lat_tiles)
    y = y * _sigmoid(y)
    q = y[:, :GDN_W]
    k = y[:, GDN_W:2 * GDN_W]
    v_ref[...] = y[:, 2 * GDN_W:].astype(v_ref.dtype)

    def head_norm(u):
        parts = jnp.concatenate(_split3(u * u), axis=-1)
        ss = jnp.dot(parts, ones_ref[...], preferred_element_type=F32)
        return u * lax.rsqrt(ss + RMS_EPS)

    q_ref[...] = (head_norm(q) * (GDN_DK ** -0.5)).astype(q_ref.dtype)
    k_ref[...] = head_norm(k).astype(k_ref.dtype)

    ba = ba_ref[...]
    a = ba + dtb_ref[...]
    softplus = jnp.maximum(a, 0.0) + jnp.log1p(jnp.exp(-jnp.abs(a)))
    lane = lax.broadcasted_iota(jnp.int32, ba.shape, 1)
    compact = jnp.where(lane < 2 * GDN_HEADS, _sigmoid(ba), -jnp.exp(alog_ref[...]) * softplus)
    wide = jnp.dot(jnp.concatenate(_split3(compact), axis=-1), exp_ref[...], preferred_element_type=F32)
    half = 2 * GDN_W
    beta_ref[...] = wide[:, :half].astype(beta_ref.dtype)
    g_ref[...] = wide[:, half:]


def _gdn_prep(p, conv_w, a_log, dt_bias, n_lat):
    b, s, _ = p.shape
    n_tiles = s // SEQ_TILE
    main, prev, make_next = _halo_specs(3 * GDN_W, 0, lambda t: t)
    ones3 = jnp.concatenate([_head_block_ones(GDN_HEADS, GDN_DK)] * 3, axis=0)
    expand = np.zeros((128, 4 * GDN_W), np.float32)
    for kind in range(2):
        for d in range(2):
            for h in range(GDN_HEADS):
                c0 = kind * 2 * GDN_W + d * GDN_W + h * GDN_DV
                expand[kind * 2 * GDN_HEADS + d * GDN_HEADS + h, c0:c0 + GDN_DV] = 1.0
    expand3 = jnp.asarray(np.concatenate([expand] * 3, axis=0), BF16)
    lanes = jnp.zeros((1, 128), F32)
    alog_e = lanes.at[0, 2 * GDN_HEADS:4 * GDN_HEADS].set(a_log.astype(F32).reshape(-1))
    dtb_e = lanes.at[0, 2 * GDN_HEADS:4 * GDN_HEADS].set(dt_bias.astype(F32).reshape(-1))
    tok = lambda w: pl.BlockSpec((None, SEQ_TILE, w), lambda bi, i: (bi, i, 0))
    return pl.pallas_call(
        functools.partial(_gdn_prep_kernel, n_lat_tiles=n_lat // SEQ_TILE),
        grid=(b, n_tiles),
        in_specs=[pl.BlockSpec((None, SEQ_TILE, 3 * GDN_W), main),
                  pl.BlockSpec((None, HALO, 3 * GDN_W), prev),
                  pl.BlockSpec((None, HALO, 3 * GDN_W), make_next(n_tiles)),
                  pl.BlockSpec((None, SEQ_TILE, 128), lambda bi, i: (bi, i, P_GDN_BA // 128)),
                  _full(conv_w), _full(ones3), _full(expand3), _full(alog_e), _full(dtb_e)],
        out_specs=[tok(GDN_W), tok(GDN_W), tok(GDN_W), tok(2 * GDN_W), tok(2 * GDN_W)],
        out_shape=[jax.ShapeDtypeStruct((b, s, GDN_W), BF16)] * 3 + [jax.ShapeDtypeStruct((b, s, 2 * GDN_W), BF16),
                                                                    jax.ShapeDtypeStruct((b, s, 2 * GDN_W), F32)],
        compiler_params=_cparams(("parallel", "parallel")),
        name="gdn_prep",
    )(p, p, p, p, conv_w, ones3, expand3, alog_e, dtb_e)


GDN_PAIR = 2 * GDN_DK


def _gdn_masks():
    c, w = GDN_CHUNK, GDN_W
    r2, c2 = np.arange(GDN_PAIR)[:, None], np.arange(GDN_PAIR)[None, :]
    bd = ((r2 // c) == (c2 // c)).astype(np.float32)
    i = np.arange(c)[:, None]
    j = (np.arange(w) % c)[None, :]
    level = np.zeros((c, w), np.int32)
    for bit in range(6):
        level += ((i ^ j) >= (1 << bit)).astype(np.int32)
    lvl = np.stack([(level == m).astype(np.float32) for m in range(7)])
    dirm = np.stack([np.stack([(j <= i), (j < i)]), np.stack([(j >= i), (j > i)])]).astype(np.float32)
    tj = (np.arange(3 * c) % c)[None, :]
    tri = np.stack([(tj <= i), (tj >= i)]).astype(np.float32)
    return jnp.asarray(bd, BF16), jnp.asarray(lvl), jnp.asarray(dirm), jnp.asarray(tri, BF16)


def _heads_mm(x, y, bd, transpose_rhs=False):
    xb = x.astype(BF16)
    yb = y.astype(BF16)
    outs = []
    for pair in range(GDN_W // GDN_PAIR):
        ls = slice(pair * GDN_PAIR, (pair + 1) * GDN_PAIR)
        w = jnp.concatenate([yb[:, ls], yb[:, ls]], axis=0) * bd
        dims = (((1,), (1,)), ((), ())) if transpose_rhs else (((1,), (0,)), ((), ()))
        outs.append(lax.dot_general(xb[:, ls], w, dims, preferred_element_type=F32))
    return jnp.concatenate(outs, axis=1)


def _gdn_intra(probs, bd, lvl_ref, dirm_ref, tri_ref):
    c = GDN_CHUNK
    n = len(probs)
    eye = lvl_ref[0]
    gc, g_last, decay, gram = [], [], [], []
    for q, k, v, beta, g, rev in probs:
        d = 1 if rev else 0
        gcp = jnp.dot(tri_ref[d], jnp.concatenate(_split3(g), axis=0), preferred_element_type=F32)
        gc.append(gcp)
        g_last.append(gcp[0:1, :] if rev else gcp[c - 1:c, :])
        gc_row = jnp.sum(gcp * eye, axis=0, keepdims=True)
        decay.append(dirm_ref[d, 0] * jnp.exp(jnp.minimum(gcp - gc_row, 0.0)))
        gram.append(_heads_mm(jnp.concatenate([k, q], axis=0), k, bd, transpose_rhs=True))
    lower = [dirm_ref[1 if p[5] else 0, 1] * p[3] * gram[x][:c] * decay[x] for x, p in enumerate(probs)]
    a_intra = [gram[x][c:] * decay[x] for x in range(n)]
    t = [eye - lower[x] * lvl_ref[1] for x in range(n)]
    for lev in range(2, 7):
        y = [_heads_mm(t[x], lower[x] * lvl_ref[lev], bd) for x in range(n)]
        z = [_heads_mm(y[x], t[x], bd) for x in range(n)]
        t = [t[x] - z[x] for x in range(n)]
    e_gc = [jnp.exp(gc[x]) for x in range(n)]
    u = [_heads_mm(t[x], p[2] * p[3], bd) for x, p in enumerate(probs)]
    w = [_heads_mm(t[x], p[1] * p[3] * e_gc[x], bd) for x, p in enumerate(probs)]
    wq = [jnp.concatenate([w[x], p[0] * e_gc[x]], axis=0).astype(BF16) for x, p in enumerate(probs)]
    k_dec = [(p[1] * jnp.exp(g_last[x] - gc[x])).astype(BF16) for x, p in enumerate(probs)]
    g_tot = [jnp.exp(g_last[x]) for x in range(n)]
    return u, wq, k_dec, a_intra, g_tot


def _gdn_state_steps(items, bd):
    c = GDN_CHUNK
    pairs = [slice(p * GDN_PAIR, (p + 1) * GDN_PAIR) for p in range(GDN_W // GDN_PAIR)]
    bdf = bd.astype(F32)
    ws_qs = [jnp.concatenate([jnp.dot(wq[:, ls], s_ref[p].astype(BF16), preferred_element_type=F32)
                              for p, ls in enumerate(pairs)], axis=1)
             for _, wq, _, _, _, s_ref in items]
    v_new = [it[0] - ws[:c] for it, ws in zip(items, ws_qs)]
    outs = [ws[c:] + _heads_mm(it[3], vn, bd) for it, ws, vn in zip(items, ws_qs, v_new)]
    for (_, _, k_dec, _, g_tot, s_ref), vn in zip(items, v_new):
        vb = vn.astype(BF16)
        for p, ls in enumerate(pairs):
            kv = lax.dot_general(k_dec[:, ls], vb[:, ls], (((0,), (0,)), ((), ())), preferred_element_type=F32)
            s_ref[p] = s_ref[p] * g_tot[:, ls] + kv * bdf
    return outs


def _gdn_scan_kernel(qf, kf, vf, bf, gf, qb, kb, vb, bb, gb, bd_ref, lvl_ref, dirm_ref, tri_ref,
                     of_ref, ob_ref, sf_ref, sb_ref):
    @pl.when(pl.program_id(1) == 0)
    def _():
        sf_ref[...] = jnp.zeros(sf_ref.shape, F32)
        sb_ref[...] = jnp.zeros(sb_ref.shape, F32)

    bd = bd_ref[...]
    n = SEQ_TILE // GDN_CHUNK
    probs, rows = [], []
    for c in range(n):
        rf = slice(c * GDN_CHUNK, (c + 1) * GDN_CHUNK)
        rb = slice((n - 1 - c) * GDN_CHUNK, (n - c) * GDN_CHUNK)
        probs.append(tuple(r[rf, :].astype(F32) for r in (qf, kf, vf, bf, gf)) + (False,))
        probs.append(tuple(r[rb, :].astype(F32) for r in (qb, kb, vb, bb, gb)) + (True,))
        rows += [rf, rb]
    u, wq, k_dec, a_intra, g_tot = _gdn_intra(probs, bd, lvl_ref, dirm_ref, tri_ref)
    for c in range(n):
        xs = (2 * c, 2 * c + 1)
        outs = _gdn_state_steps([(u[x], wq[x], k_dec[x], a_intra[x], g_tot[x], sb_ref if probs[x][5] else sf_ref)
                                 for x in xs], bd)
        for x, o in zip(xs, outs):
            (ob_ref if probs[x][5] else of_ref)[rows[x], :] = o


def _seq_tile_maps(n_lat_tiles, n_tiles):
    fwd = lambda s: lax.rem(s + n_lat_tiles, n_tiles)
    bwd = lambda s: n_tiles - 1 - s
    return fwd, bwd


def _gdn_scan(q, k, v, beta, g, n_lat):
    b, s, _ = q.shape
    n_tiles = s // SEQ_TILE
    fwd, bwd = _seq_tile_maps(n_lat // SEQ_TILE, n_tiles)
    spec = lambda tile_of, col: pl.BlockSpec((None, SEQ_TILE, GDN_W), lambda bi, t: (bi, tile_of(t), col))
    masks = _gdn_masks()
    in_specs = [spec(fwd, 0)] * 5 + [spec(bwd, 0)] * 3 + [spec(bwd, 1)] * 2 + [_full(m) for m in masks]
    return pl.pallas_call(
        _gdn_scan_kernel,
        grid=(b, n_tiles),
        in_specs=in_specs,
        out_specs=[spec(fwd, 0), spec(bwd, 0)],
        out_shape=[jax.ShapeDtypeStruct((b, s, GDN_W), F32)] * 2,
        scratch_shapes=[pltpu.VMEM((GDN_W // GDN_PAIR, GDN_PAIR, GDN_PAIR), F32)] * 2,
        compiler_params=_cparams(("parallel", "arbitrary")),
        name="gdn_scan",
    )(q, k, v, beta, g, q, k, v, beta, g, *masks)


def _lru_tile_prep(x_ref, xp_ref, xn_ref, cw_ref, cb_ref, wg_ref, bg_ref, nla_ref, a_ref, b_ref,
                   tile_idx, n_tiles, n_lat_tiles):
    xb = _tile_conv(x_ref[...], xp_ref[...], xn_ref[...], cw_ref, tile_idx, n_tiles, n_lat_tiles) + cb_ref[...]
    gates = _sigmoid(jnp.dot(xb.astype(BF16), wg_ref[...], preferred_element_type=F32) + bg_ref[...])
    log_a = nla_ref[...] * gates[:, :LRU_W]
    a_ref[...] = jnp.exp(log_a)
    th = jnp.tanh(log_a)
    b_ref[...] = jnp.sqrt(-2.0 * th / (1.0 - th)) * gates[:, LRU_W:] * xb


def _scan_group(a, b, h, row, reverse):
    for d in (1, 2, 4):
        if reverse:
            keep = row < 8 - d
            shift = 8 - d
        else:
            keep = row >= d
            shift = d
        a_s = jnp.where(keep, pltpu.roll(a, shift, 0), 1.0)
        b_s = jnp.where(keep, pltpu.roll(b, shift, 0), 0.0)
        b = a * b_s + b
        a = a * a_s
    return a * h + b


def _lru_scan_kernel(xf, xfp, xfn, xb, xbp, xbn, cw_ref, cb_ref, wgf, bgf, nlaf, wgb, bgb, nlab,
                     hf_ref, hb_ref, af_ref, bf_ref, ab_ref, bb_ref, cf_ref, cbk_ref, *, n_lat_tiles):
    s = pl.program_id(1)
    n_tiles = pl.num_programs(1)

    @pl.when(s == 0)
    def _():
        cf_ref[...] = jnp.zeros(cf_ref.shape, F32)
        cbk_ref[...] = jnp.zeros(cbk_ref.shape, F32)

    t_f, t_b = (m(s) for m in _seq_tile_maps(n_lat_tiles, n_tiles))
    _lru_tile_prep(xf, xfp, xfn, cw_ref, cb_ref, wgf, bgf, nlaf, af_ref, bf_ref, t_f, n_tiles, n_lat_tiles)
    _lru_tile_prep(xb, xbp, xbn, cw_ref, cb_ref, wgb, bgb, nlab, ab_ref, bb_ref, t_b, n_tiles, n_lat_tiles)

    n_groups = SEQ_TILE // 8
    row = lax.broadcasted_iota(jnp.int32, (8, LRU_W), 0)

    def body(gi, carry):
        h_f, h_b = carry
        rf = pl.ds(pl.multiple_of(gi * 8, 8), 8)
        rb = pl.ds(pl.multiple_of((n_groups - 1 - gi) * 8, 8), 8)
        out_f = _scan_group(af_ref[rf, :], bf_ref[rf, :], h_f, row, False)
        out_b = _scan_group(ab_ref[rb, :], bb_ref[rb, :], h_b, row, True)
        hf_ref[rf, :] = out_f
        hb_ref[rb, :] = out_b
        return (jnp.broadcast_to(out_f[7:8, :], (8, LRU_W)), jnp.broadcast_to(out_b[0:1, :], (8, LRU_W)))

    h_f, h_b = lax.fori_loop(0, n_groups, body, (cf_ref[...], cbk_ref[...]), unroll=4)
    cf_ref[...] = h_f
    cbk_ref[...] = h_b


def _lru_scan(p, conv_w, conv_b, w_r, b_r, w_i, b_i, lam, n_lat):
    b, s, _ = p.shape
    n_tiles = s // SEQ_TILE
    nlt = n_lat // SEQ_TILE
    fwd, bwd = _seq_tile_maps(nlt, n_tiles)
    col = P_LRU_X // LRU_W
    specs = []
    for tile_of in (fwd, bwd):
        main, prev, make_next = _halo_specs(LRU_W, col, tile_of)
        specs += [pl.BlockSpec((None, SEQ_TILE, LRU_W), main), pl.BlockSpec((None, HALO, LRU_W), prev),
                  pl.BlockSpec((None, HALO, LRU_W), make_next(n_tiles))]

    def blockdiag(w):
        return jax.scipy.linalg.block_diag(*[w[n] for n in range(LRU_BLOCKS)])

    dir_args = []
    for d in range(2):
        wg = jnp.concatenate([blockdiag(w_r[d]), blockdiag(w_i[d])], axis=1).astype(BF16)
        bg = jnp.concatenate([b_r[d], b_i[d]]).astype(F32).reshape(1, 2 * LRU_W)
        nla = (-LRU_C * jax.nn.softplus(-lam[d].astype(F32))).reshape(1, LRU_W)
        dir_args += [wg, bg, nla]
    cb2 = conv_b.reshape(1, LRU_W)
    out_spec = lambda tile_of: pl.BlockSpec((None, SEQ_TILE, LRU_W), lambda bi, t: (bi, tile_of(t), 0))
    return pl.pallas_call(
        functools.partial(_lru_scan_kernel, n_lat_tiles=nlt),
        grid=(b, n_tiles),
        in_specs=specs + [_full(conv_w), _full(cb2)] + [_full(a) for a in dir_args],
        out_specs=[out_spec(fwd), out_spec(bwd)],
        out_shape=[jax.ShapeDtypeStruct((b, s, LRU_W), F32)] * 2,
        scratch_shapes=[pltpu.VMEM((SEQ_TILE, LRU_W), F32)] * 4 + [pltpu.VMEM((8, LRU_W), F32)] * 2,
        compiler_params=_cparams(("parallel", "arbitrary")),
        name="lru_scan",
    )(p, p, p, p, p, p, conv_w, cb2, *dir_args)


def _post_kernel(of_ref, ob_ref, z_ref, gn_ref, ones_ref, hf_ref, hb_ref, y_ref, ua_ref, ub_ref):
    o = of_ref[...] + ob_ref[...]
    ms = jnp.dot(jnp.concatenate(_split3(o * o), axis=-1), ones_ref[...], preferred_element_type=F32) * (1.0 / GDN_DV)
    z = z_ref[...]
    ua_ref[...] = (o * lax.rsqrt(ms + RMS_EPS) * gn_ref[...] * (z * _sigmoid(z))).astype(ua_ref.dtype)
    y = y_ref[...]
    gelu = 0.5 * y * (1.0 + jnp.tanh(0.7978845608028654 * (y + 0.044715 * (y * y * y))))
    ub_ref[...] = ((hf_ref[...] + hb_ref[...]) * gelu).astype(ub_ref.dtype)


def _post(o_f, o_b, h_f, h_b, p, gdn_norm_g):
    b, s, _ = p.shape
    tm = _pick_tile(s, 544)
    ones3 = jnp.concatenate([_head_block_ones(GDN_HEADS, GDN_DV)] * 3, axis=0)
    gn = jnp.tile(gdn_norm_g.astype(F32), GDN_HEADS).reshape(1, GDN_W)
    tok = pl.BlockSpec((None, tm, GDN_W), lambda bi, i: (bi, i, 0))
    pcol = lambda c: pl.BlockSpec((None, tm, GDN_W), lambda bi, i: (bi, i, c))
    return pl.pallas_call(
        _post_kernel,
        grid=(b, s // tm),
        in_specs=[tok, tok, pcol(P_GDN_Z // GDN_W), _full(gn), _full(ones3), tok, tok, pcol(P_LRU_Y // LRU_W)],
        out_specs=[tok, tok],
        out_shape=[jax.ShapeDtypeStruct((b, s, GDN_W), BF16)] * 2,
        compiler_params=_cparams(("parallel", "parallel")),
        name="gdn_lru_post",
    )(o_f, o_b, p, gn, ones3, h_f, h_b, p)


def _merge_kernel(x_ref, mc_ref, ml_ref, u0_ref, u1_ref, u2_ref, u3_ref, t0_ref, t1_ref, t2_ref, t3_ref,
                  bg_ref, wb_ref, wo_ref, o_ref, *, n_lat):
    merged = None
    for n, (u_ref, t_ref) in enumerate(((u0_ref, t0_ref), (u1_ref, t1_ref), (u2_ref, t2_ref), (u3_ref, t3_ref))):
        gate = _sigmoid(t_ref[...] + bg_ref[n:n + 1, :].astype(BF16))
        term = gate.astype(F32) * jnp.dot(u_ref[...], wb_ref[n], preferred_element_type=F32)
        merged = term if merged is None else merged + term
    out = jnp.dot(merged.astype(BF16), wo_ref[...], preferred_element_type=F32)
    g1 = _row_select(x_ref.shape[0], pl.program_id(1), n_lat, mc_ref[2:3, :], ml_ref[2:3, :])
    o_ref[...] = x_ref[...] + g1 * out


def _merge(xs, mod_c, mod_l, branches, p, b_gate, w_branch, w_out, layer, n_lat, with_ctx):
    b, s, d = xs.shape
    rows = s if with_ctx else n_lat
    tm = _pick_tile(rows, 544)
    gate_blk = P_GATES // d
    tok = lambda w: pl.BlockSpec((None, tm, w), lambda bi, i: (bi, i, 0))
    in_specs = [tok(d), pl.BlockSpec((8, d), lambda bi, i: (0, 0)), pl.BlockSpec((None, 8, d), lambda bi, i: (bi, 0, 0))]
    in_specs += [tok(BRANCH_W)] * N_BRANCH
    in_specs += [pl.BlockSpec((None, tm, d), functools.partial(lambda bi, i, n: (bi, i, gate_blk + n), n=n))
                 for n in range(N_BRANCH)]
    in_specs += [_full(b_gate),
                 pl.BlockSpec((None,) + w_branch.shape[1:], lambda bi, i: (layer, 0, 0, 0)),
                 pl.BlockSpec((None,) + w_out.shape[1:], lambda bi, i: (layer, 0, 0))]
    return pl.pallas_call(
        functools.partial(_merge_kernel, n_lat=n_lat),
        grid=(b, rows // tm),
        in_specs=in_specs,
        out_specs=tok(d),
        out_shape=jax.ShapeDtypeStruct((b, rows, d), F32),
        input_output_aliases={0: 0} if with_ctx else {},
        compiler_params=_cparams(("parallel", "parallel")),
        name="merge",
    )(xs, mod_c, mod_l, *branches, p, p, p, p, b_gate, w_branch, w_out)


def _mlp_kernel(x_ref, mc_ref, ml_ref, gn_ref, gf_ref, w1_ref, w2_ref, o_ref, h_ref, acc_ref, *, n_lat, final_norm):
    f = pl.program_id(2)
    tm = x_ref.shape[0]
    i = pl.program_id(1)

    @pl.when(f == 0)
    def _():
        _norm_modulate(x_ref, h_ref, gn_ref, mc_ref, ml_ref, 3, i, n_lat)

    a = jnp.maximum(jnp.dot(h_ref[...], w1_ref[...], preferred_element_type=F32), 0.0)
    part = jnp.dot((a * a).astype(BF16), w2_ref[...], preferred_element_type=F32)

    @pl.when(f == 0)
    def _():
        acc_ref[...] = part

    @pl.when(f > 0)
    def _():
        acc_ref[...] += part

    @pl.when(f == pl.num_programs(2) - 1)
    def _():
        g2 = _row_select(tm, i, n_lat, mc_ref[5:6, :], ml_ref[5:6, :])
        y = x_ref[...] + g2 * acc_ref[...]
        if final_norm:
            y = y * lax.rsqrt(jnp.mean(y * y, axis=-1, keepdims=True) + RMS_EPS) * gf_ref[...]
        o_ref[...] = y


def _mlp(xs, mod_c, mod_l, gain, w1, w2, layer, final_gain, n_lat, final_norm):
    b, rows, d = xs.shape
    dff = w1.shape[-1]
    tm = _pick_tile(rows, 1088)
    tf = 1024
    row = pl.BlockSpec((1, d), lambda bi, i, f: (0, 0))
    return pl.pallas_call(
        functools.partial(_mlp_kernel, n_lat=n_lat, final_norm=final_norm),
        grid=(b, rows // tm, dff // tf),
        in_specs=[pl.BlockSpec((None, tm, d), lambda bi, i, f: (bi, i, 0)),
                  pl.BlockSpec((8, d), lambda bi, i, f: (0, 0)),
                  pl.BlockSpec((None, 8, d), lambda bi, i, f: (bi, 0, 0)),
                  row, row,
                  pl.BlockSpec((None, d, tf), lambda bi, i, f: (layer, 0, f)),
                  pl.BlockSpec((None, tf, d), lambda bi, i, f: (layer, f, 0))],
        out_specs=pl.BlockSpec((None, tm, d), lambda bi, i, f: (bi, i, 0)),
        out_shape=jax.ShapeDtypeStruct((b, rows, d), F32),
        scratch_shapes=[pltpu.VMEM((tm, d), BF16), pltpu.VMEM((tm, d), F32)],
        compiler_params=_cparams(("parallel", "parallel", "arbitrary")),
        name="mlp",
    )(xs, mod_c, mod_l, gain, final_gain, w1, w2)


def kernel(x, c, ctx, c_ctx, mod_w, mod_b, norm1_g, norm2_g, w_in, b_gate, gdn_conv_w, gdn_a_log, gdn_dt_bias,
           gdn_norm_g, lru_conv_w, lru_conv_b, lru_w_r, lru_b_r, lru_w_i, lru_b_i, lru_lambda, mla_q_norm_g,
           mla_w_uq, mla_kv_norm_g, mla_w_ukv, na_rpb, w_branch, w_out, mlp_w1, mlp_w2, final_norm_g):
    bsz, n_tok, d = x.shape
    n_ctx = ctx.shape[1]
    depth = w_in.shape[0]
    assert n_ctx % SEQ_TILE == 0 and n_tok % SEQ_TILE == 0 and n_tok % GRID_W == 0
    na_meta, na_rel_rows, na_cols = _na_geometry(n_tok // GRID_W)
    na_meta = jnp.asarray(na_meta)
    cos, sin = _rope_tables(n_tok, n_ctx)

    n_rows = -(-(bsz + 1) // 8) * 8
    cc = jnp.zeros((n_rows, d), F32).at[:bsz].set(c).at[bsz].set(c_ctx)
    final_gain = final_norm_g.reshape(1, d)

    w_in_all = _arrange_w_in(w_in)
    wb_all = w_branch.astype(BF16)
    wo_all = w_out.astype(BF16)
    w1_all = mlp_w1.astype(BF16)
    w2_all = mlp_w2.astype(BF16)

    xs = jnp.concatenate([x, ctx], axis=1)
    for l in range(depth):
        need_ctx = l < depth - 1
        mod = _modulation(cc, mod_w[l], mod_b[l]).reshape(n_rows, N_MOD, d)
        pad = jnp.zeros((8 - N_MOD, d), F32)
        mod_c = jnp.concatenate([mod[bsz], pad], axis=0)
        mod_l = jnp.concatenate([mod[:bsz], jnp.broadcast_to(pad, (bsz, 8 - N_MOD, d))], axis=1)

        wq, wk, wv, place = _arrange_mla(mla_w_uq[l], mla_w_ukv[l])
        gq = mla_q_norm_g[l].reshape(1, -1)
        gkv = mla_kv_norm_g[l].reshape(1, -1)
        g1n = norm1_g[l].reshape(1, d)
        g2n = norm2_g[l].reshape(1, d)

        p32, p16 = _inproj(xs, mod_c, mod_l, g1n, w_in_all, l, n_tok)

        gq_, gk_, gv_, gbeta, gg = _gdn_prep(p32, gdn_conv_w[l], gdn_a_log[l], gdn_dt_bias[l], n_tok)
        o_f, o_b = _gdn_scan(gq_, gk_, gv_, gbeta, gg, n_tok)
        h_f, h_b = _lru_scan(p32, lru_conv_w[l], lru_conv_b[l], lru_w_r[l], lru_b_r[l], lru_w_i[l], lru_b_i[l],
                             lru_lambda[l], n_tok)
        ua, ub = _post(o_f, o_b, h_f, h_b, p32, gdn_norm_g[l])

        mq, mk, mv = _mla_prep(p32, p16, cos, sin, gq, gkv, wq, wk, wv, place)
        uc = _mla_flash(mq, mk, mv, n_tok)
        if need_ctx:
            uc = _mla_flash(mq, mk, mv, n_tok, prev=uc)
        ud = _na_attention(p16, na_meta, _na_bias_table(na_rpb[l], na_rel_rows, na_cols), n_tok, need_ctx)

        xs = _merge(xs, mod_c, mod_l, (ua, ub, uc, ud), p16, b_gate[l], wb_all, wo_all, l, n_tok, need_ctx)
        xs = _mlp(xs, mod_c, mod_l, g2n, w1_all, w2_all, l, final_gain, n_tok, l == depth - 1)
    return xs
```

```python
import functools

import jax
import jax.numpy as jnp
import numpy as np
from jax import lax
from jax.experimental import pallas as pl
from jax.experimental.pallas import tpu as pltpu

F32 = jnp.float32
BF16 = jnp.bfloat16

GRID_W = 64
N_MOD = 6
RMS_EPS = 1e-6
GDN_HEADS = 4
GDN_DK = 64
GDN_DV = 64
GDN_CHUNK = 64
GDN_W = GDN_HEADS * GDN_DV
GDN_CONV = 4
LRU_W = 256
LRU_BLOCKS = 4
LRU_BLOCK_W = LRU_W // LRU_BLOCKS
LRU_CONV = 4
LRU_C = 8.0
MLA_HEADS = 4
MLA_Q_RANK = 256
MLA_KV_RANK = 128
MLA_NOPE = 64
MLA_ROPE = 32
MLA_V = 64
MLA_SLOT = 128
ROPE_BASE = 10000.0
LOG2_E = 1.4426950408889634
NA_HEADS = 4
NA_DH = 64
NA_W = NA_HEADS * NA_DH
NA_WIN_ROWS = 8
NA_WIN_COLS = 16
N_BRANCH = 4
BRANCH_W = 256

SEQ_TILE = 256
HALO = 8

_REF_COLS = {}
_off = 0
for _name, _w in (('gdn_qkv', 3 * GDN_W), ('gdn_z', GDN_W), ('gdn_beta', 2 * GDN_HEADS), ('gdn_a', 2 * GDN_HEADS),
                  ('lru_x', LRU_W), ('lru_y', LRU_W), ('mla_q', MLA_Q_RANK), ('mla_kv', MLA_KV_RANK),
                  ('mla_kr', MLA_ROPE), ('na_qkv', 3 * NA_W)):
    _REF_COLS[_name] = (_off, _w)
    _off += _w
N_MIX_COLS = _off

P_GDN_QKV = 0
P_GDN_Z = 768
P_LRU_X = 1024
P_LRU_Y = 1280
P_MLA_KV = 1536
P_MLA_KR = 1664
P_GDN_BA = 1792
P32_COLS = 2048
P_MLA_Q = 0
P_NA_QKV = 256
P_GATES = 1024
P16_COLS = P_GATES + N_BRANCH * 1024
PROJ_TILE = 1024

VMEM_LIMIT = 52 * 1024 * 1024


def _cparams(sem):
    return pltpu.CompilerParams(dimension_semantics=sem, vmem_limit_bytes=VMEM_LIMIT)


def _pick_tile(n, cap):
    best = 8
    for t in range(8, min(n, cap) + 1, 8):
        if n % t == 0:
            best = t
    return best


def _full(a):
    return pl.BlockSpec(a.shape, lambda *_: (0,) * a.ndim)


N_SPLIT = 2


def _split(x):
    hi = x.astype(BF16)
    lo = (x - hi.astype(F32)).astype(BF16)
    return hi, lo


def _sigmoid(x):
    return 0.5 * jnp.tanh(0.5 * x) + 0.5


def _arrange_w_in(w_in):
    pieces, pos = [], 0

    def put(dst, block):
        nonlocal pos
        if dst > pos:
            pieces.append(jnp.zeros(w_in.shape[:-1] + (dst - pos,), w_in.dtype))
        pieces.append(block)
        pos = dst + block.shape[-1]

    ref = lambda name: w_in[..., _REF_COLS[name][0]:_REF_COLS[name][0] + _REF_COLS[name][1]]
    for name, dst in (('gdn_qkv', P_GDN_QKV), ('gdn_z', P_GDN_Z), ('lru_x', P_LRU_X), ('lru_y', P_LRU_Y),
                      ('mla_kv', P_MLA_KV), ('mla_kr', P_MLA_KR), ('gdn_beta', P_GDN_BA),
                      ('gdn_a', P_GDN_BA + 2 * GDN_HEADS)):
        put(dst, ref(name))
    put(P32_COLS + P_MLA_Q, ref('mla_q'))
    put(P32_COLS + P_NA_QKV, ref('na_qkv'))
    put(P32_COLS + P_GATES, w_in[..., N_MIX_COLS:])
    assert pos == P32_COLS + P16_COLS
    return jnp.concatenate(pieces, axis=-1).astype(BF16)


def _rope_perm():
    q = MLA_ROPE // 4
    src = np.zeros(MLA_ROPE, np.int32)
    sign = np.zeros(MLA_ROPE, np.float32)
    for base in (0, 2 * q):
        for d in range(q):
            src[base + d] = base + d + q
            sign[base + d] = -1.0
            src[base + q + d] = base + d
            sign[base + q + d] = 1.0
    return src, sign


def _arrange_mla(w_uq, w_ukv):
    src, sign = _rope_perm()
    hq = MLA_NOPE + MLA_ROPE
    wq = jnp.zeros((MLA_Q_RANK, 2 * MLA_HEADS * MLA_SLOT), F32)
    wk = jnp.zeros((MLA_KV_RANK, MLA_HEADS * MLA_SLOT), F32)
    wv = jnp.zeros((MLA_KV_RANK, MLA_HEADS * MLA_V), F32)
    place = np.zeros((2 * MLA_SLOT, 2 * MLA_HEADS * MLA_SLOT), np.float32)
    rot_off = MLA_HEADS * MLA_SLOT
    for h in range(MLA_HEADS):
        nope = w_uq[:, h * hq:h * hq + MLA_NOPE]
        pe = w_uq[:, h * hq + MLA_NOPE:(h + 1) * hq]
        s = h * MLA_SLOT
        wq = wq.at[:, s:s + MLA_NOPE].set(nope)
        wq = wq.at[:, s + MLA_NOPE:s + MLA_NOPE + MLA_ROPE].set(pe)
        wq = wq.at[:, rot_off + s + MLA_NOPE:rot_off + s + MLA_NOPE + MLA_ROPE].set(pe[:, src] * sign)
        wk = wk.at[:, s:s + MLA_NOPE].set(w_ukv[:, h * 128:h * 128 + MLA_NOPE])
        wv = wv.at[:, h * MLA_V:(h + 1) * MLA_V].set(w_ukv[:, h * 128 + MLA_NOPE:(h + 1) * 128])
        for d in range(MLA_ROPE):
            for half in (0, MLA_SLOT):
                place[half + d, s + MLA_NOPE + d] = 1.0
                place[half + src[d], rot_off + s + MLA_NOPE + d] = sign[d]
    return wq.astype(BF16), wk.astype(BF16), wv.astype(BF16), jnp.asarray(place, BF16)


def _rope_tables(n_tok, n_ctx):
    cos = np.ones((n_tok + n_ctx, MLA_SLOT), np.float32)
    sin = np.zeros((n_tok + n_ctx, MLA_SLOT), np.float32)
    t = np.arange(n_tok)
    row = (t // GRID_W).astype(np.float32)
    col = (t % GRID_W).astype(np.float32)
    n_freq = MLA_ROPE // 4
    inv = (ROPE_BASE ** (-np.arange(n_freq, dtype=np.float32) / n_freq)).astype(np.float32)
    ar = row[:, None] * inv
    ac = col[:, None] * inv
    ang = np.concatenate([ar, ar, ac, ac], axis=-1).astype(np.float32)
    cos[:n_tok, MLA_NOPE:MLA_NOPE + MLA_ROPE] = np.cos(ang)
    sin[:n_tok, MLA_NOPE:MLA_NOPE + MLA_ROPE] = np.sin(ang)
    return jnp.asarray(cos), jnp.asarray(sin)


def _head_block_ones(n_heads, width):
    m = np.kron(np.eye(n_heads, dtype=np.float32), np.ones((width, width), np.float32))
    return jnp.asarray(m, BF16)


def _mod_kernel(c_ref, w_ref, b_ref, o_ref):
    c = c_ref[...]
    s = c * _sigmoid(c)
    o_ref[...] = jnp.dot(s, w_ref[...], preferred_element_type=F32) + b_ref[...]


def _modulation(cc, mod_w, mod_b, layer):
    r, d = cc.shape
    n = mod_w.shape[-1]
    tn = 1024
    return pl.pallas_call(
        _mod_kernel,
        grid=(n // tn,),
        in_specs=[pl.BlockSpec((r, d), lambda j: (0, 0)),
                  pl.BlockSpec((None, d, tn), lambda j: (layer, 0, j)),
                  pl.BlockSpec((1, tn), lambda j: (0, j))],
        out_specs=pl.BlockSpec((r, tn), lambda j: (0, j)),
        out_shape=jax.ShapeDtypeStruct((r, n), F32),
        compiler_params=_cparams(("arbitrary",)),
        name="modulation",
    )(cc, mod_w, mod_b.reshape(1, n))


def _row_select(tile_rows, tile_idx, n_lat, ctx_vec, lat_vec):
    row = tile_idx * tile_rows + lax.broadcasted_iota(jnp.int32, (tile_rows, 1), 0)
    return jnp.where(row < n_lat, lat_vec, ctx_vec)


def _norm_modulate(x_ref, h_ref, gain_ref, mc_ref, ml_ref, shift_row, tile_idx, n_lat):
    tm = x_ref.shape[0]
    x = x_ref[...]
    xn = x * lax.rsqrt(jnp.mean(x * x, axis=-1, keepdims=True) + RMS_EPS)
    sh_l, sh_c = ml_ref[shift_row:shift_row + 1, :], mc_ref[shift_row:shift_row + 1, :]
    amp_l = gain_ref[...] * (1.0 + ml_ref[shift_row + 1:shift_row + 2, :])
    amp_c = gain_ref[...] * (1.0 + mc_ref[shift_row + 1:shift_row + 2, :])
    all_latent = (tile_idx + 1) * tm <= n_lat

    @pl.when(all_latent)
    def _():
        h_ref[...] = (xn * amp_l + sh_l).astype(h_ref.dtype)

    @pl.when(jnp.logical_not(all_latent))
    def _():
        amp = _row_select(tm, tile_idx, n_lat, amp_c, amp_l)
        shift = _row_select(tm, tile_idx, n_lat, sh_c, sh_l)
        h_ref[...] = (xn * amp + shift).astype(h_ref.dtype)


def _inproj_kernel(x_ref, mc_ref, ml_ref, g_ref, w_ref, o32_ref, o16_ref, h_ref, *, n_lat):
    j = pl.program_id(2)
    n32 = P32_COLS // PROJ_TILE

    @pl.when(j == 0)
    def _():
        _norm_modulate(x_ref, h_ref, g_ref, mc_ref, ml_ref, 0, pl.program_id(1), n_lat)

    acc = jnp.dot(h_ref[...], w_ref[...], preferred_element_type=F32)

    @pl.when(j < n32)
    def _():
        o32_ref[...] = acc

    @pl.when(j >= n32)
    def _():
        o16_ref[...] = acc.astype(BF16)


def _inproj(xs, mod_c, mod_l, gain, w, layer, n_lat):
    b, s, d = xs.shape
    tm = _pick_tile(s, 1088)
    tn = PROJ_TILE
    n32 = P32_COLS // tn
    return pl.pallas_call(
        functools.partial(_inproj_kernel, n_lat=n_lat),
        grid=(b, s // tm, (P32_COLS + P16_COLS) // tn),
        in_specs=[pl.BlockSpec((None, tm, d), lambda bi, i, j: (bi, i, 0)),
                  pl.BlockSpec((8, d), lambda bi, i, j: (0, 0)),
                  pl.BlockSpec((None, 8, d), lambda bi, i, j: (bi, 0, 0)),
                  pl.BlockSpec((1, d), lambda bi, i, j: (0, 0)),
                  pl.BlockSpec((None, d, tn), lambda bi, i, j: (layer, 0, j))],
        out_specs=[pl.BlockSpec((None, tm, tn), lambda bi, i, j: (bi, i, jnp.minimum(j, n32 - 1))),
                   pl.BlockSpec((None, tm, tn), lambda bi, i, j: (bi, i, jnp.maximum(j - n32, 0)))],
        out_shape=[jax.ShapeDtypeStruct((b, s, P32_COLS), F32), jax.ShapeDtypeStruct((b, s, P16_COLS), BF16)],
        scratch_shapes=[pltpu.VMEM((tm, d), BF16)],
        compiler_params=_cparams(("parallel", "parallel", "arbitrary")),
        name="inproj",
    )(xs, mod_c, mod_l, gain, w)


def _mla_prep_kernel(ql_ref, kv_ref, cos_ref, sin_ref, gq_ref, gkv_ref, wq_ref, wk_ref, wv_ref, pl_ref,
                     q_ref, k_ref, v_ref, *, scale):
    nslot = MLA_HEADS * MLA_SLOT
    cos = jnp.concatenate([cos_ref[...]] * MLA_HEADS, axis=-1)
    sin = jnp.concatenate([sin_ref[...]] * MLA_HEADS, axis=-1)

    ql = ql_ref[...].astype(F32)
    qn = ql * lax.rsqrt(jnp.mean(ql * ql, axis=-1, keepdims=True) + RMS_EPS) * gq_ref[...]
    q2 = jnp.dot(qn.astype(BF16), wq_ref[...], preferred_element_type=F32)
    q = (q2[:, :nslot] * cos + q2[:, nslot:] * sin) * scale
    q_ref[...] = q.astype(BF16)

    kvkr = kv_ref[...]
    kvl = kvkr[:, :MLA_KV_RANK]
    kr = kvkr[:, MLA_KV_RANK:]
    kvn = (kvl * lax.rsqrt(jnp.mean(kvl * kvl, axis=-1, keepdims=True) + RMS_EPS) * gkv_ref[...]).astype(BF16)
    kn = jnp.dot(kvn, wk_ref[...], preferred_element_type=F32)
    v_ref[...] = jnp.dot(kvn, wv_ref[...], preferred_element_type=F32).astype(BF16)
    kr_hi = kr.astype(BF16)
    kr_lo = (kr - kr_hi.astype(F32)).astype(BF16)
    kr2 = jnp.dot(jnp.concatenate([kr_hi, kr_lo], axis=-1), pl_ref[...], preferred_element_type=F32)
    k_ref[...] = (kn + kr2[:, :nslot] * cos + kr2[:, nslot:] * sin).astype(BF16)


def _mla_prep(p32, p16, cos, sin, gq, gkv, wq, wk, wv, place):
    b, s, _ = p32.shape
    tm = _pick_tile(s, 544)
    nslot = MLA_HEADS * MLA_SLOT
    scale = (MLA_NOPE + MLA_ROPE) ** -0.5 * LOG2_E
    return pl.pallas_call(
        functools.partial(_mla_prep_kernel, scale=scale),
        grid=(b, s // tm),
        in_specs=[pl.BlockSpec((None, tm, MLA_Q_RANK), lambda bi, i: (bi, i, P_MLA_Q // MLA_Q_RANK)),
                  pl.BlockSpec((None, tm, 2 * MLA_SLOT), lambda bi, i: (bi, i, P_MLA_KV // (2 * MLA_SLOT))),
                  pl.BlockSpec((tm, MLA_SLOT), lambda bi, i: (i, 0)),
                  pl.BlockSpec((tm, MLA_SLOT), lambda bi, i: (i, 0)),
                  _full(gq), _full(gkv), _full(wq), _full(wk), _full(wv), _full(place)],
        out_specs=[pl.BlockSpec((None, tm, nslot), lambda bi, i: (bi, i, 0)),
                   pl.BlockSpec((None, tm, nslot), lambda bi, i: (bi, i, 0)),
                   pl.BlockSpec((None, tm, MLA_HEADS * MLA_V), lambda bi, i: (bi, i, 0))],
        out_shape=[jax.ShapeDtypeStruct((b, s, nslot), BF16),
                   jax.ShapeDtypeStruct((b, s, nslot), BF16),
                   jax.ShapeDtypeStruct((b, s, MLA_HEADS * MLA_V), BF16)],
        compiler_params=_cparams(("parallel", "parallel")),
        name="mla_prep",
    )(p16, p32, cos, sin, gq, gkv, wq, wk, wv, place)


def _flash_softmax(h, s, m_ref, l_ref):
    m_prev = m_ref[h]
    m_new = jnp.maximum(m_prev, jnp.max(s, axis=-1, keepdims=True))
    alpha = jnp.exp2(m_prev - m_new)
    p = jnp.exp2(s - jnp.concatenate([m_new] * (s.shape[1] // 128), axis=-1))
    l_ref[h] = alpha * l_ref[h] + jnp.sum(p, axis=-1, keepdims=True)
    m_ref[h] = m_new
    return alpha, p.astype(BF16)


def _mla_flash_kernel(*refs, tk, aliased):
    if aliased:
        q_ref, k_ref, v_ref, _, o_ref, m_ref, l_ref, acc_ref = refs
    else:
        q_ref, k_ref, v_ref, o_ref, m_ref, l_ref, acc_ref = refs
    tq = q_ref.shape[0]
    nk = k_ref.shape[0]
    n_loop = nk // tk
    m_ref[...] = jnp.full(m_ref.shape, -jnp.inf, F32)
    l_ref[...] = jnp.zeros(l_ref.shape, F32)
    acc_ref[...] = jnp.zeros(acc_ref.shape, F32)

    def chunk(rows):
        heads = range(MLA_HEADS)
        hs = [slice(h * MLA_SLOT, (h + 1) * MLA_SLOT) for h in heads]
        vs = [slice((h // 2) * 128, (h // 2) * 128 + 128) for h in heads]
        s = [lax.dot_general(q_ref[:, hs[h]], k_ref[rows, hs[h]], (((1,), (1,)), ((), ())),
                             preferred_element_type=F32) for h in heads]
        ap = [_flash_softmax(h, s[h], m_ref, l_ref) for h in heads]
        for h in heads:
            alpha, p = ap[h]
            acc_ref[h] = alpha * acc_ref[h] + jnp.dot(p, v_ref[rows, vs[h]], preferred_element_type=F32)

    if n_loop:
        def body(j, carry):
            chunk(pl.ds(pl.multiple_of(j * tk, tk), tk))
            return carry
        lax.fori_loop(0, n_loop, body, 0)
    if nk > n_loop * tk:
        chunk(slice(n_loop * tk, nk))

    lane = lax.broadcasted_iota(jnp.int32, (tq, 128), 1)
    outs = []
    for pair in range(MLA_HEADS // 2):
        o0 = acc_ref[2 * pair] / l_ref[2 * pair]
        o1 = acc_ref[2 * pair + 1] / l_ref[2 * pair + 1]
        outs.append(jnp.where(lane < MLA_V, o0, o1))
    o_ref[...] = jnp.concatenate(outs, axis=-1).astype(o_ref.dtype)


def _mla_flash(q, k, v, n_lat, prev=None):
    b, s, nslot = q.shape
    n_ctx = s - n_lat
    nv = MLA_HEADS * MLA_V
    tk = 512
    if prev is None:
        tq = _pick_tile(n_lat, 1024)
        q_off, n_q, kv_rows, kv_blk = 0, n_lat // tq, s, 0
    else:
        assert n_lat % n_ctx == 0
        tq = _pick_tile(n_ctx, 256)
        q_off, n_q, kv_rows, kv_blk = n_lat // tq, n_ctx // tq, n_ctx, n_lat // n_ctx
    in_specs = [pl.BlockSpec((None, tq, nslot), lambda bi, i: (bi, i + q_off, 0)),
                pl.BlockSpec((None, kv_rows, nslot), lambda bi, i: (bi, kv_blk, 0)),
                pl.BlockSpec((None, kv_rows, nv), lambda bi, i: (bi, kv_blk, 0))]
    args = [q, k, v]
    aliases = {}
    if prev is not None:
        in_specs.append(pl.BlockSpec(memory_space=pl.ANY))
        args.append(prev)
        aliases = {3: 0}
    return pl.pallas_call(
        functools.partial(_mla_flash_kernel, tk=tk, aliased=prev is not None),
        grid=(b, n_q),
        in_specs=in_specs,
        out_specs=pl.BlockSpec((None, tq, nv), lambda bi, i: (bi, i + q_off, 0)),
        out_shape=jax.ShapeDtypeStruct((b, s, nv), BF16),
        scratch_shapes=[pltpu.VMEM((MLA_HEADS, tq, 128), F32),
                        pltpu.VMEM((MLA_HEADS, tq, 128), F32),
                        pltpu.VMEM((MLA_HEADS, tq, 128), F32)],
        input_output_aliases=aliases,
        compiler_params=_cparams(("parallel", "arbitrary")),
        name="mla_flash",
    )(*args)


NA_QROWS = 4
NA_SLAB = NA_QROWS + NA_WIN_ROWS


def _na_geometry(rows):
    assert rows % NA_QROWS == 0 and rows >= NA_SLAB
    nblk = rows // NA_QROWS
    qc = np.arange(GRID_W)
    cs = np.clip(qc - NA_WIN_COLS // 2, 0, GRID_W - NA_WIN_COLS)
    col_valid = (qc[None, :] >= cs[:, None]) & (qc[None, :] < cs[:, None] + NA_WIN_COLS)
    rel_c = np.clip(qc[None, :] - qc[:, None] + NA_WIN_COLS - 1, 0, 2 * NA_WIN_COLS - 2)
    onehot_c = (rel_c[None] == np.arange(2 * NA_WIN_COLS - 1)[:, None, None]) & col_valid[None]
    patterns, cls, starts = {}, [], []
    for i in range(nblk):
        r0 = i * NA_QROWS
        start = int(np.clip(r0 - NA_WIN_ROWS // 2, 0, rows - NA_SLAB))
        qr = r0 + np.arange(NA_QROWS)
        rs = np.clip(qr - NA_WIN_ROWS // 2, 0, rows - NA_WIN_ROWS)
        key = (start - r0,) + tuple((rs - r0).tolist())
        if key not in patterns:
            kr = start + np.arange(NA_SLAB)
            row_valid = (kr[None, :] >= rs[:, None]) & (kr[None, :] < rs[:, None] + NA_WIN_ROWS)
            rel_r = np.clip(kr[None, :] - qr[:, None] + NA_WIN_ROWS - 1, 0, 2 * NA_WIN_ROWS - 2)
            patterns[key] = (len(patterns), np.where(row_valid, rel_r, -1))
        cls.append(patterns[key][0])
        starts.append(start)
    ordered = sorted(patterns.values(), key=lambda z: z[0])
    rel_rows = np.stack([z[1] for z in ordered])
    meta = np.stack([np.asarray(cls, np.int32), np.asarray(starts, np.int32)])
    return meta, rel_rows, (onehot_c.astype(np.float32), col_valid)


def _na_bias_table(rpb, rel_rows, col_tables):
    onehot_c, col_valid = col_tables
    bc = jnp.einsum('hrc,cqk->hrqk', rpb.astype(F32), onehot_c, precision=lax.Precision.HIGHEST)
    bc = jnp.where(col_valid, bc, -jnp.inf)
    outside = jnp.full(bc.shape[:1] + bc.shape[2:], -jnp.inf, F32)
    pats = []
    for pat in rel_rows:
        qrows = [jnp.concatenate([bc[:, r] if r >= 0 else outside for r in row], axis=-1) for row in pat]
        pats.append(jnp.concatenate(qrows, axis=1))
    return jnp.stack(pats)


def _na_scores(q_pair, lane, h, k_parts):
    in_head = (lane < NA_DH) if h % 2 == 0 else (lane >= NA_DH)
    qm = jnp.where(in_head, q_pair * (NA_DH ** -0.5), 0.0).astype(BF16)
    return [lax.dot_general(qm, k, (((1,), (1,)), ((), ())), preferred_element_type=F32) for k in k_parts]


def _na_softmax(scores, bias):
    if bias is not None:
        scores = [scores[0] + bias] + scores[1:]
    m = scores[0].max(axis=-1, keepdims=True)
    for s in scores[1:]:
        m = jnp.maximum(m, s.max(axis=-1, keepdims=True))
    probs = [jnp.exp(s - m) for s in scores]
    den = probs[0].sum(axis=-1, keepdims=True)
    for p in probs[1:]:
        den = den + p.sum(axis=-1, keepdims=True)
    return [p.astype(BF16) for p in probs], den


def _na_kernel(meta_ref, q_ref, k_ref, v_ref, bias_ref, o_ref, *, n_lat, with_ctx):
    i = pl.program_id(1)
    nq = q_ref.shape[0]
    n_lat_tiles = n_lat // nq
    n_all = k_ref.shape[0]
    lane = lax.broadcasted_iota(jnp.int32, (nq, 128), 1)

    def run(windowed):
        key_rows = [pl.ds(n_lat, n_all - n_lat)]
        if windowed:
            start = pl.multiple_of(meta_ref[1, jnp.minimum(i, n_lat_tiles - 1)] * GRID_W, GRID_W)
            key_rows = [pl.ds(start, NA_SLAB * GRID_W)] + key_rows
        pair_lanes = [slice((h // 2) * 128, (h // 2 + 1) * 128) for h in range(NA_HEADS)]
        scores = [_na_scores(q_ref[:, pair_lanes[h]], lane, h, [k_ref[r, pair_lanes[h]] for r in key_rows])
                  for h in range(NA_HEADS)]
        soft = [_na_softmax(scores[h], bias_ref[h] if windowed else None) for h in range(NA_HEADS)]
        heads = []
        for h in range(NA_HEADS):
            probs, den = soft[h]
            out = None
            for p, r in zip(probs, key_rows):
                po = jnp.dot(p, v_ref[r, pair_lanes[h]], preferred_element_type=F32)
                out = po if out is None else out + po
            heads.append(out / den)
        outs = [jnp.where(lane < NA_DH, heads[2 * pair], heads[2 * pair + 1]) for pair in range(NA_HEADS // 2)]
        o_ref[...] = jnp.concatenate(outs, axis=-1).astype(o_ref.dtype)

    if with_ctx:
        pl.when(i < n_lat_tiles)(lambda: run(True))
        pl.when(i >= n_lat_tiles)(lambda: run(False))
    else:
        run(True)


def _na_attention(p, meta, table, n_lat, with_ctx_queries):
    b, s, _ = p.shape
    nq = NA_QROWS * GRID_W
    nk = NA_SLAB * GRID_W
    assert (s - n_lat) % nq == 0
    nlt = n_lat // nq
    qb = P_NA_QKV // NA_W
    grid_spec = pltpu.PrefetchScalarGridSpec(
        num_scalar_prefetch=1,
        grid=(b, s // nq if with_ctx_queries else nlt),
        in_specs=[pl.BlockSpec((None, nq, NA_W), lambda bi, i, m: (bi, i, qb)),
                  pl.BlockSpec((None, s, NA_W), lambda bi, i, m: (bi, 0, qb + 1)),
                  pl.BlockSpec((None, s, NA_W), lambda bi, i, m: (bi, 0, qb + 2)),
                  pl.BlockSpec((None, NA_HEADS, nq, nk),
                               lambda bi, i, m: (m[0, jnp.minimum(i, nlt - 1)], 0, 0, 0))],
        out_specs=pl.BlockSpec((None, nq, NA_W), lambda bi, i, m: (bi, i, 0)))
    return pl.pallas_call(
        functools.partial(_na_kernel, n_lat=n_lat, with_ctx=with_ctx_queries),
        grid_spec=grid_spec,
        out_shape=jax.ShapeDtypeStruct((b, s, NA_W), BF16),
        compiler_params=_cparams(("parallel", "arbitrary")),
        name="na_attention",
    )(meta, p, p, p, table)


def _tile_conv(x, prev, nxt, w_ref, tile_idx, n_tiles, n_lat_tiles):
    r = x.shape[0]
    width = w_ref.shape[0]
    left = width // 2
    has_prev = jnp.logical_and(tile_idx != 0, tile_idx != n_lat_tiles)
    has_next = jnp.logical_and(tile_idx != n_tiles - 1, tile_idx != n_lat_tiles - 1)
    prev = jnp.where(has_prev, prev, 0.0)
    nxt = jnp.where(has_next, nxt, 0.0)
    xe = jnp.concatenate([prev, x, nxt], axis=0)
    acc = None
    for j in range(width):
        o = HALO - left + j
        term = xe[o:o + r, :] * w_ref[j:j + 1, :]
        acc = term if acc is None else acc + term
    return acc


def _halo_specs(width, col_block, tile_of):
    per = SEQ_TILE // HALO

    def main(bi, s, *_):
        return (bi, tile_of(s), col_block)

    def prev(bi, s, *_):
        return (bi, jnp.maximum(tile_of(s) * per - 1, 0), col_block)

    def make_next(n_tiles):
        def nxt(bi, s, *_):
            return (bi, jnp.minimum((tile_of(s) + 1) * per, n_tiles * per - 1), col_block)
        return nxt

    return main, prev, make_next


def _gdn_prep_kernel(x_ref, xp_ref, xn_ref, ba_ref, cw_ref, ones_ref, exp_ref, alog_ref, dtb_ref,
                     q_ref, k_ref, v_ref, beta_ref, g_ref, *, n_lat_tiles):
    i = pl.program_id(1)
    y = _tile_conv(x_ref[...], xp_ref[...], xn_ref[...], cw_ref, i, pl.num_programs(1), n_lat_tiles)
    y = y * _sigmoid(y)
    q = y[:, :GDN_W]
    k = y[:, GDN_W:2 * GDN_W]
    v_ref[...] = y[:, 2 * GDN_W:].astype(v_ref.dtype)

    def head_norm(u):
        parts = jnp.concatenate(_split(u * u), axis=-1)
        ss = jnp.dot(parts, ones_ref[...], preferred_element_type=F32)
        return u * lax.rsqrt(ss + RMS_EPS)

    q_ref[...] = (head_norm(q) * (GDN_DK ** -0.5)).astype(q_ref.dtype)
    k_ref[...] = head_norm(k).astype(k_ref.dtype)

    ba = ba_ref[...]
    a = ba + dtb_ref[...]
    softplus = jnp.maximum(a, 0.0) + jnp.log1p(jnp.exp(-jnp.abs(a)))
    lane = lax.broadcasted_iota(jnp.int32, ba.shape, 1)
    compact = jnp.where(lane < 2 * GDN_HEADS, _sigmoid(ba), -jnp.exp(alog_ref[...]) * softplus)
    wide = jnp.dot(jnp.concatenate(_split(compact), axis=-1), exp_ref[...], preferred_element_type=F32)
    half = 2 * GDN_W
    beta_ref[...] = wide[:, :half].astype(beta_ref.dtype)
    g_ref[...] = wide[:, half:]


def _gdn_prep(p, conv_w, a_log, dt_bias, n_lat):
    b, s, _ = p.shape
    n_tiles = s // SEQ_TILE
    main, prev, make_next = _halo_specs(3 * GDN_W, 0, lambda t: t)
    ones_n = jnp.concatenate([_head_block_ones(GDN_HEADS, GDN_DK)] * N_SPLIT, axis=0)
    expand = np.zeros((128, 4 * GDN_W), np.float32)
    for kind in range(2):
        for d in range(2):
            for h in range(GDN_HEADS):
                c0 = kind * 2 * GDN_W + d * GDN_W + h * GDN_DV
                expand[kind * 2 * GDN_HEADS + d * GDN_HEADS + h, c0:c0 + GDN_DV] = 1.0
    expand_n = jnp.asarray(np.concatenate([expand] * N_SPLIT, axis=0), BF16)
    lanes = jnp.zeros((1, 128), F32)
    alog_e = lanes.at[0, 2 * GDN_HEADS:4 * GDN_HEADS].set(a_log.astype(F32).reshape(-1))
    dtb_e = lanes.at[0, 2 * GDN_HEADS:4 * GDN_HEADS].set(dt_bias.astype(F32).reshape(-1))
    tok = lambda w: pl.BlockSpec((None, SEQ_TILE, w), lambda bi, i: (bi, i, 0))
    return pl.pallas_call(
        functools.partial(_gdn_prep_kernel, n_lat_tiles=n_lat // SEQ_TILE),
        grid=(b, n_tiles),
        in_specs=[pl.BlockSpec((None, SEQ_TILE, 3 * GDN_W), main),
                  pl.BlockSpec((None, HALO, 3 * GDN_W), prev),
                  pl.BlockSpec((None, HALO, 3 * GDN_W), make_next(n_tiles)),
                  pl.BlockSpec((None, SEQ_TILE, 128), lambda bi, i: (bi, i, P_GDN_BA // 128)),
                  _full(conv_w), _full(ones_n), _full(expand_n), _full(alog_e), _full(dtb_e)],
        out_specs=[tok(GDN_W), tok(GDN_W), tok(GDN_W), tok(2 * GDN_W), tok(2 * GDN_W)],
        out_shape=[jax.ShapeDtypeStruct((b, s, GDN_W), BF16)] * 3 + [jax.ShapeDtypeStruct((b, s, 2 * GDN_W), BF16),
                                                                    jax.ShapeDtypeStruct((b, s, 2 * GDN_W), F32)],
        compiler_params=_cparams(("parallel", "parallel")),
        name="gdn_prep",
    )(p, p, p, p, conv_w, ones_n, expand_n, alog_e, dtb_e)


GDN_PAIR = 2 * GDN_DK


def _gdn_masks():
    c, w = GDN_CHUNK, GDN_W
    r2, c2 = np.arange(GDN_PAIR)[:, None], np.arange(GDN_PAIR)[None, :]
    bd = ((r2 // c) == (c2 // c)).astype(np.float32)
    i = np.arange(c)[:, None]
    j = (np.arange(w) % c)[None, :]
    level = np.zeros((c, w), np.int32)
    for bit in range(6):
        level += ((i ^ j) >= (1 << bit)).astype(np.int32)
    lvl = np.stack([(level == m).astype(np.float32) for m in range(7)])
    dirm = np.stack([np.stack([(j <= i), (j < i)]), np.stack([(j >= i), (j > i)])]).astype(np.float32)
    tj = (np.arange(N_SPLIT * c) % c)[None, :]
    tri = np.stack([(tj <= i), (tj >= i)]).astype(np.float32)
    return jnp.asarray(bd, BF16), jnp.asarray(lvl), jnp.asarray(dirm), jnp.asarray(tri, BF16)


def _heads_mm(x, y, bd, transpose_rhs=False):
    xb = x.astype(BF16)
    yb = y.astype(BF16)
    outs = []
    for pair in range(GDN_W // GDN_PAIR):
        ls = slice(pair * GDN_PAIR, (pair + 1) * GDN_PAIR)
        w = jnp.concatenate([yb[:, ls], yb[:, ls]], axis=0) * bd
        dims = (((1,), (1,)), ((), ())) if transpose_rhs else (((1,), (0,)), ((), ()))
        outs.append(lax.dot_general(xb[:, ls], w, dims, preferred_element_type=F32))
    return jnp.concatenate(outs, axis=1)


def _gdn_intra(probs, bd, lvl_ref, dirm_ref, tri_ref):
    c = GDN_CHUNK
    n = len(probs)
    eye = lvl_ref[0]
    gc, g_last, decay, gram = [], [], [], []
    for q, k, v, beta, g, rev in probs:
        d = 1 if rev else 0
        gcp = jnp.dot(tri_ref[d], jnp.concatenate(_split(g), axis=0), preferred_element_type=F32)
        gc.append(gcp)
        g_last.append(gcp[0:1, :] if rev else gcp[c - 1:c, :])
        gc_row = jnp.sum(gcp * eye, axis=0, keepdims=True)
        decay.append(dirm_ref[d, 0] * jnp.exp(jnp.minimum(gcp - gc_row, 0.0)))
        gram.append(_heads_mm(jnp.concatenate([k, q], axis=0), k, bd, transpose_rhs=True))
    lower = [dirm_ref[1 if p[5] else 0, 1] * p[3] * gram[x][:c] * decay[x] for x, p in enumerate(probs)]
    a_intra = [gram[x][c:] * decay[x] for x in range(n)]
    t = [eye - lower[x] * lvl_ref[1] for x in range(n)]
    for lev in range(2, 7):
        y = [_heads_mm(t[x], lower[x] * lvl_ref[lev], bd) for x in range(n)]
        z = [_heads_mm(y[x], t[x], bd) for x in range(n)]
        t = [t[x] - z[x] for x in range(n)]
    e_gc = [jnp.exp(gc[x]) for x in range(n)]
    u = [_heads_mm(t[x], p[2] * p[3], bd) for x, p in enumerate(probs)]
    w = [_heads_mm(t[x], p[1] * p[3] * e_gc[x], bd) for x, p in enumerate(probs)]
    wq = [jnp.concatenate([w[x], p[0] * e_gc[x]], axis=0).astype(BF16) for x, p in enumerate(probs)]
    k_dec = [(p[1] * jnp.exp(g_last[x] - gc[x])).astype(BF16) for x, p in enumerate(probs)]
    g_tot = [jnp.exp(g_last[x]) for x in range(n)]
    return u, wq, k_dec, a_intra, g_tot


def _gdn_state_steps(items, bd):
    c = GDN_CHUNK
    pairs = [slice(p * GDN_PAIR, (p + 1) * GDN_PAIR) for p in range(GDN_W // GDN_PAIR)]
    bdf = bd.astype(F32)
    ws_qs = [jnp.concatenate([jnp.dot(wq[:, ls], s_ref[p].astype(BF16), preferred_element_type=F32)
                              for p, ls in enumerate(pairs)], axis=1)
             for _, wq, _, _, _, s_ref in items]
    v_new = [it[0] - ws[:c] for it, ws in zip(items, ws_qs)]
    outs = [ws[c:] + _heads_mm(it[3], vn, bd) for it, ws, vn in zip(items, ws_qs, v_new)]
    for (_, _, k_dec, _, g_tot, s_ref), vn in zip(items, v_new):
        vb = vn.astype(BF16)
        for p, ls in enumerate(pairs):
            kv = lax.dot_general(k_dec[:, ls], vb[:, ls], (((0,), (0,)), ((), ())), preferred_element_type=F32)
            s_ref[p] = s_ref[p] * g_tot[:, ls] + kv * bdf
    return outs


def _gdn_scan_kernel(qf, kf, vf, bf, gf, qb, kb, vb, bb, gb, bd_ref, lvl_ref, dirm_ref, tri_ref,
                     of_ref, ob_ref, sf_ref, sb_ref):
    @pl.when(pl.program_id(1) == 0)
    def _():
        sf_ref[...] = jnp.zeros(sf_ref.shape, F32)
        sb_ref[...] = jnp.zeros(sb_ref.shape, F32)

    bd = bd_ref[...]
    n = SEQ_TILE // GDN_CHUNK
    probs, rows = [], []
    for c in range(n):
        rf = slice(c * GDN_CHUNK, (c + 1) * GDN_CHUNK)
        rb = slice((n - 1 - c) * GDN_CHUNK, (n - c) * GDN_CHUNK)
        probs.append(tuple(r[rf, :].astype(F32) for r in (qf, kf, vf, bf, gf)) + (False,))
        probs.append(tuple(r[rb, :].astype(F32) for r in (qb, kb, vb, bb, gb)) + (True,))
        rows += [rf, rb]
    u, wq, k_dec, a_intra, g_tot = _gdn_intra(probs, bd, lvl_ref, dirm_ref, tri_ref)
    for c in range(n):
        xs = (2 * c, 2 * c + 1)
        outs = _gdn_state_steps([(u[x], wq[x], k_dec[x], a_intra[x], g_tot[x], sb_ref if probs[x][5] else sf_ref)
                                 for x in xs], bd)
        for x, o in zip(xs, outs):
            (ob_ref if probs[x][5] else of_ref)[rows[x], :] = o


def _seq_tile_maps(n_lat_tiles, n_tiles):
    fwd = lambda s: lax.rem(s + n_lat_tiles, n_tiles)
    bwd = lambda s: n_tiles - 1 - s
    return fwd, bwd


def _gdn_scan(q, k, v, beta, g, n_lat):
    b, s, _ = q.shape
    n_tiles = s // SEQ_TILE
    fwd, bwd = _seq_tile_maps(n_lat // SEQ_TILE, n_tiles)
    spec = lambda tile_of, col: pl.BlockSpec((None, SEQ_TILE, GDN_W), lambda bi, t: (bi, tile_of(t), col))
    masks = _gdn_masks()
    in_specs = [spec(fwd, 0)] * 5 + [spec(bwd, 0)] * 3 + [spec(bwd, 1)] * 2 + [_full(m) for m in masks]
    return pl.pallas_call(
        _gdn_scan_kernel,
        grid=(b, n_tiles),
        in_specs=in_specs,
        out_specs=[spec(fwd, 0), spec(bwd, 0)],
        out_shape=[jax.ShapeDtypeStruct((b, s, GDN_W), F32)] * 2,
        scratch_shapes=[pltpu.VMEM((GDN_W // GDN_PAIR, GDN_PAIR, GDN_PAIR), F32)] * 2,
        compiler_params=_cparams(("parallel", "arbitrary")),
        name="gdn_scan",
    )(q, k, v, beta, g, q, k, v, beta, g, *masks)


def _lru_tile_prep(x_ref, xp_ref, xn_ref, cw_ref, cb_ref, wg_ref, bg_ref, nla_ref, a_ref, b_ref,
                   tile_idx, n_tiles, n_lat_tiles):
    xb = _tile_conv(x_ref[...], xp_ref[...], xn_ref[...], cw_ref, tile_idx, n_tiles, n_lat_tiles) + cb_ref[...]
    gates = _sigmoid(jnp.dot(xb.astype(BF16), wg_ref[...], preferred_element_type=F32) + bg_ref[...])
    log_a = nla_ref[...] * gates[:, :LRU_W]
    a_ref[...] = jnp.exp(log_a)
    th = jnp.tanh(log_a)
    b_ref[...] = jnp.sqrt(-2.0 * th / (1.0 - th)) * gates[:, LRU_W:] * xb


def _scan_group(a, b, h, row, reverse):
    for d in (1, 2, 4):
        if reverse:
            keep = row < 8 - d
            shift = 8 - d
        else:
            keep = row >= d
            shift = d
        a_s = jnp.where(keep, pltpu.roll(a, shift, 0), 1.0)
        b_s = jnp.where(keep, pltpu.roll(b, shift, 0), 0.0)
        b = a * b_s + b
        a = a * a_s
    return a * h + b


def _lru_scan_kernel(xf, xfp, xfn, xb, xbp, xbn, cw_ref, cb_ref, wgf, bgf, nlaf, wgb, bgb, nlab,
                     hf_ref, hb_ref, af_ref, bf_ref, ab_ref, bb_ref, cf_ref, cbk_ref, *, n_lat_tiles):
    s = pl.program_id(1)
    n_tiles = pl.num_programs(1)

    @pl.when(s == 0)
    def _():
        cf_ref[...] = jnp.zeros(cf_ref.shape, F32)
        cbk_ref[...] = jnp.zeros(cbk_ref.shape, F32)

    t_f, t_b = (m(s) for m in _seq_tile_maps(n_lat_tiles, n_tiles))
    _lru_tile_prep(xf, xfp, xfn, cw_ref, cb_ref, wgf, bgf, nlaf, af_ref, bf_ref, t_f, n_tiles, n_lat_tiles)
    _lru_tile_prep(xb, xbp, xbn, cw_ref, cb_ref, wgb, bgb, nlab, ab_ref, bb_ref, t_b, n_tiles, n_lat_tiles)

    n_groups = SEQ_TILE // 8
    row = lax.broadcasted_iota(jnp.int32, (8, LRU_W), 0)

    def body(gi, carry):
        h_f, h_b = carry
        rf = pl.ds(pl.multiple_of(gi * 8, 8), 8)
        rb = pl.ds(pl.multiple_of((n_groups - 1 - gi) * 8, 8), 8)
        out_f = _scan_group(af_ref[rf, :], bf_ref[rf, :], h_f, row, False)
        out_b = _scan_group(ab_ref[rb, :], bb_ref[rb, :], h_b, row, True)
        hf_ref[rf, :] = out_f
        hb_ref[rb, :] = out_b
        return (jnp.broadcast_to(out_f[7:8, :], (8, LRU_W)), jnp.broadcast_to(out_b[0:1, :], (8, LRU_W)))

    h_f, h_b = lax.fori_loop(0, n_groups, body, (cf_ref[...], cbk_ref[...]), unroll=4)
    cf_ref[...] = h_f
    cbk_ref[...] = h_b


def _lru_scan(p, conv_w, conv_b, w_r, b_r, w_i, b_i, lam, n_lat):
    b, s, _ = p.shape
    n_tiles = s // SEQ_TILE
    nlt = n_lat // SEQ_TILE
    fwd, bwd = _seq_tile_maps(nlt, n_tiles)
    col = P_LRU_X // LRU_W
    specs = []
    for tile_of in (fwd, bwd):
        main, prev, make_next = _halo_specs(LRU_W, col, tile_of)
        specs += [pl.BlockSpec((None, SEQ_TILE, LRU_W), main), pl.BlockSpec((None, HALO, LRU_W), prev),
                  pl.BlockSpec((None, HALO, LRU_W), make_next(n_tiles))]

    def blockdiag(w):
        return jax.scipy.linalg.block_diag(*[w[n] for n in range(LRU_BLOCKS)])

    dir_args = []
    for d in range(2):
        wg = jnp.concatenate([blockdiag(w_r[d]), blockdiag(w_i[d])], axis=1).astype(BF16)
        bg = jnp.concatenate([b_r[d], b_i[d]]).astype(F32).reshape(1, 2 * LRU_W)
        nla = (-LRU_C * jax.nn.softplus(-lam[d].astype(F32))).reshape(1, LRU_W)
        dir_args += [wg, bg, nla]
    cb2 = conv_b.reshape(1, LRU_W)
    out_spec = lambda tile_of: pl.BlockSpec((None, SEQ_TILE, LRU_W), lambda bi, t: (bi, tile_of(t), 0))
    return pl.pallas_call(
        functools.partial(_lru_scan_kernel, n_lat_tiles=nlt),
        grid=(b, n_tiles),
        in_specs=specs + [_full(conv_w), _full(cb2)] + [_full(a) for a in dir_args],
        out_specs=[out_spec(fwd), out_spec(bwd)],
        out_shape=[jax.ShapeDtypeStruct((b, s, LRU_W), F32)] * 2,
        scratch_shapes=[pltpu.VMEM((SEQ_TILE, LRU_W), F32)] * 4 + [pltpu.VMEM((8, LRU_W), F32)] * 2,
        compiler_params=_cparams(("parallel", "arbitrary")),
        name="lru_scan",
    )(p, p, p, p, p, p, conv_w, cb2, *dir_args)


def _post_kernel(of_ref, ob_ref, z_ref, gn_ref, ones_ref, hf_ref, hb_ref, y_ref, ua_ref, ub_ref):
    o = of_ref[...] + ob_ref[...]
    ms = jnp.dot(jnp.concatenate(_split(o * o), axis=-1), ones_ref[...], preferred_element_type=F32) * (1.0 / GDN_DV)
    z = z_ref[...]
    ua_ref[...] = (o * lax.rsqrt(ms + RMS_EPS) * gn_ref[...] * (z * _sigmoid(z))).astype(ua_ref.dtype)
    y = y_ref[...]
    gelu = 0.5 * y * (1.0 + jnp.tanh(0.7978845608028654 * (y + 0.044715 * (y * y * y))))
    ub_ref[...] = ((hf_ref[...] + hb_ref[...]) * gelu).astype(ub_ref.dtype)


def _post(o_f, o_b, h_f, h_b, p, gdn_norm_g):
    b, s, _ = p.shape
    tm = _pick_tile(s, 544)
    ones_n = jnp.concatenate([_head_block_ones(GDN_HEADS, GDN_DV)] * N_SPLIT, axis=0)
    gn = jnp.tile(gdn_norm_g.astype(F32), GDN_HEADS).reshape(1, GDN_W)
    tok = pl.BlockSpec((None, tm, GDN_W), lambda bi, i: (bi, i, 0))
    pcol = lambda c: pl.BlockSpec((None, tm, GDN_W), lambda bi, i: (bi, i, c))
    return pl.pallas_call(
        _post_kernel,
        grid=(b, s // tm),
        in_specs=[tok, tok, pcol(P_GDN_Z // GDN_W), _full(gn), _full(ones_n), tok, tok, pcol(P_LRU_Y // LRU_W)],
        out_specs=[tok, tok],
        out_shape=[jax.ShapeDtypeStruct((b, s, GDN_W), BF16)] * 2,
        compiler_params=_cparams(("parallel", "parallel")),
        name="gdn_lru_post",
    )(o_f, o_b, p, gn, ones_n, h_f, h_b, p)


def _merge_kernel(x_ref, mc_ref, ml_ref, u0_ref, u1_ref, u2_ref, u3_ref, t0_ref, t1_ref, t2_ref, t3_ref,
                  bg_ref, wb_ref, wo_ref, o_ref, *, n_lat):
    merged = None
    for n, (u_ref, t_ref) in enumerate(((u0_ref, t0_ref), (u1_ref, t1_ref), (u2_ref, t2_ref), (u3_ref, t3_ref))):
        gate = _sigmoid(t_ref[...] + bg_ref[n:n + 1, :].astype(BF16))
        term = gate.astype(F32) * jnp.dot(u_ref[...], wb_ref[n], preferred_element_type=F32)
        merged = term if merged is None else merged + term
    out = jnp.dot(merged.astype(BF16), wo_ref[...], preferred_element_type=F32)
    g1 = _row_select(x_ref.shape[0], pl.program_id(1), n_lat, mc_ref[2:3, :], ml_ref[2:3, :])
    o_ref[...] = x_ref[...] + g1 * out


def _merge(xs, mod_c, mod_l, branches, p, b_gate, w_branch, w_out, layer, n_lat, with_ctx):
    b, s, d = xs.shape
    rows = s if with_ctx else n_lat
    tm = _pick_tile(rows, 544)
    gate_blk = P_GATES // d
    tok = lambda w: pl.BlockSpec((None, tm, w), lambda bi, i: (bi, i, 0))
    in_specs = [tok(d), pl.BlockSpec((8, d), lambda bi, i: (0, 0)), pl.BlockSpec((None, 8, d), lambda bi, i: (bi, 0, 0))]
    in_specs += [tok(BRANCH_W)] * N_BRANCH
    in_specs += [pl.BlockSpec((None, tm, d), functools.partial(lambda bi, i, n: (bi, i, gate_blk + n), n=n))
                 for n in range(N_BRANCH)]
    in_specs += [_full(b_gate),
                 pl.BlockSpec((None,) + w_branch.shape[1:], lambda bi, i: (layer, 0, 0, 0)),
                 pl.BlockSpec((None,) + w_out.shape[1:], lambda bi, i: (layer, 0, 0))]
    return pl.pallas_call(
        functools.partial(_merge_kernel, n_lat=n_lat),
        grid=(b, rows // tm),
        in_specs=in_specs,
        out_specs=tok(d),
        out_shape=jax.ShapeDtypeStruct((b, rows, d), F32),
        input_output_aliases={0: 0} if with_ctx else {},
        compiler_params=_cparams(("parallel", "parallel")),
        name="merge",
    )(xs, mod_c, mod_l, *branches, p, p, p, p, b_gate, w_branch, w_out)


def _mlp_kernel(x_ref, mc_ref, ml_ref, gn_ref, gf_ref, w1_ref, w2_ref, o_ref, h_ref, acc_ref, *, n_lat, final_norm):
    f = pl.program_id(2)
    tm = x_ref.shape[0]
    i = pl.program_id(1)

    @pl.when(f == 0)
    def _():
        _norm_modulate(x_ref, h_ref, gn_ref, mc_ref, ml_ref, 3, i, n_lat)

    a = jnp.maximum(jnp.dot(h_ref[...], w1_ref[...], preferred_element_type=F32), 0.0)
    part = jnp.dot((a * a).astype(BF16), w2_ref[...], preferred_element_type=F32)

    @pl.when(f == 0)
    def _():
        acc_ref[...] = part

    @pl.when(f > 0)
    def _():
        acc_ref[...] += part

    @pl.when(f == pl.num_programs(2) - 1)
    def _():
        g2 = _row_select(tm, i, n_lat, mc_ref[5:6, :], ml_ref[5:6, :])
        y = x_ref[...] + g2 * acc_ref[...]
        if final_norm:
            y = y * lax.rsqrt(jnp.mean(y * y, axis=-1, keepdims=True) + RMS_EPS) * gf_ref[...]
        o_ref[...] = y


def _mlp(xs, mod_c, mod_l, gain, w1, w2, layer, final_gain, n_lat, final_norm):
    b, rows, d = xs.shape
    dff = w1.shape[-1]
    tm = _pick_tile(rows, 1088)
    tf = 1024
    row = pl.BlockSpec((1, d), lambda bi, i, f: (0, 0))
    return pl.pallas_call(
        functools.partial(_mlp_kernel, n_lat=n_lat, final_norm=final_norm),
        grid=(b, rows // tm, dff // tf),
        in_specs=[pl.BlockSpec((None, tm, d), lambda bi, i, f: (bi, i, 0)),
                  pl.BlockSpec((8, d), lambda bi, i, f: (0, 0)),
                  pl.BlockSpec((None, 8, d), lambda bi, i, f: (bi, 0, 0)),
                  row, row,
                  pl.BlockSpec((None, d, tf), lambda bi, i, f: (layer, 0, f)),
                  pl.BlockSpec((None, tf, d), lambda bi, i, f: (layer, f, 0))],
        out_specs=pl.BlockSpec((None, tm, d), lambda bi, i, f: (bi, i, 0)),
        out_shape=jax.ShapeDtypeStruct((b, rows, d), F32),
        scratch_shapes=[pltpu.VMEM((tm, d), BF16), pltpu.VMEM((tm, d), F32)],
        compiler_params=_cparams(("parallel", "parallel", "arbitrary")),
        name="mlp",
    )(xs, mod_c, mod_l, gain, final_gain, w1, w2)


def kernel(x, c, ctx, c_ctx, mod_w, mod_b, norm1_g, norm2_g, w_in, b_gate, gdn_conv_w, gdn_a_log, gdn_dt_bias,
           gdn_norm_g, lru_conv_w, lru_conv_b, lru_w_r, lru_b_r, lru_w_i, lru_b_i, lru_lambda, mla_q_norm_g,
           mla_w_uq, mla_kv_norm_g, mla_w_ukv, na_rpb, w_branch, w_out, mlp_w1, mlp_w2, final_norm_g):
    bsz, n_tok, d = x.shape
    n_ctx = ctx.shape[1]
    depth = w_in.shape[0]
    assert n_ctx % SEQ_TILE == 0 and n_tok % SEQ_TILE == 0 and n_tok % GRID_W == 0
    na_meta, na_rel_rows, na_cols = _na_geometry(n_tok // GRID_W)
    na_meta = jnp.asarray(na_meta)
    cos, sin = _rope_tables(n_tok, n_ctx)

    n_rows = -(-(bsz + 1) // 8) * 8
    cc = jnp.zeros((n_rows, d), F32).at[:bsz].set(c).at[bsz].set(c_ctx)
    final_gain = final_norm_g.reshape(1, d)

    w_in_all = _arrange_w_in(w_in)
    wb_all = w_branch.astype(BF16)
    wo_all = w_out.astype(BF16)
    w1_all = mlp_w1.astype(BF16)
    w2_all = mlp_w2.astype(BF16)

    xs = jnp.concatenate([x, ctx], axis=1)
    for l in range(depth):
        need_ctx = l < depth - 1
        mod = _modulation(cc, mod_w, mod_b[l], l).reshape(n_rows, N_MOD, d)
        pad = jnp.zeros((8 - N_MOD, d), F32)
        mod_c = jnp.concatenate([mod[bsz], pad], axis=0)
        mod_l = jnp.concatenate([mod[:bsz], jnp.broadcast_to(pad, (bsz, 8 - N_MOD, d))], axis=1)

        wq, wk, wv, place = _arrange_mla(mla_w_uq[l], mla_w_ukv[l])
        gq = mla_q_norm_g[l].reshape(1, -1)
        gkv = mla_kv_norm_g[l].reshape(1, -1)
        g1n = norm1_g[l].reshape(1, d)
        g2n = norm2_g[l].reshape(1, d)

        p32, p16 = _inproj(xs, mod_c, mod_l, g1n, w_in_all, l, n_tok)

        gq_, gk_, gv_, gbeta, gg = _gdn_prep(p32, gdn_conv_w[l], gdn_a_log[l], gdn_dt_bias[l], n_tok)
        o_f, o_b = _gdn_scan(gq_, gk_, gv_, gbeta, gg, n_tok)
        h_f, h_b = _lru_scan(p32, lru_conv_w[l], lru_conv_b[l], lru_w_r[l], lru_b_r[l], lru_w_i[l], lru_b_i[l],
                             lru_lambda[l], n_tok)
        ua, ub = _post(o_f, o_b, h_f, h_b, p32, gdn_norm_g[l])

        mq, mk, mv = _mla_prep(p32, p16, cos, sin, gq, gkv, wq, wk, wv, place)
        uc = _mla_flash(mq, mk, mv, n_tok)
        if need_ctx:
            uc = _mla_flash(mq, mk, mv, n_tok, prev=uc)
        ud = _na_attention(p16, na_meta, _na_bias_table(na_rpb[l], na_rel_rows, na_cols), n_tok, need_ctx)

        xs = _merge(xs, mod_c, mod_l, (ua, ub, uc, ud), p16, b_gate[l], wb_all, wo_all, l, n_tok, need_ctx)
        xs = _mlp(xs, mod_c, mod_l, g2n, w1_all, w2_all, l, final_gain, n_tok, l == depth - 1)
    return xs
```

```python
import functools

import jax
import jax.numpy as jnp
import numpy as np
from jax import lax
from jax.experimental import pallas as pl
from jax.experimental.pallas import tpu as pltpu

F32 = jnp.float32
BF16 = jnp.bfloat16

GRID_W = 64
N_MOD = 6
RMS_EPS = 1e-6
GDN_HEADS = 4
GDN_DK = 64
GDN_DV = 64
GDN_CHUNK = 64
GDN_W = GDN_HEADS * GDN_DV
GDN_CONV = 4
LRU_W = 256
LRU_BLOCKS = 4
LRU_BLOCK_W = LRU_W // LRU_BLOCKS
LRU_CONV = 4
LRU_C = 8.0
MLA_HEADS = 4
MLA_Q_RANK = 256
MLA_KV_RANK = 128
MLA_NOPE = 64
MLA_ROPE = 32
MLA_V = 64
MLA_SLOT = 128
ROPE_BASE = 10000.0
LOG2_E = 1.4426950408889634
NA_HEADS = 4
NA_DH = 64
NA_W = NA_HEADS * NA_DH
NA_WIN_ROWS = 8
NA_WIN_COLS = 16
N_BRANCH = 4
BRANCH_W = 256

SEQ_TILE = 256
HALO = 8

_REF_COLS = {}
_off = 0
for _name, _w in (('gdn_qkv', 3 * GDN_W), ('gdn_z', GDN_W), ('gdn_beta', 2 * GDN_HEADS), ('gdn_a', 2 * GDN_HEADS),
                  ('lru_x', LRU_W), ('lru_y', LRU_W), ('mla_q', MLA_Q_RANK), ('mla_kv', MLA_KV_RANK),
                  ('mla_kr', MLA_ROPE), ('na_qkv', 3 * NA_W)):
    _REF_COLS[_name] = (_off, _w)
    _off += _w
N_MIX_COLS = _off

P_GDN_QKV = 0
P_GDN_Z = 768
P_LRU_X = 1024
P_LRU_Y = 1280
P_MLA_KV = 1536
P_MLA_KR = 1664
P_GDN_BA = 1792
P32_COLS = 2048
P_MLA_Q = 0
P_NA_QKV = 256
P_GATES = 1024
P16_COLS = P_GATES + N_BRANCH * 1024
PROJ_TILE = 1024

VMEM_LIMIT = 52 * 1024 * 1024


def _cparams(sem):
    return pltpu.CompilerParams(dimension_semantics=sem, vmem_limit_bytes=VMEM_LIMIT)


def _pick_tile(n, cap):
    best = 8
    for t in range(8, min(n, cap) + 1, 8):
        if n % t == 0:
            best = t
    return best


def _full(a):
    return pl.BlockSpec(a.shape, lambda *_: (0,) * a.ndim)


N_SPLIT = 2


def _split(x):
    hi = x.astype(BF16)
    lo = (x - hi.astype(F32)).astype(BF16)
    return hi, lo


def _sigmoid(x):
    return 0.5 * jnp.tanh(0.5 * x) + 0.5


def _arrange_w_in(w_in):
    pieces, pos = [], 0

    def put(dst, block):
        nonlocal pos
        if dst > pos:
            pieces.append(jnp.zeros(w_in.shape[:-1] + (dst - pos,), w_in.dtype))
        pieces.append(block)
        pos = dst + block.shape[-1]

    ref = lambda name: w_in[..., _REF_COLS[name][0]:_REF_COLS[name][0] + _REF_COLS[name][1]]
    for name, dst in (('gdn_qkv', P_GDN_QKV), ('gdn_z', P_GDN_Z), ('lru_x', P_LRU_X), ('lru_y', P_LRU_Y),
                      ('mla_kv', P_MLA_KV), ('mla_kr', P_MLA_KR), ('gdn_beta', P_GDN_BA),
                      ('gdn_a', P_GDN_BA + 2 * GDN_HEADS)):
        put(dst, ref(name))
    put(P32_COLS + P_MLA_Q, ref('mla_q'))
    put(P32_COLS + P_NA_QKV, ref('na_qkv'))
    put(P32_COLS + P_GATES, w_in[..., N_MIX_COLS:])
    assert pos == P32_COLS + P16_COLS
    return jnp.concatenate(pieces, axis=-1).astype(BF16)


def _rope_perm():
    q = MLA_ROPE // 4
    src = np.zeros(MLA_ROPE, np.int32)
    sign = np.zeros(MLA_ROPE, np.float32)
    for base in (0, 2 * q):
        for d in range(q):
            src[base + d] = base + d + q
            sign[base + d] = -1.0
            src[base + q + d] = base + d
            sign[base + q + d] = 1.0
    return src, sign


def _arrange_mla(w_uq, w_ukv):
    src, sign = _rope_perm()
    hq = MLA_NOPE + MLA_ROPE
    wq = jnp.zeros((MLA_Q_RANK, 2 * MLA_HEADS * MLA_SLOT), F32)
    wk = jnp.zeros((MLA_KV_RANK, MLA_HEADS * MLA_SLOT), F32)
    wv = jnp.zeros((MLA_KV_RANK, MLA_HEADS * MLA_V), F32)
    place = np.zeros((2 * MLA_SLOT, 2 * MLA_HEADS * MLA_SLOT), np.float32)
    rot_off = MLA_HEADS * MLA_SLOT
    for h in range(MLA_HEADS):
        nope = w_uq[:, h * hq:h * hq + MLA_NOPE]
        pe = w_uq[:, h * hq + MLA_NOPE:(h + 1) * hq]
        s = h * MLA_SLOT
        wq = wq.at[:, s:s + MLA_NOPE].set(nope)
        wq = wq.at[:, s + MLA_NOPE:s + MLA_NOPE + MLA_ROPE].set(pe)
        wq = wq.at[:, rot_off + s + MLA_NOPE:rot_off + s + MLA_NOPE + MLA_ROPE].set(pe[:, src] * sign)
        wk = wk.at[:, s:s + MLA_NOPE].set(w_ukv[:, h * 128:h * 128 + MLA_NOPE])
        wv = wv.at[:, h * MLA_V:(h + 1) * MLA_V].set(w_ukv[:, h * 128 + MLA_NOPE:(h + 1) * 128])
        for d in range(MLA_ROPE):
            for half in (0, MLA_SLOT):
                place[half + d, s + MLA_NOPE + d] = 1.0
                place[half + src[d], rot_off + s + MLA_NOPE + d] = sign[d]
    return wq.astype(BF16), wk.astype(BF16), wv.astype(BF16), jnp.asarray(place, BF16)


def _rope_tables(n_tok, n_ctx):
    cos = np.ones((n_tok + n_ctx, MLA_SLOT), np.float32)
    sin = np.zeros((n_tok + n_ctx, MLA_SLOT), np.float32)
    t = np.arange(n_tok)
    row = (t // GRID_W).astype(np.float32)
    col = (t % GRID_W).astype(np.float32)
    n_freq = MLA_ROPE // 4
    inv = (ROPE_BASE ** (-np.arange(n_freq, dtype=np.float32) / n_freq)).astype(np.float32)
    ar = row[:, None] * inv
    ac = col[:, None] * inv
    ang = np.concatenate([ar, ar, ac, ac], axis=-1).astype(np.float32)
    cos[:n_tok, MLA_NOPE:MLA_NOPE + MLA_ROPE] = np.cos(ang)
    sin[:n_tok, MLA_NOPE:MLA_NOPE + MLA_ROPE] = np.sin(ang)
    return jnp.asarray(cos), jnp.asarray(sin)


def _head_block_ones(n_heads, width):
    m = np.kron(np.eye(n_heads, dtype=np.float32), np.ones((width, width), np.float32))
    return jnp.asarray(m, BF16)


def _mod_kernel(c_ref, w_ref, b_ref, o_ref):
    c = c_ref[...]
    s = c * _sigmoid(c)
    o_ref[...] = jnp.dot(s, w_ref[...], preferred_element_type=F32) + b_ref[...]


def _modulation(cc, mod_w, mod_b, layer):
    r, d = cc.shape
    n = mod_w.shape[-1]
    tn = 1024
    return pl.pallas_call(
        _mod_kernel,
        grid=(n // tn,),
        in_specs=[pl.BlockSpec((r, d), lambda j: (0, 0)),
                  pl.BlockSpec((None, d, tn), lambda j: (layer, 0, j)),
                  pl.BlockSpec((1, tn), lambda j: (0, j))],
        out_specs=pl.BlockSpec((r, tn), lambda j: (0, j)),
        out_shape=jax.ShapeDtypeStruct((r, n), F32),
        compiler_params=_cparams(("arbitrary",)),
        name="modulation",
    )(cc, mod_w, mod_b.reshape(1, n))


def _row_select(tile_rows, tile_idx, n_lat, ctx_vec, lat_vec):
    row = tile_idx * tile_rows + lax.broadcasted_iota(jnp.int32, (tile_rows, 1), 0)
    return jnp.where(row < n_lat, lat_vec, ctx_vec)


def _norm_modulate(x_ref, h_ref, gain_ref, mc_ref, ml_ref, shift_row, tile_idx, n_lat):
    tm = x_ref.shape[0]
    x = x_ref[...]
    xn = x * lax.rsqrt(jnp.mean(x * x, axis=-1, keepdims=True) + RMS_EPS)
    sh_l, sh_c = ml_ref[shift_row:shift_row + 1, :], mc_ref[shift_row:shift_row + 1, :]
    amp_l = gain_ref[...] * (1.0 + ml_ref[shift_row + 1:shift_row + 2, :])
    amp_c = gain_ref[...] * (1.0 + mc_ref[shift_row + 1:shift_row + 2, :])
    all_latent = (tile_idx + 1) * tm <= n_lat

    @pl.when(all_latent)
    def _():
        h_ref[...] = (xn * amp_l + sh_l).astype(h_ref.dtype)

    @pl.when(jnp.logical_not(all_latent))
    def _():
        amp = _row_select(tm, tile_idx, n_lat, amp_c, amp_l)
        shift = _row_select(tm, tile_idx, n_lat, sh_c, sh_l)
        h_ref[...] = (xn * amp + shift).astype(h_ref.dtype)


def _inproj_kernel(x_ref, mc_ref, ml_ref, g_ref, w_ref, o32_ref, o16_ref, h_ref, *, n_lat):
    j = pl.program_id(2)
    n32 = P32_COLS // PROJ_TILE

    @pl.when(j == 0)
    def _():
        _norm_modulate(x_ref, h_ref, g_ref, mc_ref, ml_ref, 0, pl.program_id(1), n_lat)

    acc = jnp.dot(h_ref[...], w_ref[...], preferred_element_type=F32)

    @pl.when(j < n32)
    def _():
        o32_ref[...] = acc

    @pl.when(j >= n32)
    def _():
        o16_ref[...] = acc.astype(BF16)


def _inproj(xs, mod_c, mod_l, gain, w, layer, n_lat):
    b, s, d = xs.shape
    tm = _pick_tile(s, 1088)
    tn = PROJ_TILE
    n32 = P32_COLS // tn
    return pl.pallas_call(
        functools.partial(_inproj_kernel, n_lat=n_lat),
        grid=(b, s // tm, (P32_COLS + P16_COLS) // tn),
        in_specs=[pl.BlockSpec((None, tm, d), lambda bi, i, j: (bi, i, 0)),
                  pl.BlockSpec((8, d), lambda bi, i, j: (0, 0)),
                  pl.BlockSpec((None, 8, d), lambda bi, i, j: (bi, 0, 0)),
                  pl.BlockSpec((1, d), lambda bi, i, j: (0, 0)),
                  pl.BlockSpec((None, d, tn), lambda bi, i, j: (layer, 0, j))],
        out_specs=[pl.BlockSpec((None, tm, tn), lambda bi, i, j: (bi, i, jnp.minimum(j, n32 - 1))),
                   pl.BlockSpec((None, tm, tn), lambda bi, i, j: (bi, i, jnp.maximum(j - n32, 0)))],
        out_shape=[jax.ShapeDtypeStruct((b, s, P32_COLS), F32), jax.ShapeDtypeStruct((b, s, P16_COLS), BF16)],
        scratch_shapes=[pltpu.VMEM((tm, d), BF16)],
        compiler_params=_cparams(("parallel", "parallel", "arbitrary")),
        name="inproj",
    )(xs, mod_c, mod_l, gain, w)


def _mla_prep_kernel(ql_ref, kv_ref, cos_ref, sin_ref, gq_ref, gkv_ref, wq_ref, wk_ref, wv_ref, pl_ref,
                     q_ref, k_ref, v_ref, *, scale):
    nslot = MLA_HEADS * MLA_SLOT
    cos = jnp.concatenate([cos_ref[...]] * MLA_HEADS, axis=-1)
    sin = jnp.concatenate([sin_ref[...]] * MLA_HEADS, axis=-1)

    ql = ql_ref[...].astype(F32)
    qn = ql * lax.rsqrt(jnp.mean(ql * ql, axis=-1, keepdims=True) + RMS_EPS) * gq_ref[...]
    q2 = jnp.dot(qn.astype(BF16), wq_ref[...], preferred_element_type=F32)
    q = (q2[:, :nslot] * cos + q2[:, nslot:] * sin) * scale
    q_ref[...] = q.astype(BF16)

    kvkr = kv_ref[...]
    kvl = kvkr[:, :MLA_KV_RANK]
    kr = kvkr[:, MLA_KV_RANK:]
    kvn = (kvl * lax.rsqrt(jnp.mean(kvl * kvl, axis=-1, keepdims=True) + RMS_EPS) * gkv_ref[...]).astype(BF16)
    kn = jnp.dot(kvn, wk_ref[...], preferred_element_type=F32)
    v_ref[...] = jnp.dot(kvn, wv_ref[...], preferred_element_type=F32).astype(BF16)
    kr_hi = kr.astype(BF16)
    kr_lo = (kr - kr_hi.astype(F32)).astype(BF16)
    kr2 = jnp.dot(jnp.concatenate([kr_hi, kr_lo], axis=-1), pl_ref[...], preferred_element_type=F32)
    k_ref[...] = (kn + kr2[:, :nslot] * cos + kr2[:, nslot:] * sin).astype(BF16)


def _mla_prep(p32, p16, cos, sin, gq, gkv, wq, wk, wv, place):
    b, s, _ = p32.shape
    tm = _pick_tile(s, 544)
    nslot = MLA_HEADS * MLA_SLOT
    scale = (MLA_NOPE + MLA_ROPE) ** -0.5 * LOG2_E
    return pl.pallas_call(
        functools.partial(_mla_prep_kernel, scale=scale),
        grid=(b, s // tm),
        in_specs=[pl.BlockSpec((None, tm, MLA_Q_RANK), lambda bi, i: (bi, i, P_MLA_Q // MLA_Q_RANK)),
                  pl.BlockSpec((None, tm, 2 * MLA_SLOT), lambda bi, i: (bi, i, P_MLA_KV // (2 * MLA_SLOT))),
                  pl.BlockSpec((tm, MLA_SLOT), lambda bi, i: (i, 0)),
                  pl.BlockSpec((tm, MLA_SLOT), lambda bi, i: (i, 0)),
                  _full(gq), _full(gkv), _full(wq), _full(wk), _full(wv), _full(place)],
        out_specs=[pl.BlockSpec((None, tm, nslot), lambda bi, i: (bi, i, 0)),
                   pl.BlockSpec((None, tm, nslot), lambda bi, i: (bi, i, 0)),
                   pl.BlockSpec((None, tm, MLA_HEADS * MLA_V), lambda bi, i: (bi, i, 0))],
        out_shape=[jax.ShapeDtypeStruct((b, s, nslot), BF16),
                   jax.ShapeDtypeStruct((b, s, nslot), BF16),
                   jax.ShapeDtypeStruct((b, s, MLA_HEADS * MLA_V), BF16)],
        compiler_params=_cparams(("parallel", "parallel")),
        name="mla_prep",
    )(p16, p32, cos, sin, gq, gkv, wq, wk, wv, place)


def _flash_softmax(h, s, m_ref, l_ref):
    m_prev = m_ref[h]
    m_new = jnp.maximum(m_prev, jnp.max(s, axis=-1, keepdims=True))
    alpha = jnp.exp2(m_prev - m_new)
    p = jnp.exp2(s - jnp.concatenate([m_new] * (s.shape[1] // 128), axis=-1))
    l_ref[h] = alpha * l_ref[h] + jnp.sum(p, axis=-1, keepdims=True)
    m_ref[h] = m_new
    return alpha, p.astype(BF16)


def _mla_flash_kernel(*refs, tk, aliased):
    if aliased:
        q_ref, k_ref, v_ref, _, o_ref, m_ref, l_ref, acc_ref = refs
    else:
        q_ref, k_ref, v_ref, o_ref, m_ref, l_ref, acc_ref = refs
    tq = q_ref.shape[0]
    nk = k_ref.shape[0]
    n_loop = nk // tk
    m_ref[...] = jnp.full(m_ref.shape, -jnp.inf, F32)
    l_ref[...] = jnp.zeros(l_ref.shape, F32)
    acc_ref[...] = jnp.zeros(acc_ref.shape, F32)

    def chunk(rows):
        heads = range(MLA_HEADS)
        hs = [slice(h * MLA_SLOT, (h + 1) * MLA_SLOT) for h in heads]
        vs = [slice((h // 2) * 128, (h // 2) * 128 + 128) for h in heads]
        s = [lax.dot_general(q_ref[:, hs[h]], k_ref[rows, hs[h]], (((1,), (1,)), ((), ())),
                             preferred_element_type=F32) for h in heads]
        ap = [_flash_softmax(h, s[h], m_ref, l_ref) for h in heads]
        for h in heads:
            alpha, p = ap[h]
            acc_ref[h] = alpha * acc_ref[h] + jnp.dot(p, v_ref[rows, vs[h]], preferred_element_type=F32)

    if n_loop:
        def body(j, carry):
            chunk(pl.ds(pl.multiple_of(j * tk, tk), tk))
            return carry
        lax.fori_loop(0, n_loop, body, 0)
    if nk > n_loop * tk:
        chunk(slice(n_loop * tk, nk))

    lane = lax.broadcasted_iota(jnp.int32, (tq, 128), 1)
    outs = []
    for pair in range(MLA_HEADS // 2):
        o0 = acc_ref[2 * pair] / l_ref[2 * pair]
        o1 = acc_ref[2 * pair + 1] / l_ref[2 * pair + 1]
        outs.append(jnp.where(lane < MLA_V, o0, o1))
    o_ref[...] = jnp.concatenate(outs, axis=-1).astype(o_ref.dtype)


def _mla_flash(q, k, v, n_lat, prev=None):
    b, s, nslot = q.shape
    n_ctx = s - n_lat
    nv = MLA_HEADS * MLA_V
    tk = 512
    if prev is None:
        tq = _pick_tile(n_lat, 1024)
        q_off, n_q, kv_rows, kv_blk = 0, n_lat // tq, s, 0
    else:
        assert n_lat % n_ctx == 0
        tq = _pick_tile(n_ctx, 256)
        q_off, n_q, kv_rows, kv_blk = n_lat // tq, n_ctx // tq, n_ctx, n_lat // n_ctx
    in_specs = [pl.BlockSpec((None, tq, nslot), lambda bi, i: (bi, i + q_off, 0)),
                pl.BlockSpec((None, kv_rows, nslot), lambda bi, i: (bi, kv_blk, 0)),
                pl.BlockSpec((None, kv_rows, nv), lambda bi, i: (bi, kv_blk, 0))]
    args = [q, k, v]
    aliases = {}
    if prev is not None:
        in_specs.append(pl.BlockSpec(memory_space=pl.ANY))
        args.append(prev)
        aliases = {3: 0}
    return pl.pallas_call(
        functools.partial(_mla_flash_kernel, tk=tk, aliased=prev is not None),
        grid=(b, n_q),
        in_specs=in_specs,
        out_specs=pl.BlockSpec((None, tq, nv), lambda bi, i: (bi, i + q_off, 0)),
        out_shape=jax.ShapeDtypeStruct((b, s, nv), BF16),
        scratch_shapes=[pltpu.VMEM((MLA_HEADS, tq, 128), F32),
                        pltpu.VMEM((MLA_HEADS, tq, 128), F32),
                        pltpu.VMEM((MLA_HEADS, tq, 128), F32)],
        input_output_aliases=aliases,
        compiler_params=_cparams(("parallel", "arbitrary")),
        name="mla_flash",
    )(*args)


NA_QROWS = 4
NA_SLAB = NA_QROWS + NA_WIN_ROWS


def _na_geometry(rows):
    assert rows % NA_QROWS == 0 and rows >= NA_SLAB
    nblk = rows // NA_QROWS
    qc = np.arange(GRID_W)
    cs = np.clip(qc - NA_WIN_COLS // 2, 0, GRID_W - NA_WIN_COLS)
    col_valid = (qc[None, :] >= cs[:, None]) & (qc[None, :] < cs[:, None] + NA_WIN_COLS)
    rel_c = np.clip(qc[None, :] - qc[:, None] + NA_WIN_COLS - 1, 0, 2 * NA_WIN_COLS - 2)
    onehot_c = (rel_c[None] == np.arange(2 * NA_WIN_COLS - 1)[:, None, None]) & col_valid[None]
    patterns, cls, starts = {}, [], []
    for i in range(nblk):
        r0 = i * NA_QROWS
        start = int(np.clip(r0 - NA_WIN_ROWS // 2, 0, rows - NA_SLAB))
        qr = r0 + np.arange(NA_QROWS)
        rs = np.clip(qr - NA_WIN_ROWS // 2, 0, rows - NA_WIN_ROWS)
        key = (start - r0,) + tuple((rs - r0).tolist())
        if key not in patterns:
            kr = start + np.arange(NA_SLAB)
            row_valid = (kr[None, :] >= rs[:, None]) & (kr[None, :] < rs[:, None] + NA_WIN_ROWS)
            rel_r = np.clip(kr[None, :] - qr[:, None] + NA_WIN_ROWS - 1, 0, 2 * NA_WIN_ROWS - 2)
            patterns[key] = (len(patterns), np.where(row_valid, rel_r, -1))
        cls.append(patterns[key][0])
        starts.append(start)
    ordered = sorted(patterns.values(), key=lambda z: z[0])
    rel_rows = np.stack([z[1] for z in ordered])
    meta = np.stack([np.asarray(cls, np.int32), np.asarray(starts, np.int32)])
    return meta, rel_rows, (onehot_c.astype(np.float32), col_valid)


def _na_bias_table(rpb, rel_rows, col_tables):
    onehot_c, col_valid = col_tables
    bc = jnp.einsum('hrc,cqk->hrqk', rpb.astype(F32), onehot_c, precision=lax.Precision.HIGHEST)
    bc = jnp.where(col_valid, bc, -jnp.inf)
    outside = jnp.full(bc.shape[:1] + bc.shape[2:], -jnp.inf, F32)
    pats = []
    for pat in rel_rows:
        qrows = [jnp.concatenate([bc[:, r] if r >= 0 else outside for r in row], axis=-1) for row in pat]
        pats.append(jnp.concatenate(qrows, axis=1))
    return jnp.stack(pats)


def _na_scores(q_pair, lane, h, k_parts):
    in_head = (lane < NA_DH) if h % 2 == 0 else (lane >= NA_DH)
    qm = jnp.where(in_head, q_pair * (NA_DH ** -0.5), 0.0).astype(BF16)
    return [lax.dot_general(qm, k, (((1,), (1,)), ((), ())), preferred_element_type=F32) for k in k_parts]


def _na_softmax(scores, bias):
    if bias is not None:
        scores = [scores[0] + bias] + scores[1:]
    m = scores[0].max(axis=-1, keepdims=True)
    for s in scores[1:]:
        m = jnp.maximum(m, s.max(axis=-1, keepdims=True))
    probs = [jnp.exp(s - m) for s in scores]
    den = probs[0].sum(axis=-1, keepdims=True)
    for p in probs[1:]:
        den = den + p.sum(axis=-1, keepdims=True)
    return [p.astype(BF16) for p in probs], den


def _na_kernel(meta_ref, q_ref, k_ref, v_ref, bias_ref, o_ref, *, n_lat, with_ctx):
    i = pl.program_id(1)
    nq = q_ref.shape[0]
    n_lat_tiles = n_lat // nq
    n_all = k_ref.shape[0]
    lane = lax.broadcasted_iota(jnp.int32, (nq, 128), 1)

    def run(windowed):
        key_rows = [pl.ds(n_lat, n_all - n_lat)]
        if windowed:
            start = pl.multiple_of(meta_ref[1, jnp.minimum(i, n_lat_tiles - 1)] * GRID_W, GRID_W)
            key_rows = [pl.ds(start, NA_SLAB * GRID_W)] + key_rows
        pair_lanes = [slice((h // 2) * 128, (h // 2 + 1) * 128) for h in range(NA_HEADS)]
        scores = [_na_scores(q_ref[:, pair_lanes[h]], lane, h, [k_ref[r, pair_lanes[h]] for r in key_rows])
                  for h in range(NA_HEADS)]
        soft = [_na_softmax(scores[h], bias_ref[h] if windowed else None) for h in range(NA_HEADS)]
        heads = []
        for h in range(NA_HEADS):
            probs, den = soft[h]
            out = None
            for p, r in zip(probs, key_rows):
                po = jnp.dot(p, v_ref[r, pair_lanes[h]], preferred_element_type=F32)
                out = po if out is None else out + po
            heads.append(out / den)
        outs = [jnp.where(lane < NA_DH, heads[2 * pair], heads[2 * pair + 1]) for pair in range(NA_HEADS // 2)]
        o_ref[...] = jnp.concatenate(outs, axis=-1).astype(o_ref.dtype)

    if with_ctx:
        pl.when(i < n_lat_tiles)(lambda: run(True))
        pl.when(i >= n_lat_tiles)(lambda: run(False))
    else:
        run(True)


def _na_attention(p, meta, table, n_lat, with_ctx_queries):
    b, s, _ = p.shape
    nq = NA_QROWS * GRID_W
    nk = NA_SLAB * GRID_W
    assert (s - n_lat) % nq == 0
    nlt = n_lat // nq
    qb = P_NA_QKV // NA_W
    grid_spec = pltpu.PrefetchScalarGridSpec(
        num_scalar_prefetch=1,
        grid=(b, s // nq if with_ctx_queries else nlt),
        in_specs=[pl.BlockSpec((None, nq, NA_W), lambda bi, i, m: (bi, i, qb)),
                  pl.BlockSpec((None, s, NA_W), lambda bi, i, m: (bi, 0, qb + 1)),
                  pl.BlockSpec((None, s, NA_W), lambda bi, i, m: (bi, 0, qb + 2)),
                  pl.BlockSpec((None, NA_HEADS, nq, nk),
                               lambda bi, i, m: (m[0, jnp.minimum(i, nlt - 1)], 0, 0, 0))],
        out_specs=pl.BlockSpec((None, nq, NA_W), lambda bi, i, m: (bi, i, 0)))
    return pl.pallas_call(
        functools.partial(_na_kernel, n_lat=n_lat, with_ctx=with_ctx_queries),
        grid_spec=grid_spec,
        out_shape=jax.ShapeDtypeStruct((b, s, NA_W), BF16),
        compiler_params=_cparams(("parallel", "arbitrary")),
        name="na_attention",
    )(meta, p, p, p, table)


def _tile_conv(x, prev, nxt, w_ref, tile_idx, n_tiles, n_lat_tiles):
    r = x.shape[0]
    width = w_ref.shape[0]
    left = width // 2
    has_prev = jnp.logical_and(tile_idx != 0, tile_idx != n_lat_tiles)
    has_next = jnp.logical_and(tile_idx != n_tiles - 1, tile_idx != n_lat_tiles - 1)
    prev = jnp.where(has_prev, prev, 0.0)
    nxt = jnp.where(has_next, nxt, 0.0)
    xe = jnp.concatenate([prev, x, nxt], axis=0)
    acc = None
    for j in range(width):
        o = HALO - left + j
        term = xe[o:o + r, :] * w_ref[j:j + 1, :]
        acc = term if acc is None else acc + term
    return acc


def _halo_specs(width, col_block, tile_of):
    per = SEQ_TILE // HALO

    def main(bi, s, *_):
        return (bi, tile_of(s), col_block)

    def prev(bi, s, *_):
        return (bi, jnp.maximum(tile_of(s) * per - 1, 0), col_block)

    def make_next(n_tiles):
        def nxt(bi, s, *_):
            return (bi, jnp.minimum((tile_of(s) + 1) * per, n_tiles * per - 1), col_block)
        return nxt

    return main, prev, make_next


def _gdn_prep_kernel(x_ref, xp_ref, xn_ref, ba_ref, cw_ref, ones_ref, exp_ref, alog_ref, dtb_ref,
                     q_ref, k_ref, v_ref, beta_ref, g_ref, *, n_lat_tiles):
    i = pl.program_id(1)
    y = _tile_conv(x_ref[...], xp_ref[...], xn_ref[...], cw_ref, i, pl.num_programs(1), n_lat_tiles)
    y = y * _sigmoid(y)
    q = y[:, :GDN_W]
    k = y[:, GDN_W:2 * GDN_W]
    v_ref[...] = y[:, 2 * GDN_W:].astype(v_ref.dtype)

    def head_norm(u):
        parts = jnp.concatenate(_split(u * u), axis=-1)
        ss = jnp.dot(parts, ones_ref[...], preferred_element_type=F32)
        return u * lax.rsqrt(ss + RMS_EPS)

    q_ref[...] = (head_norm(q) * (GDN_DK ** -0.5)).astype(q_ref.dtype)
    k_ref[...] = head_norm(k).astype(k_ref.dtype)

    ba = ba_ref[...]
    a = ba + dtb_ref[...]
    softplus = jnp.maximum(a, 0.0) + jnp.log1p(jnp.exp(-jnp.abs(a)))
    lane = lax.broadcasted_iota(jnp.int32, ba.shape, 1)
    compact = jnp.where(lane < 2 * GDN_HEADS, _sigmoid(ba), -jnp.exp(alog_ref[...]) * softplus)
    wide = jnp.dot(jnp.concatenate(_split(compact), axis=-1), exp_ref[...], preferred_element_type=F32)
    half = 2 * GDN_W
    beta_ref[...] = wide[:, :half].astype(beta_ref.dtype)
    g_ref[...] = wide[:, half:]


def _gdn_prep(p, conv_w, a_log, dt_bias, n_lat):
    b, s, _ = p.shape
    n_tiles = s // SEQ_TILE
    main, prev, make_next = _halo_specs(3 * GDN_W, 0, lambda t: t)
    ones_n = jnp.concatenate([_head_block_ones(GDN_HEADS, GDN_DK)] * N_SPLIT, axis=0)
    expand = np.zeros((128, 4 * GDN_W), np.float32)
    for kind in range(2):
        for d in range(2):
            for h in range(GDN_HEADS):
                c0 = kind * 2 * GDN_W + d * GDN_W + h * GDN_DV
                expand[kind * 2 * GDN_HEADS + d * GDN_HEADS + h, c0:c0 + GDN_DV] = 1.0
    expand_n = jnp.asarray(np.concatenate([expand] * N_SPLIT, axis=0), BF16)
    lanes = jnp.zeros((1, 128), F32)
    alog_e = lanes.at[0, 2 * GDN_HEADS:4 * GDN_HEADS].set(a_log.astype(F32).reshape(-1))
    dtb_e = lanes.at[0, 2 * GDN_HEADS:4 * GDN_HEADS].set(dt_bias.astype(F32).reshape(-1))
    tok = lambda w: pl.BlockSpec((None, SEQ_TILE, w), lambda bi, i: (bi, i, 0))
    return pl.pallas_call(
        functools.partial(_gdn_prep_kernel, n_lat_tiles=n_lat // SEQ_TILE),
        grid=(b, n_tiles),
        in_specs=[pl.BlockSpec((None, SEQ_TILE, 3 * GDN_W), main),
                  pl.BlockSpec((None, HALO, 3 * GDN_W), prev),
                  pl.BlockSpec((None, HALO, 3 * GDN_W), make_next(n_tiles)),
                  pl.BlockSpec((None, SEQ_TILE, 128), lambda bi, i: (bi, i, P_GDN_BA // 128)),
                  _full(conv_w), _full(ones_n), _full(expand_n), _full(alog_e), _full(dtb_e)],
        out_specs=[tok(GDN_W), tok(GDN_W), tok(GDN_W), tok(2 * GDN_W), tok(2 * GDN_W)],
        out_shape=[jax.ShapeDtypeStruct((b, s, GDN_W), BF16)] * 3 + [jax.ShapeDtypeStruct((b, s, 2 * GDN_W), BF16),
                                                                    jax.ShapeDtypeStruct((b, s, 2 * GDN_W), F32)],
        compiler_params=_cparams(("parallel", "parallel")),
        name="gdn_prep",
    )(p, p, p, p, conv_w, ones_n, expand_n, alog_e, dtb_e)


GDN_PAIR = 2 * GDN_DK


def _gdn_masks():
    c, w = GDN_CHUNK, GDN_W
    r2, c2 = np.arange(GDN_PAIR)[:, None], np.arange(GDN_PAIR)[None, :]
    bd = ((r2 // c) == (c2 // c)).astype(np.float32)
    i = np.arange(c)[:, None]
    j = (np.arange(w) % c)[None, :]
    level = np.zeros((c, w), np.int32)
    for bit in range(6):
        level += ((i ^ j) >= (1 << bit)).astype(np.int32)
    lvl = np.stack([(level == m).astype(np.float32) for m in range(7)])
    dirm = np.stack([np.stack([(j <= i), (j < i)]), np.stack([(j >= i), (j > i)])]).astype(np.float32)
    tj = (np.arange(N_SPLIT * c) % c)[None, :]
    tri = np.stack([(tj <= i), (tj >= i)]).astype(np.float32)
    return jnp.asarray(bd, BF16), jnp.asarray(lvl), jnp.asarray(dirm), jnp.asarray(tri, BF16)


def _heads_mm(x, y, bd, transpose_rhs=False):
    xb = x.astype(BF16)
    yb = y.astype(BF16)
    outs = []
    for pair in range(GDN_W // GDN_PAIR):
        ls = slice(pair * GDN_PAIR, (pair + 1) * GDN_PAIR)
        w = jnp.concatenate([yb[:, ls], yb[:, ls]], axis=0) * bd
        dims = (((1,), (1,)), ((), ())) if transpose_rhs else (((1,), (0,)), ((), ()))
        outs.append(lax.dot_general(xb[:, ls], w, dims, preferred_element_type=F32))
    return jnp.concatenate(outs, axis=1)


def _gdn_intra(probs, bd, lvl_ref, dirm_ref, tri_ref):
    c = GDN_CHUNK
    n = len(probs)
    eye = lvl_ref[0]
    gc, g_last, decay, gram = [], [], [], []
    for q, k, v, beta, g, rev in probs:
        d = 1 if rev else 0
        gcp = jnp.dot(tri_ref[d], jnp.concatenate(_split(g), axis=0), preferred_element_type=F32)
        gc.append(gcp)
        g_last.append(gcp[0:1, :] if rev else gcp[c - 1:c, :])
        gc_row = jnp.sum(gcp * eye, axis=0, keepdims=True)
        decay.append(dirm_ref[d, 0] * jnp.exp(jnp.minimum(gcp - gc_row, 0.0)))
        gram.append(_heads_mm(jnp.concatenate([k, q], axis=0), k, bd, transpose_rhs=True))
    lower = [dirm_ref[1 if p[5] else 0, 1] * p[3] * gram[x][:c] * decay[x] for x, p in enumerate(probs)]
    a_intra = [gram[x][c:] * decay[x] for x in range(n)]
    t = [eye - lower[x] * lvl_ref[1] for x in range(n)]
    for lev in range(2, 7):
        y = [_heads_mm(t[x], lower[x] * lvl_ref[lev], bd) for x in range(n)]
        z = [_heads_mm(y[x], t[x], bd) for x in range(n)]
        t = [t[x] - z[x] for x in range(n)]
    e_gc = [jnp.exp(gc[x]) for x in range(n)]
    u = [_heads_mm(t[x], p[2] * p[3], bd) for x, p in enumerate(probs)]
    w = [_heads_mm(t[x], p[1] * p[3] * e_gc[x], bd) for x, p in enumerate(probs)]
    wq = [jnp.concatenate([w[x], p[0] * e_gc[x]], axis=0).astype(BF16) for x, p in enumerate(probs)]
    k_dec = [(p[1] * jnp.exp(g_last[x] - gc[x])).astype(BF16) for x, p in enumerate(probs)]
    g_tot = [jnp.exp(g_last[x]) for x in range(n)]
    return u, wq, k_dec, a_intra, g_tot


def _gdn_state_steps(items, bd):
    c = GDN_CHUNK
    pairs = [slice(p * GDN_PAIR, (p + 1) * GDN_PAIR) for p in range(GDN_W // GDN_PAIR)]
    bdf = bd.astype(F32)
    ws_qs = [jnp.concatenate([jnp.dot(wq[:, ls], s_ref[p].astype(BF16), preferred_element_type=F32)
                              for p, ls in enumerate(pairs)], axis=1)
             for _, wq, _, _, _, s_ref in items]
    v_new = [it[0] - ws[:c] for it, ws in zip(items, ws_qs)]
    outs = [ws[c:] + _heads_mm(it[3], vn, bd) for it, ws, vn in zip(items, ws_qs, v_new)]
    for (_, _, k_dec, _, g_tot, s_ref), vn in zip(items, v_new):
        vb = vn.astype(BF16)
        for p, ls in enumerate(pairs):
            kv = lax.dot_general(k_dec[:, ls], vb[:, ls], (((0,), (0,)), ((), ())), preferred_element_type=F32)
            s_ref[p] = s_ref[p] * g_tot[:, ls] + kv * bdf
    return outs


def _gdn_scan_kernel(qf, kf, vf, bf, gf, qb, kb, vb, bb, gb, bd_ref, lvl_ref, dirm_ref, tri_ref,
                     of_ref, ob_ref, sf_ref, sb_ref):
    @pl.when(pl.program_id(1) == 0)
    def _():
        sf_ref[...] = jnp.zeros(sf_ref.shape, F32)
        sb_ref[...] = jnp.zeros(sb_ref.shape, F32)

    bd = bd_ref[...]
    n = SEQ_TILE // GDN_CHUNK
    probs, rows = [], []
    for c in range(n):
        rf = slice(c * GDN_CHUNK, (c + 1) * GDN_CHUNK)
        rb = slice((n - 1 - c) * GDN_CHUNK, (n - c) * GDN_CHUNK)
        probs.append(tuple(r[rf, :].astype(F32) for r in (qf, kf, vf, bf, gf)) + (False,))
        probs.append(tuple(r[rb, :].astype(F32) for r in (qb, kb, vb, bb, gb)) + (True,))
        rows += [rf, rb]
    u, wq, k_dec, a_intra, g_tot = _gdn_intra(probs, bd, lvl_ref, dirm_ref, tri_ref)
    for c in range(n):
        xs = (2 * c, 2 * c + 1)
        outs = _gdn_state_steps([(u[x], wq[x], k_dec[x], a_intra[x], g_tot[x], sb_ref if probs[x][5] else sf_ref)
                                 for x in xs], bd)
        for x, o in zip(xs, outs):
            (ob_ref if probs[x][5] else of_ref)[rows[x], :] = o


def _seq_tile_maps(n_lat_tiles, n_tiles):
    fwd = lambda s: lax.rem(s + n_lat_tiles, n_tiles)
    bwd = lambda s: n_tiles - 1 - s
    return fwd, bwd


def _gdn_scan(q, k, v, beta, g, n_lat):
    b, s, _ = q.shape
    n_tiles = s // SEQ_TILE
    fwd, bwd = _seq_tile_maps(n_lat // SEQ_TILE, n_tiles)
    spec = lambda tile_of, col: pl.BlockSpec((None, SEQ_TILE, GDN_W), lambda bi, t: (bi, tile_of(t), col))
    masks = _gdn_masks()
    in_specs = [spec(fwd, 0)] * 5 + [spec(bwd, 0)] * 3 + [spec(bwd, 1)] * 2 + [_full(m) for m in masks]
    return pl.pallas_call(
        _gdn_scan_kernel,
        grid=(b, n_tiles),
        in_specs=in_specs,
        out_specs=[spec(fwd, 0), spec(bwd, 0)],
        out_shape=[jax.ShapeDtypeStruct((b, s, GDN_W), F32)] * 2,
        scratch_shapes=[pltpu.VMEM((GDN_W // GDN_PAIR, GDN_PAIR, GDN_PAIR), F32)] * 2,
        compiler_params=_cparams(("parallel", "arbitrary")),
        name="gdn_scan",
    )(q, k, v, beta, g, q, k, v, beta, g, *masks)


def _lru_tile_prep(x_ref, xp_ref, xn_ref, cw_ref, cb_ref, wg_ref, bg_ref, nla_ref, a_ref, b_ref,
                   tile_idx, n_tiles, n_lat_tiles):
    xb = _tile_conv(x_ref[...], xp_ref[...], xn_ref[...], cw_ref, tile_idx, n_tiles, n_lat_tiles) + cb_ref[...]
    gates = _sigmoid(jnp.dot(xb.astype(BF16), wg_ref[...], preferred_element_type=F32) + bg_ref[...])
    log_a = nla_ref[...] * gates[:, :LRU_W]
    a_ref[...] = jnp.exp(log_a)
    th = jnp.tanh(log_a)
    b_ref[...] = jnp.sqrt(-2.0 * th / (1.0 - th)) * gates[:, LRU_W:] * xb


def _scan_group(a, b, h, row, reverse):
    for d in (1, 2, 4):
        if reverse:
            keep = row < 8 - d
            shift = 8 - d
        else:
            keep = row >= d
            shift = d
        a_s = jnp.where(keep, pltpu.roll(a, shift, 0), 1.0)
        b_s = jnp.where(keep, pltpu.roll(b, shift, 0), 0.0)
        b = a * b_s + b
        a = a * a_s
    return a * h + b


def _lru_scan_kernel(xf, xfp, xfn, xb, xbp, xbn, cw_ref, cb_ref, wgf, bgf, nlaf, wgb, bgb, nlab,
                     hf_ref, hb_ref, af_ref, bf_ref, ab_ref, bb_ref, cf_ref, cbk_ref, *, n_lat_tiles):
    s = pl.program_id(1)
    n_tiles = pl.num_programs(1)

    @pl.when(s == 0)
    def _():
        cf_ref[...] = jnp.zeros(cf_ref.shape, F32)
        cbk_ref[...] = jnp.zeros(cbk_ref.shape, F32)

    t_f, t_b = (m(s) for m in _seq_tile_maps(n_lat_tiles, n_tiles))
    _lru_tile_prep(xf, xfp, xfn, cw_ref, cb_ref, wgf, bgf, nlaf, af_ref, bf_ref, t_f, n_tiles, n_lat_tiles)
    _lru_tile_prep(xb, xbp, xbn, cw_ref, cb_ref, wgb, bgb, nlab, ab_ref, bb_ref, t_b, n_tiles, n_lat_tiles)

    n_groups = SEQ_TILE // 8
    row = lax.broadcasted_iota(jnp.int32, (8, LRU_W), 0)

    def body(gi, carry):
        h_f, h_b = carry
        rf = pl.ds(pl.multiple_of(gi * 8, 8), 8)
        rb = pl.ds(pl.multiple_of((n_groups - 1 - gi) * 8, 8), 8)
        out_f = _scan_group(af_ref[rf, :], bf_ref[rf, :], h_f, row, False)
        out_b = _scan_group(ab_ref[rb, :], bb_ref[rb, :], h_b, row, True)
        hf_ref[rf, :] = out_f
        hb_ref[rb, :] = out_b
        return (jnp.broadcast_to(out_f[7:8, :], (8, LRU_W)), jnp.broadcast_to(out_b[0:1, :], (8, LRU_W)))

    h_f, h_b = lax.fori_loop(0, n_groups, body, (cf_ref[...], cbk_ref[...]), unroll=4)
    cf_ref[...] = h_f
    cbk_ref[...] = h_b


def _lru_scan(p, conv_w, conv_b, w_r, b_r, w_i, b_i, lam, n_lat):
    b, s, _ = p.shape
    n_tiles = s // SEQ_TILE
    nlt = n_lat // SEQ_TILE
    fwd, bwd = _seq_tile_maps(nlt, n_tiles)
    col = P_LRU_X // LRU_W
    specs = []
    for tile_of in (fwd, bwd):
        main, prev, make_next = _halo_specs(LRU_W, col, tile_of)
        specs += [pl.BlockSpec((None, SEQ_TILE, LRU_W), main), pl.BlockSpec((None, HALO, LRU_W), prev),
                  pl.BlockSpec((None, HALO, LRU_W), make_next(n_tiles))]

    def blockdiag(w):
        return jax.scipy.linalg.block_diag(*[w[n] for n in range(LRU_BLOCKS)])

    dir_args = []
    for d in range(2):
        wg = jnp.concatenate([blockdiag(w_r[d]), blockdiag(w_i[d])], axis=1).astype(BF16)
        bg = jnp.concatenate([b_r[d], b_i[d]]).astype(F32).reshape(1, 2 * LRU_W)
        nla = (-LRU_C * jax.nn.softplus(-lam[d].astype(F32))).reshape(1, LRU_W)
        dir_args += [wg, bg, nla]
    cb2 = conv_b.reshape(1, LRU_W)
    out_spec = lambda tile_of: pl.BlockSpec((None, SEQ_TILE, LRU_W), lambda bi, t: (bi, tile_of(t), 0))
    return pl.pallas_call(
        functools.partial(_lru_scan_kernel, n_lat_tiles=nlt),
        grid=(b, n_tiles),
        in_specs=specs + [_full(conv_w), _full(cb2)] + [_full(a) for a in dir_args],
        out_specs=[out_spec(fwd), out_spec(bwd)],
        out_shape=[jax.ShapeDtypeStruct((b, s, LRU_W), F32)] * 2,
        scratch_shapes=[pltpu.VMEM((SEQ_TILE, LRU_W), F32)] * 4 + [pltpu.VMEM((8, LRU_W), F32)] * 2,
        compiler_params=_cparams(("parallel", "arbitrary")),
        name="lru_scan",
    )(p, p, p, p, p, p, conv_w, cb2, *dir_args)


def _merge_kernel(x_ref, mc_ref, ml_ref, of_ref, ob_ref, z_ref, hf_ref, hb_ref, y_ref, uc_ref, ud_ref,
                  t0_ref, t1_ref, t2_ref, t3_ref, gn_ref, ones_ref, bg_ref, wb_ref, wo_ref, o_ref, *, n_lat):
    o = of_ref[...] + ob_ref[...]
    ms = jnp.dot(jnp.concatenate(_split(o * o), axis=-1), ones_ref[...], preferred_element_type=F32) * (1.0 / GDN_DV)
    z = z_ref[...]
    ua = (o * lax.rsqrt(ms + RMS_EPS) * gn_ref[...] * (z * _sigmoid(z))).astype(BF16)
    y = y_ref[...]
    gelu = 0.5 * y * (1.0 + jnp.tanh(0.7978845608028654 * (y + 0.044715 * (y * y * y))))
    ub = ((hf_ref[...] + hb_ref[...]) * gelu).astype(BF16)

    merged = None
    for n, (u, t_ref) in enumerate(((ua, t0_ref), (ub, t1_ref), (uc_ref[...], t2_ref), (ud_ref[...], t3_ref))):
        gate = _sigmoid(t_ref[...] + bg_ref[n:n + 1, :].astype(BF16))
        term = gate.astype(F32) * jnp.dot(u, wb_ref[n], preferred_element_type=F32)
        merged = term if merged is None else merged + term
    out = jnp.dot(merged.astype(BF16), wo_ref[...], preferred_element_type=F32)
    g1 = _row_select(x_ref.shape[0], pl.program_id(1), n_lat, mc_ref[2:3, :], ml_ref[2:3, :])
    o_ref[...] = x_ref[...] + g1 * out


def _merge(xs, mod_c, mod_l, gdn_out, lru_out, uc, ud, p32, p16, gdn_norm_g, b_gate, w_branch, w_out, layer, n_lat,
           with_ctx):
    b, s, d = xs.shape
    rows = s if with_ctx else n_lat
    tm = _pick_tile(rows, 544)
    gate_blk = P_GATES // d
    tok = lambda w: pl.BlockSpec((None, tm, w), lambda bi, i: (bi, i, 0))
    ones_n = jnp.concatenate([_head_block_ones(GDN_HEADS, GDN_DV)] * N_SPLIT, axis=0)
    gn = jnp.tile(gdn_norm_g.astype(F32), GDN_HEADS).reshape(1, GDN_W)
    pcol = lambda c: pl.BlockSpec((None, tm, BRANCH_W), lambda bi, i: (bi, i, c))
    in_specs = [tok(d), pl.BlockSpec((8, d), lambda bi, i: (0, 0)), pl.BlockSpec((None, 8, d), lambda bi, i: (bi, 0, 0))]
    in_specs += [tok(BRANCH_W), tok(BRANCH_W), pcol(P_GDN_Z // GDN_W), tok(BRANCH_W), tok(BRANCH_W),
                 pcol(P_LRU_Y // LRU_W), tok(BRANCH_W), tok(BRANCH_W)]
    in_specs += [pl.BlockSpec((None, tm, d), functools.partial(lambda bi, i, n: (bi, i, gate_blk + n), n=n))
                 for n in range(N_BRANCH)]
    in_specs += [_full(gn), _full(ones_n), _full(b_gate),
                 pl.BlockSpec((None,) + w_branch.shape[1:], lambda bi, i: (layer, 0, 0, 0)),
                 pl.BlockSpec((None,) + w_out.shape[1:], lambda bi, i: (layer, 0, 0))]
    return pl.pallas_call(
        functools.partial(_merge_kernel, n_lat=n_lat),
        grid=(b, rows // tm),
        in_specs=in_specs,
        out_specs=tok(d),
        out_shape=jax.ShapeDtypeStruct((b, rows, d), F32),
        input_output_aliases={0: 0} if with_ctx else {},
        compiler_params=_cparams(("parallel", "parallel")),
        name="merge",
    )(xs, mod_c, mod_l, gdn_out[0], gdn_out[1], p32, lru_out[0], lru_out[1], p32, uc, ud, p16, p16, p16, p16,
      gn, ones_n, b_gate, w_branch, w_out)


def _mlp_kernel(x_ref, mc_ref, ml_ref, gn_ref, gf_ref, w1_ref, w2_ref, o_ref, h_ref, acc_ref, *, n_lat, final_norm):
    f = pl.program_id(2)
    tm = x_ref.shape[0]
    i = pl.program_id(1)

    @pl.when(f == 0)
    def _():
        _norm_modulate(x_ref, h_ref, gn_ref, mc_ref, ml_ref, 3, i, n_lat)

    a = jnp.maximum(jnp.dot(h_ref[...], w1_ref[...], preferred_element_type=F32), 0.0)
    part = jnp.dot((a * a).astype(BF16), w2_ref[...], preferred_element_type=F32)

    @pl.when(f == 0)
    def _():
        acc_ref[...] = part

    @pl.when(f > 0)
    def _():
        acc_ref[...] += part

    @pl.when(f == pl.num_programs(2) - 1)
    def _():
        g2 = _row_select(tm, i, n_lat, mc_ref[5:6, :], ml_ref[5:6, :])
        y = x_ref[...] + g2 * acc_ref[...]
        if final_norm:
            y = y * lax.rsqrt(jnp.mean(y * y, axis=-1, keepdims=True) + RMS_EPS) * gf_ref[...]
        o_ref[...] = y


def _mlp(xs, mod_c, mod_l, gain, w1, w2, layer, final_gain, n_lat, final_norm):
    b, rows, d = xs.shape
    dff = w1.shape[-1]
    tm = _pick_tile(rows, 1088)
    tf = 1024
    row = pl.BlockSpec((1, d), lambda bi, i, f: (0, 0))
    return pl.pallas_call(
        functools.partial(_mlp_kernel, n_lat=n_lat, final_norm=final_norm),
        grid=(b, rows // tm, dff // tf),
        in_specs=[pl.BlockSpec((None, tm, d), lambda bi, i, f: (bi, i, 0)),
                  pl.BlockSpec((8, d), lambda bi, i, f: (0, 0)),
                  pl.BlockSpec((None, 8, d), lambda bi, i, f: (bi, 0, 0)),
                  row, row,
                  pl.BlockSpec((None, d, tf), lambda bi, i, f: (layer, 0, f)),
                  pl.BlockSpec((None, tf, d), lambda bi, i, f: (layer, f, 0))],
        out_specs=pl.BlockSpec((None, tm, d), lambda bi, i, f: (bi, i, 0)),
        out_shape=jax.ShapeDtypeStruct((b, rows, d), F32),
        scratch_shapes=[pltpu.VMEM((tm, d), BF16), pltpu.VMEM((tm, d), F32)],
        compiler_params=_cparams(("parallel", "parallel", "arbitrary")),
        name="mlp",
    )(xs, mod_c, mod_l, gain, final_gain, w1, w2)


def kernel(x, c, ctx, c_ctx, mod_w, mod_b, norm1_g, norm2_g, w_in, b_gate, gdn_conv_w, gdn_a_log, gdn_dt_bias,
           gdn_norm_g, lru_conv_w, lru_conv_b, lru_w_r, lru_b_r, lru_w_i, lru_b_i, lru_lambda, mla_q_norm_g,
           mla_w_uq, mla_kv_norm_g, mla_w_ukv, na_rpb, w_branch, w_out, mlp_w1, mlp_w2, final_norm_g):
    bsz, n_tok, d = x.shape
    n_ctx = ctx.shape[1]
    depth = w_in.shape[0]
    assert n_ctx % SEQ_TILE == 0 and n_tok % SEQ_TILE == 0 and n_tok % GRID_W == 0
    na_meta, na_rel_rows, na_cols = _na_geometry(n_tok // GRID_W)
    na_meta = jnp.asarray(na_meta)
    cos, sin = _rope_tables(n_tok, n_ctx)

    n_rows = -(-(bsz + 1) // 8) * 8
    cc = jnp.zeros((n_rows, d), F32).at[:bsz].set(c).at[bsz].set(c_ctx)
    final_gain = final_norm_g.reshape(1, d)

    w_in_all = _arrange_w_in(w_in)
    wb_all = w_branch.astype(BF16)
    wo_all = w_out.astype(BF16)
    w1_all = mlp_w1.astype(BF16)
    w2_all = mlp_w2.astype(BF16)

    xs = jnp.concatenate([x, ctx], axis=1)
    for l in range(depth):
        need_ctx = l < depth - 1
        mod = _modulation(cc, mod_w, mod_b[l], l).reshape(n_rows, N_MOD, d)
        pad = jnp.zeros((8 - N_MOD, d), F32)
        mod_c = jnp.concatenate([mod[bsz], pad], axis=0)
        mod_l = jnp.concatenate([mod[:bsz], jnp.broadcast_to(pad, (bsz, 8 - N_MOD, d))], axis=1)

        wq, wk, wv, place = _arrange_mla(mla_w_uq[l], mla_w_ukv[l])
        gq = mla_q_norm_g[l].reshape(1, -1)
        gkv = mla_kv_norm_g[l].reshape(1, -1)
        g1n = norm1_g[l].reshape(1, d)
        g2n = norm2_g[l].reshape(1, d)

        p32, p16 = _inproj(xs, mod_c, mod_l, g1n, w_in_all, l, n_tok)

        gq_, gk_, gv_, gbeta, gg = _gdn_prep(p32, gdn_conv_w[l], gdn_a_log[l], gdn_dt_bias[l], n_tok)
        o_f, o_b = _gdn_scan(gq_, gk_, gv_, gbeta, gg, n_tok)
        h_f, h_b = _lru_scan(p32, lru_conv_w[l], lru_conv_b[l], lru_w_r[l], lru_b_r[l], lru_w_i[l], lru_b_i[l],
                             lru_lambda[l], n_tok)

        mq, mk, mv = _mla_prep(p32, p16, cos, sin, gq, gkv, wq, wk, wv, place)
        uc = _mla_flash(mq, mk, mv, n_tok)
        if need_ctx:
            uc = _mla_flash(mq, mk, mv, n_tok, prev=uc)
        ud = _na_attention(p16, na_meta, _na_bias_table(na_rpb[l], na_rel_rows, na_cols), n_tok, need_ctx)

        xs = _merge(xs, mod_c, mod_l, (o_f, o_b), (h_f, h_b), uc, ud, p32, p16, gdn_norm_g[l], b_gate[l],
                    wb_all, wo_all, l, n_tok, need_ctx)
        xs = _mlp(xs, mod_c, mod_l, g2n, w1_all, w2_all, l, final_gain, n_tok, l == depth - 1)
    return xs
```

```python
import functools

import jax
import jax.numpy as jnp
import numpy as np
from jax import lax
from jax.experimental import pallas as pl
from jax.experimental.pallas import tpu as pltpu

F32 = jnp.float32
BF16 = jnp.bfloat16

GRID_W = 64
N_MOD = 6
RMS_EPS = 1e-6
GDN_HEADS = 4
GDN_DK = 64
GDN_DV = 64
GDN_CHUNK = 64
GDN_W = GDN_HEADS * GDN_DV
GDN_CONV = 4
LRU_W = 256
LRU_BLOCKS = 4
LRU_BLOCK_W = LRU_W // LRU_BLOCKS
LRU_CONV = 4
LRU_C = 8.0
MLA_HEADS = 4
MLA_Q_RANK = 256
MLA_KV_RANK = 128
MLA_NOPE = 64
MLA_ROPE = 32
MLA_V = 64
MLA_SLOT = 128
ROPE_BASE = 10000.0
LOG2_E = 1.4426950408889634
NA_HEADS = 4
NA_DH = 64
NA_W = NA_HEADS * NA_DH
NA_WIN_ROWS = 8
NA_WIN_COLS = 16
N_BRANCH = 4
BRANCH_W = 256

SEQ_TILE = 256
HALO = 8

_REF_COLS = {}
_off = 0
for _name, _w in (('gdn_qkv', 3 * GDN_W), ('gdn_z', GDN_W), ('gdn_beta', 2 * GDN_HEADS), ('gdn_a', 2 * GDN_HEADS),
                  ('lru_x', LRU_W), ('lru_y', LRU_W), ('mla_q', MLA_Q_RANK), ('mla_kv', MLA_KV_RANK),
                  ('mla_kr', MLA_ROPE), ('na_qkv', 3 * NA_W)):
    _REF_COLS[_name] = (_off, _w)
    _off += _w
N_MIX_COLS = _off

P_GDN_QKV = 0
P_GDN_Z = 768
P_LRU_X = 1024
P_LRU_Y = 1280
P_MLA_KV = 1536
P_MLA_KR = 1664
P_GDN_BA = 1792
P32_COLS = 2048
P_MLA_Q = 0
P_NA_QKV = 256
P_GATES = 1024
P16_COLS = P_GATES + N_BRANCH * 1024
PROJ_TILE = 1024

VMEM_LIMIT = 52 * 1024 * 1024


def _cparams(sem):
    return pltpu.CompilerParams(dimension_semantics=sem, vmem_limit_bytes=VMEM_LIMIT)


def _pick_tile(n, cap):
    best = 8
    for t in range(8, min(n, cap) + 1, 8):
        if n % t == 0:
            best = t
    return best


def _full(a):
    return pl.BlockSpec(a.shape, lambda *_: (0,) * a.ndim)


N_SPLIT = 2


def _split(x):
    hi = x.astype(BF16)
    lo = (x - hi.astype(F32)).astype(BF16)
    return hi, lo


def _sigmoid(x):
    return 0.5 * jnp.tanh(0.5 * x) + 0.5


def _arrange_w_in(w_in):
    pieces, pos = [], 0

    def put(dst, block):
        nonlocal pos
        if dst > pos:
            pieces.append(jnp.zeros(w_in.shape[:-1] + (dst - pos,), w_in.dtype))
        pieces.append(block)
        pos = dst + block.shape[-1]

    ref = lambda name: w_in[..., _REF_COLS[name][0]:_REF_COLS[name][0] + _REF_COLS[name][1]]
    for name, dst in (('gdn_qkv', P_GDN_QKV), ('gdn_z', P_GDN_Z), ('lru_x', P_LRU_X), ('lru_y', P_LRU_Y),
                      ('mla_kv', P_MLA_KV), ('mla_kr', P_MLA_KR), ('gdn_beta', P_GDN_BA),
                      ('gdn_a', P_GDN_BA + 2 * GDN_HEADS)):
        put(dst, ref(name))
    put(P32_COLS + P_MLA_Q, ref('mla_q'))
    put(P32_COLS + P_NA_QKV, ref('na_qkv'))
    put(P32_COLS + P_GATES, w_in[..., N_MIX_COLS:])
    assert pos == P32_COLS + P16_COLS
    return jnp.concatenate(pieces, axis=-1).astype(BF16)


def _rope_perm():
    q = MLA_ROPE // 4
    src = np.zeros(MLA_ROPE, np.int32)
    sign = np.zeros(MLA_ROPE, np.float32)
    for base in (0, 2 * q):
        for d in range(q):
            src[base + d] = base + d + q
            sign[base + d] = -1.0
            src[base + q + d] = base + d
            sign[base + q + d] = 1.0
    return src, sign


def _arrange_mla(w_uq, w_ukv):
    src, sign = _rope_perm()
    hq = MLA_NOPE + MLA_ROPE
    wq = jnp.zeros((MLA_Q_RANK, 2 * MLA_HEADS * MLA_SLOT), F32)
    wk = jnp.zeros((MLA_KV_RANK, MLA_HEADS * MLA_SLOT), F32)
    wv = jnp.zeros((MLA_KV_RANK, MLA_HEADS * MLA_V), F32)
    place = np.zeros((2 * MLA_SLOT, 2 * MLA_HEADS * MLA_SLOT), np.float32)
    rot_off = MLA_HEADS * MLA_SLOT
    for h in range(MLA_HEADS):
        nope = w_uq[:, h * hq:h * hq + MLA_NOPE]
        pe = w_uq[:, h * hq + MLA_NOPE:(h + 1) * hq]
        s = h * MLA_SLOT
        wq = wq.at[:, s:s + MLA_NOPE].set(nope)
        wq = wq.at[:, s + MLA_NOPE:s + MLA_NOPE + MLA_ROPE].set(pe)
        wq = wq.at[:, rot_off + s + MLA_NOPE:rot_off + s + MLA_NOPE + MLA_ROPE].set(pe[:, src] * sign)
        wk = wk.at[:, s:s + MLA_NOPE].set(w_ukv[:, h * 128:h * 128 + MLA_NOPE])
        wv = wv.at[:, h * MLA_V:(h + 1) * MLA_V].set(w_ukv[:, h * 128 + MLA_NOPE:(h + 1) * 128])
        for d in range(MLA_ROPE):
            for half in (0, MLA_SLOT):
                place[half + d, s + MLA_NOPE + d] = 1.0
                place[half + src[d], rot_off + s + MLA_NOPE + d] = sign[d]
    return wq.astype(BF16), wk.astype(BF16), wv.astype(BF16), jnp.asarray(place, BF16)


def _rope_tables(n_tok, n_ctx):
    cos = np.ones((n_tok + n_ctx, MLA_SLOT), np.float32)
    sin = np.zeros((n_tok + n_ctx, MLA_SLOT), np.float32)
    t = np.arange(n_tok)
    row = (t // GRID_W).astype(np.float32)
    col = (t % GRID_W).astype(np.float32)
    n_freq = MLA_ROPE // 4
    inv = (ROPE_BASE ** (-np.arange(n_freq, dtype=np.float32) / n_freq)).astype(np.float32)
    ar = row[:, None] * inv
    ac = col[:, None] * inv
    ang = np.concatenate([ar, ar, ac, ac], axis=-1).astype(np.float32)
    cos[:n_tok, MLA_NOPE:MLA_NOPE + MLA_ROPE] = np.cos(ang)
    sin[:n_tok, MLA_NOPE:MLA_NOPE + MLA_ROPE] = np.sin(ang)
    return jnp.asarray(cos), jnp.asarray(sin)


def _head_block_ones(n_heads, width):
    m = np.kron(np.eye(n_heads, dtype=np.float32), np.ones((width, width), np.float32))
    return jnp.asarray(m, BF16)


def _mod_kernel(c_ref, w_ref, b_ref, o_ref):
    c = c_ref[...]
    s = c * _sigmoid(c)
    o_ref[...] = jnp.dot(s, w_ref[...], preferred_element_type=F32) + b_ref[...]


def _modulation(cc, mod_w, mod_b, layer):
    r, d = cc.shape
    n = mod_w.shape[-1]
    tn = 1024
    return pl.pallas_call(
        _mod_kernel,
        grid=(n // tn,),
        in_specs=[pl.BlockSpec((r, d), lambda j: (0, 0)),
                  pl.BlockSpec((None, d, tn), lambda j: (layer, 0, j)),
                  pl.BlockSpec((1, tn), lambda j: (0, j))],
        out_specs=pl.BlockSpec((r, tn), lambda j: (0, j)),
        out_shape=jax.ShapeDtypeStruct((r, n), F32),
        compiler_params=_cparams(("arbitrary",)),
        name="modulation",
    )(cc, mod_w, mod_b.reshape(1, n))


def _row_select(tile_rows, tile_idx, n_lat, ctx_vec, lat_vec):
    row = tile_idx * tile_rows + lax.broadcasted_iota(jnp.int32, (tile_rows, 1), 0)
    return jnp.where(row < n_lat, lat_vec, ctx_vec)


def _norm_modulate(x_ref, h_ref, gain_ref, mc_ref, ml_ref, shift_row, tile_idx, n_lat):
    tm = x_ref.shape[0]
    x = x_ref[...]
    xn = x * lax.rsqrt(jnp.mean(x * x, axis=-1, keepdims=True) + RMS_EPS)
    sh_l, sh_c = ml_ref[shift_row:shift_row + 1, :], mc_ref[shift_row:shift_row + 1, :]
    amp_l = gain_ref[...] * (1.0 + ml_ref[shift_row + 1:shift_row + 2, :])
    amp_c = gain_ref[...] * (1.0 + mc_ref[shift_row + 1:shift_row + 2, :])
    all_latent = (tile_idx + 1) * tm <= n_lat

    @pl.when(all_latent)
    def _():
        h_ref[...] = (xn * amp_l + sh_l).astype(h_ref.dtype)

    @pl.when(jnp.logical_not(all_latent))
    def _():
        amp = _row_select(tm, tile_idx, n_lat, amp_c, amp_l)
        shift = _row_select(tm, tile_idx, n_lat, sh_c, sh_l)
        h_ref[...] = (xn * amp + shift).astype(h_ref.dtype)


def _inproj_kernel(x_ref, mc_ref, ml_ref, g_ref, w_ref, o32_ref, o16_ref, h_ref, *, n_lat):
    j = pl.program_id(2)
    n32 = P32_COLS // PROJ_TILE

    @pl.when(j == 0)
    def _():
        _norm_modulate(x_ref, h_ref, g_ref, mc_ref, ml_ref, 0, pl.program_id(1), n_lat)

    acc = jnp.dot(h_ref[...], w_ref[...], preferred_element_type=F32)

    @pl.when(j < n32)
    def _():
        o32_ref[...] = acc

    @pl.when(j >= n32)
    def _():
        o16_ref[...] = acc.astype(BF16)


def _inproj(xs, mod_c, mod_l, gain, w, layer, n_lat):
    b, s, d = xs.shape
    tm = _pick_tile(s, 1088)
    tn = PROJ_TILE
    n32 = P32_COLS // tn
    return pl.pallas_call(
        functools.partial(_inproj_kernel, n_lat=n_lat),
        grid=(b, s // tm, (P32_COLS + P16_COLS) // tn),
        in_specs=[pl.BlockSpec((None, tm, d), lambda bi, i, j: (bi, i, 0)),
                  pl.BlockSpec((8, d), lambda bi, i, j: (0, 0)),
                  pl.BlockSpec((None, 8, d), lambda bi, i, j: (bi, 0, 0)),
                  pl.BlockSpec((1, d), lambda bi, i, j: (0, 0)),
                  pl.BlockSpec((None, d, tn), lambda bi, i, j: (layer, 0, j))],
        out_specs=[pl.BlockSpec((None, tm, tn), lambda bi, i, j: (bi, i, jnp.minimum(j, n32 - 1))),
                   pl.BlockSpec((None, tm, tn), lambda bi, i, j: (bi, i, jnp.maximum(j - n32, 0)))],
        out_shape=[jax.ShapeDtypeStruct((b, s, P32_COLS), F32), jax.ShapeDtypeStruct((b, s, P16_COLS), BF16)],
        scratch_shapes=[pltpu.VMEM((tm, d), BF16)],
        compiler_params=_cparams(("parallel", "parallel", "arbitrary")),
        name="inproj",
    )(xs, mod_c, mod_l, gain, w)


def _mla_prep_kernel(ql_ref, kv_ref, cos_ref, sin_ref, gq_ref, gkv_ref, wq_ref, wk_ref, wv_ref, pl_ref,
                     q_ref, k_ref, v_ref, *, scale):
    nslot = MLA_HEADS * MLA_SLOT
    cos = jnp.concatenate([cos_ref[...]] * MLA_HEADS, axis=-1)
    sin = jnp.concatenate([sin_ref[...]] * MLA_HEADS, axis=-1)

    ql = ql_ref[...].astype(F32)
    qn = ql * lax.rsqrt(jnp.mean(ql * ql, axis=-1, keepdims=True) + RMS_EPS) * gq_ref[...]
    q2 = jnp.dot(qn.astype(BF16), wq_ref[...], preferred_element_type=F32)
    q = (q2[:, :nslot] * cos + q2[:, nslot:] * sin) * scale
    q_ref[...] = q.astype(BF16)

    kvkr = kv_ref[...]
    kvl = kvkr[:, :MLA_KV_RANK]
    kr = kvkr[:, MLA_KV_RANK:]
    kvn = (kvl * lax.rsqrt(jnp.mean(kvl * kvl, axis=-1, keepdims=True) + RMS_EPS) * gkv_ref[...]).astype(BF16)
    kn = jnp.dot(kvn, wk_ref[...], preferred_element_type=F32)
    v_ref[...] = jnp.dot(kvn, wv_ref[...], preferred_element_type=F32).astype(BF16)
    kr_hi = kr.astype(BF16)
    kr_lo = (kr - kr_hi.astype(F32)).astype(BF16)
    kr2 = jnp.dot(jnp.concatenate([kr_hi, kr_lo], axis=-1), pl_ref[...], preferred_element_type=F32)
    k_ref[...] = (kn + kr2[:, :nslot] * cos + kr2[:, nslot:] * sin).astype(BF16)


def _mla_prep(p32, p16, cos, sin, gq, gkv, wq, wk, wv, place):
    b, s, _ = p32.shape
    tm = _pick_tile(s, 544)
    nslot = MLA_HEADS * MLA_SLOT
    scale = (MLA_NOPE + MLA_ROPE) ** -0.5 * LOG2_E
    return pl.pallas_call(
        functools.partial(_mla_prep_kernel, scale=scale),
        grid=(b, s // tm),
        in_specs=[pl.BlockSpec((None, tm, MLA_Q_RANK), lambda bi, i: (bi, i, P_MLA_Q // MLA_Q_RANK)),
                  pl.BlockSpec((None, tm, 2 * MLA_SLOT), lambda bi, i: (bi, i, P_MLA_KV // (2 * MLA_SLOT))),
                  pl.BlockSpec((tm, MLA_SLOT), lambda bi, i: (i, 0)),
                  pl.BlockSpec((tm, MLA_SLOT), lambda bi, i: (i, 0)),
                  _full(gq), _full(gkv), _full(wq), _full(wk), _full(wv), _full(place)],
        out_specs=[pl.BlockSpec((None, tm, nslot), lambda bi, i: (bi, i, 0)),
                   pl.BlockSpec((None, tm, nslot), lambda bi, i: (bi, i, 0)),
                   pl.BlockSpec((None, tm, MLA_HEADS * MLA_V), lambda bi, i: (bi, i, 0))],
        out_shape=[jax.ShapeDtypeStruct((b, s, nslot), BF16),
                   jax.ShapeDtypeStruct((b, s, nslot), BF16),
                   jax.ShapeDtypeStruct((b, s, MLA_HEADS * MLA_V), BF16)],
        compiler_params=_cparams(("parallel", "parallel")),
        name="mla_prep",
    )(p16, p32, cos, sin, gq, gkv, wq, wk, wv, place)


def _flash_softmax(h, s, m_ref, l_ref):
    m_prev = m_ref[h]
    m_new = jnp.maximum(m_prev, jnp.max(s, axis=-1, keepdims=True))
    alpha = jnp.exp2(m_prev - m_new)
    p = jnp.exp2(s - jnp.concatenate([m_new] * (s.shape[1] // 128), axis=-1))
    l_ref[h] = alpha * l_ref[h] + jnp.sum(p, axis=-1, keepdims=True)
    m_ref[h] = m_new
    return alpha, p.astype(BF16)


def _mla_flash_kernel(*refs, tk, aliased):
    if aliased:
        q_ref, k_ref, v_ref, _, o_ref, m_ref, l_ref, acc_ref = refs
    else:
        q_ref, k_ref, v_ref, o_ref, m_ref, l_ref, acc_ref = refs
    tq = q_ref.shape[0]
    nk = k_ref.shape[0]
    n_loop = nk // tk
    m_ref[...] = jnp.full(m_ref.shape, -jnp.inf, F32)
    l_ref[...] = jnp.zeros(l_ref.shape, F32)
    acc_ref[...] = jnp.zeros(acc_ref.shape, F32)

    def chunk(rows):
        heads = range(MLA_HEADS)
        hs = [slice(h * MLA_SLOT, (h + 1) * MLA_SLOT) for h in heads]
        vs = [slice((h // 2) * 128, (h // 2) * 128 + 128) for h in heads]
        s = [lax.dot_general(q_ref[:, hs[h]], k_ref[rows, hs[h]], (((1,), (1,)), ((), ())),
                             preferred_element_type=F32) for h in heads]
        ap = [_flash_softmax(h, s[h], m_ref, l_ref) for h in heads]
        for h in heads:
            alpha, p = ap[h]
            acc_ref[h] = alpha * acc_ref[h] + jnp.dot(p, v_ref[rows, vs[h]], preferred_element_type=F32)

    if n_loop:
        def body(j, carry):
            chunk(pl.ds(pl.multiple_of(j * tk, tk), tk))
            return carry
        lax.fori_loop(0, n_loop, body, 0)
    if nk > n_loop * tk:
        chunk(slice(n_loop * tk, nk))

    lane = lax.broadcasted_iota(jnp.int32, (tq, 128), 1)
    outs = []
    for pair in range(MLA_HEADS // 2):
        o0 = acc_ref[2 * pair] / l_ref[2 * pair]
        o1 = acc_ref[2 * pair + 1] / l_ref[2 * pair + 1]
        outs.append(jnp.where(lane < MLA_V, o0, o1))
    o_ref[...] = jnp.concatenate(outs, axis=-1).astype(o_ref.dtype)


def _mla_flash(q, k, v, n_lat, prev=None):
    b, s, nslot = q.shape
    n_ctx = s - n_lat
    nv = MLA_HEADS * MLA_V
    tk = 512
    if prev is None:
        tq = _pick_tile(n_lat, 1024)
        q_off, n_q, kv_rows, kv_blk = 0, n_lat // tq, s, 0
    else:
        assert n_lat % n_ctx == 0
        tq = _pick_tile(n_ctx, 256)
        q_off, n_q, kv_rows, kv_blk = n_lat // tq, n_ctx // tq, n_ctx, n_lat // n_ctx
    in_specs = [pl.BlockSpec((None, tq, nslot), lambda bi, i: (bi, i + q_off, 0)),
                pl.BlockSpec((None, kv_rows, nslot), lambda bi, i: (bi, kv_blk, 0)),
                pl.BlockSpec((None, kv_rows, nv), lambda bi, i: (bi, kv_blk, 0))]
    args = [q, k, v]
    aliases = {}
    if prev is not None:
        in_specs.append(pl.BlockSpec(memory_space=pl.ANY))
        args.append(prev)
        aliases = {3: 0}
    return pl.pallas_call(
        functools.partial(_mla_flash_kernel, tk=tk, aliased=prev is not None),
        grid=(b, n_q),
        in_specs=in_specs,
        out_specs=pl.BlockSpec((None, tq, nv), lambda bi, i: (bi, i + q_off, 0)),
        out_shape=jax.ShapeDtypeStruct((b, s, nv), BF16),
        scratch_shapes=[pltpu.VMEM((MLA_HEADS, tq, 128), F32),
                        pltpu.VMEM((MLA_HEADS, tq, 128), F32),
                        pltpu.VMEM((MLA_HEADS, tq, 128), F32)],
        input_output_aliases=aliases,
        compiler_params=_cparams(("parallel", "arbitrary")),
        name="mla_flash",
    )(*args)


NA_QROWS = 4
NA_SLAB = NA_QROWS + NA_WIN_ROWS


def _na_geometry(rows):
    assert rows % NA_QROWS == 0 and rows >= NA_SLAB
    nblk = rows // NA_QROWS
    qc = np.arange(GRID_W)
    cs = np.clip(qc - NA_WIN_COLS // 2, 0, GRID_W - NA_WIN_COLS)
    col_valid = (qc[None, :] >= cs[:, None]) & (qc[None, :] < cs[:, None] + NA_WIN_COLS)
    rel_c = np.clip(qc[None, :] - qc[:, None] + NA_WIN_COLS - 1, 0, 2 * NA_WIN_COLS - 2)
    onehot_c = (rel_c[None] == np.arange(2 * NA_WIN_COLS - 1)[:, None, None]) & col_valid[None]
    patterns, cls, starts = {}, [], []
    for i in range(nblk):
        r0 = i * NA_QROWS
        start = int(np.clip(r0 - NA_WIN_ROWS // 2, 0, rows - NA_SLAB))
        qr = r0 + np.arange(NA_QROWS)
        rs = np.clip(qr - NA_WIN_ROWS // 2, 0, rows - NA_WIN_ROWS)
        key = (start - r0,) + tuple((rs - r0).tolist())
        if key not in patterns:
            kr = start + np.arange(NA_SLAB)
            row_valid = (kr[None, :] >= rs[:, None]) & (kr[None, :] < rs[:, None] + NA_WIN_ROWS)
            rel_r = np.clip(kr[None, :] - qr[:, None] + NA_WIN_ROWS - 1, 0, 2 * NA_WIN_ROWS - 2)
            patterns[key] = (len(patterns), np.where(row_valid, rel_r, -1))
        cls.append(patterns[key][0])
        starts.append(start)
    ordered = sorted(patterns.values(), key=lambda z: z[0])
    rel_rows = np.stack([z[1] for z in ordered])
    meta = np.stack([np.asarray(cls, np.int32), np.asarray(starts, np.int32)])
    return meta, rel_rows, (onehot_c.astype(np.float32), col_valid)


def _na_bias_table(rpb, rel_rows, col_tables):
    onehot_c, col_valid = col_tables
    bc = jnp.einsum('hrc,cqk->hrqk', rpb.astype(F32), onehot_c, precision=lax.Precision.HIGHEST)
    bc = jnp.where(col_valid, bc, -jnp.inf)
    outside = jnp.full(bc.shape[:1] + bc.shape[2:], -jnp.inf, F32)
    pats = []
    for pat in rel_rows:
        qrows = [jnp.concatenate([bc[:, r] if r >= 0 else outside for r in row], axis=-1) for row in pat]
        pats.append(jnp.concatenate(qrows, axis=1))
    return jnp.stack(pats)


def _na_scores(q_pair, lane, h, k_parts):
    in_head = (lane < NA_DH) if h % 2 == 0 else (lane >= NA_DH)
    qm = jnp.where(in_head, q_pair * (NA_DH ** -0.5), 0.0).astype(BF16)
    return [lax.dot_general(qm, k, (((1,), (1,)), ((), ())), preferred_element_type=F32) for k in k_parts]


def _na_softmax(scores, bias):
    if bias is not None:
        scores = [scores[0] + bias] + scores[1:]
    m = scores[0].max(axis=-1, keepdims=True)
    for s in scores[1:]:
        m = jnp.maximum(m, s.max(axis=-1, keepdims=True))
    probs = [jnp.exp(s - m) for s in scores]
    den = probs[0].sum(axis=-1, keepdims=True)
    for p in probs[1:]:
        den = den + p.sum(axis=-1, keepdims=True)
    return [p.astype(BF16) for p in probs], den


def _na_kernel(meta_ref, q_ref, k_ref, v_ref, bias_ref, o_ref, *, n_lat, with_ctx):
    i = pl.program_id(1)
    nq = q_ref.shape[0]
    n_lat_tiles = n_lat // nq
    n_all = k_ref.shape[0]
    lane = lax.broadcasted_iota(jnp.int32, (nq, 128), 1)

    def run(windowed):
        key_rows = [pl.ds(n_lat, n_all - n_lat)]
        if windowed:
            start = pl.multiple_of(meta_ref[1, jnp.minimum(i, n_lat_tiles - 1)] * GRID_W, GRID_W)
            key_rows = [pl.ds(start, NA_SLAB * GRID_W)] + key_rows
        pair_lanes = [slice((h // 2) * 128, (h // 2 + 1) * 128) for h in range(NA_HEADS)]
        scores = [_na_scores(q_ref[:, pair_lanes[h]], lane, h, [k_ref[r, pair_lanes[h]] for r in key_rows])
                  for h in range(NA_HEADS)]
        soft = [_na_softmax(scores[h], bias_ref[h] if windowed else None) for h in range(NA_HEADS)]
        heads = []
        for h in range(NA_HEADS):
            probs, den = soft[h]
            out = None
            for p, r in zip(probs, key_rows):
                po = jnp.dot(p, v_ref[r, pair_lanes[h]], preferred_element_type=F32)
                out = po if out is None else out + po
            heads.append(out / den)
        outs = [jnp.where(lane < NA_DH, heads[2 * pair], heads[2 * pair + 1]) for pair in range(NA_HEADS // 2)]
        o_ref[...] = jnp.concatenate(outs, axis=-1).astype(o_ref.dtype)

    if with_ctx:
        pl.when(i < n_lat_tiles)(lambda: run(True))
        pl.when(i >= n_lat_tiles)(lambda: run(False))
    else:
        run(True)


def _na_attention(p, meta, table, n_lat, with_ctx_queries):
    b, s, _ = p.shape
    nq = NA_QROWS * GRID_W
    nk = NA_SLAB * GRID_W
    assert (s - n_lat) % nq == 0
    nlt = n_lat // nq
    qb = P_NA_QKV // NA_W
    grid_spec = pltpu.PrefetchScalarGridSpec(
        num_scalar_prefetch=1,
        grid=(b, s // nq if with_ctx_queries else nlt),
        in_specs=[pl.BlockSpec((None, nq, NA_W), lambda bi, i, m: (bi, i, qb)),
                  pl.BlockSpec((None, s, NA_W), lambda bi, i, m: (bi, 0, qb + 1)),
                  pl.BlockSpec((None, s, NA_W), lambda bi, i, m: (bi, 0, qb + 2)),
                  pl.BlockSpec((None, NA_HEADS, nq, nk),
                               lambda bi, i, m: (m[0, jnp.minimum(i, nlt - 1)], 0, 0, 0))],
        out_specs=pl.BlockSpec((None, nq, NA_W), lambda bi, i, m: (bi, i, 0)))
    return pl.pallas_call(
        functools.partial(_na_kernel, n_lat=n_lat, with_ctx=with_ctx_queries),
        grid_spec=grid_spec,
        out_shape=jax.ShapeDtypeStruct((b, s, NA_W), BF16),
        compiler_params=_cparams(("parallel", "arbitrary")),
        name="na_attention",
    )(meta, p, p, p, table)


def _tile_conv(x, prev, nxt, w_ref, tile_idx, n_tiles, n_lat_tiles):
    r = x.shape[0]
    width = w_ref.shape[0]
    left = width // 2
    has_prev = jnp.logical_and(tile_idx != 0, tile_idx != n_lat_tiles)
    has_next = jnp.logical_and(tile_idx != n_tiles - 1, tile_idx != n_lat_tiles - 1)
    prev = jnp.where(has_prev, prev, 0.0)
    nxt = jnp.where(has_next, nxt, 0.0)
    xe = jnp.concatenate([prev, x, nxt], axis=0)
    acc = None
    for j in range(width):
        o = HALO - left + j
        term = xe[o:o + r, :] * w_ref[j:j + 1, :]
        acc = term if acc is None else acc + term
    return acc


def _halo_specs(width, col_block, tile_of):
    per = SEQ_TILE // HALO

    def main(bi, s, *_):
        return (bi, tile_of(s), col_block)

    def prev(bi, s, *_):
        return (bi, jnp.maximum(tile_of(s) * per - 1, 0), col_block)

    def make_next(n_tiles):
        def nxt(bi, s, *_):
            return (bi, jnp.minimum((tile_of(s) + 1) * per, n_tiles * per - 1), col_block)
        return nxt

    return main, prev, make_next


def _gdn_prep_kernel(x_ref, xp_ref, xn_ref, ba_ref, cw_ref, ones_ref, exp_ref, alog_ref, dtb_ref,
                     q_ref, k_ref, v_ref, beta_ref, g_ref, *, n_lat_tiles):
    i = pl.program_id(1)
    y = _tile_conv(x_ref[...], xp_ref[...], xn_ref[...], cw_ref, i, pl.num_programs(1), n_lat_tiles)
    y = y * _sigmoid(y)
    q = y[:, :GDN_W]
    k = y[:, GDN_W:2 * GDN_W]
    v_ref[...] = y[:, 2 * GDN_W:].astype(v_ref.dtype)

    def head_norm(u):
        parts = jnp.concatenate(_split(u * u), axis=-1)
        ss = jnp.dot(parts, ones_ref[...], preferred_element_type=F32)
        return u * lax.rsqrt(ss + RMS_EPS)

    q_ref[...] = (head_norm(q) * (GDN_DK ** -0.5)).astype(q_ref.dtype)
    k_ref[...] = head_norm(k).astype(k_ref.dtype)

    ba = ba_ref[...]
    a = ba + dtb_ref[...]
    softplus = jnp.maximum(a, 0.0) + jnp.log1p(jnp.exp(-jnp.abs(a)))
    lane = lax.broadcasted_iota(jnp.int32, ba.shape, 1)
    compact = jnp.where(lane < 2 * GDN_HEADS, _sigmoid(ba), -jnp.exp(alog_ref[...]) * softplus)
    wide = jnp.dot(jnp.concatenate(_split(compact), axis=-1), exp_ref[...], preferred_element_type=F32)
    half = 2 * GDN_W
    beta_ref[...] = wide[:, :half].astype(beta_ref.dtype)
    g_ref[...] = wide[:, half:]


def _gdn_prep(p, conv_w, a_log, dt_bias, n_lat):
    b, s, _ = p.shape
    n_tiles = s // SEQ_TILE
    main, prev, make_next = _halo_specs(3 * GDN_W, 0, lambda t: t)
    ones_n = jnp.concatenate([_head_block_ones(GDN_HEADS, GDN_DK)] * N_SPLIT, axis=0)
    expand = np.zeros((128, 4 * GDN_W), np.float32)
    for kind in range(2):
        for d in range(2):
            for h in range(GDN_HEADS):
                c0 = kind * 2 * GDN_W + d * GDN_W + h * GDN_DV
                expand[kind * 2 * GDN_HEADS + d * GDN_HEADS + h, c0:c0 + GDN_DV] = 1.0
    expand_n = jnp.asarray(np.concatenate([expand] * N_SPLIT, axis=0), BF16)
    lanes = jnp.zeros((1, 128), F32)
    alog_e = lanes.at[0, 2 * GDN_HEADS:4 * GDN_HEADS].set(a_log.astype(F32).reshape(-1))
    dtb_e = lanes.at[0, 2 * GDN_HEADS:4 * GDN_HEADS].set(dt_bias.astype(F32).reshape(-1))
    tok = lambda w: pl.BlockSpec((None, SEQ_TILE, w), lambda bi, i: (bi, i, 0))
    return pl.pallas_call(
        functools.partial(_gdn_prep_kernel, n_lat_tiles=n_lat // SEQ_TILE),
        grid=(b, n_tiles),
        in_specs=[pl.BlockSpec((None, SEQ_TILE, 3 * GDN_W), main),
                  pl.BlockSpec((None, HALO, 3 * GDN_W), prev),
                  pl.BlockSpec((None, HALO, 3 * GDN_W), make_next(n_tiles)),
                  pl.BlockSpec((None, SEQ_TILE, 128), lambda bi, i: (bi, i, P_GDN_BA // 128)),
                  _full(conv_w), _full(ones_n), _full(expand_n), _full(alog_e), _full(dtb_e)],
        out_specs=[tok(GDN_W), tok(GDN_W), tok(GDN_W), tok(2 * GDN_W), tok(2 * GDN_W)],
        out_shape=[jax.ShapeDtypeStruct((b, s, GDN_W), BF16)] * 3 + [jax.ShapeDtypeStruct((b, s, 2 * GDN_W), BF16),
                                                                    jax.ShapeDtypeStruct((b, s, 2 * GDN_W), F32)],
        compiler_params=_cparams(("parallel", "parallel")),
        name="gdn_prep",
    )(p, p, p, p, conv_w, ones_n, expand_n, alog_e, dtb_e)


GDN_PAIR = 2 * GDN_DK


def _gdn_masks():
    c, w = GDN_CHUNK, GDN_W
    r2, c2 = np.arange(GDN_PAIR)[:, None], np.arange(GDN_PAIR)[None, :]
    bd = ((r2 // c) == (c2 // c)).astype(np.float32)
    i = np.arange(c)[:, None]
    j = (np.arange(w) % c)[None, :]
    level = np.zeros((c, w), np.int32)
    for bit in range(6):
        level += ((i ^ j) >= (1 << bit)).astype(np.int32)
    lvl = np.stack([(level == m).astype(np.float32) for m in range(7)])
    dirm = np.stack([np.stack([(j <= i), (j < i)]), np.stack([(j >= i), (j > i)])]).astype(np.float32)
    tj = (np.arange(N_SPLIT * c) % c)[None, :]
    tri = np.stack([(tj <= i), (tj >= i)]).astype(np.float32)
    return jnp.asarray(bd, BF16), jnp.asarray(lvl), jnp.asarray(dirm), jnp.asarray(tri, BF16)


def _heads_mm(x, y, bd, transpose_rhs=False):
    xb = x.astype(BF16)
    yb = y.astype(BF16)
    outs = []
    for pair in range(GDN_W // GDN_PAIR):
        ls = slice(pair * GDN_PAIR, (pair + 1) * GDN_PAIR)
        w = jnp.concatenate([yb[:, ls], yb[:, ls]], axis=0) * bd
        dims = (((1,), (1,)), ((), ())) if transpose_rhs else (((1,), (0,)), ((), ()))
        outs.append(lax.dot_general(xb[:, ls], w, dims, preferred_element_type=F32))
    return jnp.concatenate(outs, axis=1)


def _gdn_intra(probs, bd, lvl_ref, dirm_ref, tri_ref):
    c = GDN_CHUNK
    n = len(probs)
    eye = lvl_ref[0]
    gc, g_last, decay, gram = [], [], [], []
    for q, k, v, beta, g, rev in probs:
        d = 1 if rev else 0
        gcp = jnp.dot(tri_ref[d], jnp.concatenate(_split(g), axis=0), preferred_element_type=F32)
        gc.append(gcp)
        g_last.append(gcp[0:1, :] if rev else gcp[c - 1:c, :])
        gc_row = jnp.sum(gcp * eye, axis=0, keepdims=True)
        decay.append(dirm_ref[d, 0] * jnp.exp(jnp.minimum(gcp - gc_row, 0.0)))
        gram.append(_heads_mm(jnp.concatenate([k, q], axis=0), k, bd, transpose_rhs=True))
    lower = [dirm_ref[1 if p[5] else 0, 1] * p[3] * gram[x][:c] * decay[x] for x, p in enumerate(probs)]
    a_intra = [gram[x][c:] * decay[x] for x in range(n)]
    t = [eye - lower[x] * lvl_ref[1] for x in range(n)]
    for lev in range(2, 7):
        y = [_heads_mm(t[x], lower[x] * lvl_ref[lev], bd) for x in range(n)]
        z = [_heads_mm(y[x], t[x], bd) for x in range(n)]
        t = [t[x] - z[x] for x in range(n)]
    e_gc = [jnp.exp(gc[x]) for x in range(n)]
    u = [_heads_mm(t[x], p[2] * p[3], bd) for x, p in enumerate(probs)]
    w = [_heads_mm(t[x], p[1] * p[3] * e_gc[x], bd) for x, p in enumerate(probs)]
    wq = [jnp.concatenate([w[x], p[0] * e_gc[x]], axis=0).astype(BF16) for x, p in enumerate(probs)]
    k_dec = [(p[1] * jnp.exp(g_last[x] - gc[x])).astype(BF16) for x, p in enumerate(probs)]
    g_tot = [jnp.exp(g_last[x]) for x in range(n)]
    return u, wq, k_dec, a_intra, g_tot


def _gdn_state_steps(items, bd):
    c = GDN_CHUNK
    pairs = [slice(p * GDN_PAIR, (p + 1) * GDN_PAIR) for p in range(GDN_W // GDN_PAIR)]
    bdf = bd.astype(F32)
    ws_qs = [jnp.concatenate([jnp.dot(wq[:, ls], s_ref[p].astype(BF16), preferred_element_type=F32)
                              for p, ls in enumerate(pairs)], axis=1)
             for _, wq, _, _, _, s_ref in items]
    v_new = [it[0] - ws[:c] for it, ws in zip(items, ws_qs)]
    outs = [ws[c:] + _heads_mm(it[3], vn, bd) for it, ws, vn in zip(items, ws_qs, v_new)]
    for (_, _, k_dec, _, g_tot, s_ref), vn in zip(items, v_new):
        vb = vn.astype(BF16)
        for p, ls in enumerate(pairs):
            kv = lax.dot_general(k_dec[:, ls], vb[:, ls], (((0,), (0,)), ((), ())), preferred_element_type=F32)
            s_ref[p] = s_ref[p] * g_tot[:, ls] + kv * bdf
    return outs


def _gdn_scan_body(qf, kf, vf, bf, gf, qb, kb, vb, bb, gb, bd_ref, lvl_ref, dirm_ref, tri_ref,
                   of_ref, ob_ref, sf_ref, sb_ref):
    bd = bd_ref[...]
    n = SEQ_TILE // GDN_CHUNK
    probs, rows = [], []
    for c in range(n):
        rf = slice(c * GDN_CHUNK, (c + 1) * GDN_CHUNK)
        rb = slice((n - 1 - c) * GDN_CHUNK, (n - c) * GDN_CHUNK)
        probs.append(tuple(r[rf, :].astype(F32) for r in (qf, kf, vf, bf, gf)) + (False,))
        probs.append(tuple(r[rb, :].astype(F32) for r in (qb, kb, vb, bb, gb)) + (True,))
        rows += [rf, rb]
    u, wq, k_dec, a_intra, g_tot = _gdn_intra(probs, bd, lvl_ref, dirm_ref, tri_ref)
    for c in range(n):
        xs = (2 * c, 2 * c + 1)
        outs = _gdn_state_steps([(u[x], wq[x], k_dec[x], a_intra[x], g_tot[x], sb_ref if probs[x][5] else sf_ref)
                                 for x in xs], bd)
        for x, o in zip(xs, outs):
            (ob_ref if probs[x][5] else of_ref)[rows[x], :] = o


def _seq_tile_maps(n_lat_tiles, n_tiles):
    fwd = lambda s: lax.rem(s + n_lat_tiles, n_tiles)
    bwd = lambda s: n_tiles - 1 - s
    return fwd, bwd


def _gdn_scan_specs(q, k, v, beta, g, n_lat):
    b, s, _ = q.shape
    n_tiles = s // SEQ_TILE
    fwd, bwd = _seq_tile_maps(n_lat // SEQ_TILE, n_tiles)
    spec = lambda tile_of, col: pl.BlockSpec((None, SEQ_TILE, GDN_W), lambda bi, t: (bi, tile_of(t), col))
    masks = _gdn_masks()
    in_specs = [spec(fwd, 0)] * 5 + [spec(bwd, 0)] * 3 + [spec(bwd, 1)] * 2 + [_full(m) for m in masks]
    assert len(in_specs) == N_GDN_IN
    return (in_specs, [q, k, v, beta, g, q, k, v, beta, g, *masks], [spec(fwd, 0), spec(bwd, 0)],
            [jax.ShapeDtypeStruct((b, s, GDN_W), F32)] * 2,
            [pltpu.VMEM((GDN_W // GDN_PAIR, GDN_PAIR, GDN_PAIR), F32)] * 2)


def _lru_tile_prep(x_ref, xp_ref, xn_ref, cw_ref, cb_ref, wg_ref, bg_ref, nla_ref, a_ref, b_ref,
                   tile_idx, n_tiles, n_lat_tiles):
    xb = _tile_conv(x_ref[...], xp_ref[...], xn_ref[...], cw_ref, tile_idx, n_tiles, n_lat_tiles) + cb_ref[...]
    gates = _sigmoid(jnp.dot(xb.astype(BF16), wg_ref[...], preferred_element_type=F32) + bg_ref[...])
    log_a = nla_ref[...] * gates[:, :LRU_W]
    a_ref[...] = jnp.exp(log_a)
    th = jnp.tanh(log_a)
    b_ref[...] = jnp.sqrt(-2.0 * th / (1.0 - th)) * gates[:, LRU_W:] * xb


def _scan_group(a, b, h, row, reverse):
    for d in (1, 2, 4):
        if reverse:
            keep = row < 8 - d
            shift = 8 - d
        else:
            keep = row >= d
            shift = d
        a_s = jnp.where(keep, pltpu.roll(a, shift, 0), 1.0)
        b_s = jnp.where(keep, pltpu.roll(b, shift, 0), 0.0)
        b = a * b_s + b
        a = a * a_s
    return a * h + b


def _lru_scan_body(xf, xfp, xfn, xb, xbp, xbn, cw_ref, cb_ref, wgf, bgf, nlaf, wgb, bgb, nlab,
                   hf_ref, hb_ref, af_ref, bf_ref, ab_ref, bb_ref, cf_ref, cbk_ref, *, n_lat_tiles):
    s = pl.program_id(1)
    n_tiles = pl.num_programs(1)
    t_f, t_b = (m(s) for m in _seq_tile_maps(n_lat_tiles, n_tiles))
    _lru_tile_prep(xf, xfp, xfn, cw_ref, cb_ref, wgf, bgf, nlaf, af_ref, bf_ref, t_f, n_tiles, n_lat_tiles)
    _lru_tile_prep(xb, xbp, xbn, cw_ref, cb_ref, wgb, bgb, nlab, ab_ref, bb_ref, t_b, n_tiles, n_lat_tiles)

    n_groups = SEQ_TILE // 8
    row = lax.broadcasted_iota(jnp.int32, (8, LRU_W), 0)

    h_f, h_b = cf_ref[...], cbk_ref[...]
    for gi in range(n_groups):
        rf = slice(gi * 8, gi * 8 + 8)
        rb = slice((n_groups - 1 - gi) * 8, (n_groups - gi) * 8)
        out_f = _scan_group(af_ref[rf, :], bf_ref[rf, :], h_f, row, False)
        out_b = _scan_group(ab_ref[rb, :], bb_ref[rb, :], h_b, row, True)
        hf_ref[rf, :] = out_f
        hb_ref[rb, :] = out_b
        h_f = jnp.broadcast_to(out_f[7:8, :], (8, LRU_W))
        h_b = jnp.broadcast_to(out_b[0:1, :], (8, LRU_W))
    cf_ref[...] = h_f
    cbk_ref[...] = h_b


N_LRU_IN, N_GDN_IN, N_LRU_SCRATCH = 14, 14, 6


def _seq_mixers_kernel(*refs, n_lat_tiles):
    lru_in = refs[:N_LRU_IN]
    gdn_in = refs[N_LRU_IN:N_LRU_IN + N_GDN_IN]
    hf_ref, hb_ref, of_ref, ob_ref = refs[N_LRU_IN + N_GDN_IN:N_LRU_IN + N_GDN_IN + 4]
    scratch = refs[N_LRU_IN + N_GDN_IN + 4:]
    lru_scratch, gdn_scratch = scratch[:N_LRU_SCRATCH], scratch[N_LRU_SCRATCH:]

    @pl.when(pl.program_id(1) == 0)
    def _():
        for ref in lru_scratch[4:] + gdn_scratch:
            ref[...] = jnp.zeros(ref.shape, ref.dtype)

    _lru_scan_body(*lru_in, hf_ref, hb_ref, *lru_scratch, n_lat_tiles=n_lat_tiles)
    _gdn_scan_body(*gdn_in, of_ref, ob_ref, *gdn_scratch)


def _lru_scan_specs(p, conv_w, conv_b, w_r, b_r, w_i, b_i, lam, n_lat):
    b, s, _ = p.shape
    n_tiles = s // SEQ_TILE
    nlt = n_lat // SEQ_TILE
    fwd, bwd = _seq_tile_maps(nlt, n_tiles)
    col = P_LRU_X // LRU_W
    specs = []
    for tile_of in (fwd, bwd):
        main, prev, make_next = _halo_specs(LRU_W, col, tile_of)
        specs += [pl.BlockSpec((None, SEQ_TILE, LRU_W), main), pl.BlockSpec((None, HALO, LRU_W), prev),
                  pl.BlockSpec((None, HALO, LRU_W), make_next(n_tiles))]

    def blockdiag(w):
        return jax.scipy.linalg.block_diag(*[w[n] for n in range(LRU_BLOCKS)])

    dir_args = []
    for d in range(2):
        wg = jnp.concatenate([blockdiag(w_r[d]), blockdiag(w_i[d])], axis=1).astype(BF16)
        bg = jnp.concatenate([b_r[d], b_i[d]]).astype(F32).reshape(1, 2 * LRU_W)
        nla = (-LRU_C * jax.nn.softplus(-lam[d].astype(F32))).reshape(1, LRU_W)
        dir_args += [wg, bg, nla]
    cb2 = conv_b.reshape(1, LRU_W)
    out_spec = lambda tile_of: pl.BlockSpec((None, SEQ_TILE, LRU_W), lambda bi, t: (bi, tile_of(t), 0))
    in_specs = specs + [_full(conv_w), _full(cb2)] + [_full(a) for a in dir_args]
    assert len(in_specs) == N_LRU_IN
    return (in_specs, [p, p, p, p, p, p, conv_w, cb2, *dir_args], [out_spec(fwd), out_spec(bwd)],
            [jax.ShapeDtypeStruct((b, s, LRU_W), F32)] * 2,
            [pltpu.VMEM((SEQ_TILE, LRU_W), F32)] * 4 + [pltpu.VMEM((8, LRU_W), F32)] * 2)


def _seq_mixers(lru_parts, gdn_parts, bsz, n_tiles, n_lat):
    in_specs, args, out_specs, out_shape, scratch = ([*a, *b] for a, b in zip(lru_parts, gdn_parts))
    return pl.pallas_call(
        functools.partial(_seq_mixers_kernel, n_lat_tiles=n_lat // SEQ_TILE),
        grid=(bsz, n_tiles),
        in_specs=in_specs,
        out_specs=out_specs,
        out_shape=out_shape,
        scratch_shapes=scratch,
        compiler_params=_cparams(("parallel", "arbitrary")),
        name="seq_mixers",
    )(*args)


def _merge_kernel(x_ref, mc_ref, ml_ref, of_ref, ob_ref, z_ref, hf_ref, hb_ref, y_ref, uc_ref, ud_ref,
                  t0_ref, t1_ref, t2_ref, t3_ref, gn_ref, ones_ref, bg_ref, wb_ref, wo_ref, o_ref, *, n_lat):
    o = of_ref[...] + ob_ref[...]
    ms = jnp.dot(jnp.concatenate(_split(o * o), axis=-1), ones_ref[...], preferred_element_type=F32) * (1.0 / GDN_DV)
    z = z_ref[...]
    ua = (o * lax.rsqrt(ms + RMS_EPS) * gn_ref[...] * (z * _sigmoid(z))).astype(BF16)
    y = y_ref[...]
    gelu = 0.5 * y * (1.0 + jnp.tanh(0.7978845608028654 * (y + 0.044715 * (y * y * y))))
    ub = ((hf_ref[...] + hb_ref[...]) * gelu).astype(BF16)

    merged = None
    for n, (u, t_ref) in enumerate(((ua, t0_ref), (ub, t1_ref), (uc_ref[...], t2_ref), (ud_ref[...], t3_ref))):
        gate = _sigmoid(t_ref[...] + bg_ref[n:n + 1, :].astype(BF16))
        term = gate.astype(F32) * jnp.dot(u, wb_ref[n], preferred_element_type=F32)
        merged = term if merged is None else merged + term
    out = jnp.dot(merged.astype(BF16), wo_ref[...], preferred_element_type=F32)
    g1 = _row_select(x_ref.shape[0], pl.program_id(1), n_lat, mc_ref[2:3, :], ml_ref[2:3, :])
    o_ref[...] = x_ref[...] + g1 * out


def _merge(xs, mod_c, mod_l, gdn_out, lru_out, uc, ud, p32, p16, gdn_norm_g, b_gate, w_branch, w_out, layer, n_lat,
           with_ctx):
    b, s, d = xs.shape
    rows = s if with_ctx else n_lat
    tm = _pick_tile(rows, 544)
    gate_blk = P_GATES // d
    tok = lambda w: pl.BlockSpec((None, tm, w), lambda bi, i: (bi, i, 0))
    ones_n = jnp.concatenate([_head_block_ones(GDN_HEADS, GDN_DV)] * N_SPLIT, axis=0)
    gn = jnp.tile(gdn_norm_g.astype(F32), GDN_HEADS).reshape(1, GDN_W)
    pcol = lambda c: pl.BlockSpec((None, tm, BRANCH_W), lambda bi, i: (bi, i, c))
    in_specs = [tok(d), pl.BlockSpec((8, d), lambda bi, i: (0, 0)), pl.BlockSpec((None, 8, d), lambda bi, i: (bi, 0, 0))]
    in_specs += [tok(BRANCH_W), tok(BRANCH_W), pcol(P_GDN_Z // GDN_W), tok(BRANCH_W), tok(BRANCH_W),
                 pcol(P_LRU_Y // LRU_W), tok(BRANCH_W), tok(BRANCH_W)]
    in_specs += [pl.BlockSpec((None, tm, d), functools.partial(lambda bi, i, n: (bi, i, gate_blk + n), n=n))
                 for n in range(N_BRANCH)]
    in_specs += [_full(gn), _full(ones_n), _full(b_gate),
                 pl.BlockSpec((None,) + w_branch.shape[1:], lambda bi, i: (layer, 0, 0, 0)),
                 pl.BlockSpec((None,) + w_out.shape[1:], lambda bi, i: (layer, 0, 0))]
    return pl.pallas_call(
        functools.partial(_merge_kernel, n_lat=n_lat),
        grid=(b, rows // tm),
        in_specs=in_specs,
        out_specs=tok(d),
        out_shape=jax.ShapeDtypeStruct((b, rows, d), F32),
        input_output_aliases={0: 0} if with_ctx else {},
        compiler_params=_cparams(("parallel", "parallel")),
        name="merge",
    )(xs, mod_c, mod_l, gdn_out[0], gdn_out[1], p32, lru_out[0], lru_out[1], p32, uc, ud, p16, p16, p16, p16,
      gn, ones_n, b_gate, w_branch, w_out)


def _mlp_kernel(x_ref, mc_ref, ml_ref, gn_ref, gf_ref, w1_ref, w2_ref, o_ref, h_ref, acc_ref, *, n_lat, final_norm):
    f = pl.program_id(2)
    tm = x_ref.shape[0]
    i = pl.program_id(1)

    @pl.when(f == 0)
    def _():
        _norm_modulate(x_ref, h_ref, gn_ref, mc_ref, ml_ref, 3, i, n_lat)

    a = jnp.maximum(jnp.dot(h_ref[...], w1_ref[...], preferred_element_type=F32), 0.0)
    part = jnp.dot((a * a).astype(BF16), w2_ref[...], preferred_element_type=F32)

    @pl.when(f == 0)
    def _():
        acc_ref[...] = part

    @pl.when(f > 0)
    def _():
        acc_ref[...] += part

    @pl.when(f == pl.num_programs(2) - 1)
    def _():
        g2 = _row_select(tm, i, n_lat, mc_ref[5:6, :], ml_ref[5:6, :])
        y = x_ref[...] + g2 * acc_ref[...]
        if final_norm:
            y = y * lax.rsqrt(jnp.mean(y * y, axis=-1, keepdims=True) + RMS_EPS) * gf_ref[...]
        o_ref[...] = y


def _mlp(xs, mod_c, mod_l, gain, w1, w2, layer, final_gain, n_lat, final_norm):
    b, rows, d = xs.shape
    dff = w1.shape[-1]
    tm = _pick_tile(rows, 1088)
    tf = 1024
    row = pl.BlockSpec((1, d), lambda bi, i, f: (0, 0))
    return pl.pallas_call(
        functools.partial(_mlp_kernel, n_lat=n_lat, final_norm=final_norm),
        grid=(b, rows // tm, dff // tf),
        in_specs=[pl.BlockSpec((None, tm, d), lambda bi, i, f: (bi, i, 0)),
                  pl.BlockSpec((8, d), lambda bi, i, f: (0, 0)),
                  pl.BlockSpec((None, 8, d), lambda bi, i, f: (bi, 0, 0)),
                  row, row,
                  pl.BlockSpec((None, d, tf), lambda bi, i, f: (layer, 0, f)),
                  pl.BlockSpec((None, tf, d), lambda bi, i, f: (layer, f, 0))],
        out_specs=pl.BlockSpec((None, tm, d), lambda bi, i, f: (bi, i, 0)),
        out_shape=jax.ShapeDtypeStruct((b, rows, d), F32),
        scratch_shapes=[pltpu.VMEM((tm, d), BF16), pltpu.VMEM((tm, d), F32)],
        compiler_params=_cparams(("parallel", "parallel", "arbitrary")),
        name="mlp",
    )(xs, mod_c, mod_l, gain, final_gain, w1, w2)


def kernel(x, c, ctx, c_ctx, mod_w, mod_b, norm1_g, norm2_g, w_in, b_gate, gdn_conv_w, gdn_a_log, gdn_dt_bias,
           gdn_norm_g, lru_conv_w, lru_conv_b, lru_w_r, lru_b_r, lru_w_i, lru_b_i, lru_lambda, mla_q_norm_g,
           mla_w_uq, mla_kv_norm_g, mla_w_ukv, na_rpb, w_branch, w_out, mlp_w1, mlp_w2, final_norm_g):
    bsz, n_tok, d = x.shape
    n_ctx = ctx.shape[1]
    depth = w_in.shape[0]
    assert n_ctx % SEQ_TILE == 0 and n_tok % SEQ_TILE == 0 and n_tok % GRID_W == 0
    na_meta, na_rel_rows, na_cols = _na_geometry(n_tok // GRID_W)
    na_meta = jnp.asarray(na_meta)
    cos, sin = _rope_tables(n_tok, n_ctx)

    n_rows = -(-(bsz + 1) // 8) * 8
    cc = jnp.zeros((n_rows, d), F32).at[:bsz].set(c).at[bsz].set(c_ctx)
    final_gain = final_norm_g.reshape(1, d)

    w_in_all = _arrange_w_in(w_in)
    wb_all = w_branch.astype(BF16)
    wo_all = w_out.astype(BF16)
    w1_all = mlp_w1.astype(BF16)
    w2_all = mlp_w2.astype(BF16)

    xs = jnp.concatenate([x, ctx], axis=1)
    for l in range(depth):
        need_ctx = l < depth - 1
        mod = _modulation(cc, mod_w, mod_b[l], l).reshape(n_rows, N_MOD, d)
        pad = jnp.zeros((8 - N_MOD, d), F32)
        mod_c = jnp.concatenate([mod[bsz], pad], axis=0)
        mod_l = jnp.concatenate([mod[:bsz], jnp.broadcast_to(pad, (bsz, 8 - N_MOD, d))], axis=1)

        wq, wk, wv, place = _arrange_mla(mla_w_uq[l], mla_w_ukv[l])
        gq = mla_q_norm_g[l].reshape(1, -1)
        gkv = mla_kv_norm_g[l].reshape(1, -1)
        g1n = norm1_g[l].reshape(1, d)
        g2n = norm2_g[l].reshape(1, d)

        p32, p16 = _inproj(xs, mod_c, mod_l, g1n, w_in_all, l, n_tok)

        gq_, gk_, gv_, gbeta, gg = _gdn_prep(p32, gdn_conv_w[l], gdn_a_log[l], gdn_dt_bias[l], n_tok)
        h_f, h_b, o_f, o_b = _seq_mixers(
            _lru_scan_specs(p32, lru_conv_w[l], lru_conv_b[l], lru_w_r[l], lru_b_r[l], lru_w_i[l], lru_b_i[l],
                            lru_lambda[l], n_tok),
            _gdn_scan_specs(gq_, gk_, gv_, gbeta, gg, n_tok), bsz, (n_tok + n_ctx) // SEQ_TILE, n_tok)

        mq, mk, mv = _mla_prep(p32, p16, cos, sin, gq, gkv, wq, wk, wv, place)
        uc = _mla_flash(mq, mk, mv, n_tok)
        if need_ctx:
            uc = _mla_flash(mq, mk, mv, n_tok, prev=uc)
        ud = _na_attention(p16, na_meta, _na_bias_table(na_rpb[l], na_rel_rows, na_cols), n_tok, need_ctx)

        xs = _merge(xs, mod_c, mod_l, (o_f, o_b), (h_f, h_b), uc, ud, p32, p16, gdn_norm_g[l], b_gate[l],
                    wb_all, wo_all, l, n_tok, need_ctx)
        xs = _mlp(xs, mod_c, mod_l, g2n, w1_all, w2_all, l, final_gain, n_tok, l == depth - 1)
    return xs
```

```python
import functools

import jax
import jax.numpy as jnp
import numpy as np
from jax import lax
from jax.experimental import pallas as pl
from jax.experimental.pallas import tpu as pltpu

F32 = jnp.float32
BF16 = jnp.bfloat16

GRID_W = 64
N_MOD = 6
RMS_EPS = 1e-6
GDN_HEADS = 4
GDN_DK = 64
GDN_DV = 64
GDN_CHUNK = 64
GDN_W = GDN_HEADS * GDN_DV
GDN_CONV = 4
LRU_W = 256
LRU_BLOCKS = 4
LRU_BLOCK_W = LRU_W // LRU_BLOCKS
LRU_CONV = 4
LRU_C = 8.0
MLA_HEADS = 4
MLA_Q_RANK = 256
MLA_KV_RANK = 128
MLA_NOPE = 64
MLA_ROPE = 32
MLA_V = 64
MLA_SLOT = 128
ROPE_BASE = 10000.0
LOG2_E = 1.4426950408889634
NA_HEADS = 4
NA_DH = 64
NA_W = NA_HEADS * NA_DH
NA_WIN_ROWS = 8
NA_WIN_COLS = 16
N_BRANCH = 4
BRANCH_W = 256

SEQ_TILE = 256
HALO = 8

_REF_COLS = {}
_off = 0
for _name, _w in (('gdn_qkv', 3 * GDN_W), ('gdn_z', GDN_W), ('gdn_beta', 2 * GDN_HEADS), ('gdn_a', 2 * GDN_HEADS),
                  ('lru_x', LRU_W), ('lru_y', LRU_W), ('mla_q', MLA_Q_RANK), ('mla_kv', MLA_KV_RANK),
                  ('mla_kr', MLA_ROPE), ('na_qkv', 3 * NA_W)):
    _REF_COLS[_name] = (_off, _w)
    _off += _w
N_MIX_COLS = _off

P_GDN_QKV = 0
P_GDN_Z = 768
P_LRU_X = 1024
P_LRU_Y = 1280
P_MLA_KV = 1536
P_MLA_KR = 1664
P_GDN_BA = 1792
P32_COLS = 2048
P_MLA_Q = 0
P_NA_QKV = 256
P_GATES = 1024
P16_COLS = P_GATES + N_BRANCH * 1024
PROJ_TILE = 1024

VMEM_LIMIT = 52 * 1024 * 1024


def _cparams(sem):
    return pltpu.CompilerParams(dimension_semantics=sem, vmem_limit_bytes=VMEM_LIMIT)


def _pick_tile(n, cap):
    best = 8
    for t in range(8, min(n, cap) + 1, 8):
        if n % t == 0:
            best = t
    return best


def _full(a):
    return pl.BlockSpec(a.shape, lambda *_: (0,) * a.ndim)


N_SPLIT = 2


def _split(x):
    hi = x.astype(BF16)
    lo = (x - hi.astype(F32)).astype(BF16)
    return hi, lo


def _sigmoid(x):
    return 0.5 * jnp.tanh(0.5 * x) + 0.5


def _arrange_w_in(w_in):
    pieces, pos = [], 0

    def put(dst, block):
        nonlocal pos
        if dst > pos:
            pieces.append(jnp.zeros(w_in.shape[:-1] + (dst - pos,), w_in.dtype))
        pieces.append(block)
        pos = dst + block.shape[-1]

    ref = lambda name: w_in[..., _REF_COLS[name][0]:_REF_COLS[name][0] + _REF_COLS[name][1]]
    for name, dst in (('gdn_qkv', P_GDN_QKV), ('gdn_z', P_GDN_Z), ('lru_x', P_LRU_X), ('lru_y', P_LRU_Y),
                      ('mla_kv', P_MLA_KV), ('mla_kr', P_MLA_KR), ('gdn_beta', P_GDN_BA),
                      ('gdn_a', P_GDN_BA + 2 * GDN_HEADS)):
        put(dst, ref(name))
    put(P32_COLS + P_MLA_Q, ref('mla_q'))
    put(P32_COLS + P_NA_QKV, ref('na_qkv'))
    put(P32_COLS + P_GATES, w_in[..., N_MIX_COLS:])
    assert pos == P32_COLS + P16_COLS
    return jnp.concatenate(pieces, axis=-1).astype(BF16)


def _rope_perm():
    q = MLA_ROPE // 4
    src = np.zeros(MLA_ROPE, np.int32)
    sign = np.zeros(MLA_ROPE, np.float32)
    for base in (0, 2 * q):
        for d in range(q):
            src[base + d] = base + d + q
            sign[base + d] = -1.0
            src[base + q + d] = base + d
            sign[base + q + d] = 1.0
    return src, sign


def _arrange_mla(w_uq, w_ukv):
    src, sign = _rope_perm()
    hq = MLA_NOPE + MLA_ROPE
    wq = jnp.zeros((MLA_Q_RANK, 2 * MLA_HEADS * MLA_SLOT), F32)
    wk = jnp.zeros((MLA_KV_RANK, MLA_HEADS * MLA_SLOT), F32)
    wv = jnp.zeros((MLA_KV_RANK, MLA_HEADS * MLA_V), F32)
    place = np.zeros((2 * MLA_SLOT, 2 * MLA_HEADS * MLA_SLOT), np.float32)
    rot_off = MLA_HEADS * MLA_SLOT
    for h in range(MLA_HEADS):
        nope = w_uq[:, h * hq:h * hq + MLA_NOPE]
        pe = w_uq[:, h * hq + MLA_NOPE:(h + 1) * hq]
        s = h * MLA_SLOT
        wq = wq.at[:, s:s + MLA_NOPE].set(nope)
        wq = wq.at[:, s + MLA_NOPE:s + MLA_NOPE + MLA_ROPE].set(pe)
        wq = wq.at[:, rot_off + s + MLA_NOPE:rot_off + s + MLA_NOPE + MLA_ROPE].set(pe[:, src] * sign)
        wk = wk.at[:, s:s + MLA_NOPE].set(w_ukv[:, h * 128:h * 128 + MLA_NOPE])
        wv = wv.at[:, h * MLA_V:(h + 1) * MLA_V].set(w_ukv[:, h * 128 + MLA_NOPE:(h + 1) * 128])
        for d in range(MLA_ROPE):
            for half in (0, MLA_SLOT):
                place[half + d, s + MLA_NOPE + d] = 1.0
                place[half + src[d], rot_off + s + MLA_NOPE + d] = sign[d]
    return wq.astype(BF16), wk.astype(BF16), wv.astype(BF16), jnp.asarray(place, BF16)


def _rope_tables(n_tok, n_ctx):
    cos = np.ones((n_tok + n_ctx, MLA_SLOT), np.float32)
    sin = np.zeros((n_tok + n_ctx, MLA_SLOT), np.float32)
    t = np.arange(n_tok)
    row = (t // GRID_W).astype(np.float32)
    col = (t % GRID_W).astype(np.float32)
    n_freq = MLA_ROPE // 4
    inv = (ROPE_BASE ** (-np.arange(n_freq, dtype=np.float32) / n_freq)).astype(np.float32)
    ar = row[:, None] * inv
    ac = col[:, None] * inv
    ang = np.concatenate([ar, ar, ac, ac], axis=-1).astype(np.float32)
    cos[:n_tok, MLA_NOPE:MLA_NOPE + MLA_ROPE] = np.cos(ang)
    sin[:n_tok, MLA_NOPE:MLA_NOPE + MLA_ROPE] = np.sin(ang)
    return jnp.asarray(cos), jnp.asarray(sin)


def _head_block_ones(n_heads, width):
    m = np.kron(np.eye(n_heads, dtype=np.float32), np.ones((width, width), np.float32))
    return jnp.asarray(m, BF16)


def _mod_kernel(c_ref, w_ref, b_ref, o_ref):
    c = c_ref[...]
    s = c * _sigmoid(c)
    o_ref[...] = jnp.dot(s, w_ref[...], preferred_element_type=F32) + b_ref[...]


def _modulation(cc, mod_w, mod_b, layer):
    r, d = cc.shape
    n = mod_w.shape[-1]
    tn = 1024
    return pl.pallas_call(
        _mod_kernel,
        grid=(n // tn,),
        in_specs=[pl.BlockSpec((r, d), lambda j: (0, 0)),
                  pl.BlockSpec((None, d, tn), lambda j: (layer, 0, j)),
                  pl.BlockSpec((1, tn), lambda j: (0, j))],
        out_specs=pl.BlockSpec((r, tn), lambda j: (0, j)),
        out_shape=jax.ShapeDtypeStruct((r, n), F32),
        compiler_params=_cparams(("arbitrary",)),
        name="modulation",
    )(cc, mod_w, mod_b.reshape(1, n))


def _row_select(tile_rows, tile_idx, n_lat, ctx_vec, lat_vec):
    row = tile_idx * tile_rows + lax.broadcasted_iota(jnp.int32, (tile_rows, 1), 0)
    return jnp.where(row < n_lat, lat_vec, ctx_vec)


def _norm_modulate(x_ref, h_ref, gain_ref, mc_ref, ml_ref, shift_row, tile_idx, n_lat):
    tm = x_ref.shape[0]
    x = x_ref[...]
    xn = x * lax.rsqrt(jnp.mean(x * x, axis=-1, keepdims=True) + RMS_EPS)
    sh_l, sh_c = ml_ref[shift_row:shift_row + 1, :], mc_ref[shift_row:shift_row + 1, :]
    amp_l = gain_ref[...] * (1.0 + ml_ref[shift_row + 1:shift_row + 2, :])
    amp_c = gain_ref[...] * (1.0 + mc_ref[shift_row + 1:shift_row + 2, :])
    all_latent = (tile_idx + 1) * tm <= n_lat

    @pl.when(all_latent)
    def _():
        h_ref[...] = (xn * amp_l + sh_l).astype(h_ref.dtype)

    @pl.when(jnp.logical_not(all_latent))
    def _():
        amp = _row_select(tm, tile_idx, n_lat, amp_c, amp_l)
        shift = _row_select(tm, tile_idx, n_lat, sh_c, sh_l)
        h_ref[...] = (xn * amp + shift).astype(h_ref.dtype)


def _inproj32_kernel(x_ref, mc_ref, ml_ref, g_ref, w_ref, o_ref, h_ref, *, n_lat):
    @pl.when(pl.program_id(2) == 0)
    def _():
        _norm_modulate(x_ref, h_ref, g_ref, mc_ref, ml_ref, 0, pl.program_id(1), n_lat)

    o_ref[...] = jnp.dot(h_ref[...], w_ref[...], preferred_element_type=F32)


def _inproj16_kernel(h_ref, w_ref, o_ref):
    o_ref[...] = jnp.dot(h_ref[...], w_ref[...], preferred_element_type=F32).astype(o_ref.dtype)


def _inproj(xs, mod_c, mod_l, gain, w, layer, n_lat):
    b, s, d = xs.shape
    tm = _pick_tile(s, 1088)
    tn = PROJ_TILE
    n32 = P32_COLS // tn
    p32, h = pl.pallas_call(
        functools.partial(_inproj32_kernel, n_lat=n_lat),
        grid=(b, s // tm, n32),
        in_specs=[pl.BlockSpec((None, tm, d), lambda bi, i, j: (bi, i, 0)),
                  pl.BlockSpec((8, d), lambda bi, i, j: (0, 0)),
                  pl.BlockSpec((None, 8, d), lambda bi, i, j: (bi, 0, 0)),
                  pl.BlockSpec((1, d), lambda bi, i, j: (0, 0)),
                  pl.BlockSpec((None, d, tn), lambda bi, i, j: (layer, 0, j))],
        out_specs=[pl.BlockSpec((None, tm, tn), lambda bi, i, j: (bi, i, j)),
                   pl.BlockSpec((None, tm, d), lambda bi, i, j: (bi, i, 0))],
        out_shape=[jax.ShapeDtypeStruct((b, s, P32_COLS), F32), jax.ShapeDtypeStruct((b, s, d), BF16)],
        compiler_params=_cparams(("parallel", "parallel", "arbitrary")),
        name="inproj_f32",
    )(xs, mod_c, mod_l, gain, w)
    p16 = pl.pallas_call(
        _inproj16_kernel,
        grid=(b, s // tm, P16_COLS // tn),
        in_specs=[pl.BlockSpec((None, tm, d), lambda bi, i, j: (bi, i, 0)),
                  pl.BlockSpec((None, d, tn), lambda bi, i, j: (layer, 0, j + n32))],
        out_specs=pl.BlockSpec((None, tm, tn), lambda bi, i, j: (bi, i, j)),
        out_shape=jax.ShapeDtypeStruct((b, s, P16_COLS), BF16),
        compiler_params=_cparams(("parallel", "parallel", "arbitrary")),
        name="inproj_bf16",
    )(h, w)
    return p32, p16


def _mla_prep_kernel(ql_ref, kv_ref, cos_ref, sin_ref, gq_ref, gkv_ref, wq_ref, wk_ref, wv_ref, pl_ref,
                     q_ref, k_ref, v_ref, *, scale):
    nslot = MLA_HEADS * MLA_SLOT
    cos = jnp.concatenate([cos_ref[...]] * MLA_HEADS, axis=-1)
    sin = jnp.concatenate([sin_ref[...]] * MLA_HEADS, axis=-1)

    ql = ql_ref[...].astype(F32)
    qn = ql * lax.rsqrt(jnp.mean(ql * ql, axis=-1, keepdims=True) + RMS_EPS) * gq_ref[...]
    q2 = jnp.dot(qn.astype(BF16), wq_ref[...], preferred_element_type=F32)
    q = (q2[:, :nslot] * cos + q2[:, nslot:] * sin) * scale
    q_ref[...] = q.astype(BF16)

    kvkr = kv_ref[...]
    kvl = kvkr[:, :MLA_KV_RANK]
    kr = kvkr[:, MLA_KV_RANK:]
    kvn = (kvl * lax.rsqrt(jnp.mean(kvl * kvl, axis=-1, keepdims=True) + RMS_EPS) * gkv_ref[...]).astype(BF16)
    kn = jnp.dot(kvn, wk_ref[...], preferred_element_type=F32)
    v_ref[...] = jnp.dot(kvn, wv_ref[...], preferred_element_type=F32).astype(BF16)
    kr_hi = kr.astype(BF16)
    kr_lo = (kr - kr_hi.astype(F32)).astype(BF16)
    kr2 = jnp.dot(jnp.concatenate([kr_hi, kr_lo], axis=-1), pl_ref[...], preferred_element_type=F32)
    k_ref[...] = (kn + kr2[:, :nslot] * cos + kr2[:, nslot:] * sin).astype(BF16)


def _mla_prep(p32, p16, cos, sin, gq, gkv, wq, wk, wv, place):
    b, s, _ = p32.shape
    tm = _pick_tile(s, 544)
    nslot = MLA_HEADS * MLA_SLOT
    scale = (MLA_NOPE + MLA_ROPE) ** -0.5 * LOG2_E
    return pl.pallas_call(
        functools.partial(_mla_prep_kernel, scale=scale),
        grid=(b, s // tm),
        in_specs=[pl.BlockSpec((None, tm, MLA_Q_RANK), lambda bi, i: (bi, i, P_MLA_Q // MLA_Q_RANK)),
                  pl.BlockSpec((None, tm, 2 * MLA_SLOT), lambda bi, i: (bi, i, P_MLA_KV // (2 * MLA_SLOT))),
                  pl.BlockSpec((tm, MLA_SLOT), lambda bi, i: (i, 0)),
                  pl.BlockSpec((tm, MLA_SLOT), lambda bi, i: (i, 0)),
                  _full(gq), _full(gkv), _full(wq), _full(wk), _full(wv), _full(place)],
        out_specs=[pl.BlockSpec((None, tm, nslot), lambda bi, i: (bi, i, 0)),
                   pl.BlockSpec((None, tm, nslot), lambda bi, i: (bi, i, 0)),
                   pl.BlockSpec((None, tm, MLA_HEADS * MLA_V), lambda bi, i: (bi, i, 0))],
        out_shape=[jax.ShapeDtypeStruct((b, s, nslot), BF16),
                   jax.ShapeDtypeStruct((b, s, nslot), BF16),
                   jax.ShapeDtypeStruct((b, s, MLA_HEADS * MLA_V), BF16)],
        compiler_params=_cparams(("parallel", "parallel")),
        name="mla_prep",
    )(p16, p32, cos, sin, gq, gkv, wq, wk, wv, place)


def _flash_softmax(h, s, m_ref, l_ref):
    m_prev = m_ref[h]
    m_new = jnp.maximum(m_prev, jnp.max(s, axis=-1, keepdims=True))
    alpha = jnp.exp2(m_prev - m_new)
    p = jnp.exp2(s - jnp.concatenate([m_new] * (s.shape[1] // 128), axis=-1))
    l_ref[h] = alpha * l_ref[h] + jnp.sum(p, axis=-1, keepdims=True)
    m_ref[h] = m_new
    return alpha, p.astype(BF16)


def _mla_flash_kernel(*refs, tk, aliased):
    if aliased:
        q_ref, k_ref, v_ref, _, o_ref, m_ref, l_ref, acc_ref = refs
    else:
        q_ref, k_ref, v_ref, o_ref, m_ref, l_ref, acc_ref = refs
    tq = q_ref.shape[0]
    nk = k_ref.shape[0]
    n_loop = nk // tk
    m_ref[...] = jnp.full(m_ref.shape, -jnp.inf, F32)
    l_ref[...] = jnp.zeros(l_ref.shape, F32)
    acc_ref[...] = jnp.zeros(acc_ref.shape, F32)

    def chunk(rows):
        heads = range(MLA_HEADS)
        hs = [slice(h * MLA_SLOT, (h + 1) * MLA_SLOT) for h in heads]
        vs = [slice((h // 2) * 128, (h // 2) * 128 + 128) for h in heads]
        s = [lax.dot_general(q_ref[:, hs[h]], k_ref[rows, hs[h]], (((1,), (1,)), ((), ())),
                             preferred_element_type=F32) for h in heads]
        ap = [_flash_softmax(h, s[h], m_ref, l_ref) for h in heads]
        for h in heads:
            alpha, p = ap[h]
            acc_ref[h] = alpha * acc_ref[h] + jnp.dot(p, v_ref[rows, vs[h]], preferred_element_type=F32)

    if n_loop:
        def body(j, carry):
            chunk(pl.ds(pl.multiple_of(j * tk, tk), tk))
            return carry
        lax.fori_loop(0, n_loop, body, 0)
    if nk > n_loop * tk:
        chunk(slice(n_loop * tk, nk))

    lane = lax.broadcasted_iota(jnp.int32, (tq, 128), 1)
    outs = []
    for pair in range(MLA_HEADS // 2):
        o0 = acc_ref[2 * pair] / l_ref[2 * pair]
        o1 = acc_ref[2 * pair + 1] / l_ref[2 * pair + 1]
        outs.append(jnp.where(lane < MLA_V, o0, o1))
    o_ref[...] = jnp.concatenate(outs, axis=-1).astype(o_ref.dtype)


def _mla_flash(q, k, v, n_lat, prev=None):
    b, s, nslot = q.shape
    n_ctx = s - n_lat
    nv = MLA_HEADS * MLA_V
    tk = 512
    if prev is None:
        tq = _pick_tile(n_lat, 1024)
        q_off, n_q, kv_rows, kv_blk = 0, n_lat // tq, s, 0
    else:
        assert n_lat % n_ctx == 0
        tq = _pick_tile(n_ctx, 256)
        q_off, n_q, kv_rows, kv_blk = n_lat // tq, n_ctx // tq, n_ctx, n_lat // n_ctx
    in_specs = [pl.BlockSpec((None, tq, nslot), lambda bi, i: (bi, i + q_off, 0)),
                pl.BlockSpec((None, kv_rows, nslot), lambda bi, i: (bi, kv_blk, 0)),
                pl.BlockSpec((None, kv_rows, nv), lambda bi, i: (bi, kv_blk, 0))]
    args = [q, k, v]
    aliases = {}
    if prev is not None:
        in_specs.append(pl.BlockSpec(memory_space=pl.ANY))
        args.append(prev)
        aliases = {3: 0}
    return pl.pallas_call(
        functools.partial(_mla_flash_kernel, tk=tk, aliased=prev is not None),
        grid=(b, n_q),
        in_specs=in_specs,
        out_specs=pl.BlockSpec((None, tq, nv), lambda bi, i: (bi, i + q_off, 0)),
        out_shape=jax.ShapeDtypeStruct((b, s, nv), BF16),
        scratch_shapes=[pltpu.VMEM((MLA_HEADS, tq, 128), F32),
                        pltpu.VMEM((MLA_HEADS, tq, 128), F32),
                        pltpu.VMEM((MLA_HEADS, tq, 128), F32)],
        input_output_aliases=aliases,
        compiler_params=_cparams(("parallel", "arbitrary")),
        name="mla_flash",
    )(*args)


NA_QROWS = 4
NA_SLAB = NA_QROWS + NA_WIN_ROWS


def _na_geometry(rows):
    assert rows % NA_QROWS == 0 and rows >= NA_SLAB
    nblk = rows // NA_QROWS
    qc = np.arange(GRID_W)
    cs = np.clip(qc - NA_WIN_COLS // 2, 0, GRID_W - NA_WIN_COLS)
    col_valid = (qc[None, :] >= cs[:, None]) & (qc[None, :] < cs[:, None] + NA_WIN_COLS)
    rel_c = np.clip(qc[None, :] - qc[:, None] + NA_WIN_COLS - 1, 0, 2 * NA_WIN_COLS - 2)
    onehot_c = (rel_c[None] == np.arange(2 * NA_WIN_COLS - 1)[:, None, None]) & col_valid[None]
    patterns, cls, starts = {}, [], []
    for i in range(nblk):
        r0 = i * NA_QROWS
        start = int(np.clip(r0 - NA_WIN_ROWS // 2, 0, rows - NA_SLAB))
        qr = r0 + np.arange(NA_QROWS)
        rs = np.clip(qr - NA_WIN_ROWS // 2, 0, rows - NA_WIN_ROWS)
        key = (start - r0,) + tuple((rs - r0).tolist())
        if key not in patterns:
            kr = start + np.arange(NA_SLAB)
            row_valid = (kr[None, :] >= rs[:, None]) & (kr[None, :] < rs[:, None] + NA_WIN_ROWS)
            rel_r = np.clip(kr[None, :] - qr[:, None] + NA_WIN_ROWS - 1, 0, 2 * NA_WIN_ROWS - 2)
            patterns[key] = (len(patterns), np.where(row_valid, rel_r, -1))
        cls.append(patterns[key][0])
        starts.append(start)
    ordered = sorted(patterns.values(), key=lambda z: z[0])
    rel_rows = np.stack([z[1] for z in ordered])
    meta = np.stack([np.asarray(cls, np.int32), np.asarray(starts, np.int32)])
    return meta, rel_rows, (onehot_c.astype(np.float32), col_valid)


def _na_bias_table(rpb, rel_rows, col_tables):
    onehot_c, col_valid = col_tables
    bc = jnp.einsum('hrc,cqk->hrqk', rpb.astype(F32), onehot_c, precision=lax.Precision.HIGHEST)
    bc = jnp.where(col_valid, bc, -jnp.inf)
    outside = jnp.full(bc.shape[:1] + bc.shape[2:], -jnp.inf, F32)
    pats = []
    for pat in rel_rows:
        qrows = [jnp.concatenate([bc[:, r] if r >= 0 else outside for r in row], axis=-1) for row in pat]
        pats.append(jnp.concatenate(qrows, axis=1))
    return jnp.stack(pats)


def _na_scores(q_pair, lane, h, k_parts):
    in_head = (lane < NA_DH) if h % 2 == 0 else (lane >= NA_DH)
    qm = jnp.where(in_head, q_pair * (NA_DH ** -0.5), 0.0).astype(BF16)
    return [lax.dot_general(qm, k, (((1,), (1,)), ((), ())), preferred_element_type=F32) for k in k_parts]


def _na_softmax(scores, bias):
    if bias is not None:
        scores = [scores[0] + bias] + scores[1:]
    m = scores[0].max(axis=-1, keepdims=True)
    for s in scores[1:]:
        m = jnp.maximum(m, s.max(axis=-1, keepdims=True))
    probs = [jnp.exp(s - m) for s in scores]
    den = probs[0].sum(axis=-1, keepdims=True)
    for p in probs[1:]:
        den = den + p.sum(axis=-1, keepdims=True)
    return [p.astype(BF16) for p in probs], den


def _na_kernel(meta_ref, q_ref, k_ref, v_ref, bias_ref, o_ref, *, n_lat, with_ctx):
    i = pl.program_id(1)
    nq = q_ref.shape[0]
    n_lat_tiles = n_lat // nq
    n_all = k_ref.shape[0]
    lane = lax.broadcasted_iota(jnp.int32, (nq, 128), 1)

    def run(windowed):
        key_rows = [pl.ds(n_lat, n_all - n_lat)]
        if windowed:
            start = pl.multiple_of(meta_ref[1, jnp.minimum(i, n_lat_tiles - 1)] * GRID_W, GRID_W)
            key_rows = [pl.ds(start, NA_SLAB * GRID_W)] + key_rows
        pair_lanes = [slice((h // 2) * 128, (h // 2 + 1) * 128) for h in range(NA_HEADS)]
        scores = [_na_scores(q_ref[:, pair_lanes[h]], lane, h, [k_ref[r, pair_lanes[h]] for r in key_rows])
                  for h in range(NA_HEADS)]
        soft = [_na_softmax(scores[h], bias_ref[h] if windowed else None) for h in range(NA_HEADS)]
        heads = []
        for h in range(NA_HEADS):
            probs, den = soft[h]
            out = None
            for p, r in zip(probs, key_rows):
                po = jnp.dot(p, v_ref[r, pair_lanes[h]], preferred_element_type=F32)
                out = po if out is None else out + po
            heads.append(out / den)
        outs = [jnp.where(lane < NA_DH, heads[2 * pair], heads[2 * pair + 1]) for pair in range(NA_HEADS // 2)]
        o_ref[...] = jnp.concatenate(outs, axis=-1).astype(o_ref.dtype)

    if with_ctx:
        pl.when(i < n_lat_tiles)(lambda: run(True))
        pl.when(i >= n_lat_tiles)(lambda: run(False))
    else:
        run(True)


def _na_attention(p, meta, table, n_lat, with_ctx_queries):
    b, s, _ = p.shape
    nq = NA_QROWS * GRID_W
    nk = NA_SLAB * GRID_W
    assert (s - n_lat) % nq == 0
    nlt = n_lat // nq
    qb = P_NA_QKV // NA_W
    grid_spec = pltpu.PrefetchScalarGridSpec(
        num_scalar_prefetch=1,
        grid=(b, s // nq if with_ctx_queries else nlt),
        in_specs=[pl.BlockSpec((None, nq, NA_W), lambda bi, i, m: (bi, i, qb)),
                  pl.BlockSpec((None, s, NA_W), lambda bi, i, m: (bi, 0, qb + 1)),
                  pl.BlockSpec((None, s, NA_W), lambda bi, i, m: (bi, 0, qb + 2)),
                  pl.BlockSpec((None, NA_HEADS, nq, nk),
                               lambda bi, i, m: (m[0, jnp.minimum(i, nlt - 1)], 0, 0, 0))],
        out_specs=pl.BlockSpec((None, nq, NA_W), lambda bi, i, m: (bi, i, 0)))
    return pl.pallas_call(
        functools.partial(_na_kernel, n_lat=n_lat, with_ctx=with_ctx_queries),
        grid_spec=grid_spec,
        out_shape=jax.ShapeDtypeStruct((b, s, NA_W), BF16),
        compiler_params=_cparams(("parallel", "arbitrary")),
        name="na_attention",
    )(meta, p, p, p, table)


def _tile_conv(x, prev, nxt, w_ref, tile_idx, n_tiles, n_lat_tiles):
    r = x.shape[0]
    width = w_ref.shape[0]
    left = width // 2
    has_prev = jnp.logical_and(tile_idx != 0, tile_idx != n_lat_tiles)
    has_next = jnp.logical_and(tile_idx != n_tiles - 1, tile_idx != n_lat_tiles - 1)
    prev = jnp.where(has_prev, prev, 0.0)
    nxt = jnp.where(has_next, nxt, 0.0)
    xe = jnp.concatenate([prev, x, nxt], axis=0)
    acc = None
    for j in range(width):
        o = HALO - left + j
        term = xe[o:o + r, :] * w_ref[j:j + 1, :]
        acc = term if acc is None else acc + term
    return acc


def _halo_specs(width, col_block, tile_of):
    per = SEQ_TILE // HALO

    def main(bi, s, *_):
        return (bi, tile_of(s), col_block)

    def prev(bi, s, *_):
        return (bi, jnp.maximum(tile_of(s) * per - 1, 0), col_block)

    def make_next(n_tiles):
        def nxt(bi, s, *_):
            return (bi, jnp.minimum((tile_of(s) + 1) * per, n_tiles * per - 1), col_block)
        return nxt

    return main, prev, make_next


def _gdn_prep_kernel(x_ref, xp_ref, xn_ref, ba_ref, cw_ref, ones_ref, exp_ref, alog_ref, dtb_ref,
                     q_ref, k_ref, v_ref, beta_ref, g_ref, *, n_lat_tiles):
    i = pl.program_id(1)
    y = _tile_conv(x_ref[...], xp_ref[...], xn_ref[...], cw_ref, i, pl.num_programs(1), n_lat_tiles)
    y = y * _sigmoid(y)
    q = y[:, :GDN_W]
    k = y[:, GDN_W:2 * GDN_W]
    v_ref[...] = y[:, 2 * GDN_W:].astype(v_ref.dtype)

    def head_norm(u):
        parts = jnp.concatenate(_split(u * u), axis=-1)
        ss = jnp.dot(parts, ones_ref[...], preferred_element_type=F32)
        return u * lax.rsqrt(ss + RMS_EPS)

    q_ref[...] = (head_norm(q) * (GDN_DK ** -0.5)).astype(q_ref.dtype)
    k_ref[...] = head_norm(k).astype(k_ref.dtype)

    ba = ba_ref[...]
    a = ba + dtb_ref[...]
    softplus = jnp.maximum(a, 0.0) + jnp.log1p(jnp.exp(-jnp.abs(a)))
    lane = lax.broadcasted_iota(jnp.int32, ba.shape, 1)
    compact = jnp.where(lane < 2 * GDN_HEADS, _sigmoid(ba), -jnp.exp(alog_ref[...]) * softplus)
    wide = jnp.dot(jnp.concatenate(_split(compact), axis=-1), exp_ref[...], preferred_element_type=F32)
    half = 2 * GDN_W
    beta_ref[...] = wide[:, :half].astype(beta_ref.dtype)
    g_ref[...] = wide[:, half:]


def _gdn_prep(p, conv_w, a_log, dt_bias, n_lat):
    b, s, _ = p.shape
    n_tiles = s // SEQ_TILE
    main, prev, make_next = _halo_specs(3 * GDN_W, 0, lambda t: t)
    ones_n = jnp.concatenate([_head_block_ones(GDN_HEADS, GDN_DK)] * N_SPLIT, axis=0)
    expand = np.zeros((128, 4 * GDN_W), np.float32)
    for kind in range(2):
        for d in range(2):
            for h in range(GDN_HEADS):
                c0 = kind * 2 * GDN_W + d * GDN_W + h * GDN_DV
                expand[kind * 2 * GDN_HEADS + d * GDN_HEADS + h, c0:c0 + GDN_DV] = 1.0
    expand_n = jnp.asarray(np.concatenate([expand] * N_SPLIT, axis=0), BF16)
    lanes = jnp.zeros((1, 128), F32)
    alog_e = lanes.at[0, 2 * GDN_HEADS:4 * GDN_HEADS].set(a_log.astype(F32).reshape(-1))
    dtb_e = lanes.at[0, 2 * GDN_HEADS:4 * GDN_HEADS].set(dt_bias.astype(F32).reshape(-1))
    tok = lambda w: pl.BlockSpec((None, SEQ_TILE, w), lambda bi, i: (bi, i, 0))
    return pl.pallas_call(
        functools.partial(_gdn_prep_kernel, n_lat_tiles=n_lat // SEQ_TILE),
        grid=(b, n_tiles),
        in_specs=[pl.BlockSpec((None, SEQ_TILE, 3 * GDN_W), main),
                  pl.BlockSpec((None, HALO, 3 * GDN_W), prev),
                  pl.BlockSpec((None, HALO, 3 * GDN_W), make_next(n_tiles)),
                  pl.BlockSpec((None, SEQ_TILE, 128), lambda bi, i: (bi, i, P_GDN_BA // 128)),
                  _full(conv_w), _full(ones_n), _full(expand_n), _full(alog_e), _full(dtb_e)],
        out_specs=[tok(GDN_W), tok(GDN_W), tok(GDN_W), tok(2 * GDN_W), tok(2 * GDN_W)],
        out_shape=[jax.ShapeDtypeStruct((b, s, GDN_W), BF16)] * 3 + [jax.ShapeDtypeStruct((b, s, 2 * GDN_W), BF16),
                                                                    jax.ShapeDtypeStruct((b, s, 2 * GDN_W), F32)],
        compiler_params=_cparams(("parallel", "parallel")),
        name="gdn_prep",
    )(p, p, p, p, conv_w, ones_n, expand_n, alog_e, dtb_e)


GDN_PAIR = 2 * GDN_DK


def _gdn_masks():
    c, w = GDN_CHUNK, GDN_W
    r2, c2 = np.arange(GDN_PAIR)[:, None], np.arange(GDN_PAIR)[None, :]
    bd = ((r2 // c) == (c2 // c)).astype(np.float32)
    i = np.arange(c)[:, None]
    j = (np.arange(w) % c)[None, :]
    level = np.zeros((c, w), np.int32)
    for bit in range(6):
        level += ((i ^ j) >= (1 << bit)).astype(np.int32)
    lvl = np.stack([(level == m).astype(np.float32) for m in range(7)])
    dirm = np.stack([np.stack([(j <= i), (j < i)]), np.stack([(j >= i), (j > i)])]).astype(np.float32)
    tj = (np.arange(N_SPLIT * c) % c)[None, :]
    tri = np.stack([(tj <= i), (tj >= i)]).astype(np.float32)
    return jnp.asarray(bd, BF16), jnp.asarray(lvl), jnp.asarray(dirm), jnp.asarray(tri, BF16)


def _heads_mm(x, y, bd, transpose_rhs=False):
    xb = x.astype(BF16)
    yb = y.astype(BF16)
    outs = []
    for pair in range(GDN_W // GDN_PAIR):
        ls = slice(pair * GDN_PAIR, (pair + 1) * GDN_PAIR)
        w = jnp.concatenate([yb[:, ls], yb[:, ls]], axis=0) * bd
        dims = (((1,), (1,)), ((), ())) if transpose_rhs else (((1,), (0,)), ((), ()))
        outs.append(lax.dot_general(xb[:, ls], w, dims, preferred_element_type=F32))
    return jnp.concatenate(outs, axis=1)


def _gdn_intra(probs, bd, lvl_ref, dirm_ref, tri_ref):
    c = GDN_CHUNK
    n = len(probs)
    eye = lvl_ref[0]
    gc, g_last, decay, gram = [], [], [], []
    for q, k, v, beta, g, rev in probs:
        d = 1 if rev else 0
        gcp = jnp.dot(tri_ref[d], jnp.concatenate(_split(g), axis=0), preferred_element_type=F32)
        gc.append(gcp)
        g_last.append(gcp[0:1, :] if rev else gcp[c - 1:c, :])
        gc_row = jnp.sum(gcp * eye, axis=0, keepdims=True)
        decay.append(dirm_ref[d, 0] * jnp.exp(jnp.minimum(gcp - gc_row, 0.0)))
        gram.append(_heads_mm(jnp.concatenate([k, q], axis=0), k, bd, transpose_rhs=True))
    lower = [dirm_ref[1 if p[5] else 0, 1] * p[3] * gram[x][:c] * decay[x] for x, p in enumerate(probs)]
    a_intra = [gram[x][c:] * decay[x] for x in range(n)]
    t = [eye - lower[x] * lvl_ref[1] for x in range(n)]
    for lev in range(2, 7):
        y = [_heads_mm(t[x], lower[x] * lvl_ref[lev], bd) for x in range(n)]
        z = [_heads_mm(y[x], t[x], bd) for x in range(n)]
        t = [t[x] - z[x] for x in range(n)]
    e_gc = [jnp.exp(gc[x]) for x in range(n)]
    u = [_heads_mm(t[x], p[2] * p[3], bd) for x, p in enumerate(probs)]
    w = [_heads_mm(t[x], p[1] * p[3] * e_gc[x], bd) for x, p in enumerate(probs)]
    wq = [jnp.concatenate([w[x], p[0] * e_gc[x]], axis=0).astype(BF16) for x, p in enumerate(probs)]
    k_dec = [(p[1] * jnp.exp(g_last[x] - gc[x])).astype(BF16) for x, p in enumerate(probs)]
    g_tot = [jnp.exp(g_last[x]) for x in range(n)]
    return u, wq, k_dec, a_intra, g_tot


def _gdn_state_steps(items, bd):
    c = GDN_CHUNK
    pairs = [slice(p * GDN_PAIR, (p + 1) * GDN_PAIR) for p in range(GDN_W // GDN_PAIR)]
    bdf = bd.astype(F32)
    ws_qs = [jnp.concatenate([jnp.dot(wq[:, ls], s_ref[p].astype(BF16), preferred_element_type=F32)
                              for p, ls in enumerate(pairs)], axis=1)
             for _, wq, _, _, _, s_ref in items]
    v_new = [it[0] - ws[:c] for it, ws in zip(items, ws_qs)]
    outs = [ws[c:] + _heads_mm(it[3], vn, bd) for it, ws, vn in zip(items, ws_qs, v_new)]
    for (_, _, k_dec, _, g_tot, s_ref), vn in zip(items, v_new):
        vb = vn.astype(BF16)
        for p, ls in enumerate(pairs):
            kv = lax.dot_general(k_dec[:, ls], vb[:, ls], (((0,), (0,)), ((), ())), preferred_element_type=F32)
            s_ref[p] = s_ref[p] * g_tot[:, ls] + kv * bdf
    return outs


def _gdn_scan_body(qf, kf, vf, bf, gf, qb, kb, vb, bb, gb, bd_ref, lvl_ref, dirm_ref, tri_ref,
                   of_ref, ob_ref, sf_ref, sb_ref):
    bd = bd_ref[...]
    n = SEQ_TILE // GDN_CHUNK
    probs, rows = [], []
    for c in range(n):
        rf = slice(c * GDN_CHUNK, (c + 1) * GDN_CHUNK)
        rb = slice((n - 1 - c) * GDN_CHUNK, (n - c) * GDN_CHUNK)
        probs.append(tuple(r[rf, :].astype(F32) for r in (qf, kf, vf, bf, gf)) + (False,))
        probs.append(tuple(r[rb, :].astype(F32) for r in (qb, kb, vb, bb, gb)) + (True,))
        rows += [rf, rb]
    u, wq, k_dec, a_intra, g_tot = _gdn_intra(probs, bd, lvl_ref, dirm_ref, tri_ref)
    for c in range(n):
        xs = (2 * c, 2 * c + 1)
        outs = _gdn_state_steps([(u[x], wq[x], k_dec[x], a_intra[x], g_tot[x], sb_ref if probs[x][5] else sf_ref)
                                 for x in xs], bd)
        for x, o in zip(xs, outs):
            (ob_ref if probs[x][5] else of_ref)[rows[x], :] = o


def _seq_tile_maps(n_lat_tiles, n_tiles):
    fwd = lambda s: lax.rem(s + n_lat_tiles, n_tiles)
    bwd = lambda s: n_tiles - 1 - s
    return fwd, bwd


def _gdn_scan_specs(q, k, v, beta, g, n_lat):
    b, s, _ = q.shape
    n_tiles = s // SEQ_TILE
    fwd, bwd = _seq_tile_maps(n_lat // SEQ_TILE, n_tiles)
    spec = lambda tile_of, col: pl.BlockSpec((None, SEQ_TILE, GDN_W), lambda bi, t: (bi, tile_of(t), col))
    masks = _gdn_masks()
    in_specs = [spec(fwd, 0)] * 5 + [spec(bwd, 0)] * 3 + [spec(bwd, 1)] * 2 + [_full(m) for m in masks]
    assert len(in_specs) == N_GDN_IN
    return (in_specs, [q, k, v, beta, g, q, k, v, beta, g, *masks], [spec(fwd, 0), spec(bwd, 0)],
            [jax.ShapeDtypeStruct((b, s, GDN_W), F32)] * 2,
            [pltpu.VMEM((GDN_W // GDN_PAIR, GDN_PAIR, GDN_PAIR), F32)] * 2)


def _lru_tile_prep(x_ref, xp_ref, xn_ref, cw_ref, cb_ref, wg_ref, bg_ref, nla_ref, a_ref, b_ref,
                   tile_idx, n_tiles, n_lat_tiles):
    xb = _tile_conv(x_ref[...], xp_ref[...], xn_ref[...], cw_ref, tile_idx, n_tiles, n_lat_tiles) + cb_ref[...]
    gates = _sigmoid(jnp.dot(xb.astype(BF16), wg_ref[...], preferred_element_type=F32) + bg_ref[...])
    log_a = nla_ref[...] * gates[:, :LRU_W]
    a_ref[...] = jnp.exp(log_a)
    th = jnp.tanh(log_a)
    b_ref[...] = jnp.sqrt(-2.0 * th / (1.0 - th)) * gates[:, LRU_W:] * xb


def _scan_group(a, b, h, row, reverse):
    for d in (1, 2, 4):
        if reverse:
            keep = row < 8 - d
            shift = 8 - d
        else:
            keep = row >= d
            shift = d
        a_s = jnp.where(keep, pltpu.roll(a, shift, 0), 1.0)
        b_s = jnp.where(keep, pltpu.roll(b, shift, 0), 0.0)
        b = a * b_s + b
        a = a * a_s
    return a * h + b


def _lru_scan_body(xf, xfp, xfn, xb, xbp, xbn, cw_ref, cb_ref, wgf, bgf, nlaf, wgb, bgb, nlab,
                   hf_ref, hb_ref, af_ref, bf_ref, ab_ref, bb_ref, cf_ref, cbk_ref, *, n_lat_tiles):
    s = pl.program_id(1)
    n_tiles = pl.num_programs(1)
    t_f, t_b = (m(s) for m in _seq_tile_maps(n_lat_tiles, n_tiles))
    _lru_tile_prep(xf, xfp, xfn, cw_ref, cb_ref, wgf, bgf, nlaf, af_ref, bf_ref, t_f, n_tiles, n_lat_tiles)
    _lru_tile_prep(xb, xbp, xbn, cw_ref, cb_ref, wgb, bgb, nlab, ab_ref, bb_ref, t_b, n_tiles, n_lat_tiles)

    n_groups = SEQ_TILE // 8
    row = lax.broadcasted_iota(jnp.int32, (8, LRU_W), 0)

    h_f, h_b = cf_ref[...], cbk_ref[...]
    for gi in range(n_groups):
        rf = slice(gi * 8, gi * 8 + 8)
        rb = slice((n_groups - 1 - gi) * 8, (n_groups - gi) * 8)
        out_f = _scan_group(af_ref[rf, :], bf_ref[rf, :], h_f, row, False)
        out_b = _scan_group(ab_ref[rb, :], bb_ref[rb, :], h_b, row, True)
        hf_ref[rf, :] = out_f
        hb_ref[rb, :] = out_b
        h_f = jnp.broadcast_to(out_f[7:8, :], (8, LRU_W))
        h_b = jnp.broadcast_to(out_b[0:1, :], (8, LRU_W))
    cf_ref[...] = h_f
    cbk_ref[...] = h_b


N_LRU_IN, N_GDN_IN, N_LRU_SCRATCH = 14, 14, 6


def _seq_mixers_kernel(*refs, n_lat_tiles):
    lru_in = refs[:N_LRU_IN]
    gdn_in = refs[N_LRU_IN:N_LRU_IN + N_GDN_IN]
    hf_ref, hb_ref, of_ref, ob_ref = refs[N_LRU_IN + N_GDN_IN:N_LRU_IN + N_GDN_IN + 4]
    scratch = refs[N_LRU_IN + N_GDN_IN + 4:]
    lru_scratch, gdn_scratch = scratch[:N_LRU_SCRATCH], scratch[N_LRU_SCRATCH:]

    @pl.when(pl.program_id(1) == 0)
    def _():
        for ref in lru_scratch[4:] + gdn_scratch:
            ref[...] = jnp.zeros(ref.shape, ref.dtype)

    _lru_scan_body(*lru_in, hf_ref, hb_ref, *lru_scratch, n_lat_tiles=n_lat_tiles)
    _gdn_scan_body(*gdn_in, of_ref, ob_ref, *gdn_scratch)


def _lru_scan_specs(p, conv_w, conv_b, w_r, b_r, w_i, b_i, lam, n_lat):
    b, s, _ = p.shape
    n_tiles = s // SEQ_TILE
    nlt = n_lat // SEQ_TILE
    fwd, bwd = _seq_tile_maps(nlt, n_tiles)
    col = P_LRU_X // LRU_W
    specs = []
    for tile_of in (fwd, bwd):
        main, prev, make_next = _halo_specs(LRU_W, col, tile_of)
        specs += [pl.BlockSpec((None, SEQ_TILE, LRU_W), main), pl.BlockSpec((None, HALO, LRU_W), prev),
                  pl.BlockSpec((None, HALO, LRU_W), make_next(n_tiles))]

    def blockdiag(w):
        return jax.scipy.linalg.block_diag(*[w[n] for n in range(LRU_BLOCKS)])

    dir_args = []
    for d in range(2):
        wg = jnp.concatenate([blockdiag(w_r[d]), blockdiag(w_i[d])], axis=1).astype(BF16)
        bg = jnp.concatenate([b_r[d], b_i[d]]).astype(F32).reshape(1, 2 * LRU_W)
        nla = (-LRU_C * jax.nn.softplus(-lam[d].astype(F32))).reshape(1, LRU_W)
        dir_args += [wg, bg, nla]
    cb2 = conv_b.reshape(1, LRU_W)
    out_spec = lambda tile_of: pl.BlockSpec((None, SEQ_TILE, LRU_W), lambda bi, t: (bi, tile_of(t), 0))
    in_specs = specs + [_full(conv_w), _full(cb2)] + [_full(a) for a in dir_args]
    assert len(in_specs) == N_LRU_IN
    return (in_specs, [p, p, p, p, p, p, conv_w, cb2, *dir_args], [out_spec(fwd), out_spec(bwd)],
            [jax.ShapeDtypeStruct((b, s, LRU_W), F32)] * 2,
            [pltpu.VMEM((SEQ_TILE, LRU_W), F32)] * 4 + [pltpu.VMEM((8, LRU_W), F32)] * 2)


def _seq_mixers(lru_parts, gdn_parts, bsz, n_tiles, n_lat):
    in_specs, args, out_specs, out_shape, scratch = ([*a, *b] for a, b in zip(lru_parts, gdn_parts))
    return pl.pallas_call(
        functools.partial(_seq_mixers_kernel, n_lat_tiles=n_lat // SEQ_TILE),
        grid=(bsz, n_tiles),
        in_specs=in_specs,
        out_specs=out_specs,
        out_shape=out_shape,
        scratch_shapes=scratch,
        compiler_params=_cparams(("parallel", "arbitrary")),
        name="seq_mixers",
    )(*args)


def _merge_kernel(x_ref, mc_ref, ml_ref, of_ref, ob_ref, z_ref, hf_ref, hb_ref, y_ref, uc_ref, ud_ref,
                  t0_ref, t1_ref, t2_ref, t3_ref, gn_ref, ones_ref, bg_ref, wb_ref, wo_ref, o_ref, *, n_lat):
    o = of_ref[...] + ob_ref[...]
    ms = jnp.dot(jnp.concatenate(_split(o * o), axis=-1), ones_ref[...], preferred_element_type=F32) * (1.0 / GDN_DV)
    z = z_ref[...]
    ua = (o * lax.rsqrt(ms + RMS_EPS) * gn_ref[...] * (z * _sigmoid(z))).astype(BF16)
    y = y_ref[...]
    gelu = 0.5 * y * (1.0 + jnp.tanh(0.7978845608028654 * (y + 0.044715 * (y * y * y))))
    ub = ((hf_ref[...] + hb_ref[...]) * gelu).astype(BF16)

    merged = None
    for n, (u, t_ref) in enumerate(((ua, t0_ref), (ub, t1_ref), (uc_ref[...], t2_ref), (ud_ref[...], t3_ref))):
        gate = _sigmoid(t_ref[...] + bg_ref[n:n + 1, :].astype(BF16))
        term = gate.astype(F32) * jnp.dot(u, wb_ref[n], preferred_element_type=F32)
        merged = term if merged is None else merged + term
    out = jnp.dot(merged.astype(BF16), wo_ref[...], preferred_element_type=F32)
    g1 = _row_select(x_ref.shape[0], pl.program_id(1), n_lat, mc_ref[2:3, :], ml_ref[2:3, :])
    o_ref[...] = x_ref[...] + g1 * out


def _merge(xs, mod_c, mod_l, gdn_out, lru_out, uc, ud, p32, p16, gdn_norm_g, b_gate, w_branch, w_out, layer, n_lat,
           with_ctx):
    b, s, d = xs.shape
    rows = s if with_ctx else n_lat
    tm = _pick_tile(rows, 544)
    gate_blk = P_GATES // d
    tok = lambda w: pl.BlockSpec((None, tm, w), lambda bi, i: (bi, i, 0))
    ones_n = jnp.concatenate([_head_block_ones(GDN_HEADS, GDN_DV)] * N_SPLIT, axis=0)
    gn = jnp.tile(gdn_norm_g.astype(F32), GDN_HEADS).reshape(1, GDN_W)
    pcol = lambda c: pl.BlockSpec((None, tm, BRANCH_W), lambda bi, i: (bi, i, c))
    in_specs = [tok(d), pl.BlockSpec((8, d), lambda bi, i: (0, 0)), pl.BlockSpec((None, 8, d), lambda bi, i: (bi, 0, 0))]
    in_specs += [tok(BRANCH_W), tok(BRANCH_W), pcol(P_GDN_Z // GDN_W), tok(BRANCH_W), tok(BRANCH_W),
                 pcol(P_LRU_Y // LRU_W), tok(BRANCH_W), tok(BRANCH_W)]
    in_specs += [pl.BlockSpec((None, tm, d), functools.partial(lambda bi, i, n: (bi, i, gate_blk + n), n=n))
                 for n in range(N_BRANCH)]
    in_specs += [_full(gn), _full(ones_n), _full(b_gate),
                 pl.BlockSpec((None,) + w_branch.shape[1:], lambda bi, i: (layer, 0, 0, 0)),
                 pl.BlockSpec((None,) + w_out.shape[1:], lambda bi, i: (layer, 0, 0))]
    return pl.pallas_call(
        functools.partial(_merge_kernel, n_lat=n_lat),
        grid=(b, rows // tm),
        in_specs=in_specs,
        out_specs=tok(d),
        out_shape=jax.ShapeDtypeStruct((b, rows, d), F32),
        input_output_aliases={0: 0} if with_ctx else {},
        compiler_params=_cparams(("parallel", "parallel")),
        name="merge",
    )(xs, mod_c, mod_l, gdn_out[0], gdn_out[1], p32, lru_out[0], lru_out[1], p32, uc, ud, p16, p16, p16, p16,
      gn, ones_n, b_gate, w_branch, w_out)


def _mlp_kernel(x_ref, mc_ref, ml_ref, gn_ref, gf_ref, w1_ref, w2_ref, o_ref, h_ref, acc_ref, *, n_lat, final_norm):
    f = pl.program_id(2)
    tm = x_ref.shape[0]
    i = pl.program_id(1)

    @pl.when(f == 0)
    def _():
        _norm_modulate(x_ref, h_ref, gn_ref, mc_ref, ml_ref, 3, i, n_lat)
        acc_ref[...] = jnp.zeros(acc_ref.shape, F32)

    a = jnp.maximum(jnp.dot(h_ref[...], w1_ref[...], preferred_element_type=F32), 0.0)
    acc_ref[...] += jnp.dot((a * a).astype(BF16), w2_ref[...], preferred_element_type=F32)

    @pl.when(f == pl.num_programs(2) - 1)
    def _():
        g2 = _row_select(tm, i, n_lat, mc_ref[5:6, :], ml_ref[5:6, :])
        y = x_ref[...] + g2 * acc_ref[...]
        if final_norm:
            y = y * lax.rsqrt(jnp.mean(y * y, axis=-1, keepdims=True) + RMS_EPS) * gf_ref[...]
        o_ref[...] = y


def _mlp(xs, mod_c, mod_l, gain, w1, w2, layer, final_gain, n_lat, final_norm):
    b, rows, d = xs.shape
    dff = w1.shape[-1]
    tm = _pick_tile(rows, 1088)
    tf = 1024
    row = pl.BlockSpec((1, d), lambda bi, i, f: (0, 0))
    return pl.pallas_call(
        functools.partial(_mlp_kernel, n_lat=n_lat, final_norm=final_norm),
        grid=(b, rows // tm, dff // tf),
        in_specs=[pl.BlockSpec((None, tm, d), lambda bi, i, f: (bi, i, 0)),
                  pl.BlockSpec((8, d), lambda bi, i, f: (0, 0)),
                  pl.BlockSpec((None, 8, d), lambda bi, i, f: (bi, 0, 0)),
                  row, row,
                  pl.BlockSpec((None, d, tf), lambda bi, i, f: (layer, 0, f)),
                  pl.BlockSpec((None, tf, d), lambda bi, i, f: (layer, f, 0))],
        out_specs=pl.BlockSpec((None, tm, d), lambda bi, i, f: (bi, i, 0)),
        out_shape=jax.ShapeDtypeStruct((b, rows, d), F32),
        scratch_shapes=[pltpu.VMEM((tm, d), BF16), pltpu.VMEM((tm, d), F32)],
        compiler_params=_cparams(("parallel", "parallel", "arbitrary")),
        name="mlp",
    )(xs, mod_c, mod_l, gain, final_gain, w1, w2)


def kernel(x, c, ctx, c_ctx, mod_w, mod_b, norm1_g, norm2_g, w_in, b_gate, gdn_conv_w, gdn_a_log, gdn_dt_bias,
           gdn_norm_g, lru_conv_w, lru_conv_b, lru_w_r, lru_b_r, lru_w_i, lru_b_i, lru_lambda, mla_q_norm_g,
           mla_w_uq, mla_kv_norm_g, mla_w_ukv, na_rpb, w_branch, w_out, mlp_w1, mlp_w2, final_norm_g):
    bsz, n_tok, d = x.shape
    n_ctx = ctx.shape[1]
    depth = w_in.shape[0]
    assert n_ctx % SEQ_TILE == 0 and n_tok % SEQ_TILE == 0 and n_tok % GRID_W == 0
    na_meta, na_rel_rows, na_cols = _na_geometry(n_tok // GRID_W)
    na_meta = jnp.asarray(na_meta)
    cos, sin = _rope_tables(n_tok, n_ctx)

    n_rows = -(-(bsz + 1) // 8) * 8
    cc = jnp.zeros((n_rows, d), F32).at[:bsz].set(c).at[bsz].set(c_ctx)
    final_gain = final_norm_g.reshape(1, d)

    w_in_all = _arrange_w_in(w_in)
    wb_all = w_branch.astype(BF16)
    wo_all = w_out.astype(BF16)
    w1_all = mlp_w1.astype(BF16)
    w2_all = mlp_w2.astype(BF16)

    xs = jnp.concatenate([x, ctx], axis=1)
    for l in range(depth):
        need_ctx = l < depth - 1
        mod = _modulation(cc, mod_w, mod_b[l], l).reshape(n_rows, N_MOD, d)
        pad = jnp.zeros((8 - N_MOD, d), F32)
        mod_c = jnp.concatenate([mod[bsz], pad], axis=0)
        mod_l = jnp.concatenate([mod[:bsz], jnp.broadcast_to(pad, (bsz, 8 - N_MOD, d))], axis=1)

        wq, wk, wv, place = _arrange_mla(mla_w_uq[l], mla_w_ukv[l])
        gq = mla_q_norm_g[l].reshape(1, -1)
        gkv = mla_kv_norm_g[l].reshape(1, -1)
        g1n = norm1_g[l].reshape(1, d)
        g2n = norm2_g[l].reshape(1, d)

        p32, p16 = _inproj(xs, mod_c, mod_l, g1n, w_in_all, l, n_tok)

        gq_, gk_, gv_, gbeta, gg = _gdn_prep(p32, gdn_conv_w[l], gdn_a_log[l], gdn_dt_bias[l], n_tok)
        h_f, h_b, o_f, o_b = _seq_mixers(
            _lru_scan_specs(p32, lru_conv_w[l], lru_conv_b[l], lru_w_r[l], lru_b_r[l], lru_w_i[l], lru_b_i[l],
                            lru_lambda[l], n_tok),
            _gdn_scan_specs(gq_, gk_, gv_, gbeta, gg, n_tok), bsz, (n_tok + n_ctx) // SEQ_TILE, n_tok)

        mq, mk, mv = _mla_prep(p32, p16, cos, sin, gq, gkv, wq, wk, wv, place)
        uc = _mla_flash(mq, mk, mv, n_tok)
        if need_ctx:
            uc = _mla_flash(mq, mk, mv, n_tok, prev=uc)
        ud = _na_attention(p16, na_meta, _na_bias_table(na_rpb[l], na_rel_rows, na_cols), n_tok, need_ctx)

        xs = _merge(xs, mod_c, mod_l, (o_f, o_b), (h_f, h_b), uc, ud, p32, p16, gdn_norm_g[l], b_gate[l],
                    wb_all, wo_all, l, n_tok, need_ctx)
        xs = _mlp(xs, mod_c, mod_l, g2n, w1_all, w2_all, l, final_gain, n_tok, l == depth - 1)
    return xs
```

```python
import functools

import jax
import jax.numpy as jnp
import numpy as np
from jax import lax
from jax.experimental import pallas as pl
from jax.experimental.pallas import tpu as pltpu

F32 = jnp.float32
BF16 = jnp.bfloat16

GRID_W = 64
N_MOD = 6
RMS_EPS = 1e-6
GDN_HEADS = 4
GDN_DK = 64
GDN_DV = 64
GDN_CHUNK = 64
GDN_W = GDN_HEADS * GDN_DV
GDN_CONV = 4
LRU_W = 256
LRU_BLOCKS = 4
LRU_BLOCK_W = LRU_W // LRU_BLOCKS
LRU_CONV = 4
LRU_C = 8.0
MLA_HEADS = 4
MLA_Q_RANK = 256
MLA_KV_RANK = 128
MLA_NOPE = 64
MLA_ROPE = 32
MLA_V = 64
MLA_SLOT = 128
ROPE_BASE = 10000.0
LOG2_E = 1.4426950408889634
NA_HEADS = 4
NA_DH = 64
NA_W = NA_HEADS * NA_DH
NA_WIN_ROWS = 8
NA_WIN_COLS = 16
N_BRANCH = 4
BRANCH_W = 256

SEQ_TILE = 256
HALO = 8

_REF_COLS = {}
_off = 0
for _name, _w in (('gdn_qkv', 3 * GDN_W), ('gdn_z', GDN_W), ('gdn_beta', 2 * GDN_HEADS), ('gdn_a', 2 * GDN_HEADS),
                  ('lru_x', LRU_W), ('lru_y', LRU_W), ('mla_q', MLA_Q_RANK), ('mla_kv', MLA_KV_RANK),
                  ('mla_kr', MLA_ROPE), ('na_qkv', 3 * NA_W)):
    _REF_COLS[_name] = (_off, _w)
    _off += _w
N_MIX_COLS = _off

P_GDN_QKV = 0
P_GDN_Z = 768
P_LRU_X = 1024
P_LRU_Y = 1280
P_MLA_KV = 1536
P_MLA_KR = 1664
P_GDN_BA = 1792
P32_COLS = 2048
P_MLA_Q = 0
P_NA_QKV = 256
P_GATES = 1024
P16_COLS = P_GATES + N_BRANCH * 1024
PROJ_TILE = 1024

VMEM_LIMIT = 52 * 1024 * 1024


def _cparams(sem):
    return pltpu.CompilerParams(dimension_semantics=sem, vmem_limit_bytes=VMEM_LIMIT)


def _pick_tile(n, cap):
    best = 8
    for t in range(8, min(n, cap) + 1, 8):
        if n % t == 0:
            best = t
    return best


def _full(a):
    return pl.BlockSpec(a.shape, lambda *_: (0,) * a.ndim)


N_SPLIT = 2


def _split(x):
    hi = x.astype(BF16)
    lo = (x - hi.astype(F32)).astype(BF16)
    return hi, lo


def _sigmoid(x):
    return 0.5 * jnp.tanh(0.5 * x) + 0.5


def _arrange_w_in(w_in):
    pieces, pos = [], 0

    def put(dst, block):
        nonlocal pos
        if dst > pos:
            pieces.append(jnp.zeros(w_in.shape[:-1] + (dst - pos,), w_in.dtype))
        pieces.append(block)
        pos = dst + block.shape[-1]

    ref = lambda name: w_in[..., _REF_COLS[name][0]:_REF_COLS[name][0] + _REF_COLS[name][1]]
    for name, dst in (('gdn_qkv', P_GDN_QKV), ('gdn_z', P_GDN_Z), ('lru_x', P_LRU_X), ('lru_y', P_LRU_Y),
                      ('mla_kv', P_MLA_KV), ('mla_kr', P_MLA_KR), ('gdn_beta', P_GDN_BA),
                      ('gdn_a', P_GDN_BA + 2 * GDN_HEADS)):
        put(dst, ref(name))
    put(P32_COLS + P_MLA_Q, ref('mla_q'))
    put(P32_COLS + P_NA_QKV, ref('na_qkv'))
    put(P32_COLS + P_GATES, w_in[..., N_MIX_COLS:])
    assert pos == P32_COLS + P16_COLS
    return jnp.concatenate(pieces, axis=-1).astype(BF16)


def _rope_perm():
    q = MLA_ROPE // 4
    src = np.zeros(MLA_ROPE, np.int32)
    sign = np.zeros(MLA_ROPE, np.float32)
    for base in (0, 2 * q):
        for d in range(q):
            src[base + d] = base + d + q
            sign[base + d] = -1.0
            src[base + q + d] = base + d
            sign[base + q + d] = 1.0
    return src, sign


def _arrange_mla(w_uq, w_ukv):
    src, sign = _rope_perm()
    hq = MLA_NOPE + MLA_ROPE
    wq = jnp.zeros((MLA_Q_RANK, 2 * MLA_HEADS * MLA_SLOT), F32)
    wk = jnp.zeros((MLA_KV_RANK, MLA_HEADS * MLA_SLOT), F32)
    wv = jnp.zeros((MLA_KV_RANK, MLA_HEADS * MLA_V), F32)
    place = np.zeros((2 * MLA_SLOT, 2 * MLA_HEADS * MLA_SLOT), np.float32)
    rot_off = MLA_HEADS * MLA_SLOT
    for h in range(MLA_HEADS):
        nope = w_uq[:, h * hq:h * hq + MLA_NOPE]
        pe = w_uq[:, h * hq + MLA_NOPE:(h + 1) * hq]
        s = h * MLA_SLOT
        wq = wq.at[:, s:s + MLA_NOPE].set(nope)
        wq = wq.at[:, s + MLA_NOPE:s + MLA_NOPE + MLA_ROPE].set(pe)
        wq = wq.at[:, rot_off + s + MLA_NOPE:rot_off + s + MLA_NOPE + MLA_ROPE].set(pe[:, src] * sign)
        wk = wk.at[:, s:s + MLA_NOPE].set(w_ukv[:, h * 128:h * 128 + MLA_NOPE])
        wv = wv.at[:, h * MLA_V:(h + 1) * MLA_V].set(w_ukv[:, h * 128 + MLA_NOPE:(h + 1) * 128])
        for d in range(MLA_ROPE):
            for half in (0, MLA_SLOT):
                place[half + d, s + MLA_NOPE + d] = 1.0
                place[half + src[d], rot_off + s + MLA_NOPE + d] = sign[d]
    return wq.astype(BF16), wk.astype(BF16), wv.astype(BF16), jnp.asarray(place, BF16)


def _rope_tables(n_tok, n_ctx):
    cos = np.ones((n_tok + n_ctx, MLA_SLOT), np.float32)
    sin = np.zeros((n_tok + n_ctx, MLA_SLOT), np.float32)
    t = np.arange(n_tok)
    row = (t // GRID_W).astype(np.float32)
    col = (t % GRID_W).astype(np.float32)
    n_freq = MLA_ROPE // 4
    inv = (ROPE_BASE ** (-np.arange(n_freq, dtype=np.float32) / n_freq)).astype(np.float32)
    ar = row[:, None] * inv
    ac = col[:, None] * inv
    ang = np.concatenate([ar, ar, ac, ac], axis=-1).astype(np.float32)
    cos[:n_tok, MLA_NOPE:MLA_NOPE + MLA_ROPE] = np.cos(ang)
    sin[:n_tok, MLA_NOPE:MLA_NOPE + MLA_ROPE] = np.sin(ang)
    return jnp.asarray(cos), jnp.asarray(sin)


def _head_block_ones(n_heads, width):
    m = np.kron(np.eye(n_heads, dtype=np.float32), np.ones((width, width), np.float32))
    return jnp.asarray(m, BF16)


def _mod_kernel(c_ref, w_ref, b_ref, o_ref):
    c = c_ref[...]
    s = c * _sigmoid(c)
    o_ref[...] = jnp.dot(s, w_ref[...], preferred_element_type=F32) + b_ref[...]


def _modulation(cc, mod_w, mod_b, layer):
    r, d = cc.shape
    n = mod_w.shape[-1]
    tn = 1024
    return pl.pallas_call(
        _mod_kernel,
        grid=(n // tn,),
        in_specs=[pl.BlockSpec((r, d), lambda j: (0, 0)),
                  pl.BlockSpec((None, d, tn), lambda j: (layer, 0, j)),
                  pl.BlockSpec((1, tn), lambda j: (0, j))],
        out_specs=pl.BlockSpec((r, tn), lambda j: (0, j)),
        out_shape=jax.ShapeDtypeStruct((r, n), F32),
        compiler_params=_cparams(("arbitrary",)),
        name="modulation",
    )(cc, mod_w, mod_b.reshape(1, n))


def _row_select(tile_rows, tile_idx, n_lat, ctx_vec, lat_vec):
    row = tile_idx * tile_rows + lax.broadcasted_iota(jnp.int32, (tile_rows, 1), 0)
    return jnp.where(row < n_lat, lat_vec, ctx_vec)


def _norm_modulate(x_ref, h_ref, gain_ref, mc_ref, ml_ref, shift_row, tile_idx, n_lat):
    tm = x_ref.shape[0]
    x = x_ref[...]
    xn = x * lax.rsqrt(jnp.mean(x * x, axis=-1, keepdims=True) + RMS_EPS)
    sh_l, sh_c = ml_ref[shift_row:shift_row + 1, :], mc_ref[shift_row:shift_row + 1, :]
    amp_l = gain_ref[...] * (1.0 + ml_ref[shift_row + 1:shift_row + 2, :])
    amp_c = gain_ref[...] * (1.0 + mc_ref[shift_row + 1:shift_row + 2, :])
    all_latent = (tile_idx + 1) * tm <= n_lat

    @pl.when(all_latent)
    def _():
        h_ref[...] = (xn * amp_l + sh_l).astype(h_ref.dtype)

    @pl.when(jnp.logical_not(all_latent))
    def _():
        amp = _row_select(tm, tile_idx, n_lat, amp_c, amp_l)
        shift = _row_select(tm, tile_idx, n_lat, sh_c, sh_l)
        h_ref[...] = (xn * amp + shift).astype(h_ref.dtype)


def _inproj_first_kernel(x_ref, mc_ref, ml_ref, g_ref, w_ref, o_ref, h_ref, *, n_lat):
    @pl.when(pl.program_id(2) == 0)
    def _():
        _norm_modulate(x_ref, h_ref, g_ref, mc_ref, ml_ref, 0, pl.program_id(1), n_lat)

    o_ref[...] = jnp.dot(h_ref[...], w_ref[...], preferred_element_type=F32).astype(o_ref.dtype)


def _inproj_second_kernel(h_ref, w_ref, o_ref):
    o_ref[...] = jnp.dot(h_ref[...], w_ref[...], preferred_element_type=F32).astype(o_ref.dtype)


def _inproj(xs, mod_c, mod_l, gain, w, layer, n_lat):
    b, s, d = xs.shape
    tm = _pick_tile(s, 1088)
    tn = PROJ_TILE
    n32 = P32_COLS // tn
    p16, h = pl.pallas_call(
        functools.partial(_inproj_first_kernel, n_lat=n_lat),
        grid=(b, s // tm, P16_COLS // tn),
        in_specs=[pl.BlockSpec((None, tm, d), lambda bi, i, j: (bi, i, 0)),
                  pl.BlockSpec((8, d), lambda bi, i, j: (0, 0)),
                  pl.BlockSpec((None, 8, d), lambda bi, i, j: (bi, 0, 0)),
                  pl.BlockSpec((1, d), lambda bi, i, j: (0, 0)),
                  pl.BlockSpec((None, d, tn), lambda bi, i, j: (layer, 0, j + n32))],
        out_specs=[pl.BlockSpec((None, tm, tn), lambda bi, i, j: (bi, i, j)),
                   pl.BlockSpec((None, tm, d), lambda bi, i, j: (bi, i, 0))],
        out_shape=[jax.ShapeDtypeStruct((b, s, P16_COLS), BF16), jax.ShapeDtypeStruct((b, s, d), BF16)],
        compiler_params=_cparams(("parallel", "parallel", "arbitrary")),
        name="inproj_bf16",
    )(xs, mod_c, mod_l, gain, w)
    p32 = pl.pallas_call(
        _inproj_second_kernel,
        grid=(b, s // tm, n32),
        in_specs=[pl.BlockSpec((None, tm, d), lambda bi, i, j: (bi, i, 0)),
                  pl.BlockSpec((None, d, tn), lambda bi, i, j: (layer, 0, j))],
        out_specs=pl.BlockSpec((None, tm, tn), lambda bi, i, j: (bi, i, j)),
        out_shape=jax.ShapeDtypeStruct((b, s, P32_COLS), F32),
        compiler_params=_cparams(("parallel", "parallel", "arbitrary")),
        name="inproj_f32",
    )(h, w)
    return p32, p16


def _mla_prep_kernel(ql_ref, kv_ref, cos_ref, sin_ref, gq_ref, gkv_ref, wq_ref, wk_ref, wv_ref, pl_ref,
                     q_ref, k_ref, v_ref, *, scale):
    nslot = MLA_HEADS * MLA_SLOT
    cos = jnp.concatenate([cos_ref[...]] * MLA_HEADS, axis=-1)
    sin = jnp.concatenate([sin_ref[...]] * MLA_HEADS, axis=-1)

    ql = ql_ref[...].astype(F32)
    qn = ql * lax.rsqrt(jnp.mean(ql * ql, axis=-1, keepdims=True) + RMS_EPS) * gq_ref[...]
    q2 = jnp.dot(qn.astype(BF16), wq_ref[...], preferred_element_type=F32)
    q = (q2[:, :nslot] * cos + q2[:, nslot:] * sin) * scale
    q_ref[...] = q.astype(BF16)

    kvkr = kv_ref[...]
    kvl = kvkr[:, :MLA_KV_RANK]
    kr = kvkr[:, MLA_KV_RANK:]
    kvn = (kvl * lax.rsqrt(jnp.mean(kvl * kvl, axis=-1, keepdims=True) + RMS_EPS) * gkv_ref[...]).astype(BF16)
    kn = jnp.dot(kvn, wk_ref[...], preferred_element_type=F32)
    v_ref[...] = jnp.dot(kvn, wv_ref[...], preferred_element_type=F32).astype(BF16)
    kr_hi = kr.astype(BF16)
    kr_lo = (kr - kr_hi.astype(F32)).astype(BF16)
    kr2 = jnp.dot(jnp.concatenate([kr_hi, kr_lo], axis=-1), pl_ref[...], preferred_element_type=F32)
    k_ref[...] = (kn + kr2[:, :nslot] * cos + kr2[:, nslot:] * sin).astype(BF16)


def _mla_prep(p32, p16, cos, sin, gq, gkv, wq, wk, wv, place):
    b, s, _ = p32.shape
    tm = _pick_tile(s, 544)
    nslot = MLA_HEADS * MLA_SLOT
    scale = (MLA_NOPE + MLA_ROPE) ** -0.5 * LOG2_E
    return pl.pallas_call(
        functools.partial(_mla_prep_kernel, scale=scale),
        grid=(b, s // tm),
        in_specs=[pl.BlockSpec((None, tm, MLA_Q_RANK), lambda bi, i: (bi, i, P_MLA_Q // MLA_Q_RANK)),
                  pl.BlockSpec((None, tm, 2 * MLA_SLOT), lambda bi, i: (bi, i, P_MLA_KV // (2 * MLA_SLOT))),
                  pl.BlockSpec((tm, MLA_SLOT), lambda bi, i: (i, 0)),
                  pl.BlockSpec((tm, MLA_SLOT), lambda bi, i: (i, 0)),
                  _full(gq), _full(gkv), _full(wq), _full(wk), _full(wv), _full(place)],
        out_specs=[pl.BlockSpec((None, tm, nslot), lambda bi, i: (bi, i, 0)),
                   pl.BlockSpec((None, tm, nslot), lambda bi, i: (bi, i, 0)),
                   pl.BlockSpec((None, tm, MLA_HEADS * MLA_V), lambda bi, i: (bi, i, 0))],
        out_shape=[jax.ShapeDtypeStruct((b, s, nslot), BF16),
                   jax.ShapeDtypeStruct((b, s, nslot), BF16),
                   jax.ShapeDtypeStruct((b, s, MLA_HEADS * MLA_V), BF16)],
        compiler_params=_cparams(("parallel", "parallel")),
        name="mla_prep",
    )(p16, p32, cos, sin, gq, gkv, wq, wk, wv, place)


def _flash_softmax(h, s, m_ref, l_ref):
    m_prev = m_ref[h]
    m_new = jnp.maximum(m_prev, jnp.max(s, axis=-1, keepdims=True))
    alpha = jnp.exp2(m_prev - m_new)
    p = jnp.exp2(s - jnp.concatenate([m_new] * (s.shape[1] // 128), axis=-1))
    l_ref[h] = alpha * l_ref[h] + jnp.sum(p, axis=-1, keepdims=True)
    m_ref[h] = m_new
    return alpha, p.astype(BF16)


def _mla_flash_kernel(*refs, tk, aliased):
    if aliased:
        q_ref, k_ref, v_ref, _, o_ref, m_ref, l_ref, acc_ref = refs
    else:
        q_ref, k_ref, v_ref, o_ref, m_ref, l_ref, acc_ref = refs
    tq = q_ref.shape[0]
    nk = k_ref.shape[0]
    n_loop = nk // tk
    m_ref[...] = jnp.full(m_ref.shape, -jnp.inf, F32)
    l_ref[...] = jnp.zeros(l_ref.shape, F32)
    acc_ref[...] = jnp.zeros(acc_ref.shape, F32)

    def chunk(rows):
        heads = range(MLA_HEADS)
        hs = [slice(h * MLA_SLOT, (h + 1) * MLA_SLOT) for h in heads]
        vs = [slice((h // 2) * 128, (h // 2) * 128 + 128) for h in heads]
        s = [lax.dot_general(q_ref[:, hs[h]], k_ref[rows, hs[h]], (((1,), (1,)), ((), ())),
                             preferred_element_type=F32) for h in heads]
        ap = [_flash_softmax(h, s[h], m_ref, l_ref) for h in heads]
        for h in heads:
            alpha, p = ap[h]
            acc_ref[h] = alpha * acc_ref[h] + jnp.dot(p, v_ref[rows, vs[h]], preferred_element_type=F32)

    if n_loop:
        def body(j, carry):
            chunk(pl.ds(pl.multiple_of(j * tk, tk), tk))
            return carry
        lax.fori_loop(0, n_loop, body, 0)
    if nk > n_loop * tk:
        chunk(slice(n_loop * tk, nk))

    lane = lax.broadcasted_iota(jnp.int32, (tq, 128), 1)
    outs = []
    for pair in range(MLA_HEADS // 2):
        o0 = acc_ref[2 * pair] / l_ref[2 * pair]
        o1 = acc_ref[2 * pair + 1] / l_ref[2 * pair + 1]
        outs.append(jnp.where(lane < MLA_V, o0, o1))
    o_ref[...] = jnp.concatenate(outs, axis=-1).astype(o_ref.dtype)


def _mla_flash(q, k, v, n_lat, prev=None):
    b, s, nslot = q.shape
    n_ctx = s - n_lat
    nv = MLA_HEADS * MLA_V
    tk = 512
    if prev is None:
        tq = _pick_tile(n_lat, 1024)
        q_off, n_q, kv_rows, kv_blk = 0, n_lat // tq, s, 0
    else:
        assert n_lat % n_ctx == 0
        tq = _pick_tile(n_ctx, 256)
        q_off, n_q, kv_rows, kv_blk = n_lat // tq, n_ctx // tq, n_ctx, n_lat // n_ctx
    in_specs = [pl.BlockSpec((None, tq, nslot), lambda bi, i: (bi, i + q_off, 0)),
                pl.BlockSpec((None, kv_rows, nslot), lambda bi, i: (bi, kv_blk, 0)),
                pl.BlockSpec((None, kv_rows, nv), lambda bi, i: (bi, kv_blk, 0))]
    args = [q, k, v]
    aliases = {}
    if prev is not None:
        in_specs.append(pl.BlockSpec(memory_space=pl.ANY))
        args.append(prev)
        aliases = {3: 0}
    return pl.pallas_call(
        functools.partial(_mla_flash_kernel, tk=tk, aliased=prev is not None),
        grid=(b, n_q),
        in_specs=in_specs,
        out_specs=pl.BlockSpec((None, tq, nv), lambda bi, i: (bi, i + q_off, 0)),
        out_shape=jax.ShapeDtypeStruct((b, s, nv), BF16),
        scratch_shapes=[pltpu.VMEM((MLA_HEADS, tq, 128), F32),
                        pltpu.VMEM((MLA_HEADS, tq, 128), F32),
                        pltpu.VMEM((MLA_HEADS, tq, 128), F32)],
        input_output_aliases=aliases,
        compiler_params=_cparams(("parallel", "arbitrary")),
        name="mla_flash",
    )(*args)


NA_QROWS = 4
NA_SLAB = NA_QROWS + NA_WIN_ROWS


def _na_geometry(rows):
    assert rows % NA_QROWS == 0 and rows >= NA_SLAB
    nblk = rows // NA_QROWS
    qc = np.arange(GRID_W)
    cs = np.clip(qc - NA_WIN_COLS // 2, 0, GRID_W - NA_WIN_COLS)
    col_valid = (qc[None, :] >= cs[:, None]) & (qc[None, :] < cs[:, None] + NA_WIN_COLS)
    rel_c = np.clip(qc[None, :] - qc[:, None] + NA_WIN_COLS - 1, 0, 2 * NA_WIN_COLS - 2)
    onehot_c = (rel_c[None] == np.arange(2 * NA_WIN_COLS - 1)[:, None, None]) & col_valid[None]
    patterns, cls, starts = {}, [], []
    for i in range(nblk):
        r0 = i * NA_QROWS
        start = int(np.clip(r0 - NA_WIN_ROWS // 2, 0, rows - NA_SLAB))
        qr = r0 + np.arange(NA_QROWS)
        rs = np.clip(qr - NA_WIN_ROWS // 2, 0, rows - NA_WIN_ROWS)
        key = (start - r0,) + tuple((rs - r0).tolist())
        if key not in patterns:
            kr = start + np.arange(NA_SLAB)
            row_valid = (kr[None, :] >= rs[:, None]) & (kr[None, :] < rs[:, None] + NA_WIN_ROWS)
            rel_r = np.clip(kr[None, :] - qr[:, None] + NA_WIN_ROWS - 1, 0, 2 * NA_WIN_ROWS - 2)
            patterns[key] = (len(patterns), np.where(row_valid, rel_r, -1))
        cls.append(patterns[key][0])
        starts.append(start)
    ordered = sorted(patterns.values(), key=lambda z: z[0])
    rel_rows = np.stack([z[1] for z in ordered])
    meta = np.stack([np.asarray(cls, np.int32), np.asarray(starts, np.int32)])
    return meta, rel_rows, (onehot_c.astype(np.float32), col_valid)


def _na_bias_table(rpb, rel_rows, col_tables):
    onehot_c, col_valid = col_tables
    bc = jnp.einsum('hrc,cqk->hrqk', rpb.astype(F32), onehot_c, precision=lax.Precision.HIGHEST)
    bc = jnp.where(col_valid, bc, -jnp.inf)
    outside = jnp.full(bc.shape[:1] + bc.shape[2:], -jnp.inf, F32)
    pats = []
    for pat in rel_rows:
        qrows = [jnp.concatenate([bc[:, r] if r >= 0 else outside for r in row], axis=-1) for row in pat]
        pats.append(jnp.concatenate(qrows, axis=1))
    return jnp.stack(pats)


def _na_scores(q_pair, lane, h, k_parts):
    in_head = (lane < NA_DH) if h % 2 == 0 else (lane >= NA_DH)
    qm = jnp.where(in_head, q_pair * (NA_DH ** -0.5), 0.0).astype(BF16)
    return [lax.dot_general(qm, k, (((1,), (1,)), ((), ())), preferred_element_type=F32) for k in k_parts]


def _na_softmax(scores, bias):
    if bias is not None:
        scores = [scores[0] + bias] + scores[1:]
    m = scores[0].max(axis=-1, keepdims=True)
    for s in scores[1:]:
        m = jnp.maximum(m, s.max(axis=-1, keepdims=True))
    probs = [jnp.exp(s - m) for s in scores]
    den = probs[0].sum(axis=-1, keepdims=True)
    for p in probs[1:]:
        den = den + p.sum(axis=-1, keepdims=True)
    return [p.astype(BF16) for p in probs], den


def _na_kernel(meta_ref, q_ref, k_ref, v_ref, bias_ref, o_ref, *, n_lat, with_ctx):
    i = pl.program_id(1)
    nq = q_ref.shape[0]
    n_lat_tiles = n_lat // nq
    n_all = k_ref.shape[0]
    lane = lax.broadcasted_iota(jnp.int32, (nq, 128), 1)

    def run(windowed):
        key_rows = [pl.ds(n_lat, n_all - n_lat)]
        if windowed:
            start = pl.multiple_of(meta_ref[1, jnp.minimum(i, n_lat_tiles - 1)] * GRID_W, GRID_W)
            key_rows = [pl.ds(start, NA_SLAB * GRID_W)] + key_rows
        pair_lanes = [slice((h // 2) * 128, (h // 2 + 1) * 128) for h in range(NA_HEADS)]
        scores = [_na_scores(q_ref[:, pair_lanes[h]], lane, h, [k_ref[r, pair_lanes[h]] for r in key_rows])
                  for h in range(NA_HEADS)]
        soft = [_na_softmax(scores[h], bias_ref[h] if windowed else None) for h in range(NA_HEADS)]
        heads = []
        for h in range(NA_HEADS):
            probs, den = soft[h]
            out = None
            for p, r in zip(probs, key_rows):
                po = jnp.dot(p, v_ref[r, pair_lanes[h]], preferred_element_type=F32)
                out = po if out is None else out + po
            heads.append(out / den)
        outs = [jnp.where(lane < NA_DH, heads[2 * pair], heads[2 * pair + 1]) for pair in range(NA_HEADS // 2)]
        o_ref[...] = jnp.concatenate(outs, axis=-1).astype(o_ref.dtype)

    if with_ctx:
        pl.when(i < n_lat_tiles)(lambda: run(True))
        pl.when(i >= n_lat_tiles)(lambda: run(False))
    else:
        run(True)


def _na_attention(p, meta, table, n_lat, with_ctx_queries):
    b, s, _ = p.shape
    nq = NA_QROWS * GRID_W
    nk = NA_SLAB * GRID_W
    assert (s - n_lat) % nq == 0
    nlt = n_lat // nq
    qb = P_NA_QKV // NA_W
    grid_spec = pltpu.PrefetchScalarGridSpec(
        num_scalar_prefetch=1,
        grid=(b, s // nq if with_ctx_queries else nlt),
        in_specs=[pl.BlockSpec((None, nq, NA_W), lambda bi, i, m: (bi, i, qb)),
                  pl.BlockSpec((None, s, NA_W), lambda bi, i, m: (bi, 0, qb + 1)),
                  pl.BlockSpec((None, s, NA_W), lambda bi, i, m: (bi, 0, qb + 2)),
                  pl.BlockSpec((None, NA_HEADS, nq, nk),
                               lambda bi, i, m: (m[0, jnp.minimum(i, nlt - 1)], 0, 0, 0))],
        out_specs=pl.BlockSpec((None, nq, NA_W), lambda bi, i, m: (bi, i, 0)))
    return pl.pallas_call(
        functools.partial(_na_kernel, n_lat=n_lat, with_ctx=with_ctx_queries),
        grid_spec=grid_spec,
        out_shape=jax.ShapeDtypeStruct((b, s, NA_W), BF16),
        compiler_params=_cparams(("parallel", "arbitrary")),
        name="na_attention",
    )(meta, p, p, p, table)


def _tile_conv(x, prev, nxt, w_ref, tile_idx, n_tiles, n_lat_tiles):
    r = x.shape[0]
    width = w_ref.shape[0]
    left = width // 2
    has_prev = jnp.logical_and(tile_idx != 0, tile_idx != n_lat_tiles)
    has_next = jnp.logical_and(tile_idx != n_tiles - 1, tile_idx != n_lat_tiles - 1)
    prev = jnp.where(has_prev, prev, 0.0)
    nxt = jnp.where(has_next, nxt, 0.0)
    xe = jnp.concatenate([prev, x, nxt], axis=0)
    acc = None
    for j in range(width):
        o = HALO - left + j
        term = xe[o:o + r, :] * w_ref[j:j + 1, :]
        acc = term if acc is None else acc + term
    return acc


def _halo_specs(width, col_block, tile_of):
    per = SEQ_TILE // HALO

    def main(bi, s, *_):
        return (bi, tile_of(s), col_block)

    def prev(bi, s, *_):
        return (bi, jnp.maximum(tile_of(s) * per - 1, 0), col_block)

    def make_next(n_tiles):
        def nxt(bi, s, *_):
            return (bi, jnp.minimum((tile_of(s) + 1) * per, n_tiles * per - 1), col_block)
        return nxt

    return main, prev, make_next


def _gdn_prep_kernel(x_ref, xp_ref, xn_ref, ba_ref, cw_ref, ones_ref, exp_ref, alog_ref, dtb_ref,
                     q_ref, k_ref, v_ref, beta_ref, g_ref, *, n_lat_tiles):
    i = pl.program_id(1)
    y = _tile_conv(x_ref[...], xp_ref[...], xn_ref[...], cw_ref, i, pl.num_programs(1), n_lat_tiles)
    y = y * _sigmoid(y)
    q = y[:, :GDN_W]
    k = y[:, GDN_W:2 * GDN_W]
    v_ref[...] = y[:, 2 * GDN_W:].astype(v_ref.dtype)

    def head_norm(u):
        parts = jnp.concatenate(_split(u * u), axis=-1)
        ss = jnp.dot(parts, ones_ref[...], preferred_element_type=F32)
        return u * lax.rsqrt(ss + RMS_EPS)

    q_ref[...] = (head_norm(q) * (GDN_DK ** -0.5)).astype(q_ref.dtype)
    k_ref[...] = head_norm(k).astype(k_ref.dtype)

    ba = ba_ref[...]
    a = ba + dtb_ref[...]
    softplus = jnp.maximum(a, 0.0) + jnp.log1p(jnp.exp(-jnp.abs(a)))
    lane = lax.broadcasted_iota(jnp.int32, ba.shape, 1)
    compact = jnp.where(lane < 2 * GDN_HEADS, _sigmoid(ba), -jnp.exp(alog_ref[...]) * softplus)
    wide = jnp.dot(jnp.concatenate(_split(compact), axis=-1), exp_ref[...], preferred_element_type=F32)
    half = 2 * GDN_W
    beta_ref[...] = wide[:, :half].astype(beta_ref.dtype)
    g_ref[...] = wide[:, half:]


def _gdn_prep(p, conv_w, a_log, dt_bias, n_lat):
    b, s, _ = p.shape
    n_tiles = s // SEQ_TILE
    main, prev, make_next = _halo_specs(3 * GDN_W, 0, lambda t: t)
    ones_n = jnp.concatenate([_head_block_ones(GDN_HEADS, GDN_DK)] * N_SPLIT, axis=0)
    expand = np.zeros((128, 4 * GDN_W), np.float32)
    for kind in range(2):
        for d in range(2):
            for h in range(GDN_HEADS):
                c0 = kind * 2 * GDN_W + d * GDN_W + h * GDN_DV
                expand[kind * 2 * GDN_HEADS + d * GDN_HEADS + h, c0:c0 + GDN_DV] = 1.0
    expand_n = jnp.asarray(np.concatenate([expand] * N_SPLIT, axis=0), BF16)
    lanes = jnp.zeros((1, 128), F32)
    alog_e = lanes.at[0, 2 * GDN_HEADS:4 * GDN_HEADS].set(a_log.astype(F32).reshape(-1))
    dtb_e = lanes.at[0, 2 * GDN_HEADS:4 * GDN_HEADS].set(dt_bias.astype(F32).reshape(-1))
    tok = lambda w: pl.BlockSpec((None, SEQ_TILE, w), lambda bi, i: (bi, i, 0))
    return pl.pallas_call(
        functools.partial(_gdn_prep_kernel, n_lat_tiles=n_lat // SEQ_TILE),
        grid=(b, n_tiles),
        in_specs=[pl.BlockSpec((None, SEQ_TILE, 3 * GDN_W), main),
                  pl.BlockSpec((None, HALO, 3 * GDN_W), prev),
                  pl.BlockSpec((None, HALO, 3 * GDN_W), make_next(n_tiles)),
                  pl.BlockSpec((None, SEQ_TILE, 128), lambda bi, i: (bi, i, P_GDN_BA // 128)),
                  _full(conv_w), _full(ones_n), _full(expand_n), _full(alog_e), _full(dtb_e)],
        out_specs=[tok(GDN_W), tok(GDN_W), tok(GDN_W), tok(2 * GDN_W), tok(2 * GDN_W)],
        out_shape=[jax.ShapeDtypeStruct((b, s, GDN_W), BF16)] * 3 + [jax.ShapeDtypeStruct((b, s, 2 * GDN_W), BF16),
                                                                    jax.ShapeDtypeStruct((b, s, 2 * GDN_W), F32)],
        compiler_params=_cparams(("parallel", "parallel")),
        name="gdn_prep",
    )(p, p, p, p, conv_w, ones_n, expand_n, alog_e, dtb_e)


GDN_PAIR = 2 * GDN_DK


def _gdn_masks():
    c, w = GDN_CHUNK, GDN_W
    r2, c2 = np.arange(GDN_PAIR)[:, None], np.arange(GDN_PAIR)[None, :]
    bd = ((r2 // c) == (c2 // c)).astype(np.float32)
    i = np.arange(c)[:, None]
    j = (np.arange(w) % c)[None, :]
    level = np.zeros((c, w), np.int32)
    for bit in range(6):
        level += ((i ^ j) >= (1 << bit)).astype(np.int32)
    lvl = np.stack([(level == m).astype(np.float32) for m in range(7)])
    dirm = np.stack([np.stack([(j <= i), (j < i)]), np.stack([(j >= i), (j > i)])]).astype(np.float32)
    tj = (np.arange(N_SPLIT * c) % c)[None, :]
    tri = np.stack([(tj <= i), (tj >= i)]).astype(np.float32)
    return jnp.asarray(bd, BF16), jnp.asarray(lvl), jnp.asarray(dirm), jnp.asarray(tri, BF16)


def _heads_mm(x, y, bd, transpose_rhs=False):
    xb = x.astype(BF16)
    yb = y.astype(BF16)
    outs = []
    for pair in range(GDN_W // GDN_PAIR):
        ls = slice(pair * GDN_PAIR, (pair + 1) * GDN_PAIR)
        w = jnp.concatenate([yb[:, ls], yb[:, ls]], axis=0) * bd
        dims = (((1,), (1,)), ((), ())) if transpose_rhs else (((1,), (0,)), ((), ()))
        outs.append(lax.dot_general(xb[:, ls], w, dims, preferred_element_type=F32))
    return jnp.concatenate(outs, axis=1)


def _gdn_intra(probs, bd, lvl_ref, dirm_ref, tri_ref):
    c = GDN_CHUNK
    n = len(probs)
    eye = lvl_ref[0]
    gc, g_last, decay, gram = [], [], [], []
    for q, k, v, beta, g, rev in probs:
        d = 1 if rev else 0
        gcp = jnp.dot(tri_ref[d], jnp.concatenate(_split(g), axis=0), preferred_element_type=F32)
        gc.append(gcp)
        g_last.append(gcp[0:1, :] if rev else gcp[c - 1:c, :])
        gc_row = jnp.sum(gcp * eye, axis=0, keepdims=True)
        decay.append(dirm_ref[d, 0] * jnp.exp(jnp.minimum(gcp - gc_row, 0.0)))
        gram.append(_heads_mm(jnp.concatenate([k, q], axis=0), k, bd, transpose_rhs=True))
    lower = [dirm_ref[1 if p[5] else 0, 1] * p[3] * gram[x][:c] * decay[x] for x, p in enumerate(probs)]
    a_intra = [gram[x][c:] * decay[x] for x in range(n)]
    t = [eye - lower[x] * lvl_ref[1] for x in range(n)]
    for lev in range(2, 7):
        y = [_heads_mm(t[x], lower[x] * lvl_ref[lev], bd) for x in range(n)]
        z = [_heads_mm(y[x], t[x], bd) for x in range(n)]
        t = [t[x] - z[x] for x in range(n)]
    e_gc = [jnp.exp(gc[x]) for x in range(n)]
    u = [_heads_mm(t[x], p[2] * p[3], bd) for x, p in enumerate(probs)]
    w = [_heads_mm(t[x], p[1] * p[3] * e_gc[x], bd) for x, p in enumerate(probs)]
    wq = [jnp.concatenate([w[x], p[0] * e_gc[x]], axis=0).astype(BF16) for x, p in enumerate(probs)]
    k_dec = [(p[1] * jnp.exp(g_last[x] - gc[x])).astype(BF16) for x, p in enumerate(probs)]
    g_tot = [jnp.exp(g_last[x]) for x in range(n)]
    return u, wq, k_dec, a_intra, g_tot


def _gdn_state_steps(items, bd):
    c = GDN_CHUNK
    pairs = [slice(p * GDN_PAIR, (p + 1) * GDN_PAIR) for p in range(GDN_W // GDN_PAIR)]
    bdf = bd.astype(F32)
    ws_qs = [jnp.concatenate([jnp.dot(wq[:, ls], s_ref[p].astype(BF16), preferred_element_type=F32)
                              for p, ls in enumerate(pairs)], axis=1)
             for _, wq, _, _, _, s_ref in items]
    v_new = [it[0] - ws[:c] for it, ws in zip(items, ws_qs)]
    outs = [ws[c:] + _heads_mm(it[3], vn, bd) for it, ws, vn in zip(items, ws_qs, v_new)]
    for (_, _, k_dec, _, g_tot, s_ref), vn in zip(items, v_new):
        vb = vn.astype(BF16)
        for p, ls in enumerate(pairs):
            kv = lax.dot_general(k_dec[:, ls], vb[:, ls], (((0,), (0,)), ((), ())), preferred_element_type=F32)
            s_ref[p] = s_ref[p] * g_tot[:, ls] + kv * bdf
    return outs


def _gdn_scan_body(qf, kf, vf, bf, gf, qb, kb, vb, bb, gb, bd_ref, lvl_ref, dirm_ref, tri_ref,
                   of_ref, ob_ref, sf_ref, sb_ref):
    bd = bd_ref[...]
    n = SEQ_TILE // GDN_CHUNK
    probs, rows = [], []
    for c in range(n):
        rf = slice(c * GDN_CHUNK, (c + 1) * GDN_CHUNK)
        rb = slice((n - 1 - c) * GDN_CHUNK, (n - c) * GDN_CHUNK)
        probs.append(tuple(r[rf, :].astype(F32) for r in (qf, kf, vf, bf, gf)) + (False,))
        probs.append(tuple(r[rb, :].astype(F32) for r in (qb, kb, vb, bb, gb)) + (True,))
        rows += [rf, rb]
    u, wq, k_dec, a_intra, g_tot = _gdn_intra(probs, bd, lvl_ref, dirm_ref, tri_ref)
    for c in range(n):
        xs = (2 * c, 2 * c + 1)
        outs = _gdn_state_steps([(u[x], wq[x], k_dec[x], a_intra[x], g_tot[x], sb_ref if probs[x][5] else sf_ref)
                                 for x in xs], bd)
        for x, o in zip(xs, outs):
            (ob_ref if probs[x][5] else of_ref)[rows[x], :] = o


def _seq_tile_maps(n_lat_tiles, n_tiles):
    fwd = lambda s: lax.rem(s + n_lat_tiles, n_tiles)
    bwd = lambda s: n_tiles - 1 - s
    return fwd, bwd


def _gdn_scan_specs(q, k, v, beta, g, n_lat):
    b, s, _ = q.shape
    n_tiles = s // SEQ_TILE
    fwd, bwd = _seq_tile_maps(n_lat // SEQ_TILE, n_tiles)
    spec = lambda tile_of, col: pl.BlockSpec((None, SEQ_TILE, GDN_W), lambda bi, t: (bi, tile_of(t), col))
    masks = _gdn_masks()
    in_specs = [spec(fwd, 0)] * 5 + [spec(bwd, 0)] * 3 + [spec(bwd, 1)] * 2 + [_full(m) for m in masks]
    assert len(in_specs) == N_GDN_IN
    return (in_specs, [q, k, v, beta, g, q, k, v, beta, g, *masks], [spec(fwd, 0), spec(bwd, 0)],
            [jax.ShapeDtypeStruct((b, s, GDN_W), F32)] * 2,
            [pltpu.VMEM((GDN_W // GDN_PAIR, GDN_PAIR, GDN_PAIR), F32)] * 2)


def _lru_tile_prep(x_ref, xp_ref, xn_ref, cw_ref, cb_ref, wg_ref, bg_ref, nla_ref, a_ref, b_ref,
                   tile_idx, n_tiles, n_lat_tiles):
    xb = _tile_conv(x_ref[...], xp_ref[...], xn_ref[...], cw_ref, tile_idx, n_tiles, n_lat_tiles) + cb_ref[...]
    gates = _sigmoid(jnp.dot(xb.astype(BF16), wg_ref[...], preferred_element_type=F32) + bg_ref[...])
    log_a = nla_ref[...] * gates[:, :LRU_W]
    a_ref[...] = jnp.exp(log_a)
    th = jnp.tanh(log_a)
    b_ref[...] = jnp.sqrt(-2.0 * th / (1.0 - th)) * gates[:, LRU_W:] * xb


def _scan_group(a, b, h, row, reverse):
    for d in (1, 2, 4):
        if reverse:
            keep = row < 8 - d
            shift = 8 - d
        else:
            keep = row >= d
            shift = d
        a_s = jnp.where(keep, pltpu.roll(a, shift, 0), 1.0)
        b_s = jnp.where(keep, pltpu.roll(b, shift, 0), 0.0)
        b = a * b_s + b
        a = a * a_s
    return a * h + b


def _lru_scan_body(xf, xfp, xfn, xb, xbp, xbn, cw_ref, cb_ref, wgf, bgf, nlaf, wgb, bgb, nlab,
                   hf_ref, hb_ref, af_ref, bf_ref, ab_ref, bb_ref, cf_ref, cbk_ref, *, n_lat_tiles):
    s = pl.program_id(1)
    n_tiles = pl.num_programs(1)
    t_f, t_b = (m(s) for m in _seq_tile_maps(n_lat_tiles, n_tiles))
    _lru_tile_prep(xf, xfp, xfn, cw_ref, cb_ref, wgf, bgf, nlaf, af_ref, bf_ref, t_f, n_tiles, n_lat_tiles)
    _lru_tile_prep(xb, xbp, xbn, cw_ref, cb_ref, wgb, bgb, nlab, ab_ref, bb_ref, t_b, n_tiles, n_lat_tiles)

    n_groups = SEQ_TILE // 8
    row = lax.broadcasted_iota(jnp.int32, (8, LRU_W), 0)

    h_f, h_b = cf_ref[...], cbk_ref[...]
    for gi in range(n_groups):
        rf = slice(gi * 8, gi * 8 + 8)
        rb = slice((n_groups - 1 - gi) * 8, (n_groups - gi) * 8)
        out_f = _scan_group(af_ref[rf, :], bf_ref[rf, :], h_f, row, False)
        out_b = _scan_group(ab_ref[rb, :], bb_ref[rb, :], h_b, row, True)
        hf_ref[rf, :] = out_f
        hb_ref[rb, :] = out_b
        h_f = jnp.broadcast_to(out_f[7:8, :], (8, LRU_W))
        h_b = jnp.broadcast_to(out_b[0:1, :], (8, LRU_W))
    cf_ref[...] = h_f
    cbk_ref[...] = h_b


N_LRU_IN, N_GDN_IN, N_LRU_SCRATCH = 14, 14, 6


def _seq_mixers_kernel(*refs, n_lat_tiles):
    lru_in = refs[:N_LRU_IN]
    gdn_in = refs[N_LRU_IN:N_LRU_IN + N_GDN_IN]
    hf_ref, hb_ref, of_ref, ob_ref = refs[N_LRU_IN + N_GDN_IN:N_LRU_IN + N_GDN_IN + 4]
    scratch = refs[N_LRU_IN + N_GDN_IN + 4:]
    lru_scratch, gdn_scratch = scratch[:N_LRU_SCRATCH], scratch[N_LRU_SCRATCH:]

    @pl.when(pl.program_id(1) == 0)
    def _():
        for ref in lru_scratch[4:] + gdn_scratch:
            ref[...] = jnp.zeros(ref.shape, ref.dtype)

    _lru_scan_body(*lru_in, hf_ref, hb_ref, *lru_scratch, n_lat_tiles=n_lat_tiles)
    _gdn_scan_body(*gdn_in, of_ref, ob_ref, *gdn_scratch)


def _lru_scan_specs(p, conv_w, conv_b, w_r, b_r, w_i, b_i, lam, n_lat):
    b, s, _ = p.shape
    n_tiles = s // SEQ_TILE
    nlt = n_lat // SEQ_TILE
    fwd, bwd = _seq_tile_maps(nlt, n_tiles)
    col = P_LRU_X // LRU_W
    specs = []
    for tile_of in (fwd, bwd):
        main, prev, make_next = _halo_specs(LRU_W, col, tile_of)
        specs += [pl.BlockSpec((None, SEQ_TILE, LRU_W), main), pl.BlockSpec((None, HALO, LRU_W), prev),
                  pl.BlockSpec((None, HALO, LRU_W), make_next(n_tiles))]

    def blockdiag(w):
        return jax.scipy.linalg.block_diag(*[w[n] for n in range(LRU_BLOCKS)])

    dir_args = []
    for d in range(2):
        wg = jnp.concatenate([blockdiag(w_r[d]), blockdiag(w_i[d])], axis=1).astype(BF16)
        bg = jnp.concatenate([b_r[d], b_i[d]]).astype(F32).reshape(1, 2 * LRU_W)
        nla = (-LRU_C * jax.nn.softplus(-lam[d].astype(F32))).reshape(1, LRU_W)
        dir_args += [wg, bg, nla]
    cb2 = conv_b.reshape(1, LRU_W)
    out_spec = lambda tile_of: pl.BlockSpec((None, SEQ_TILE, LRU_W), lambda bi, t: (bi, tile_of(t), 0))
    in_specs = specs + [_full(conv_w), _full(cb2)] + [_full(a) for a in dir_args]
    assert len(in_specs) == N_LRU_IN
    return (in_specs, [p, p, p, p, p, p, conv_w, cb2, *dir_args], [out_spec(fwd), out_spec(bwd)],
            [jax.ShapeDtypeStruct((b, s, LRU_W), F32)] * 2,
            [pltpu.VMEM((SEQ_TILE, LRU_W), F32)] * 4 + [pltpu.VMEM((8, LRU_W), F32)] * 2)


def _seq_mixers(lru_parts, gdn_parts, bsz, n_tiles, n_lat):
    in_specs, args, out_specs, out_shape, scratch = ([*a, *b] for a, b in zip(lru_parts, gdn_parts))
    return pl.pallas_call(
        functools.partial(_seq_mixers_kernel, n_lat_tiles=n_lat // SEQ_TILE),
        grid=(bsz, n_tiles),
        in_specs=in_specs,
        out_specs=out_specs,
        out_shape=out_shape,
        scratch_shapes=scratch,
        compiler_params=_cparams(("parallel", "arbitrary")),
        name="seq_mixers",
    )(*args)


def _merge_kernel(x_ref, mc_ref, ml_ref, of_ref, ob_ref, z_ref, hf_ref, hb_ref, y_ref, uc_ref, ud_ref,
                  t0_ref, t1_ref, t2_ref, t3_ref, gn_ref, ones_ref, bg_ref, wb_ref, wo_ref, o_ref, *, n_lat):
    o = of_ref[...] + ob_ref[...]
    ms = jnp.dot(jnp.concatenate(_split(o * o), axis=-1), ones_ref[...], preferred_element_type=F32) * (1.0 / GDN_DV)
    z = z_ref[...]
    ua = (o * lax.rsqrt(ms + RMS_EPS) * gn_ref[...] * (z * _sigmoid(z))).astype(BF16)
    y = y_ref[...]
    gelu = 0.5 * y * (1.0 + jnp.tanh(0.7978845608028654 * (y + 0.044715 * (y * y * y))))
    ub = ((hf_ref[...] + hb_ref[...]) * gelu).astype(BF16)

    merged = None
    for n, (u, t_ref) in enumerate(((ua, t0_ref), (ub, t1_ref), (uc_ref[...], t2_ref), (ud_ref[...], t3_ref))):
        gate = _sigmoid(t_ref[...] + bg_ref[n:n + 1, :].astype(BF16))
        term = gate.astype(F32) * jnp.dot(u, wb_ref[n], preferred_element_type=F32)
        merged = term if merged is None else merged + term
    out = jnp.dot(merged.astype(BF16), wo_ref[...], preferred_element_type=F32)
    g1 = _row_select(x_ref.shape[0], pl.program_id(1), n_lat, mc_ref[2:3, :], ml_ref[2:3, :])
    o_ref[...] = x_ref[...] + g1 * out


def _merge(xs, mod_c, mod_l, gdn_out, lru_out, uc, ud, p32, p16, gdn_norm_g, b_gate, w_branch, w_out, layer, n_lat,
           with_ctx):
    b, s, d = xs.shape
    rows = s if with_ctx else n_lat
    tm = _pick_tile(rows, 544)
    gate_blk = P_GATES // d
    tok = lambda w: pl.BlockSpec((None, tm, w), lambda bi, i: (bi, i, 0))
    ones_n = jnp.concatenate([_head_block_ones(GDN_HEADS, GDN_DV)] * N_SPLIT, axis=0)
    gn = jnp.tile(gdn_norm_g.astype(F32), GDN_HEADS).reshape(1, GDN_W)
    pcol = lambda c: pl.BlockSpec((None, tm, BRANCH_W), lambda bi, i: (bi, i, c))
    in_specs = [tok(d), pl.BlockSpec((8, d), lambda bi, i: (0, 0)), pl.BlockSpec((None, 8, d), lambda bi, i: (bi, 0, 0))]
    in_specs += [tok(BRANCH_W), tok(BRANCH_W), pcol(P_GDN_Z // GDN_W), tok(BRANCH_W), tok(BRANCH_W),
                 pcol(P_LRU_Y // LRU_W), tok(BRANCH_W), tok(BRANCH_W)]
    in_specs += [pl.BlockSpec((None, tm, d), functools.partial(lambda bi, i, n: (bi, i, gate_blk + n), n=n))
                 for n in range(N_BRANCH)]
    in_specs += [_full(gn), _full(ones_n), _full(b_gate),
                 pl.BlockSpec((None,) + w_branch.shape[1:], lambda bi, i: (layer, 0, 0, 0)),
                 pl.BlockSpec((None,) + w_out.shape[1:], lambda bi, i: (layer, 0, 0))]
    return pl.pallas_call(
        functools.partial(_merge_kernel, n_lat=n_lat),
        grid=(b, rows // tm),
        in_specs=in_specs,
        out_specs=tok(d),
        out_shape=jax.ShapeDtypeStruct((b, rows, d), F32),
        input_output_aliases={0: 0} if with_ctx else {},
        compiler_params=_cparams(("parallel", "parallel")),
        name="merge",
    )(xs, mod_c, mod_l, gdn_out[0], gdn_out[1], p32, lru_out[0], lru_out[1], p32, uc, ud, p16, p16, p16, p16,
      gn, ones_n, b_gate, w_branch, w_out)


def _mlp_kernel(x_ref, mc_ref, ml_ref, gn_ref, gf_ref, w1_ref, w2_ref, o_ref, h_ref, acc_ref, *, n_lat, final_norm):
    f = pl.program_id(2)
    tm = x_ref.shape[0]
    i = pl.program_id(1)

    @pl.when(f == 0)
    def _():
        _norm_modulate(x_ref, h_ref, gn_ref, mc_ref, ml_ref, 3, i, n_lat)
        acc_ref[...] = jnp.zeros(acc_ref.shape, F32)

    a = jnp.maximum(jnp.dot(h_ref[...], w1_ref[...], preferred_element_type=F32), 0.0)
    acc_ref[...] += jnp.dot((a * a).astype(BF16), w2_ref[...], preferred_element_type=F32)

    @pl.when(f == pl.num_programs(2) - 1)
    def _():
        g2 = _row_select(tm, i, n_lat, mc_ref[5:6, :], ml_ref[5:6, :])
        y = x_ref[...] + g2 * acc_ref[...]
        if final_norm:
            y = y * lax.rsqrt(jnp.mean(y * y, axis=-1, keepdims=True) + RMS_EPS) * gf_ref[...]
        o_ref[...] = y


def _mlp(xs, mod_c, mod_l, gain, w1, w2, layer, final_gain, n_lat, final_norm):
    b, rows, d = xs.shape
    dff = w1.shape[-1]
    tm = _pick_tile(rows, 1088)
    tf = 1024
    row = pl.BlockSpec((1, d), lambda bi, i, f: (0, 0))
    return pl.pallas_call(
        functools.partial(_mlp_kernel, n_lat=n_lat, final_norm=final_norm),
        grid=(b, rows // tm, dff // tf),
        in_specs=[pl.BlockSpec((None, tm, d), lambda bi, i, f: (bi, i, 0)),
                  pl.BlockSpec((8, d), lambda bi, i, f: (0, 0)),
                  pl.BlockSpec((None, 8, d), lambda bi, i, f: (bi, 0, 0)),
                  row, row,
                  pl.BlockSpec((None, d, tf), lambda bi, i, f: (layer, 0, f)),
                  pl.BlockSpec((None, tf, d), lambda bi, i, f: (layer, f, 0))],
        out_specs=pl.BlockSpec((None, tm, d), lambda bi, i, f: (bi, i, 0)),
        out_shape=jax.ShapeDtypeStruct((b, rows, d), F32),
        scratch_shapes=[pltpu.VMEM((tm, d), BF16), pltpu.VMEM((tm, d), F32)],
        compiler_params=_cparams(("parallel", "parallel", "arbitrary")),
        name="mlp",
    )(xs, mod_c, mod_l, gain, final_gain, w1, w2)


def kernel(x, c, ctx, c_ctx, mod_w, mod_b, norm1_g, norm2_g, w_in, b_gate, gdn_conv_w, gdn_a_log, gdn_dt_bias,
           gdn_norm_g, lru_conv_w, lru_conv_b, lru_w_r, lru_b_r, lru_w_i, lru_b_i, lru_lambda, mla_q_norm_g,
           mla_w_uq, mla_kv_norm_g, mla_w_ukv, na_rpb, w_branch, w_out, mlp_w1, mlp_w2, final_norm_g):
    bsz, n_tok, d = x.shape
    n_ctx = ctx.shape[1]
    depth = w_in.shape[0]
    assert n_ctx % SEQ_TILE == 0 and n_tok % SEQ_TILE == 0 and n_tok % GRID_W == 0
    na_meta, na_rel_rows, na_cols = _na_geometry(n_tok // GRID_W)
    na_meta = jnp.asarray(na_meta)
    cos, sin = _rope_tables(n_tok, n_ctx)

    n_rows = -(-(bsz + 1) // 8) * 8
    cc = jnp.zeros((n_rows, d), F32).at[:bsz].set(c).at[bsz].set(c_ctx)
    final_gain = final_norm_g.reshape(1, d)

    w_in_all = _arrange_w_in(w_in)
    wb_all = w_branch.astype(BF16)
    wo_all = w_out.astype(BF16)
    w1_all = mlp_w1.astype(BF16)
    w2_all = mlp_w2.astype(BF16)

    xs = jnp.concatenate([x, ctx], axis=1)
    for l in range(depth):
        need_ctx = l < depth - 1
        mod = _modulation(cc, mod_w, mod_b[l], l).reshape(n_rows, N_MOD, d)
        pad = jnp.zeros((8 - N_MOD, d), F32)
        mod_c = jnp.concatenate([mod[bsz], pad], axis=0)
        mod_l = jnp.concatenate([mod[:bsz], jnp.broadcast_to(pad, (bsz, 8 - N_MOD, d))], axis=1)

        wq, wk, wv, place = _arrange_mla(mla_w_uq[l], mla_w_ukv[l])
        gq = mla_q_norm_g[l].reshape(1, -1)
        gkv = mla_kv_norm_g[l].reshape(1, -1)
        g1n = norm1_g[l].reshape(1, d)
        g2n = norm2_g[l].reshape(1, d)

        p32, p16 = _inproj(xs, mod_c, mod_l, g1n, w_in_all, l, n_tok)

        gq_, gk_, gv_, gbeta, gg = _gdn_prep(p32, gdn_conv_w[l], gdn_a_log[l], gdn_dt_bias[l], n_tok)
        h_f, h_b, o_f, o_b = _seq_mixers(
            _lru_scan_specs(p32, lru_conv_w[l], lru_conv_b[l], lru_w_r[l], lru_b_r[l], lru_w_i[l], lru_b_i[l],
                            lru_lambda[l], n_tok),
            _gdn_scan_specs(gq_, gk_, gv_, gbeta, gg, n_tok), bsz, (n_tok + n_ctx) // SEQ_TILE, n_tok)

        mq, mk, mv = _mla_prep(p32, p16, cos, sin, gq, gkv, wq, wk, wv, place)
        uc = _mla_flash(mq, mk, mv, n_tok)
        if need_ctx:
            uc = _mla_flash(mq, mk, mv, n_tok, prev=uc)
        ud = _na_attention(p16, na_meta, _na_bias_table(na_rpb[l], na_rel_rows, na_cols), n_tok, need_ctx)

        xs = _merge(xs, mod_c, mod_l, (o_f, o_b), (h_f, h_b), uc, ud, p32, p16, gdn_norm_g[l], b_gate[l],
                    wb_all, wo_all, l, n_tok, need_ctx)
        xs = _mlp(xs, mod_c, mod_l, g2n, w1_all, w2_all, l, final_gain, n_tok, l == depth - 1)
    return xs
```

```python
import functools

import jax
import jax.numpy as jnp
import numpy as np
from jax import lax
from jax.experimental import pallas as pl
from jax.experimental.pallas import tpu as pltpu

F32 = jnp.float32
BF16 = jnp.bfloat16

GRID_W = 64
N_MOD = 6
RMS_EPS = 1e-6
GDN_HEADS = 4
GDN_DK = 64
GDN_DV = 64
GDN_CHUNK = 64
GDN_W = GDN_HEADS * GDN_DV
LRU_W = 256
LRU_BLOCKS = 4
LRU_C = 8.0
MLA_HEADS = 4
MLA_Q_RANK = 256
MLA_KV_RANK = 128
MLA_NOPE = 64
MLA_ROPE = 32
MLA_V = 64
MLA_SLOT = 128
ROPE_BASE = 10000.0
LOG2_E = 1.4426950408889634
NA_HEADS = 4
NA_DH = 64
NA_W = NA_HEADS * NA_DH
NA_WIN_ROWS = 8
NA_WIN_COLS = 16
N_BRANCH = 4
BRANCH_W = 256

SEQ_TILE = 256
HALO = 8

_REF_COLS = {}
_off = 0
for _name, _w in (('gdn_qkv', 3 * GDN_W), ('gdn_z', GDN_W), ('gdn_beta', 2 * GDN_HEADS), ('gdn_a', 2 * GDN_HEADS),
                  ('lru_x', LRU_W), ('lru_y', LRU_W), ('mla_q', MLA_Q_RANK), ('mla_kv', MLA_KV_RANK),
                  ('mla_kr', MLA_ROPE), ('na_qkv', 3 * NA_W)):
    _REF_COLS[_name] = (_off, _w)
    _off += _w
N_MIX_COLS = _off

P_GDN_QKV = 0
P_GDN_Z = 768
P_LRU_X = 1024
P_LRU_Y = 1280
P_MLA_KV = 1536
P_MLA_KR = 1664
P_GDN_BA = 1792
P32_COLS = 2048
P_MLA_Q = 0
P_NA_QKV = 256
P_GATES = 1024
P16_COLS = P_GATES + N_BRANCH * 1024
PROJ_TILE = 1024

V7X_VMEM_BYTES = 64 * 1024 * 1024
VMEM_LIMIT = V7X_VMEM_BYTES - 12 * 1024 * 1024


def _cparams(sem):
    return pltpu.CompilerParams(dimension_semantics=sem, vmem_limit_bytes=VMEM_LIMIT)


def _pick_tile(n, cap):
    best = 8
    for t in range(8, min(n, cap) + 1, 8):
        if n % t == 0:
            best = t
    return best


def _full(a):
    return pl.BlockSpec(a.shape, lambda *_: (0,) * a.ndim)


N_SPLIT = 2


def _split(x):
    hi = x.astype(BF16)
    lo = (x - hi.astype(F32)).astype(BF16)
    return hi, lo


def _sigmoid(x):
    return 0.5 * jnp.tanh(0.5 * x) + 0.5


def _arrange_w_in(w_in):
    pieces, pos = [], 0

    def put(dst, block):
        nonlocal pos
        if dst > pos:
            pieces.append(jnp.zeros(w_in.shape[:-1] + (dst - pos,), w_in.dtype))
        pieces.append(block)
        pos = dst + block.shape[-1]

    ref = lambda name: w_in[..., _REF_COLS[name][0]:_REF_COLS[name][0] + _REF_COLS[name][1]]
    for name, dst in (('gdn_qkv', P_GDN_QKV), ('gdn_z', P_GDN_Z), ('lru_x', P_LRU_X), ('lru_y', P_LRU_Y),
                      ('mla_kv', P_MLA_KV), ('mla_kr', P_MLA_KR), ('gdn_beta', P_GDN_BA),
                      ('gdn_a', P_GDN_BA + 2 * GDN_HEADS)):
        put(dst, ref(name))
    put(P32_COLS + P_MLA_Q, ref('mla_q'))
    put(P32_COLS + P_NA_QKV, ref('na_qkv'))
    put(P32_COLS + P_GATES, w_in[..., N_MIX_COLS:])
    assert pos == P32_COLS + P16_COLS
    return jnp.concatenate(pieces, axis=-1).astype(BF16)


def _rope_perm():
    q = MLA_ROPE // 4
    src = np.zeros(MLA_ROPE, np.int32)
    sign = np.zeros(MLA_ROPE, np.float32)
    for base in (0, 2 * q):
        for d in range(q):
            src[base + d] = base + d + q
            sign[base + d] = -1.0
            src[base + q + d] = base + d
            sign[base + q + d] = 1.0
    return src, sign


def _arrange_mla(w_uq, w_ukv):
    src, sign = _rope_perm()
    hq = MLA_NOPE + MLA_ROPE
    wq = jnp.zeros((MLA_Q_RANK, 2 * MLA_HEADS * MLA_SLOT), F32)
    wk = jnp.zeros((MLA_KV_RANK, MLA_HEADS * MLA_SLOT), F32)
    wv = jnp.zeros((MLA_KV_RANK, MLA_HEADS * MLA_V), F32)
    place = np.zeros((2 * MLA_SLOT, 2 * MLA_HEADS * MLA_SLOT), np.float32)
    rot_off = MLA_HEADS * MLA_SLOT
    for h in range(MLA_HEADS):
        nope = w_uq[:, h * hq:h * hq + MLA_NOPE]
        pe = w_uq[:, h * hq + MLA_NOPE:(h + 1) * hq]
        s = h * MLA_SLOT
        wq = wq.at[:, s:s + MLA_NOPE].set(nope)
        wq = wq.at[:, s + MLA_NOPE:s + MLA_NOPE + MLA_ROPE].set(pe)
        wq = wq.at[:, rot_off + s + MLA_NOPE:rot_off + s + MLA_NOPE + MLA_ROPE].set(pe[:, src] * sign)
        wk = wk.at[:, s:s + MLA_NOPE].set(w_ukv[:, h * 128:h * 128 + MLA_NOPE])
        wv = wv.at[:, h * MLA_V:(h + 1) * MLA_V].set(w_ukv[:, h * 128 + MLA_NOPE:(h + 1) * 128])
        for d in range(MLA_ROPE):
            for half in (0, MLA_SLOT):
                place[half + d, s + MLA_NOPE + d] = 1.0
                place[half + src[d], rot_off + s + MLA_NOPE + d] = sign[d]
    return wq.astype(BF16), wk.astype(BF16), wv.astype(BF16), jnp.asarray(place, BF16)


def _rope_tables(n_tok, n_ctx):
    cos = np.ones((n_tok + n_ctx, MLA_SLOT), np.float32)
    sin = np.zeros((n_tok + n_ctx, MLA_SLOT), np.float32)
    t = np.arange(n_tok)
    row = (t // GRID_W).astype(np.float32)
    col = (t % GRID_W).astype(np.float32)
    n_freq = MLA_ROPE // 4
    inv = (ROPE_BASE ** (-np.arange(n_freq, dtype=np.float32) / n_freq)).astype(np.float32)
    ar = row[:, None] * inv
    ac = col[:, None] * inv
    ang = np.concatenate([ar, ar, ac, ac], axis=-1).astype(np.float32)
    cos[:n_tok, MLA_NOPE:MLA_NOPE + MLA_ROPE] = np.cos(ang)
    sin[:n_tok, MLA_NOPE:MLA_NOPE + MLA_ROPE] = np.sin(ang)
    return jnp.asarray(cos), jnp.asarray(sin)


def _head_block_ones(n_heads, width):
    m = np.kron(np.eye(n_heads, dtype=np.float32), np.ones((width, width), np.float32))
    return jnp.asarray(m, BF16)


def _mod_kernel(c_ref, w_ref, b_ref, o_ref):
    c = c_ref[...]
    s = c * _sigmoid(c)
    o_ref[...] = jnp.dot(s, w_ref[...], preferred_element_type=F32) + b_ref[...]


def _modulation(cc, mod_w, mod_b, layer):
    r, d = cc.shape
    n = mod_w.shape[-1]
    tn = 1024
    return pl.pallas_call(
        _mod_kernel,
        grid=(n // tn,),
        in_specs=[pl.BlockSpec((r, d), lambda j: (0, 0)),
                  pl.BlockSpec((None, d, tn), lambda j: (layer, 0, j)),
                  pl.BlockSpec((1, tn), lambda j: (0, j))],
        out_specs=pl.BlockSpec((r, tn), lambda j: (0, j)),
        out_shape=jax.ShapeDtypeStruct((r, n), F32),
        compiler_params=_cparams(("arbitrary",)),
        name="modulation",
    )(cc, mod_w, mod_b.reshape(1, n))


def _row_select(tile_rows, tile_idx, n_lat, ctx_vec, lat_vec):
    row = tile_idx * tile_rows + lax.broadcasted_iota(jnp.int32, (tile_rows, 1), 0)
    return jnp.where(row < n_lat, lat_vec, ctx_vec)


def _norm_modulate(x_ref, h_ref, gain_ref, mc_ref, ml_ref, shift_row, tile_idx, n_lat):
    tm = x_ref.shape[0]
    x = x_ref[...]
    xn = x * lax.rsqrt(jnp.mean(x * x, axis=-1, keepdims=True) + RMS_EPS)
    sh_l, sh_c = ml_ref[shift_row:shift_row + 1, :], mc_ref[shift_row:shift_row + 1, :]
    amp_l = gain_ref[...] * (1.0 + ml_ref[shift_row + 1:shift_row + 2, :])
    amp_c = gain_ref[...] * (1.0 + mc_ref[shift_row + 1:shift_row + 2, :])
    all_latent = (tile_idx + 1) * tm <= n_lat

    @pl.when(all_latent)
    def _():
        h_ref[...] = (xn * amp_l + sh_l).astype(h_ref.dtype)

    @pl.when(jnp.logical_not(all_latent))
    def _():
        amp = _row_select(tm, tile_idx, n_lat, amp_c, amp_l)
        shift = _row_select(tm, tile_idx, n_lat, sh_c, sh_l)
        h_ref[...] = (xn * amp + shift).astype(h_ref.dtype)


def _inproj_first_kernel(x_ref, mc_ref, ml_ref, g_ref, w_ref, o_ref, h_ref, *, n_lat):
    @pl.when(pl.program_id(2) == 0)
    def _():
        _norm_modulate(x_ref, h_ref, g_ref, mc_ref, ml_ref, 0, pl.program_id(1), n_lat)

    o_ref[...] = jnp.dot(h_ref[...], w_ref[...], preferred_element_type=F32).astype(o_ref.dtype)


def _inproj_second_kernel(h_ref, w_ref, o_ref):
    o_ref[...] = jnp.dot(h_ref[...], w_ref[...], preferred_element_type=F32).astype(o_ref.dtype)


def _inproj(xs, mod_c, mod_l, gain, w, layer, n_lat):
    b, s, d = xs.shape
    tm = _pick_tile(s, 1088)
    tn = PROJ_TILE
    n32 = P32_COLS // tn
    p16, h = pl.pallas_call(
        functools.partial(_inproj_first_kernel, n_lat=n_lat),
        grid=(b, s // tm, P16_COLS // tn),
        in_specs=[pl.BlockSpec((None, tm, d), lambda bi, i, j: (bi, i, 0)),
                  pl.BlockSpec((8, d), lambda bi, i, j: (0, 0)),
                  pl.BlockSpec((None, 8, d), lambda bi, i, j: (bi, 0, 0)),
                  pl.BlockSpec((1, d), lambda bi, i, j: (0, 0)),
                  pl.BlockSpec((None, d, tn), lambda bi, i, j: (layer, 0, j + n32))],
        out_specs=[pl.BlockSpec((None, tm, tn), lambda bi, i, j: (bi, i, j)),
                   pl.BlockSpec((None, tm, d), lambda bi, i, j: (bi, i, 0))],
        out_shape=[jax.ShapeDtypeStruct((b, s, P16_COLS), BF16), jax.ShapeDtypeStruct((b, s, d), BF16)],
        compiler_params=_cparams(("parallel", "parallel", "arbitrary")),
        name="inproj_bf16",
    )(xs, mod_c, mod_l, gain, w)
    p32 = pl.pallas_call(
        _inproj_second_kernel,
        grid=(b, s // tm, n32),
        in_specs=[pl.BlockSpec((None, tm, d), lambda bi, i, j: (bi, i, 0)),
                  pl.BlockSpec((None, d, tn), lambda bi, i, j: (layer, 0, j))],
        out_specs=pl.BlockSpec((None, tm, tn), lambda bi, i, j: (bi, i, j)),
        out_shape=jax.ShapeDtypeStruct((b, s, P32_COLS), F32),
        compiler_params=_cparams(("parallel", "parallel", "arbitrary")),
        name="inproj_f32",
    )(h, w)
    return p32, p16


def _mla_prep_kernel(ql_ref, kv_ref, cos_ref, sin_ref, gq_ref, gkv_ref, wq_ref, wk_ref, wv_ref, pl_ref,
                     q_ref, k_ref, v_ref, *, scale):
    nslot = MLA_HEADS * MLA_SLOT
    cos = jnp.concatenate([cos_ref[...]] * MLA_HEADS, axis=-1)
    sin = jnp.concatenate([sin_ref[...]] * MLA_HEADS, axis=-1)

    ql = ql_ref[...].astype(F32)
    qn = ql * lax.rsqrt(jnp.mean(ql * ql, axis=-1, keepdims=True) + RMS_EPS) * gq_ref[...]
    q2 = jnp.dot(qn.astype(BF16), wq_ref[...], preferred_element_type=F32)
    q = (q2[:, :nslot] * cos + q2[:, nslot:] * sin) * scale
    q_ref[...] = q.astype(BF16)

    kvkr = kv_ref[...]
    kvl = kvkr[:, :MLA_KV_RANK]
    kr = kvkr[:, MLA_KV_RANK:]
    kvn = (kvl * lax.rsqrt(jnp.mean(kvl * kvl, axis=-1, keepdims=True) + RMS_EPS) * gkv_ref[...]).astype(BF16)
    kn = jnp.dot(kvn, wk_ref[...], preferred_element_type=F32)
    v_ref[...] = jnp.dot(kvn, wv_ref[...], preferred_element_type=F32).astype(BF16)
    kr_hi = kr.astype(BF16)
    kr_lo = (kr - kr_hi.astype(F32)).astype(BF16)
    kr2 = jnp.dot(jnp.concatenate([kr_hi, kr_lo], axis=-1), pl_ref[...], preferred_element_type=F32)
    k_ref[...] = (kn + kr2[:, :nslot] * cos + kr2[:, nslot:] * sin).astype(BF16)


def _mla_prep(p32, p16, cos, sin, gq, gkv, wq, wk, wv, place):
    b, s, _ = p32.shape
    tm = _pick_tile(s, 544)
    nslot = MLA_HEADS * MLA_SLOT
    scale = (MLA_NOPE + MLA_ROPE) ** -0.5 * LOG2_E
    return pl.pallas_call(
        functools.partial(_mla_prep_kernel, scale=scale),
        grid=(b, s // tm),
        in_specs=[pl.BlockSpec((None, tm, MLA_Q_RANK), lambda bi, i: (bi, i, P_MLA_Q // MLA_Q_RANK)),
                  pl.BlockSpec((None, tm, 2 * MLA_SLOT), lambda bi, i: (bi, i, P_MLA_KV // (2 * MLA_SLOT))),
                  pl.BlockSpec((tm, MLA_SLOT), lambda bi, i: (i, 0)),
                  pl.BlockSpec((tm, MLA_SLOT), lambda bi, i: (i, 0)),
                  _full(gq), _full(gkv), _full(wq), _full(wk), _full(wv), _full(place)],
        out_specs=[pl.BlockSpec((None, tm, nslot), lambda bi, i: (bi, i, 0)),
                   pl.BlockSpec((None, tm, nslot), lambda bi, i: (bi, i, 0)),
                   pl.BlockSpec((None, tm, MLA_HEADS * MLA_V), lambda bi, i: (bi, i, 0))],
        out_shape=[jax.ShapeDtypeStruct((b, s, nslot), BF16),
                   jax.ShapeDtypeStruct((b, s, nslot), BF16),
                   jax.ShapeDtypeStruct((b, s, MLA_HEADS * MLA_V), BF16)],
        compiler_params=_cparams(("parallel", "parallel")),
        name="mla_prep",
    )(p16, p32, cos, sin, gq, gkv, wq, wk, wv, place)


def _flash_softmax(h, s, m_ref, l_ref):
    m_prev = m_ref[h]
    m_new = jnp.maximum(m_prev, jnp.max(s, axis=-1, keepdims=True))
    alpha = jnp.exp2(m_prev - m_new)
    p = jnp.exp2(s - jnp.concatenate([m_new] * (s.shape[1] // 128), axis=-1))
    l_ref[h] = alpha * l_ref[h] + jnp.sum(p, axis=-1, keepdims=True)
    m_ref[h] = m_new
    return alpha, p.astype(BF16)


def _mla_flash_kernel(*refs, tk, aliased):
    if aliased:
        q_ref, k_ref, v_ref, _, o_ref, m_ref, l_ref, acc_ref = refs
    else:
        q_ref, k_ref, v_ref, o_ref, m_ref, l_ref, acc_ref = refs
    tq = q_ref.shape[0]
    nk = k_ref.shape[0]
    n_loop = nk // tk
    m_ref[...] = jnp.full(m_ref.shape, -jnp.inf, F32)
    l_ref[...] = jnp.zeros(l_ref.shape, F32)
    acc_ref[...] = jnp.zeros(acc_ref.shape, F32)

    def chunk(rows):
        heads = range(MLA_HEADS)
        hs = [slice(h * MLA_SLOT, (h + 1) * MLA_SLOT) for h in heads]
        vs = [slice((h // 2) * 128, (h // 2) * 128 + 128) for h in heads]
        s = [lax.dot_general(q_ref[:, hs[h]], k_ref[rows, hs[h]], (((1,), (1,)), ((), ())),
                             preferred_element_type=F32) for h in heads]
        ap = [_flash_softmax(h, s[h], m_ref, l_ref) for h in heads]
        for h in heads:
            alpha, p = ap[h]
            acc_ref[h] = alpha * acc_ref[h] + jnp.dot(p, v_ref[rows, vs[h]], preferred_element_type=F32)

    if n_loop:
        def body(j, carry):
            chunk(pl.ds(pl.multiple_of(j * tk, tk), tk))
            return carry
        lax.fori_loop(0, n_loop, body, 0)
    if nk > n_loop * tk:
        chunk(slice(n_loop * tk, nk))

    lane = lax.broadcasted_iota(jnp.int32, (tq, 128), 1)
    outs = []
    for pair in range(MLA_HEADS // 2):
        o0 = acc_ref[2 * pair] / l_ref[2 * pair]
        o1 = acc_ref[2 * pair + 1] / l_ref[2 * pair + 1]
        outs.append(jnp.where(lane < MLA_V, o0, o1))
    o_ref[...] = jnp.concatenate(outs, axis=-1).astype(o_ref.dtype)


def _mla_flash(q, k, v, n_lat, prev=None):
    b, s, nslot = q.shape
    n_ctx = s - n_lat
    nv = MLA_HEADS * MLA_V
    tk = 512
    if prev is None:
        tq = _pick_tile(n_lat, 1024)
        q_off, n_q, kv_rows, kv_blk = 0, n_lat // tq, s, 0
    else:
        assert n_lat % n_ctx == 0
        tq = _pick_tile(n_ctx, 256)
        q_off, n_q, kv_rows, kv_blk = n_lat // tq, n_ctx // tq, n_ctx, n_lat // n_ctx
    in_specs = [pl.BlockSpec((None, tq, nslot), lambda bi, i: (bi, i + q_off, 0)),
                pl.BlockSpec((None, kv_rows, nslot), lambda bi, i: (bi, kv_blk, 0)),
                pl.BlockSpec((None, kv_rows, nv), lambda bi, i: (bi, kv_blk, 0))]
    args = [q, k, v]
    aliases = {}
    if prev is not None:
        in_specs.append(pl.BlockSpec(memory_space=pl.ANY))
        args.append(prev)
        aliases = {3: 0}
    return pl.pallas_call(
        functools.partial(_mla_flash_kernel, tk=tk, aliased=prev is not None),
        grid=(b, n_q),
        in_specs=in_specs,
        out_specs=pl.BlockSpec((None, tq, nv), lambda bi, i: (bi, i + q_off, 0)),
        out_shape=jax.ShapeDtypeStruct((b, s, nv), BF16),
        scratch_shapes=[pltpu.VMEM((MLA_HEADS, tq, 128), F32),
                        pltpu.VMEM((MLA_HEADS, tq, 128), F32),
                        pltpu.VMEM((MLA_HEADS, tq, 128), F32)],
        input_output_aliases=aliases,
        compiler_params=_cparams(("parallel", "arbitrary")),
        name="mla_flash",
    )(*args)


NA_QROWS = 4
NA_SLAB = NA_QROWS + NA_WIN_ROWS


def _na_geometry(rows):
    assert rows % NA_QROWS == 0 and rows >= NA_SLAB
    nblk = rows // NA_QROWS
    qc = np.arange(GRID_W)
    cs = np.clip(qc - NA_WIN_COLS // 2, 0, GRID_W - NA_WIN_COLS)
    col_valid = (qc[None, :] >= cs[:, None]) & (qc[None, :] < cs[:, None] + NA_WIN_COLS)
    rel_c = np.clip(qc[None, :] - qc[:, None] + NA_WIN_COLS - 1, 0, 2 * NA_WIN_COLS - 2)
    onehot_c = (rel_c[None] == np.arange(2 * NA_WIN_COLS - 1)[:, None, None]) & col_valid[None]
    patterns, cls, starts = {}, [], []
    for i in range(nblk):
        r0 = i * NA_QROWS
        start = int(np.clip(r0 - NA_WIN_ROWS // 2, 0, rows - NA_SLAB))
        qr = r0 + np.arange(NA_QROWS)
        rs = np.clip(qr - NA_WIN_ROWS // 2, 0, rows - NA_WIN_ROWS)
        key = (start - r0,) + tuple((rs - r0).tolist())
        if key not in patterns:
            kr = start + np.arange(NA_SLAB)
            row_valid = (kr[None, :] >= rs[:, None]) & (kr[None, :] < rs[:, None] + NA_WIN_ROWS)
            rel_r = np.clip(kr[None, :] - qr[:, None] + NA_WIN_ROWS - 1, 0, 2 * NA_WIN_ROWS - 2)
            patterns[key] = (len(patterns), np.where(row_valid, rel_r, -1))
        cls.append(patterns[key][0])
        starts.append(start)
    ordered = sorted(patterns.values(), key=lambda z: z[0])
    rel_rows = np.stack([z[1] for z in ordered])
    meta = np.stack([np.asarray(cls, np.int32), np.asarray(starts, np.int32)])
    return meta, rel_rows, (onehot_c.astype(np.float32), col_valid)


def _na_bias_table(rpb, rel_rows, col_tables):
    onehot_c, col_valid = col_tables
    bc = jnp.einsum('hrc,cqk->hrqk', rpb.astype(F32), onehot_c, precision=lax.Precision.HIGHEST)
    bc = jnp.where(col_valid, bc, -jnp.inf)
    outside = jnp.full(bc.shape[:1] + bc.shape[2:], -jnp.inf, F32)
    pats = []
    for pat in rel_rows:
        qrows = [jnp.concatenate([bc[:, r] if r >= 0 else outside for r in row], axis=-1) for row in pat]
        pats.append(jnp.concatenate(qrows, axis=1))
    return jnp.stack(pats)


def _na_scores(q_pair, lane, h, k_parts):
    in_head = (lane < NA_DH) if h % 2 == 0 else (lane >= NA_DH)
    qm = jnp.where(in_head, q_pair * (NA_DH ** -0.5), 0.0).astype(BF16)
    return [lax.dot_general(qm, k, (((1,), (1,)), ((), ())), preferred_element_type=F32) for k in k_parts]


def _na_softmax(scores, bias):
    if bias is not None:
        scores = [scores[0] + bias] + scores[1:]
    m = scores[0].max(axis=-1, keepdims=True)
    for s in scores[1:]:
        m = jnp.maximum(m, s.max(axis=-1, keepdims=True))
    probs = [jnp.exp(s - m) for s in scores]
    den = probs[0].sum(axis=-1, keepdims=True)
    for p in probs[1:]:
        den = den + p.sum(axis=-1, keepdims=True)
    return [p.astype(BF16) for p in probs], den


def _na_kernel(meta_ref, q_ref, k_ref, v_ref, bias_ref, o_ref, *, n_lat, with_ctx):
    i = pl.program_id(1)
    nq = q_ref.shape[0]
    n_lat_tiles = n_lat // nq
    n_all = k_ref.shape[0]
    lane = lax.broadcasted_iota(jnp.int32, (nq, 128), 1)

    def run(windowed):
        key_rows = [pl.ds(n_lat, n_all - n_lat)]
        if windowed:
            start = pl.multiple_of(meta_ref[1, jnp.minimum(i, n_lat_tiles - 1)] * GRID_W, GRID_W)
            key_rows = [pl.ds(start, NA_SLAB * GRID_W)] + key_rows
        pair_lanes = [slice((h // 2) * 128, (h // 2 + 1) * 128) for h in range(NA_HEADS)]
        scores = [_na_scores(q_ref[:, pair_lanes[h]], lane, h, [k_ref[r, pair_lanes[h]] for r in key_rows])
                  for h in range(NA_HEADS)]
        soft = [_na_softmax(scores[h], bias_ref[h] if windowed else None) for h in range(NA_HEADS)]
        heads = []
        for h in range(NA_HEADS):
            probs, den = soft[h]
            out = None
            for p, r in zip(probs, key_rows):
                po = jnp.dot(p, v_ref[r, pair_lanes[h]], preferred_element_type=F32)
                out = po if out is None else out + po
            heads.append(out / den)
        outs = [jnp.where(lane < NA_DH, heads[2 * pair], heads[2 * pair + 1]) for pair in range(NA_HEADS // 2)]
        o_ref[...] = jnp.concatenate(outs, axis=-1).astype(o_ref.dtype)

    if with_ctx:
        pl.when(i < n_lat_tiles)(lambda: run(True))
        pl.when(i >= n_lat_tiles)(lambda: run(False))
    else:
        run(True)


def _na_attention(p, meta, table, n_lat, with_ctx_queries):
    b, s, _ = p.shape
    nq = NA_QROWS * GRID_W
    nk = NA_SLAB * GRID_W
    assert (s - n_lat) % nq == 0
    nlt = n_lat // nq
    qb = P_NA_QKV // NA_W
    grid_spec = pltpu.PrefetchScalarGridSpec(
        num_scalar_prefetch=1,
        grid=(b, s // nq if with_ctx_queries else nlt),
        in_specs=[pl.BlockSpec((None, nq, NA_W), lambda bi, i, m: (bi, i, qb)),
                  pl.BlockSpec((None, s, NA_W), lambda bi, i, m: (bi, 0, qb + 1)),
                  pl.BlockSpec((None, s, NA_W), lambda bi, i, m: (bi, 0, qb + 2)),
                  pl.BlockSpec((None, NA_HEADS, nq, nk),
                               lambda bi, i, m: (m[0, jnp.minimum(i, nlt - 1)], 0, 0, 0))],
        out_specs=pl.BlockSpec((None, nq, NA_W), lambda bi, i, m: (bi, i, 0)))
    return pl.pallas_call(
        functools.partial(_na_kernel, n_lat=n_lat, with_ctx=with_ctx_queries),
        grid_spec=grid_spec,
        out_shape=jax.ShapeDtypeStruct((b, s, NA_W), BF16),
        compiler_params=_cparams(("parallel", "arbitrary")),
        name="na_attention",
    )(meta, p, p, p, table)


def _tile_conv(x, prev, nxt, w_ref, tile_idx, n_tiles, n_lat_tiles):
    r = x.shape[0]
    width = w_ref.shape[0]
    left = width // 2
    has_prev = jnp.logical_and(tile_idx != 0, tile_idx != n_lat_tiles)
    has_next = jnp.logical_and(tile_idx != n_tiles - 1, tile_idx != n_lat_tiles - 1)
    prev = jnp.where(has_prev, prev, 0.0)
    nxt = jnp.where(has_next, nxt, 0.0)
    xe = jnp.concatenate([prev, x, nxt], axis=0)
    acc = None
    for j in range(width):
        o = HALO - left + j
        term = xe[o:o + r, :] * w_ref[j:j + 1, :]
        acc = term if acc is None else acc + term
    return acc


def _halo_specs(width, col_block, tile_of):
    per = SEQ_TILE // HALO

    def main(bi, s, *_):
        return (bi, tile_of(s), col_block)

    def prev(bi, s, *_):
        return (bi, jnp.maximum(tile_of(s) * per - 1, 0), col_block)

    def make_next(n_tiles):
        def nxt(bi, s, *_):
            return (bi, jnp.minimum((tile_of(s) + 1) * per, n_tiles * per - 1), col_block)
        return nxt

    return main, prev, make_next


def _gdn_prep_kernel(x_ref, xp_ref, xn_ref, ba_ref, cw_ref, ones_ref, exp_ref, alog_ref, dtb_ref,
                     q_ref, k_ref, v_ref, beta_ref, g_ref, *, n_lat_tiles):
    i = pl.program_id(1)
    y = _tile_conv(x_ref[...], xp_ref[...], xn_ref[...], cw_ref, i, pl.num_programs(1), n_lat_tiles)
    y = y * _sigmoid(y)
    q = y[:, :GDN_W]
    k = y[:, GDN_W:2 * GDN_W]
    v_ref[...] = y[:, 2 * GDN_W:].astype(v_ref.dtype)

    def head_norm(u):
        parts = jnp.concatenate(_split(u * u), axis=-1)
        ss = jnp.dot(parts, ones_ref[...], preferred_element_type=F32)
        return u * lax.rsqrt(ss + RMS_EPS)

    q_ref[...] = (head_norm(q) * (GDN_DK ** -0.5)).astype(q_ref.dtype)
    k_ref[...] = head_norm(k).astype(k_ref.dtype)

    ba = ba_ref[...]
    a = ba + dtb_ref[...]
    softplus = jnp.maximum(a, 0.0) + jnp.log1p(jnp.exp(-jnp.abs(a)))
    lane = lax.broadcasted_iota(jnp.int32, ba.shape, 1)
    compact = jnp.where(lane < 2 * GDN_HEADS, _sigmoid(ba), -jnp.exp(alog_ref[...]) * softplus)
    wide = jnp.dot(jnp.concatenate(_split(compact), axis=-1), exp_ref[...], preferred_element_type=F32)
    half = 2 * GDN_W
    beta_ref[...] = wide[:, :half].astype(beta_ref.dtype)
    g_ref[...] = wide[:, half:]


def _gdn_prep(p, conv_w, a_log, dt_bias, n_lat):
    b, s, _ = p.shape
    n_tiles = s // SEQ_TILE
    main, prev, make_next = _halo_specs(3 * GDN_W, 0, lambda t: t)
    ones_n = jnp.concatenate([_head_block_ones(GDN_HEADS, GDN_DK)] * N_SPLIT, axis=0)
    expand = np.zeros((128, 4 * GDN_W), np.float32)
    for kind in range(2):
        for d in range(2):
            for h in range(GDN_HEADS):
                c0 = kind * 2 * GDN_W + d * GDN_W + h * GDN_DV
                expand[kind * 2 * GDN_HEADS + d * GDN_HEADS + h, c0:c0 + GDN_DV] = 1.0
    expand_n = jnp.asarray(np.concatenate([expand] * N_SPLIT, axis=0), BF16)
    lanes = jnp.zeros((1, 128), F32)
    alog_e = lanes.at[0, 2 * GDN_HEADS:4 * GDN_HEADS].set(a_log.astype(F32).reshape(-1))
    dtb_e = lanes.at[0, 2 * GDN_HEADS:4 * GDN_HEADS].set(dt_bias.astype(F32).reshape(-1))
    tok = lambda w: pl.BlockSpec((None, SEQ_TILE, w), lambda bi, i: (bi, i, 0))
    return pl.pallas_call(
        functools.partial(_gdn_prep_kernel, n_lat_tiles=n_lat // SEQ_TILE),
        grid=(b, n_tiles),
        in_specs=[pl.BlockSpec((None, SEQ_TILE, 3 * GDN_W), main),
                  pl.BlockSpec((None, HALO, 3 * GDN_W), prev),
                  pl.BlockSpec((None, HALO, 3 * GDN_W), make_next(n_tiles)),
                  pl.BlockSpec((None, SEQ_TILE, 128), lambda bi, i: (bi, i, P_GDN_BA // 128)),
                  _full(conv_w), _full(ones_n), _full(expand_n), _full(alog_e), _full(dtb_e)],
        out_specs=[tok(GDN_W), tok(GDN_W), tok(GDN_W), tok(2 * GDN_W), tok(2 * GDN_W)],
        out_shape=[jax.ShapeDtypeStruct((b, s, GDN_W), BF16)] * 3 + [jax.ShapeDtypeStruct((b, s, 2 * GDN_W), BF16),
                                                                    jax.ShapeDtypeStruct((b, s, 2 * GDN_W), F32)],
        compiler_params=_cparams(("parallel", "parallel")),
        name="gdn_prep",
    )(p, p, p, p, conv_w, ones_n, expand_n, alog_e, dtb_e)


GDN_PAIR = 2 * GDN_DK


def _gdn_masks():
    c, w = GDN_CHUNK, GDN_W
    r2, c2 = np.arange(GDN_PAIR)[:, None], np.arange(GDN_PAIR)[None, :]
    bd = ((r2 // c) == (c2 // c)).astype(np.float32)
    i = np.arange(c)[:, None]
    j = (np.arange(w) % c)[None, :]
    level = np.zeros((c, w), np.int32)
    for bit in range(6):
        level += ((i ^ j) >= (1 << bit)).astype(np.int32)
    lvl = np.stack([(level == m).astype(np.float32) for m in range(7)])
    dirm = np.stack([np.stack([(j <= i), (j < i)]), np.stack([(j >= i), (j > i)])]).astype(np.float32)
    tj = (np.arange(N_SPLIT * c) % c)[None, :]
    tri = np.stack([(tj <= i), (tj >= i)]).astype(np.float32)
    return jnp.asarray(bd, BF16), jnp.asarray(lvl), jnp.asarray(dirm), jnp.asarray(tri, BF16)


def _heads_mm(x, y, bd, transpose_rhs=False):
    xb = x.astype(BF16)
    yb = y.astype(BF16)
    outs = []
    for pair in range(GDN_W // GDN_PAIR):
        ls = slice(pair * GDN_PAIR, (pair + 1) * GDN_PAIR)
        w = jnp.concatenate([yb[:, ls], yb[:, ls]], axis=0) * bd
        dims = (((1,), (1,)), ((), ())) if transpose_rhs else (((1,), (0,)), ((), ()))
        outs.append(lax.dot_general(xb[:, ls], w, dims, preferred_element_type=F32))
    return jnp.concatenate(outs, axis=1)


def _gdn_intra(probs, bd, lvl_ref, dirm_ref, tri_ref):
    c = GDN_CHUNK
    n = len(probs)
    eye = lvl_ref[0]
    gc, g_last, decay, gram = [], [], [], []
    for q, k, v, beta, g, rev in probs:
        d = 1 if rev else 0
        gcp = jnp.dot(tri_ref[d], jnp.concatenate(_split(g), axis=0), preferred_element_type=F32)
        gc.append(gcp)
        g_last.append(gcp[0:1, :] if rev else gcp[c - 1:c, :])
        gc_row = jnp.sum(gcp * eye, axis=0, keepdims=True)
        decay.append(dirm_ref[d, 0] * jnp.exp(jnp.minimum(gcp - gc_row, 0.0)))
        gram.append(_heads_mm(jnp.concatenate([k, q], axis=0), k, bd, transpose_rhs=True))
    lower = [dirm_ref[1 if p[5] else 0, 1] * p[3] * gram[x][:c] * decay[x] for x, p in enumerate(probs)]
    a_intra = [gram[x][c:] * decay[x] for x in range(n)]
    t = [eye - lower[x] * lvl_ref[1] for x in range(n)]
    for lev in range(2, 7):
        y = [_heads_mm(t[x], lower[x] * lvl_ref[lev], bd) for x in range(n)]
        z = [_heads_mm(y[x], t[x], bd) for x in range(n)]
        t = [t[x] - z[x] for x in range(n)]
    e_gc = [jnp.exp(gc[x]) for x in range(n)]
    u = [_heads_mm(t[x], p[2] * p[3], bd) for x, p in enumerate(probs)]
    w = [_heads_mm(t[x], p[1] * p[3] * e_gc[x], bd) for x, p in enumerate(probs)]
    wq = [jnp.concatenate([w[x], p[0] * e_gc[x]], axis=0).astype(BF16) for x, p in enumerate(probs)]
    k_dec = [(p[1] * jnp.exp(g_last[x] - gc[x])).astype(BF16) for x, p in enumerate(probs)]
    g_tot = [jnp.exp(g_last[x]) for x in range(n)]
    return u, wq, k_dec, a_intra, g_tot


def _gdn_state_steps(items, bd):
    c = GDN_CHUNK
    pairs = [slice(p * GDN_PAIR, (p + 1) * GDN_PAIR) for p in range(GDN_W // GDN_PAIR)]
    bdf = bd.astype(F32)
    ws_qs = [jnp.concatenate([jnp.dot(wq[:, ls], s_ref[p].astype(BF16), preferred_element_type=F32)
                              for p, ls in enumerate(pairs)], axis=1)
             for _, wq, _, _, _, s_ref in items]
    v_new = [it[0] - ws[:c] for it, ws in zip(items, ws_qs)]
    outs = [ws[c:] + _heads_mm(it[3], vn, bd) for it, ws, vn in zip(items, ws_qs, v_new)]
    for (_, _, k_dec, _, g_tot, s_ref), vn in zip(items, v_new):
        vb = vn.astype(BF16)
        for p, ls in enumerate(pairs):
            kv = lax.dot_general(k_dec[:, ls], vb[:, ls], (((0,), (0,)), ((), ())), preferred_element_type=F32)
            s_ref[p] = s_ref[p] * g_tot[:, ls] + kv * bdf
    return outs


def _gdn_scan_body(qf, kf, vf, bf, gf, qb, kb, vb, bb, gb, bd_ref, lvl_ref, dirm_ref, tri_ref,
                   of_ref, ob_ref, sf_ref, sb_ref):
    bd = bd_ref[...]
    n = SEQ_TILE // GDN_CHUNK
    probs, rows = [], []
    for c in range(n):
        rf = slice(c * GDN_CHUNK, (c + 1) * GDN_CHUNK)
        rb = slice((n - 1 - c) * GDN_CHUNK, (n - c) * GDN_CHUNK)
        probs.append(tuple(r[rf, :].astype(F32) for r in (qf, kf, vf, bf, gf)) + (False,))
        probs.append(tuple(r[rb, :].astype(F32) for r in (qb, kb, vb, bb, gb)) + (True,))
        rows += [rf, rb]
    u, wq, k_dec, a_intra, g_tot = _gdn_intra(probs, bd, lvl_ref, dirm_ref, tri_ref)
    for c in range(n):
        xs = (2 * c, 2 * c + 1)
        outs = _gdn_state_steps([(u[x], wq[x], k_dec[x], a_intra[x], g_tot[x], sb_ref if probs[x][5] else sf_ref)
                                 for x in xs], bd)
        for x, o in zip(xs, outs):
            (ob_ref if probs[x][5] else of_ref)[rows[x], :] = o


def _seq_tile_maps(n_lat_tiles, n_tiles):
    fwd = lambda s: lax.rem(s + n_lat_tiles, n_tiles)
    bwd = lambda s: n_tiles - 1 - s
    return fwd, bwd


def _gdn_scan_specs(q, k, v, beta, g, n_lat):
    b, s, _ = q.shape
    n_tiles = s // SEQ_TILE
    fwd, bwd = _seq_tile_maps(n_lat // SEQ_TILE, n_tiles)
    spec = lambda tile_of, col: pl.BlockSpec((None, SEQ_TILE, GDN_W), lambda bi, t: (bi, tile_of(t), col))
    masks = _gdn_masks()
    in_specs = [spec(fwd, 0)] * 5 + [spec(bwd, 0)] * 3 + [spec(bwd, 1)] * 2 + [_full(m) for m in masks]
    assert len(in_specs) == N_GDN_IN
    return (in_specs, [q, k, v, beta, g, q, k, v, beta, g, *masks], [spec(fwd, 0), spec(bwd, 0)],
            [jax.ShapeDtypeStruct((b, s, GDN_W), F32)] * 2,
            [pltpu.VMEM((GDN_W // GDN_PAIR, GDN_PAIR, GDN_PAIR), F32)] * 2)


def _lru_tile_prep(x_ref, xp_ref, xn_ref, cw_ref, cb_ref, wg_ref, bg_ref, nla_ref, a_ref, b_ref,
                   tile_idx, n_tiles, n_lat_tiles):
    xb = _tile_conv(x_ref[...], xp_ref[...], xn_ref[...], cw_ref, tile_idx, n_tiles, n_lat_tiles) + cb_ref[...]
    gates = _sigmoid(jnp.dot(xb.astype(BF16), wg_ref[...], preferred_element_type=F32) + bg_ref[...])
    log_a = nla_ref[...] * gates[:, :LRU_W]
    a_ref[...] = jnp.exp(log_a)
    th = jnp.tanh(log_a)
    b_ref[...] = jnp.sqrt(-2.0 * th / (1.0 - th)) * gates[:, LRU_W:] * xb


def _scan_group(a, b, h, row, reverse):
    for d in (1, 2, 4):
        if reverse:
            keep = row < 8 - d
            shift = 8 - d
        else:
            keep = row >= d
            shift = d
        a_s = jnp.where(keep, pltpu.roll(a, shift, 0), 1.0)
        b_s = jnp.where(keep, pltpu.roll(b, shift, 0), 0.0)
        b = a * b_s + b
        a = a * a_s
    return a * h + b


def _lru_scan_body(xf, xfp, xfn, xb, xbp, xbn, cw_ref, cb_ref, wgf, bgf, nlaf, wgb, bgb, nlab,
                   hf_ref, hb_ref, af_ref, bf_ref, ab_ref, bb_ref, cf_ref, cbk_ref, *, n_lat_tiles):
    s = pl.program_id(1)
    n_tiles = pl.num_programs(1)
    t_f, t_b = (m(s) for m in _seq_tile_maps(n_lat_tiles, n_tiles))
    _lru_tile_prep(xf, xfp, xfn, cw_ref, cb_ref, wgf, bgf, nlaf, af_ref, bf_ref, t_f, n_tiles, n_lat_tiles)
    _lru_tile_prep(xb, xbp, xbn, cw_ref, cb_ref, wgb, bgb, nlab, ab_ref, bb_ref, t_b, n_tiles, n_lat_tiles)

    n_groups = SEQ_TILE // 8
    row = lax.broadcasted_iota(jnp.int32, (8, LRU_W), 0)

    h_f, h_b = cf_ref[...], cbk_ref[...]
    for gi in range(n_groups):
        rf = slice(gi * 8, gi * 8 + 8)
        rb = slice((n_groups - 1 - gi) * 8, (n_groups - gi) * 8)
        out_f = _scan_group(af_ref[rf, :], bf_ref[rf, :], h_f, row, False)
        out_b = _scan_group(ab_ref[rb, :], bb_ref[rb, :], h_b, row, True)
        hf_ref[rf, :] = out_f
        hb_ref[rb, :] = out_b
        h_f = jnp.broadcast_to(out_f[7:8, :], (8, LRU_W))
        h_b = jnp.broadcast_to(out_b[0:1, :], (8, LRU_W))
    cf_ref[...] = h_f
    cbk_ref[...] = h_b


N_LRU_IN, N_GDN_IN, N_LRU_SCRATCH = 14, 14, 6


def _seq_mixers_kernel(*refs, n_lat_tiles):
    lru_in = refs[:N_LRU_IN]
    gdn_in = refs[N_LRU_IN:N_LRU_IN + N_GDN_IN]
    hf_ref, hb_ref, of_ref, ob_ref = refs[N_LRU_IN + N_GDN_IN:N_LRU_IN + N_GDN_IN + 4]
    scratch = refs[N_LRU_IN + N_GDN_IN + 4:]
    lru_scratch, gdn_scratch = scratch[:N_LRU_SCRATCH], scratch[N_LRU_SCRATCH:]

    @pl.when(pl.program_id(1) == 0)
    def _():
        for ref in lru_scratch[4:] + gdn_scratch:
            ref[...] = jnp.zeros(ref.shape, ref.dtype)

    _lru_scan_body(*lru_in, hf_ref, hb_ref, *lru_scratch, n_lat_tiles=n_lat_tiles)
    _gdn_scan_body(*gdn_in, of_ref, ob_ref, *gdn_scratch)


def _lru_scan_specs(p, conv_w, conv_b, w_r, b_r, w_i, b_i, lam, n_lat):
    b, s, _ = p.shape
    n_tiles = s // SEQ_TILE
    nlt = n_lat // SEQ_TILE
    fwd, bwd = _seq_tile_maps(nlt, n_tiles)
    col = P_LRU_X // LRU_W
    specs = []
    for tile_of in (fwd, bwd):
        main, prev, make_next = _halo_specs(LRU_W, col, tile_of)
        specs += [pl.BlockSpec((None, SEQ_TILE, LRU_W), main), pl.BlockSpec((None, HALO, LRU_W), prev),
                  pl.BlockSpec((None, HALO, LRU_W), make_next(n_tiles))]

    def blockdiag(w):
        return jax.scipy.linalg.block_diag(*[w[n] for n in range(LRU_BLOCKS)])

    dir_args = []
    for d in range(2):
        wg = jnp.concatenate([blockdiag(w_r[d]), blockdiag(w_i[d])], axis=1).astype(BF16)
        bg = jnp.concatenate([b_r[d], b_i[d]]).astype(F32).reshape(1, 2 * LRU_W)
        nla = (-LRU_C * jax.nn.softplus(-lam[d].astype(F32))).reshape(1, LRU_W)
        dir_args += [wg, bg, nla]
    cb2 = conv_b.reshape(1, LRU_W)
    out_spec = lambda tile_of: pl.BlockSpec((None, SEQ_TILE, LRU_W), lambda bi, t: (bi, tile_of(t), 0))
    in_specs = specs + [_full(conv_w), _full(cb2)] + [_full(a) for a in dir_args]
    assert len(in_specs) == N_LRU_IN
    return (in_specs, [p, p, p, p, p, p, conv_w, cb2, *dir_args], [out_spec(fwd), out_spec(bwd)],
            [jax.ShapeDtypeStruct((b, s, LRU_W), F32)] * 2,
            [pltpu.VMEM((SEQ_TILE, LRU_W), F32)] * 4 + [pltpu.VMEM((8, LRU_W), F32)] * 2)


def _seq_mixers(lru_parts, gdn_parts, bsz, n_tiles, n_lat):
    in_specs, args, out_specs, out_shape, scratch = ([*a, *b] for a, b in zip(lru_parts, gdn_parts))
    return pl.pallas_call(
        functools.partial(_seq_mixers_kernel, n_lat_tiles=n_lat // SEQ_TILE),
        grid=(bsz, n_tiles),
        in_specs=in_specs,
        out_specs=out_specs,
        out_shape=out_shape,
        scratch_shapes=scratch,
        compiler_params=_cparams(("parallel", "arbitrary")),
        name="seq_mixers",
    )(*args)


def _merge_kernel(x_ref, mc_ref, ml_ref, of_ref, ob_ref, z_ref, hf_ref, hb_ref, y_ref, uc_ref, ud_ref,
                  t0_ref, t1_ref, t2_ref, t3_ref, gn_ref, ones_ref, bg_ref, wb_ref, wo_ref, o_ref, *, n_lat):
    o = of_ref[...] + ob_ref[...]
    ms = jnp.dot(jnp.concatenate(_split(o * o), axis=-1), ones_ref[...], preferred_element_type=F32) * (1.0 / GDN_DV)
    z = z_ref[...]
    ua = (o * lax.rsqrt(ms + RMS_EPS) * gn_ref[...] * (z * _sigmoid(z))).astype(BF16)
    y = y_ref[...]
    gelu = 0.5 * y * (1.0 + jnp.tanh(0.7978845608028654 * (y + 0.044715 * (y * y * y))))
    ub = ((hf_ref[...] + hb_ref[...]) * gelu).astype(BF16)

    merged = None
    for n, (u, t_ref) in enumerate(((ua, t0_ref), (ub, t1_ref), (uc_ref[...], t2_ref), (ud_ref[...], t3_ref))):
        gate = _sigmoid(t_ref[...] + bg_ref[n:n + 1, :].astype(BF16))
        term = gate.astype(F32) * jnp.dot(u, wb_ref[n], preferred_element_type=F32)
        merged = term if merged is None else merged + term
    out = jnp.dot(merged.astype(BF16), wo_ref[...], preferred_element_type=F32)
    g1 = _row_select(x_ref.shape[0], pl.program_id(1), n_lat, mc_ref[2:3, :], ml_ref[2:3, :])
    o_ref[...] = x_ref[...] + g1 * out


def _merge(xs, mod_c, mod_l, gdn_out, lru_out, uc, ud, p32, p16, gdn_norm_g, b_gate, w_branch, w_out, layer, n_lat,
           with_ctx):
    b, s, d = xs.shape
    rows = s if with_ctx else n_lat
    tm = _pick_tile(rows, 544)
    gate_blk = P_GATES // d
    tok = lambda w: pl.BlockSpec((None, tm, w), lambda bi, i: (bi, i, 0))
    ones_n = jnp.concatenate([_head_block_ones(GDN_HEADS, GDN_DV)] * N_SPLIT, axis=0)
    gn = jnp.tile(gdn_norm_g.astype(F32), GDN_HEADS).reshape(1, GDN_W)
    pcol = lambda c: pl.BlockSpec((None, tm, BRANCH_W), lambda bi, i: (bi, i, c))
    in_specs = [tok(d), pl.BlockSpec((8, d), lambda bi, i: (0, 0)), pl.BlockSpec((None, 8, d), lambda bi, i: (bi, 0, 0))]
    in_specs += [tok(BRANCH_W), tok(BRANCH_W), pcol(P_GDN_Z // GDN_W), tok(BRANCH_W), tok(BRANCH_W),
                 pcol(P_LRU_Y // LRU_W), tok(BRANCH_W), tok(BRANCH_W)]
    in_specs += [pl.BlockSpec((None, tm, d), functools.partial(lambda bi, i, n: (bi, i, gate_blk + n), n=n))
                 for n in range(N_BRANCH)]
    in_specs += [_full(gn), _full(ones_n), _full(b_gate),
                 pl.BlockSpec((None,) + w_branch.shape[1:], lambda bi, i: (layer, 0, 0, 0)),
                 pl.BlockSpec((None,) + w_out.shape[1:], lambda bi, i: (layer, 0, 0))]
    return pl.pallas_call(
        functools.partial(_merge_kernel, n_lat=n_lat),
        grid=(b, rows // tm),
        in_specs=in_specs,
        out_specs=tok(d),
        out_shape=jax.ShapeDtypeStruct((b, rows, d), F32),
        input_output_aliases={0: 0} if with_ctx else {},
        compiler_params=_cparams(("parallel", "parallel")),
        name="merge",
    )(xs, mod_c, mod_l, gdn_out[0], gdn_out[1], p32, lru_out[0], lru_out[1], p32, uc, ud, p16, p16, p16, p16,
      gn, ones_n, b_gate, w_branch, w_out)


def _mlp_kernel(x_ref, mc_ref, ml_ref, gn_ref, gf_ref, w1_ref, w2_ref, o_ref, h_ref, acc_ref, *, n_lat, final_norm):
    f = pl.program_id(2)
    tm = x_ref.shape[0]
    i = pl.program_id(1)

    @pl.when(f == 0)
    def _():
        _norm_modulate(x_ref, h_ref, gn_ref, mc_ref, ml_ref, 3, i, n_lat)
        acc_ref[...] = jnp.zeros(acc_ref.shape, F32)

    a = jnp.maximum(jnp.dot(h_ref[...], w1_ref[...], preferred_element_type=F32), 0.0)
    total = acc_ref[...] + jnp.dot((a * a).astype(BF16), w2_ref[...], preferred_element_type=F32)
    acc_ref[...] = total
    g2 = _row_select(tm, i, n_lat, mc_ref[5:6, :], ml_ref[5:6, :])
    y = x_ref[...] + g2 * total
    if final_norm:
        y = y * lax.rsqrt(jnp.mean(y * y, axis=-1, keepdims=True) + RMS_EPS) * gf_ref[...]
    o_ref[...] = y


def _mlp(xs, mod_c, mod_l, gain, w1, w2, layer, final_gain, n_lat, final_norm):
    b, rows, d = xs.shape
    dff = w1.shape[-1]
    tm = _pick_tile(rows, 1088)
    tf = 1024
    row = pl.BlockSpec((1, d), lambda bi, i, f: (0, 0))
    return pl.pallas_call(
        functools.partial(_mlp_kernel, n_lat=n_lat, final_norm=final_norm),
        grid=(b, rows // tm, dff // tf),
        in_specs=[pl.BlockSpec((None, tm, d), lambda bi, i, f: (bi, i, 0)),
                  pl.BlockSpec((8, d), lambda bi, i, f: (0, 0)),
                  pl.BlockSpec((None, 8, d), lambda bi, i, f: (bi, 0, 0)),
                  row, row,
                  pl.BlockSpec((None, d, tf), lambda bi, i, f: (layer, 0, f)),
                  pl.BlockSpec((None, tf, d), lambda bi, i, f: (layer, f, 0))],
        out_specs=pl.BlockSpec((None, tm, d), lambda bi, i, f: (bi, i, 0)),
        out_shape=jax.ShapeDtypeStruct((b, rows, d), F32),
        scratch_shapes=[pltpu.VMEM((tm, d), BF16), pltpu.VMEM((tm, d), F32)],
        compiler_params=_cparams(("parallel", "parallel", "arbitrary")),
        name="mlp",
    )(xs, mod_c, mod_l, gain, final_gain, w1, w2)


def kernel(x, c, ctx, c_ctx, mod_w, mod_b, norm1_g, norm2_g, w_in, b_gate, gdn_conv_w, gdn_a_log, gdn_dt_bias,
           gdn_norm_g, lru_conv_w, lru_conv_b, lru_w_r, lru_b_r, lru_w_i, lru_b_i, lru_lambda, mla_q_norm_g,
           mla_w_uq, mla_kv_norm_g, mla_w_ukv, na_rpb, w_branch, w_out, mlp_w1, mlp_w2, final_norm_g):
    bsz, n_tok, d = x.shape
    n_ctx = ctx.shape[1]
    depth = w_in.shape[0]
    assert n_ctx % SEQ_TILE == 0 and n_tok % SEQ_TILE == 0 and n_tok % GRID_W == 0
    na_meta, na_rel_rows, na_cols = _na_geometry(n_tok // GRID_W)
    na_meta = jnp.asarray(na_meta)
    cos, sin = _rope_tables(n_tok, n_ctx)

    n_rows = -(-(bsz + 1) // 8) * 8
    cc = jnp.zeros((n_rows, d), F32).at[:bsz].set(c).at[bsz].set(c_ctx)
    final_gain = final_norm_g.reshape(1, d)

    w_in_all = _arrange_w_in(w_in)
    wb_all = w_branch.astype(BF16)
    wo_all = w_out.astype(BF16)
    w1_all = mlp_w1.astype(BF16)
    w2_all = mlp_w2.astype(BF16)

    xs = jnp.concatenate([x, ctx], axis=1)
    for l in range(depth):
        need_ctx = l < depth - 1
        mod = _modulation(cc, mod_w, mod_b[l], l).reshape(n_rows, N_MOD, d)
        pad = jnp.zeros((8 - N_MOD, d), F32)
        mod_c = jnp.concatenate([mod[bsz], pad], axis=0)
        mod_l = jnp.concatenate([mod[:bsz], jnp.broadcast_to(pad, (bsz, 8 - N_MOD, d))], axis=1)

        wq, wk, wv, place = _arrange_mla(mla_w_uq[l], mla_w_ukv[l])
        gq = mla_q_norm_g[l].reshape(1, -1)
        gkv = mla_kv_norm_g[l].reshape(1, -1)
        g1n = norm1_g[l].reshape(1, d)
        g2n = norm2_g[l].reshape(1, d)

        p32, p16 = _inproj(xs, mod_c, mod_l, g1n, w_in_all, l, n_tok)

        gq_, gk_, gv_, gbeta, gg = _gdn_prep(p32, gdn_conv_w[l], gdn_a_log[l], gdn_dt_bias[l], n_tok)
        h_f, h_b, o_f, o_b = _seq_mixers(
            _lru_scan_specs(p32, lru_conv_w[l], lru_conv_b[l], lru_w_r[l], lru_b_r[l], lru_w_i[l], lru_b_i[l],
                            lru_lambda[l], n_tok),
            _gdn_scan_specs(gq_, gk_, gv_, gbeta, gg, n_tok), bsz, (n_tok + n_ctx) // SEQ_TILE, n_tok)

        mq, mk, mv = _mla_prep(p32, p16, cos, sin, gq, gkv, wq, wk, wv, place)
        uc = _mla_flash(mq, mk, mv, n_tok)
        if need_ctx:
            uc = _mla_flash(mq, mk, mv, n_tok, prev=uc)
        ud = _na_attention(p16, na_meta, _na_bias_table(na_rpb[l], na_rel_rows, na_cols), n_tok, need_ctx)

        xs = _merge(xs, mod_c, mod_l, (o_f, o_b), (h_f, h_b), uc, ud, p32, p16, gdn_norm_g[l], b_gate[l],
                    wb_all, wo_all, l, n_tok, need_ctx)
        xs = _mlp(xs, mod_c, mod_l, g2n, w1_all, w2_all, l, final_gain, n_tok, l == depth - 1)
    return xs
```

```python
import functools

import jax
import jax.numpy as jnp
import numpy as np
from jax import lax
from jax.experimental import pallas as pl
from jax.experimental.pallas import tpu as pltpu

F32 = jnp.float32
BF16 = jnp.bfloat16

GRID_W = 64
N_MOD = 6
RMS_EPS = 1e-6
GDN_HEADS = 4
GDN_DK = 64
GDN_DV = 64
GDN_CHUNK = 64
GDN_W = GDN_HEADS * GDN_DV
LRU_W = 256
LRU_BLOCKS = 4
LRU_C = 8.0
MLA_HEADS = 4
MLA_Q_RANK = 256
MLA_KV_RANK = 128
MLA_NOPE = 64
MLA_ROPE = 32
MLA_V = 64
MLA_SLOT = 128
ROPE_BASE = 10000.0
LOG2_E = 1.4426950408889634
NA_HEADS = 4
NA_DH = 64
NA_W = NA_HEADS * NA_DH
NA_WIN_ROWS = 8
NA_WIN_COLS = 16
N_BRANCH = 4
BRANCH_W = 256

SEQ_TILE = 256
HALO = 8

_REF_COLS = {}
_off = 0
for _name, _w in (('gdn_qkv', 3 * GDN_W), ('gdn_z', GDN_W), ('gdn_beta', 2 * GDN_HEADS), ('gdn_a', 2 * GDN_HEADS),
                  ('lru_x', LRU_W), ('lru_y', LRU_W), ('mla_q', MLA_Q_RANK), ('mla_kv', MLA_KV_RANK),
                  ('mla_kr', MLA_ROPE), ('na_qkv', 3 * NA_W)):
    _REF_COLS[_name] = (_off, _w)
    _off += _w
N_MIX_COLS = _off

P_GDN_QKV = 0
P_GDN_Z = 768
P_LRU_X = 1024
P_LRU_Y = 1280
P_MLA_KV = 1536
P_MLA_KR = 1664
P_GDN_BA = 1792
P32_COLS = 2048
P_MLA_Q = 0
P_NA_QKV = 256
P_GATES = 1024
P16_COLS = P_GATES + N_BRANCH * 1024
PROJ_TILE = 1024

V7X_VMEM_BYTES = 64 * 1024 * 1024
VMEM_LIMIT = V7X_VMEM_BYTES - 12 * 1024 * 1024


def _cparams(sem):
    return pltpu.CompilerParams(dimension_semantics=sem, vmem_limit_bytes=VMEM_LIMIT)


def _pick_tile(n, cap):
    best = 8
    for t in range(8, min(n, cap) + 1, 8):
        if n % t == 0:
            best = t
    return best


def _full(a):
    return pl.BlockSpec(a.shape, lambda *_: (0,) * a.ndim)


N_SPLIT = 2


def _split(x):
    hi = x.astype(BF16)
    lo = (x - hi.astype(F32)).astype(BF16)
    return hi, lo


def _sigmoid(x):
    return 0.5 * jnp.tanh(0.5 * x) + 0.5


def _arrange_w_in(w_in):
    pieces, pos = [], 0

    def put(dst, block):
        nonlocal pos
        if dst > pos:
            pieces.append(jnp.zeros(w_in.shape[:-1] + (dst - pos,), w_in.dtype))
        pieces.append(block)
        pos = dst + block.shape[-1]

    ref = lambda name: w_in[..., _REF_COLS[name][0]:_REF_COLS[name][0] + _REF_COLS[name][1]]
    for name, dst in (('gdn_qkv', P_GDN_QKV), ('gdn_z', P_GDN_Z), ('lru_x', P_LRU_X), ('lru_y', P_LRU_Y),
                      ('mla_kv', P_MLA_KV), ('mla_kr', P_MLA_KR), ('gdn_beta', P_GDN_BA),
                      ('gdn_a', P_GDN_BA + 2 * GDN_HEADS)):
        put(dst, ref(name))
    put(P32_COLS + P_MLA_Q, ref('mla_q'))
    put(P32_COLS + P_NA_QKV, ref('na_qkv'))
    put(P32_COLS + P_GATES, w_in[..., N_MIX_COLS:])
    assert pos == P32_COLS + P16_COLS
    return jnp.concatenate(pieces, axis=-1).astype(BF16)


def _rope_perm():
    q = MLA_ROPE // 4
    src = np.zeros(MLA_ROPE, np.int32)
    sign = np.zeros(MLA_ROPE, np.float32)
    for base in (0, 2 * q):
        for d in range(q):
            src[base + d] = base + d + q
            sign[base + d] = -1.0
            src[base + q + d] = base + d
            sign[base + q + d] = 1.0
    return src, sign


def _arrange_mla(w_uq, w_ukv):
    src, sign = _rope_perm()
    hq = MLA_NOPE + MLA_ROPE
    wq = jnp.zeros((MLA_Q_RANK, 2 * MLA_HEADS * MLA_SLOT), F32)
    wk = jnp.zeros((MLA_KV_RANK, MLA_HEADS * MLA_SLOT), F32)
    wv = jnp.zeros((MLA_KV_RANK, MLA_HEADS * MLA_V), F32)
    place = np.zeros((2 * MLA_SLOT, 2 * MLA_HEADS * MLA_SLOT), np.float32)
    rot_off = MLA_HEADS * MLA_SLOT
    for h in range(MLA_HEADS):
        nope = w_uq[:, h * hq:h * hq + MLA_NOPE]
        pe = w_uq[:, h * hq + MLA_NOPE:(h + 1) * hq]
        s = h * MLA_SLOT
        wq = wq.at[:, s:s + MLA_NOPE].set(nope)
        wq = wq.at[:, s + MLA_NOPE:s + MLA_NOPE + MLA_ROPE].set(pe)
        wq = wq.at[:, rot_off + s + MLA_NOPE:rot_off + s + MLA_NOPE + MLA_ROPE].set(pe[:, src] * sign)
        wk = wk.at[:, s:s + MLA_NOPE].set(w_ukv[:, h * 128:h * 128 + MLA_NOPE])
        wv = wv.at[:, h * MLA_V:(h + 1) * MLA_V].set(w_ukv[:, h * 128 + MLA_NOPE:(h + 1) * 128])
        for d in range(MLA_ROPE):
            for half in (0, MLA_SLOT):
                place[half + d, s + MLA_NOPE + d] = 1.0
                place[half + src[d], rot_off + s + MLA_NOPE + d] = sign[d]
    return wq.astype(BF16), wk.astype(BF16), wv.astype(BF16), jnp.asarray(place, BF16)


def _rope_tables(n_tok, n_ctx):
    cos = np.ones((n_tok + n_ctx, MLA_SLOT), np.float32)
    sin = np.zeros((n_tok + n_ctx, MLA_SLOT), np.float32)
    t = np.arange(n_tok)
    row = (t // GRID_W).astype(np.float32)
    col = (t % GRID_W).astype(np.float32)
    n_freq = MLA_ROPE // 4
    inv = (ROPE_BASE ** (-np.arange(n_freq, dtype=np.float32) / n_freq)).astype(np.float32)
    ar = row[:, None] * inv
    ac = col[:, None] * inv
    ang = np.concatenate([ar, ar, ac, ac], axis=-1).astype(np.float32)
    cos[:n_tok, MLA_NOPE:MLA_NOPE + MLA_ROPE] = np.cos(ang)
    sin[:n_tok, MLA_NOPE:MLA_NOPE + MLA_ROPE] = np.sin(ang)
    return jnp.asarray(cos), jnp.asarray(sin)


def _head_block_ones(n_heads, width):
    m = np.kron(np.eye(n_heads, dtype=np.float32), np.ones((width, width), np.float32))
    return jnp.asarray(m, BF16)


def _mod_kernel(c_ref, w_ref, b_ref, o_ref):
    c = c_ref[...]
    s = c * _sigmoid(c)
    o_ref[...] = jnp.dot(s, w_ref[...], preferred_element_type=F32) + b_ref[...]


def _modulation(cc, mod_w, mod_b, layer):
    r, d = cc.shape
    n = mod_w.shape[-1]
    tn = 1024
    return pl.pallas_call(
        _mod_kernel,
        grid=(n // tn,),
        in_specs=[pl.BlockSpec((r, d), lambda j: (0, 0)),
                  pl.BlockSpec((None, d, tn), lambda j: (layer, 0, j)),
                  pl.BlockSpec((1, tn), lambda j: (0, j))],
        out_specs=pl.BlockSpec((r, tn), lambda j: (0, j)),
        out_shape=jax.ShapeDtypeStruct((r, n), F32),
        compiler_params=_cparams(("arbitrary",)),
        name="modulation",
    )(cc, mod_w, mod_b.reshape(1, n))


def _row_select(tile_rows, tile_idx, n_lat, ctx_vec, lat_vec):
    row = tile_idx * tile_rows + lax.broadcasted_iota(jnp.int32, (tile_rows, 1), 0)
    return jnp.where(row < n_lat, lat_vec, ctx_vec)


def _norm_modulate(x_ref, h_ref, gain_ref, mc_ref, ml_ref, shift_row, tile_idx, n_lat):
    tm = x_ref.shape[0]
    x = x_ref[...]
    xn = x * lax.rsqrt(jnp.mean(x * x, axis=-1, keepdims=True) + RMS_EPS)
    sh_l, sh_c = ml_ref[shift_row:shift_row + 1, :], mc_ref[shift_row:shift_row + 1, :]
    amp_l = gain_ref[...] * (1.0 + ml_ref[shift_row + 1:shift_row + 2, :])
    amp_c = gain_ref[...] * (1.0 + mc_ref[shift_row + 1:shift_row + 2, :])
    all_latent = (tile_idx + 1) * tm <= n_lat

    @pl.when(all_latent)
    def _():
        h_ref[...] = (xn * amp_l + sh_l).astype(h_ref.dtype)

    @pl.when(jnp.logical_not(all_latent))
    def _():
        amp = _row_select(tm, tile_idx, n_lat, amp_c, amp_l)
        shift = _row_select(tm, tile_idx, n_lat, sh_c, sh_l)
        h_ref[...] = (xn * amp + shift).astype(h_ref.dtype)


def _inproj_first_kernel(x_ref, mc_ref, ml_ref, g_ref, w_ref, o_ref, h_ref, *, n_lat):
    @pl.when(pl.program_id(2) == 0)
    def _():
        _norm_modulate(x_ref, h_ref, g_ref, mc_ref, ml_ref, 0, pl.program_id(1), n_lat)

    o_ref[...] = jnp.dot(h_ref[...], w_ref[...], preferred_element_type=F32).astype(o_ref.dtype)


def _inproj_second_kernel(h_ref, w_ref, o_ref):
    o_ref[...] = jnp.dot(h_ref[...], w_ref[...], preferred_element_type=F32).astype(o_ref.dtype)


def _inproj(xs, mod_c, mod_l, gain, w, layer, n_lat):
    b, s, d = xs.shape
    tm = _pick_tile(s, 1088)
    tn = PROJ_TILE
    n32 = P32_COLS // tn
    p16, h = pl.pallas_call(
        functools.partial(_inproj_first_kernel, n_lat=n_lat),
        grid=(b, s // tm, P16_COLS // tn),
        in_specs=[pl.BlockSpec((None, tm, d), lambda bi, i, j: (bi, i, 0)),
                  pl.BlockSpec((8, d), lambda bi, i, j: (0, 0)),
                  pl.BlockSpec((None, 8, d), lambda bi, i, j: (bi, 0, 0)),
                  pl.BlockSpec((1, d), lambda bi, i, j: (0, 0)),
                  pl.BlockSpec((None, d, tn), lambda bi, i, j: (layer, 0, j + n32))],
        out_specs=[pl.BlockSpec((None, tm, tn), lambda bi, i, j: (bi, i, j)),
                   pl.BlockSpec((None, tm, d), lambda bi, i, j: (bi, i, 0))],
        out_shape=[jax.ShapeDtypeStruct((b, s, P16_COLS), BF16), jax.ShapeDtypeStruct((b, s, d), BF16)],
        compiler_params=_cparams(("parallel", "parallel", "arbitrary")),
        name="inproj_bf16",
    )(xs, mod_c, mod_l, gain, w)
    p32 = pl.pallas_call(
        _inproj_second_kernel,
        grid=(b, s // tm, n32),
        in_specs=[pl.BlockSpec((None, tm, d), lambda bi, i, j: (bi, i, 0)),
                  pl.BlockSpec((None, d, tn), lambda bi, i, j: (layer, 0, j))],
        out_specs=pl.BlockSpec((None, tm, tn), lambda bi, i, j: (bi, i, j)),
        out_shape=jax.ShapeDtypeStruct((b, s, P32_COLS), F32),
        compiler_params=_cparams(("parallel", "parallel", "arbitrary")),
        name="inproj_f32",
    )(h, w)
    return p32, p16


def _mla_prep_kernel(ql_ref, kv_ref, cos_ref, sin_ref, gq_ref, gkv_ref, wq_ref, wk_ref, wv_ref, pl_ref,
                     q_ref, k_ref, v_ref, *, scale):
    nslot = MLA_HEADS * MLA_SLOT
    cos = jnp.concatenate([cos_ref[...]] * MLA_HEADS, axis=-1)
    sin = jnp.concatenate([sin_ref[...]] * MLA_HEADS, axis=-1)

    ql = ql_ref[...].astype(F32)
    qn = ql * lax.rsqrt(jnp.mean(ql * ql, axis=-1, keepdims=True) + RMS_EPS) * gq_ref[...]
    q2 = jnp.dot(qn.astype(BF16), wq_ref[...], preferred_element_type=F32)
    q = (q2[:, :nslot] * cos + q2[:, nslot:] * sin) * scale
    q_ref[...] = q.astype(BF16)

    kvkr = kv_ref[...]
    kvl = kvkr[:, :MLA_KV_RANK]
    kr = kvkr[:, MLA_KV_RANK:]
    kvn = (kvl * lax.rsqrt(jnp.mean(kvl * kvl, axis=-1, keepdims=True) + RMS_EPS) * gkv_ref[...]).astype(BF16)
    kn = jnp.dot(kvn, wk_ref[...], preferred_element_type=F32)
    v_ref[...] = jnp.dot(kvn, wv_ref[...], preferred_element_type=F32).astype(BF16)
    kr_hi = kr.astype(BF16)
    kr_lo = (kr - kr_hi.astype(F32)).astype(BF16)
    kr2 = jnp.dot(jnp.concatenate([kr_hi, kr_lo], axis=-1), pl_ref[...], preferred_element_type=F32)
    k_ref[...] = (kn + kr2[:, :nslot] * cos + kr2[:, nslot:] * sin).astype(BF16)


def _mla_prep(p32, p16, cos, sin, gq, gkv, wq, wk, wv, place):
    b, s, _ = p32.shape
    tm = _pick_tile(s, 544)
    nslot = MLA_HEADS * MLA_SLOT
    scale = (MLA_NOPE + MLA_ROPE) ** -0.5 * LOG2_E
    return pl.pallas_call(
        functools.partial(_mla_prep_kernel, scale=scale),
        grid=(b, s // tm),
        in_specs=[pl.BlockSpec((None, tm, MLA_Q_RANK), lambda bi, i: (bi, i, P_MLA_Q // MLA_Q_RANK)),
                  pl.BlockSpec((None, tm, 2 * MLA_SLOT), lambda bi, i: (bi, i, P_MLA_KV // (2 * MLA_SLOT))),
                  pl.BlockSpec((tm, MLA_SLOT), lambda bi, i: (i, 0)),
                  pl.BlockSpec((tm, MLA_SLOT), lambda bi, i: (i, 0)),
                  _full(gq), _full(gkv), _full(wq), _full(wk), _full(wv), _full(place)],
        out_specs=[pl.BlockSpec((None, tm, nslot), lambda bi, i: (bi, i, 0)),
                   pl.BlockSpec((None, tm, nslot), lambda bi, i: (bi, i, 0)),
                   pl.BlockSpec((None, tm, MLA_HEADS * MLA_V), lambda bi, i: (bi, i, 0))],
        out_shape=[jax.ShapeDtypeStruct((b, s, nslot), BF16),
                   jax.ShapeDtypeStruct((b, s, nslot), BF16),
                   jax.ShapeDtypeStruct((b, s, MLA_HEADS * MLA_V), BF16)],
        compiler_params=_cparams(("parallel", "parallel")),
        name="mla_prep",
    )(p16, p32, cos, sin, gq, gkv, wq, wk, wv, place)


def _flash_softmax(h, s, m_ref, l_ref, first):
    row_max = jnp.max(s, axis=-1, keepdims=True)
    m_new = jnp.broadcast_to(row_max, m_ref.shape[1:]) if first else jnp.maximum(m_ref[h], row_max)
    p = jnp.exp2(s - jnp.concatenate([m_new] * (s.shape[1] // 128), axis=-1))
    row_sum = jnp.sum(p, axis=-1, keepdims=True)
    if first:
        alpha = None
        l_ref[h] = jnp.broadcast_to(row_sum, l_ref.shape[1:])
    else:
        alpha = jnp.exp2(m_ref[h] - m_new)
        l_ref[h] = alpha * l_ref[h] + row_sum
    m_ref[h] = m_new
    return alpha, p.astype(BF16)


def _mla_flash_kernel(*refs, tk, aliased):
    if aliased:
        q_ref, k_ref, v_ref, _, o_ref, m_ref, l_ref, acc_ref = refs
    else:
        q_ref, k_ref, v_ref, o_ref, m_ref, l_ref, acc_ref = refs
    tq = q_ref.shape[0]
    nk = k_ref.shape[0]
    n_loop = nk // tk

    def chunk(rows, first=False):
        heads = range(MLA_HEADS)
        hs = [slice(h * MLA_SLOT, (h + 1) * MLA_SLOT) for h in heads]
        vs = [slice((h // 2) * 128, (h // 2) * 128 + 128) for h in heads]
        s = [lax.dot_general(q_ref[:, hs[h]], k_ref[rows, hs[h]], (((1,), (1,)), ((), ())),
                             preferred_element_type=F32) for h in heads]
        ap = [_flash_softmax(h, s[h], m_ref, l_ref, first) for h in heads]
        for h in heads:
            alpha, p = ap[h]
            pv = jnp.dot(p, v_ref[rows, vs[h]], preferred_element_type=F32)
            acc_ref[h] = pv if first else alpha * acc_ref[h] + pv

    has_tail = nk > n_loop * tk
    if has_tail:
        chunk(slice(n_loop * tk, nk), first=True)
    else:
        chunk(slice(0, tk), first=True)
    lo = 0 if has_tail else 1
    if n_loop > lo:
        def body(j, carry):
            chunk(pl.ds(pl.multiple_of(j * tk, tk), tk))
            return carry
        lax.fori_loop(lo, n_loop, body, 0)

    lane = lax.broadcasted_iota(jnp.int32, (tq, 128), 1)
    outs = []
    for pair in range(MLA_HEADS // 2):
        o0 = acc_ref[2 * pair] / l_ref[2 * pair]
        o1 = acc_ref[2 * pair + 1] / l_ref[2 * pair + 1]
        outs.append(jnp.where(lane < MLA_V, o0, o1))
    o_ref[...] = jnp.concatenate(outs, axis=-1).astype(o_ref.dtype)


def _mla_flash(q, k, v, n_lat, prev=None):
    b, s, nslot = q.shape
    n_ctx = s - n_lat
    nv = MLA_HEADS * MLA_V
    tk = 512
    if prev is None:
        tq = _pick_tile(n_lat, 1024)
        q_off, n_q, kv_rows, kv_blk = 0, n_lat // tq, s, 0
    else:
        assert n_lat % n_ctx == 0
        tq = _pick_tile(n_ctx, 256)
        q_off, n_q, kv_rows, kv_blk = n_lat // tq, n_ctx // tq, n_ctx, n_lat // n_ctx
    in_specs = [pl.BlockSpec((None, tq, nslot), lambda bi, i: (bi, i + q_off, 0)),
                pl.BlockSpec((None, kv_rows, nslot), lambda bi, i: (bi, kv_blk, 0)),
                pl.BlockSpec((None, kv_rows, nv), lambda bi, i: (bi, kv_blk, 0))]
    args = [q, k, v]
    aliases = {}
    if prev is not None:
        in_specs.append(pl.BlockSpec(memory_space=pl.ANY))
        args.append(prev)
        aliases = {3: 0}
    return pl.pallas_call(
        functools.partial(_mla_flash_kernel, tk=tk, aliased=prev is not None),
        grid=(b, n_q),
        in_specs=in_specs,
        out_specs=pl.BlockSpec((None, tq, nv), lambda bi, i: (bi, i + q_off, 0)),
        out_shape=jax.ShapeDtypeStruct((b, s, nv), BF16),
        scratch_shapes=[pltpu.VMEM((MLA_HEADS, tq, 128), F32),
                        pltpu.VMEM((MLA_HEADS, tq, 128), F32),
                        pltpu.VMEM((MLA_HEADS, tq, 128), F32)],
        input_output_aliases=aliases,
        compiler_params=_cparams(("parallel", "arbitrary")),
        name="mla_flash",
    )(*args)


NA_QROWS = 4
NA_SLAB = NA_QROWS + NA_WIN_ROWS


def _na_geometry(rows):
    assert rows % NA_QROWS == 0 and rows >= NA_SLAB
    nblk = rows // NA_QROWS
    qc = np.arange(GRID_W)
    cs = np.clip(qc - NA_WIN_COLS // 2, 0, GRID_W - NA_WIN_COLS)
    col_valid = (qc[None, :] >= cs[:, None]) & (qc[None, :] < cs[:, None] + NA_WIN_COLS)
    rel_c = np.clip(qc[None, :] - qc[:, None] + NA_WIN_COLS - 1, 0, 2 * NA_WIN_COLS - 2)
    onehot_c = (rel_c[None] == np.arange(2 * NA_WIN_COLS - 1)[:, None, None]) & col_valid[None]
    patterns, cls, starts = {}, [], []
    for i in range(nblk):
        r0 = i * NA_QROWS
        start = int(np.clip(r0 - NA_WIN_ROWS // 2, 0, rows - NA_SLAB))
        qr = r0 + np.arange(NA_QROWS)
        rs = np.clip(qr - NA_WIN_ROWS // 2, 0, rows - NA_WIN_ROWS)
        key = (start - r0,) + tuple((rs - r0).tolist())
        if key not in patterns:
            kr = start + np.arange(NA_SLAB)
            row_valid = (kr[None, :] >= rs[:, None]) & (kr[None, :] < rs[:, None] + NA_WIN_ROWS)
            rel_r = np.clip(kr[None, :] - qr[:, None] + NA_WIN_ROWS - 1, 0, 2 * NA_WIN_ROWS - 2)
            patterns[key] = (len(patterns), np.where(row_valid, rel_r, -1))
        cls.append(patterns[key][0])
        starts.append(start)
    ordered = sorted(patterns.values(), key=lambda z: z[0])
    rel_rows = np.stack([z[1] for z in ordered])
    meta = np.stack([np.asarray(cls, np.int32), np.asarray(starts, np.int32)])
    return meta, rel_rows, (onehot_c.astype(np.float32), col_valid)


def _na_bias_table(rpb, rel_rows, col_tables):
    onehot_c, col_valid = col_tables
    bc = jnp.einsum('hrc,cqk->hrqk', rpb.astype(F32), onehot_c, precision=lax.Precision.HIGHEST)
    bc = jnp.where(col_valid, bc, -jnp.inf)
    outside = jnp.full(bc.shape[:1] + bc.shape[2:], -jnp.inf, F32)
    pats = []
    for pat in rel_rows:
        qrows = [jnp.concatenate([bc[:, r] if r >= 0 else outside for r in row], axis=-1) for row in pat]
        pats.append(jnp.concatenate(qrows, axis=1))
    return jnp.stack(pats)


def _na_scores(q_pair, lane, h, k_parts):
    in_head = (lane < NA_DH) if h % 2 == 0 else (lane >= NA_DH)
    qm = jnp.where(in_head, q_pair * (NA_DH ** -0.5), 0.0).astype(BF16)
    return [lax.dot_general(qm, k, (((1,), (1,)), ((), ())), preferred_element_type=F32) for k in k_parts]


def _na_softmax(scores, bias):
    if bias is not None:
        scores = [scores[0] + bias] + scores[1:]
    m = scores[0].max(axis=-1, keepdims=True)
    for s in scores[1:]:
        m = jnp.maximum(m, s.max(axis=-1, keepdims=True))
    probs = [jnp.exp(s - m) for s in scores]
    den = probs[0].sum(axis=-1, keepdims=True)
    for p in probs[1:]:
        den = den + p.sum(axis=-1, keepdims=True)
    return [p.astype(BF16) for p in probs], den


def _na_kernel(meta_ref, q_ref, k_ref, v_ref, bias_ref, o_ref, *, n_lat, with_ctx):
    i = pl.program_id(1)
    nq = q_ref.shape[0]
    n_lat_tiles = n_lat // nq
    n_all = k_ref.shape[0]
    lane = lax.broadcasted_iota(jnp.int32, (nq, 128), 1)

    def run(windowed):
        key_rows = [pl.ds(n_lat, n_all - n_lat)]
        if windowed:
            start = pl.multiple_of(meta_ref[1, jnp.minimum(i, n_lat_tiles - 1)] * GRID_W, GRID_W)
            key_rows = [pl.ds(start, NA_SLAB * GRID_W)] + key_rows
        pair_lanes = [slice((h // 2) * 128, (h // 2 + 1) * 128) for h in range(NA_HEADS)]
        scores = [_na_scores(q_ref[:, pair_lanes[h]], lane, h, [k_ref[r, pair_lanes[h]] for r in key_rows])
                  for h in range(NA_HEADS)]
        soft = [_na_softmax(scores[h], bias_ref[h] if windowed else None) for h in range(NA_HEADS)]
        heads = []
        for h in range(NA_HEADS):
            probs, den = soft[h]
            out = None
            for p, r in zip(probs, key_rows):
                po = jnp.dot(p, v_ref[r, pair_lanes[h]], preferred_element_type=F32)
                out = po if out is None else out + po
            heads.append(out / den)
        outs = [jnp.where(lane < NA_DH, heads[2 * pair], heads[2 * pair + 1]) for pair in range(NA_HEADS // 2)]
        o_ref[...] = jnp.concatenate(outs, axis=-1).astype(o_ref.dtype)

    if with_ctx:
        pl.when(i < n_lat_tiles)(lambda: run(True))
        pl.when(i >= n_lat_tiles)(lambda: run(False))
    else:
        run(True)


def _na_attention(p, meta, table, n_lat, with_ctx_queries):
    b, s, _ = p.shape
    nq = NA_QROWS * GRID_W
    nk = NA_SLAB * GRID_W
    assert (s - n_lat) % nq == 0
    nlt = n_lat // nq
    qb = P_NA_QKV // NA_W
    grid_spec = pltpu.PrefetchScalarGridSpec(
        num_scalar_prefetch=1,
        grid=(b, s // nq if with_ctx_queries else nlt),
        in_specs=[pl.BlockSpec((None, nq, NA_W), lambda bi, i, m: (bi, i, qb)),
                  pl.BlockSpec((None, s, NA_W), lambda bi, i, m: (bi, 0, qb + 1)),
                  pl.BlockSpec((None, s, NA_W), lambda bi, i, m: (bi, 0, qb + 2)),
                  pl.BlockSpec((None, NA_HEADS, nq, nk),
                               lambda bi, i, m: (m[0, jnp.minimum(i, nlt - 1)], 0, 0, 0))],
        out_specs=pl.BlockSpec((None, nq, NA_W), lambda bi, i, m: (bi, i, 0)))
    return pl.pallas_call(
        functools.partial(_na_kernel, n_lat=n_lat, with_ctx=with_ctx_queries),
        grid_spec=grid_spec,
        out_shape=jax.ShapeDtypeStruct((b, s, NA_W), BF16),
        compiler_params=_cparams(("parallel", "arbitrary")),
        name="na_attention",
    )(meta, p, p, p, table)


def _tile_conv(x, prev, nxt, w_ref, tile_idx, n_tiles, n_lat_tiles):
    r = x.shape[0]
    width = w_ref.shape[0]
    left = width // 2
    has_prev = jnp.logical_and(tile_idx != 0, tile_idx != n_lat_tiles)
    has_next = jnp.logical_and(tile_idx != n_tiles - 1, tile_idx != n_lat_tiles - 1)
    prev = jnp.where(has_prev, prev, 0.0)
    nxt = jnp.where(has_next, nxt, 0.0)
    xe = jnp.concatenate([prev, x, nxt], axis=0)
    acc = None
    for j in range(width):
        o = HALO - left + j
        term = xe[o:o + r, :] * w_ref[j:j + 1, :]
        acc = term if acc is None else acc + term
    return acc


def _halo_specs(width, col_block, tile_of):
    per = SEQ_TILE // HALO

    def main(bi, s, *_):
        return (bi, tile_of(s), col_block)

    def prev(bi, s, *_):
        return (bi, jnp.maximum(tile_of(s) * per - 1, 0), col_block)

    def make_next(n_tiles):
        def nxt(bi, s, *_):
            return (bi, jnp.minimum((tile_of(s) + 1) * per, n_tiles * per - 1), col_block)
        return nxt

    return main, prev, make_next


def _gdn_prep_kernel(x_ref, xp_ref, xn_ref, ba_ref, cw_ref, ones_ref, exp_ref, alog_ref, dtb_ref,
                     q_ref, k_ref, v_ref, beta_ref, g_ref, *, n_lat_tiles):
    i = pl.program_id(1)
    y = _tile_conv(x_ref[...], xp_ref[...], xn_ref[...], cw_ref, i, pl.num_programs(1), n_lat_tiles)
    y = y * _sigmoid(y)
    q = y[:, :GDN_W]
    k = y[:, GDN_W:2 * GDN_W]
    v_ref[...] = y[:, 2 * GDN_W:].astype(v_ref.dtype)

    def head_norm(u):
        parts = jnp.concatenate(_split(u * u), axis=-1)
        ss = jnp.dot(parts, ones_ref[...], preferred_element_type=F32)
        return u * lax.rsqrt(ss + RMS_EPS)

    q_ref[...] = (head_norm(q) * (GDN_DK ** -0.5)).astype(q_ref.dtype)
    k_ref[...] = head_norm(k).astype(k_ref.dtype)

    ba = ba_ref[...]
    a = ba + dtb_ref[...]
    softplus = jnp.maximum(a, 0.0) + jnp.log1p(jnp.exp(-jnp.abs(a)))
    lane = lax.broadcasted_iota(jnp.int32, ba.shape, 1)
    compact = jnp.where(lane < 2 * GDN_HEADS, _sigmoid(ba), -jnp.exp(alog_ref[...]) * softplus)
    wide = jnp.dot(jnp.concatenate(_split(compact), axis=-1), exp_ref[...], preferred_element_type=F32)
    half = 2 * GDN_W
    beta_ref[...] = wide[:, :half].astype(beta_ref.dtype)
    g_ref[...] = wide[:, half:]


def _gdn_prep(p, conv_w, a_log, dt_bias, n_lat):
    b, s, _ = p.shape
    n_tiles = s // SEQ_TILE
    main, prev, make_next = _halo_specs(3 * GDN_W, 0, lambda t: t)
    ones_n = jnp.concatenate([_head_block_ones(GDN_HEADS, GDN_DK)] * N_SPLIT, axis=0)
    expand = np.zeros((128, 4 * GDN_W), np.float32)
    for kind in range(2):
        for d in range(2):
            for h in range(GDN_HEADS):
                c0 = kind * 2 * GDN_W + d * GDN_W + h * GDN_DV
                expand[kind * 2 * GDN_HEADS + d * GDN_HEADS + h, c0:c0 + GDN_DV] = 1.0
    expand_n = jnp.asarray(np.concatenate([expand] * N_SPLIT, axis=0), BF16)
    lanes = jnp.zeros((1, 128), F32)
    alog_e = lanes.at[0, 2 * GDN_HEADS:4 * GDN_HEADS].set(a_log.astype(F32).reshape(-1))
    dtb_e = lanes.at[0, 2 * GDN_HEADS:4 * GDN_HEADS].set(dt_bias.astype(F32).reshape(-1))
    tok = lambda w: pl.BlockSpec((None, SEQ_TILE, w), lambda bi, i: (bi, i, 0))
    return pl.pallas_call(
        functools.partial(_gdn_prep_kernel, n_lat_tiles=n_lat // SEQ_TILE),
        grid=(b, n_tiles),
        in_specs=[pl.BlockSpec((None, SEQ_TILE, 3 * GDN_W), main),
                  pl.BlockSpec((None, HALO, 3 * GDN_W), prev),
                  pl.BlockSpec((None, HALO, 3 * GDN_W), make_next(n_tiles)),
                  pl.BlockSpec((None, SEQ_TILE, 128), lambda bi, i: (bi, i, P_GDN_BA // 128)),
                  _full(conv_w), _full(ones_n), _full(expand_n), _full(alog_e), _full(dtb_e)],
        out_specs=[tok(GDN_W), tok(GDN_W), tok(GDN_W), tok(2 * GDN_W), tok(2 * GDN_W)],
        out_shape=[jax.ShapeDtypeStruct((b, s, GDN_W), BF16)] * 3 + [jax.ShapeDtypeStruct((b, s, 2 * GDN_W), BF16),
                                                                    jax.ShapeDtypeStruct((b, s, 2 * GDN_W), F32)],
        compiler_params=_cparams(("parallel", "parallel")),
        name="gdn_prep",
    )(p, p, p, p, conv_w, ones_n, expand_n, alog_e, dtb_e)


GDN_PAIR = 2 * GDN_DK


def _gdn_masks():
    c, w = GDN_CHUNK, GDN_W
    r2, c2 = np.arange(GDN_PAIR)[:, None], np.arange(GDN_PAIR)[None, :]
    bd = ((r2 // c) == (c2 // c)).astype(np.float32)
    i = np.arange(c)[:, None]
    j = (np.arange(w) % c)[None, :]
    level = np.zeros((c, w), np.int32)
    for bit in range(6):
        level += ((i ^ j) >= (1 << bit)).astype(np.int32)
    lvl = np.stack([(level == m).astype(np.float32) for m in range(7)])
    dirm = np.stack([np.stack([(j <= i), (j < i)]), np.stack([(j >= i), (j > i)])]).astype(np.float32)
    tj = (np.arange(N_SPLIT * c) % c)[None, :]
    tri = np.stack([(tj <= i), (tj >= i)]).astype(np.float32)
    return jnp.asarray(bd, BF16), jnp.asarray(lvl), jnp.asarray(dirm), jnp.asarray(tri, BF16)


def _heads_mm(x, y, bd, transpose_rhs=False):
    xb = x.astype(BF16)
    yb = y.astype(BF16)
    outs = []
    for pair in range(GDN_W // GDN_PAIR):
        ls = slice(pair * GDN_PAIR, (pair + 1) * GDN_PAIR)
        w = jnp.concatenate([yb[:, ls], yb[:, ls]], axis=0) * bd
        dims = (((1,), (1,)), ((), ())) if transpose_rhs else (((1,), (0,)), ((), ()))
        outs.append(lax.dot_general(xb[:, ls], w, dims, preferred_element_type=F32))
    return jnp.concatenate(outs, axis=1)


def _gdn_intra(probs, bd, lvl_ref, dirm_ref, tri_ref):
    c = GDN_CHUNK
    n = len(probs)
    eye = lvl_ref[0]
    gc, g_last, decay, gram = [], [], [], []
    for q, k, v, beta, g, rev in probs:
        d = 1 if rev else 0
        gcp = jnp.dot(tri_ref[d], jnp.concatenate(_split(g), axis=0), preferred_element_type=F32)
        gc.append(gcp)
        g_last.append(gcp[0:1, :] if rev else gcp[c - 1:c, :])
        gc_row = jnp.sum(gcp * eye, axis=0, keepdims=True)
        decay.append(dirm_ref[d, 0] * jnp.exp(jnp.minimum(gcp - gc_row, 0.0)))
        gram.append(_heads_mm(jnp.concatenate([k, q], axis=0), k, bd, transpose_rhs=True))
    lower = [dirm_ref[1 if p[5] else 0, 1] * p[3] * gram[x][:c] * decay[x] for x, p in enumerate(probs)]
    a_intra = [gram[x][c:] * decay[x] for x in range(n)]
    t = [eye - lower[x] * lvl_ref[1] for x in range(n)]
    for lev in range(2, 7):
        y = [_heads_mm(t[x], lower[x] * lvl_ref[lev], bd) for x in range(n)]
        z = [_heads_mm(y[x], t[x], bd) for x in range(n)]
        t = [t[x] - z[x] for x in range(n)]
    e_gc = [jnp.exp(gc[x]) for x in range(n)]
    u = [_heads_mm(t[x], p[2] * p[3], bd) for x, p in enumerate(probs)]
    w = [_heads_mm(t[x], p[1] * p[3] * e_gc[x], bd) for x, p in enumerate(probs)]
    wq = [jnp.concatenate([w[x], p[0] * e_gc[x]], axis=0).astype(BF16) for x, p in enumerate(probs)]
    k_dec = [(p[1] * jnp.exp(g_last[x] - gc[x])).astype(BF16) for x, p in enumerate(probs)]
    g_tot = [jnp.exp(g_last[x]) for x in range(n)]
    return u, wq, k_dec, a_intra, g_tot


def _gdn_state_steps(items, bd):
    c = GDN_CHUNK
    pairs = [slice(p * GDN_PAIR, (p + 1) * GDN_PAIR) for p in range(GDN_W // GDN_PAIR)]
    bdf = bd.astype(F32)
    ws_qs = [jnp.concatenate([jnp.dot(wq[:, ls], s_ref[p].astype(BF16), preferred_element_type=F32)
                              for p, ls in enumerate(pairs)], axis=1)
             for _, wq, _, _, _, s_ref in items]
    v_new = [it[0] - ws[:c] for it, ws in zip(items, ws_qs)]
    outs = [ws[c:] + _heads_mm(it[3], vn, bd) for it, ws, vn in zip(items, ws_qs, v_new)]
    for (_, _, k_dec, _, g_tot, s_ref), vn in zip(items, v_new):
        vb = vn.astype(BF16)
        for p, ls in enumerate(pairs):
            kv = lax.dot_general(k_dec[:, ls], vb[:, ls], (((0,), (0,)), ((), ())), preferred_element_type=F32)
            s_ref[p] = s_ref[p] * g_tot[:, ls] + kv * bdf
    return outs


def _gdn_scan_body(qf, kf, vf, bf, gf, qb, kb, vb, bb, gb, bd_ref, lvl_ref, dirm_ref, tri_ref,
                   of_ref, ob_ref, sf_ref, sb_ref):
    bd = bd_ref[...]
    n = SEQ_TILE // GDN_CHUNK
    probs, rows = [], []
    for c in range(n):
        rf = slice(c * GDN_CHUNK, (c + 1) * GDN_CHUNK)
        rb = slice((n - 1 - c) * GDN_CHUNK, (n - c) * GDN_CHUNK)
        probs.append(tuple(r[rf, :].astype(F32) for r in (qf, kf, vf, bf, gf)) + (False,))
        probs.append(tuple(r[rb, :].astype(F32) for r in (qb, kb, vb, bb, gb)) + (True,))
        rows += [rf, rb]
    u, wq, k_dec, a_intra, g_tot = _gdn_intra(probs, bd, lvl_ref, dirm_ref, tri_ref)
    for c in range(n):
        xs = (2 * c, 2 * c + 1)
        outs = _gdn_state_steps([(u[x], wq[x], k_dec[x], a_intra[x], g_tot[x], sb_ref if probs[x][5] else sf_ref)
                                 for x in xs], bd)
        for x, o in zip(xs, outs):
            (ob_ref if probs[x][5] else of_ref)[rows[x], :] = o


def _seq_tile_maps(n_lat_tiles, n_tiles):
    fwd = lambda s: lax.rem(s + n_lat_tiles, n_tiles)
    bwd = lambda s: n_tiles - 1 - s
    return fwd, bwd


def _gdn_scan_specs(q, k, v, beta, g, n_lat):
    b, s, _ = q.shape
    n_tiles = s // SEQ_TILE
    fwd, bwd = _seq_tile_maps(n_lat // SEQ_TILE, n_tiles)
    spec = lambda tile_of, col: pl.BlockSpec((None, SEQ_TILE, GDN_W), lambda bi, t: (bi, tile_of(t), col))
    masks = _gdn_masks()
    in_specs = [spec(fwd, 0)] * 5 + [spec(bwd, 0)] * 3 + [spec(bwd, 1)] * 2 + [_full(m) for m in masks]
    assert len(in_specs) == N_GDN_IN
    return (in_specs, [q, k, v, beta, g, q, k, v, beta, g, *masks], [spec(fwd, 0), spec(bwd, 0)],
            [jax.ShapeDtypeStruct((b, s, GDN_W), F32)] * 2,
            [pltpu.VMEM((GDN_W // GDN_PAIR, GDN_PAIR, GDN_PAIR), F32)] * 2)


def _lru_tile_prep(x_ref, xp_ref, xn_ref, cw_ref, cb_ref, wg_ref, bg_ref, nla_ref, a_ref, b_ref,
                   tile_idx, n_tiles, n_lat_tiles):
    xb = _tile_conv(x_ref[...], xp_ref[...], xn_ref[...], cw_ref, tile_idx, n_tiles, n_lat_tiles) + cb_ref[...]
    gates = _sigmoid(jnp.dot(xb.astype(BF16), wg_ref[...], preferred_element_type=F32) + bg_ref[...])
    log_a = nla_ref[...] * gates[:, :LRU_W]
    a_ref[...] = jnp.exp(log_a)
    th = jnp.tanh(log_a)
    b_ref[...] = jnp.sqrt(-2.0 * th / (1.0 - th)) * gates[:, LRU_W:] * xb


def _scan_group(a, b, h, row, reverse):
    for d in (1, 2, 4):
        if reverse:
            keep = row < 8 - d
            shift = 8 - d
        else:
            keep = row >= d
            shift = d
        a_s = jnp.where(keep, pltpu.roll(a, shift, 0), 1.0)
        b_s = jnp.where(keep, pltpu.roll(b, shift, 0), 0.0)
        b = a * b_s + b
        a = a * a_s
    return a * h + b


def _lru_scan_body(xf, xfp, xfn, xb, xbp, xbn, cw_ref, cb_ref, wgf, bgf, nlaf, wgb, bgb, nlab,
                   hf_ref, hb_ref, af_ref, bf_ref, ab_ref, bb_ref, cf_ref, cbk_ref, *, n_lat_tiles):
    s = pl.program_id(1)
    n_tiles = pl.num_programs(1)
    t_f, t_b = (m(s) for m in _seq_tile_maps(n_lat_tiles, n_tiles))
    _lru_tile_prep(xf, xfp, xfn, cw_ref, cb_ref, wgf, bgf, nlaf, af_ref, bf_ref, t_f, n_tiles, n_lat_tiles)
    _lru_tile_prep(xb, xbp, xbn, cw_ref, cb_ref, wgb, bgb, nlab, ab_ref, bb_ref, t_b, n_tiles, n_lat_tiles)

    n_groups = SEQ_TILE // 8
    row = lax.broadcasted_iota(jnp.int32, (8, LRU_W), 0)

    h_f, h_b = cf_ref[...], cbk_ref[...]
    for gi in range(n_groups):
        rf = slice(gi * 8, gi * 8 + 8)
        rb = slice((n_groups - 1 - gi) * 8, (n_groups - gi) * 8)
        out_f = _scan_group(af_ref[rf, :], bf_ref[rf, :], h_f, row, False)
        out_b = _scan_group(ab_ref[rb, :], bb_ref[rb, :], h_b, row, True)
        hf_ref[rf, :] = out_f
        hb_ref[rb, :] = out_b
        h_f = jnp.broadcast_to(out_f[7:8, :], (8, LRU_W))
        h_b = jnp.broadcast_to(out_b[0:1, :], (8, LRU_W))
    cf_ref[...] = h_f
    cbk_ref[...] = h_b


N_LRU_IN, N_GDN_IN, N_LRU_SCRATCH = 14, 14, 6


def _seq_mixers_kernel(*refs, n_lat_tiles):
    lru_in = refs[:N_LRU_IN]
    gdn_in = refs[N_LRU_IN:N_LRU_IN + N_GDN_IN]
    hf_ref, hb_ref, of_ref, ob_ref = refs[N_LRU_IN + N_GDN_IN:N_LRU_IN + N_GDN_IN + 4]
    scratch = refs[N_LRU_IN + N_GDN_IN + 4:]
    lru_scratch, gdn_scratch = scratch[:N_LRU_SCRATCH], scratch[N_LRU_SCRATCH:]

    @pl.when(pl.program_id(1) == 0)
    def _():
        for ref in lru_scratch[4:] + gdn_scratch:
            ref[...] = jnp.zeros(ref.shape, ref.dtype)

    _lru_scan_body(*lru_in, hf_ref, hb_ref, *lru_scratch, n_lat_tiles=n_lat_tiles)
    _gdn_scan_body(*gdn_in, of_ref, ob_ref, *gdn_scratch)


def _lru_scan_specs(p, conv_w, conv_b, w_r, b_r, w_i, b_i, lam, n_lat):
    b, s, _ = p.shape
    n_tiles = s // SEQ_TILE
    nlt = n_lat // SEQ_TILE
    fwd, bwd = _seq_tile_maps(nlt, n_tiles)
    col = P_LRU_X // LRU_W
    specs = []
    for tile_of in (fwd, bwd):
        main, prev, make_next = _halo_specs(LRU_W, col, tile_of)
        specs += [pl.BlockSpec((None, SEQ_TILE, LRU_W), main), pl.BlockSpec((None, HALO, LRU_W), prev),
                  pl.BlockSpec((None, HALO, LRU_W), make_next(n_tiles))]

    def blockdiag(w):
        return jax.scipy.linalg.block_diag(*[w[n] for n in range(LRU_BLOCKS)])

    dir_args = []
    for d in range(2):
        wg = jnp.concatenate([blockdiag(w_r[d]), blockdiag(w_i[d])], axis=1).astype(BF16)
        bg = jnp.concatenate([b_r[d], b_i[d]]).astype(F32).reshape(1, 2 * LRU_W)
        nla = (-LRU_C * jax.nn.softplus(-lam[d].astype(F32))).reshape(1, LRU_W)
        dir_args += [wg, bg, nla]
    cb2 = conv_b.reshape(1, LRU_W)
    out_spec = lambda tile_of: pl.BlockSpec((None, SEQ_TILE, LRU_W), lambda bi, t: (bi, tile_of(t), 0))
    in_specs = specs + [_full(conv_w), _full(cb2)] + [_full(a) for a in dir_args]
    assert len(in_specs) == N_LRU_IN
    return (in_specs, [p, p, p, p, p, p, conv_w, cb2, *dir_args], [out_spec(fwd), out_spec(bwd)],
            [jax.ShapeDtypeStruct((b, s, LRU_W), F32)] * 2,
            [pltpu.VMEM((SEQ_TILE, LRU_W), F32)] * 4 + [pltpu.VMEM((8, LRU_W), F32)] * 2)


def _seq_mixers(lru_parts, gdn_parts, bsz, n_tiles, n_lat):
    in_specs, args, out_specs, out_shape, scratch = ([*a, *b] for a, b in zip(lru_parts, gdn_parts))
    return pl.pallas_call(
        functools.partial(_seq_mixers_kernel, n_lat_tiles=n_lat // SEQ_TILE),
        grid=(bsz, n_tiles),
        in_specs=in_specs,
        out_specs=out_specs,
        out_shape=out_shape,
        scratch_shapes=scratch,
        compiler_params=_cparams(("parallel", "arbitrary")),
        name="seq_mixers",
    )(*args)


def _merge_kernel(x_ref, mc_ref, ml_ref, of_ref, ob_ref, z_ref, hf_ref, hb_ref, y_ref, uc_ref, ud_ref,
                  t0_ref, t1_ref, t2_ref, t3_ref, gn_ref, ones_ref, bg_ref, wb_ref, wo_ref, o_ref, *, n_lat):
    o = of_ref[...] + ob_ref[...]
    ms = jnp.dot(jnp.concatenate(_split(o * o), axis=-1), ones_ref[...], preferred_element_type=F32) * (1.0 / GDN_DV)
    z = z_ref[...]
    ua = (o * lax.rsqrt(ms + RMS_EPS) * gn_ref[...] * (z * _sigmoid(z))).astype(BF16)
    y = y_ref[...]
    gelu = 0.5 * y * (1.0 + jnp.tanh(0.7978845608028654 * (y + 0.044715 * (y * y * y))))
    ub = ((hf_ref[...] + hb_ref[...]) * gelu).astype(BF16)

    merged = None
    for n, (u, t_ref) in enumerate(((ua, t0_ref), (ub, t1_ref), (uc_ref[...], t2_ref), (ud_ref[...], t3_ref))):
        gate = _sigmoid(t_ref[...] + bg_ref[n:n + 1, :].astype(BF16))
        term = gate.astype(F32) * jnp.dot(u, wb_ref[n], preferred_element_type=F32)
        merged = term if merged is None else merged + term
    out = jnp.dot(merged.astype(BF16), wo_ref[...], preferred_element_type=F32)
    g1 = _row_select(x_ref.shape[0], pl.program_id(1), n_lat, mc_ref[2:3, :], ml_ref[2:3, :])
    o_ref[...] = x_ref[...] + g1 * out


def _merge(xs, mod_c, mod_l, gdn_out, lru_out, uc, ud, p32, p16, gdn_norm_g, b_gate, w_branch, w_out, layer, n_lat,
           with_ctx):
    b, s, d = xs.shape
    rows = s if with_ctx else n_lat
    tm = _pick_tile(rows, 544)
    gate_blk = P_GATES // d
    tok = lambda w: pl.BlockSpec((None, tm, w), lambda bi, i: (bi, i, 0))
    ones_n = jnp.concatenate([_head_block_ones(GDN_HEADS, GDN_DV)] * N_SPLIT, axis=0)
    gn = jnp.tile(gdn_norm_g.astype(F32), GDN_HEADS).reshape(1, GDN_W)
    pcol = lambda c: pl.BlockSpec((None, tm, BRANCH_W), lambda bi, i: (bi, i, c))
    in_specs = [tok(d), pl.BlockSpec((8, d), lambda bi, i: (0, 0)), pl.BlockSpec((None, 8, d), lambda bi, i: (bi, 0, 0))]
    in_specs += [tok(BRANCH_W), tok(BRANCH_W), pcol(P_GDN_Z // GDN_W), tok(BRANCH_W), tok(BRANCH_W),
                 pcol(P_LRU_Y // LRU_W), tok(BRANCH_W), tok(BRANCH_W)]
    in_specs += [pl.BlockSpec((None, tm, d), functools.partial(lambda bi, i, n: (bi, i, gate_blk + n), n=n))
                 for n in range(N_BRANCH)]
    in_specs += [_full(gn), _full(ones_n), _full(b_gate),
                 pl.BlockSpec((None,) + w_branch.shape[1:], lambda bi, i: (layer, 0, 0, 0)),
                 pl.BlockSpec((None,) + w_out.shape[1:], lambda bi, i: (layer, 0, 0))]
    return pl.pallas_call(
        functools.partial(_merge_kernel, n_lat=n_lat),
        grid=(b, rows // tm),
        in_specs=in_specs,
        out_specs=tok(d),
        out_shape=jax.ShapeDtypeStruct((b, rows, d), F32),
        input_output_aliases={0: 0} if with_ctx else {},
        compiler_params=_cparams(("parallel", "parallel")),
        name="merge",
    )(xs, mod_c, mod_l, gdn_out[0], gdn_out[1], p32, lru_out[0], lru_out[1], p32, uc, ud, p16, p16, p16, p16,
      gn, ones_n, b_gate, w_branch, w_out)


def _mlp_kernel(x_ref, mc_ref, ml_ref, gn_ref, gf_ref, w1_ref, w2_ref, o_ref, h_ref, acc_ref, *, n_lat, final_norm):
    f = pl.program_id(2)
    tm = x_ref.shape[0]
    i = pl.program_id(1)

    @pl.when(f == 0)
    def _():
        _norm_modulate(x_ref, h_ref, gn_ref, mc_ref, ml_ref, 3, i, n_lat)
        acc_ref[...] = jnp.zeros(acc_ref.shape, F32)

    a = jnp.maximum(jnp.dot(h_ref[...], w1_ref[...], preferred_element_type=F32), 0.0)
    acc_ref[...] += jnp.dot((a * a).astype(BF16), w2_ref[...], preferred_element_type=F32)

    @pl.when(f == pl.num_programs(2) - 1)
    def _():
        g2 = _row_select(tm, i, n_lat, mc_ref[5:6, :], ml_ref[5:6, :])
        y = x_ref[...] + g2 * acc_ref[...]
        if final_norm:
            y = y * lax.rsqrt(jnp.mean(y * y, axis=-1, keepdims=True) + RMS_EPS) * gf_ref[...]
        o_ref[...] = y


def _mlp(xs, mod_c, mod_l, gain, w1, w2, layer, final_gain, n_lat, final_norm):
    b, rows, d = xs.shape
    dff = w1.shape[-1]
    tm = _pick_tile(rows, 1088)
    tf = 1024
    row = pl.BlockSpec((1, d), lambda bi, i, f: (0, 0))
    return pl.pallas_call(
        functools.partial(_mlp_kernel, n_lat=n_lat, final_norm=final_norm),
        grid=(b, rows // tm, dff // tf),
        in_specs=[pl.BlockSpec((None, tm, d), lambda bi, i, f: (bi, i, 0)),
                  pl.BlockSpec((8, d), lambda bi, i, f: (0, 0)),
                  pl.BlockSpec((None, 8, d), lambda bi, i, f: (bi, 0, 0)),
                  row, row,
                  pl.BlockSpec((None, d, tf), lambda bi, i, f: (layer, 0, f)),
                  pl.BlockSpec((None, tf, d), lambda bi, i, f: (layer, f, 0))],
        out_specs=pl.BlockSpec((None, tm, d), lambda bi, i, f: (bi, i, 0)),
        out_shape=jax.ShapeDtypeStruct((b, rows, d), F32),
        scratch_shapes=[pltpu.VMEM((tm, d), BF16), pltpu.VMEM((tm, d), F32)],
        compiler_params=_cparams(("parallel", "parallel", "arbitrary")),
        name="mlp",
    )(xs, mod_c, mod_l, gain, final_gain, w1, w2)


def kernel(x, c, ctx, c_ctx, mod_w, mod_b, norm1_g, norm2_g, w_in, b_gate, gdn_conv_w, gdn_a_log, gdn_dt_bias,
           gdn_norm_g, lru_conv_w, lru_conv_b, lru_w_r, lru_b_r, lru_w_i, lru_b_i, lru_lambda, mla_q_norm_g,
           mla_w_uq, mla_kv_norm_g, mla_w_ukv, na_rpb, w_branch, w_out, mlp_w1, mlp_w2, final_norm_g):
    bsz, n_tok, d = x.shape
    n_ctx = ctx.shape[1]
    depth = w_in.shape[0]
    assert n_ctx % SEQ_TILE == 0 and n_tok % SEQ_TILE == 0 and n_tok % GRID_W == 0
    na_meta, na_rel_rows, na_cols = _na_geometry(n_tok // GRID_W)
    na_meta = jnp.asarray(na_meta)
    cos, sin = _rope_tables(n_tok, n_ctx)

    n_rows = -(-(bsz + 1) // 8) * 8
    cc = jnp.zeros((n_rows, d), F32).at[:bsz].set(c).at[bsz].set(c_ctx)
    final_gain = final_norm_g.reshape(1, d)

    w_in_all = _arrange_w_in(w_in)
    wb_all = w_branch.astype(BF16)
    wo_all = w_out.astype(BF16)
    w1_all = mlp_w1.astype(BF16)
    w2_all = mlp_w2.astype(BF16)

    xs = jnp.concatenate([x, ctx], axis=1)
    for l in range(depth):
        need_ctx = l < depth - 1
        mod = _modulation(cc, mod_w, mod_b[l], l).reshape(n_rows, N_MOD, d)
        pad = jnp.zeros((8 - N_MOD, d), F32)
        mod_c = jnp.concatenate([mod[bsz], pad], axis=0)
        mod_l = jnp.concatenate([mod[:bsz], jnp.broadcast_to(pad, (bsz, 8 - N_MOD, d))], axis=1)

        wq, wk, wv, place = _arrange_mla(mla_w_uq[l], mla_w_ukv[l])
        gq = mla_q_norm_g[l].reshape(1, -1)
        gkv = mla_kv_norm_g[l].reshape(1, -1)
        g1n = norm1_g[l].reshape(1, d)
        g2n = norm2_g[l].reshape(1, d)

        p32, p16 = _inproj(xs, mod_c, mod_l, g1n, w_in_all, l, n_tok)

        gq_, gk_, gv_, gbeta, gg = _gdn_prep(p32, gdn_conv_w[l], gdn_a_log[l], gdn_dt_bias[l], n_tok)
        h_f, h_b, o_f, o_b = _seq_mixers(
            _lru_scan_specs(p32, lru_conv_w[l], lru_conv_b[l], lru_w_r[l], lru_b_r[l], lru_w_i[l], lru_b_i[l],
                            lru_lambda[l], n_tok),
            _gdn_scan_specs(gq_, gk_, gv_, gbeta, gg, n_tok), bsz, (n_tok + n_ctx) // SEQ_TILE, n_tok)

        mq, mk, mv = _mla_prep(p32, p16, cos, sin, gq, gkv, wq, wk, wv, place)
        uc = _mla_flash(mq, mk, mv, n_tok)
        if need_ctx:
            uc = _mla_flash(mq, mk, mv, n_tok, prev=uc)
        ud = _na_attention(p16, na_meta, _na_bias_table(na_rpb[l], na_rel_rows, na_cols), n_tok, need_ctx)

        xs = _merge(xs, mod_c, mod_l, (o_f, o_b), (h_f, h_b), uc, ud, p32, p16, gdn_norm_g[l], b_gate[l],
                    wb_all, wo_all, l, n_tok, need_ctx)
        xs = _mlp(xs, mod_c, mod_l, g2n, w1_all, w2_all, l, final_gain, n_tok, l == depth - 1)
    return xs
```

```python
import functools

import jax
import jax.numpy as jnp
import numpy as np
from jax import lax
from jax.experimental import pallas as pl
from jax.experimental.pallas import tpu as pltpu

F32 = jnp.float32
BF16 = jnp.bfloat16

GRID_W = 64
N_MOD = 6
RMS_EPS = 1e-6
GDN_HEADS = 4
GDN_DK = 64
GDN_DV = 64
GDN_CHUNK = 64
GDN_W = GDN_HEADS * GDN_DV
LRU_W = 256
LRU_BLOCKS = 4
LRU_C = 8.0
MLA_HEADS = 4
MLA_Q_RANK = 256
MLA_KV_RANK = 128
MLA_NOPE = 64
MLA_ROPE = 32
MLA_V = 64
MLA_SLOT = 128
ROPE_BASE = 10000.0
LOG2_E = 1.4426950408889634
NA_HEADS = 4
NA_DH = 64
NA_W = NA_HEADS * NA_DH
NA_WIN_ROWS = 8
NA_WIN_COLS = 16
N_BRANCH = 4
BRANCH_W = 256

SEQ_TILE = 256
HALO = 8

_REF_COLS = {}
_off = 0
for _name, _w in (('gdn_qkv', 3 * GDN_W), ('gdn_z', GDN_W), ('gdn_beta', 2 * GDN_HEADS), ('gdn_a', 2 * GDN_HEADS),
                  ('lru_x', LRU_W), ('lru_y', LRU_W), ('mla_q', MLA_Q_RANK), ('mla_kv', MLA_KV_RANK),
                  ('mla_kr', MLA_ROPE), ('na_qkv', 3 * NA_W)):
    _REF_COLS[_name] = (_off, _w)
    _off += _w
N_MIX_COLS = _off

P_GDN_QKV = 0
P_GDN_Z = 768
P_LRU_X = 1024
P_LRU_Y = 1280
P_MLA_KV = 1536
P_MLA_KR = 1664
P_GDN_BA = 1792
P32_COLS = 2048
P_MLA_Q = 0
P_NA_QKV = 256
P_GATES = 1024
P16_COLS = P_GATES + N_BRANCH * 1024
PROJ_TILE = 1024

V7X_VMEM_BYTES = 64 * 1024 * 1024
VMEM_LIMIT = V7X_VMEM_BYTES - 12 * 1024 * 1024


def _cparams(sem):
    return pltpu.CompilerParams(dimension_semantics=sem, vmem_limit_bytes=VMEM_LIMIT)


def _pick_tile(n, cap):
    best = 8
    for t in range(8, min(n, cap) + 1, 8):
        if n % t == 0:
            best = t
    return best


def _full(a):
    return pl.BlockSpec(a.shape, lambda *_: (0,) * a.ndim)


N_SPLIT = 2


def _split(x):
    hi = x.astype(BF16)
    lo = (x - hi.astype(F32)).astype(BF16)
    return hi, lo


def _sigmoid(x):
    return 0.5 * jnp.tanh(0.5 * x) + 0.5


def _arrange_w_in(w_in):
    pieces, pos = [], 0

    def put(dst, block):
        nonlocal pos
        if dst > pos:
            pieces.append(jnp.zeros(w_in.shape[:-1] + (dst - pos,), w_in.dtype))
        pieces.append(block)
        pos = dst + block.shape[-1]

    ref = lambda name: w_in[..., _REF_COLS[name][0]:_REF_COLS[name][0] + _REF_COLS[name][1]]
    for name, dst in (('gdn_qkv', P_GDN_QKV), ('gdn_z', P_GDN_Z), ('lru_x', P_LRU_X), ('lru_y', P_LRU_Y),
                      ('mla_kv', P_MLA_KV), ('mla_kr', P_MLA_KR), ('gdn_beta', P_GDN_BA),
                      ('gdn_a', P_GDN_BA + 2 * GDN_HEADS)):
        put(dst, ref(name))
    put(P32_COLS + P_MLA_Q, ref('mla_q'))
    put(P32_COLS + P_NA_QKV, ref('na_qkv'))
    put(P32_COLS + P_GATES, w_in[..., N_MIX_COLS:])
    assert pos == P32_COLS + P16_COLS
    return jnp.concatenate(pieces, axis=-1).astype(BF16)


def _rope_perm():
    q = MLA_ROPE // 4
    src = np.zeros(MLA_ROPE, np.int32)
    sign = np.zeros(MLA_ROPE, np.float32)
    for base in (0, 2 * q):
        for d in range(q):
            src[base + d] = base + d + q
            sign[base + d] = -1.0
            src[base + q + d] = base + d
            sign[base + q + d] = 1.0
    return src, sign


def _arrange_mla(w_uq, w_ukv):
    src, sign = _rope_perm()
    hq = MLA_NOPE + MLA_ROPE
    wq = jnp.zeros((MLA_Q_RANK, 2 * MLA_HEADS * MLA_SLOT), F32)
    wk = jnp.zeros((MLA_KV_RANK, MLA_HEADS * MLA_SLOT), F32)
    wv = jnp.zeros((MLA_KV_RANK, MLA_HEADS * MLA_V), F32)
    place = np.zeros((2 * MLA_SLOT, 2 * MLA_HEADS * MLA_SLOT), np.float32)
    rot_off = MLA_HEADS * MLA_SLOT
    for h in range(MLA_HEADS):
        nope = w_uq[:, h * hq:h * hq + MLA_NOPE]
        pe = w_uq[:, h * hq + MLA_NOPE:(h + 1) * hq]
        s = h * MLA_SLOT
        wq = wq.at[:, s:s + MLA_NOPE].set(nope)
        wq = wq.at[:, s + MLA_NOPE:s + MLA_NOPE + MLA_ROPE].set(pe)
        wq = wq.at[:, rot_off + s + MLA_NOPE:rot_off + s + MLA_NOPE + MLA_ROPE].set(pe[:, src] * sign)
        wk = wk.at[:, s:s + MLA_NOPE].set(w_ukv[:, h * 128:h * 128 + MLA_NOPE])
        wv = wv.at[:, h * MLA_V:(h + 1) * MLA_V].set(w_ukv[:, h * 128 + MLA_NOPE:(h + 1) * 128])
        for d in range(MLA_ROPE):
            for half in (0, MLA_SLOT):
                place[half + d, s + MLA_NOPE + d] = 1.0
                place[half + src[d], rot_off + s + MLA_NOPE + d] = sign[d]
    return wq.astype(BF16), wk.astype(BF16), wv.astype(BF16), jnp.asarray(place, BF16)


def _rope_tables(n_tok, n_ctx):
    cos = np.ones((n_tok + n_ctx, MLA_SLOT), np.float32)
    sin = np.zeros((n_tok + n_ctx, MLA_SLOT), np.float32)
    t = np.arange(n_tok)
    row = (t // GRID_W).astype(np.float32)
    col = (t % GRID_W).astype(np.float32)
    n_freq = MLA_ROPE // 4
    inv = (ROPE_BASE ** (-np.arange(n_freq, dtype=np.float32) / n_freq)).astype(np.float32)
    ar = row[:, None] * inv
    ac = col[:, None] * inv
    ang = np.concatenate([ar, ar, ac, ac], axis=-1).astype(np.float32)
    cos[:n_tok, MLA_NOPE:MLA_NOPE + MLA_ROPE] = np.cos(ang)
    sin[:n_tok, MLA_NOPE:MLA_NOPE + MLA_ROPE] = np.sin(ang)
    return jnp.asarray(cos), jnp.asarray(sin)


def _head_block_ones(n_heads, width):
    m = np.kron(np.eye(n_heads, dtype=np.float32), np.ones((width, width), np.float32))
    return jnp.asarray(m, BF16)


def _mod_kernel(c_ref, w_ref, b_ref, o_ref):
    c = c_ref[...]
    s = c * _sigmoid(c)
    o_ref[...] = jnp.dot(s, w_ref[...], preferred_element_type=F32) + b_ref[...]


def _modulation(cc, mod_w, mod_b, layer):
    r, d = cc.shape
    n = mod_w.shape[-1]
    tn = 1024
    return pl.pallas_call(
        _mod_kernel,
        grid=(n // tn,),
        in_specs=[pl.BlockSpec((r, d), lambda j: (0, 0)),
                  pl.BlockSpec((None, d, tn), lambda j: (layer, 0, j)),
                  pl.BlockSpec((1, tn), lambda j: (0, j))],
        out_specs=pl.BlockSpec((r, tn), lambda j: (0, j)),
        out_shape=jax.ShapeDtypeStruct((r, n), F32),
        compiler_params=_cparams(("arbitrary",)),
        name="modulation",
    )(cc, mod_w, mod_b.reshape(1, n))


def _row_select(tile_rows, tile_idx, n_lat, ctx_vec, lat_vec):
    row = tile_idx * tile_rows + lax.broadcasted_iota(jnp.int32, (tile_rows, 1), 0)
    return jnp.where(row < n_lat, lat_vec, ctx_vec)


def _norm_modulate(x_ref, h_ref, gain_ref, mc_ref, ml_ref, shift_row, tile_idx, n_lat):
    tm = x_ref.shape[0]
    x = x_ref[...]
    xn = x * lax.rsqrt(jnp.mean(x * x, axis=-1, keepdims=True) + RMS_EPS)
    sh_l, sh_c = ml_ref[shift_row:shift_row + 1, :], mc_ref[shift_row:shift_row + 1, :]
    amp_l = gain_ref[...] * (1.0 + ml_ref[shift_row + 1:shift_row + 2, :])
    amp_c = gain_ref[...] * (1.0 + mc_ref[shift_row + 1:shift_row + 2, :])
    all_latent = (tile_idx + 1) * tm <= n_lat

    @pl.when(all_latent)
    def _():
        h_ref[...] = (xn * amp_l + sh_l).astype(h_ref.dtype)

    @pl.when(jnp.logical_not(all_latent))
    def _():
        amp = _row_select(tm, tile_idx, n_lat, amp_c, amp_l)
        shift = _row_select(tm, tile_idx, n_lat, sh_c, sh_l)
        h_ref[...] = (xn * amp + shift).astype(h_ref.dtype)


def _inproj_first_kernel(x_ref, mc_ref, ml_ref, g_ref, w_ref, o_ref, h_ref, *, n_lat):
    @pl.when(pl.program_id(2) == 0)
    def _():
        _norm_modulate(x_ref, h_ref, g_ref, mc_ref, ml_ref, 0, pl.program_id(1), n_lat)

    o_ref[...] = jnp.dot(h_ref[...], w_ref[...], preferred_element_type=F32).astype(o_ref.dtype)


def _inproj_second_kernel(h_ref, w_ref, o_ref):
    o_ref[...] = jnp.dot(h_ref[...], w_ref[...], preferred_element_type=F32).astype(o_ref.dtype)


def _inproj(xs, mod_c, mod_l, gain, w, layer, n_lat):
    b, s, d = xs.shape
    tm = _pick_tile(s, 1088)
    tn = PROJ_TILE
    n32 = P32_COLS // tn
    p16, h = pl.pallas_call(
        functools.partial(_inproj_first_kernel, n_lat=n_lat),
        grid=(b, s // tm, P16_COLS // tn),
        in_specs=[pl.BlockSpec((None, tm, d), lambda bi, i, j: (bi, i, 0)),
                  pl.BlockSpec((8, d), lambda bi, i, j: (0, 0)),
                  pl.BlockSpec((None, 8, d), lambda bi, i, j: (bi, 0, 0)),
                  pl.BlockSpec((1, d), lambda bi, i, j: (0, 0)),
                  pl.BlockSpec((None, d, tn), lambda bi, i, j: (layer, 0, j + n32))],
        out_specs=[pl.BlockSpec((None, tm, tn), lambda bi, i, j: (bi, i, j)),
                   pl.BlockSpec((None, tm, d), lambda bi, i, j: (bi, i, 0))],
        out_shape=[jax.ShapeDtypeStruct((b, s, P16_COLS), BF16), jax.ShapeDtypeStruct((b, s, d), BF16)],
        compiler_params=_cparams(("parallel", "parallel", "arbitrary")),
        name="inproj_bf16",
    )(xs, mod_c, mod_l, gain, w)
    p32 = pl.pallas_call(
        _inproj_second_kernel,
        grid=(b, s // tm, n32),
        in_specs=[pl.BlockSpec((None, tm, d), lambda bi, i, j: (bi, i, 0)),
                  pl.BlockSpec((None, d, tn), lambda bi, i, j: (layer, 0, j))],
        out_specs=pl.BlockSpec((None, tm, tn), lambda bi, i, j: (bi, i, j)),
        out_shape=jax.ShapeDtypeStruct((b, s, P32_COLS), F32),
        compiler_params=_cparams(("parallel", "parallel", "arbitrary")),
        name="inproj_f32",
    )(h, w)
    return p32, p16


def _mla_prep_kernel(ql_ref, kv_ref, cos_ref, sin_ref, gq_ref, gkv_ref, wq_ref, wk_ref, wv_ref, pl_ref,
                     q_ref, k_ref, v_ref, *, scale):
    nslot = MLA_HEADS * MLA_SLOT
    cos = jnp.concatenate([cos_ref[...]] * MLA_HEADS, axis=-1)
    sin = jnp.concatenate([sin_ref[...]] * MLA_HEADS, axis=-1)

    ql = ql_ref[...].astype(F32)
    qn = ql * lax.rsqrt(jnp.mean(ql * ql, axis=-1, keepdims=True) + RMS_EPS) * gq_ref[...]
    q2 = jnp.dot(qn.astype(BF16), wq_ref[...], preferred_element_type=F32)
    q = (q2[:, :nslot] * cos + q2[:, nslot:] * sin) * scale
    q_ref[...] = q.astype(BF16)

    kvkr = kv_ref[...]
    kvl = kvkr[:, :MLA_KV_RANK]
    kr = kvkr[:, MLA_KV_RANK:]
    kvn = (kvl * lax.rsqrt(jnp.mean(kvl * kvl, axis=-1, keepdims=True) + RMS_EPS) * gkv_ref[...]).astype(BF16)
    kn = jnp.dot(kvn, wk_ref[...], preferred_element_type=F32)
    v_ref[...] = jnp.dot(kvn, wv_ref[...], preferred_element_type=F32).astype(BF16)
    kr_hi = kr.astype(BF16)
    kr_lo = (kr - kr_hi.astype(F32)).astype(BF16)
    kr2 = jnp.dot(jnp.concatenate([kr_hi, kr_lo], axis=-1), pl_ref[...], preferred_element_type=F32)
    k_ref[...] = (kn + kr2[:, :nslot] * cos + kr2[:, nslot:] * sin).astype(BF16)


def _mla_prep(p32, p16, cos, sin, gq, gkv, wq, wk, wv, place):
    b, s, _ = p32.shape
    tm = _pick_tile(s, 544)
    nslot = MLA_HEADS * MLA_SLOT
    scale = (MLA_NOPE + MLA_ROPE) ** -0.5 * LOG2_E
    return pl.pallas_call(
        functools.partial(_mla_prep_kernel, scale=scale),
        grid=(b, s // tm),
        in_specs=[pl.BlockSpec((None, tm, MLA_Q_RANK), lambda bi, i: (bi, i, P_MLA_Q // MLA_Q_RANK)),
                  pl.BlockSpec((None, tm, 2 * MLA_SLOT), lambda bi, i: (bi, i, P_MLA_KV // (2 * MLA_SLOT))),
                  pl.BlockSpec((tm, MLA_SLOT), lambda bi, i: (i, 0)),
                  pl.BlockSpec((tm, MLA_SLOT), lambda bi, i: (i, 0)),
                  _full(gq), _full(gkv), _full(wq), _full(wk), _full(wv), _full(place)],
        out_specs=[pl.BlockSpec((None, tm, nslot), lambda bi, i: (bi, i, 0)),
                   pl.BlockSpec((None, tm, nslot), lambda bi, i: (bi, i, 0)),
                   pl.BlockSpec((None, tm, MLA_HEADS * MLA_V), lambda bi, i: (bi, i, 0))],
        out_shape=[jax.ShapeDtypeStruct((b, s, nslot), BF16),
                   jax.ShapeDtypeStruct((b, s, nslot), BF16),
                   jax.ShapeDtypeStruct((b, s, MLA_HEADS * MLA_V), BF16)],
        compiler_params=_cparams(("parallel", "parallel")),
        name="mla_prep",
    )(p16, p32, cos, sin, gq, gkv, wq, wk, wv, place)


def _flash_softmax(h, s, m_ref, l_ref, first):
    row_max = jnp.max(s, axis=-1, keepdims=True)
    m_new = jnp.broadcast_to(row_max, m_ref.shape[1:]) if first else jnp.maximum(m_ref[h], row_max)
    p = jnp.exp2(s - jnp.concatenate([m_new] * (s.shape[1] // 128), axis=-1))
    row_sum = jnp.sum(p, axis=-1, keepdims=True)
    if first:
        alpha = None
        l_ref[h] = jnp.broadcast_to(row_sum, l_ref.shape[1:])
    else:
        alpha = jnp.exp2(m_ref[h] - m_new)
        l_ref[h] = alpha * l_ref[h] + row_sum
    m_ref[h] = m_new
    return alpha, p.astype(BF16)


def _mla_flash_kernel(*refs, tk, aliased):
    if aliased:
        q_ref, k_ref, v_ref, _, o_ref, m_ref, l_ref, acc_ref = refs
    else:
        q_ref, k_ref, v_ref, o_ref, m_ref, l_ref, acc_ref = refs
    tq = q_ref.shape[0]
    nk = k_ref.shape[0]
    n_loop = nk // tk

    def chunk(rows, first=False):
        heads = range(MLA_HEADS)
        hs = [slice(h * MLA_SLOT, (h + 1) * MLA_SLOT) for h in heads]
        vs = [slice((h // 2) * 128, (h // 2) * 128 + 128) for h in heads]
        s = [lax.dot_general(q_ref[:, hs[h]], k_ref[rows, hs[h]], (((1,), (1,)), ((), ())),
                             preferred_element_type=F32) for h in heads]
        ap = [_flash_softmax(h, s[h], m_ref, l_ref, first) for h in heads]
        for h in heads:
            alpha, p = ap[h]
            pv = jnp.dot(p, v_ref[rows, vs[h]], preferred_element_type=F32)
            acc_ref[h] = pv if first else alpha * acc_ref[h] + pv

    has_tail = nk > n_loop * tk
    if has_tail:
        chunk(slice(n_loop * tk, nk), first=True)
    else:
        chunk(slice(0, tk), first=True)
    lo = 0 if has_tail else 1
    if n_loop > lo:
        def body(j, carry):
            chunk(pl.ds(pl.multiple_of(j * tk, tk), tk))
            return carry
        lax.fori_loop(lo, n_loop, body, 0)

    lane = lax.broadcasted_iota(jnp.int32, (tq, 128), 1)
    outs = []
    for pair in range(MLA_HEADS // 2):
        o0 = acc_ref[2 * pair] / l_ref[2 * pair]
        o1 = acc_ref[2 * pair + 1] / l_ref[2 * pair + 1]
        outs.append(jnp.where(lane < MLA_V, o0, o1))
    o_ref[...] = jnp.concatenate(outs, axis=-1).astype(o_ref.dtype)


def _mla_flash(q, k, v, n_lat, prev=None):
    b, s, nslot = q.shape
    n_ctx = s - n_lat
    nv = MLA_HEADS * MLA_V
    tk = 512
    if prev is None:
        tq = _pick_tile(n_lat, 1024)
        q_off, n_q, kv_rows, kv_blk = 0, n_lat // tq, s, 0
    else:
        assert n_lat % n_ctx == 0
        tq = _pick_tile(n_ctx, 256)
        q_off, n_q, kv_rows, kv_blk = n_lat // tq, n_ctx // tq, n_ctx, n_lat // n_ctx
    in_specs = [pl.BlockSpec((None, tq, nslot), lambda bi, i: (bi, i + q_off, 0)),
                pl.BlockSpec((None, kv_rows, nslot), lambda bi, i: (bi, kv_blk, 0)),
                pl.BlockSpec((None, kv_rows, nv), lambda bi, i: (bi, kv_blk, 0))]
    args = [q, k, v]
    aliases = {}
    if prev is not None:
        in_specs.append(pl.BlockSpec(memory_space=pl.ANY))
        args.append(prev)
        aliases = {3: 0}
    return pl.pallas_call(
        functools.partial(_mla_flash_kernel, tk=tk, aliased=prev is not None),
        grid=(b, n_q),
        in_specs=in_specs,
        out_specs=pl.BlockSpec((None, tq, nv), lambda bi, i: (bi, i + q_off, 0)),
        out_shape=jax.ShapeDtypeStruct((b, s, nv), BF16),
        scratch_shapes=[pltpu.VMEM((MLA_HEADS, tq, 128), F32),
                        pltpu.VMEM((MLA_HEADS, tq, 128), F32),
                        pltpu.VMEM((MLA_HEADS, tq, 128), F32)],
        input_output_aliases=aliases,
        compiler_params=_cparams(("parallel", "arbitrary")),
        name="mla_flash",
    )(*args)


NA_QROWS = 4
NA_SLAB = NA_QROWS + NA_WIN_ROWS


def _na_geometry(rows):
    assert rows % NA_QROWS == 0 and rows >= NA_SLAB
    nblk = rows // NA_QROWS
    qc = np.arange(GRID_W)
    cs = np.clip(qc - NA_WIN_COLS // 2, 0, GRID_W - NA_WIN_COLS)
    col_valid = (qc[None, :] >= cs[:, None]) & (qc[None, :] < cs[:, None] + NA_WIN_COLS)
    rel_c = np.clip(qc[None, :] - qc[:, None] + NA_WIN_COLS - 1, 0, 2 * NA_WIN_COLS - 2)
    onehot_c = (rel_c[None] == np.arange(2 * NA_WIN_COLS - 1)[:, None, None]) & col_valid[None]
    patterns, cls, starts = {}, [], []
    for i in range(nblk):
        r0 = i * NA_QROWS
        start = int(np.clip(r0 - NA_WIN_ROWS // 2, 0, rows - NA_SLAB))
        qr = r0 + np.arange(NA_QROWS)
        rs = np.clip(qr - NA_WIN_ROWS // 2, 0, rows - NA_WIN_ROWS)
        key = (start - r0,) + tuple((rs - r0).tolist())
        if key not in patterns:
            kr = start + np.arange(NA_SLAB)
            row_valid = (kr[None, :] >= rs[:, None]) & (kr[None, :] < rs[:, None] + NA_WIN_ROWS)
            rel_r = np.clip(kr[None, :] - qr[:, None] + NA_WIN_ROWS - 1, 0, 2 * NA_WIN_ROWS - 2)
            patterns[key] = (len(patterns), np.where(row_valid, rel_r, -1))
        cls.append(patterns[key][0])
        starts.append(start)
    ordered = sorted(patterns.values(), key=lambda z: z[0])
    rel_rows = np.stack([z[1] for z in ordered])
    meta = np.stack([np.asarray(cls, np.int32), np.asarray(starts, np.int32)])
    return meta, rel_rows, (onehot_c.astype(np.float32), col_valid)


def _na_bias_table(rpb, rel_rows, col_tables):
    onehot_c, col_valid = col_tables
    bc = jnp.einsum('hrc,cqk->hrqk', rpb.astype(F32), onehot_c, precision=lax.Precision.HIGHEST)
    bc = jnp.where(col_valid, bc, -jnp.inf)
    outside = jnp.full(bc.shape[:1] + bc.shape[2:], -jnp.inf, F32)
    pats = []
    for pat in rel_rows:
        qrows = [jnp.concatenate([bc[:, r] if r >= 0 else outside for r in row], axis=-1) for row in pat]
        pats.append(jnp.concatenate(qrows, axis=1))
    return jnp.stack(pats)


def _na_scores(q_pair, lane, h, k_parts):
    in_head = (lane < NA_DH) if h % 2 == 0 else (lane >= NA_DH)
    qm = jnp.where(in_head, q_pair * (NA_DH ** -0.5), 0.0).astype(BF16)
    return [lax.dot_general(qm, k, (((1,), (1,)), ((), ())), preferred_element_type=F32) for k in k_parts]


def _na_softmax(scores, bias):
    if bias is not None:
        scores = [scores[0] + bias] + scores[1:]
    m = scores[0].max(axis=-1, keepdims=True)
    for s in scores[1:]:
        m = jnp.maximum(m, s.max(axis=-1, keepdims=True))
    probs = [jnp.exp(s - m) for s in scores]
    den = probs[0].sum(axis=-1, keepdims=True)
    for p in probs[1:]:
        den = den + p.sum(axis=-1, keepdims=True)
    return [p.astype(BF16) for p in probs], den


def _na_kernel(meta_ref, q_ref, k_ref, v_ref, bias_ref, o_ref, *, n_lat, with_ctx):
    i = pl.program_id(1)
    nq = q_ref.shape[0]
    n_lat_tiles = n_lat // nq
    n_all = k_ref.shape[0]
    lane = lax.broadcasted_iota(jnp.int32, (nq, 128), 1)

    def run(windowed):
        key_rows = [pl.ds(n_lat, n_all - n_lat)]
        if windowed:
            start = pl.multiple_of(meta_ref[1, jnp.minimum(i, n_lat_tiles - 1)] * GRID_W, GRID_W)
            key_rows = [pl.ds(start, NA_SLAB * GRID_W)] + key_rows
        pair_lanes = [slice((h // 2) * 128, (h // 2 + 1) * 128) for h in range(NA_HEADS)]
        scores = [_na_scores(q_ref[:, pair_lanes[h]], lane, h, [k_ref[r, pair_lanes[h]] for r in key_rows])
                  for h in range(NA_HEADS)]
        soft = [_na_softmax(scores[h], bias_ref[h] if windowed else None) for h in range(NA_HEADS)]
        heads = []
        for h in range(NA_HEADS):
            probs, den = soft[h]
            out = None
            for p, r in zip(probs, key_rows):
                po = jnp.dot(p, v_ref[r, pair_lanes[h]], preferred_element_type=F32)
                out = po if out is None else out + po
            heads.append(out / den)
        outs = [jnp.where(lane < NA_DH, heads[2 * pair], heads[2 * pair + 1]) for pair in range(NA_HEADS // 2)]
        o_ref[...] = jnp.concatenate(outs, axis=-1).astype(o_ref.dtype)

    if with_ctx:
        pl.when(i < n_lat_tiles)(lambda: run(True))
        pl.when(i >= n_lat_tiles)(lambda: run(False))
    else:
        run(True)


def _na_attention(p, meta, table, n_lat, with_ctx_queries):
    b, s, _ = p.shape
    nq = NA_QROWS * GRID_W
    nk = NA_SLAB * GRID_W
    assert (s - n_lat) % nq == 0
    nlt = n_lat // nq
    qb = P_NA_QKV // NA_W
    grid_spec = pltpu.PrefetchScalarGridSpec(
        num_scalar_prefetch=1,
        grid=(b, s // nq if with_ctx_queries else nlt),
        in_specs=[pl.BlockSpec((None, nq, NA_W), lambda bi, i, m: (bi, i, qb)),
                  pl.BlockSpec((None, s, NA_W), lambda bi, i, m: (bi, 0, qb + 1)),
                  pl.BlockSpec((None, s, NA_W), lambda bi, i, m: (bi, 0, qb + 2)),
                  pl.BlockSpec((None, NA_HEADS, nq, nk),
                               lambda bi, i, m: (m[0, jnp.minimum(i, nlt - 1)], 0, 0, 0))],
        out_specs=pl.BlockSpec((None, nq, NA_W), lambda bi, i, m: (bi, i, 0)))
    return pl.pallas_call(
        functools.partial(_na_kernel, n_lat=n_lat, with_ctx=with_ctx_queries),
        grid_spec=grid_spec,
        out_shape=jax.ShapeDtypeStruct((b, s, NA_W), BF16),
        compiler_params=_cparams(("parallel", "arbitrary")),
        name="na_attention",
    )(meta, p, p, p, table)


def _tile_conv(x, prev, nxt, w_ref, tile_idx, n_tiles, n_lat_tiles):
    r = x.shape[0]
    width = w_ref.shape[0]
    left = width // 2
    has_prev = jnp.logical_and(tile_idx != 0, tile_idx != n_lat_tiles)
    has_next = jnp.logical_and(tile_idx != n_tiles - 1, tile_idx != n_lat_tiles - 1)
    prev = jnp.where(has_prev, prev, 0.0)
    nxt = jnp.where(has_next, nxt, 0.0)
    xe = jnp.concatenate([prev, x, nxt], axis=0)
    acc = None
    for j in range(width):
        o = HALO - left + j
        term = xe[o:o + r, :] * w_ref[j:j + 1, :]
        acc = term if acc is None else acc + term
    return acc


def _halo_specs(width, col_block, tile_of):
    per = SEQ_TILE // HALO

    def main(bi, s, *_):
        return (bi, tile_of(s), col_block)

    def prev(bi, s, *_):
        return (bi, jnp.maximum(tile_of(s) * per - 1, 0), col_block)

    def make_next(n_tiles):
        def nxt(bi, s, *_):
            return (bi, jnp.minimum((tile_of(s) + 1) * per, n_tiles * per - 1), col_block)
        return nxt

    return main, prev, make_next


def _gdn_prep_kernel(x_ref, xp_ref, xn_ref, ba_ref, cw_ref, ones_ref, exp_ref, alog_ref, dtb_ref,
                     q_ref, k_ref, v_ref, beta_ref, g_ref, *, n_lat_tiles):
    i = pl.program_id(1)
    y = _tile_conv(x_ref[...], xp_ref[...], xn_ref[...], cw_ref, i, pl.num_programs(1), n_lat_tiles)
    y = y * _sigmoid(y)
    q = y[:, :GDN_W]
    k = y[:, GDN_W:2 * GDN_W]
    v_ref[...] = y[:, 2 * GDN_W:].astype(v_ref.dtype)

    def head_norm(u):
        parts = jnp.concatenate(_split(u * u), axis=-1)
        ss = jnp.dot(parts, ones_ref[...], preferred_element_type=F32)
        return u * lax.rsqrt(ss + RMS_EPS)

    q_ref[...] = (head_norm(q) * (GDN_DK ** -0.5)).astype(q_ref.dtype)
    k_ref[...] = head_norm(k).astype(k_ref.dtype)

    ba = ba_ref[...]
    a = ba + dtb_ref[...]
    softplus = jnp.maximum(a, 0.0) + jnp.log1p(jnp.exp(-jnp.abs(a)))
    lane = lax.broadcasted_iota(jnp.int32, ba.shape, 1)
    compact = jnp.where(lane < 2 * GDN_HEADS, _sigmoid(ba), -jnp.exp(alog_ref[...]) * softplus)
    wide = jnp.dot(jnp.concatenate(_split(compact), axis=-1), exp_ref[...], preferred_element_type=F32)
    half = 2 * GDN_W
    beta_ref[...] = wide[:, :half].astype(beta_ref.dtype)
    g_ref[...] = wide[:, half:]


def _gdn_prep(p, conv_w, a_log, dt_bias, n_lat):
    b, s, _ = p.shape
    n_tiles = s // SEQ_TILE
    main, prev, make_next = _halo_specs(3 * GDN_W, 0, lambda t: t)
    ones_n = jnp.concatenate([_head_block_ones(GDN_HEADS, GDN_DK)] * N_SPLIT, axis=0)
    expand = np.zeros((128, 4 * GDN_W), np.float32)
    for kind in range(2):
        for d in range(2):
            for h in range(GDN_HEADS):
                c0 = kind * 2 * GDN_W + d * GDN_W + h * GDN_DV
                expand[kind * 2 * GDN_HEADS + d * GDN_HEADS + h, c0:c0 + GDN_DV] = 1.0
    expand_n = jnp.asarray(np.concatenate([expand] * N_SPLIT, axis=0), BF16)
    lanes = jnp.zeros((1, 128), F32)
    alog_e = lanes.at[0, 2 * GDN_HEADS:4 * GDN_HEADS].set(a_log.astype(F32).reshape(-1))
    dtb_e = lanes.at[0, 2 * GDN_HEADS:4 * GDN_HEADS].set(dt_bias.astype(F32).reshape(-1))
    tok = lambda w: pl.BlockSpec((None, SEQ_TILE, w), lambda bi, i: (bi, i, 0))
    return pl.pallas_call(
        functools.partial(_gdn_prep_kernel, n_lat_tiles=n_lat // SEQ_TILE),
        grid=(b, n_tiles),
        in_specs=[pl.BlockSpec((None, SEQ_TILE, 3 * GDN_W), main),
                  pl.BlockSpec((None, HALO, 3 * GDN_W), prev),
                  pl.BlockSpec((None, HALO, 3 * GDN_W), make_next(n_tiles)),
                  pl.BlockSpec((None, SEQ_TILE, 128), lambda bi, i: (bi, i, P_GDN_BA // 128)),
                  _full(conv_w), _full(ones_n), _full(expand_n), _full(alog_e), _full(dtb_e)],
        out_specs=[tok(GDN_W), tok(GDN_W), tok(GDN_W), tok(2 * GDN_W), tok(2 * GDN_W)],
        out_shape=[jax.ShapeDtypeStruct((b, s, GDN_W), BF16)] * 3 + [jax.ShapeDtypeStruct((b, s, 2 * GDN_W), BF16),
                                                                    jax.ShapeDtypeStruct((b, s, 2 * GDN_W), F32)],
        compiler_params=_cparams(("parallel", "parallel")),
        name="gdn_prep",
    )(p, p, p, p, conv_w, ones_n, expand_n, alog_e, dtb_e)


GDN_PAIR = 2 * GDN_DK


def _gdn_masks():
    c, w = GDN_CHUNK, GDN_W
    r2, c2 = np.arange(GDN_PAIR)[:, None], np.arange(GDN_PAIR)[None, :]
    bd = ((r2 // c) == (c2 // c)).astype(np.float32)
    i = np.arange(c)[:, None]
    j = (np.arange(w) % c)[None, :]
    level = np.zeros((c, w), np.int32)
    for bit in range(6):
        level += ((i ^ j) >= (1 << bit)).astype(np.int32)
    lvl = np.stack([(level == m).astype(np.float32) for m in range(7)])
    dirm = np.stack([np.stack([(j <= i), (j < i)]), np.stack([(j >= i), (j > i)])]).astype(np.float32)
    tj = (np.arange(N_SPLIT * c) % c)[None, :]
    tri = np.stack([(tj <= i), (tj >= i)]).astype(np.float32)
    return jnp.asarray(bd, BF16), jnp.asarray(lvl), jnp.asarray(dirm), jnp.asarray(tri, BF16)


def _heads_mm(x, y, bd, transpose_rhs=False):
    xb = x.astype(BF16)
    yb = y.astype(BF16)
    outs = []
    for pair in range(GDN_W // GDN_PAIR):
        ls = slice(pair * GDN_PAIR, (pair + 1) * GDN_PAIR)
        w = jnp.concatenate([yb[:, ls], yb[:, ls]], axis=0) * bd
        dims = (((1,), (1,)), ((), ())) if transpose_rhs else (((1,), (0,)), ((), ()))
        outs.append(lax.dot_general(xb[:, ls], w, dims, preferred_element_type=F32))
    return jnp.concatenate(outs, axis=1)


def _gdn_intra(probs, bd, lvl_ref, dirm_ref, tri_ref):
    c = GDN_CHUNK
    n = len(probs)
    eye = lvl_ref[0]
    gc, g_last, decay, gram = [], [], [], []
    for q, k, v, beta, g, rev in probs:
        d = 1 if rev else 0
        gcp = jnp.dot(tri_ref[d], jnp.concatenate(_split(g), axis=0), preferred_element_type=F32)
        gc.append(gcp)
        g_last.append(gcp[0:1, :] if rev else gcp[c - 1:c, :])
        gc_row = jnp.sum(gcp * eye, axis=0, keepdims=True)
        decay.append(dirm_ref[d, 0] * jnp.exp(jnp.minimum(gcp - gc_row, 0.0)))
        gram.append(_heads_mm(jnp.concatenate([k, q], axis=0), k, bd, transpose_rhs=True))
    lower = [dirm_ref[1 if p[5] else 0, 1] * p[3] * gram[x][:c] * decay[x] for x, p in enumerate(probs)]
    a_intra = [gram[x][c:] * decay[x] for x in range(n)]
    t = [eye - lower[x] * lvl_ref[1] for x in range(n)]
    for lev in range(2, 7):
        y = [_heads_mm(t[x], lower[x] * lvl_ref[lev], bd) for x in range(n)]
        z = [_heads_mm(y[x], t[x], bd) for x in range(n)]
        t = [t[x] - z[x] for x in range(n)]
    e_gc = [jnp.exp(gc[x]) for x in range(n)]
    u = [_heads_mm(t[x], p[2] * p[3], bd) for x, p in enumerate(probs)]
    w = [_heads_mm(t[x], p[1] * p[3] * e_gc[x], bd) for x, p in enumerate(probs)]
    wq = [jnp.concatenate([w[x], p[0] * e_gc[x]], axis=0).astype(BF16) for x, p in enumerate(probs)]
    k_dec = [(p[1] * jnp.exp(g_last[x] - gc[x])).astype(BF16) for x, p in enumerate(probs)]
    g_tot = [jnp.exp(g_last[x]) for x in range(n)]
    return u, wq, k_dec, a_intra, g_tot


def _gdn_state_steps(items, bd):
    c = GDN_CHUNK
    pairs = [slice(p * GDN_PAIR, (p + 1) * GDN_PAIR) for p in range(GDN_W // GDN_PAIR)]
    bdf = bd.astype(F32)
    ws_qs = [jnp.concatenate([jnp.dot(wq[:, ls], s_ref[p].astype(BF16), preferred_element_type=F32)
                              for p, ls in enumerate(pairs)], axis=1)
             for _, wq, _, _, _, s_ref in items]
    v_new = [it[0] - ws[:c] for it, ws in zip(items, ws_qs)]
    outs = [ws[c:] + _heads_mm(it[3], vn, bd) for it, ws, vn in zip(items, ws_qs, v_new)]
    for (_, _, k_dec, _, g_tot, s_ref), vn in zip(items, v_new):
        vb = vn.astype(BF16)
        for p, ls in enumerate(pairs):
            kv = lax.dot_general(k_dec[:, ls], vb[:, ls], (((0,), (0,)), ((), ())), preferred_element_type=F32)
            s_ref[p] = s_ref[p] * g_tot[:, ls] + kv * bdf
    return outs


def _gdn_scan_body(qf, kf, vf, bf, gf, qb, kb, vb, bb, gb, bd_ref, lvl_ref, dirm_ref, tri_ref,
                   of_ref, ob_ref, sf_ref, sb_ref):
    bd = bd_ref[...]
    n = SEQ_TILE // GDN_CHUNK
    probs, rows = [], []
    for c in range(n):
        rf = slice(c * GDN_CHUNK, (c + 1) * GDN_CHUNK)
        rb = slice((n - 1 - c) * GDN_CHUNK, (n - c) * GDN_CHUNK)
        probs.append(tuple(r[rf, :].astype(F32) for r in (qf, kf, vf, bf, gf)) + (False,))
        probs.append(tuple(r[rb, :].astype(F32) for r in (qb, kb, vb, bb, gb)) + (True,))
        rows += [rf, rb]
    u, wq, k_dec, a_intra, g_tot = _gdn_intra(probs, bd, lvl_ref, dirm_ref, tri_ref)
    for c in range(n):
        xs = (2 * c, 2 * c + 1)
        outs = _gdn_state_steps([(u[x], wq[x], k_dec[x], a_intra[x], g_tot[x], sb_ref if probs[x][5] else sf_ref)
                                 for x in xs], bd)
        for x, o in zip(xs, outs):
            (ob_ref if probs[x][5] else of_ref)[rows[x], :] = o


def _seq_tile_maps(n_lat_tiles, n_tiles):
    fwd = lambda s: lax.rem(s + n_lat_tiles, n_tiles)
    bwd = lambda s: n_tiles - 1 - s
    return fwd, bwd


def _gdn_scan_specs(q, k, v, beta, g, n_lat):
    b, s, _ = q.shape
    n_tiles = s // SEQ_TILE
    fwd, bwd = _seq_tile_maps(n_lat // SEQ_TILE, n_tiles)
    spec = lambda tile_of, col: pl.BlockSpec((None, SEQ_TILE, GDN_W), lambda bi, t: (bi, tile_of(t), col))
    masks = _gdn_masks()
    in_specs = [spec(fwd, 0)] * 5 + [spec(bwd, 0)] * 3 + [spec(bwd, 1)] * 2 + [_full(m) for m in masks]
    assert len(in_specs) == N_GDN_IN
    return (in_specs, [q, k, v, beta, g, q, k, v, beta, g, *masks], [spec(fwd, 0), spec(bwd, 0)],
            [jax.ShapeDtypeStruct((b, s, GDN_W), F32)] * 2,
            [pltpu.VMEM((GDN_W // GDN_PAIR, GDN_PAIR, GDN_PAIR), F32)] * 2)


def _lru_tile_prep(x_ref, xp_ref, xn_ref, cw_ref, cb_ref, wg_ref, bg_ref, nla_ref, a_ref, b_ref,
                   tile_idx, n_tiles, n_lat_tiles):
    xb = _tile_conv(x_ref[...], xp_ref[...], xn_ref[...], cw_ref, tile_idx, n_tiles, n_lat_tiles) + cb_ref[...]
    gates = _sigmoid(jnp.dot(xb.astype(BF16), wg_ref[...], preferred_element_type=F32) + bg_ref[...])
    log_a = nla_ref[...] * gates[:, :LRU_W]
    a_ref[...] = jnp.exp(log_a)
    th = jnp.tanh(log_a)
    b_ref[...] = jnp.sqrt(-2.0 * th / (1.0 - th)) * gates[:, LRU_W:] * xb


def _scan_group(a, b, h, row, reverse):
    for d in (1, 2, 4):
        if reverse:
            keep = row < 8 - d
            shift = 8 - d
        else:
            keep = row >= d
            shift = d
        a_s = jnp.where(keep, pltpu.roll(a, shift, 0), 1.0)
        b_s = jnp.where(keep, pltpu.roll(b, shift, 0), 0.0)
        b = a * b_s + b
        a = a * a_s
    return a * h + b


def _lru_scan_body(xf, xfp, xfn, xb, xbp, xbn, cw_ref, cb_ref, wgf, bgf, nlaf, wgb, bgb, nlab,
                   hf_ref, hb_ref, af_ref, bf_ref, ab_ref, bb_ref, cf_ref, cbk_ref, *, n_lat_tiles):
    s = pl.program_id(1)
    n_tiles = pl.num_programs(1)
    t_f, t_b = (m(s) for m in _seq_tile_maps(n_lat_tiles, n_tiles))
    _lru_tile_prep(xf, xfp, xfn, cw_ref, cb_ref, wgf, bgf, nlaf, af_ref, bf_ref, t_f, n_tiles, n_lat_tiles)
    _lru_tile_prep(xb, xbp, xbn, cw_ref, cb_ref, wgb, bgb, nlab, ab_ref, bb_ref, t_b, n_tiles, n_lat_tiles)

    n_groups = SEQ_TILE // 8
    row = lax.broadcasted_iota(jnp.int32, (8, LRU_W), 0)

    h_f, h_b = cf_ref[...], cbk_ref[...]
    for gi in range(n_groups):
        rf = slice(gi * 8, gi * 8 + 8)
        rb = slice((n_groups - 1 - gi) * 8, (n_groups - gi) * 8)
        out_f = _scan_group(af_ref[rf, :], bf_ref[rf, :], h_f, row, False)
        out_b = _scan_group(ab_ref[rb, :], bb_ref[rb, :], h_b, row, True)
        hf_ref[rf, :] = out_f
        hb_ref[rb, :] = out_b
        h_f = jnp.broadcast_to(out_f[7:8, :], (8, LRU_W))
        h_b = jnp.broadcast_to(out_b[0:1, :], (8, LRU_W))
    cf_ref[...] = h_f
    cbk_ref[...] = h_b


N_LRU_IN, N_GDN_IN, N_LRU_SCRATCH = 14, 14, 6


def _seq_mixers_kernel(*refs, n_lat_tiles):
    lru_in = refs[:N_LRU_IN]
    gdn_in = refs[N_LRU_IN:N_LRU_IN + N_GDN_IN]
    hf_ref, hb_ref, of_ref, ob_ref = refs[N_LRU_IN + N_GDN_IN:N_LRU_IN + N_GDN_IN + 4]
    scratch = refs[N_LRU_IN + N_GDN_IN + 4:]
    lru_scratch, gdn_scratch = scratch[:N_LRU_SCRATCH], scratch[N_LRU_SCRATCH:]

    @pl.when(pl.program_id(1) == 0)
    def _():
        for ref in lru_scratch[4:] + gdn_scratch:
            ref[...] = jnp.zeros(ref.shape, ref.dtype)

    _lru_scan_body(*lru_in, hf_ref, hb_ref, *lru_scratch, n_lat_tiles=n_lat_tiles)
    _gdn_scan_body(*gdn_in, of_ref, ob_ref, *gdn_scratch)


def _lru_scan_specs(p, conv_w, conv_b, w_r, b_r, w_i, b_i, lam, n_lat):
    b, s, _ = p.shape
    n_tiles = s // SEQ_TILE
    nlt = n_lat // SEQ_TILE
    fwd, bwd = _seq_tile_maps(nlt, n_tiles)
    col = P_LRU_X // LRU_W
    specs = []
    for tile_of in (fwd, bwd):
        main, prev, make_next = _halo_specs(LRU_W, col, tile_of)
        specs += [pl.BlockSpec((None, SEQ_TILE, LRU_W), main), pl.BlockSpec((None, HALO, LRU_W), prev),
                  pl.BlockSpec((None, HALO, LRU_W), make_next(n_tiles))]

    def blockdiag(w):
        return jax.scipy.linalg.block_diag(*[w[n] for n in range(LRU_BLOCKS)])

    dir_args = []
    for d in range(2):
        wg = jnp.concatenate([blockdiag(w_r[d]), blockdiag(w_i[d])], axis=1).astype(BF16)
        bg = jnp.concatenate([b_r[d], b_i[d]]).astype(F32).reshape(1, 2 * LRU_W)
        nla = (-LRU_C * jax.nn.softplus(-lam[d].astype(F32))).reshape(1, LRU_W)
        dir_args += [wg, bg, nla]
    cb2 = conv_b.reshape(1, LRU_W)
    out_spec = lambda tile_of: pl.BlockSpec((None, SEQ_TILE, LRU_W), lambda bi, t: (bi, tile_of(t), 0))
    in_specs = specs + [_full(conv_w), _full(cb2)] + [_full(a) for a in dir_args]
    assert len(in_specs) == N_LRU_IN
    return (in_specs, [p, p, p, p, p, p, conv_w, cb2, *dir_args], [out_spec(fwd), out_spec(bwd)],
            [jax.ShapeDtypeStruct((b, s, LRU_W), F32)] * 2,
            [pltpu.VMEM((SEQ_TILE, LRU_W), F32)] * 4 + [pltpu.VMEM((8, LRU_W), F32)] * 2)


def _seq_mixers(lru_parts, gdn_parts, bsz, n_tiles, n_lat):
    in_specs, args, out_specs, out_shape, scratch = ([*a, *b] for a, b in zip(lru_parts, gdn_parts))
    return pl.pallas_call(
        functools.partial(_seq_mixers_kernel, n_lat_tiles=n_lat // SEQ_TILE),
        grid=(bsz, n_tiles),
        in_specs=in_specs,
        out_specs=out_specs,
        out_shape=out_shape,
        scratch_shapes=scratch,
        compiler_params=_cparams(("parallel", "arbitrary")),
        name="seq_mixers",
    )(*args)


def _merge_kernel(x_ref, mc_ref, ml_ref, of_ref, ob_ref, z_ref, hf_ref, hb_ref, y_ref, uc_ref, ud_ref,
                  t0_ref, t1_ref, t2_ref, t3_ref, gn_ref, ones_ref, bg_ref, wb_ref, wo_ref, o_ref, *, n_lat):
    project = lambda u, n: jnp.dot(u, wb_ref[n], preferred_element_type=F32)
    o = of_ref[...] + ob_ref[...]
    ms = jnp.dot(jnp.concatenate(_split(o * o), axis=-1), ones_ref[...], preferred_element_type=F32) * (1.0 / GDN_DV)
    proj = {2: project(uc_ref[...], 2), 3: project(ud_ref[...], 3)}
    z = z_ref[...]
    ua = (o * lax.rsqrt(ms + RMS_EPS) * gn_ref[...] * (z * _sigmoid(z))).astype(BF16)
    y = y_ref[...]
    gelu = 0.5 * y * (1.0 + jnp.tanh(0.7978845608028654 * (y + 0.044715 * (y * y * y))))
    ub = ((hf_ref[...] + hb_ref[...]) * gelu).astype(BF16)
    proj[0] = project(ua, 0)
    proj[1] = project(ub, 1)

    merged = None
    for n, t_ref in enumerate((t0_ref, t1_ref, t2_ref, t3_ref)):
        gate = _sigmoid(t_ref[...] + bg_ref[n:n + 1, :].astype(BF16))
        term = gate.astype(F32) * proj[n]
        merged = term if merged is None else merged + term
    out = jnp.dot(merged.astype(BF16), wo_ref[...], preferred_element_type=F32)
    g1 = _row_select(x_ref.shape[0], pl.program_id(1), n_lat, mc_ref[2:3, :], ml_ref[2:3, :])
    o_ref[...] = x_ref[...] + g1 * out


def _merge(xs, mod_c, mod_l, gdn_out, lru_out, uc, ud, p32, p16, gdn_norm_g, b_gate, w_branch, w_out, layer, n_lat,
           with_ctx):
    b, s, d = xs.shape
    rows = s if with_ctx else n_lat
    tm = _pick_tile(rows, 544)
    gate_blk = P_GATES // d
    tok = lambda w: pl.BlockSpec((None, tm, w), lambda bi, i: (bi, i, 0))
    ones_n = jnp.concatenate([_head_block_ones(GDN_HEADS, GDN_DV)] * N_SPLIT, axis=0)
    gn = jnp.tile(gdn_norm_g.astype(F32), GDN_HEADS).reshape(1, GDN_W)
    pcol = lambda c: pl.BlockSpec((None, tm, BRANCH_W), lambda bi, i: (bi, i, c))
    in_specs = [tok(d), pl.BlockSpec((8, d), lambda bi, i: (0, 0)), pl.BlockSpec((None, 8, d), lambda bi, i: (bi, 0, 0))]
    in_specs += [tok(BRANCH_W), tok(BRANCH_W), pcol(P_GDN_Z // GDN_W), tok(BRANCH_W), tok(BRANCH_W),
                 pcol(P_LRU_Y // LRU_W), tok(BRANCH_W), tok(BRANCH_W)]
    in_specs += [pl.BlockSpec((None, tm, d), functools.partial(lambda bi, i, n: (bi, i, gate_blk + n), n=n))
                 for n in range(N_BRANCH)]
    in_specs += [_full(gn), _full(ones_n), _full(b_gate),
                 pl.BlockSpec((None,) + w_branch.shape[1:], lambda bi, i: (layer, 0, 0, 0)),
                 pl.BlockSpec((None,) + w_out.shape[1:], lambda bi, i: (layer, 0, 0))]
    return pl.pallas_call(
        functools.partial(_merge_kernel, n_lat=n_lat),
        grid=(b, rows // tm),
        in_specs=in_specs,
        out_specs=tok(d),
        out_shape=jax.ShapeDtypeStruct((b, rows, d), F32),
        input_output_aliases={0: 0} if with_ctx else {},
        compiler_params=_cparams(("parallel", "parallel")),
        name="merge",
    )(xs, mod_c, mod_l, gdn_out[0], gdn_out[1], p32, lru_out[0], lru_out[1], p32, uc, ud, p16, p16, p16, p16,
      gn, ones_n, b_gate, w_branch, w_out)


def _mlp_kernel(x_ref, mc_ref, ml_ref, gn_ref, gf_ref, w1_ref, w2_ref, o_ref, h_ref, acc_ref, *, n_lat, final_norm):
    f = pl.program_id(2)
    tm = x_ref.shape[0]
    i = pl.program_id(1)

    @pl.when(f == 0)
    def _():
        _norm_modulate(x_ref, h_ref, gn_ref, mc_ref, ml_ref, 3, i, n_lat)
        acc_ref[...] = jnp.zeros(acc_ref.shape, F32)

    a = jnp.maximum(jnp.dot(h_ref[...], w1_ref[...], preferred_element_type=F32), 0.0)
    acc_ref[...] += jnp.dot((a * a).astype(BF16), w2_ref[...], preferred_element_type=F32)

    @pl.when(f == pl.num_programs(2) - 1)
    def _():
        g2 = _row_select(tm, i, n_lat, mc_ref[5:6, :], ml_ref[5:6, :])
        y = x_ref[...] + g2 * acc_ref[...]
        if final_norm:
            y = y * lax.rsqrt(jnp.mean(y * y, axis=-1, keepdims=True) + RMS_EPS) * gf_ref[...]
        o_ref[...] = y


def _mlp(xs, mod_c, mod_l, gain, w1, w2, layer, final_gain, n_lat, final_norm):
    b, rows, d = xs.shape
    dff = w1.shape[-1]
    tm = _pick_tile(rows, 1088)
    tf = 1024
    row = pl.BlockSpec((1, d), lambda bi, i, f: (0, 0))
    return pl.pallas_call(
        functools.partial(_mlp_kernel, n_lat=n_lat, final_norm=final_norm),
        grid=(b, rows // tm, dff // tf),
        in_specs=[pl.BlockSpec((None, tm, d), lambda bi, i, f: (bi, i, 0)),
                  pl.BlockSpec((8, d), lambda bi, i, f: (0, 0)),
                  pl.BlockSpec((None, 8, d), lambda bi, i, f: (bi, 0, 0)),
                  row, row,
                  pl.BlockSpec((None, d, tf), lambda bi, i, f: (layer, 0, f)),
                  pl.BlockSpec((None, tf, d), lambda bi, i, f: (layer, f, 0))],
        out_specs=pl.BlockSpec((None, tm, d), lambda bi, i, f: (bi, i, 0)),
        out_shape=jax.ShapeDtypeStruct((b, rows, d), F32),
        scratch_shapes=[pltpu.VMEM((tm, d), BF16), pltpu.VMEM((tm, d), F32)],
        compiler_params=_cparams(("parallel", "parallel", "arbitrary")),
        name="mlp",
    )(xs, mod_c, mod_l, gain, final_gain, w1, w2)


def kernel(x, c, ctx, c_ctx, mod_w, mod_b, norm1_g, norm2_g, w_in, b_gate, gdn_conv_w, gdn_a_log, gdn_dt_bias,
           gdn_norm_g, lru_conv_w, lru_conv_b, lru_w_r, lru_b_r, lru_w_i, lru_b_i, lru_lambda, mla_q_norm_g,
           mla_w_uq, mla_kv_norm_g, mla_w_ukv, na_rpb, w_branch, w_out, mlp_w1, mlp_w2, final_norm_g):
    bsz, n_tok, d = x.shape
    n_ctx = ctx.shape[1]
    depth = w_in.shape[0]
    assert n_ctx % SEQ_TILE == 0 and n_tok % SEQ_TILE == 0 and n_tok % GRID_W == 0
    na_meta, na_rel_rows, na_cols = _na_geometry(n_tok // GRID_W)
    na_meta = jnp.asarray(na_meta)
    cos, sin = _rope_tables(n_tok, n_ctx)

    n_rows = -(-(bsz + 1) // 8) * 8
    cc = jnp.zeros((n_rows, d), F32).at[:bsz].set(c).at[bsz].set(c_ctx)
    final_gain = final_norm_g.reshape(1, d)

    w_in_all = _arrange_w_in(w_in)
    wb_all = w_branch.astype(BF16)
    wo_all = w_out.astype(BF16)
    w1_all = mlp_w1.astype(BF16)
    w2_all = mlp_w2.astype(BF16)

    xs = jnp.concatenate([x, ctx], axis=1)
    for l in range(depth):
        need_ctx = l < depth - 1
        mod = _modulation(cc, mod_w, mod_b[l], l).reshape(n_rows, N_MOD, d)
        pad = jnp.zeros((8 - N_MOD, d), F32)
        mod_c = jnp.concatenate([mod[bsz], pad], axis=0)
        mod_l = jnp.concatenate([mod[:bsz], jnp.broadcast_to(pad, (bsz, 8 - N_MOD, d))], axis=1)

        wq, wk, wv, place = _arrange_mla(mla_w_uq[l], mla_w_ukv[l])
        gq = mla_q_norm_g[l].reshape(1, -1)
        gkv = mla_kv_norm_g[l].reshape(1, -1)
        g1n = norm1_g[l].reshape(1, d)
        g2n = norm2_g[l].reshape(1, d)

        p32, p16 = _inproj(xs, mod_c, mod_l, g1n, w_in_all, l, n_tok)

        gq_, gk_, gv_, gbeta, gg = _gdn_prep(p32, gdn_conv_w[l], gdn_a_log[l], gdn_dt_bias[l], n_tok)
        h_f, h_b, o_f, o_b = _seq_mixers(
            _lru_scan_specs(p32, lru_conv_w[l], lru_conv_b[l], lru_w_r[l], lru_b_r[l], lru_w_i[l], lru_b_i[l],
                            lru_lambda[l], n_tok),
            _gdn_scan_specs(gq_, gk_, gv_, gbeta, gg, n_tok), bsz, (n_tok + n_ctx) // SEQ_TILE, n_tok)

        mq, mk, mv = _mla_prep(p32, p16, cos, sin, gq, gkv, wq, wk, wv, place)
        uc = _mla_flash(mq, mk, mv, n_tok)
        if need_ctx:
            uc = _mla_flash(mq, mk, mv, n_tok, prev=uc)
        ud = _na_attention(p16, na_meta, _na_bias_table(na_rpb[l], na_rel_rows, na_cols), n_tok, need_ctx)

        xs = _merge(xs, mod_c, mod_l, (o_f, o_b), (h_f, h_b), uc, ud, p32, p16, gdn_norm_g[l], b_gate[l],
                    wb_all, wo_all, l, n_tok, need_ctx)
        xs = _mlp(xs, mod_c, mod_l, g2n, w1_all, w2_all, l, final_gain, n_tok, l == depth - 1)
    return xs
```

```python
import functools

import jax
import jax.numpy as jnp
import numpy as np
from jax import lax
from jax.experimental import pallas as pl
from jax.experimental.pallas import tpu as pltpu

F32 = jnp.float32
BF16 = jnp.bfloat16

GRID_W = 64
N_MOD = 6
RMS_EPS = 1e-6
GDN_HEADS = 4
GDN_DK = 64
GDN_DV = 64
GDN_CHUNK = 64
GDN_W = GDN_HEADS * GDN_DV
LRU_W = 256
LRU_BLOCKS = 4
LRU_C = 8.0
MLA_HEADS = 4
MLA_Q_RANK = 256
MLA_KV_RANK = 128
MLA_NOPE = 64
MLA_ROPE = 32
MLA_V = 64
MLA_SLOT = 128
ROPE_BASE = 10000.0
LOG2_E = 1.4426950408889634
NA_HEADS = 4
NA_DH = 64
NA_W = NA_HEADS * NA_DH
NA_WIN_ROWS = 8
NA_WIN_COLS = 16
N_BRANCH = 4
BRANCH_W = 256

SEQ_TILE = 256
HALO = 8

_REF_COLS = {}
_off = 0
for _name, _w in (('gdn_qkv', 3 * GDN_W), ('gdn_z', GDN_W), ('gdn_beta', 2 * GDN_HEADS), ('gdn_a', 2 * GDN_HEADS),
                  ('lru_x', LRU_W), ('lru_y', LRU_W), ('mla_q', MLA_Q_RANK), ('mla_kv', MLA_KV_RANK),
                  ('mla_kr', MLA_ROPE), ('na_qkv', 3 * NA_W)):
    _REF_COLS[_name] = (_off, _w)
    _off += _w
N_MIX_COLS = _off

P_GDN_QKV = 0
P_GDN_Z = 768
P_LRU_X = 1024
P_LRU_Y = 1280
P_MLA_KV = 1536
P_MLA_KR = 1664
P_GDN_BA = 1792
P32_COLS = 2048
P_MLA_Q = 0
P_NA_QKV = 256
P_GATES = 1024
P16_COLS = P_GATES + N_BRANCH * 1024
PROJ_TILE = 1024

V7X_VMEM_BYTES = 64 * 1024 * 1024
VMEM_LIMIT = V7X_VMEM_BYTES - 12 * 1024 * 1024


def _cparams(sem):
    return pltpu.CompilerParams(dimension_semantics=sem, vmem_limit_bytes=VMEM_LIMIT)


def _pick_tile(n, cap):
    best = 8
    for t in range(8, min(n, cap) + 1, 8):
        if n % t == 0:
            best = t
    return best


def _full(a):
    return pl.BlockSpec(a.shape, lambda *_: (0,) * a.ndim)


N_SPLIT = 2


def _split(x):
    hi = x.astype(BF16)
    lo = (x - hi.astype(F32)).astype(BF16)
    return hi, lo


def _sigmoid(x):
    return 0.5 * jnp.tanh(0.5 * x) + 0.5


def _arrange_w_in(w_in):
    pieces, pos = [], 0

    def put(dst, block):
        nonlocal pos
        if dst > pos:
            pieces.append(jnp.zeros(w_in.shape[:-1] + (dst - pos,), w_in.dtype))
        pieces.append(block)
        pos = dst + block.shape[-1]

    ref = lambda name: w_in[..., _REF_COLS[name][0]:_REF_COLS[name][0] + _REF_COLS[name][1]]
    for name, dst in (('gdn_qkv', P_GDN_QKV), ('gdn_z', P_GDN_Z), ('lru_x', P_LRU_X), ('lru_y', P_LRU_Y),
                      ('mla_kv', P_MLA_KV), ('mla_kr', P_MLA_KR), ('gdn_beta', P_GDN_BA),
                      ('gdn_a', P_GDN_BA + 2 * GDN_HEADS)):
        put(dst, ref(name))
    put(P32_COLS + P_MLA_Q, ref('mla_q'))
    put(P32_COLS + P_NA_QKV, ref('na_qkv'))
    put(P32_COLS + P_GATES, w_in[..., N_MIX_COLS:])
    assert pos == P32_COLS + P16_COLS
    return jnp.concatenate(pieces, axis=-1).astype(BF16)


def _rope_perm():
    q = MLA_ROPE // 4
    src = np.zeros(MLA_ROPE, np.int32)
    sign = np.zeros(MLA_ROPE, np.float32)
    for base in (0, 2 * q):
        for d in range(q):
            src[base + d] = base + d + q
            sign[base + d] = -1.0
            src[base + q + d] = base + d
            sign[base + q + d] = 1.0
    return src, sign


def _arrange_mla(w_uq, w_ukv):
    src, sign = _rope_perm()
    hq = MLA_NOPE + MLA_ROPE
    wq = jnp.zeros((MLA_Q_RANK, 2 * MLA_HEADS * MLA_SLOT), F32)
    wk = jnp.zeros((MLA_KV_RANK, MLA_HEADS * MLA_SLOT), F32)
    wv = jnp.zeros((MLA_KV_RANK, MLA_HEADS * MLA_V), F32)
    place = np.zeros((2 * MLA_SLOT, 2 * MLA_HEADS * MLA_SLOT), np.float32)
    rot_off = MLA_HEADS * MLA_SLOT
    for h in range(MLA_HEADS):
        nope = w_uq[:, h * hq:h * hq + MLA_NOPE]
        pe = w_uq[:, h * hq + MLA_NOPE:(h + 1) * hq]
        s = h * MLA_SLOT
        wq = wq.at[:, s:s + MLA_NOPE].set(nope)
        wq = wq.at[:, s + MLA_NOPE:s + MLA_NOPE + MLA_ROPE].set(pe)
        wq = wq.at[:, rot_off + s + MLA_NOPE:rot_off + s + MLA_NOPE + MLA_ROPE].set(pe[:, src] * sign)
        wk = wk.at[:, s:s + MLA_NOPE].set(w_ukv[:, h * 128:h * 128 + MLA_NOPE])
        wv = wv.at[:, h * MLA_V:(h + 1) * MLA_V].set(w_ukv[:, h * 128 + MLA_NOPE:(h + 1) * 128])
        for d in range(MLA_ROPE):
            for half in (0, MLA_SLOT):
                place[half + d, s + MLA_NOPE + d] = 1.0
                place[half + src[d], rot_off + s + MLA_NOPE + d] = sign[d]
    return wq.astype(BF16), wk.astype(BF16), wv.astype(BF16), jnp.asarray(place, BF16)


def _rope_tables(n_tok, n_ctx):
    cos = np.ones((n_tok + n_ctx, MLA_SLOT), np.float32)
    sin = np.zeros((n_tok + n_ctx, MLA_SLOT), np.float32)
    t = np.arange(n_tok)
    row = (t // GRID_W).astype(np.float32)
    col = (t % GRID_W).astype(np.float32)
    n_freq = MLA_ROPE // 4
    inv = (ROPE_BASE ** (-np.arange(n_freq, dtype=np.float32) / n_freq)).astype(np.float32)
    ar = row[:, None] * inv
    ac = col[:, None] * inv
    ang = np.concatenate([ar, ar, ac, ac], axis=-1).astype(np.float32)
    cos[:n_tok, MLA_NOPE:MLA_NOPE + MLA_ROPE] = np.cos(ang)
    sin[:n_tok, MLA_NOPE:MLA_NOPE + MLA_ROPE] = np.sin(ang)
    return jnp.asarray(cos), jnp.asarray(sin)


def _head_block_ones(n_heads, width):
    m = np.kron(np.eye(n_heads, dtype=np.float32), np.ones((width, width), np.float32))
    return jnp.asarray(m, BF16)


def _mod_kernel(c_ref, w_ref, b_ref, o_ref):
    c = c_ref[...]
    s = c * _sigmoid(c)
    o_ref[...] = jnp.dot(s, w_ref[...], preferred_element_type=F32) + b_ref[...]


def _modulation(cc, mod_w, mod_b, layer):
    r, d = cc.shape
    n = mod_w.shape[-1]
    tn = 1024
    return pl.pallas_call(
        _mod_kernel,
        grid=(n // tn,),
        in_specs=[pl.BlockSpec((r, d), lambda j: (0, 0)),
                  pl.BlockSpec((None, d, tn), lambda j: (layer, 0, j)),
                  pl.BlockSpec((1, tn), lambda j: (0, j))],
        out_specs=pl.BlockSpec((r, tn), lambda j: (0, j)),
        out_shape=jax.ShapeDtypeStruct((r, n), F32),
        compiler_params=_cparams(("arbitrary",)),
        name="modulation",
    )(cc, mod_w, mod_b.reshape(1, n))


def _row_select(tile_rows, tile_idx, n_lat, ctx_vec, lat_vec):
    row = tile_idx * tile_rows + lax.broadcasted_iota(jnp.int32, (tile_rows, 1), 0)
    return jnp.where(row < n_lat, lat_vec, ctx_vec)


def _norm_modulate(x_ref, h_ref, gain_ref, mc_ref, ml_ref, shift_row, tile_idx, n_lat):
    tm = x_ref.shape[0]
    x = x_ref[...]
    xn = x * lax.rsqrt(jnp.mean(x * x, axis=-1, keepdims=True) + RMS_EPS)
    sh_l, sh_c = ml_ref[shift_row:shift_row + 1, :], mc_ref[shift_row:shift_row + 1, :]
    amp_l = gain_ref[...] * (1.0 + ml_ref[shift_row + 1:shift_row + 2, :])
    amp_c = gain_ref[...] * (1.0 + mc_ref[shift_row + 1:shift_row + 2, :])
    all_latent = (tile_idx + 1) * tm <= n_lat

    @pl.when(all_latent)
    def _():
        h_ref[...] = (xn * amp_l + sh_l).astype(h_ref.dtype)

    @pl.when(jnp.logical_not(all_latent))
    def _():
        amp = _row_select(tm, tile_idx, n_lat, amp_c, amp_l)
        shift = _row_select(tm, tile_idx, n_lat, sh_c, sh_l)
        h_ref[...] = (xn * amp + shift).astype(h_ref.dtype)


def _inproj_first_kernel(x_ref, mc_ref, ml_ref, g_ref, w_ref, o_ref, h_ref, *, n_lat):
    @pl.when(pl.program_id(2) == 0)
    def _():
        _norm_modulate(x_ref, h_ref, g_ref, mc_ref, ml_ref, 0, pl.program_id(1), n_lat)

    o_ref[...] = jnp.dot(h_ref[...], w_ref[...], preferred_element_type=F32).astype(o_ref.dtype)


def _inproj_second_kernel(h_ref, w_ref, o_ref):
    o_ref[...] = jnp.dot(h_ref[...], w_ref[...], preferred_element_type=F32).astype(o_ref.dtype)


def _inproj(xs, mod_c, mod_l, gain, w, layer, n_lat):
    b, s, d = xs.shape
    tm = _pick_tile(s, 1088)
    tn = PROJ_TILE
    n32 = P32_COLS // tn
    p16, h = pl.pallas_call(
        functools.partial(_inproj_first_kernel, n_lat=n_lat),
        grid=(b, s // tm, P16_COLS // tn),
        in_specs=[pl.BlockSpec((None, tm, d), lambda bi, i, j: (bi, i, 0)),
                  pl.BlockSpec((8, d), lambda bi, i, j: (0, 0)),
                  pl.BlockSpec((None, 8, d), lambda bi, i, j: (bi, 0, 0)),
                  pl.BlockSpec((1, d), lambda bi, i, j: (0, 0)),
                  pl.BlockSpec((None, d, tn), lambda bi, i, j: (layer, 0, j + n32))],
        out_specs=[pl.BlockSpec((None, tm, tn), lambda bi, i, j: (bi, i, j)),
                   pl.BlockSpec((None, tm, d), lambda bi, i, j: (bi, i, 0))],
        out_shape=[jax.ShapeDtypeStruct((b, s, P16_COLS), BF16), jax.ShapeDtypeStruct((b, s, d), BF16)],
        compiler_params=_cparams(("parallel", "parallel", "arbitrary")),
        name="inproj_bf16",
    )(xs, mod_c, mod_l, gain, w)
    p32 = pl.pallas_call(
        _inproj_second_kernel,
        grid=(b, s // tm, n32),
        in_specs=[pl.BlockSpec((None, tm, d), lambda bi, i, j: (bi, i, 0)),
                  pl.BlockSpec((None, d, tn), lambda bi, i, j: (layer, 0, j))],
        out_specs=pl.BlockSpec((None, tm, tn), lambda bi, i, j: (bi, i, j)),
        out_shape=jax.ShapeDtypeStruct((b, s, P32_COLS), F32),
        compiler_params=_cparams(("parallel", "parallel", "arbitrary")),
        name="inproj_f32",
    )(h, w)
    return p32, p16


def _mla_prep_kernel(ql_ref, kv_ref, cos_ref, sin_ref, gq_ref, gkv_ref, wq_ref, wk_ref, wv_ref, pl_ref,
                     q_ref, k_ref, v_ref, *, scale):
    nslot = MLA_HEADS * MLA_SLOT
    cos = jnp.concatenate([cos_ref[...]] * MLA_HEADS, axis=-1)
    sin = jnp.concatenate([sin_ref[...]] * MLA_HEADS, axis=-1)

    ql = ql_ref[...].astype(F32)
    qn = ql * lax.rsqrt(jnp.mean(ql * ql, axis=-1, keepdims=True) + RMS_EPS) * gq_ref[...]
    q2 = jnp.dot(qn.astype(BF16), wq_ref[...], preferred_element_type=F32)
    q = (q2[:, :nslot] * cos + q2[:, nslot:] * sin) * scale
    q_ref[...] = q.astype(BF16)

    kvkr = kv_ref[...]
    kvl = kvkr[:, :MLA_KV_RANK]
    kr = kvkr[:, MLA_KV_RANK:]
    kvn = (kvl * lax.rsqrt(jnp.mean(kvl * kvl, axis=-1, keepdims=True) + RMS_EPS) * gkv_ref[...]).astype(BF16)
    kn = jnp.dot(kvn, wk_ref[...], preferred_element_type=F32)
    v_ref[...] = jnp.dot(kvn, wv_ref[...], preferred_element_type=F32).astype(BF16)
    kr_hi = kr.astype(BF16)
    kr_lo = (kr - kr_hi.astype(F32)).astype(BF16)
    kr2 = jnp.dot(jnp.concatenate([kr_hi, kr_lo], axis=-1), pl_ref[...], preferred_element_type=F32)
    k_ref[...] = (kn + kr2[:, :nslot] * cos + kr2[:, nslot:] * sin).astype(BF16)


def _mla_prep(p32, p16, cos, sin, gq, gkv, wq, wk, wv, place):
    b, s, _ = p32.shape
    tm = _pick_tile(s, 544)
    nslot = MLA_HEADS * MLA_SLOT
    scale = (MLA_NOPE + MLA_ROPE) ** -0.5 * LOG2_E
    return pl.pallas_call(
        functools.partial(_mla_prep_kernel, scale=scale),
        grid=(b, s // tm),
        in_specs=[pl.BlockSpec((None, tm, MLA_Q_RANK), lambda bi, i: (bi, i, P_MLA_Q // MLA_Q_RANK)),
                  pl.BlockSpec((None, tm, 2 * MLA_SLOT), lambda bi, i: (bi, i, P_MLA_KV // (2 * MLA_SLOT))),
                  pl.BlockSpec((tm, MLA_SLOT), lambda bi, i: (i, 0)),
                  pl.BlockSpec((tm, MLA_SLOT), lambda bi, i: (i, 0)),
                  _full(gq), _full(gkv), _full(wq), _full(wk), _full(wv), _full(place)],
        out_specs=[pl.BlockSpec((None, tm, nslot), lambda bi, i: (bi, i, 0)),
                   pl.BlockSpec((None, tm, nslot), lambda bi, i: (bi, i, 0)),
                   pl.BlockSpec((None, tm, MLA_HEADS * MLA_V), lambda bi, i: (bi, i, 0))],
        out_shape=[jax.ShapeDtypeStruct((b, s, nslot), BF16),
                   jax.ShapeDtypeStruct((b, s, nslot), BF16),
                   jax.ShapeDtypeStruct((b, s, MLA_HEADS * MLA_V), BF16)],
        compiler_params=_cparams(("parallel", "parallel")),
        name="mla_prep",
    )(p16, p32, cos, sin, gq, gkv, wq, wk, wv, place)


def _flash_softmax(h, s, m_ref, l_ref, first):
    row_max = jnp.max(s, axis=-1, keepdims=True)
    m_new = jnp.broadcast_to(row_max, m_ref.shape[1:]) if first else jnp.maximum(m_ref[h], row_max)
    p = jnp.exp2(s - jnp.concatenate([m_new] * (s.shape[1] // 128), axis=-1))
    row_sum = jnp.sum(p, axis=-1, keepdims=True)
    if first:
        alpha = None
        l_ref[h] = jnp.broadcast_to(row_sum, l_ref.shape[1:])
    else:
        alpha = jnp.exp2(m_ref[h] - m_new)
        l_ref[h] = alpha * l_ref[h] + row_sum
    m_ref[h] = m_new
    return alpha, p.astype(BF16)


def _mla_flash_kernel(*refs, tk, aliased):
    if aliased:
        q_ref, k_ref, v_ref, _, o_ref, m_ref, l_ref, acc_ref = refs
    else:
        q_ref, k_ref, v_ref, o_ref, m_ref, l_ref, acc_ref = refs
    tq = q_ref.shape[0]
    nk = k_ref.shape[0]
    n_loop = nk // tk

    def chunk(rows, first=False):
        heads = range(MLA_HEADS)
        hs = [slice(h * MLA_SLOT, (h + 1) * MLA_SLOT) for h in heads]
        vs = [slice((h // 2) * 128, (h // 2) * 128 + 128) for h in heads]
        s = [lax.dot_general(q_ref[:, hs[h]], k_ref[rows, hs[h]], (((1,), (1,)), ((), ())),
                             preferred_element_type=F32) for h in heads]
        ap = [_flash_softmax(h, s[h], m_ref, l_ref, first) for h in heads]
        for h in heads:
            alpha, p = ap[h]
            pv = jnp.dot(p, v_ref[rows, vs[h]], preferred_element_type=F32)
            acc_ref[h] = pv if first else alpha * acc_ref[h] + pv

    has_tail = nk > n_loop * tk
    if has_tail:
        chunk(slice(n_loop * tk, nk), first=True)
    else:
        chunk(slice(0, tk), first=True)
    lo = 0 if has_tail else 1
    if n_loop > lo:
        def body(j, carry):
            chunk(pl.ds(pl.multiple_of(j * tk, tk), tk))
            return carry
        lax.fori_loop(lo, n_loop, body, 0)

    lane = lax.broadcasted_iota(jnp.int32, (tq, 128), 1)
    outs = []
    for pair in range(MLA_HEADS // 2):
        o0 = acc_ref[2 * pair] / l_ref[2 * pair]
        o1 = acc_ref[2 * pair + 1] / l_ref[2 * pair + 1]
        outs.append(jnp.where(lane < MLA_V, o0, o1))
    o_ref[...] = jnp.concatenate(outs, axis=-1).astype(o_ref.dtype)


def _mla_flash(q, k, v, n_lat, prev=None):
    b, s, nslot = q.shape
    n_ctx = s - n_lat
    nv = MLA_HEADS * MLA_V
    tk = 512
    if prev is None:
        tq = _pick_tile(n_lat, 1024)
        q_off, n_q, kv_rows, kv_blk = 0, n_lat // tq, s, 0
    else:
        assert n_lat % n_ctx == 0
        tq = _pick_tile(n_ctx, 256)
        q_off, n_q, kv_rows, kv_blk = n_lat // tq, n_ctx // tq, n_ctx, n_lat // n_ctx
    in_specs = [pl.BlockSpec((None, tq, nslot), lambda bi, i: (bi, i + q_off, 0)),
                pl.BlockSpec((None, kv_rows, nslot), lambda bi, i: (bi, kv_blk, 0)),
                pl.BlockSpec((None, kv_rows, nv), lambda bi, i: (bi, kv_blk, 0))]
    args = [q, k, v]
    aliases = {}
    if prev is not None:
        in_specs.append(pl.BlockSpec(memory_space=pl.ANY))
        args.append(prev)
        aliases = {3: 0}
    return pl.pallas_call(
        functools.partial(_mla_flash_kernel, tk=tk, aliased=prev is not None),
        grid=(b, n_q),
        in_specs=in_specs,
        out_specs=pl.BlockSpec((None, tq, nv), lambda bi, i: (bi, i + q_off, 0)),
        out_shape=jax.ShapeDtypeStruct((b, s, nv), BF16),
        scratch_shapes=[pltpu.VMEM((MLA_HEADS, tq, 128), F32),
                        pltpu.VMEM((MLA_HEADS, tq, 128), F32),
                        pltpu.VMEM((MLA_HEADS, tq, 128), F32)],
        input_output_aliases=aliases,
        compiler_params=_cparams(("parallel", "arbitrary")),
        name="mla_flash",
    )(*args)


NA_QROWS = 4
NA_SLAB = NA_QROWS + NA_WIN_ROWS


def _na_geometry(rows):
    assert rows % NA_QROWS == 0 and rows >= NA_SLAB
    nblk = rows // NA_QROWS
    qc = np.arange(GRID_W)
    cs = np.clip(qc - NA_WIN_COLS // 2, 0, GRID_W - NA_WIN_COLS)
    col_valid = (qc[None, :] >= cs[:, None]) & (qc[None, :] < cs[:, None] + NA_WIN_COLS)
    rel_c = np.clip(qc[None, :] - qc[:, None] + NA_WIN_COLS - 1, 0, 2 * NA_WIN_COLS - 2)
    onehot_c = (rel_c[None] == np.arange(2 * NA_WIN_COLS - 1)[:, None, None]) & col_valid[None]
    patterns, cls, starts = {}, [], []
    for i in range(nblk):
        r0 = i * NA_QROWS
        start = int(np.clip(r0 - NA_WIN_ROWS // 2, 0, rows - NA_SLAB))
        qr = r0 + np.arange(NA_QROWS)
        rs = np.clip(qr - NA_WIN_ROWS // 2, 0, rows - NA_WIN_ROWS)
        key = (start - r0,) + tuple((rs - r0).tolist())
        if key not in patterns:
            kr = start + np.arange(NA_SLAB)
            row_valid = (kr[None, :] >= rs[:, None]) & (kr[None, :] < rs[:, None] + NA_WIN_ROWS)
            rel_r = np.clip(kr[None, :] - qr[:, None] + NA_WIN_ROWS - 1, 0, 2 * NA_WIN_ROWS - 2)
            patterns[key] = (len(patterns), np.where(row_valid, rel_r, -1))
        cls.append(patterns[key][0])
        starts.append(start)
    ordered = sorted(patterns.values(), key=lambda z: z[0])
    rel_rows = np.stack([z[1] for z in ordered])
    meta = np.stack([np.asarray(cls, np.int32), np.asarray(starts, np.int32)])
    return meta, rel_rows, (onehot_c.astype(np.float32), col_valid)


def _na_bias_table(rpb, rel_rows, col_tables):
    onehot_c, col_valid = col_tables
    bc = jnp.einsum('hrc,cqk->hrqk', rpb.astype(F32), onehot_c, precision=lax.Precision.HIGHEST)
    bc = jnp.where(col_valid, bc, -jnp.inf)
    outside = jnp.full(bc.shape[:1] + bc.shape[2:], -jnp.inf, F32)
    pats = []
    for pat in rel_rows:
        qrows = [jnp.concatenate([bc[:, r] if r >= 0 else outside for r in row], axis=-1) for row in pat]
        pats.append(jnp.concatenate(qrows, axis=1))
    return jnp.stack(pats)


def _na_scores(q_pair, lane, h, k_parts):
    in_head = (lane < NA_DH) if h % 2 == 0 else (lane >= NA_DH)
    qm = jnp.where(in_head, q_pair * (NA_DH ** -0.5), 0.0).astype(BF16)
    return [lax.dot_general(qm, k, (((1,), (1,)), ((), ())), preferred_element_type=F32) for k in k_parts]


def _na_softmax(scores, bias):
    if bias is not None:
        scores = [scores[0] + bias] + scores[1:]
    m = scores[0].max(axis=-1, keepdims=True)
    for s in scores[1:]:
        m = jnp.maximum(m, s.max(axis=-1, keepdims=True))
    probs = [jnp.exp(s - m) for s in scores]
    den = probs[0].sum(axis=-1, keepdims=True)
    for p in probs[1:]:
        den = den + p.sum(axis=-1, keepdims=True)
    return [p.astype(BF16) for p in probs], den


def _na_kernel(meta_ref, q_ref, k_ref, v_ref, bias_ref, o_ref, *, n_lat, with_ctx):
    i = pl.program_id(1)
    nq = q_ref.shape[0]
    n_lat_tiles = n_lat // nq
    n_all = k_ref.shape[0]
    lane = lax.broadcasted_iota(jnp.int32, (nq, 128), 1)

    def run(windowed):
        key_rows = [pl.ds(n_lat, n_all - n_lat)]
        if windowed:
            start = pl.multiple_of(meta_ref[1, jnp.minimum(i, n_lat_tiles - 1)] * GRID_W, GRID_W)
            key_rows = [pl.ds(start, NA_SLAB * GRID_W)] + key_rows
        pair_lanes = [slice((h // 2) * 128, (h // 2 + 1) * 128) for h in range(NA_HEADS)]
        scores = [_na_scores(q_ref[:, pair_lanes[h]], lane, h, [k_ref[r, pair_lanes[h]] for r in key_rows])
                  for h in range(NA_HEADS)]
        soft = [_na_softmax(scores[h], bias_ref[h] if windowed else None) for h in range(NA_HEADS)]
        heads = []
        for h in range(NA_HEADS):
            probs, den = soft[h]
            out = None
            for p, r in zip(probs, key_rows):
                po = jnp.dot(p, v_ref[r, pair_lanes[h]], preferred_element_type=F32)
                out = po if out is None else out + po
            heads.append(out / den)
        outs = [jnp.where(lane < NA_DH, heads[2 * pair], heads[2 * pair + 1]) for pair in range(NA_HEADS // 2)]
        o_ref[...] = jnp.concatenate(outs, axis=-1).astype(o_ref.dtype)

    if with_ctx:
        pl.when(i < n_lat_tiles)(lambda: run(True))
        pl.when(i >= n_lat_tiles)(lambda: run(False))
    else:
        run(True)


def _na_attention(p, meta, table, n_lat, with_ctx_queries):
    b, s, _ = p.shape
    nq = NA_QROWS * GRID_W
    nk = NA_SLAB * GRID_W
    assert (s - n_lat) % nq == 0
    nlt = n_lat // nq
    qb = P_NA_QKV // NA_W
    grid_spec = pltpu.PrefetchScalarGridSpec(
        num_scalar_prefetch=1,
        grid=(b, s // nq if with_ctx_queries else nlt),
        in_specs=[pl.BlockSpec((None, nq, NA_W), lambda bi, i, m: (bi, i, qb)),
                  pl.BlockSpec((None, s, NA_W), lambda bi, i, m: (bi, 0, qb + 1)),
                  pl.BlockSpec((None, s, NA_W), lambda bi, i, m: (bi, 0, qb + 2)),
                  pl.BlockSpec((None, NA_HEADS, nq, nk),
                               lambda bi, i, m: (m[0, jnp.minimum(i, nlt - 1)], 0, 0, 0))],
        out_specs=pl.BlockSpec((None, nq, NA_W), lambda bi, i, m: (bi, i, 0)))
    return pl.pallas_call(
        functools.partial(_na_kernel, n_lat=n_lat, with_ctx=with_ctx_queries),
        grid_spec=grid_spec,
        out_shape=jax.ShapeDtypeStruct((b, s, NA_W), BF16),
        compiler_params=_cparams(("parallel", "arbitrary")),
        name="na_attention",
    )(meta, p, p, p, table)


def _tile_conv(x, prev, nxt, w_ref, tile_idx, n_tiles, n_lat_tiles):
    r = x.shape[0]
    width = w_ref.shape[0]
    left = width // 2
    has_prev = jnp.logical_and(tile_idx != 0, tile_idx != n_lat_tiles)
    has_next = jnp.logical_and(tile_idx != n_tiles - 1, tile_idx != n_lat_tiles - 1)
    prev = jnp.where(has_prev, prev, 0.0)
    nxt = jnp.where(has_next, nxt, 0.0)
    xe = jnp.concatenate([prev, x, nxt], axis=0)
    acc = None
    for j in range(width):
        o = HALO - left + j
        term = xe[o:o + r, :] * w_ref[j:j + 1, :]
        acc = term if acc is None else acc + term
    return acc


def _halo_specs(width, col_block, tile_of):
    per = SEQ_TILE // HALO

    def main(bi, s, *_):
        return (bi, tile_of(s), col_block)

    def prev(bi, s, *_):
        return (bi, jnp.maximum(tile_of(s) * per - 1, 0), col_block)

    def make_next(n_tiles):
        def nxt(bi, s, *_):
            return (bi, jnp.minimum((tile_of(s) + 1) * per, n_tiles * per - 1), col_block)
        return nxt

    return main, prev, make_next


def _gdn_prep_kernel(x_ref, xp_ref, xn_ref, ba_ref, cw_ref, ones_ref, exp_ref, alog_ref, dtb_ref,
                     q_ref, k_ref, v_ref, beta_ref, g_ref, *, n_lat_tiles):
    i = pl.program_id(1)
    y = _tile_conv(x_ref[...], xp_ref[...], xn_ref[...], cw_ref, i, pl.num_programs(1), n_lat_tiles)
    y = y * _sigmoid(y)
    q = y[:, :GDN_W]
    k = y[:, GDN_W:2 * GDN_W]
    v_ref[...] = y[:, 2 * GDN_W:].astype(v_ref.dtype)

    def head_norm(u):
        parts = jnp.concatenate(_split(u * u), axis=-1)
        ss = jnp.dot(parts, ones_ref[...], preferred_element_type=F32)
        return u * lax.rsqrt(ss + RMS_EPS)

    q_ref[...] = (head_norm(q) * (GDN_DK ** -0.5)).astype(q_ref.dtype)
    k_ref[...] = head_norm(k).astype(k_ref.dtype)

    ba = ba_ref[...]
    a = ba + dtb_ref[...]
    softplus = jnp.maximum(a, 0.0) + jnp.log1p(jnp.exp(-jnp.abs(a)))
    lane = lax.broadcasted_iota(jnp.int32, ba.shape, 1)
    compact = jnp.where(lane < 2 * GDN_HEADS, _sigmoid(ba), -jnp.exp(alog_ref[...]) * softplus)
    wide = jnp.dot(jnp.concatenate(_split(compact), axis=-1), exp_ref[...], preferred_element_type=F32)
    half = 2 * GDN_W
    beta_ref[...] = wide[:, :half].astype(beta_ref.dtype)
    g_ref[...] = wide[:, half:]


def _gdn_prep(p, conv_w, a_log, dt_bias, n_lat):
    b, s, _ = p.shape
    n_tiles = s // SEQ_TILE
    main, prev, make_next = _halo_specs(3 * GDN_W, 0, lambda t: t)
    ones_n = jnp.concatenate([_head_block_ones(GDN_HEADS, GDN_DK)] * N_SPLIT, axis=0)
    expand = np.zeros((128, 4 * GDN_W), np.float32)
    for kind in range(2):
        for d in range(2):
            for h in range(GDN_HEADS):
                c0 = kind * 2 * GDN_W + d * GDN_W + h * GDN_DV
                expand[kind * 2 * GDN_HEADS + d * GDN_HEADS + h, c0:c0 + GDN_DV] = 1.0
    expand_n = jnp.asarray(np.concatenate([expand] * N_SPLIT, axis=0), BF16)
    lanes = jnp.zeros((1, 128), F32)
    alog_e = lanes.at[0, 2 * GDN_HEADS:4 * GDN_HEADS].set(a_log.astype(F32).reshape(-1))
    dtb_e = lanes.at[0, 2 * GDN_HEADS:4 * GDN_HEADS].set(dt_bias.astype(F32).reshape(-1))
    tok = lambda w: pl.BlockSpec((None, SEQ_TILE, w), lambda bi, i: (bi, i, 0))
    return pl.pallas_call(
        functools.partial(_gdn_prep_kernel, n_lat_tiles=n_lat // SEQ_TILE),
        grid=(b, n_tiles),
        in_specs=[pl.BlockSpec((None, SEQ_TILE, 3 * GDN_W), main),
                  pl.BlockSpec((None, HALO, 3 * GDN_W), prev),
                  pl.BlockSpec((None, HALO, 3 * GDN_W), make_next(n_tiles)),
                  pl.BlockSpec((None, SEQ_TILE, 128), lambda bi, i: (bi, i, P_GDN_BA // 128)),
                  _full(conv_w), _full(ones_n), _full(expand_n), _full(alog_e), _full(dtb_e)],
        out_specs=[tok(GDN_W), tok(GDN_W), tok(GDN_W), tok(2 * GDN_W), tok(2 * GDN_W)],
        out_shape=[jax.ShapeDtypeStruct((b, s, GDN_W), BF16)] * 3 + [jax.ShapeDtypeStruct((b, s, 2 * GDN_W), BF16),
                                                                    jax.ShapeDtypeStruct((b, s, 2 * GDN_W), F32)],
        compiler_params=_cparams(("parallel", "parallel")),
        name="gdn_prep",
    )(p, p, p, p, conv_w, ones_n, expand_n, alog_e, dtb_e)


GDN_PAIR = 2 * GDN_DK


def _gdn_masks():
    c, w = GDN_CHUNK, GDN_W
    r2, c2 = np.arange(GDN_PAIR)[:, None], np.arange(GDN_PAIR)[None, :]
    bd = ((r2 // c) == (c2 // c)).astype(np.float32)
    i = np.arange(c)[:, None]
    j = (np.arange(w) % c)[None, :]
    level = np.zeros((c, w), np.int32)
    for bit in range(6):
        level += ((i ^ j) >= (1 << bit)).astype(np.int32)
    lvl = np.stack([(level == m).astype(np.float32) for m in range(7)])
    dirm = np.stack([np.stack([(j <= i), (j < i)]), np.stack([(j >= i), (j > i)])]).astype(np.float32)
    tj = (np.arange(N_SPLIT * c) % c)[None, :]
    tri = np.stack([(tj <= i), (tj >= i)]).astype(np.float32)
    return jnp.asarray(bd, BF16), jnp.asarray(lvl), jnp.asarray(dirm), jnp.asarray(tri, BF16)


def _heads_mm(x, y, bd, transpose_rhs=False):
    xb = x.astype(BF16)
    yb = y.astype(BF16)
    outs = []
    for pair in range(GDN_W // GDN_PAIR):
        ls = slice(pair * GDN_PAIR, (pair + 1) * GDN_PAIR)
        w = jnp.concatenate([yb[:, ls], yb[:, ls]], axis=0) * bd
        dims = (((1,), (1,)), ((), ())) if transpose_rhs else (((1,), (0,)), ((), ()))
        outs.append(lax.dot_general(xb[:, ls], w, dims, preferred_element_type=F32))
    return jnp.concatenate(outs, axis=1)


def _gdn_intra(probs, bd, lvl_ref, dirm_ref, tri_ref):
    c = GDN_CHUNK
    n = len(probs)
    eye = lvl_ref[0]
    gc, g_last, decay, gram = [], [], [], []
    for q, k, v, beta, g, rev in probs:
        d = 1 if rev else 0
        gcp = jnp.dot(tri_ref[d], jnp.concatenate(_split(g), axis=0), preferred_element_type=F32)
        gc.append(gcp)
        g_last.append(gcp[0:1, :] if rev else gcp[c - 1:c, :])
        gc_row = jnp.sum(gcp * eye, axis=0, keepdims=True)
        decay.append(dirm_ref[d, 0] * jnp.exp(jnp.minimum(gcp - gc_row, 0.0)))
        gram.append(_heads_mm(jnp.concatenate([k, q], axis=0), k, bd, transpose_rhs=True))
    lower = [dirm_ref[1 if p[5] else 0, 1] * p[3] * gram[x][:c] * decay[x] for x, p in enumerate(probs)]
    a_intra = [gram[x][c:] * decay[x] for x in range(n)]
    t = [eye - lower[x] * lvl_ref[1] for x in range(n)]
    for lev in range(2, 7):
        y = [_heads_mm(t[x], lower[x] * lvl_ref[lev], bd) for x in range(n)]
        z = [_heads_mm(y[x], t[x], bd) for x in range(n)]
        t = [t[x] - z[x] for x in range(n)]
    e_gc = [jnp.exp(gc[x]) for x in range(n)]
    u = [_heads_mm(t[x], p[2] * p[3], bd) for x, p in enumerate(probs)]
    w = [_heads_mm(t[x], p[1] * p[3] * e_gc[x], bd) for x, p in enumerate(probs)]
    wq = [jnp.concatenate([w[x], p[0] * e_gc[x]], axis=0).astype(BF16) for x, p in enumerate(probs)]
    k_dec = [(p[1] * jnp.exp(g_last[x] - gc[x])).astype(BF16) for x, p in enumerate(probs)]
    g_tot = [jnp.exp(g_last[x]) for x in range(n)]
    return u, wq, k_dec, a_intra, g_tot


def _gdn_state_steps(items, bd):
    c = GDN_CHUNK
    pairs = [slice(p * GDN_PAIR, (p + 1) * GDN_PAIR) for p in range(GDN_W // GDN_PAIR)]
    bdf = bd.astype(F32)
    ws_qs = [jnp.concatenate([jnp.dot(wq[:, ls], s_ref[p].astype(BF16), preferred_element_type=F32)
                              for p, ls in enumerate(pairs)], axis=1)
             for _, wq, _, _, _, s_ref in items]
    v_new = [it[0] - ws[:c] for it, ws in zip(items, ws_qs)]
    outs = [ws[c:] + _heads_mm(it[3], vn, bd) for it, ws, vn in zip(items, ws_qs, v_new)]
    for (_, _, k_dec, _, g_tot, s_ref), vn in zip(items, v_new):
        vb = vn.astype(BF16)
        for p, ls in enumerate(pairs):
            kv = lax.dot_general(k_dec[:, ls], vb[:, ls], (((0,), (0,)), ((), ())), preferred_element_type=F32)
            s_ref[p] = s_ref[p] * g_tot[:, ls] + kv * bdf
    return outs


def _gdn_scan_body(qf, kf, vf, bf, gf, qb, kb, vb, bb, gb, bd_ref, lvl_ref, dirm_ref, tri_ref,
                   of_ref, ob_ref, sf_ref, sb_ref):
    bd = bd_ref[...]
    n = SEQ_TILE // GDN_CHUNK
    probs, rows = [], []
    for c in range(n):
        rf = slice(c * GDN_CHUNK, (c + 1) * GDN_CHUNK)
        rb = slice((n - 1 - c) * GDN_CHUNK, (n - c) * GDN_CHUNK)
        probs.append(tuple(r[rf, :].astype(F32) for r in (qf, kf, vf, bf, gf)) + (False,))
        probs.append(tuple(r[rb, :].astype(F32) for r in (qb, kb, vb, bb, gb)) + (True,))
        rows += [rf, rb]
    u, wq, k_dec, a_intra, g_tot = _gdn_intra(probs, bd, lvl_ref, dirm_ref, tri_ref)
    for c in range(n):
        xs = (2 * c, 2 * c + 1)
        outs = _gdn_state_steps([(u[x], wq[x], k_dec[x], a_intra[x], g_tot[x], sb_ref if probs[x][5] else sf_ref)
                                 for x in xs], bd)
        for x, o in zip(xs, outs):
            (ob_ref if probs[x][5] else of_ref)[rows[x], :] = o.astype(of_ref.dtype)


def _seq_tile_maps(n_lat_tiles, n_tiles):
    fwd = lambda s: lax.rem(s + n_lat_tiles, n_tiles)
    bwd = lambda s: n_tiles - 1 - s
    return fwd, bwd


def _gdn_scan_specs(q, k, v, beta, g, n_lat):
    b, s, _ = q.shape
    n_tiles = s // SEQ_TILE
    fwd, bwd = _seq_tile_maps(n_lat // SEQ_TILE, n_tiles)
    spec = lambda tile_of, col: pl.BlockSpec((None, SEQ_TILE, GDN_W), lambda bi, t: (bi, tile_of(t), col))
    masks = _gdn_masks()
    in_specs = [spec(fwd, 0)] * 5 + [spec(bwd, 0)] * 3 + [spec(bwd, 1)] * 2 + [_full(m) for m in masks]
    assert len(in_specs) == N_GDN_IN
    return (in_specs, [q, k, v, beta, g, q, k, v, beta, g, *masks], [spec(fwd, 0), spec(bwd, 0)],
            [jax.ShapeDtypeStruct((b, s, GDN_W), BF16)] * 2,
            [pltpu.VMEM((GDN_W // GDN_PAIR, GDN_PAIR, GDN_PAIR), F32)] * 2)


def _lru_tile_prep(x_ref, xp_ref, xn_ref, cw_ref, cb_ref, wg_ref, bg_ref, nla_ref, a_ref, b_ref,
                   tile_idx, n_tiles, n_lat_tiles):
    xb = _tile_conv(x_ref[...], xp_ref[...], xn_ref[...], cw_ref, tile_idx, n_tiles, n_lat_tiles) + cb_ref[...]
    gates = _sigmoid(jnp.dot(xb.astype(BF16), wg_ref[...], preferred_element_type=F32) + bg_ref[...])
    log_a = nla_ref[...] * gates[:, :LRU_W]
    a_ref[...] = jnp.exp(log_a)
    th = jnp.tanh(log_a)
    b_ref[...] = jnp.sqrt(-2.0 * th / (1.0 - th)) * gates[:, LRU_W:] * xb


def _scan_group(a, b, h, row, reverse):
    for d in (1, 2, 4):
        if reverse:
            keep = row < 8 - d
            shift = 8 - d
        else:
            keep = row >= d
            shift = d
        a_s = jnp.where(keep, pltpu.roll(a, shift, 0), 1.0)
        b_s = jnp.where(keep, pltpu.roll(b, shift, 0), 0.0)
        b = a * b_s + b
        a = a * a_s
    return a * h + b


def _lru_scan_body(xf, xfp, xfn, xb, xbp, xbn, cw_ref, cb_ref, wgf, bgf, nlaf, wgb, bgb, nlab,
                   hf_ref, hb_ref, af_ref, bf_ref, ab_ref, bb_ref, cf_ref, cbk_ref, *, n_lat_tiles):
    s = pl.program_id(1)
    n_tiles = pl.num_programs(1)
    t_f, t_b = (m(s) for m in _seq_tile_maps(n_lat_tiles, n_tiles))
    _lru_tile_prep(xf, xfp, xfn, cw_ref, cb_ref, wgf, bgf, nlaf, af_ref, bf_ref, t_f, n_tiles, n_lat_tiles)
    _lru_tile_prep(xb, xbp, xbn, cw_ref, cb_ref, wgb, bgb, nlab, ab_ref, bb_ref, t_b, n_tiles, n_lat_tiles)

    n_groups = SEQ_TILE // 8
    row = lax.broadcasted_iota(jnp.int32, (8, LRU_W), 0)

    h_f, h_b = cf_ref[...], cbk_ref[...]
    pend_f, pend_b = None, None
    for gi in range(n_groups):
        rf = slice(gi * 8, gi * 8 + 8)
        rb = slice((n_groups - 1 - gi) * 8, (n_groups - gi) * 8)
        out_f = _scan_group(af_ref[rf, :], bf_ref[rf, :], h_f, row, False)
        out_b = _scan_group(ab_ref[rb, :], bb_ref[rb, :], h_b, row, True)
        if gi % 2 == 0:
            pend_f, pend_b = out_f, out_b
        else:
            hf_ref[(gi - 1) * 8:(gi + 1) * 8, :] = jnp.concatenate([pend_f, out_f], axis=0).astype(hf_ref.dtype)
            hb_ref[(n_groups - 1 - gi) * 8:(n_groups + 1 - gi) * 8, :] = (
                jnp.concatenate([out_b, pend_b], axis=0).astype(hb_ref.dtype))
        h_f = jnp.broadcast_to(out_f[7:8, :], (8, LRU_W))
        h_b = jnp.broadcast_to(out_b[0:1, :], (8, LRU_W))
    cf_ref[...] = h_f
    cbk_ref[...] = h_b


N_LRU_IN, N_GDN_IN, N_LRU_SCRATCH = 14, 14, 6


def _seq_mixers_kernel(*refs, n_lat_tiles):
    lru_in = refs[:N_LRU_IN]
    gdn_in = refs[N_LRU_IN:N_LRU_IN + N_GDN_IN]
    hf_ref, hb_ref, of_ref, ob_ref = refs[N_LRU_IN + N_GDN_IN:N_LRU_IN + N_GDN_IN + 4]
    scratch = refs[N_LRU_IN + N_GDN_IN + 4:]
    lru_scratch, gdn_scratch = scratch[:N_LRU_SCRATCH], scratch[N_LRU_SCRATCH:]

    @pl.when(pl.program_id(1) == 0)
    def _():
        for ref in lru_scratch[4:] + gdn_scratch:
            ref[...] = jnp.zeros(ref.shape, ref.dtype)

    _lru_scan_body(*lru_in, hf_ref, hb_ref, *lru_scratch, n_lat_tiles=n_lat_tiles)
    _gdn_scan_body(*gdn_in, of_ref, ob_ref, *gdn_scratch)


def _lru_scan_specs(p, conv_w, conv_b, w_r, b_r, w_i, b_i, lam, n_lat):
    b, s, _ = p.shape
    n_tiles = s // SEQ_TILE
    nlt = n_lat // SEQ_TILE
    fwd, bwd = _seq_tile_maps(nlt, n_tiles)
    col = P_LRU_X // LRU_W
    specs = []
    for tile_of in (fwd, bwd):
        main, prev, make_next = _halo_specs(LRU_W, col, tile_of)
        specs += [pl.BlockSpec((None, SEQ_TILE, LRU_W), main), pl.BlockSpec((None, HALO, LRU_W), prev),
                  pl.BlockSpec((None, HALO, LRU_W), make_next(n_tiles))]

    def blockdiag(w):
        return jax.scipy.linalg.block_diag(*[w[n] for n in range(LRU_BLOCKS)])

    dir_args = []
    for d in range(2):
        wg = jnp.concatenate([blockdiag(w_r[d]), blockdiag(w_i[d])], axis=1).astype(BF16)
        bg = jnp.concatenate([b_r[d], b_i[d]]).astype(F32).reshape(1, 2 * LRU_W)
        nla = (-LRU_C * jax.nn.softplus(-lam[d].astype(F32))).reshape(1, LRU_W)
        dir_args += [wg, bg, nla]
    cb2 = conv_b.reshape(1, LRU_W)
    out_spec = lambda tile_of: pl.BlockSpec((None, SEQ_TILE, LRU_W), lambda bi, t: (bi, tile_of(t), 0))
    in_specs = specs + [_full(conv_w), _full(cb2)] + [_full(a) for a in dir_args]
    assert len(in_specs) == N_LRU_IN
    return (in_specs, [p, p, p, p, p, p, conv_w, cb2, *dir_args], [out_spec(fwd), out_spec(bwd)],
            [jax.ShapeDtypeStruct((b, s, LRU_W), BF16)] * 2,
            [pltpu.VMEM((SEQ_TILE, LRU_W), F32)] * 4 + [pltpu.VMEM((8, LRU_W), F32)] * 2)


def _seq_mixers(lru_parts, gdn_parts, bsz, n_tiles, n_lat):
    in_specs, args, out_specs, out_shape, scratch = ([*a, *b] for a, b in zip(lru_parts, gdn_parts))
    return pl.pallas_call(
        functools.partial(_seq_mixers_kernel, n_lat_tiles=n_lat // SEQ_TILE),
        grid=(bsz, n_tiles),
        in_specs=in_specs,
        out_specs=out_specs,
        out_shape=out_shape,
        scratch_shapes=scratch,
        compiler_params=_cparams(("parallel", "arbitrary")),
        name="seq_mixers",
    )(*args)


def _merge_kernel(x_ref, mc_ref, ml_ref, of_ref, ob_ref, z_ref, hf_ref, hb_ref, y_ref, uc_ref, ud_ref,
                  t0_ref, t1_ref, t2_ref, t3_ref, gn_ref, ones_ref, bg_ref, wb_ref, wo_ref, o_ref, *, n_lat):
    project = lambda u, n: jnp.dot(u, wb_ref[n], preferred_element_type=F32)
    o = of_ref[...].astype(F32) + ob_ref[...].astype(F32)
    ms = jnp.dot(jnp.concatenate(_split(o * o), axis=-1), ones_ref[...], preferred_element_type=F32) * (1.0 / GDN_DV)
    proj = {2: project(uc_ref[...], 2), 3: project(ud_ref[...], 3)}
    z = z_ref[...]
    ua = (o * lax.rsqrt(ms + RMS_EPS) * gn_ref[...] * (z * _sigmoid(z))).astype(BF16)
    y = y_ref[...]
    gelu = 0.5 * y * (1.0 + jnp.tanh(0.7978845608028654 * (y + 0.044715 * (y * y * y))))
    ub = ((hf_ref[...].astype(F32) + hb_ref[...].astype(F32)) * gelu).astype(BF16)
    proj[0] = project(ua, 0)
    proj[1] = project(ub, 1)

    merged = None
    for n, t_ref in enumerate((t0_ref, t1_ref, t2_ref, t3_ref)):
        gate = _sigmoid(t_ref[...] + bg_ref[n:n + 1, :].astype(BF16))
        term = gate.astype(F32) * proj[n]
        merged = term if merged is None else merged + term
    out = jnp.dot(merged.astype(BF16), wo_ref[...], preferred_element_type=F32)
    g1 = _row_select(x_ref.shape[0], pl.program_id(1), n_lat, mc_ref[2:3, :], ml_ref[2:3, :])
    o_ref[...] = x_ref[...] + g1 * out


def _merge(xs, mod_c, mod_l, gdn_out, lru_out, uc, ud, p32, p16, gdn_norm_g, b_gate, w_branch, w_out, layer, n_lat,
           with_ctx):
    b, s, d = xs.shape
    rows = s if with_ctx else n_lat
    tm = _pick_tile(rows, 544)
    gate_blk = P_GATES // d
    tok = lambda w: pl.BlockSpec((None, tm, w), lambda bi, i: (bi, i, 0))
    ones_n = jnp.concatenate([_head_block_ones(GDN_HEADS, GDN_DV)] * N_SPLIT, axis=0)
    gn = jnp.tile(gdn_norm_g.astype(F32), GDN_HEADS).reshape(1, GDN_W)
    pcol = lambda c: pl.BlockSpec((None, tm, BRANCH_W), lambda bi, i: (bi, i, c))
    in_specs = [tok(d), pl.BlockSpec((8, d), lambda bi, i: (0, 0)), pl.BlockSpec((None, 8, d), lambda bi, i: (bi, 0, 0))]
    in_specs += [tok(BRANCH_W), tok(BRANCH_W), pcol(P_GDN_Z // GDN_W), tok(BRANCH_W), tok(BRANCH_W),
                 pcol(P_LRU_Y // LRU_W), tok(BRANCH_W), tok(BRANCH_W)]
    in_specs += [pl.BlockSpec((None, tm, d), functools.partial(lambda bi, i, n: (bi, i, gate_blk + n), n=n))
                 for n in range(N_BRANCH)]
    in_specs += [_full(gn), _full(ones_n), _full(b_gate),
                 pl.BlockSpec((None,) + w_branch.shape[1:], lambda bi, i: (layer, 0, 0, 0)),
                 pl.BlockSpec((None,) + w_out.shape[1:], lambda bi, i: (layer, 0, 0))]
    return pl.pallas_call(
        functools.partial(_merge_kernel, n_lat=n_lat),
        grid=(b, rows // tm),
        in_specs=in_specs,
        out_specs=tok(d),
        out_shape=jax.ShapeDtypeStruct((b, rows, d), F32),
        input_output_aliases={0: 0} if with_ctx else {},
        compiler_params=_cparams(("parallel", "parallel")),
        name="merge",
    )(xs, mod_c, mod_l, gdn_out[0], gdn_out[1], p32, lru_out[0], lru_out[1], p32, uc, ud, p16, p16, p16, p16,
      gn, ones_n, b_gate, w_branch, w_out)


def _mlp_kernel(x_ref, mc_ref, ml_ref, gn_ref, gf_ref, w1_ref, w2_ref, o_ref, h_ref, acc_ref, *, n_lat, final_norm):
    f = pl.program_id(2)
    tm = x_ref.shape[0]
    i = pl.program_id(1)

    @pl.when(f == 0)
    def _():
        _norm_modulate(x_ref, h_ref, gn_ref, mc_ref, ml_ref, 3, i, n_lat)
        acc_ref[...] = jnp.zeros(acc_ref.shape, F32)

    a = jnp.maximum(jnp.dot(h_ref[...], w1_ref[...], preferred_element_type=F32), 0.0)
    acc_ref[...] += jnp.dot((a * a).astype(BF16), w2_ref[...], preferred_element_type=F32)

    @pl.when(f == pl.num_programs(2) - 1)
    def _():
        g2 = _row_select(tm, i, n_lat, mc_ref[5:6, :], ml_ref[5:6, :])
        y = x_ref[...] + g2 * acc_ref[...]
        if final_norm:
            y = y * lax.rsqrt(jnp.mean(y * y, axis=-1, keepdims=True) + RMS_EPS) * gf_ref[...]
        o_ref[...] = y


def _mlp(xs, mod_c, mod_l, gain, w1, w2, layer, final_gain, n_lat, final_norm):
    b, rows, d = xs.shape
    dff = w1.shape[-1]
    tm = _pick_tile(rows, 1088)
    tf = 1024
    row = pl.BlockSpec((1, d), lambda bi, i, f: (0, 0))
    return pl.pallas_call(
        functools.partial(_mlp_kernel, n_lat=n_lat, final_norm=final_norm),
        grid=(b, rows // tm, dff // tf),
        in_specs=[pl.BlockSpec((None, tm, d), lambda bi, i, f: (bi, i, 0)),
                  pl.BlockSpec((8, d), lambda bi, i, f: (0, 0)),
                  pl.BlockSpec((None, 8, d), lambda bi, i, f: (bi, 0, 0)),
                  row, row,
                  pl.BlockSpec((None, d, tf), lambda bi, i, f: (layer, 0, f)),
                  pl.BlockSpec((None, tf, d), lambda bi, i, f: (layer, f, 0))],
        out_specs=pl.BlockSpec((None, tm, d), lambda bi, i, f: (bi, i, 0)),
        out_shape=jax.ShapeDtypeStruct((b, rows, d), F32),
        scratch_shapes=[pltpu.VMEM((tm, d), BF16), pltpu.VMEM((tm, d), F32)],
        compiler_params=_cparams(("parallel", "parallel", "arbitrary")),
        name="mlp",
    )(xs, mod_c, mod_l, gain, final_gain, w1, w2)


def kernel(x, c, ctx, c_ctx, mod_w, mod_b, norm1_g, norm2_g, w_in, b_gate, gdn_conv_w, gdn_a_log, gdn_dt_bias,
           gdn_norm_g, lru_conv_w, lru_conv_b, lru_w_r, lru_b_r, lru_w_i, lru_b_i, lru_lambda, mla_q_norm_g,
           mla_w_uq, mla_kv_norm_g, mla_w_ukv, na_rpb, w_branch, w_out, mlp_w1, mlp_w2, final_norm_g):
    bsz, n_tok, d = x.shape
    n_ctx = ctx.shape[1]
    depth = w_in.shape[0]
    assert n_ctx % SEQ_TILE == 0 and n_tok % SEQ_TILE == 0 and n_tok % GRID_W == 0
    na_meta, na_rel_rows, na_cols = _na_geometry(n_tok // GRID_W)
    na_meta = jnp.asarray(na_meta)
    cos, sin = _rope_tables(n_tok, n_ctx)

    n_rows = -(-(bsz + 1) // 8) * 8
    cc = jnp.zeros((n_rows, d), F32).at[:bsz].set(c).at[bsz].set(c_ctx)
    final_gain = final_norm_g.reshape(1, d)

    w_in_all = _arrange_w_in(w_in)
    wb_all = w_branch.astype(BF16)
    wo_all = w_out.astype(BF16)
    w1_all = mlp_w1.astype(BF16)
    w2_all = mlp_w2.astype(BF16)

    xs = jnp.concatenate([x, ctx], axis=1)
    for l in range(depth):
        need_ctx = l < depth - 1
        mod = _modulation(cc, mod_w, mod_b[l], l).reshape(n_rows, N_MOD, d)
        pad = jnp.zeros((8 - N_MOD, d), F32)
        mod_c = jnp.concatenate([mod[bsz], pad], axis=0)
        mod_l = jnp.concatenate([mod[:bsz], jnp.broadcast_to(pad, (bsz, 8 - N_MOD, d))], axis=1)

        wq, wk, wv, place = _arrange_mla(mla_w_uq[l], mla_w_ukv[l])
        gq = mla_q_norm_g[l].reshape(1, -1)
        gkv = mla_kv_norm_g[l].reshape(1, -1)
        g1n = norm1_g[l].reshape(1, d)
        g2n = norm2_g[l].reshape(1, d)

        p32, p16 = _inproj(xs, mod_c, mod_l, g1n, w_in_all, l, n_tok)

        gq_, gk_, gv_, gbeta, gg = _gdn_prep(p32, gdn_conv_w[l], gdn_a_log[l], gdn_dt_bias[l], n_tok)
        h_f, h_b, o_f, o_b = _seq_mixers(
            _lru_scan_specs(p32, lru_conv_w[l], lru_conv_b[l], lru_w_r[l], lru_b_r[l], lru_w_i[l], lru_b_i[l],
                            lru_lambda[l], n_tok),
            _gdn_scan_specs(gq_, gk_, gv_, gbeta, gg, n_tok), bsz, (n_tok + n_ctx) // SEQ_TILE, n_tok)

        mq, mk, mv = _mla_prep(p32, p16, cos, sin, gq, gkv, wq, wk, wv, place)
        uc = _mla_flash(mq, mk, mv, n_tok)
        if need_ctx:
            uc = _mla_flash(mq, mk, mv, n_tok, prev=uc)
        ud = _na_attention(p16, na_meta, _na_bias_table(na_rpb[l], na_rel_rows, na_cols), n_tok, need_ctx)

        xs = _merge(xs, mod_c, mod_l, (o_f, o_b), (h_f, h_b), uc, ud, p32, p16, gdn_norm_g[l], b_gate[l],
                    wb_all, wo_all, l, n_tok, need_ctx)
        xs = _mlp(xs, mod_c, mod_l, g2n, w1_all, w2_all, l, final_gain, n_tok, l == depth - 1)
    return xs
```

```python
import functools

import jax
import jax.numpy as jnp
import numpy as np
from jax import lax
from jax.experimental import pallas as pl
from jax.experimental.pallas import tpu as pltpu

F32 = jnp.float32
BF16 = jnp.bfloat16

GRID_W = 64
N_MOD = 6
RMS_EPS = 1e-6
GDN_HEADS = 4
GDN_DK = 64
GDN_DV = 64
GDN_CHUNK = 64
GDN_W = GDN_HEADS * GDN_DV
LRU_W = 256
LRU_BLOCKS = 4
LRU_C = 8.0
MLA_HEADS = 4
MLA_Q_RANK = 256
MLA_KV_RANK = 128
MLA_NOPE = 64
MLA_ROPE = 32
MLA_V = 64
MLA_SLOT = 128
ROPE_BASE = 10000.0
LOG2_E = 1.4426950408889634
NA_HEADS = 4
NA_DH = 64
NA_W = NA_HEADS * NA_DH
NA_WIN_ROWS = 8
NA_WIN_COLS = 16
N_BRANCH = 4
BRANCH_W = 256

SEQ_TILE = 256
HALO = 8

_REF_COLS = {}
_off = 0
for _name, _w in (('gdn_qkv', 3 * GDN_W), ('gdn_z', GDN_W), ('gdn_beta', 2 * GDN_HEADS), ('gdn_a', 2 * GDN_HEADS),
                  ('lru_x', LRU_W), ('lru_y', LRU_W), ('mla_q', MLA_Q_RANK), ('mla_kv', MLA_KV_RANK),
                  ('mla_kr', MLA_ROPE), ('na_qkv', 3 * NA_W)):
    _REF_COLS[_name] = (_off, _w)
    _off += _w
N_MIX_COLS = _off

P_GDN_QKV = 0
P_GDN_Z = 768
P_LRU_X = 1024
P_LRU_Y = 1280
P_MLA_KV = 1536
P_MLA_KR = 1664
P_GDN_BA = 1792
P32_COLS = 2048
P_MLA_Q = 0
P_NA_QKV = 256
P_GATES = 1024
P16_COLS = P_GATES + N_BRANCH * 1024
PROJ_TILE = 1024

V7X_VMEM_BYTES = 64 * 1024 * 1024
VMEM_LIMIT = V7X_VMEM_BYTES - 12 * 1024 * 1024


def _cparams(sem):
    return pltpu.CompilerParams(dimension_semantics=sem, vmem_limit_bytes=VMEM_LIMIT)


def _pick_tile(n, cap):
    best = 8
    for t in range(8, min(n, cap) + 1, 8):
        if n % t == 0:
            best = t
    return best


def _full(a):
    return pl.BlockSpec(a.shape, lambda *_: (0,) * a.ndim)


N_SPLIT = 2


def _split(x):
    hi = x.astype(BF16)
    lo = (x - hi.astype(F32)).astype(BF16)
    return hi, lo


def _sigmoid(x):
    return 0.5 * jnp.tanh(0.5 * x) + 0.5


def _arrange_w_in(w_in):
    pieces, pos = [], 0

    def put(dst, block):
        nonlocal pos
        if dst > pos:
            pieces.append(jnp.zeros(w_in.shape[:-1] + (dst - pos,), w_in.dtype))
        pieces.append(block)
        pos = dst + block.shape[-1]

    ref = lambda name: w_in[..., _REF_COLS[name][0]:_REF_COLS[name][0] + _REF_COLS[name][1]]
    for name, dst in (('gdn_qkv', P_GDN_QKV), ('gdn_z', P_GDN_Z), ('lru_x', P_LRU_X), ('lru_y', P_LRU_Y),
                      ('mla_kv', P_MLA_KV), ('mla_kr', P_MLA_KR), ('gdn_beta', P_GDN_BA),
                      ('gdn_a', P_GDN_BA + 2 * GDN_HEADS)):
        put(dst, ref(name))
    put(P32_COLS + P_MLA_Q, ref('mla_q'))
    put(P32_COLS + P_NA_QKV, ref('na_qkv'))
    put(P32_COLS + P_GATES, w_in[..., N_MIX_COLS:])
    assert pos == P32_COLS + P16_COLS
    return jnp.concatenate(pieces, axis=-1).astype(BF16)


def _rope_perm():
    q = MLA_ROPE // 4
    src = np.zeros(MLA_ROPE, np.int32)
    sign = np.zeros(MLA_ROPE, np.float32)
    for base in (0, 2 * q):
        for d in range(q):
            src[base + d] = base + d + q
            sign[base + d] = -1.0
            src[base + q + d] = base + d
            sign[base + q + d] = 1.0
    return src, sign


def _arrange_mla(w_uq, w_ukv):
    src, sign = _rope_perm()
    hq = MLA_NOPE + MLA_ROPE
    wq = jnp.zeros((MLA_Q_RANK, 2 * MLA_HEADS * MLA_SLOT), F32)
    wk = jnp.zeros((MLA_KV_RANK, MLA_HEADS * MLA_SLOT), F32)
    wv = jnp.zeros((MLA_KV_RANK, MLA_HEADS * MLA_V), F32)
    place = np.zeros((2 * MLA_SLOT, 2 * MLA_HEADS * MLA_SLOT), np.float32)
    rot_off = MLA_HEADS * MLA_SLOT
    for h in range(MLA_HEADS):
        nope = w_uq[:, h * hq:h * hq + MLA_NOPE]
        pe = w_uq[:, h * hq + MLA_NOPE:(h + 1) * hq]
        s = h * MLA_SLOT
        wq = wq.at[:, s:s + MLA_NOPE].set(nope)
        wq = wq.at[:, s + MLA_NOPE:s + MLA_NOPE + MLA_ROPE].set(pe)
        wq = wq.at[:, rot_off + s + MLA_NOPE:rot_off + s + MLA_NOPE + MLA_ROPE].set(pe[:, src] * sign)
        wk = wk.at[:, s:s + MLA_NOPE].set(w_ukv[:, h * 128:h * 128 + MLA_NOPE])
        wv = wv.at[:, h * MLA_V:(h + 1) * MLA_V].set(w_ukv[:, h * 128 + MLA_NOPE:(h + 1) * 128])
        for d in range(MLA_ROPE):
            for half in (0, MLA_SLOT):
                place[half + d, s + MLA_NOPE + d] = 1.0
                place[half + src[d], rot_off + s + MLA_NOPE + d] = sign[d]
    return wq.astype(BF16), wk.astype(BF16), wv.astype(BF16), jnp.asarray(place, BF16)


def _rope_tables(n_tok, n_ctx):
    cos = np.ones((n_tok + n_ctx, MLA_SLOT), np.float32)
    sin = np.zeros((n_tok + n_ctx, MLA_SLOT), np.float32)
    t = np.arange(n_tok)
    row = (t // GRID_W).astype(np.float32)
    col = (t % GRID_W).astype(np.float32)
    n_freq = MLA_ROPE // 4
    inv = (ROPE_BASE ** (-np.arange(n_freq, dtype=np.float32) / n_freq)).astype(np.float32)
    ar = row[:, None] * inv
    ac = col[:, None] * inv
    ang = np.concatenate([ar, ar, ac, ac], axis=-1).astype(np.float32)
    cos[:n_tok, MLA_NOPE:MLA_NOPE + MLA_ROPE] = np.cos(ang)
    sin[:n_tok, MLA_NOPE:MLA_NOPE + MLA_ROPE] = np.sin(ang)
    return jnp.asarray(cos), jnp.asarray(sin)


def _head_block_ones(n_heads, width):
    m = np.kron(np.eye(n_heads, dtype=np.float32), np.ones((width, width), np.float32))
    return jnp.asarray(m, BF16)


def _mod_kernel(c_ref, w_ref, b_ref, o_ref):
    c = c_ref[...]
    s = c * _sigmoid(c)
    o_ref[...] = jnp.dot(s, w_ref[...], preferred_element_type=F32) + b_ref[...]


def _modulation(cc, mod_w, mod_b, layer):
    r, d = cc.shape
    n = mod_w.shape[-1]
    tn = 1024
    return pl.pallas_call(
        _mod_kernel,
        grid=(n // tn,),
        in_specs=[pl.BlockSpec((r, d), lambda j: (0, 0)),
                  pl.BlockSpec((None, d, tn), lambda j: (layer, 0, j)),
                  pl.BlockSpec((1, tn), lambda j: (0, j))],
        out_specs=pl.BlockSpec((r, tn), lambda j: (0, j)),
        out_shape=jax.ShapeDtypeStruct((r, n), F32),
        compiler_params=_cparams(("arbitrary",)),
        name="modulation",
    )(cc, mod_w, mod_b.reshape(1, n))


def _row_select(tile_rows, tile_idx, n_lat, ctx_vec, lat_vec):
    row = tile_idx * tile_rows + lax.broadcasted_iota(jnp.int32, (tile_rows, 1), 0)
    return jnp.where(row < n_lat, lat_vec, ctx_vec)


def _norm_modulate(x_ref, h_ref, gain_ref, mc_ref, ml_ref, shift_row, tile_idx, n_lat):
    tm = x_ref.shape[0]
    x = x_ref[...]
    xn = x * lax.rsqrt(jnp.mean(x * x, axis=-1, keepdims=True) + RMS_EPS)
    sh_l, sh_c = ml_ref[shift_row:shift_row + 1, :], mc_ref[shift_row:shift_row + 1, :]
    amp_l = gain_ref[...] * (1.0 + ml_ref[shift_row + 1:shift_row + 2, :])
    amp_c = gain_ref[...] * (1.0 + mc_ref[shift_row + 1:shift_row + 2, :])
    all_latent = (tile_idx + 1) * tm <= n_lat

    @pl.when(all_latent)
    def _():
        h_ref[...] = (xn * amp_l + sh_l).astype(h_ref.dtype)

    @pl.when(jnp.logical_not(all_latent))
    def _():
        amp = _row_select(tm, tile_idx, n_lat, amp_c, amp_l)
        shift = _row_select(tm, tile_idx, n_lat, sh_c, sh_l)
        h_ref[...] = (xn * amp + shift).astype(h_ref.dtype)


def _inproj_first_kernel(x_ref, mc_ref, ml_ref, g_ref, w_ref, o_ref, h_ref, *, n_lat):
    @pl.when(pl.program_id(2) == 0)
    def _():
        _norm_modulate(x_ref, h_ref, g_ref, mc_ref, ml_ref, 0, pl.program_id(1), n_lat)

    o_ref[...] = jnp.dot(h_ref[...], w_ref[...], preferred_element_type=F32).astype(o_ref.dtype)


def _inproj_second_kernel(h_ref, w_ref, o_ref):
    o_ref[...] = jnp.dot(h_ref[...], w_ref[...], preferred_element_type=F32).astype(o_ref.dtype)


def _inproj(xs, mod_c, mod_l, gain, w, layer, n_lat):
    b, s, d = xs.shape
    tm = _pick_tile(s, 1088)
    tn = PROJ_TILE
    n32 = P32_COLS // tn
    p16, h = pl.pallas_call(
        functools.partial(_inproj_first_kernel, n_lat=n_lat),
        grid=(b, s // tm, P16_COLS // tn),
        in_specs=[pl.BlockSpec((None, tm, d), lambda bi, i, j: (bi, i, 0)),
                  pl.BlockSpec((8, d), lambda bi, i, j: (0, 0)),
                  pl.BlockSpec((None, 8, d), lambda bi, i, j: (bi, 0, 0)),
                  pl.BlockSpec((1, d), lambda bi, i, j: (0, 0)),
                  pl.BlockSpec((None, d, tn), lambda bi, i, j: (layer, 0, j + n32))],
        out_specs=[pl.BlockSpec((None, tm, tn), lambda bi, i, j: (bi, i, j)),
                   pl.BlockSpec((None, tm, d), lambda bi, i, j: (bi, i, 0))],
        out_shape=[jax.ShapeDtypeStruct((b, s, P16_COLS), BF16), jax.ShapeDtypeStruct((b, s, d), BF16)],
        compiler_params=_cparams(("parallel", "parallel", "arbitrary")),
        name="inproj_bf16",
    )(xs, mod_c, mod_l, gain, w)
    tm2 = _pick_tile(s, 2 * tm)
    p32 = pl.pallas_call(
        _inproj_second_kernel,
        grid=(b, s // tm2, n32),
        in_specs=[pl.BlockSpec((None, tm2, d), lambda bi, i, j: (bi, i, 0)),
                  pl.BlockSpec((None, d, tn), lambda bi, i, j: (layer, 0, j))],
        out_specs=pl.BlockSpec((None, tm2, tn), lambda bi, i, j: (bi, i, j)),
        out_shape=jax.ShapeDtypeStruct((b, s, P32_COLS), F32),
        compiler_params=_cparams(("parallel", "parallel", "arbitrary")),
        name="inproj_f32",
    )(h, w)
    return p32, p16


def _mla_prep_kernel(ql_ref, kv_ref, cos_ref, sin_ref, gq_ref, gkv_ref, wq_ref, wk_ref, wv_ref, pl_ref,
                     q_ref, k_ref, v_ref, *, scale):
    nslot = MLA_HEADS * MLA_SLOT
    cos = jnp.concatenate([cos_ref[...]] * MLA_HEADS, axis=-1)
    sin = jnp.concatenate([sin_ref[...]] * MLA_HEADS, axis=-1)

    ql = ql_ref[...].astype(F32)
    qn = ql * lax.rsqrt(jnp.mean(ql * ql, axis=-1, keepdims=True) + RMS_EPS) * gq_ref[...]
    q2 = jnp.dot(qn.astype(BF16), wq_ref[...], preferred_element_type=F32)
    q = (q2[:, :nslot] * cos + q2[:, nslot:] * sin) * scale
    q_ref[...] = q.astype(BF16)

    kvkr = kv_ref[...]
    kvl = kvkr[:, :MLA_KV_RANK]
    kr = kvkr[:, MLA_KV_RANK:]
    kvn = (kvl * lax.rsqrt(jnp.mean(kvl * kvl, axis=-1, keepdims=True) + RMS_EPS) * gkv_ref[...]).astype(BF16)
    kn = jnp.dot(kvn, wk_ref[...], preferred_element_type=F32)
    v_ref[...] = jnp.dot(kvn, wv_ref[...], preferred_element_type=F32).astype(BF16)
    kr_hi = kr.astype(BF16)
    kr_lo = (kr - kr_hi.astype(F32)).astype(BF16)
    kr2 = jnp.dot(jnp.concatenate([kr_hi, kr_lo], axis=-1), pl_ref[...], preferred_element_type=F32)
    k_ref[...] = (kn + kr2[:, :nslot] * cos + kr2[:, nslot:] * sin).astype(BF16)


def _mla_prep(p32, p16, cos, sin, gq, gkv, wq, wk, wv, place):
    b, s, _ = p32.shape
    tm = _pick_tile(s, 544)
    nslot = MLA_HEADS * MLA_SLOT
    scale = (MLA_NOPE + MLA_ROPE) ** -0.5 * LOG2_E
    return pl.pallas_call(
        functools.partial(_mla_prep_kernel, scale=scale),
        grid=(b, s // tm),
        in_specs=[pl.BlockSpec((None, tm, MLA_Q_RANK), lambda bi, i: (bi, i, P_MLA_Q // MLA_Q_RANK)),
                  pl.BlockSpec((None, tm, 2 * MLA_SLOT), lambda bi, i: (bi, i, P_MLA_KV // (2 * MLA_SLOT))),
                  pl.BlockSpec((tm, MLA_SLOT), lambda bi, i: (i, 0)),
                  pl.BlockSpec((tm, MLA_SLOT), lambda bi, i: (i, 0)),
                  _full(gq), _full(gkv), _full(wq), _full(wk), _full(wv), _full(place)],
        out_specs=[pl.BlockSpec((None, tm, nslot), lambda bi, i: (bi, i, 0)),
                   pl.BlockSpec((None, tm, nslot), lambda bi, i: (bi, i, 0)),
                   pl.BlockSpec((None, tm, MLA_HEADS * MLA_V), lambda bi, i: (bi, i, 0))],
        out_shape=[jax.ShapeDtypeStruct((b, s, nslot), BF16),
                   jax.ShapeDtypeStruct((b, s, nslot), BF16),
                   jax.ShapeDtypeStruct((b, s, MLA_HEADS * MLA_V), BF16)],
        compiler_params=_cparams(("parallel", "parallel")),
        name="mla_prep",
    )(p16, p32, cos, sin, gq, gkv, wq, wk, wv, place)


def _flash_softmax(h, s, m_ref, l_ref, first):
    row_max = jnp.max(s, axis=-1, keepdims=True)
    m_new = jnp.broadcast_to(row_max, m_ref.shape[1:]) if first else jnp.maximum(m_ref[h], row_max)
    p = jnp.exp2(s - jnp.concatenate([m_new] * (s.shape[1] // 128), axis=-1))
    row_sum = jnp.sum(p, axis=-1, keepdims=True)
    if first:
        alpha = None
        l_ref[h] = jnp.broadcast_to(row_sum, l_ref.shape[1:])
    else:
        alpha = jnp.exp2(m_ref[h] - m_new)
        l_ref[h] = alpha * l_ref[h] + row_sum
    m_ref[h] = m_new
    return alpha, p.astype(BF16)


def _mla_flash_kernel(*refs, tk, aliased):
    if aliased:
        q_ref, k_ref, v_ref, _, o_ref, m_ref, l_ref, acc_ref = refs
    else:
        q_ref, k_ref, v_ref, o_ref, m_ref, l_ref, acc_ref = refs
    tq = q_ref.shape[0]
    nk = k_ref.shape[0]
    n_loop = nk // tk

    def chunk(rows, first=False):
        heads = range(MLA_HEADS)
        hs = [slice(h * MLA_SLOT, (h + 1) * MLA_SLOT) for h in heads]
        vs = [slice((h // 2) * 128, (h // 2) * 128 + 128) for h in heads]
        s = [lax.dot_general(q_ref[:, hs[h]], k_ref[rows, hs[h]], (((1,), (1,)), ((), ())),
                             preferred_element_type=F32) for h in heads]
        ap = [_flash_softmax(h, s[h], m_ref, l_ref, first) for h in heads]
        for h in heads:
            alpha, p = ap[h]
            pv = jnp.dot(p, v_ref[rows, vs[h]], preferred_element_type=F32)
            acc_ref[h] = pv if first else alpha * acc_ref[h] + pv

    has_tail = nk > n_loop * tk
    if has_tail:
        chunk(slice(n_loop * tk, nk), first=True)
    else:
        chunk(slice(0, tk), first=True)
    lo = 0 if has_tail else 1
    if n_loop > lo:
        def body(j, carry):
            chunk(pl.ds(pl.multiple_of(j * tk, tk), tk))
            return carry
        lax.fori_loop(lo, n_loop, body, 0)

    lane = lax.broadcasted_iota(jnp.int32, (tq, 128), 1)
    outs = []
    for pair in range(MLA_HEADS // 2):
        o0 = acc_ref[2 * pair] / l_ref[2 * pair]
        o1 = acc_ref[2 * pair + 1] / l_ref[2 * pair + 1]
        outs.append(jnp.where(lane < MLA_V, o0, o1))
    o_ref[...] = jnp.concatenate(outs, axis=-1).astype(o_ref.dtype)


def _mla_flash(q, k, v, n_lat, prev=None):
    b, s, nslot = q.shape
    n_ctx = s - n_lat
    nv = MLA_HEADS * MLA_V
    tk = 512
    if prev is None:
        tq = _pick_tile(n_lat, 1024)
        q_off, n_q, kv_rows, kv_blk = 0, n_lat // tq, s, 0
    else:
        assert n_lat % n_ctx == 0
        tq = _pick_tile(n_ctx, 256)
        q_off, n_q, kv_rows, kv_blk = n_lat // tq, n_ctx // tq, n_ctx, n_lat // n_ctx
    in_specs = [pl.BlockSpec((None, tq, nslot), lambda bi, i: (bi, i + q_off, 0)),
                pl.BlockSpec((None, kv_rows, nslot), lambda bi, i: (bi, kv_blk, 0)),
                pl.BlockSpec((None, kv_rows, nv), lambda bi, i: (bi, kv_blk, 0))]
    args = [q, k, v]
    aliases = {}
    if prev is not None:
        in_specs.append(pl.BlockSpec(memory_space=pl.ANY))
        args.append(prev)
        aliases = {3: 0}
    return pl.pallas_call(
        functools.partial(_mla_flash_kernel, tk=tk, aliased=prev is not None),
        grid=(b, n_q),
        in_specs=in_specs,
        out_specs=pl.BlockSpec((None, tq, nv), lambda bi, i: (bi, i + q_off, 0)),
        out_shape=jax.ShapeDtypeStruct((b, s, nv), BF16),
        scratch_shapes=[pltpu.VMEM((MLA_HEADS, tq, 128), F32),
                        pltpu.VMEM((MLA_HEADS, tq, 128), F32),
                        pltpu.VMEM((MLA_HEADS, tq, 128), F32)],
        input_output_aliases=aliases,
        compiler_params=_cparams(("parallel", "arbitrary")),
        name="mla_flash",
    )(*args)


NA_QROWS = 4
NA_SLAB = NA_QROWS + NA_WIN_ROWS


def _na_geometry(rows):
    assert rows % NA_QROWS == 0 and rows >= NA_SLAB
    nblk = rows // NA_QROWS
    qc = np.arange(GRID_W)
    cs = np.clip(qc - NA_WIN_COLS // 2, 0, GRID_W - NA_WIN_COLS)
    col_valid = (qc[None, :] >= cs[:, None]) & (qc[None, :] < cs[:, None] + NA_WIN_COLS)
    rel_c = np.clip(qc[None, :] - qc[:, None] + NA_WIN_COLS - 1, 0, 2 * NA_WIN_COLS - 2)
    onehot_c = (rel_c[None] == np.arange(2 * NA_WIN_COLS - 1)[:, None, None]) & col_valid[None]
    patterns, cls, starts = {}, [], []
    for i in range(nblk):
        r0 = i * NA_QROWS
        start = int(np.clip(r0 - NA_WIN_ROWS // 2, 0, rows - NA_SLAB))
        qr = r0 + np.arange(NA_QROWS)
        rs = np.clip(qr - NA_WIN_ROWS // 2, 0, rows - NA_WIN_ROWS)
        key = (start - r0,) + tuple((rs - r0).tolist())
        if key not in patterns:
            kr = start + np.arange(NA_SLAB)
            row_valid = (kr[None, :] >= rs[:, None]) & (kr[None, :] < rs[:, None] + NA_WIN_ROWS)
            rel_r = np.clip(kr[None, :] - qr[:, None] + NA_WIN_ROWS - 1, 0, 2 * NA_WIN_ROWS - 2)
            patterns[key] = (len(patterns), np.where(row_valid, rel_r, -1))
        cls.append(patterns[key][0])
        starts.append(start)
    ordered = sorted(patterns.values(), key=lambda z: z[0])
    rel_rows = np.stack([z[1] for z in ordered])
    meta = np.stack([np.asarray(cls, np.int32), np.asarray(starts, np.int32)])
    return meta, rel_rows, (onehot_c.astype(np.float32), col_valid)


def _na_bias_table(rpb, rel_rows, col_tables):
    onehot_c, col_valid = col_tables
    bc = jnp.einsum('hrc,cqk->hrqk', rpb.astype(F32), onehot_c, precision=lax.Precision.HIGHEST)
    bc = jnp.where(col_valid, bc, -jnp.inf)
    outside = jnp.full(bc.shape[:1] + bc.shape[2:], -jnp.inf, F32)
    pats = []
    for pat in rel_rows:
        qrows = [jnp.concatenate([bc[:, r] if r >= 0 else outside for r in row], axis=-1) for row in pat]
        pats.append(jnp.concatenate(qrows, axis=1))
    return jnp.stack(pats)


def _na_scores(q_pair, lane, h, k_parts):
    in_head = (lane < NA_DH) if h % 2 == 0 else (lane >= NA_DH)
    qm = jnp.where(in_head, q_pair * (NA_DH ** -0.5), 0.0).astype(BF16)
    return [lax.dot_general(qm, k, (((1,), (1,)), ((), ())), preferred_element_type=F32) for k in k_parts]


def _na_softmax(scores, bias):
    if bias is not None:
        scores = [scores[0] + bias] + scores[1:]
    m = scores[0].max(axis=-1, keepdims=True)
    for s in scores[1:]:
        m = jnp.maximum(m, s.max(axis=-1, keepdims=True))
    probs = [jnp.exp(s - m) for s in scores]
    den = probs[0].sum(axis=-1, keepdims=True)
    for p in probs[1:]:
        den = den + p.sum(axis=-1, keepdims=True)
    return [p.astype(BF16) for p in probs], den


def _na_kernel(meta_ref, q_ref, k_ref, v_ref, bias_ref, o_ref, *, n_lat, with_ctx):
    i = pl.program_id(1)
    nq = q_ref.shape[0]
    n_lat_tiles = n_lat // nq
    n_all = k_ref.shape[0]
    lane = lax.broadcasted_iota(jnp.int32, (nq, 128), 1)

    def run(windowed):
        key_rows = [pl.ds(n_lat, n_all - n_lat)]
        if windowed:
            start = pl.multiple_of(meta_ref[1, jnp.minimum(i, n_lat_tiles - 1)] * GRID_W, GRID_W)
            key_rows = [pl.ds(start, NA_SLAB * GRID_W)] + key_rows
        pair_lanes = [slice((h // 2) * 128, (h // 2 + 1) * 128) for h in range(NA_HEADS)]
        scores = [_na_scores(q_ref[:, pair_lanes[h]], lane, h, [k_ref[r, pair_lanes[h]] for r in key_rows])
                  for h in range(NA_HEADS)]
        soft = [_na_softmax(scores[h], bias_ref[h] if windowed else None) for h in range(NA_HEADS)]
        heads = []
        for h in range(NA_HEADS):
            probs, den = soft[h]
            out = None
            for p, r in zip(probs, key_rows):
                po = jnp.dot(p, v_ref[r, pair_lanes[h]], preferred_element_type=F32)
                out = po if out is None else out + po
            heads.append(out / den)
        outs = [jnp.where(lane < NA_DH, heads[2 * pair], heads[2 * pair + 1]) for pair in range(NA_HEADS // 2)]
        o_ref[...] = jnp.concatenate(outs, axis=-1).astype(o_ref.dtype)

    if with_ctx:
        pl.when(i < n_lat_tiles)(lambda: run(True))
        pl.when(i >= n_lat_tiles)(lambda: run(False))
    else:
        run(True)


def _na_attention(p, meta, table, n_lat, with_ctx_queries):
    b, s, _ = p.shape
    nq = NA_QROWS * GRID_W
    nk = NA_SLAB * GRID_W
    assert (s - n_lat) % nq == 0
    nlt = n_lat // nq
    qb = P_NA_QKV // NA_W
    grid_spec = pltpu.PrefetchScalarGridSpec(
        num_scalar_prefetch=1,
        grid=(b, s // nq if with_ctx_queries else nlt),
        in_specs=[pl.BlockSpec((None, nq, NA_W), lambda bi, i, m: (bi, i, qb)),
                  pl.BlockSpec((None, s, NA_W), lambda bi, i, m: (bi, 0, qb + 1)),
                  pl.BlockSpec((None, s, NA_W), lambda bi, i, m: (bi, 0, qb + 2)),
                  pl.BlockSpec((None, NA_HEADS, nq, nk),
                               lambda bi, i, m: (m[0, jnp.minimum(i, nlt - 1)], 0, 0, 0))],
        out_specs=pl.BlockSpec((None, nq, NA_W), lambda bi, i, m: (bi, i, 0)))
    return pl.pallas_call(
        functools.partial(_na_kernel, n_lat=n_lat, with_ctx=with_ctx_queries),
        grid_spec=grid_spec,
        out_shape=jax.ShapeDtypeStruct((b, s, NA_W), BF16),
        compiler_params=_cparams(("parallel", "arbitrary")),
        name="na_attention",
    )(meta, p, p, p, table)


def _tile_conv(x, prev, nxt, w_ref, tile_idx, n_tiles, n_lat_tiles):
    r = x.shape[0]
    width = w_ref.shape[0]
    left = width // 2
    has_prev = jnp.logical_and(tile_idx != 0, tile_idx != n_lat_tiles)
    has_next = jnp.logical_and(tile_idx != n_tiles - 1, tile_idx != n_lat_tiles - 1)
    prev = jnp.where(has_prev, prev, 0.0)
    nxt = jnp.where(has_next, nxt, 0.0)
    xe = jnp.concatenate([prev, x, nxt], axis=0)
    acc = None
    for j in range(width):
        o = HALO - left + j
        term = xe[o:o + r, :] * w_ref[j:j + 1, :]
        acc = term if acc is None else acc + term
    return acc


def _halo_specs(width, col_block, tile_of):
    per = SEQ_TILE // HALO

    def main(bi, s, *_):
        return (bi, tile_of(s), col_block)

    def prev(bi, s, *_):
        return (bi, jnp.maximum(tile_of(s) * per - 1, 0), col_block)

    def make_next(n_tiles):
        def nxt(bi, s, *_):
            return (bi, jnp.minimum((tile_of(s) + 1) * per, n_tiles * per - 1), col_block)
        return nxt

    return main, prev, make_next


def _gdn_prep_kernel(x_ref, xp_ref, xn_ref, ba_ref, cw_ref, ones_ref, exp_ref, alog_ref, dtb_ref,
                     q_ref, k_ref, v_ref, beta_ref, g_ref, *, n_lat_tiles):
    i = pl.program_id(1)
    y = _tile_conv(x_ref[...], xp_ref[...], xn_ref[...], cw_ref, i, pl.num_programs(1), n_lat_tiles)
    y = y * _sigmoid(y)
    q = y[:, :GDN_W]
    k = y[:, GDN_W:2 * GDN_W]
    v_ref[...] = y[:, 2 * GDN_W:].astype(v_ref.dtype)

    def head_norm(u):
        parts = jnp.concatenate(_split(u * u), axis=-1)
        ss = jnp.dot(parts, ones_ref[...], preferred_element_type=F32)
        return u * lax.rsqrt(ss + RMS_EPS)

    q_ref[...] = (head_norm(q) * (GDN_DK ** -0.5)).astype(q_ref.dtype)
    k_ref[...] = head_norm(k).astype(k_ref.dtype)

    ba = ba_ref[...]
    a = ba + dtb_ref[...]
    softplus = jnp.maximum(a, 0.0) + jnp.log1p(jnp.exp(-jnp.abs(a)))
    lane = lax.broadcasted_iota(jnp.int32, ba.shape, 1)
    compact = jnp.where(lane < 2 * GDN_HEADS, _sigmoid(ba), -jnp.exp(alog_ref[...]) * softplus)
    wide = jnp.dot(jnp.concatenate(_split(compact), axis=-1), exp_ref[...], preferred_element_type=F32)
    half = 2 * GDN_W
    beta_ref[...] = wide[:, :half].astype(beta_ref.dtype)
    g_ref[...] = wide[:, half:]


def _gdn_prep(p, conv_w, a_log, dt_bias, n_lat):
    b, s, _ = p.shape
    n_tiles = s // SEQ_TILE
    main, prev, make_next = _halo_specs(3 * GDN_W, 0, lambda t: t)
    ones_n = jnp.concatenate([_head_block_ones(GDN_HEADS, GDN_DK)] * N_SPLIT, axis=0)
    expand = np.zeros((128, 4 * GDN_W), np.float32)
    for kind in range(2):
        for d in range(2):
            for h in range(GDN_HEADS):
                c0 = kind * 2 * GDN_W + d * GDN_W + h * GDN_DV
                expand[kind * 2 * GDN_HEADS + d * GDN_HEADS + h, c0:c0 + GDN_DV] = 1.0
    expand_n = jnp.asarray(np.concatenate([expand] * N_SPLIT, axis=0), BF16)
    lanes = jnp.zeros((1, 128), F32)
    alog_e = lanes.at[0, 2 * GDN_HEADS:4 * GDN_HEADS].set(a_log.astype(F32).reshape(-1))
    dtb_e = lanes.at[0, 2 * GDN_HEADS:4 * GDN_HEADS].set(dt_bias.astype(F32).reshape(-1))
    tok = lambda w: pl.BlockSpec((None, SEQ_TILE, w), lambda bi, i: (bi, i, 0))
    return pl.pallas_call(
        functools.partial(_gdn_prep_kernel, n_lat_tiles=n_lat // SEQ_TILE),
        grid=(b, n_tiles),
        in_specs=[pl.BlockSpec((None, SEQ_TILE, 3 * GDN_W), main),
                  pl.BlockSpec((None, HALO, 3 * GDN_W), prev),
                  pl.BlockSpec((None, HALO, 3 * GDN_W), make_next(n_tiles)),
                  pl.BlockSpec((None, SEQ_TILE, 128), lambda bi, i: (bi, i, P_GDN_BA // 128)),
                  _full(conv_w), _full(ones_n), _full(expand_n), _full(alog_e), _full(dtb_e)],
        out_specs=[tok(GDN_W), tok(GDN_W), tok(GDN_W), tok(2 * GDN_W), tok(2 * GDN_W)],
        out_shape=[jax.ShapeDtypeStruct((b, s, GDN_W), BF16)] * 3 + [jax.ShapeDtypeStruct((b, s, 2 * GDN_W), BF16),
                                                                    jax.ShapeDtypeStruct((b, s, 2 * GDN_W), F32)],
        compiler_params=_cparams(("parallel", "parallel")),
        name="gdn_prep",
    )(p, p, p, p, conv_w, ones_n, expand_n, alog_e, dtb_e)


GDN_PAIR = 2 * GDN_DK


def _gdn_masks():
    c, w = GDN_CHUNK, GDN_W
    r2, c2 = np.arange(GDN_PAIR)[:, None], np.arange(GDN_PAIR)[None, :]
    bd = ((r2 // c) == (c2 // c)).astype(np.float32)
    i = np.arange(c)[:, None]
    j = (np.arange(w) % c)[None, :]
    level = np.zeros((c, w), np.int32)
    for bit in range(6):
        level += ((i ^ j) >= (1 << bit)).astype(np.int32)
    lvl = np.stack([(level == m).astype(np.float32) for m in range(7)])
    dirm = np.stack([np.stack([(j <= i), (j < i)]), np.stack([(j >= i), (j > i)])]).astype(np.float32)
    tj = (np.arange(N_SPLIT * c) % c)[None, :]
    tri = np.stack([(tj <= i), (tj >= i)]).astype(np.float32)
    return jnp.asarray(bd, BF16), jnp.asarray(lvl), jnp.asarray(dirm), jnp.asarray(tri, BF16)


def _heads_mm(x, y, bd, transpose_rhs=False):
    xb = x.astype(BF16)
    yb = y.astype(BF16)
    outs = []
    for pair in range(GDN_W // GDN_PAIR):
        ls = slice(pair * GDN_PAIR, (pair + 1) * GDN_PAIR)
        w = jnp.concatenate([yb[:, ls], yb[:, ls]], axis=0) * bd
        dims = (((1,), (1,)), ((), ())) if transpose_rhs else (((1,), (0,)), ((), ()))
        outs.append(lax.dot_general(xb[:, ls], w, dims, preferred_element_type=F32))
    return jnp.concatenate(outs, axis=1)


def _gdn_intra(probs, bd, lvl_ref, dirm_ref, tri_ref):
    c = GDN_CHUNK
    n = len(probs)
    eye = lvl_ref[0]
    gc, g_last, decay, gram = [], [], [], []
    for q, k, v, beta, g, rev in probs:
        d = 1 if rev else 0
        gcp = jnp.dot(tri_ref[d], jnp.concatenate(_split(g), axis=0), preferred_element_type=F32)
        gc.append(gcp)
        g_last.append(gcp[0:1, :] if rev else gcp[c - 1:c, :])
        gc_row = jnp.sum(gcp * eye, axis=0, keepdims=True)
        decay.append(dirm_ref[d, 0] * jnp.exp(jnp.minimum(gcp - gc_row, 0.0)))
        gram.append(_heads_mm(jnp.concatenate([k, q], axis=0), k, bd, transpose_rhs=True))
    lower = [dirm_ref[1 if p[5] else 0, 1] * p[3] * gram[x][:c] * decay[x] for x, p in enumerate(probs)]
    a_intra = [gram[x][c:] * decay[x] for x in range(n)]
    t = [eye - lower[x] * lvl_ref[1] for x in range(n)]
    for lev in range(2, 7):
        y = [_heads_mm(t[x], lower[x] * lvl_ref[lev], bd) for x in range(n)]
        z = [_heads_mm(y[x], t[x], bd) for x in range(n)]
        t = [t[x] - z[x] for x in range(n)]
    e_gc = [jnp.exp(gc[x]) for x in range(n)]
    u = [_heads_mm(t[x], p[2] * p[3], bd) for x, p in enumerate(probs)]
    w = [_heads_mm(t[x], p[1] * p[3] * e_gc[x], bd) for x, p in enumerate(probs)]
    wq = [jnp.concatenate([w[x], p[0] * e_gc[x]], axis=0).astype(BF16) for x, p in enumerate(probs)]
    k_dec = [(p[1] * jnp.exp(g_last[x] - gc[x])).astype(BF16) for x, p in enumerate(probs)]
    g_tot = [jnp.exp(g_last[x]) for x in range(n)]
    return u, wq, k_dec, a_intra, g_tot


def _gdn_state_steps(items, bd):
    c = GDN_CHUNK
    pairs = [slice(p * GDN_PAIR, (p + 1) * GDN_PAIR) for p in range(GDN_W // GDN_PAIR)]
    bdf = bd.astype(F32)
    ws_qs = [jnp.concatenate([jnp.dot(wq[:, ls], s_ref[p].astype(BF16), preferred_element_type=F32)
                              for p, ls in enumerate(pairs)], axis=1)
             for _, wq, _, _, _, s_ref in items]
    v_new = [it[0] - ws[:c] for it, ws in zip(items, ws_qs)]
    outs = [ws[c:] + _heads_mm(it[3], vn, bd) for it, ws, vn in zip(items, ws_qs, v_new)]
    for (_, _, k_dec, _, g_tot, s_ref), vn in zip(items, v_new):
        vb = vn.astype(BF16)
        for p, ls in enumerate(pairs):
            kv = lax.dot_general(k_dec[:, ls], vb[:, ls], (((0,), (0,)), ((), ())), preferred_element_type=F32)
            s_ref[p] = s_ref[p] * g_tot[:, ls] + kv * bdf
    return outs


def _gdn_scan_body(qf, kf, vf, bf, gf, qb, kb, vb, bb, gb, bd_ref, lvl_ref, dirm_ref, tri_ref,
                   of_ref, ob_ref, sf_ref, sb_ref):
    bd = bd_ref[...]
    n = SEQ_TILE // GDN_CHUNK
    probs, rows = [], []
    for c in range(n):
        rf = slice(c * GDN_CHUNK, (c + 1) * GDN_CHUNK)
        rb = slice((n - 1 - c) * GDN_CHUNK, (n - c) * GDN_CHUNK)
        probs.append(tuple(r[rf, :].astype(F32) for r in (qf, kf, vf, bf, gf)) + (False,))
        probs.append(tuple(r[rb, :].astype(F32) for r in (qb, kb, vb, bb, gb)) + (True,))
        rows += [rf, rb]
    u, wq, k_dec, a_intra, g_tot = _gdn_intra(probs, bd, lvl_ref, dirm_ref, tri_ref)
    for c in range(n):
        xs = (2 * c, 2 * c + 1)
        outs = _gdn_state_steps([(u[x], wq[x], k_dec[x], a_intra[x], g_tot[x], sb_ref if probs[x][5] else sf_ref)
                                 for x in xs], bd)
        for x, o in zip(xs, outs):
            (ob_ref if probs[x][5] else of_ref)[rows[x], :] = o.astype(of_ref.dtype)


def _seq_tile_maps(n_lat_tiles, n_tiles):
    fwd = lambda s: lax.rem(s + n_lat_tiles, n_tiles)
    bwd = lambda s: n_tiles - 1 - s
    return fwd, bwd


def _gdn_scan_specs(q, k, v, beta, g, n_lat):
    b, s, _ = q.shape
    n_tiles = s // SEQ_TILE
    fwd, bwd = _seq_tile_maps(n_lat // SEQ_TILE, n_tiles)
    spec = lambda tile_of, col: pl.BlockSpec((None, SEQ_TILE, GDN_W), lambda bi, t: (bi, tile_of(t), col))
    masks = _gdn_masks()
    in_specs = [spec(fwd, 0)] * 5 + [spec(bwd, 0)] * 3 + [spec(bwd, 1)] * 2 + [_full(m) for m in masks]
    assert len(in_specs) == N_GDN_IN
    return (in_specs, [q, k, v, beta, g, q, k, v, beta, g, *masks], [spec(fwd, 0), spec(bwd, 0)],
            [jax.ShapeDtypeStruct((b, s, GDN_W), BF16)] * 2,
            [pltpu.VMEM((GDN_W // GDN_PAIR, GDN_PAIR, GDN_PAIR), F32)] * 2)


def _lru_tile_prep(x_ref, xp_ref, xn_ref, cw_ref, cb_ref, wg_ref, bg_ref, nla_ref, a_ref, b_ref,
                   tile_idx, n_tiles, n_lat_tiles):
    xb = _tile_conv(x_ref[...], xp_ref[...], xn_ref[...], cw_ref, tile_idx, n_tiles, n_lat_tiles) + cb_ref[...]
    gates = _sigmoid(jnp.dot(xb.astype(BF16), wg_ref[...], preferred_element_type=F32) + bg_ref[...])
    log_a = nla_ref[...] * gates[:, :LRU_W]
    a_ref[...] = jnp.exp(log_a)
    th = jnp.tanh(log_a)
    b_ref[...] = jnp.sqrt(-2.0 * th / (1.0 - th)) * gates[:, LRU_W:] * xb


def _scan_group(a, b, h, row, reverse):
    for d in (1, 2, 4):
        if reverse:
            keep = row < 8 - d
            shift = 8 - d
        else:
            keep = row >= d
            shift = d
        a_s = jnp.where(keep, pltpu.roll(a, shift, 0), 1.0)
        b_s = jnp.where(keep, pltpu.roll(b, shift, 0), 0.0)
        b = a * b_s + b
        a = a * a_s
    return a * h + b


def _lru_scan_body(xf, xfp, xfn, xb, xbp, xbn, cw_ref, cb_ref, wgf, bgf, nlaf, wgb, bgb, nlab,
                   hf_ref, hb_ref, af_ref, bf_ref, ab_ref, bb_ref, cf_ref, cbk_ref, *, n_lat_tiles):
    s = pl.program_id(1)
    n_tiles = pl.num_programs(1)
    t_f, t_b = (m(s) for m in _seq_tile_maps(n_lat_tiles, n_tiles))
    _lru_tile_prep(xf, xfp, xfn, cw_ref, cb_ref, wgf, bgf, nlaf, af_ref, bf_ref, t_f, n_tiles, n_lat_tiles)
    _lru_tile_prep(xb, xbp, xbn, cw_ref, cb_ref, wgb, bgb, nlab, ab_ref, bb_ref, t_b, n_tiles, n_lat_tiles)

    n_groups = SEQ_TILE // 8
    row = lax.broadcasted_iota(jnp.int32, (8, LRU_W), 0)

    h_f, h_b = cf_ref[...], cbk_ref[...]
    pend_f, pend_b = None, None
    for gi in range(n_groups):
        rf = slice(gi * 8, gi * 8 + 8)
        rb = slice((n_groups - 1 - gi) * 8, (n_groups - gi) * 8)
        out_f = _scan_group(af_ref[rf, :], bf_ref[rf, :], h_f, row, False)
        out_b = _scan_group(ab_ref[rb, :], bb_ref[rb, :], h_b, row, True)
        if gi % 2 == 0:
            pend_f, pend_b = out_f, out_b
        else:
            hf_ref[(gi - 1) * 8:(gi + 1) * 8, :] = jnp.concatenate([pend_f, out_f], axis=0).astype(hf_ref.dtype)
            hb_ref[(n_groups - 1 - gi) * 8:(n_groups + 1 - gi) * 8, :] = (
                jnp.concatenate([out_b, pend_b], axis=0).astype(hb_ref.dtype))
        h_f = jnp.broadcast_to(out_f[7:8, :], (8, LRU_W))
        h_b = jnp.broadcast_to(out_b[0:1, :], (8, LRU_W))
    cf_ref[...] = h_f
    cbk_ref[...] = h_b


N_LRU_IN, N_GDN_IN, N_LRU_SCRATCH = 14, 14, 6


def _seq_mixers_kernel(*refs, n_lat_tiles):
    lru_in = refs[:N_LRU_IN]
    gdn_in = refs[N_LRU_IN:N_LRU_IN + N_GDN_IN]
    hf_ref, hb_ref, of_ref, ob_ref = refs[N_LRU_IN + N_GDN_IN:N_LRU_IN + N_GDN_IN + 4]
    scratch = refs[N_LRU_IN + N_GDN_IN + 4:]
    lru_scratch, gdn_scratch = scratch[:N_LRU_SCRATCH], scratch[N_LRU_SCRATCH:]

    @pl.when(pl.program_id(1) == 0)
    def _():
        for ref in lru_scratch[4:] + gdn_scratch:
            ref[...] = jnp.zeros(ref.shape, ref.dtype)

    _lru_scan_body(*lru_in, hf_ref, hb_ref, *lru_scratch, n_lat_tiles=n_lat_tiles)
    _gdn_scan_body(*gdn_in, of_ref, ob_ref, *gdn_scratch)


def _lru_scan_specs(p, conv_w, conv_b, w_r, b_r, w_i, b_i, lam, n_lat):
    b, s, _ = p.shape
    n_tiles = s // SEQ_TILE
    nlt = n_lat // SEQ_TILE
    fwd, bwd = _seq_tile_maps(nlt, n_tiles)
    col = P_LRU_X // LRU_W
    specs = []
    for tile_of in (fwd, bwd):
        main, prev, make_next = _halo_specs(LRU_W, col, tile_of)
        specs += [pl.BlockSpec((None, SEQ_TILE, LRU_W), main), pl.BlockSpec((None, HALO, LRU_W), prev),
                  pl.BlockSpec((None, HALO, LRU_W), make_next(n_tiles))]

    def blockdiag(w):
        return jax.scipy.linalg.block_diag(*[w[n] for n in range(LRU_BLOCKS)])

    dir_args = []
    for d in range(2):
        wg = jnp.concatenate([blockdiag(w_r[d]), blockdiag(w_i[d])], axis=1).astype(BF16)
        bg = jnp.concatenate([b_r[d], b_i[d]]).astype(F32).reshape(1, 2 * LRU_W)
        nla = (-LRU_C * jax.nn.softplus(-lam[d].astype(F32))).reshape(1, LRU_W)
        dir_args += [wg, bg, nla]
    cb2 = conv_b.reshape(1, LRU_W)
    out_spec = lambda tile_of: pl.BlockSpec((None, SEQ_TILE, LRU_W), lambda bi, t: (bi, tile_of(t), 0))
    in_specs = specs + [_full(conv_w), _full(cb2)] + [_full(a) for a in dir_args]
    assert len(in_specs) == N_LRU_IN
    return (in_specs, [p, p, p, p, p, p, conv_w, cb2, *dir_args], [out_spec(fwd), out_spec(bwd)],
            [jax.ShapeDtypeStruct((b, s, LRU_W), BF16)] * 2,
            [pltpu.VMEM((SEQ_TILE, LRU_W), F32)] * 4 + [pltpu.VMEM((8, LRU_W), F32)] * 2)


def _seq_mixers(lru_parts, gdn_parts, bsz, n_tiles, n_lat):
    in_specs, args, out_specs, out_shape, scratch = ([*a, *b] for a, b in zip(lru_parts, gdn_parts))
    return pl.pallas_call(
        functools.partial(_seq_mixers_kernel, n_lat_tiles=n_lat // SEQ_TILE),
        grid=(bsz, n_tiles),
        in_specs=in_specs,
        out_specs=out_specs,
        out_shape=out_shape,
        scratch_shapes=scratch,
        compiler_params=_cparams(("parallel", "arbitrary")),
        name="seq_mixers",
    )(*args)


def _merge_kernel(x_ref, mc_ref, ml_ref, of_ref, ob_ref, z_ref, hf_ref, hb_ref, y_ref, uc_ref, ud_ref,
                  t0_ref, t1_ref, t2_ref, t3_ref, gn_ref, ones_ref, bg_ref, wb_ref, wo_ref, o_ref, *, n_lat):
    project = lambda u, n: jnp.dot(u, wb_ref[n], preferred_element_type=F32)
    o = of_ref[...].astype(F32) + ob_ref[...].astype(F32)
    ms = jnp.dot(jnp.concatenate(_split(o * o), axis=-1), ones_ref[...], preferred_element_type=F32) * (1.0 / GDN_DV)
    proj = {2: project(uc_ref[...], 2), 3: project(ud_ref[...], 3)}
    z = z_ref[...]
    ua = (o * lax.rsqrt(ms + RMS_EPS) * gn_ref[...] * (z * _sigmoid(z))).astype(BF16)
    y = y_ref[...]
    gelu = 0.5 * y * (1.0 + jnp.tanh(0.7978845608028654 * (y + 0.044715 * (y * y * y))))
    ub = ((hf_ref[...].astype(F32) + hb_ref[...].astype(F32)) * gelu).astype(BF16)
    proj[0] = project(ua, 0)
    proj[1] = project(ub, 1)

    merged = None
    for n, t_ref in enumerate((t0_ref, t1_ref, t2_ref, t3_ref)):
        gate = _sigmoid(t_ref[...] + bg_ref[n:n + 1, :].astype(BF16))
        term = gate.astype(F32) * proj[n]
        merged = term if merged is None else merged + term
    out = jnp.dot(merged.astype(BF16), wo_ref[...], preferred_element_type=F32)
    g1 = _row_select(x_ref.shape[0], pl.program_id(1), n_lat, mc_ref[2:3, :], ml_ref[2:3, :])
    o_ref[...] = x_ref[...] + g1 * out


def _merge(xs, mod_c, mod_l, gdn_out, lru_out, uc, ud, p32, p16, gdn_norm_g, b_gate, w_branch, w_out, layer, n_lat,
           with_ctx):
    b, s, d = xs.shape
    rows = s if with_ctx else n_lat
    tm = _pick_tile(rows, 544)
    gate_blk = P_GATES // d
    tok = lambda w: pl.BlockSpec((None, tm, w), lambda bi, i: (bi, i, 0))
    ones_n = jnp.concatenate([_head_block_ones(GDN_HEADS, GDN_DV)] * N_SPLIT, axis=0)
    gn = jnp.tile(gdn_norm_g.astype(F32), GDN_HEADS).reshape(1, GDN_W)
    pcol = lambda c: pl.BlockSpec((None, tm, BRANCH_W), lambda bi, i: (bi, i, c))
    in_specs = [tok(d), pl.BlockSpec((8, d), lambda bi, i: (0, 0)), pl.BlockSpec((None, 8, d), lambda bi, i: (bi, 0, 0))]
    in_specs += [tok(BRANCH_W), tok(BRANCH_W), pcol(P_GDN_Z // GDN_W), tok(BRANCH_W), tok(BRANCH_W),
                 pcol(P_LRU_Y // LRU_W), tok(BRANCH_W), tok(BRANCH_W)]
    in_specs += [pl.BlockSpec((None, tm, d), functools.partial(lambda bi, i, n: (bi, i, gate_blk + n), n=n))
                 for n in range(N_BRANCH)]
    in_specs += [_full(gn), _full(ones_n), _full(b_gate),
                 pl.BlockSpec((None,) + w_branch.shape[1:], lambda bi, i: (layer, 0, 0, 0)),
                 pl.BlockSpec((None,) + w_out.shape[1:], lambda bi, i: (layer, 0, 0))]
    return pl.pallas_call(
        functools.partial(_merge_kernel, n_lat=n_lat),
        grid=(b, rows // tm),
        in_specs=in_specs,
        out_specs=tok(d),
        out_shape=jax.ShapeDtypeStruct((b, rows, d), F32),
        input_output_aliases={0: 0} if with_ctx else {},
        compiler_params=_cparams(("parallel", "parallel")),
        name="merge",
    )(xs, mod_c, mod_l, gdn_out[0], gdn_out[1], p32, lru_out[0], lru_out[1], p32, uc, ud, p16, p16, p16, p16,
      gn, ones_n, b_gate, w_branch, w_out)


def _mlp_kernel(x_ref, mc_ref, ml_ref, gn_ref, gf_ref, w1_ref, w2_ref, o_ref, h_ref, acc_ref, *, n_lat, final_norm):
    f = pl.program_id(2)
    tm = x_ref.shape[0]
    i = pl.program_id(1)

    @pl.when(f == 0)
    def _():
        _norm_modulate(x_ref, h_ref, gn_ref, mc_ref, ml_ref, 3, i, n_lat)
        acc_ref[...] = jnp.zeros(acc_ref.shape, F32)

    a = jnp.maximum(jnp.dot(h_ref[...], w1_ref[...], preferred_element_type=F32), 0.0)
    acc_ref[...] += jnp.dot((a * a).astype(BF16), w2_ref[...], preferred_element_type=F32)

    @pl.when(f == pl.num_programs(2) - 1)
    def _():
        g2 = _row_select(tm, i, n_lat, mc_ref[5:6, :], ml_ref[5:6, :])
        y = x_ref[...] + g2 * acc_ref[...]
        if final_norm:
            y = y * lax.rsqrt(jnp.mean(y * y, axis=-1, keepdims=True) + RMS_EPS) * gf_ref[...]
        o_ref[...] = y


def _mlp(xs, mod_c, mod_l, gain, w1, w2, layer, final_gain, n_lat, final_norm):
    b, rows, d = xs.shape
    dff = w1.shape[-1]
    tm = _pick_tile(rows, 1088)
    tf = 1024
    row = pl.BlockSpec((1, d), lambda bi, i, f: (0, 0))
    return pl.pallas_call(
        functools.partial(_mlp_kernel, n_lat=n_lat, final_norm=final_norm),
        grid=(b, rows // tm, dff // tf),
        in_specs=[pl.BlockSpec((None, tm, d), lambda bi, i, f: (bi, i, 0)),
                  pl.BlockSpec((8, d), lambda bi, i, f: (0, 0)),
                  pl.BlockSpec((None, 8, d), lambda bi, i, f: (bi, 0, 0)),
                  row, row,
                  pl.BlockSpec((None, d, tf), lambda bi, i, f: (layer, 0, f)),
                  pl.BlockSpec((None, tf, d), lambda bi, i, f: (layer, f, 0))],
        out_specs=pl.BlockSpec((None, tm, d), lambda bi, i, f: (bi, i, 0)),
        out_shape=jax.ShapeDtypeStruct((b, rows, d), F32),
        scratch_shapes=[pltpu.VMEM((tm, d), BF16), pltpu.VMEM((tm, d), F32)],
        compiler_params=_cparams(("parallel", "parallel", "arbitrary")),
        name="mlp",
    )(xs, mod_c, mod_l, gain, final_gain, w1, w2)


def kernel(x, c, ctx, c_ctx, mod_w, mod_b, norm1_g, norm2_g, w_in, b_gate, gdn_conv_w, gdn_a_log, gdn_dt_bias,
           gdn_norm_g, lru_conv_w, lru_conv_b, lru_w_r, lru_b_r, lru_w_i, lru_b_i, lru_lambda, mla_q_norm_g,
           mla_w_uq, mla_kv_norm_g, mla_w_ukv, na_rpb, w_branch, w_out, mlp_w1, mlp_w2, final_norm_g):
    bsz, n_tok, d = x.shape
    n_ctx = ctx.shape[1]
    depth = w_in.shape[0]
    assert n_ctx % SEQ_TILE == 0 and n_tok % SEQ_TILE == 0 and n_tok % GRID_W == 0
    na_meta, na_rel_rows, na_cols = _na_geometry(n_tok // GRID_W)
    na_meta = jnp.asarray(na_meta)
    cos, sin = _rope_tables(n_tok, n_ctx)

    n_rows = -(-(bsz + 1) // 8) * 8
    cc = jnp.zeros((n_rows, d), F32).at[:bsz].set(c).at[bsz].set(c_ctx)
    final_gain = final_norm_g.reshape(1, d)

    w_in_all = _arrange_w_in(w_in)
    wb_all = w_branch.astype(BF16)
    wo_all = w_out.astype(BF16)
    w1_all = mlp_w1.astype(BF16)
    w2_all = mlp_w2.astype(BF16)

    xs = jnp.concatenate([x, ctx], axis=1)
    for l in range(depth):
        need_ctx = l < depth - 1
        mod = _modulation(cc, mod_w, mod_b[l], l).reshape(n_rows, N_MOD, d)
        pad = jnp.zeros((8 - N_MOD, d), F32)
        mod_c = jnp.concatenate([mod[bsz], pad], axis=0)
        mod_l = jnp.concatenate([mod[:bsz], jnp.broadcast_to(pad, (bsz, 8 - N_MOD, d))], axis=1)

        wq, wk, wv, place = _arrange_mla(mla_w_uq[l], mla_w_ukv[l])
        gq = mla_q_norm_g[l].reshape(1, -1)
        gkv = mla_kv_norm_g[l].reshape(1, -1)
        g1n = norm1_g[l].reshape(1, d)
        g2n = norm2_g[l].reshape(1, d)

        p32, p16 = _inproj(xs, mod_c, mod_l, g1n, w_in_all, l, n_tok)

        gq_, gk_, gv_, gbeta, gg = _gdn_prep(p32, gdn_conv_w[l], gdn_a_log[l], gdn_dt_bias[l], n_tok)
        h_f, h_b, o_f, o_b = _seq_mixers(
            _lru_scan_specs(p32, lru_conv_w[l], lru_conv_b[l], lru_w_r[l], lru_b_r[l], lru_w_i[l], lru_b_i[l],
                            lru_lambda[l], n_tok),
            _gdn_scan_specs(gq_, gk_, gv_, gbeta, gg, n_tok), bsz, (n_tok + n_ctx) // SEQ_TILE, n_tok)

        mq, mk, mv = _mla_prep(p32, p16, cos, sin, gq, gkv, wq, wk, wv, place)
        uc = _mla_flash(mq, mk, mv, n_tok)
        if need_ctx:
            uc = _mla_flash(mq, mk, mv, n_tok, prev=uc)
        ud = _na_attention(p16, na_meta, _na_bias_table(na_rpb[l], na_rel_rows, na_cols), n_tok, need_ctx)

        xs = _merge(xs, mod_c, mod_l, (o_f, o_b), (h_f, h_b), uc, ud, p32, p16, gdn_norm_g[l], b_gate[l],
                    wb_all, wo_all, l, n_tok, need_ctx)
        xs = _mlp(xs, mod_c, mod_l, g2n, w1_all, w2_all, l, final_gain, n_tok, l == depth - 1)
    return xs
```

```python
import functools

import jax
import jax.numpy as jnp
import numpy as np
from jax import lax
from jax.experimental import pallas as pl
from jax.experimental.pallas import tpu as pltpu

F32 = jnp.float32
BF16 = jnp.bfloat16

GRID_W = 64
N_MOD = 6
RMS_EPS = 1e-6
GDN_HEADS = 4
GDN_DK = 64
GDN_DV = 64
GDN_CHUNK = 64
GDN_W = GDN_HEADS * GDN_DV
LRU_W = 256
LRU_BLOCKS = 4
LRU_C = 8.0
MLA_HEADS = 4
MLA_Q_RANK = 256
MLA_KV_RANK = 128
MLA_NOPE = 64
MLA_ROPE = 32
MLA_V = 64
MLA_SLOT = 128
ROPE_BASE = 10000.0
LOG2_E = 1.4426950408889634
NA_HEADS = 4
NA_DH = 64
NA_W = NA_HEADS * NA_DH
NA_WIN_ROWS = 8
NA_WIN_COLS = 16
N_BRANCH = 4
BRANCH_W = 256

SEQ_TILE = 256
HALO = 8

_REF_COLS = {}
_off = 0
for _name, _w in (('gdn_qkv', 3 * GDN_W), ('gdn_z', GDN_W), ('gdn_beta', 2 * GDN_HEADS), ('gdn_a', 2 * GDN_HEADS),
                  ('lru_x', LRU_W), ('lru_y', LRU_W), ('mla_q', MLA_Q_RANK), ('mla_kv', MLA_KV_RANK),
                  ('mla_kr', MLA_ROPE), ('na_qkv', 3 * NA_W)):
    _REF_COLS[_name] = (_off, _w)
    _off += _w
N_MIX_COLS = _off

P_GDN_QKV = 0
P_GDN_Z = 768
P_LRU_X = 1024
P_LRU_Y = 1280
P_MLA_KV = 1536
P_MLA_KR = 1664
P_GDN_BA = 1792
P32_COLS = 2048
P_MLA_Q = 0
P_NA_QKV = 256
P_GATES = 1024
P16_COLS = P_GATES + N_BRANCH * 1024
PROJ_TILE = 1024

V7X_VMEM_BYTES = 64 * 1024 * 1024
VMEM_LIMIT = V7X_VMEM_BYTES - 12 * 1024 * 1024


def _cparams(sem):
    return pltpu.CompilerParams(dimension_semantics=sem, vmem_limit_bytes=VMEM_LIMIT)


def _pick_tile(n, cap):
    best = 8
    for t in range(8, min(n, cap) + 1, 8):
        if n % t == 0:
            best = t
    return best


def _full(a):
    return pl.BlockSpec(a.shape, lambda *_: (0,) * a.ndim)


N_SPLIT = 2


def _split(x):
    hi = x.astype(BF16)
    lo = (x - hi.astype(F32)).astype(BF16)
    return hi, lo


def _sigmoid(x):
    return 0.5 * jnp.tanh(0.5 * x) + 0.5


def _arrange_w_in(w_in):
    pieces, pos = [], 0

    def put(dst, block):
        nonlocal pos
        if dst > pos:
            pieces.append(jnp.zeros(w_in.shape[:-1] + (dst - pos,), w_in.dtype))
        pieces.append(block)
        pos = dst + block.shape[-1]

    ref = lambda name: w_in[..., _REF_COLS[name][0]:_REF_COLS[name][0] + _REF_COLS[name][1]]
    for name, dst in (('gdn_qkv', P_GDN_QKV), ('gdn_z', P_GDN_Z), ('lru_x', P_LRU_X), ('lru_y', P_LRU_Y),
                      ('mla_kv', P_MLA_KV), ('mla_kr', P_MLA_KR), ('gdn_beta', P_GDN_BA),
                      ('gdn_a', P_GDN_BA + 2 * GDN_HEADS)):
        put(dst, ref(name))
    put(P32_COLS + P_MLA_Q, ref('mla_q'))
    put(P32_COLS + P_NA_QKV, ref('na_qkv'))
    put(P32_COLS + P_GATES, w_in[..., N_MIX_COLS:])
    assert pos == P32_COLS + P16_COLS
    return jnp.concatenate(pieces, axis=-1).astype(BF16)


def _rope_perm():
    q = MLA_ROPE // 4
    src = np.zeros(MLA_ROPE, np.int32)
    sign = np.zeros(MLA_ROPE, np.float32)
    for base in (0, 2 * q):
        for d in range(q):
            src[base + d] = base + d + q
            sign[base + d] = -1.0
            src[base + q + d] = base + d
            sign[base + q + d] = 1.0
    return src, sign


def _arrange_mla(w_uq, w_ukv):
    src, sign = _rope_perm()
    hq = MLA_NOPE + MLA_ROPE
    wq = jnp.zeros((MLA_Q_RANK, 2 * MLA_HEADS * MLA_SLOT), F32)
    wk = jnp.zeros((MLA_KV_RANK, MLA_HEADS * MLA_SLOT), F32)
    wv = jnp.zeros((MLA_KV_RANK, MLA_HEADS * MLA_V), F32)
    place = np.zeros((2 * MLA_SLOT, 2 * MLA_HEADS * MLA_SLOT), np.float32)
    rot_off = MLA_HEADS * MLA_SLOT
    for h in range(MLA_HEADS):
        nope = w_uq[:, h * hq:h * hq + MLA_NOPE]
        pe = w_uq[:, h * hq + MLA_NOPE:(h + 1) * hq]
        s = h * MLA_SLOT
        wq = wq.at[:, s:s + MLA_NOPE].set(nope)
        wq = wq.at[:, s + MLA_NOPE:s + MLA_NOPE + MLA_ROPE].set(pe)
        wq = wq.at[:, rot_off + s + MLA_NOPE:rot_off + s + MLA_NOPE + MLA_ROPE].set(pe[:, src] * sign)
        wk = wk.at[:, s:s + MLA_NOPE].set(w_ukv[:, h * 128:h * 128 + MLA_NOPE])
        wv = wv.at[:, h * MLA_V:(h + 1) * MLA_V].set(w_ukv[:, h * 128 + MLA_NOPE:(h + 1) * 128])
        for d in range(MLA_ROPE):
            for half in (0, MLA_SLOT):
                place[half + d, s + MLA_NOPE + d] = 1.0
                place[half + src[d], rot_off + s + MLA_NOPE + d] = sign[d]
    return wq.astype(BF16), wk.astype(BF16), wv.astype(BF16), jnp.asarray(place, BF16)


def _rope_tables(n_tok, n_ctx):
    cos = np.ones((n_tok + n_ctx, MLA_SLOT), np.float32)
    sin = np.zeros((n_tok + n_ctx, MLA_SLOT), np.float32)
    t = np.arange(n_tok)
    row = (t // GRID_W).astype(np.float32)
    col = (t % GRID_W).astype(np.float32)
    n_freq = MLA_ROPE // 4
    inv = (ROPE_BASE ** (-np.arange(n_freq, dtype=np.float32) / n_freq)).astype(np.float32)
    ar = row[:, None] * inv
    ac = col[:, None] * inv
    ang = np.concatenate([ar, ar, ac, ac], axis=-1).astype(np.float32)
    cos[:n_tok, MLA_NOPE:MLA_NOPE + MLA_ROPE] = np.cos(ang)
    sin[:n_tok, MLA_NOPE:MLA_NOPE + MLA_ROPE] = np.sin(ang)
    return jnp.asarray(cos), jnp.asarray(sin)


def _head_block_ones(n_heads, width):
    m = np.kron(np.eye(n_heads, dtype=np.float32), np.ones((width, width), np.float32))
    return jnp.asarray(m, BF16)


def _mod_kernel(c_ref, w_ref, b_ref, o_ref):
    c = c_ref[...]
    s = c * _sigmoid(c)
    o_ref[...] = jnp.dot(s, w_ref[...], preferred_element_type=F32) + b_ref[...]


def _modulation(cc, mod_w, mod_b, layer):
    r, d = cc.shape
    n = mod_w.shape[-1]
    tn = 1024
    return pl.pallas_call(
        _mod_kernel,
        grid=(n // tn,),
        in_specs=[pl.BlockSpec((r, d), lambda j: (0, 0)),
                  pl.BlockSpec((None, d, tn), lambda j: (layer, 0, j)),
                  pl.BlockSpec((1, tn), lambda j: (0, j))],
        out_specs=pl.BlockSpec((r, tn), lambda j: (0, j)),
        out_shape=jax.ShapeDtypeStruct((r, n), F32),
        compiler_params=_cparams(("arbitrary",)),
        name="modulation",
    )(cc, mod_w, mod_b.reshape(1, n))


def _row_select(tile_rows, tile_idx, n_lat, ctx_vec, lat_vec):
    row = tile_idx * tile_rows + lax.broadcasted_iota(jnp.int32, (tile_rows, 1), 0)
    return jnp.where(row < n_lat, lat_vec, ctx_vec)


def _norm_modulate(x_ref, h_ref, gain_ref, mc_ref, ml_ref, shift_row, tile_idx, n_lat):
    tm = x_ref.shape[0]
    x = x_ref[...]
    xn = x * lax.rsqrt(jnp.mean(x * x, axis=-1, keepdims=True) + RMS_EPS)
    sh_l, sh_c = ml_ref[shift_row:shift_row + 1, :], mc_ref[shift_row:shift_row + 1, :]
    amp_l = gain_ref[...] * (1.0 + ml_ref[shift_row + 1:shift_row + 2, :])
    amp_c = gain_ref[...] * (1.0 + mc_ref[shift_row + 1:shift_row + 2, :])
    all_latent = (tile_idx + 1) * tm <= n_lat

    @pl.when(all_latent)
    def _():
        h_ref[...] = (xn * amp_l + sh_l).astype(h_ref.dtype)

    @pl.when(jnp.logical_not(all_latent))
    def _():
        amp = _row_select(tm, tile_idx, n_lat, amp_c, amp_l)
        shift = _row_select(tm, tile_idx, n_lat, sh_c, sh_l)
        h_ref[...] = (xn * amp + shift).astype(h_ref.dtype)


def _inproj_first_kernel(x_ref, mc_ref, ml_ref, g_ref, w_ref, o_ref, h_ref, *, n_lat):
    @pl.when(pl.program_id(2) == 0)
    def _():
        _norm_modulate(x_ref, h_ref, g_ref, mc_ref, ml_ref, 0, pl.program_id(1), n_lat)

    o_ref[...] = jnp.dot(h_ref[...], w_ref[...], preferred_element_type=F32).astype(o_ref.dtype)


def _inproj_second_kernel(h_ref, w_ref, o_ref):
    o_ref[...] = jnp.dot(h_ref[...], w_ref[...], preferred_element_type=F32).astype(o_ref.dtype)


def _inproj(xs, mod_c, mod_l, gain, w, layer, n_lat):
    b, s, d = xs.shape
    tm = _pick_tile(s, 1088)
    tn = PROJ_TILE
    n32 = P32_COLS // tn
    p16, h = pl.pallas_call(
        functools.partial(_inproj_first_kernel, n_lat=n_lat),
        grid=(b, s // tm, P16_COLS // tn),
        in_specs=[pl.BlockSpec((None, tm, d), lambda bi, i, j: (bi, i, 0)),
                  pl.BlockSpec((8, d), lambda bi, i, j: (0, 0)),
                  pl.BlockSpec((None, 8, d), lambda bi, i, j: (bi, 0, 0)),
                  pl.BlockSpec((1, d), lambda bi, i, j: (0, 0)),
                  pl.BlockSpec((None, d, tn), lambda bi, i, j: (layer, 0, j + n32))],
        out_specs=[pl.BlockSpec((None, tm, tn), lambda bi, i, j: (bi, i, j)),
                   pl.BlockSpec((None, tm, d), lambda bi, i, j: (bi, i, 0))],
        out_shape=[jax.ShapeDtypeStruct((b, s, P16_COLS), BF16), jax.ShapeDtypeStruct((b, s, d), BF16)],
        compiler_params=_cparams(("parallel", "parallel", "arbitrary")),
        name="inproj_bf16",
    )(xs, mod_c, mod_l, gain, w)
    tm2 = _pick_tile(s, 2 * tm)
    p32 = pl.pallas_call(
        _inproj_second_kernel,
        grid=(b, s // tm2, n32),
        in_specs=[pl.BlockSpec((None, tm2, d), lambda bi, i, j: (bi, i, 0)),
                  pl.BlockSpec((None, d, tn), lambda bi, i, j: (layer, 0, j))],
        out_specs=pl.BlockSpec((None, tm2, tn), lambda bi, i, j: (bi, i, j)),
        out_shape=jax.ShapeDtypeStruct((b, s, P32_COLS), F32),
        compiler_params=_cparams(("parallel", "parallel", "arbitrary")),
        name="inproj_f32",
    )(h, w)
    return p32, p16


def _mla_prep_kernel(ql_ref, kv_ref, cos_ref, sin_ref, gq_ref, gkv_ref, wq_ref, wk_ref, wv_ref, pl_ref,
                     q_ref, k_ref, v_ref, *, scale):
    nslot = MLA_HEADS * MLA_SLOT
    cos = jnp.concatenate([cos_ref[...]] * MLA_HEADS, axis=-1)
    sin = jnp.concatenate([sin_ref[...]] * MLA_HEADS, axis=-1)

    ql = ql_ref[...].astype(F32)
    qn = ql * lax.rsqrt(jnp.mean(ql * ql, axis=-1, keepdims=True) + RMS_EPS) * gq_ref[...]
    q2 = jnp.dot(qn.astype(BF16), wq_ref[...], preferred_element_type=F32)
    q = (q2[:, :nslot] * cos + q2[:, nslot:] * sin) * scale
    q_ref[...] = q.astype(BF16)

    kvkr = kv_ref[...]
    kvl = kvkr[:, :MLA_KV_RANK]
    kr = kvkr[:, MLA_KV_RANK:]
    kvn = (kvl * lax.rsqrt(jnp.mean(kvl * kvl, axis=-1, keepdims=True) + RMS_EPS) * gkv_ref[...]).astype(BF16)
    kn = jnp.dot(kvn, wk_ref[...], preferred_element_type=F32)
    v_ref[...] = jnp.dot(kvn, wv_ref[...], preferred_element_type=F32).astype(BF16)
    kr_hi = kr.astype(BF16)
    kr_lo = (kr - kr_hi.astype(F32)).astype(BF16)
    kr2 = jnp.dot(jnp.concatenate([kr_hi, kr_lo], axis=-1), pl_ref[...], preferred_element_type=F32)
    k_ref[...] = (kn + kr2[:, :nslot] * cos + kr2[:, nslot:] * sin).astype(BF16)


def _mla_prep(p32, p16, cos, sin, gq, gkv, wq, wk, wv, place):
    b, s, _ = p32.shape
    tm = _pick_tile(s, 544)
    nslot = MLA_HEADS * MLA_SLOT
    scale = (MLA_NOPE + MLA_ROPE) ** -0.5 * LOG2_E
    return pl.pallas_call(
        functools.partial(_mla_prep_kernel, scale=scale),
        grid=(b, s // tm),
        in_specs=[pl.BlockSpec((None, tm, MLA_Q_RANK), lambda bi, i: (bi, i, P_MLA_Q // MLA_Q_RANK)),
                  pl.BlockSpec((None, tm, 2 * MLA_SLOT), lambda bi, i: (bi, i, P_MLA_KV // (2 * MLA_SLOT))),
                  pl.BlockSpec((tm, MLA_SLOT), lambda bi, i: (i, 0)),
                  pl.BlockSpec((tm, MLA_SLOT), lambda bi, i: (i, 0)),
                  _full(gq), _full(gkv), _full(wq), _full(wk), _full(wv), _full(place)],
        out_specs=[pl.BlockSpec((None, tm, nslot), lambda bi, i: (bi, i, 0)),
                   pl.BlockSpec((None, tm, nslot), lambda bi, i: (bi, i, 0)),
                   pl.BlockSpec((None, tm, MLA_HEADS * MLA_V), lambda bi, i: (bi, i, 0))],
        out_shape=[jax.ShapeDtypeStruct((b, s, nslot), BF16),
                   jax.ShapeDtypeStruct((b, s, nslot), BF16),
                   jax.ShapeDtypeStruct((b, s, MLA_HEADS * MLA_V), BF16)],
        compiler_params=_cparams(("parallel", "parallel")),
        name="mla_prep",
    )(p16, p32, cos, sin, gq, gkv, wq, wk, wv, place)


def _flash_softmax(h, s, m_ref, l_ref, first):
    row_max = jnp.max(s, axis=-1, keepdims=True)
    m_new = jnp.broadcast_to(row_max, m_ref.shape[1:]) if first else jnp.maximum(m_ref[h], row_max)
    p = jnp.exp2(s - jnp.concatenate([m_new] * (s.shape[1] // 128), axis=-1))
    row_sum = jnp.sum(p, axis=-1, keepdims=True)
    if first:
        alpha = None
        l_ref[h] = jnp.broadcast_to(row_sum, l_ref.shape[1:])
    else:
        alpha = jnp.exp2(m_ref[h] - m_new)
        l_ref[h] = alpha * l_ref[h] + row_sum
    m_ref[h] = m_new
    return alpha, p.astype(BF16)


def _mla_flash_kernel(*refs, tk, aliased):
    if aliased:
        q_ref, k_ref, v_ref, _, o_ref, m_ref, l_ref, acc_ref = refs
    else:
        q_ref, k_ref, v_ref, o_ref, m_ref, l_ref, acc_ref = refs
    tq = q_ref.shape[0]
    nk = k_ref.shape[0]
    n_loop = nk // tk

    def chunk(rows, first=False):
        heads = range(MLA_HEADS)
        hs = [slice(h * MLA_SLOT, (h + 1) * MLA_SLOT) for h in heads]
        vs = [slice((h // 2) * 128, (h // 2) * 128 + 128) for h in heads]
        s = [lax.dot_general(q_ref[:, hs[h]], k_ref[rows, hs[h]], (((1,), (1,)), ((), ())),
                             preferred_element_type=F32) for h in heads]
        ap = [_flash_softmax(h, s[h], m_ref, l_ref, first) for h in heads]
        for h in heads:
            alpha, p = ap[h]
            pv = jnp.dot(p, v_ref[rows, vs[h]], preferred_element_type=F32)
            acc_ref[h] = pv if first else alpha * acc_ref[h] + pv

    has_tail = nk > n_loop * tk
    if has_tail:
        chunk(slice(n_loop * tk, nk), first=True)
    else:
        chunk(slice(0, tk), first=True)
    lo = 0 if has_tail else 1
    if n_loop > lo:
        def body(j, carry):
            chunk(pl.ds(pl.multiple_of(j * tk, tk), tk))
            return carry
        lax.fori_loop(lo, n_loop, body, 0)

    lane = lax.broadcasted_iota(jnp.int32, (tq, 128), 1)
    outs = []
    for pair in range(MLA_HEADS // 2):
        o0 = acc_ref[2 * pair] / l_ref[2 * pair]
        o1 = acc_ref[2 * pair + 1] / l_ref[2 * pair + 1]
        outs.append(jnp.where(lane < MLA_V, o0, o1))
    o_ref[...] = jnp.concatenate(outs, axis=-1).astype(o_ref.dtype)


def _mla_flash(q, k, v, n_lat, prev=None):
    b, s, nslot = q.shape
    n_ctx = s - n_lat
    nv = MLA_HEADS * MLA_V
    tk = 512
    if prev is None:
        tq = _pick_tile(n_lat, 1024)
        q_off, n_q, kv_rows, kv_blk = 0, n_lat // tq, s, 0
    else:
        assert n_lat % n_ctx == 0
        tq = _pick_tile(n_ctx, 256)
        q_off, n_q, kv_rows, kv_blk = n_lat // tq, n_ctx // tq, n_ctx, n_lat // n_ctx
    in_specs = [pl.BlockSpec((None, tq, nslot), lambda bi, i: (bi, i + q_off, 0)),
                pl.BlockSpec((None, kv_rows, nslot), lambda bi, i: (bi, kv_blk, 0)),
                pl.BlockSpec((None, kv_rows, nv), lambda bi, i: (bi, kv_blk, 0))]
    args = [q, k, v]
    aliases = {}
    if prev is not None:
        in_specs.append(pl.BlockSpec(memory_space=pl.ANY))
        args.append(prev)
        aliases = {3: 0}
    return pl.pallas_call(
        functools.partial(_mla_flash_kernel, tk=tk, aliased=prev is not None),
        grid=(b, n_q),
        in_specs=in_specs,
        out_specs=pl.BlockSpec((None, tq, nv), lambda bi, i: (bi, i + q_off, 0)),
        out_shape=jax.ShapeDtypeStruct((b, s, nv), BF16),
        scratch_shapes=[pltpu.VMEM((MLA_HEADS, tq, 128), F32),
                        pltpu.VMEM((MLA_HEADS, tq, 128), F32),
                        pltpu.VMEM((MLA_HEADS, tq, 128), F32)],
        input_output_aliases=aliases,
        compiler_params=_cparams(("parallel", "arbitrary")),
        name="mla_flash",
    )(*args)


NA_QROWS = 4
NA_SLAB = NA_QROWS + NA_WIN_ROWS


def _na_geometry(rows):
    assert rows % NA_QROWS == 0 and rows >= NA_SLAB
    nblk = rows // NA_QROWS
    qc = np.arange(GRID_W)
    cs = np.clip(qc - NA_WIN_COLS // 2, 0, GRID_W - NA_WIN_COLS)
    col_valid = (qc[None, :] >= cs[:, None]) & (qc[None, :] < cs[:, None] + NA_WIN_COLS)
    rel_c = np.clip(qc[None, :] - qc[:, None] + NA_WIN_COLS - 1, 0, 2 * NA_WIN_COLS - 2)
    onehot_c = (rel_c[None] == np.arange(2 * NA_WIN_COLS - 1)[:, None, None]) & col_valid[None]
    patterns, cls, starts = {}, [], []
    for i in range(nblk):
        r0 = i * NA_QROWS
        start = int(np.clip(r0 - NA_WIN_ROWS // 2, 0, rows - NA_SLAB))
        qr = r0 + np.arange(NA_QROWS)
        rs = np.clip(qr - NA_WIN_ROWS // 2, 0, rows - NA_WIN_ROWS)
        key = (start - r0,) + tuple((rs - r0).tolist())
        if key not in patterns:
            kr = start + np.arange(NA_SLAB)
            row_valid = (kr[None, :] >= rs[:, None]) & (kr[None, :] < rs[:, None] + NA_WIN_ROWS)
            rel_r = np.clip(kr[None, :] - qr[:, None] + NA_WIN_ROWS - 1, 0, 2 * NA_WIN_ROWS - 2)
            patterns[key] = (len(patterns), np.where(row_valid, rel_r, -1))
        cls.append(patterns[key][0])
        starts.append(start)
    ordered = sorted(patterns.values(), key=lambda z: z[0])
    rel_rows = np.stack([z[1] for z in ordered])
    meta = np.stack([np.asarray(cls, np.int32), np.asarray(starts, np.int32)])
    return meta, rel_rows, (onehot_c.astype(np.float32), col_valid)


def _na_bias_table(rpb, rel_rows, col_tables):
    onehot_c, col_valid = col_tables
    bc = jnp.einsum('hrc,cqk->hrqk', rpb.astype(F32), onehot_c, precision=lax.Precision.HIGHEST)
    bc = jnp.where(col_valid, bc, -jnp.inf)
    outside = jnp.full(bc.shape[:1] + bc.shape[2:], -jnp.inf, F32)
    pats = []
    for pat in rel_rows:
        qrows = [jnp.concatenate([bc[:, r] if r >= 0 else outside for r in row], axis=-1) for row in pat]
        pats.append(jnp.concatenate(qrows, axis=1))
    return jnp.stack(pats)


def _na_scores(q_pair, lane, h, k_parts):
    in_head = (lane < NA_DH) if h % 2 == 0 else (lane >= NA_DH)
    qm = jnp.where(in_head, q_pair * (NA_DH ** -0.5), 0.0).astype(BF16)
    return [lax.dot_general(qm, k, (((1,), (1,)), ((), ())), preferred_element_type=F32) for k in k_parts]


def _na_softmax(scores, bias):
    if bias is not None:
        scores = [scores[0] + bias] + scores[1:]
    m = scores[0].max(axis=-1, keepdims=True)
    for s in scores[1:]:
        m = jnp.maximum(m, s.max(axis=-1, keepdims=True))
    probs = [jnp.exp(s - m) for s in scores]
    den = probs[0].sum(axis=-1, keepdims=True)
    for p in probs[1:]:
        den = den + p.sum(axis=-1, keepdims=True)
    return [p.astype(BF16) for p in probs], den


def _na_kernel(meta_ref, q_ref, k_ref, v_ref, bias_ref, o_ref, *, n_lat, with_ctx):
    i = pl.program_id(1)
    nq = q_ref.shape[0]
    n_lat_tiles = n_lat // nq
    n_all = k_ref.shape[0]
    lane = lax.broadcasted_iota(jnp.int32, (nq, 128), 1)

    def run(windowed):
        key_rows = [pl.ds(n_lat, n_all - n_lat)]
        if windowed:
            start = pl.multiple_of(meta_ref[1, jnp.minimum(i, n_lat_tiles - 1)] * GRID_W, GRID_W)
            key_rows = [pl.ds(start, NA_SLAB * GRID_W)] + key_rows
        pair_lanes = [slice((h // 2) * 128, (h // 2 + 1) * 128) for h in range(NA_HEADS)]
        scores = [_na_scores(q_ref[:, pair_lanes[h]], lane, h, [k_ref[r, pair_lanes[h]] for r in key_rows])
                  for h in range(NA_HEADS)]
        soft = [_na_softmax(scores[h], bias_ref[h] if windowed else None) for h in range(NA_HEADS)]
        heads = []
        for h in range(NA_HEADS):
            probs, den = soft[h]
            out = None
            for p, r in zip(probs, key_rows):
                po = jnp.dot(p, v_ref[r, pair_lanes[h]], preferred_element_type=F32)
                out = po if out is None else out + po
            heads.append(out / den)
        outs = [jnp.where(lane < NA_DH, heads[2 * pair], heads[2 * pair + 1]) for pair in range(NA_HEADS // 2)]
        o_ref[...] = jnp.concatenate(outs, axis=-1).astype(o_ref.dtype)

    if with_ctx:
        pl.when(i < n_lat_tiles)(lambda: run(True))
        pl.when(i >= n_lat_tiles)(lambda: run(False))
    else:
        run(True)


def _na_attention(p, meta, table, n_lat, with_ctx_queries):
    b, s, _ = p.shape
    nq = NA_QROWS * GRID_W
    nk = NA_SLAB * GRID_W
    assert (s - n_lat) % nq == 0
    nlt = n_lat // nq
    qb = P_NA_QKV // NA_W
    grid_spec = pltpu.PrefetchScalarGridSpec(
        num_scalar_prefetch=1,
        grid=(b, s // nq if with_ctx_queries else nlt),
        in_specs=[pl.BlockSpec((None, nq, NA_W), lambda bi, i, m: (bi, i, qb)),
                  pl.BlockSpec((None, s, NA_W), lambda bi, i, m: (bi, 0, qb + 1)),
                  pl.BlockSpec((None, s, NA_W), lambda bi, i, m: (bi, 0, qb + 2)),
                  pl.BlockSpec((None, NA_HEADS, nq, nk),
                               lambda bi, i, m: (m[0, jnp.minimum(i, nlt - 1)], 0, 0, 0))],
        out_specs=pl.BlockSpec((None, nq, NA_W), lambda bi, i, m: (bi, i, 0)))
    return pl.pallas_call(
        functools.partial(_na_kernel, n_lat=n_lat, with_ctx=with_ctx_queries),
        grid_spec=grid_spec,
        out_shape=jax.ShapeDtypeStruct((b, s, NA_W), BF16),
        compiler_params=_cparams(("parallel", "arbitrary")),
        name="na_attention",
    )(meta, p, p, p, table)


def _tile_conv(x, prev, nxt, w_ref, tile_idx, n_tiles, n_lat_tiles):
    r = x.shape[0]
    width = w_ref.shape[0]
    left = width // 2
    has_prev = jnp.logical_and(tile_idx != 0, tile_idx != n_lat_tiles)
    has_next = jnp.logical_and(tile_idx != n_tiles - 1, tile_idx != n_lat_tiles - 1)
    prev = jnp.where(has_prev, prev, 0.0)
    nxt = jnp.where(has_next, nxt, 0.0)
    xe = jnp.concatenate([prev, x, nxt], axis=0)
    acc = None
    for j in range(width):
        o = HALO - left + j
        term = xe[o:o + r, :] * w_ref[j:j + 1, :]
        acc = term if acc is None else acc + term
    return acc


def _halo_specs(width, col_block, tile_of):
    per = SEQ_TILE // HALO

    def main(bi, s, *_):
        return (bi, tile_of(s), col_block)

    def prev(bi, s, *_):
        return (bi, jnp.maximum(tile_of(s) * per - 1, 0), col_block)

    def make_next(n_tiles):
        def nxt(bi, s, *_):
            return (bi, jnp.minimum((tile_of(s) + 1) * per, n_tiles * per - 1), col_block)
        return nxt

    return main, prev, make_next


GDN_PREP_GROUP = 2


def _gdn_prep_kernel(*refs, n_lat_tiles):
    for e in range(refs[0].shape[0]):
        _gdn_prep_one(*(r.at[e] for r in refs[:4]), *refs[4:9], *(r.at[e] for r in refs[9:]),
                      n_lat_tiles=n_lat_tiles)


def _gdn_prep_one(x_ref, xp_ref, xn_ref, ba_ref, cw_ref, ones_ref, exp_ref, alog_ref, dtb_ref,
                  q_ref, k_ref, v_ref, beta_ref, g_ref, *, n_lat_tiles):
    i = pl.program_id(1)
    y = _tile_conv(x_ref[...], xp_ref[...], xn_ref[...], cw_ref, i, pl.num_programs(1), n_lat_tiles)
    y = y * _sigmoid(y)
    q = y[:, :GDN_W]
    k = y[:, GDN_W:2 * GDN_W]
    v_ref[...] = y[:, 2 * GDN_W:].astype(v_ref.dtype)

    def head_norm(u):
        parts = jnp.concatenate(_split(u * u), axis=-1)
        ss = jnp.dot(parts, ones_ref[...], preferred_element_type=F32)
        return u * lax.rsqrt(ss + RMS_EPS)

    q_ref[...] = (head_norm(q) * (GDN_DK ** -0.5)).astype(q_ref.dtype)
    k_ref[...] = head_norm(k).astype(k_ref.dtype)

    ba = ba_ref[...]
    a = ba + dtb_ref[...]
    softplus = jnp.maximum(a, 0.0) + jnp.log1p(jnp.exp(-jnp.abs(a)))
    lane = lax.broadcasted_iota(jnp.int32, ba.shape, 1)
    compact = jnp.where(lane < 2 * GDN_HEADS, _sigmoid(ba), -jnp.exp(alog_ref[...]) * softplus)
    wide = jnp.dot(jnp.concatenate(_split(compact), axis=-1), exp_ref[...], preferred_element_type=F32)
    half = 2 * GDN_W
    beta_ref[...] = wide[:, :half].astype(beta_ref.dtype)
    g_ref[...] = wide[:, half:]


def _gdn_prep(p, conv_w, a_log, dt_bias, n_lat):
    b, s, _ = p.shape
    n_tiles = s // SEQ_TILE
    main, prev, make_next = _halo_specs(3 * GDN_W, 0, lambda t: t)
    ones_n = jnp.concatenate([_head_block_ones(GDN_HEADS, GDN_DK)] * N_SPLIT, axis=0)
    expand = np.zeros((128, 4 * GDN_W), np.float32)
    for kind in range(2):
        for d in range(2):
            for h in range(GDN_HEADS):
                c0 = kind * 2 * GDN_W + d * GDN_W + h * GDN_DV
                expand[kind * 2 * GDN_HEADS + d * GDN_HEADS + h, c0:c0 + GDN_DV] = 1.0
    expand_n = jnp.asarray(np.concatenate([expand] * N_SPLIT, axis=0), BF16)
    lanes = jnp.zeros((1, 128), F32)
    alog_e = lanes.at[0, 2 * GDN_HEADS:4 * GDN_HEADS].set(a_log.astype(F32).reshape(-1))
    dtb_e = lanes.at[0, 2 * GDN_HEADS:4 * GDN_HEADS].set(dt_bias.astype(F32).reshape(-1))
    grp = GDN_PREP_GROUP if b % GDN_PREP_GROUP == 0 else 1
    tok = lambda w: pl.BlockSpec((grp, SEQ_TILE, w), lambda bi, i: (bi, i, 0))
    return pl.pallas_call(
        functools.partial(_gdn_prep_kernel, n_lat_tiles=n_lat // SEQ_TILE),
        grid=(b // grp, n_tiles),
        in_specs=[pl.BlockSpec((grp, SEQ_TILE, 3 * GDN_W), main),
                  pl.BlockSpec((grp, HALO, 3 * GDN_W), prev),
                  pl.BlockSpec((grp, HALO, 3 * GDN_W), make_next(n_tiles)),
                  pl.BlockSpec((grp, SEQ_TILE, 128), lambda bi, i: (bi, i, P_GDN_BA // 128)),
                  _full(conv_w), _full(ones_n), _full(expand_n), _full(alog_e), _full(dtb_e)],
        out_specs=[tok(GDN_W), tok(GDN_W), tok(GDN_W), tok(2 * GDN_W), tok(2 * GDN_W)],
        out_shape=[jax.ShapeDtypeStruct((b, s, GDN_W), BF16)] * 3 + [jax.ShapeDtypeStruct((b, s, 2 * GDN_W), BF16),
                                                                    jax.ShapeDtypeStruct((b, s, 2 * GDN_W), F32)],
        compiler_params=_cparams(("parallel", "parallel")),
        name="gdn_prep",
    )(p, p, p, p, conv_w, ones_n, expand_n, alog_e, dtb_e)


GDN_PAIR = 2 * GDN_DK


def _gdn_masks():
    c, w = GDN_CHUNK, GDN_W
    r2, c2 = np.arange(GDN_PAIR)[:, None], np.arange(GDN_PAIR)[None, :]
    bd = ((r2 // c) == (c2 // c)).astype(np.float32)
    i = np.arange(c)[:, None]
    j = (np.arange(w) % c)[None, :]
    level = np.zeros((c, w), np.int32)
    for bit in range(6):
        level += ((i ^ j) >= (1 << bit)).astype(np.int32)
    lvl = np.stack([(level == m).astype(np.float32) for m in range(7)])
    dirm = np.stack([np.stack([(j <= i), (j < i)]), np.stack([(j >= i), (j > i)])]).astype(np.float32)
    tj = (np.arange(N_SPLIT * c) % c)[None, :]
    tri = np.stack([(tj <= i), (tj >= i)]).astype(np.float32)
    return jnp.asarray(bd, BF16), jnp.asarray(lvl), jnp.asarray(dirm), jnp.asarray(tri, BF16)


def _heads_mm(x, y, bd, transpose_rhs=False):
    xb = x.astype(BF16)
    yb = y.astype(BF16)
    outs = []
    for pair in range(GDN_W // GDN_PAIR):
        ls = slice(pair * GDN_PAIR, (pair + 1) * GDN_PAIR)
        w = jnp.concatenate([yb[:, ls], yb[:, ls]], axis=0) * bd
        dims = (((1,), (1,)), ((), ())) if transpose_rhs else (((1,), (0,)), ((), ()))
        outs.append(lax.dot_general(xb[:, ls], w, dims, preferred_element_type=F32))
    return jnp.concatenate(outs, axis=1)


def _gdn_intra(probs, bd, lvl_ref, dirm_ref, tri_ref):
    c = GDN_CHUNK
    n = len(probs)
    eye = lvl_ref[0]
    gc, g_last, decay, gram = [], [], [], []
    for q, k, v, beta, g, rev in probs:
        d = 1 if rev else 0
        gcp = jnp.dot(tri_ref[d], jnp.concatenate(_split(g), axis=0), preferred_element_type=F32)
        gc.append(gcp)
        g_last.append(gcp[0:1, :] if rev else gcp[c - 1:c, :])
        gc_row = jnp.sum(gcp * eye, axis=0, keepdims=True)
        decay.append(dirm_ref[d, 0] * jnp.exp(jnp.minimum(gcp - gc_row, 0.0)))
        gram.append(_heads_mm(jnp.concatenate([k, q], axis=0), k, bd, transpose_rhs=True))
    lower = [dirm_ref[1 if p[5] else 0, 1] * p[3] * gram[x][:c] * decay[x] for x, p in enumerate(probs)]
    a_intra = [gram[x][c:] * decay[x] for x in range(n)]
    t = [eye - lower[x] * lvl_ref[1] for x in range(n)]
    for lev in range(2, 7):
        y = [_heads_mm(t[x], lower[x] * lvl_ref[lev], bd) for x in range(n)]
        z = [_heads_mm(y[x], t[x], bd) for x in range(n)]
        t = [t[x] - z[x] for x in range(n)]
    e_gc = [jnp.exp(gc[x]) for x in range(n)]
    u = [_heads_mm(t[x], p[2] * p[3], bd) for x, p in enumerate(probs)]
    w = [_heads_mm(t[x], p[1] * p[3] * e_gc[x], bd) for x, p in enumerate(probs)]
    wq = [jnp.concatenate([w[x], p[0] * e_gc[x]], axis=0).astype(BF16) for x, p in enumerate(probs)]
    k_dec = [(p[1] * jnp.exp(g_last[x] - gc[x])).astype(BF16) for x, p in enumerate(probs)]
    g_tot = [jnp.exp(g_last[x]) for x in range(n)]
    return u, wq, k_dec, a_intra, g_tot


def _gdn_state_steps(items, bd):
    c = GDN_CHUNK
    pairs = [slice(p * GDN_PAIR, (p + 1) * GDN_PAIR) for p in range(GDN_W // GDN_PAIR)]
    bdf = bd.astype(F32)
    ws_qs = [jnp.concatenate([jnp.dot(wq[:, ls], s_ref[p].astype(BF16), preferred_element_type=F32)
                              for p, ls in enumerate(pairs)], axis=1)
             for _, wq, _, _, _, s_ref in items]
    v_new = [it[0] - ws[:c] for it, ws in zip(items, ws_qs)]
    outs = [ws[c:] + _heads_mm(it[3], vn, bd) for it, ws, vn in zip(items, ws_qs, v_new)]
    for (_, _, k_dec, _, g_tot, s_ref), vn in zip(items, v_new):
        vb = vn.astype(BF16)
        for p, ls in enumerate(pairs):
            kv = lax.dot_general(k_dec[:, ls], vb[:, ls], (((0,), (0,)), ((), ())), preferred_element_type=F32)
            s_ref[p] = s_ref[p] * g_tot[:, ls] + kv * bdf
    return outs


def _gdn_scan_body(qf, kf, vf, bf, gf, qb, kb, vb, bb, gb, bd_ref, lvl_ref, dirm_ref, tri_ref,
                   of_ref, ob_ref, sf_ref, sb_ref):
    bd = bd_ref[...]
    n = SEQ_TILE // GDN_CHUNK
    probs, rows = [], []
    for c in range(n):
        rf = slice(c * GDN_CHUNK, (c + 1) * GDN_CHUNK)
        rb = slice((n - 1 - c) * GDN_CHUNK, (n - c) * GDN_CHUNK)
        probs.append(tuple(r[rf, :].astype(F32) for r in (qf, kf, vf, bf, gf)) + (False,))
        probs.append(tuple(r[rb, :].astype(F32) for r in (qb, kb, vb, bb, gb)) + (True,))
        rows += [rf, rb]
    u, wq, k_dec, a_intra, g_tot = _gdn_intra(probs, bd, lvl_ref, dirm_ref, tri_ref)
    for c in range(n):
        xs = (2 * c, 2 * c + 1)
        outs = _gdn_state_steps([(u[x], wq[x], k_dec[x], a_intra[x], g_tot[x], sb_ref if probs[x][5] else sf_ref)
                                 for x in xs], bd)
        for x, o in zip(xs, outs):
            (ob_ref if probs[x][5] else of_ref)[rows[x], :] = o.astype(of_ref.dtype)


def _seq_tile_maps(n_lat_tiles, n_tiles):
    fwd = lambda s: lax.rem(s + n_lat_tiles, n_tiles)
    bwd = lambda s: n_tiles - 1 - s
    return fwd, bwd


def _gdn_scan_specs(q, k, v, beta, g, n_lat):
    b, s, _ = q.shape
    n_tiles = s // SEQ_TILE
    fwd, bwd = _seq_tile_maps(n_lat // SEQ_TILE, n_tiles)
    spec = lambda tile_of, col: pl.BlockSpec((None, SEQ_TILE, GDN_W), lambda bi, t: (bi, tile_of(t), col))
    masks = _gdn_masks()
    in_specs = [spec(fwd, 0)] * 5 + [spec(bwd, 0)] * 3 + [spec(bwd, 1)] * 2 + [_full(m) for m in masks]
    assert len(in_specs) == N_GDN_IN
    return (in_specs, [q, k, v, beta, g, q, k, v, beta, g, *masks], [spec(fwd, 0), spec(bwd, 0)],
            [jax.ShapeDtypeStruct((b, s, GDN_W), BF16)] * 2,
            [pltpu.VMEM((GDN_W // GDN_PAIR, GDN_PAIR, GDN_PAIR), F32)] * 2)


def _lru_tile_prep(x_ref, xp_ref, xn_ref, cw_ref, cb_ref, wg_ref, bg_ref, nla_ref, a_ref, b_ref,
                   tile_idx, n_tiles, n_lat_tiles):
    xb = _tile_conv(x_ref[...], xp_ref[...], xn_ref[...], cw_ref, tile_idx, n_tiles, n_lat_tiles) + cb_ref[...]
    gates = _sigmoid(jnp.dot(xb.astype(BF16), wg_ref[...], preferred_element_type=F32) + bg_ref[...])
    log_a = nla_ref[...] * gates[:, :LRU_W]
    a_ref[...] = jnp.exp(log_a)
    th = jnp.tanh(log_a)
    b_ref[...] = jnp.sqrt(-2.0 * th / (1.0 - th)) * gates[:, LRU_W:] * xb


def _scan_group(a, b, h, row, reverse):
    for d in (1, 2, 4):
        if reverse:
            keep = row < 8 - d
            shift = 8 - d
        else:
            keep = row >= d
            shift = d
        a_s = jnp.where(keep, pltpu.roll(a, shift, 0), 1.0)
        b_s = jnp.where(keep, pltpu.roll(b, shift, 0), 0.0)
        b = a * b_s + b
        a = a * a_s
    return a * h + b


def _lru_scan_body(xf, xfp, xfn, xb, xbp, xbn, cw_ref, cb_ref, wgf, bgf, nlaf, wgb, bgb, nlab,
                   hf_ref, hb_ref, af_ref, bf_ref, ab_ref, bb_ref, cf_ref, cbk_ref, *, n_lat_tiles):
    s = pl.program_id(1)
    n_tiles = pl.num_programs(1)
    t_f, t_b = (m(s) for m in _seq_tile_maps(n_lat_tiles, n_tiles))
    _lru_tile_prep(xf, xfp, xfn, cw_ref, cb_ref, wgf, bgf, nlaf, af_ref, bf_ref, t_f, n_tiles, n_lat_tiles)
    _lru_tile_prep(xb, xbp, xbn, cw_ref, cb_ref, wgb, bgb, nlab, ab_ref, bb_ref, t_b, n_tiles, n_lat_tiles)

    n_groups = SEQ_TILE // 8
    row = lax.broadcasted_iota(jnp.int32, (8, LRU_W), 0)

    h_f, h_b = cf_ref[...], cbk_ref[...]
    pend_f, pend_b = None, None
    for gi in range(n_groups):
        rf = slice(gi * 8, gi * 8 + 8)
        rb = slice((n_groups - 1 - gi) * 8, (n_groups - gi) * 8)
        out_f = _scan_group(af_ref[rf, :], bf_ref[rf, :], h_f, row, False)
        out_b = _scan_group(ab_ref[rb, :], bb_ref[rb, :], h_b, row, True)
        if gi % 2 == 0:
            pend_f, pend_b = out_f, out_b
        else:
            hf_ref[(gi - 1) * 8:(gi + 1) * 8, :] = jnp.concatenate([pend_f, out_f], axis=0).astype(hf_ref.dtype)
            hb_ref[(n_groups - 1 - gi) * 8:(n_groups + 1 - gi) * 8, :] = (
                jnp.concatenate([out_b, pend_b], axis=0).astype(hb_ref.dtype))
        h_f = jnp.broadcast_to(out_f[7:8, :], (8, LRU_W))
        h_b = jnp.broadcast_to(out_b[0:1, :], (8, LRU_W))
    cf_ref[...] = h_f
    cbk_ref[...] = h_b


N_LRU_IN, N_GDN_IN, N_LRU_SCRATCH = 14, 14, 6


def _seq_mixers_kernel(*refs, n_lat_tiles):
    lru_in = refs[:N_LRU_IN]
    gdn_in = refs[N_LRU_IN:N_LRU_IN + N_GDN_IN]
    hf_ref, hb_ref, of_ref, ob_ref = refs[N_LRU_IN + N_GDN_IN:N_LRU_IN + N_GDN_IN + 4]
    scratch = refs[N_LRU_IN + N_GDN_IN + 4:]
    lru_scratch, gdn_scratch = scratch[:N_LRU_SCRATCH], scratch[N_LRU_SCRATCH:]

    @pl.when(pl.program_id(1) == 0)
    def _():
        for ref in lru_scratch[4:] + gdn_scratch:
            ref[...] = jnp.zeros(ref.shape, ref.dtype)

    _lru_scan_body(*lru_in, hf_ref, hb_ref, *lru_scratch, n_lat_tiles=n_lat_tiles)
    _gdn_scan_body(*gdn_in, of_ref, ob_ref, *gdn_scratch)


def _lru_scan_specs(p, conv_w, conv_b, w_r, b_r, w_i, b_i, lam, n_lat):
    b, s, _ = p.shape
    n_tiles = s // SEQ_TILE
    nlt = n_lat // SEQ_TILE
    fwd, bwd = _seq_tile_maps(nlt, n_tiles)
    col = P_LRU_X // LRU_W
    specs = []
    for tile_of in (fwd, bwd):
        main, prev, make_next = _halo_specs(LRU_W, col, tile_of)
        specs += [pl.BlockSpec((None, SEQ_TILE, LRU_W), main), pl.BlockSpec((None, HALO, LRU_W), prev),
                  pl.BlockSpec((None, HALO, LRU_W), make_next(n_tiles))]

    def blockdiag(w):
        return jax.scipy.linalg.block_diag(*[w[n] for n in range(LRU_BLOCKS)])

    dir_args = []
    for d in range(2):
        wg = jnp.concatenate([blockdiag(w_r[d]), blockdiag(w_i[d])], axis=1).astype(BF16)
        bg = jnp.concatenate([b_r[d], b_i[d]]).astype(F32).reshape(1, 2 * LRU_W)
        nla = (-LRU_C * jax.nn.softplus(-lam[d].astype(F32))).reshape(1, LRU_W)
        dir_args += [wg, bg, nla]
    cb2 = conv_b.reshape(1, LRU_W)
    out_spec = lambda tile_of: pl.BlockSpec((None, SEQ_TILE, LRU_W), lambda bi, t: (bi, tile_of(t), 0))
    in_specs = specs + [_full(conv_w), _full(cb2)] + [_full(a) for a in dir_args]
    assert len(in_specs) == N_LRU_IN
    return (in_specs, [p, p, p, p, p, p, conv_w, cb2, *dir_args], [out_spec(fwd), out_spec(bwd)],
            [jax.ShapeDtypeStruct((b, s, LRU_W), BF16)] * 2,
            [pltpu.VMEM((SEQ_TILE, LRU_W), F32)] * 4 + [pltpu.VMEM((8, LRU_W), F32)] * 2)


def _seq_mixers(lru_parts, gdn_parts, bsz, n_tiles, n_lat):
    in_specs, args, out_specs, out_shape, scratch = ([*a, *b] for a, b in zip(lru_parts, gdn_parts))
    return pl.pallas_call(
        functools.partial(_seq_mixers_kernel, n_lat_tiles=n_lat // SEQ_TILE),
        grid=(bsz, n_tiles),
        in_specs=in_specs,
        out_specs=out_specs,
        out_shape=out_shape,
        scratch_shapes=scratch,
        compiler_params=_cparams(("parallel", "arbitrary")),
        name="seq_mixers",
    )(*args)


def _merge_kernel(x_ref, mc_ref, ml_ref, of_ref, ob_ref, z_ref, hf_ref, hb_ref, y_ref, uc_ref, ud_ref,
                  t0_ref, t1_ref, t2_ref, t3_ref, gn_ref, ones_ref, bg_ref, wb_ref, wo_ref, o_ref, *, n_lat):
    project = lambda u, n: jnp.dot(u, wb_ref[n], preferred_element_type=F32)
    o = of_ref[...].astype(F32) + ob_ref[...].astype(F32)
    ms = jnp.dot(jnp.concatenate(_split(o * o), axis=-1), ones_ref[...], preferred_element_type=F32) * (1.0 / GDN_DV)
    proj = {2: project(uc_ref[...], 2), 3: project(ud_ref[...], 3)}
    z = z_ref[...]
    ua = (o * lax.rsqrt(ms + RMS_EPS) * gn_ref[...] * (z * _sigmoid(z))).astype(BF16)
    y = y_ref[...]
    gelu = 0.5 * y * (1.0 + jnp.tanh(0.7978845608028654 * (y + 0.044715 * (y * y * y))))
    ub = ((hf_ref[...].astype(F32) + hb_ref[...].astype(F32)) * gelu).astype(BF16)
    proj[0] = project(ua, 0)
    proj[1] = project(ub, 1)

    merged = None
    for n, t_ref in enumerate((t0_ref, t1_ref, t2_ref, t3_ref)):
        gate = _sigmoid(t_ref[...] + bg_ref[n:n + 1, :].astype(BF16))
        term = gate.astype(F32) * proj[n]
        merged = term if merged is None else merged + term
    out = jnp.dot(merged.astype(BF16), wo_ref[...], preferred_element_type=F32)
    g1 = _row_select(x_ref.shape[0], pl.program_id(1), n_lat, mc_ref[2:3, :], ml_ref[2:3, :])
    o_ref[...] = x_ref[...] + g1 * out


def _merge(xs, mod_c, mod_l, gdn_out, lru_out, uc, ud, p32, p16, gdn_norm_g, b_gate, w_branch, w_out, layer, n_lat,
           with_ctx):
    b, s, d = xs.shape
    rows = s if with_ctx else n_lat
    tm = _pick_tile(rows, 544)
    gate_blk = P_GATES // d
    tok = lambda w: pl.BlockSpec((None, tm, w), lambda bi, i: (bi, i, 0))
    ones_n = jnp.concatenate([_head_block_ones(GDN_HEADS, GDN_DV)] * N_SPLIT, axis=0)
    gn = jnp.tile(gdn_norm_g.astype(F32), GDN_HEADS).reshape(1, GDN_W)
    pcol = lambda c: pl.BlockSpec((None, tm, BRANCH_W), lambda bi, i: (bi, i, c))
    in_specs = [tok(d), pl.BlockSpec((8, d), lambda bi, i: (0, 0)), pl.BlockSpec((None, 8, d), lambda bi, i: (bi, 0, 0))]
    in_specs += [tok(BRANCH_W), tok(BRANCH_W), pcol(P_GDN_Z // GDN_W), tok(BRANCH_W), tok(BRANCH_W),
                 pcol(P_LRU_Y // LRU_W), tok(BRANCH_W), tok(BRANCH_W)]
    in_specs += [pl.BlockSpec((None, tm, d), functools.partial(lambda bi, i, n: (bi, i, gate_blk + n), n=n))
                 for n in range(N_BRANCH)]
    in_specs += [_full(gn), _full(ones_n), _full(b_gate),
                 pl.BlockSpec((None,) + w_branch.shape[1:], lambda bi, i: (layer, 0, 0, 0)),
                 pl.BlockSpec((None,) + w_out.shape[1:], lambda bi, i: (layer, 0, 0))]
    return pl.pallas_call(
        functools.partial(_merge_kernel, n_lat=n_lat),
        grid=(b, rows // tm),
        in_specs=in_specs,
        out_specs=tok(d),
        out_shape=jax.ShapeDtypeStruct((b, rows, d), F32),
        input_output_aliases={0: 0} if with_ctx else {},
        compiler_params=_cparams(("parallel", "parallel")),
        name="merge",
    )(xs, mod_c, mod_l, gdn_out[0], gdn_out[1], p32, lru_out[0], lru_out[1], p32, uc, ud, p16, p16, p16, p16,
      gn, ones_n, b_gate, w_branch, w_out)


def _mlp_kernel(x_ref, mc_ref, ml_ref, gn_ref, gf_ref, w1_ref, w2_ref, o_ref, h_ref, acc_ref, *, n_lat, final_norm):
    f = pl.program_id(2)
    tm = x_ref.shape[0]
    i = pl.program_id(1)

    @pl.when(f == 0)
    def _():
        _norm_modulate(x_ref, h_ref, gn_ref, mc_ref, ml_ref, 3, i, n_lat)
        acc_ref[...] = jnp.zeros(acc_ref.shape, F32)

    a = jnp.maximum(jnp.dot(h_ref[...], w1_ref[...], preferred_element_type=F32), 0.0)
    acc_ref[...] += jnp.dot((a * a).astype(BF16), w2_ref[...], preferred_element_type=F32)

    @pl.when(f == pl.num_programs(2) - 1)
    def _():
        g2 = _row_select(tm, i, n_lat, mc_ref[5:6, :], ml_ref[5:6, :])
        y = x_ref[...] + g2 * acc_ref[...]
        if final_norm:
            y = y * lax.rsqrt(jnp.mean(y * y, axis=-1, keepdims=True) + RMS_EPS) * gf_ref[...]
        o_ref[...] = y


def _mlp(xs, mod_c, mod_l, gain, w1, w2, layer, final_gain, n_lat, final_norm):
    b, rows, d = xs.shape
    dff = w1.shape[-1]
    tm = _pick_tile(rows, 1088)
    tf = 1024
    row = pl.BlockSpec((1, d), lambda bi, i, f: (0, 0))
    return pl.pallas_call(
        functools.partial(_mlp_kernel, n_lat=n_lat, final_norm=final_norm),
        grid=(b, rows // tm, dff // tf),
        in_specs=[pl.BlockSpec((None, tm, d), lambda bi, i, f: (bi, i, 0)),
                  pl.BlockSpec((8, d), lambda bi, i, f: (0, 0)),
                  pl.BlockSpec((None, 8, d), lambda bi, i, f: (bi, 0, 0)),
                  row, row,
                  pl.BlockSpec((None, d, tf), lambda bi, i, f: (layer, 0, f)),
                  pl.BlockSpec((None, tf, d), lambda bi, i, f: (layer, f, 0))],
        out_specs=pl.BlockSpec((None, tm, d), lambda bi, i, f: (bi, i, 0)),
        out_shape=jax.ShapeDtypeStruct((b, rows, d), F32),
        scratch_shapes=[pltpu.VMEM((tm, d), BF16), pltpu.VMEM((tm, d), F32)],
        compiler_params=_cparams(("parallel", "parallel", "arbitrary")),
        name="mlp",
    )(xs, mod_c, mod_l, gain, final_gain, w1, w2)


def kernel(x, c, ctx, c_ctx, mod_w, mod_b, norm1_g, norm2_g, w_in, b_gate, gdn_conv_w, gdn_a_log, gdn_dt_bias,
           gdn_norm_g, lru_conv_w, lru_conv_b, lru_w_r, lru_b_r, lru_w_i, lru_b_i, lru_lambda, mla_q_norm_g,
           mla_w_uq, mla_kv_norm_g, mla_w_ukv, na_rpb, w_branch, w_out, mlp_w1, mlp_w2, final_norm_g):
    bsz, n_tok, d = x.shape
    n_ctx = ctx.shape[1]
    depth = w_in.shape[0]
    assert n_ctx % SEQ_TILE == 0 and n_tok % SEQ_TILE == 0 and n_tok % GRID_W == 0
    na_meta, na_rel_rows, na_cols = _na_geometry(n_tok // GRID_W)
    na_meta = jnp.asarray(na_meta)
    cos, sin = _rope_tables(n_tok, n_ctx)

    n_rows = -(-(bsz + 1) // 8) * 8
    cc = jnp.zeros((n_rows, d), F32).at[:bsz].set(c).at[bsz].set(c_ctx)
    final_gain = final_norm_g.reshape(1, d)

    w_in_all = _arrange_w_in(w_in)
    wb_all = w_branch.astype(BF16)
    wo_all = w_out.astype(BF16)
    w1_all = mlp_w1.astype(BF16)
    w2_all = mlp_w2.astype(BF16)

    xs = jnp.concatenate([x, ctx], axis=1)
    for l in range(depth):
        need_ctx = l < depth - 1
        mod = _modulation(cc, mod_w, mod_b[l], l).reshape(n_rows, N_MOD, d)
        pad = jnp.zeros((8 - N_MOD, d), F32)
        mod_c = jnp.concatenate([mod[bsz], pad], axis=0)
        mod_l = jnp.concatenate([mod[:bsz], jnp.broadcast_to(pad, (bsz, 8 - N_MOD, d))], axis=1)

        wq, wk, wv, place = _arrange_mla(mla_w_uq[l], mla_w_ukv[l])
        gq = mla_q_norm_g[l].reshape(1, -1)
        gkv = mla_kv_norm_g[l].reshape(1, -1)
        g1n = norm1_g[l].reshape(1, d)
        g2n = norm2_g[l].reshape(1, d)

        p32, p16 = _inproj(xs, mod_c, mod_l, g1n, w_in_all, l, n_tok)

        gq_, gk_, gv_, gbeta, gg = _gdn_prep(p32, gdn_conv_w[l], gdn_a_log[l], gdn_dt_bias[l], n_tok)
        h_f, h_b, o_f, o_b = _seq_mixers(
            _lru_scan_specs(p32, lru_conv_w[l], lru_conv_b[l], lru_w_r[l], lru_b_r[l], lru_w_i[l], lru_b_i[l],
                            lru_lambda[l], n_tok),
            _gdn_scan_specs(gq_, gk_, gv_, gbeta, gg, n_tok), bsz, (n_tok + n_ctx) // SEQ_TILE, n_tok)

        mq, mk, mv = _mla_prep(p32, p16, cos, sin, gq, gkv, wq, wk, wv, place)
        uc = _mla_flash(mq, mk, mv, n_tok)
        if need_ctx:
            uc = _mla_flash(mq, mk, mv, n_tok, prev=uc)
        ud = _na_attention(p16, na_meta, _na_bias_table(na_rpb[l], na_rel_rows, na_cols), n_tok, need_ctx)

        xs = _merge(xs, mod_c, mod_l, (o_f, o_b), (h_f, h_b), uc, ud, p32, p16, gdn_norm_g[l], b_gate[l],
                    wb_all, wo_all, l, n_tok, need_ctx)
        xs = _mlp(xs, mod_c, mod_l, g2n, w1_all, w2_all, l, final_gain, n_tok, l == depth - 1)
    return xs
```

```python
import functools

import jax
import jax.numpy as jnp
import numpy as np
from jax import lax
from jax.experimental import pallas as pl
from jax.experimental.pallas import tpu as pltpu

F32 = jnp.float32
BF16 = jnp.bfloat16

GRID_W = 64
N_MOD = 6
RMS_EPS = 1e-6
GDN_HEADS = 4
GDN_DK = 64
GDN_DV = 64
GDN_CHUNK = 64
GDN_W = GDN_HEADS * GDN_DV
LRU_W = 256
LRU_BLOCKS = 4
LRU_C = 8.0
MLA_HEADS = 4
MLA_Q_RANK = 256
MLA_KV_RANK = 128
MLA_NOPE = 64
MLA_ROPE = 32
MLA_V = 64
MLA_SLOT = 128
ROPE_BASE = 10000.0
LOG2_E = 1.4426950408889634
NA_HEADS = 4
NA_DH = 64
NA_W = NA_HEADS * NA_DH
NA_WIN_ROWS = 8
NA_WIN_COLS = 16
N_BRANCH = 4
BRANCH_W = 256

SEQ_TILE = 256
HALO = 8

_REF_COLS = {}
_off = 0
for _name, _w in (('gdn_qkv', 3 * GDN_W), ('gdn_z', GDN_W), ('gdn_beta', 2 * GDN_HEADS), ('gdn_a', 2 * GDN_HEADS),
                  ('lru_x', LRU_W), ('lru_y', LRU_W), ('mla_q', MLA_Q_RANK), ('mla_kv', MLA_KV_RANK),
                  ('mla_kr', MLA_ROPE), ('na_qkv', 3 * NA_W)):
    _REF_COLS[_name] = (_off, _w)
    _off += _w
N_MIX_COLS = _off

P_GDN_QKV = 0
P_GDN_Z = 768
P_LRU_X = 1024
P_LRU_Y = 1280
P_MLA_KV = 1536
P_MLA_KR = 1664
P_GDN_BA = 1792
P32_COLS = 2048
P_MLA_Q = 0
P_NA_QKV = 256
P_GATES = 1024
P16_COLS = P_GATES + N_BRANCH * 1024
PROJ_TILE = 1024

V7X_VMEM_BYTES = 64 * 1024 * 1024
VMEM_LIMIT = V7X_VMEM_BYTES - 12 * 1024 * 1024


def _cparams(sem):
    return pltpu.CompilerParams(dimension_semantics=sem, vmem_limit_bytes=VMEM_LIMIT)


def _pick_tile(n, cap):
    best = 8
    for t in range(8, min(n, cap) + 1, 8):
        if n % t == 0:
            best = t
    return best


def _full(a):
    return pl.BlockSpec(a.shape, lambda *_: (0,) * a.ndim)


N_SPLIT = 2


def _split(x):
    hi = x.astype(BF16)
    lo = (x - hi.astype(F32)).astype(BF16)
    return hi, lo


def _sigmoid(x):
    return 0.5 * jnp.tanh(0.5 * x) + 0.5


def _arrange_w_in(w_in):
    pieces, pos = [], 0

    def put(dst, block):
        nonlocal pos
        if dst > pos:
            pieces.append(jnp.zeros(w_in.shape[:-1] + (dst - pos,), w_in.dtype))
        pieces.append(block)
        pos = dst + block.shape[-1]

    ref = lambda name: w_in[..., _REF_COLS[name][0]:_REF_COLS[name][0] + _REF_COLS[name][1]]
    for name, dst in (('gdn_qkv', P_GDN_QKV), ('gdn_z', P_GDN_Z), ('lru_x', P_LRU_X), ('lru_y', P_LRU_Y),
                      ('mla_kv', P_MLA_KV), ('mla_kr', P_MLA_KR), ('gdn_beta', P_GDN_BA),
                      ('gdn_a', P_GDN_BA + 2 * GDN_HEADS)):
        put(dst, ref(name))
    put(P32_COLS + P_MLA_Q, ref('mla_q'))
    put(P32_COLS + P_NA_QKV, ref('na_qkv'))
    put(P32_COLS + P_GATES, w_in[..., N_MIX_COLS:])
    assert pos == P32_COLS + P16_COLS
    return jnp.concatenate(pieces, axis=-1).astype(BF16)


def _rope_perm():
    q = MLA_ROPE // 4
    src = np.zeros(MLA_ROPE, np.int32)
    sign = np.zeros(MLA_ROPE, np.float32)
    for base in (0, 2 * q):
        for d in range(q):
            src[base + d] = base + d + q
            sign[base + d] = -1.0
            src[base + q + d] = base + d
            sign[base + q + d] = 1.0
    return src, sign


def _arrange_mla(w_uq, w_ukv):
    src, sign = _rope_perm()
    hq = MLA_NOPE + MLA_ROPE
    wq = jnp.zeros((MLA_Q_RANK, 2 * MLA_HEADS * MLA_SLOT), F32)
    wk = jnp.zeros((MLA_KV_RANK, MLA_HEADS * MLA_SLOT), F32)
    wv = jnp.zeros((MLA_KV_RANK, MLA_HEADS * MLA_V), F32)
    place = np.zeros((2 * MLA_SLOT, 2 * MLA_HEADS * MLA_SLOT), np.float32)
    rot_off = MLA_HEADS * MLA_SLOT
    for h in range(MLA_HEADS):
        nope = w_uq[:, h * hq:h * hq + MLA_NOPE]
        pe = w_uq[:, h * hq + MLA_NOPE:(h + 1) * hq]
        s = h * MLA_SLOT
        wq = wq.at[:, s:s + MLA_NOPE].set(nope)
        wq = wq.at[:, s + MLA_NOPE:s + MLA_NOPE + MLA_ROPE].set(pe)
        wq = wq.at[:, rot_off + s + MLA_NOPE:rot_off + s + MLA_NOPE + MLA_ROPE].set(pe[:, src] * sign)
        wk = wk.at[:, s:s + MLA_NOPE].set(w_ukv[:, h * 128:h * 128 + MLA_NOPE])
        wv = wv.at[:, h * MLA_V:(h + 1) * MLA_V].set(w_ukv[:, h * 128 + MLA_NOPE:(h + 1) * 128])
        for d in range(MLA_ROPE):
            for half in (0, MLA_SLOT):
                place[half + d, s + MLA_NOPE + d] = 1.0
                place[half + src[d], rot_off + s + MLA_NOPE + d] = sign[d]
    return wq.astype(BF16), wk.astype(BF16), wv.astype(BF16), jnp.asarray(place, BF16)


def _rope_tables(n_tok, n_ctx):
    cos = np.ones((n_tok + n_ctx, MLA_SLOT), np.float32)
    sin = np.zeros((n_tok + n_ctx, MLA_SLOT), np.float32)
    t = np.arange(n_tok)
    row = (t // GRID_W).astype(np.float32)
    col = (t % GRID_W).astype(np.float32)
    n_freq = MLA_ROPE // 4
    inv = (ROPE_BASE ** (-np.arange(n_freq, dtype=np.float32) / n_freq)).astype(np.float32)
    ar = row[:, None] * inv
    ac = col[:, None] * inv
    ang = np.concatenate([ar, ar, ac, ac], axis=-1).astype(np.float32)
    cos[:n_tok, MLA_NOPE:MLA_NOPE + MLA_ROPE] = np.cos(ang)
    sin[:n_tok, MLA_NOPE:MLA_NOPE + MLA_ROPE] = np.sin(ang)
    return jnp.asarray(cos), jnp.asarray(sin)


def _head_block_ones(n_heads, width):
    m = np.kron(np.eye(n_heads, dtype=np.float32), np.ones((width, width), np.float32))
    return jnp.asarray(m, BF16)


def _mod_kernel(c_ref, w_ref, b_ref, o_ref):
    c = c_ref[...]
    s = c * _sigmoid(c)
    o_ref[...] = jnp.dot(s, w_ref[...], preferred_element_type=F32) + b_ref[...]


def _modulation(cc, mod_w, mod_b, layer):
    r, d = cc.shape
    n = mod_w.shape[-1]
    tn = 1024
    return pl.pallas_call(
        _mod_kernel,
        grid=(n // tn,),
        in_specs=[pl.BlockSpec((r, d), lambda j: (0, 0)),
                  pl.BlockSpec((None, d, tn), lambda j: (layer, 0, j)),
                  pl.BlockSpec((1, tn), lambda j: (0, j))],
        out_specs=pl.BlockSpec((r, tn), lambda j: (0, j)),
        out_shape=jax.ShapeDtypeStruct((r, n), F32),
        compiler_params=_cparams(("arbitrary",)),
        name="modulation",
    )(cc, mod_w, mod_b.reshape(1, n))


def _row_select(tile_rows, tile_idx, n_lat, ctx_vec, lat_vec):
    row = tile_idx * tile_rows + lax.broadcasted_iota(jnp.int32, (tile_rows, 1), 0)
    return jnp.where(row < n_lat, lat_vec, ctx_vec)


def _norm_modulate(x_ref, h_ref, gain_ref, mc_ref, ml_ref, shift_row, tile_idx, n_lat):
    tm = x_ref.shape[0]
    x = x_ref[...]
    xn = x * lax.rsqrt(jnp.mean(x * x, axis=-1, keepdims=True) + RMS_EPS)
    sh_l, sh_c = ml_ref[shift_row:shift_row + 1, :], mc_ref[shift_row:shift_row + 1, :]
    amp_l = gain_ref[...] * (1.0 + ml_ref[shift_row + 1:shift_row + 2, :])
    amp_c = gain_ref[...] * (1.0 + mc_ref[shift_row + 1:shift_row + 2, :])
    all_latent = (tile_idx + 1) * tm <= n_lat

    @pl.when(all_latent)
    def _():
        h_ref[...] = (xn * amp_l + sh_l).astype(h_ref.dtype)

    @pl.when(jnp.logical_not(all_latent))
    def _():
        amp = _row_select(tm, tile_idx, n_lat, amp_c, amp_l)
        shift = _row_select(tm, tile_idx, n_lat, sh_c, sh_l)
        h_ref[...] = (xn * amp + shift).astype(h_ref.dtype)


def _inproj_first_kernel(x_ref, mc_ref, ml_ref, g_ref, w_ref, o_ref, h_ref, *, n_lat):
    @pl.when(pl.program_id(2) == 0)
    def _():
        _norm_modulate(x_ref, h_ref, g_ref, mc_ref, ml_ref, 0, pl.program_id(1), n_lat)

    o_ref[...] = jnp.dot(h_ref[...], w_ref[...], preferred_element_type=F32).astype(o_ref.dtype)


def _inproj_second_kernel(h_ref, w_ref, o_ref):
    o_ref[...] = jnp.dot(h_ref[...], w_ref[...], preferred_element_type=F32).astype(o_ref.dtype)


def _inproj(xs, mod_c, mod_l, gain, w, layer, n_lat):
    b, s, d = xs.shape
    tm = _pick_tile(s, 1088)
    tn = PROJ_TILE
    n32 = P32_COLS // tn
    p16, h = pl.pallas_call(
        functools.partial(_inproj_first_kernel, n_lat=n_lat),
        grid=(b, s // tm, P16_COLS // tn),
        in_specs=[pl.BlockSpec((None, tm, d), lambda bi, i, j: (bi, i, 0)),
                  pl.BlockSpec((8, d), lambda bi, i, j: (0, 0)),
                  pl.BlockSpec((None, 8, d), lambda bi, i, j: (bi, 0, 0)),
                  pl.BlockSpec((1, d), lambda bi, i, j: (0, 0)),
                  pl.BlockSpec((None, d, tn), lambda bi, i, j: (layer, 0, j + n32))],
        out_specs=[pl.BlockSpec((None, tm, tn), lambda bi, i, j: (bi, i, j)),
                   pl.BlockSpec((None, tm, d), lambda bi, i, j: (bi, i, 0))],
        out_shape=[jax.ShapeDtypeStruct((b, s, P16_COLS), BF16), jax.ShapeDtypeStruct((b, s, d), BF16)],
        compiler_params=_cparams(("parallel", "parallel", "arbitrary")),
        name="inproj_bf16",
    )(xs, mod_c, mod_l, gain, w)
    tm2 = _pick_tile(s, 2 * tm)
    p32 = pl.pallas_call(
        _inproj_second_kernel,
        grid=(b, s // tm2, n32),
        in_specs=[pl.BlockSpec((None, tm2, d), lambda bi, i, j: (bi, i, 0)),
                  pl.BlockSpec((None, d, tn), lambda bi, i, j: (layer, 0, j))],
        out_specs=pl.BlockSpec((None, tm2, tn), lambda bi, i, j: (bi, i, j)),
        out_shape=jax.ShapeDtypeStruct((b, s, P32_COLS), F32),
        compiler_params=_cparams(("parallel", "parallel", "arbitrary")),
        name="inproj_f32",
    )(h, w)
    return p32, p16


def _mla_prep_kernel(ql_ref, kv_ref, cos_ref, sin_ref, gq_ref, gkv_ref, wq_ref, wk_ref, wv_ref, pl_ref,
                     q_ref, k_ref, v_ref, *, scale):
    nslot = MLA_HEADS * MLA_SLOT
    cos = jnp.concatenate([cos_ref[...]] * MLA_HEADS, axis=-1)
    sin = jnp.concatenate([sin_ref[...]] * MLA_HEADS, axis=-1)

    ql = ql_ref[...].astype(F32)
    qn = ql * lax.rsqrt(jnp.mean(ql * ql, axis=-1, keepdims=True) + RMS_EPS) * gq_ref[...]
    q2 = jnp.dot(qn.astype(BF16), wq_ref[...], preferred_element_type=F32)
    q = (q2[:, :nslot] * cos + q2[:, nslot:] * sin) * scale
    q_ref[...] = q.astype(BF16)

    kvkr = kv_ref[...]
    kvl = kvkr[:, :MLA_KV_RANK]
    kr = kvkr[:, MLA_KV_RANK:]
    kvn = (kvl * lax.rsqrt(jnp.mean(kvl * kvl, axis=-1, keepdims=True) + RMS_EPS) * gkv_ref[...]).astype(BF16)
    kn = jnp.dot(kvn, wk_ref[...], preferred_element_type=F32)
    v_ref[...] = jnp.dot(kvn, wv_ref[...], preferred_element_type=F32).astype(BF16)
    kr_hi = kr.astype(BF16)
    kr_lo = (kr - kr_hi.astype(F32)).astype(BF16)
    kr2 = jnp.dot(jnp.concatenate([kr_hi, kr_lo], axis=-1), pl_ref[...], preferred_element_type=F32)
    k_ref[...] = (kn + kr2[:, :nslot] * cos + kr2[:, nslot:] * sin).astype(BF16)


def _mla_prep(p32, p16, cos, sin, gq, gkv, wq, wk, wv, place):
    b, s, _ = p32.shape
    tm = _pick_tile(s, 1088)
    nslot = MLA_HEADS * MLA_SLOT
    scale = (MLA_NOPE + MLA_ROPE) ** -0.5 * LOG2_E
    return pl.pallas_call(
        functools.partial(_mla_prep_kernel, scale=scale),
        grid=(b, s // tm),
        in_specs=[pl.BlockSpec((None, tm, MLA_Q_RANK), lambda bi, i: (bi, i, P_MLA_Q // MLA_Q_RANK)),
                  pl.BlockSpec((None, tm, 2 * MLA_SLOT), lambda bi, i: (bi, i, P_MLA_KV // (2 * MLA_SLOT))),
                  pl.BlockSpec((tm, MLA_SLOT), lambda bi, i: (i, 0)),
                  pl.BlockSpec((tm, MLA_SLOT), lambda bi, i: (i, 0)),
                  _full(gq), _full(gkv), _full(wq), _full(wk), _full(wv), _full(place)],
        out_specs=[pl.BlockSpec((None, tm, nslot), lambda bi, i: (bi, i, 0)),
                   pl.BlockSpec((None, tm, nslot), lambda bi, i: (bi, i, 0)),
                   pl.BlockSpec((None, tm, MLA_HEADS * MLA_V), lambda bi, i: (bi, i, 0))],
        out_shape=[jax.ShapeDtypeStruct((b, s, nslot), BF16),
                   jax.ShapeDtypeStruct((b, s, nslot), BF16),
                   jax.ShapeDtypeStruct((b, s, MLA_HEADS * MLA_V), BF16)],
        compiler_params=_cparams(("parallel", "parallel")),
        name="mla_prep",
    )(p16, p32, cos, sin, gq, gkv, wq, wk, wv, place)


def _flash_softmax(h, s, m_ref, l_ref, first):
    row_max = jnp.max(s, axis=-1, keepdims=True)
    m_new = jnp.broadcast_to(row_max, m_ref.shape[1:]) if first else jnp.maximum(m_ref[h], row_max)
    p = jnp.exp2(s - jnp.concatenate([m_new] * (s.shape[1] // 128), axis=-1))
    row_sum = jnp.sum(p, axis=-1, keepdims=True)
    if first:
        alpha = None
        l_ref[h] = jnp.broadcast_to(row_sum, l_ref.shape[1:])
    else:
        alpha = jnp.exp2(m_ref[h] - m_new)
        l_ref[h] = alpha * l_ref[h] + row_sum
    m_ref[h] = m_new
    return alpha, p.astype(BF16)


def _mla_flash_kernel(*refs, tk, aliased):
    if aliased:
        q_ref, k_ref, v_ref, _, o_ref, m_ref, l_ref, acc_ref = refs
    else:
        q_ref, k_ref, v_ref, o_ref, m_ref, l_ref, acc_ref = refs
    tq = q_ref.shape[0]
    nk = k_ref.shape[0]
    n_loop = nk // tk

    def chunk(rows, first=False):
        heads = range(MLA_HEADS)
        hs = [slice(h * MLA_SLOT, (h + 1) * MLA_SLOT) for h in heads]
        vs = [slice((h // 2) * 128, (h // 2) * 128 + 128) for h in heads]
        s = [lax.dot_general(q_ref[:, hs[h]], k_ref[rows, hs[h]], (((1,), (1,)), ((), ())),
                             preferred_element_type=F32) for h in heads]
        ap = [_flash_softmax(h, s[h], m_ref, l_ref, first) for h in heads]
        for h in heads:
            alpha, p = ap[h]
            pv = jnp.dot(p, v_ref[rows, vs[h]], preferred_element_type=F32)
            acc_ref[h] = pv if first else alpha * acc_ref[h] + pv

    has_tail = nk > n_loop * tk
    if has_tail:
        chunk(slice(n_loop * tk, nk), first=True)
    else:
        chunk(slice(0, tk), first=True)
    lo = 0 if has_tail else 1
    if n_loop > lo:
        def body(j, carry):
            chunk(pl.ds(pl.multiple_of(j * tk, tk), tk))
            return carry
        lax.fori_loop(lo, n_loop, body, 0)

    lane = lax.broadcasted_iota(jnp.int32, (tq, 128), 1)
    outs = []
    for pair in range(MLA_HEADS // 2):
        o0 = acc_ref[2 * pair] / l_ref[2 * pair]
        o1 = acc_ref[2 * pair + 1] / l_ref[2 * pair + 1]
        outs.append(jnp.where(lane < MLA_V, o0, o1))
    o_ref[...] = jnp.concatenate(outs, axis=-1).astype(o_ref.dtype)


def _mla_flash(q, k, v, n_lat, prev=None):
    b, s, nslot = q.shape
    n_ctx = s - n_lat
    nv = MLA_HEADS * MLA_V
    tk = 512
    if prev is None:
        tq = _pick_tile(n_lat, 1024)
        q_off, n_q, kv_rows, kv_blk = 0, n_lat // tq, s, 0
    else:
        assert n_lat % n_ctx == 0
        tq = _pick_tile(n_ctx, 256)
        q_off, n_q, kv_rows, kv_blk = n_lat // tq, n_ctx // tq, n_ctx, n_lat // n_ctx
    in_specs = [pl.BlockSpec((None, tq, nslot), lambda bi, i: (bi, i + q_off, 0)),
                pl.BlockSpec((None, kv_rows, nslot), lambda bi, i: (bi, kv_blk, 0)),
                pl.BlockSpec((None, kv_rows, nv), lambda bi, i: (bi, kv_blk, 0))]
    args = [q, k, v]
    aliases = {}
    if prev is not None:
        in_specs.append(pl.BlockSpec(memory_space=pl.ANY))
        args.append(prev)
        aliases = {3: 0}
    return pl.pallas_call(
        functools.partial(_mla_flash_kernel, tk=tk, aliased=prev is not None),
        grid=(b, n_q),
        in_specs=in_specs,
        out_specs=pl.BlockSpec((None, tq, nv), lambda bi, i: (bi, i + q_off, 0)),
        out_shape=jax.ShapeDtypeStruct((b, s, nv), BF16),
        scratch_shapes=[pltpu.VMEM((MLA_HEADS, tq, 128), F32),
                        pltpu.VMEM((MLA_HEADS, tq, 128), F32),
                        pltpu.VMEM((MLA_HEADS, tq, 128), F32)],
        input_output_aliases=aliases,
        compiler_params=_cparams(("parallel", "arbitrary")),
        name="mla_flash",
    )(*args)


NA_QROWS = 4
NA_SLAB = NA_QROWS + NA_WIN_ROWS


def _na_geometry(rows):
    assert rows % NA_QROWS == 0 and rows >= NA_SLAB
    nblk = rows // NA_QROWS
    qc = np.arange(GRID_W)
    cs = np.clip(qc - NA_WIN_COLS // 2, 0, GRID_W - NA_WIN_COLS)
    col_valid = (qc[None, :] >= cs[:, None]) & (qc[None, :] < cs[:, None] + NA_WIN_COLS)
    rel_c = np.clip(qc[None, :] - qc[:, None] + NA_WIN_COLS - 1, 0, 2 * NA_WIN_COLS - 2)
    onehot_c = (rel_c[None] == np.arange(2 * NA_WIN_COLS - 1)[:, None, None]) & col_valid[None]
    patterns, cls, starts = {}, [], []
    for i in range(nblk):
        r0 = i * NA_QROWS
        start = int(np.clip(r0 - NA_WIN_ROWS // 2, 0, rows - NA_SLAB))
        qr = r0 + np.arange(NA_QROWS)
        rs = np.clip(qr - NA_WIN_ROWS // 2, 0, rows - NA_WIN_ROWS)
        key = (start - r0,) + tuple((rs - r0).tolist())
        if key not in patterns:
            kr = start + np.arange(NA_SLAB)
            row_valid = (kr[None, :] >= rs[:, None]) & (kr[None, :] < rs[:, None] + NA_WIN_ROWS)
            rel_r = np.clip(kr[None, :] - qr[:, None] + NA_WIN_ROWS - 1, 0, 2 * NA_WIN_ROWS - 2)
            patterns[key] = (len(patterns), np.where(row_valid, rel_r, -1))
        cls.append(patterns[key][0])
        starts.append(start)
    ordered = sorted(patterns.values(), key=lambda z: z[0])
    rel_rows = np.stack([z[1] for z in ordered])
    meta = np.stack([np.asarray(cls, np.int32), np.asarray(starts, np.int32)])
    return meta, rel_rows, (onehot_c.astype(np.float32), col_valid)


def _na_bias_table(rpb, rel_rows, col_tables):
    onehot_c, col_valid = col_tables
    bc = jnp.einsum('hrc,cqk->hrqk', rpb.astype(F32), onehot_c, precision=lax.Precision.HIGHEST)
    bc = jnp.where(col_valid, bc, -jnp.inf)
    outside = jnp.full(bc.shape[:1] + bc.shape[2:], -jnp.inf, F32)
    pats = []
    for pat in rel_rows:
        qrows = [jnp.concatenate([bc[:, r] if r >= 0 else outside for r in row], axis=-1) for row in pat]
        pats.append(jnp.concatenate(qrows, axis=1))
    return jnp.stack(pats)


def _na_scores(q_pair, lane, h, k_parts):
    in_head = (lane < NA_DH) if h % 2 == 0 else (lane >= NA_DH)
    qm = jnp.where(in_head, q_pair * (NA_DH ** -0.5), 0.0).astype(BF16)
    return [lax.dot_general(qm, k, (((1,), (1,)), ((), ())), preferred_element_type=F32) for k in k_parts]


def _na_softmax(scores, bias):
    if bias is not None:
        scores = [scores[0] + bias] + scores[1:]
    m = scores[0].max(axis=-1, keepdims=True)
    for s in scores[1:]:
        m = jnp.maximum(m, s.max(axis=-1, keepdims=True))
    probs = [jnp.exp(s - m) for s in scores]
    den = probs[0].sum(axis=-1, keepdims=True)
    for p in probs[1:]:
        den = den + p.sum(axis=-1, keepdims=True)
    return [p.astype(BF16) for p in probs], den


def _na_kernel(meta_ref, q_ref, k_ref, v_ref, bias_ref, o_ref, *, n_lat, with_ctx):
    i = pl.program_id(1)
    nq = q_ref.shape[0]
    n_lat_tiles = n_lat // nq
    n_all = k_ref.shape[0]
    lane = lax.broadcasted_iota(jnp.int32, (nq, 128), 1)

    def run(windowed):
        key_rows = [pl.ds(n_lat, n_all - n_lat)]
        if windowed:
            start = pl.multiple_of(meta_ref[1, jnp.minimum(i, n_lat_tiles - 1)] * GRID_W, GRID_W)
            key_rows = [pl.ds(start, NA_SLAB * GRID_W)] + key_rows
        pair_lanes = [slice((h // 2) * 128, (h // 2 + 1) * 128) for h in range(NA_HEADS)]
        scores = [_na_scores(q_ref[:, pair_lanes[h]], lane, h, [k_ref[r, pair_lanes[h]] for r in key_rows])
                  for h in range(NA_HEADS)]
        soft = [_na_softmax(scores[h], bias_ref[h] if windowed else None) for h in range(NA_HEADS)]
        heads = []
        for h in range(NA_HEADS):
            probs, den = soft[h]
            out = None
            for p, r in zip(probs, key_rows):
                po = jnp.dot(p, v_ref[r, pair_lanes[h]], preferred_element_type=F32)
                out = po if out is None else out + po
            heads.append(out / den)
        outs = [jnp.where(lane < NA_DH, heads[2 * pair], heads[2 * pair + 1]) for pair in range(NA_HEADS // 2)]
        o_ref[...] = jnp.concatenate(outs, axis=-1).astype(o_ref.dtype)

    if with_ctx:
        pl.when(i < n_lat_tiles)(lambda: run(True))
        pl.when(i >= n_lat_tiles)(lambda: run(False))
    else:
        run(True)


def _na_attention(p, meta, table, n_lat, with_ctx_queries):
    b, s, _ = p.shape
    nq = NA_QROWS * GRID_W
    nk = NA_SLAB * GRID_W
    assert (s - n_lat) % nq == 0
    nlt = n_lat // nq
    qb = P_NA_QKV // NA_W
    grid_spec = pltpu.PrefetchScalarGridSpec(
        num_scalar_prefetch=1,
        grid=(b, s // nq if with_ctx_queries else nlt),
        in_specs=[pl.BlockSpec((None, nq, NA_W), lambda bi, i, m: (bi, i, qb)),
                  pl.BlockSpec((None, s, NA_W), lambda bi, i, m: (bi, 0, qb + 1)),
                  pl.BlockSpec((None, s, NA_W), lambda bi, i, m: (bi, 0, qb + 2)),
                  pl.BlockSpec((None, NA_HEADS, nq, nk),
                               lambda bi, i, m: (m[0, jnp.minimum(i, nlt - 1)], 0, 0, 0))],
        out_specs=pl.BlockSpec((None, nq, NA_W), lambda bi, i, m: (bi, i, 0)))
    return pl.pallas_call(
        functools.partial(_na_kernel, n_lat=n_lat, with_ctx=with_ctx_queries),
        grid_spec=grid_spec,
        out_shape=jax.ShapeDtypeStruct((b, s, NA_W), BF16),
        compiler_params=_cparams(("parallel", "arbitrary")),
        name="na_attention",
    )(meta, p, p, p, table)


def _tile_conv(x, prev, nxt, w_ref, tile_idx, n_tiles, n_lat_tiles):
    r = x.shape[0]
    width = w_ref.shape[0]
    left = width // 2
    has_prev = jnp.logical_and(tile_idx != 0, tile_idx != n_lat_tiles)
    has_next = jnp.logical_and(tile_idx != n_tiles - 1, tile_idx != n_lat_tiles - 1)
    prev = jnp.where(has_prev, prev, 0.0)
    nxt = jnp.where(has_next, nxt, 0.0)
    xe = jnp.concatenate([prev, x, nxt], axis=0)
    acc = None
    for j in range(width):
        o = HALO - left + j
        term = xe[o:o + r, :] * w_ref[j:j + 1, :]
        acc = term if acc is None else acc + term
    return acc


def _halo_specs(width, col_block, tile_of):
    per = SEQ_TILE // HALO

    def main(bi, s, *_):
        return (bi, tile_of(s), col_block)

    def prev(bi, s, *_):
        return (bi, jnp.maximum(tile_of(s) * per - 1, 0), col_block)

    def make_next(n_tiles):
        def nxt(bi, s, *_):
            return (bi, jnp.minimum((tile_of(s) + 1) * per, n_tiles * per - 1), col_block)
        return nxt

    return main, prev, make_next


GDN_PREP_GROUP = 4


def _gdn_prep_kernel(*refs, n_lat_tiles):
    for e in range(refs[0].shape[0]):
        _gdn_prep_one(*(r.at[e] for r in refs[:4]), *refs[4:9], *(r.at[e] for r in refs[9:]),
                      n_lat_tiles=n_lat_tiles)


def _gdn_prep_one(x_ref, xp_ref, xn_ref, ba_ref, cw_ref, ones_ref, exp_ref, alog_ref, dtb_ref,
                  q_ref, k_ref, v_ref, beta_ref, g_ref, *, n_lat_tiles):
    i = pl.program_id(1)
    y = _tile_conv(x_ref[...], xp_ref[...], xn_ref[...], cw_ref, i, pl.num_programs(1), n_lat_tiles)
    y = y * _sigmoid(y)
    q = y[:, :GDN_W]
    k = y[:, GDN_W:2 * GDN_W]
    v_ref[...] = y[:, 2 * GDN_W:].astype(v_ref.dtype)

    def head_norm(u):
        parts = jnp.concatenate(_split(u * u), axis=-1)
        ss = jnp.dot(parts, ones_ref[...], preferred_element_type=F32)
        return u * lax.rsqrt(ss + RMS_EPS)

    q_ref[...] = (head_norm(q) * (GDN_DK ** -0.5)).astype(q_ref.dtype)
    k_ref[...] = head_norm(k).astype(k_ref.dtype)

    ba = ba_ref[...]
    a = ba + dtb_ref[...]
    softplus = jnp.maximum(a, 0.0) + jnp.log1p(jnp.exp(-jnp.abs(a)))
    lane = lax.broadcasted_iota(jnp.int32, ba.shape, 1)
    compact = jnp.where(lane < 2 * GDN_HEADS, _sigmoid(ba), -jnp.exp(alog_ref[...]) * softplus)
    wide = jnp.dot(jnp.concatenate(_split(compact), axis=-1), exp_ref[...], preferred_element_type=F32)
    half = 2 * GDN_W
    beta_ref[...] = wide[:, :half].astype(beta_ref.dtype)
    g_ref[...] = wide[:, half:]


def _gdn_prep(p, conv_w, a_log, dt_bias, n_lat):
    b, s, _ = p.shape
    n_tiles = s // SEQ_TILE
    main, prev, make_next = _halo_specs(3 * GDN_W, 0, lambda t: t)
    ones_n = jnp.concatenate([_head_block_ones(GDN_HEADS, GDN_DK)] * N_SPLIT, axis=0)
    expand = np.zeros((128, 4 * GDN_W), np.float32)
    for kind in range(2):
        for d in range(2):
            for h in range(GDN_HEADS):
                c0 = kind * 2 * GDN_W + d * GDN_W + h * GDN_DV
                expand[kind * 2 * GDN_HEADS + d * GDN_HEADS + h, c0:c0 + GDN_DV] = 1.0
    expand_n = jnp.asarray(np.concatenate([expand] * N_SPLIT, axis=0), BF16)
    lanes = jnp.zeros((1, 128), F32)
    alog_e = lanes.at[0, 2 * GDN_HEADS:4 * GDN_HEADS].set(a_log.astype(F32).reshape(-1))
    dtb_e = lanes.at[0, 2 * GDN_HEADS:4 * GDN_HEADS].set(dt_bias.astype(F32).reshape(-1))
    grp = GDN_PREP_GROUP if b % GDN_PREP_GROUP == 0 else 1
    tok = lambda w: pl.BlockSpec((grp, SEQ_TILE, w), lambda bi, i: (bi, i, 0))
    return pl.pallas_call(
        functools.partial(_gdn_prep_kernel, n_lat_tiles=n_lat // SEQ_TILE),
        grid=(b // grp, n_tiles),
        in_specs=[pl.BlockSpec((grp, SEQ_TILE, 3 * GDN_W), main),
                  pl.BlockSpec((grp, HALO, 3 * GDN_W), prev),
                  pl.BlockSpec((grp, HALO, 3 * GDN_W), make_next(n_tiles)),
                  pl.BlockSpec((grp, SEQ_TILE, 128), lambda bi, i: (bi, i, P_GDN_BA // 128)),
                  _full(conv_w), _full(ones_n), _full(expand_n), _full(alog_e), _full(dtb_e)],
        out_specs=[tok(GDN_W), tok(GDN_W), tok(GDN_W), tok(2 * GDN_W), tok(2 * GDN_W)],
        out_shape=[jax.ShapeDtypeStruct((b, s, GDN_W), BF16)] * 3 + [jax.ShapeDtypeStruct((b, s, 2 * GDN_W), BF16),
                                                                    jax.ShapeDtypeStruct((b, s, 2 * GDN_W), F32)],
        compiler_params=_cparams(("parallel", "parallel")),
        name="gdn_prep",
    )(p, p, p, p, conv_w, ones_n, expand_n, alog_e, dtb_e)


GDN_PAIR = 2 * GDN_DK


def _gdn_masks():
    c, w = GDN_CHUNK, GDN_W
    r2, c2 = np.arange(GDN_PAIR)[:, None], np.arange(GDN_PAIR)[None, :]
    bd = ((r2 // c) == (c2 // c)).astype(np.float32)
    i = np.arange(c)[:, None]
    j = (np.arange(w) % c)[None, :]
    level = np.zeros((c, w), np.int32)
    for bit in range(6):
        level += ((i ^ j) >= (1 << bit)).astype(np.int32)
    lvl = np.stack([(level == m).astype(np.float32) for m in range(7)])
    dirm = np.stack([np.stack([(j <= i), (j < i)]), np.stack([(j >= i), (j > i)])]).astype(np.float32)
    tj = (np.arange(N_SPLIT * c) % c)[None, :]
    tri = np.stack([(tj <= i), (tj >= i)]).astype(np.float32)
    return jnp.asarray(bd, BF16), jnp.asarray(lvl), jnp.asarray(dirm), jnp.asarray(tri, BF16)


def _heads_mm(x, y, bd, transpose_rhs=False):
    xb = x.astype(BF16)
    yb = y.astype(BF16)
    outs = []
    for pair in range(GDN_W // GDN_PAIR):
        ls = slice(pair * GDN_PAIR, (pair + 1) * GDN_PAIR)
        w = jnp.concatenate([yb[:, ls], yb[:, ls]], axis=0) * bd
        dims = (((1,), (1,)), ((), ())) if transpose_rhs else (((1,), (0,)), ((), ()))
        outs.append(lax.dot_general(xb[:, ls], w, dims, preferred_element_type=F32))
    return jnp.concatenate(outs, axis=1)


def _gdn_intra(probs, bd, lvl_ref, dirm_ref, tri_ref):
    c = GDN_CHUNK
    n = len(probs)
    eye = lvl_ref[0]
    gc, g_last, decay, gram = [], [], [], []
    for q, k, v, beta, g, rev in probs:
        d = 1 if rev else 0
        gcp = jnp.dot(tri_ref[d], jnp.concatenate(_split(g), axis=0), preferred_element_type=F32)
        gc.append(gcp)
        g_last.append(gcp[0:1, :] if rev else gcp[c - 1:c, :])
        gc_row = jnp.sum(gcp * eye, axis=0, keepdims=True)
        decay.append(dirm_ref[d, 0] * jnp.exp(jnp.minimum(gcp - gc_row, 0.0)))
        gram.append(_heads_mm(jnp.concatenate([k, q], axis=0), k, bd, transpose_rhs=True))
    lower = [dirm_ref[1 if p[5] else 0, 1] * p[3] * gram[x][:c] * decay[x] for x, p in enumerate(probs)]
    a_intra = [gram[x][c:] * decay[x] for x in range(n)]
    t = [eye - lower[x] * lvl_ref[1] for x in range(n)]
    for lev in range(2, 7):
        y = [_heads_mm(t[x], lower[x] * lvl_ref[lev], bd) for x in range(n)]
        z = [_heads_mm(y[x], t[x], bd) for x in range(n)]
        t = [t[x] - z[x] for x in range(n)]
    e_gc = [jnp.exp(gc[x]) for x in range(n)]
    u = [_heads_mm(t[x], p[2] * p[3], bd) for x, p in enumerate(probs)]
    w = [_heads_mm(t[x], p[1] * p[3] * e_gc[x], bd) for x, p in enumerate(probs)]
    wq = [jnp.concatenate([w[x], p[0] * e_gc[x]], axis=0).astype(BF16) for x, p in enumerate(probs)]
    k_dec = [(p[1] * jnp.exp(g_last[x] - gc[x])).astype(BF16) for x, p in enumerate(probs)]
    g_tot = [jnp.exp(g_last[x]) for x in range(n)]
    return u, wq, k_dec, a_intra, g_tot


def _gdn_state_steps(items, bd):
    c = GDN_CHUNK
    pairs = [slice(p * GDN_PAIR, (p + 1) * GDN_PAIR) for p in range(GDN_W // GDN_PAIR)]
    bdf = bd.astype(F32)
    ws_qs = [jnp.concatenate([jnp.dot(wq[:, ls], s_ref[p].astype(BF16), preferred_element_type=F32)
                              for p, ls in enumerate(pairs)], axis=1)
             for _, wq, _, _, _, s_ref in items]
    v_new = [it[0] - ws[:c] for it, ws in zip(items, ws_qs)]
    outs = [ws[c:] + _heads_mm(it[3], vn, bd) for it, ws, vn in zip(items, ws_qs, v_new)]
    for (_, _, k_dec, _, g_tot, s_ref), vn in zip(items, v_new):
        vb = vn.astype(BF16)
        for p, ls in enumerate(pairs):
            kv = lax.dot_general(k_dec[:, ls], vb[:, ls], (((0,), (0,)), ((), ())), preferred_element_type=F32)
            s_ref[p] = s_ref[p] * g_tot[:, ls] + kv * bdf
    return outs


def _gdn_scan_body(qf, kf, vf, bf, gf, qb, kb, vb, bb, gb, bd_ref, lvl_ref, dirm_ref, tri_ref,
                   of_ref, ob_ref, sf_ref, sb_ref):
    bd = bd_ref[...]
    n = SEQ_TILE // GDN_CHUNK
    probs, rows = [], []
    for c in range(n):
        rf = slice(c * GDN_CHUNK, (c + 1) * GDN_CHUNK)
        rb = slice((n - 1 - c) * GDN_CHUNK, (n - c) * GDN_CHUNK)
        probs.append(tuple(r[rf, :].astype(F32) for r in (qf, kf, vf, bf, gf)) + (False,))
        probs.append(tuple(r[rb, :].astype(F32) for r in (qb, kb, vb, bb, gb)) + (True,))
        rows += [rf, rb]
    u, wq, k_dec, a_intra, g_tot = _gdn_intra(probs, bd, lvl_ref, dirm_ref, tri_ref)
    for c in range(n):
        xs = (2 * c, 2 * c + 1)
        outs = _gdn_state_steps([(u[x], wq[x], k_dec[x], a_intra[x], g_tot[x], sb_ref if probs[x][5] else sf_ref)
                                 for x in xs], bd)
        for x, o in zip(xs, outs):
            (ob_ref if probs[x][5] else of_ref)[rows[x], :] = o.astype(of_ref.dtype)


def _seq_tile_maps(n_lat_tiles, n_tiles):
    fwd = lambda s: lax.rem(s + n_lat_tiles, n_tiles)
    bwd = lambda s: n_tiles - 1 - s
    return fwd, bwd


def _gdn_scan_specs(q, k, v, beta, g, n_lat):
    b, s, _ = q.shape
    n_tiles = s // SEQ_TILE
    fwd, bwd = _seq_tile_maps(n_lat // SEQ_TILE, n_tiles)
    spec = lambda tile_of, col: pl.BlockSpec((None, SEQ_TILE, GDN_W), lambda bi, t: (bi, tile_of(t), col))
    masks = _gdn_masks()
    in_specs = [spec(fwd, 0)] * 5 + [spec(bwd, 0)] * 3 + [spec(bwd, 1)] * 2 + [_full(m) for m in masks]
    assert len(in_specs) == N_GDN_IN
    return (in_specs, [q, k, v, beta, g, q, k, v, beta, g, *masks], [spec(fwd, 0), spec(bwd, 0)],
            [jax.ShapeDtypeStruct((b, s, GDN_W), BF16)] * 2,
            [pltpu.VMEM((GDN_W // GDN_PAIR, GDN_PAIR, GDN_PAIR), F32)] * 2)


def _lru_tile_prep(x_ref, xp_ref, xn_ref, cw_ref, cb_ref, wg_ref, bg_ref, nla_ref, a_ref, b_ref,
                   tile_idx, n_tiles, n_lat_tiles):
    xb = _tile_conv(x_ref[...], xp_ref[...], xn_ref[...], cw_ref, tile_idx, n_tiles, n_lat_tiles) + cb_ref[...]
    gates = _sigmoid(jnp.dot(xb.astype(BF16), wg_ref[...], preferred_element_type=F32) + bg_ref[...])
    log_a = nla_ref[...] * gates[:, :LRU_W]
    a_ref[...] = jnp.exp(log_a)
    th = jnp.tanh(log_a)
    b_ref[...] = jnp.sqrt(-2.0 * th / (1.0 - th)) * gates[:, LRU_W:] * xb


def _scan_group(a, b, h, row, reverse):
    for d in (1, 2, 4):
        if reverse:
            keep = row < 8 - d
            shift = 8 - d
        else:
            keep = row >= d
            shift = d
        a_s = jnp.where(keep, pltpu.roll(a, shift, 0), 1.0)
        b_s = jnp.where(keep, pltpu.roll(b, shift, 0), 0.0)
        b = a * b_s + b
        a = a * a_s
    return a * h + b


def _lru_scan_body(xf, xfp, xfn, xb, xbp, xbn, cw_ref, cb_ref, wgf, bgf, nlaf, wgb, bgb, nlab,
                   hf_ref, hb_ref, af_ref, bf_ref, ab_ref, bb_ref, cf_ref, cbk_ref, *, n_lat_tiles):
    s = pl.program_id(1)
    n_tiles = pl.num_programs(1)
    t_f, t_b = (m(s) for m in _seq_tile_maps(n_lat_tiles, n_tiles))
    _lru_tile_prep(xf, xfp, xfn, cw_ref, cb_ref, wgf, bgf, nlaf, af_ref, bf_ref, t_f, n_tiles, n_lat_tiles)
    _lru_tile_prep(xb, xbp, xbn, cw_ref, cb_ref, wgb, bgb, nlab, ab_ref, bb_ref, t_b, n_tiles, n_lat_tiles)

    n_groups = SEQ_TILE // 8
    row = lax.broadcasted_iota(jnp.int32, (8, LRU_W), 0)

    h_f, h_b = cf_ref[...], cbk_ref[...]
    pend_f, pend_b = None, None
    for gi in range(n_groups):
        rf = slice(gi * 8, gi * 8 + 8)
        rb = slice((n_groups - 1 - gi) * 8, (n_groups - gi) * 8)
        out_f = _scan_group(af_ref[rf, :], bf_ref[rf, :], h_f, row, False)
        out_b = _scan_group(ab_ref[rb, :], bb_ref[rb, :], h_b, row, True)
        if gi % 2 == 0:
            pend_f, pend_b = out_f, out_b
        else:
            hf_ref[(gi - 1) * 8:(gi + 1) * 8, :] = jnp.concatenate([pend_f, out_f], axis=0).astype(hf_ref.dtype)
            hb_ref[(n_groups - 1 - gi) * 8:(n_groups + 1 - gi) * 8, :] = (
                jnp.concatenate([out_b, pend_b], axis=0).astype(hb_ref.dtype))
        h_f = jnp.broadcast_to(out_f[7:8, :], (8, LRU_W))
        h_b = jnp.broadcast_to(out_b[0:1, :], (8, LRU_W))
    cf_ref[...] = h_f
    cbk_ref[...] = h_b


N_LRU_IN, N_GDN_IN, N_LRU_SCRATCH = 14, 14, 6


def _seq_mixers_kernel(*refs, n_lat_tiles):
    lru_in = refs[:N_LRU_IN]
    gdn_in = refs[N_LRU_IN:N_LRU_IN + N_GDN_IN]
    hf_ref, hb_ref, of_ref, ob_ref = refs[N_LRU_IN + N_GDN_IN:N_LRU_IN + N_GDN_IN + 4]
    scratch = refs[N_LRU_IN + N_GDN_IN + 4:]
    lru_scratch, gdn_scratch = scratch[:N_LRU_SCRATCH], scratch[N_LRU_SCRATCH:]

    @pl.when(pl.program_id(1) == 0)
    def _():
        for ref in lru_scratch[4:] + gdn_scratch:
            ref[...] = jnp.zeros(ref.shape, ref.dtype)

    _lru_scan_body(*lru_in, hf_ref, hb_ref, *lru_scratch, n_lat_tiles=n_lat_tiles)
    _gdn_scan_body(*gdn_in, of_ref, ob_ref, *gdn_scratch)


def _lru_scan_specs(p, conv_w, conv_b, w_r, b_r, w_i, b_i, lam, n_lat):
    b, s, _ = p.shape
    n_tiles = s // SEQ_TILE
    nlt = n_lat // SEQ_TILE
    fwd, bwd = _seq_tile_maps(nlt, n_tiles)
    col = P_LRU_X // LRU_W
    specs = []
    for tile_of in (fwd, bwd):
        main, prev, make_next = _halo_specs(LRU_W, col, tile_of)
        specs += [pl.BlockSpec((None, SEQ_TILE, LRU_W), main), pl.BlockSpec((None, HALO, LRU_W), prev),
                  pl.BlockSpec((None, HALO, LRU_W), make_next(n_tiles))]

    def blockdiag(w):
        return jax.scipy.linalg.block_diag(*[w[n] for n in range(LRU_BLOCKS)])

    dir_args = []
    for d in range(2):
        wg = jnp.concatenate([blockdiag(w_r[d]), blockdiag(w_i[d])], axis=1).astype(BF16)
        bg = jnp.concatenate([b_r[d], b_i[d]]).astype(F32).reshape(1, 2 * LRU_W)
        nla = (-LRU_C * jax.nn.softplus(-lam[d].astype(F32))).reshape(1, LRU_W)
        dir_args += [wg, bg, nla]
    cb2 = conv_b.reshape(1, LRU_W)
    out_spec = lambda tile_of: pl.BlockSpec((None, SEQ_TILE, LRU_W), lambda bi, t: (bi, tile_of(t), 0))
    in_specs = specs + [_full(conv_w), _full(cb2)] + [_full(a) for a in dir_args]
    assert len(in_specs) == N_LRU_IN
    return (in_specs, [p, p, p, p, p, p, conv_w, cb2, *dir_args], [out_spec(fwd), out_spec(bwd)],
            [jax.ShapeDtypeStruct((b, s, LRU_W), BF16)] * 2,
            [pltpu.VMEM((SEQ_TILE, LRU_W), F32)] * 4 + [pltpu.VMEM((8, LRU_W), F32)] * 2)


def _seq_mixers(lru_parts, gdn_parts, bsz, n_tiles, n_lat):
    in_specs, args, out_specs, out_shape, scratch = ([*a, *b] for a, b in zip(lru_parts, gdn_parts))
    return pl.pallas_call(
        functools.partial(_seq_mixers_kernel, n_lat_tiles=n_lat // SEQ_TILE),
        grid=(bsz, n_tiles),
        in_specs=in_specs,
        out_specs=out_specs,
        out_shape=out_shape,
        scratch_shapes=scratch,
        compiler_params=_cparams(("parallel", "arbitrary")),
        name="seq_mixers",
    )(*args)


def _merge_kernel(x_ref, mc_ref, ml_ref, of_ref, ob_ref, z_ref, hf_ref, hb_ref, y_ref, uc_ref, ud_ref,
                  t0_ref, t1_ref, t2_ref, t3_ref, gn_ref, ones_ref, bg_ref, wb_ref, wo_ref, o_ref, *, n_lat):
    project = lambda u, n: jnp.dot(u, wb_ref[n], preferred_element_type=F32)
    o = of_ref[...].astype(F32) + ob_ref[...].astype(F32)
    ms = jnp.dot(jnp.concatenate(_split(o * o), axis=-1), ones_ref[...], preferred_element_type=F32) * (1.0 / GDN_DV)
    proj = {2: project(uc_ref[...], 2), 3: project(ud_ref[...], 3)}
    z = z_ref[...]
    ua = (o * lax.rsqrt(ms + RMS_EPS) * gn_ref[...] * (z * _sigmoid(z))).astype(BF16)
    y = y_ref[...]
    gelu = 0.5 * y * (1.0 + jnp.tanh(0.7978845608028654 * (y + 0.044715 * (y * y * y))))
    ub = ((hf_ref[...].astype(F32) + hb_ref[...].astype(F32)) * gelu).astype(BF16)
    proj[0] = project(ua, 0)
    proj[1] = project(ub, 1)

    merged = None
    for n, t_ref in enumerate((t0_ref, t1_ref, t2_ref, t3_ref)):
        gate = _sigmoid(t_ref[...] + bg_ref[n:n + 1, :].astype(BF16))
        term = gate.astype(F32) * proj[n]
        merged = term if merged is None else merged + term
    out = jnp.dot(merged.astype(BF16), wo_ref[...], preferred_element_type=F32)
    g1 = _row_select(x_ref.shape[0], pl.program_id(1), n_lat, mc_ref[2:3, :], ml_ref[2:3, :])
    o_ref[...] = x_ref[...] + g1 * out


def _merge(xs, mod_c, mod_l, gdn_out, lru_out, uc, ud, p32, p16, gdn_norm_g, b_gate, w_branch, w_out, layer, n_lat,
           with_ctx):
    b, s, d = xs.shape
    rows = s if with_ctx else n_lat
    tm = _pick_tile(rows, 544)
    gate_blk = P_GATES // d
    tok = lambda w: pl.BlockSpec((None, tm, w), lambda bi, i: (bi, i, 0))
    ones_n = jnp.concatenate([_head_block_ones(GDN_HEADS, GDN_DV)] * N_SPLIT, axis=0)
    gn = jnp.tile(gdn_norm_g.astype(F32), GDN_HEADS).reshape(1, GDN_W)
    pcol = lambda c: pl.BlockSpec((None, tm, BRANCH_W), lambda bi, i: (bi, i, c))
    in_specs = [tok(d), pl.BlockSpec((8, d), lambda bi, i: (0, 0)), pl.BlockSpec((None, 8, d), lambda bi, i: (bi, 0, 0))]
    in_specs += [tok(BRANCH_W), tok(BRANCH_W), pcol(P_GDN_Z // GDN_W), tok(BRANCH_W), tok(BRANCH_W),
                 pcol(P_LRU_Y // LRU_W), tok(BRANCH_W), tok(BRANCH_W)]
    in_specs += [pl.BlockSpec((None, tm, d), functools.partial(lambda bi, i, n: (bi, i, gate_blk + n), n=n))
                 for n in range(N_BRANCH)]
    in_specs += [_full(gn), _full(ones_n), _full(b_gate),
                 pl.BlockSpec((None,) + w_branch.shape[1:], lambda bi, i: (layer, 0, 0, 0)),
                 pl.BlockSpec((None,) + w_out.shape[1:], lambda bi, i: (layer, 0, 0))]
    return pl.pallas_call(
        functools.partial(_merge_kernel, n_lat=n_lat),
        grid=(b, rows // tm),
        in_specs=in_specs,
        out_specs=tok(d),
        out_shape=jax.ShapeDtypeStruct((b, rows, d), F32),
        input_output_aliases={0: 0} if with_ctx else {},
        compiler_params=_cparams(("parallel", "parallel")),
        name="merge",
    )(xs, mod_c, mod_l, gdn_out[0], gdn_out[1], p32, lru_out[0], lru_out[1], p32, uc, ud, p16, p16, p16, p16,
      gn, ones_n, b_gate, w_branch, w_out)


def _mlp_kernel(x_ref, mc_ref, ml_ref, gn_ref, gf_ref, w1_ref, w2_ref, o_ref, h_ref, acc_ref, *, n_lat, final_norm):
    f = pl.program_id(2)
    tm = x_ref.shape[0]
    i = pl.program_id(1)

    @pl.when(f == 0)
    def _():
        _norm_modulate(x_ref, h_ref, gn_ref, mc_ref, ml_ref, 3, i, n_lat)
        acc_ref[...] = jnp.zeros(acc_ref.shape, F32)

    a = jnp.maximum(jnp.dot(h_ref[...], w1_ref[...], preferred_element_type=F32), 0.0)
    acc_ref[...] += jnp.dot((a * a).astype(BF16), w2_ref[...], preferred_element_type=F32)

    @pl.when(f == pl.num_programs(2) - 1)
    def _():
        g2 = _row_select(tm, i, n_lat, mc_ref[5:6, :], ml_ref[5:6, :])
        y = x_ref[...] + g2 * acc_ref[...]
        if final_norm:
            y = y * lax.rsqrt(jnp.mean(y * y, axis=-1, keepdims=True) + RMS_EPS) * gf_ref[...]
        o_ref[...] = y


def _mlp(xs, mod_c, mod_l, gain, w1, w2, layer, final_gain, n_lat, final_norm):
    b, rows, d = xs.shape
    dff = w1.shape[-1]
    tm = _pick_tile(rows, 1088)
    tf = 1024
    row = pl.BlockSpec((1, d), lambda bi, i, f: (0, 0))
    return pl.pallas_call(
        functools.partial(_mlp_kernel, n_lat=n_lat, final_norm=final_norm),
        grid=(b, rows // tm, dff // tf),
        in_specs=[pl.BlockSpec((None, tm, d), lambda bi, i, f: (bi, i, 0)),
                  pl.BlockSpec((8, d), lambda bi, i, f: (0, 0)),
                  pl.BlockSpec((None, 8, d), lambda bi, i, f: (bi, 0, 0)),
                  row, row,
                  pl.BlockSpec((None, d, tf), lambda bi, i, f: (layer, 0, f)),
                  pl.BlockSpec((None, tf, d), lambda bi, i, f: (layer, f, 0))],
        out_specs=pl.BlockSpec((None, tm, d), lambda bi, i, f: (bi, i, 0)),
        out_shape=jax.ShapeDtypeStruct((b, rows, d), F32),
        scratch_shapes=[pltpu.VMEM((tm, d), BF16), pltpu.VMEM((tm, d), F32)],
        compiler_params=_cparams(("parallel", "parallel", "arbitrary")),
        name="mlp",
    )(xs, mod_c, mod_l, gain, final_gain, w1, w2)


def kernel(x, c, ctx, c_ctx, mod_w, mod_b, norm1_g, norm2_g, w_in, b_gate, gdn_conv_w, gdn_a_log, gdn_dt_bias,
           gdn_norm_g, lru_conv_w, lru_conv_b, lru_w_r, lru_b_r, lru_w_i, lru_b_i, lru_lambda, mla_q_norm_g,
           mla_w_uq, mla_kv_norm_g, mla_w_ukv, na_rpb, w_branch, w_out, mlp_w1, mlp_w2, final_norm_g):
    bsz, n_tok, d = x.shape
    n_ctx = ctx.shape[1]
    depth = w_in.shape[0]
    assert n_ctx % SEQ_TILE == 0 and n_tok % SEQ_TILE == 0 and n_tok % GRID_W == 0
    na_meta, na_rel_rows, na_cols = _na_geometry(n_tok // GRID_W)
    na_meta = jnp.asarray(na_meta)
    cos, sin = _rope_tables(n_tok, n_ctx)

    n_rows = -(-(bsz + 1) // 8) * 8
    cc = jnp.zeros((n_rows, d), F32).at[:bsz].set(c).at[bsz].set(c_ctx)
    final_gain = final_norm_g.reshape(1, d)

    w_in_all = _arrange_w_in(w_in)
    wb_all = w_branch.astype(BF16)
    wo_all = w_out.astype(BF16)
    w1_all = mlp_w1.astype(BF16)
    w2_all = mlp_w2.astype(BF16)

    xs = jnp.concatenate([x, ctx], axis=1)
    for l in range(depth):
        need_ctx = l < depth - 1
        mod = _modulation(cc, mod_w, mod_b[l], l).reshape(n_rows, N_MOD, d)
        pad = jnp.zeros((8 - N_MOD, d), F32)
        mod_c = jnp.concatenate([mod[bsz], pad], axis=0)
        mod_l = jnp.concatenate([mod[:bsz], jnp.broadcast_to(pad, (bsz, 8 - N_MOD, d))], axis=1)

        wq, wk, wv, place = _arrange_mla(mla_w_uq[l], mla_w_ukv[l])
        gq = mla_q_norm_g[l].reshape(1, -1)
        gkv = mla_kv_norm_g[l].reshape(1, -1)
        g1n = norm1_g[l].reshape(1, d)
        g2n = norm2_g[l].reshape(1, d)

        p32, p16 = _inproj(xs, mod_c, mod_l, g1n, w_in_all, l, n_tok)

        gq_, gk_, gv_, gbeta, gg = _gdn_prep(p32, gdn_conv_w[l], gdn_a_log[l], gdn_dt_bias[l], n_tok)
        h_f, h_b, o_f, o_b = _seq_mixers(
            _lru_scan_specs(p32, lru_conv_w[l], lru_conv_b[l], lru_w_r[l], lru_b_r[l], lru_w_i[l], lru_b_i[l],
                            lru_lambda[l], n_tok),
            _gdn_scan_specs(gq_, gk_, gv_, gbeta, gg, n_tok), bsz, (n_tok + n_ctx) // SEQ_TILE, n_tok)

        mq, mk, mv = _mla_prep(p32, p16, cos, sin, gq, gkv, wq, wk, wv, place)
        uc = _mla_flash(mq, mk, mv, n_tok)
        if need_ctx:
            uc = _mla_flash(mq, mk, mv, n_tok, prev=uc)
        ud = _na_attention(p16, na_meta, _na_bias_table(na_rpb[l], na_rel_rows, na_cols), n_tok, need_ctx)

        xs = _merge(xs, mod_c, mod_l, (o_f, o_b), (h_f, h_b), uc, ud, p32, p16, gdn_norm_g[l], b_gate[l],
                    wb_all, wo_all, l, n_tok, need_ctx)
        xs = _mlp(xs, mod_c, mod_l, g2n, w1_all, w2_all, l, final_gain, n_tok, l == depth - 1)
    return xs
```

```python
import functools

import jax
import jax.numpy as jnp
import numpy as np
from jax import lax
from jax.experimental import pallas as pl
from jax.experimental.pallas import tpu as pltpu

F32 = jnp.float32
BF16 = jnp.bfloat16

GRID_W = 64
N_MOD = 6
RMS_EPS = 1e-6
GDN_HEADS = 4
GDN_DK = 64
GDN_DV = 64
GDN_CHUNK = 64
GDN_W = GDN_HEADS * GDN_DV
LRU_W = 256
LRU_BLOCKS = 4
LRU_C = 8.0
MLA_HEADS = 4
MLA_Q_RANK = 256
MLA_KV_RANK = 128
MLA_NOPE = 64
MLA_ROPE = 32
MLA_V = 64
MLA_SLOT = 128
ROPE_BASE = 10000.0
LOG2_E = 1.4426950408889634
NA_HEADS = 4
NA_DH = 64
NA_W = NA_HEADS * NA_DH
NA_WIN_ROWS = 8
NA_WIN_COLS = 16
N_BRANCH = 4
BRANCH_W = 256

SEQ_TILE = 256
HALO = 8

_REF_COLS = {}
_off = 0
for _name, _w in (('gdn_qkv', 3 * GDN_W), ('gdn_z', GDN_W), ('gdn_beta', 2 * GDN_HEADS), ('gdn_a', 2 * GDN_HEADS),
                  ('lru_x', LRU_W), ('lru_y', LRU_W), ('mla_q', MLA_Q_RANK), ('mla_kv', MLA_KV_RANK),
                  ('mla_kr', MLA_ROPE), ('na_qkv', 3 * NA_W)):
    _REF_COLS[_name] = (_off, _w)
    _off += _w
N_MIX_COLS = _off

P_GDN_QKV = 0
P_GDN_Z = 768
P_LRU_X = 1024
P_LRU_Y = 1280
P_MLA_KV = 1536
P_MLA_KR = 1664
P_GDN_BA = 1792
P32_COLS = 2048
P_MLA_Q = 0
P_NA_QKV = 256
P_GATES = 1024
P16_COLS = P_GATES + N_BRANCH * 1024
PROJ_TILE = 1024

V7X_VMEM_BYTES = 64 * 1024 * 1024
VMEM_LIMIT = V7X_VMEM_BYTES - 12 * 1024 * 1024


def _cparams(sem):
    return pltpu.CompilerParams(dimension_semantics=sem, vmem_limit_bytes=VMEM_LIMIT)


def _pick_tile(n, cap):
    best = 8
    for t in range(8, min(n, cap) + 1, 8):
        if n % t == 0:
            best = t
    return best


def _full(a):
    return pl.BlockSpec(a.shape, lambda *_: (0,) * a.ndim)


N_SPLIT = 2


def _split(x):
    hi = x.astype(BF16)
    lo = (x - hi.astype(F32)).astype(BF16)
    return hi, lo


def _sigmoid(x):
    return 0.5 * jnp.tanh(0.5 * x) + 0.5


def _arrange_w_in(w_in):
    pieces, pos = [], 0

    def put(dst, block):
        nonlocal pos
        if dst > pos:
            pieces.append(jnp.zeros(w_in.shape[:-1] + (dst - pos,), w_in.dtype))
        pieces.append(block)
        pos = dst + block.shape[-1]

    ref = lambda name: w_in[..., _REF_COLS[name][0]:_REF_COLS[name][0] + _REF_COLS[name][1]]
    for name, dst in (('gdn_qkv', P_GDN_QKV), ('gdn_z', P_GDN_Z), ('lru_x', P_LRU_X), ('lru_y', P_LRU_Y),
                      ('mla_kv', P_MLA_KV), ('mla_kr', P_MLA_KR), ('gdn_beta', P_GDN_BA),
                      ('gdn_a', P_GDN_BA + 2 * GDN_HEADS)):
        put(dst, ref(name))
    put(P32_COLS + P_MLA_Q, ref('mla_q'))
    put(P32_COLS + P_NA_QKV, ref('na_qkv'))
    put(P32_COLS + P_GATES, w_in[..., N_MIX_COLS:])
    assert pos == P32_COLS + P16_COLS
    return jnp.concatenate(pieces, axis=-1).astype(BF16)


def _rope_perm():
    q = MLA_ROPE // 4
    src = np.zeros(MLA_ROPE, np.int32)
    sign = np.zeros(MLA_ROPE, np.float32)
    for base in (0, 2 * q):
        for d in range(q):
            src[base + d] = base + d + q
            sign[base + d] = -1.0
            src[base + q + d] = base + d
            sign[base + q + d] = 1.0
    return src, sign


def _arrange_mla(w_uq, w_ukv):
    src, sign = _rope_perm()
    hq = MLA_NOPE + MLA_ROPE
    wq = jnp.zeros((MLA_Q_RANK, 2 * MLA_HEADS * MLA_SLOT), F32)
    wk = jnp.zeros((MLA_KV_RANK, MLA_HEADS * MLA_SLOT), F32)
    wv = jnp.zeros((MLA_KV_RANK, MLA_HEADS * MLA_V), F32)
    place = np.zeros((2 * MLA_SLOT, 2 * MLA_HEADS * MLA_SLOT), np.float32)
    rot_off = MLA_HEADS * MLA_SLOT
    for h in range(MLA_HEADS):
        nope = w_uq[:, h * hq:h * hq + MLA_NOPE]
        pe = w_uq[:, h * hq + MLA_NOPE:(h + 1) * hq]
        s = h * MLA_SLOT
        wq = wq.at[:, s:s + MLA_NOPE].set(nope)
        wq = wq.at[:, s + MLA_NOPE:s + MLA_NOPE + MLA_ROPE].set(pe)
        wq = wq.at[:, rot_off + s + MLA_NOPE:rot_off + s + MLA_NOPE + MLA_ROPE].set(pe[:, src] * sign)
        wk = wk.at[:, s:s + MLA_NOPE].set(w_ukv[:, h * 128:h * 128 + MLA_NOPE])
        wv = wv.at[:, h * MLA_V:(h + 1) * MLA_V].set(w_ukv[:, h * 128 + MLA_NOPE:(h + 1) * 128])
        for d in range(MLA_ROPE):
            for half in (0, MLA_SLOT):
                place[half + d, s + MLA_NOPE + d] = 1.0
                place[half + src[d], rot_off + s + MLA_NOPE + d] = sign[d]
    return wq.astype(BF16), wk.astype(BF16), wv.astype(BF16), jnp.asarray(place, BF16)


def _rope_tables(n_tok, n_ctx):
    cos = np.ones((n_tok + n_ctx, MLA_SLOT), np.float32)
    sin = np.zeros((n_tok + n_ctx, MLA_SLOT), np.float32)
    t = np.arange(n_tok)
    row = (t // GRID_W).astype(np.float32)
    col = (t % GRID_W).astype(np.float32)
    n_freq = MLA_ROPE // 4
    inv = (ROPE_BASE ** (-np.arange(n_freq, dtype=np.float32) / n_freq)).astype(np.float32)
    ar = row[:, None] * inv
    ac = col[:, None] * inv
    ang = np.concatenate([ar, ar, ac, ac], axis=-1).astype(np.float32)
    cos[:n_tok, MLA_NOPE:MLA_NOPE + MLA_ROPE] = np.cos(ang)
    sin[:n_tok, MLA_NOPE:MLA_NOPE + MLA_ROPE] = np.sin(ang)
    return jnp.asarray(cos), jnp.asarray(sin)


def _head_block_ones(n_heads, width):
    m = np.kron(np.eye(n_heads, dtype=np.float32), np.ones((width, width), np.float32))
    return jnp.asarray(m, BF16)


def _mod_kernel(c_ref, w_ref, b_ref, o_ref):
    c = c_ref[...]
    s = c * _sigmoid(c)
    o_ref[...] = jnp.dot(s, w_ref[...], preferred_element_type=F32) + b_ref[...]


def _modulation(cc, mod_w, mod_b, layer):
    r, d = cc.shape
    n = mod_w.shape[-1]
    tn = 1024
    return pl.pallas_call(
        _mod_kernel,
        grid=(n // tn,),
        in_specs=[pl.BlockSpec((r, d), lambda j: (0, 0)),
                  pl.BlockSpec((None, d, tn), lambda j: (layer, 0, j)),
                  pl.BlockSpec((1, tn), lambda j: (0, j))],
        out_specs=pl.BlockSpec((r, tn), lambda j: (0, j)),
        out_shape=jax.ShapeDtypeStruct((r, n), F32),
        compiler_params=_cparams(("arbitrary",)),
        name="modulation",
    )(cc, mod_w, mod_b.reshape(1, n))


def _row_select(tile_rows, tile_idx, n_lat, ctx_vec, lat_vec):
    row = tile_idx * tile_rows + lax.broadcasted_iota(jnp.int32, (tile_rows, 1), 0)
    return jnp.where(row < n_lat, lat_vec, ctx_vec)


def _norm_modulate(x_ref, h_ref, gain_ref, mc_ref, ml_ref, shift_row, tile_idx, n_lat):
    tm = x_ref.shape[0]
    x = x_ref[...]
    xn = x * lax.rsqrt(jnp.mean(x * x, axis=-1, keepdims=True) + RMS_EPS)
    sh_l, sh_c = ml_ref[shift_row:shift_row + 1, :], mc_ref[shift_row:shift_row + 1, :]
    amp_l = gain_ref[...] * (1.0 + ml_ref[shift_row + 1:shift_row + 2, :])
    amp_c = gain_ref[...] * (1.0 + mc_ref[shift_row + 1:shift_row + 2, :])
    all_latent = (tile_idx + 1) * tm <= n_lat

    @pl.when(all_latent)
    def _():
        h_ref[...] = (xn * amp_l + sh_l).astype(h_ref.dtype)

    @pl.when(jnp.logical_not(all_latent))
    def _():
        amp = _row_select(tm, tile_idx, n_lat, amp_c, amp_l)
        shift = _row_select(tm, tile_idx, n_lat, sh_c, sh_l)
        h_ref[...] = (xn * amp + shift).astype(h_ref.dtype)


def _inproj_first_kernel(x_ref, mc_ref, ml_ref, g_ref, w_ref, o_ref, h_ref, *, n_lat):
    @pl.when(pl.program_id(2) == 0)
    def _():
        _norm_modulate(x_ref, h_ref, g_ref, mc_ref, ml_ref, 0, pl.program_id(1), n_lat)

    o_ref[...] = jnp.dot(h_ref[...], w_ref[...], preferred_element_type=F32).astype(o_ref.dtype)


def _inproj_second_kernel(h_ref, w_ref, o_ref):
    o_ref[...] = jnp.dot(h_ref[...], w_ref[...], preferred_element_type=F32).astype(o_ref.dtype)


def _inproj(xs, mod_c, mod_l, gain, w, layer, n_lat):
    b, s, d = xs.shape
    tm = _pick_tile(s, 1088)
    tn = PROJ_TILE
    n32 = P32_COLS // tn
    p16, h = pl.pallas_call(
        functools.partial(_inproj_first_kernel, n_lat=n_lat),
        grid=(b, s // tm, P16_COLS // tn),
        in_specs=[pl.BlockSpec((None, tm, d), lambda bi, i, j: (bi, i, 0)),
                  pl.BlockSpec((8, d), lambda bi, i, j: (0, 0)),
                  pl.BlockSpec((None, 8, d), lambda bi, i, j: (bi, 0, 0)),
                  pl.BlockSpec((1, d), lambda bi, i, j: (0, 0)),
                  pl.BlockSpec((None, d, tn), lambda bi, i, j: (layer, 0, j + n32))],
        out_specs=[pl.BlockSpec((None, tm, tn), lambda bi, i, j: (bi, i, j)),
                   pl.BlockSpec((None, tm, d), lambda bi, i, j: (bi, i, 0))],
        out_shape=[jax.ShapeDtypeStruct((b, s, P16_COLS), BF16), jax.ShapeDtypeStruct((b, s, d), BF16)],
        compiler_params=_cparams(("parallel", "parallel", "arbitrary")),
        name="inproj_bf16",
    )(xs, mod_c, mod_l, gain, w)
    tm2 = _pick_tile(s, 2 * tm)
    p32 = pl.pallas_call(
        _inproj_second_kernel,
        grid=(b, s // tm2, n32),
        in_specs=[pl.BlockSpec((None, tm2, d), lambda bi, i, j: (bi, i, 0)),
                  pl.BlockSpec((None, d, tn), lambda bi, i, j: (layer, 0, j))],
        out_specs=pl.BlockSpec((None, tm2, tn), lambda bi, i, j: (bi, i, j)),
        out_shape=jax.ShapeDtypeStruct((b, s, P32_COLS), F32),
        compiler_params=_cparams(("parallel", "parallel", "arbitrary")),
        name="inproj_f32",
    )(h, w)
    return p32, p16


def _mla_prep_kernel(ql_ref, kv_ref, cos_ref, sin_ref, gq_ref, gkv_ref, wq_ref, wk_ref, wv_ref, pl_ref,
                     q_ref, k_ref, v_ref, *, scale):
    nslot = MLA_HEADS * MLA_SLOT
    cos = jnp.concatenate([cos_ref[...]] * MLA_HEADS, axis=-1)
    sin = jnp.concatenate([sin_ref[...]] * MLA_HEADS, axis=-1)

    ql = ql_ref[...].astype(F32)
    qn = ql * lax.rsqrt(jnp.mean(ql * ql, axis=-1, keepdims=True) + RMS_EPS) * gq_ref[...]
    q2 = jnp.dot(qn.astype(BF16), wq_ref[...], preferred_element_type=F32)
    q = (q2[:, :nslot] * cos + q2[:, nslot:] * sin) * scale
    q_ref[...] = q.astype(BF16)

    kvkr = kv_ref[...]
    kvl = kvkr[:, :MLA_KV_RANK]
    kr = kvkr[:, MLA_KV_RANK:]
    kvn = (kvl * lax.rsqrt(jnp.mean(kvl * kvl, axis=-1, keepdims=True) + RMS_EPS) * gkv_ref[...]).astype(BF16)
    kn = jnp.dot(kvn, wk_ref[...], preferred_element_type=F32)
    v_ref[...] = jnp.dot(kvn, wv_ref[...], preferred_element_type=F32).astype(BF16)
    kr_hi = kr.astype(BF16)
    kr_lo = (kr - kr_hi.astype(F32)).astype(BF16)
    kr2 = jnp.dot(jnp.concatenate([kr_hi, kr_lo], axis=-1), pl_ref[...], preferred_element_type=F32)
    k_ref[...] = (kn + kr2[:, :nslot] * cos + kr2[:, nslot:] * sin).astype(BF16)


def _mla_prep(p32, p16, cos, sin, gq, gkv, wq, wk, wv, place):
    b, s, _ = p32.shape
    tm = _pick_tile(s, 1088)
    nslot = MLA_HEADS * MLA_SLOT
    scale = (MLA_NOPE + MLA_ROPE) ** -0.5 * LOG2_E
    return pl.pallas_call(
        functools.partial(_mla_prep_kernel, scale=scale),
        grid=(b, s // tm),
        in_specs=[pl.BlockSpec((None, tm, MLA_Q_RANK), lambda bi, i: (bi, i, P_MLA_Q // MLA_Q_RANK)),
                  pl.BlockSpec((None, tm, 2 * MLA_SLOT), lambda bi, i: (bi, i, P_MLA_KV // (2 * MLA_SLOT))),
                  pl.BlockSpec((tm, MLA_SLOT), lambda bi, i: (i, 0)),
                  pl.BlockSpec((tm, MLA_SLOT), lambda bi, i: (i, 0)),
                  _full(gq), _full(gkv), _full(wq), _full(wk), _full(wv), _full(place)],
        out_specs=[pl.BlockSpec((None, tm, nslot), lambda bi, i: (bi, i, 0)),
                   pl.BlockSpec((None, tm, nslot), lambda bi, i: (bi, i, 0)),
                   pl.BlockSpec((None, tm, MLA_HEADS * MLA_V), lambda bi, i: (bi, i, 0))],
        out_shape=[jax.ShapeDtypeStruct((b, s, nslot), BF16),
                   jax.ShapeDtypeStruct((b, s, nslot), BF16),
                   jax.ShapeDtypeStruct((b, s, MLA_HEADS * MLA_V), BF16)],
        compiler_params=_cparams(("parallel", "parallel")),
        name="mla_prep",
    )(p16, p32, cos, sin, gq, gkv, wq, wk, wv, place)


def _flash_softmax(h, s, m_ref, l_ref, first):
    row_max = jnp.max(s, axis=-1, keepdims=True)
    m_new = jnp.broadcast_to(row_max, m_ref.shape[1:]) if first else jnp.maximum(m_ref[h], row_max)
    p = jnp.exp2(s - jnp.concatenate([m_new] * (s.shape[1] // 128), axis=-1))
    row_sum = jnp.sum(p, axis=-1, keepdims=True)
    if first:
        alpha = None
        l_ref[h] = jnp.broadcast_to(row_sum, l_ref.shape[1:])
    else:
        alpha = jnp.exp2(m_ref[h] - m_new)
        l_ref[h] = alpha * l_ref[h] + row_sum
    m_ref[h] = m_new
    return alpha, p.astype(BF16)


def _mla_flash_kernel(*refs, tk, aliased):
    if aliased:
        q_ref, k_ref, v_ref, _, o_ref, m_ref, l_ref, acc_ref = refs
    else:
        q_ref, k_ref, v_ref, o_ref, m_ref, l_ref, acc_ref = refs
    tq = q_ref.shape[0]
    nk = k_ref.shape[0]
    n_loop = nk // tk

    def chunk(rows, first=False):
        heads = range(MLA_HEADS)
        hs = [slice(h * MLA_SLOT, (h + 1) * MLA_SLOT) for h in heads]
        vs = [slice((h // 2) * 128, (h // 2) * 128 + 128) for h in heads]
        s = [lax.dot_general(q_ref[:, hs[h]], k_ref[rows, hs[h]], (((1,), (1,)), ((), ())),
                             preferred_element_type=F32) for h in heads]
        ap = [_flash_softmax(h, s[h], m_ref, l_ref, first) for h in heads]
        for h in heads:
            alpha, p = ap[h]
            pv = jnp.dot(p, v_ref[rows, vs[h]], preferred_element_type=F32)
            acc_ref[h] = pv if first else alpha * acc_ref[h] + pv

    has_tail = nk > n_loop * tk
    if has_tail:
        chunk(slice(n_loop * tk, nk), first=True)
    else:
        chunk(slice(0, tk), first=True)
    lo = 0 if has_tail else 1
    if n_loop > lo:
        def body(j, carry):
            chunk(pl.ds(pl.multiple_of(j * tk, tk), tk))
            return carry
        lax.fori_loop(lo, n_loop, body, 0)

    lane = lax.broadcasted_iota(jnp.int32, (tq, 128), 1)
    outs = []
    for pair in range(MLA_HEADS // 2):
        o0 = acc_ref[2 * pair] / l_ref[2 * pair]
        o1 = acc_ref[2 * pair + 1] / l_ref[2 * pair + 1]
        outs.append(jnp.where(lane < MLA_V, o0, o1))
    o_ref[...] = jnp.concatenate(outs, axis=-1).astype(o_ref.dtype)


def _mla_flash(q, k, v, n_lat, prev=None):
    b, s, nslot = q.shape
    n_ctx = s - n_lat
    nv = MLA_HEADS * MLA_V
    tk = 512
    if prev is None:
        tq = _pick_tile(n_lat, 1024)
        q_off, n_q, kv_rows, kv_blk = 0, n_lat // tq, s, 0
    else:
        assert n_lat % n_ctx == 0
        tq = _pick_tile(n_ctx, 256)
        q_off, n_q, kv_rows, kv_blk = n_lat // tq, n_ctx // tq, n_ctx, n_lat // n_ctx
    in_specs = [pl.BlockSpec((None, tq, nslot), lambda bi, i: (bi, i + q_off, 0)),
                pl.BlockSpec((None, kv_rows, nslot), lambda bi, i: (bi, kv_blk, 0)),
                pl.BlockSpec((None, kv_rows, nv), lambda bi, i: (bi, kv_blk, 0))]
    args = [q, k, v]
    aliases = {}
    if prev is not None:
        in_specs.append(pl.BlockSpec(memory_space=pl.ANY))
        args.append(prev)
        aliases = {3: 0}
    return pl.pallas_call(
        functools.partial(_mla_flash_kernel, tk=tk, aliased=prev is not None),
        grid=(b, n_q),
        in_specs=in_specs,
        out_specs=pl.BlockSpec((None, tq, nv), lambda bi, i: (bi, i + q_off, 0)),
        out_shape=jax.ShapeDtypeStruct((b, s, nv), BF16),
        scratch_shapes=[pltpu.VMEM((MLA_HEADS, tq, 128), F32),
                        pltpu.VMEM((MLA_HEADS, tq, 128), F32),
                        pltpu.VMEM((MLA_HEADS, tq, 128), F32)],
        input_output_aliases=aliases,
        compiler_params=_cparams(("parallel", "arbitrary")),
        name="mla_flash",
    )(*args)


NA_QROWS = 4
NA_SLAB = NA_QROWS + NA_WIN_ROWS


def _na_geometry(rows):
    assert rows % NA_QROWS == 0 and rows >= NA_SLAB
    nblk = rows // NA_QROWS
    qc = np.arange(GRID_W)
    cs = np.clip(qc - NA_WIN_COLS // 2, 0, GRID_W - NA_WIN_COLS)
    col_valid = (qc[None, :] >= cs[:, None]) & (qc[None, :] < cs[:, None] + NA_WIN_COLS)
    rel_c = np.clip(qc[None, :] - qc[:, None] + NA_WIN_COLS - 1, 0, 2 * NA_WIN_COLS - 2)
    onehot_c = (rel_c[None] == np.arange(2 * NA_WIN_COLS - 1)[:, None, None]) & col_valid[None]
    patterns, cls, starts = {}, [], []
    for i in range(nblk):
        r0 = i * NA_QROWS
        start = int(np.clip(r0 - NA_WIN_ROWS // 2, 0, rows - NA_SLAB))
        qr = r0 + np.arange(NA_QROWS)
        rs = np.clip(qr - NA_WIN_ROWS // 2, 0, rows - NA_WIN_ROWS)
        key = (start - r0,) + tuple((rs - r0).tolist())
        if key not in patterns:
            kr = start + np.arange(NA_SLAB)
            row_valid = (kr[None, :] >= rs[:, None]) & (kr[None, :] < rs[:, None] + NA_WIN_ROWS)
            rel_r = np.clip(kr[None, :] - qr[:, None] + NA_WIN_ROWS - 1, 0, 2 * NA_WIN_ROWS - 2)
            patterns[key] = (len(patterns), np.where(row_valid, rel_r, -1))
        cls.append(patterns[key][0])
        starts.append(start)
    ordered = sorted(patterns.values(), key=lambda z: z[0])
    rel_rows = np.stack([z[1] for z in ordered])
    meta = np.stack([np.asarray(cls, np.int32), np.asarray(starts, np.int32)])
    return meta, rel_rows, (onehot_c.astype(np.float32), col_valid)


def _na_bias_table(rpb, rel_rows, col_tables):
    onehot_c, col_valid = col_tables
    bc = jnp.einsum('hrc,cqk->hrqk', rpb.astype(F32), onehot_c, precision=lax.Precision.HIGHEST)
    bc = jnp.where(col_valid, bc, -jnp.inf)
    outside = jnp.full(bc.shape[:1] + bc.shape[2:], -jnp.inf, F32)
    pats = []
    for pat in rel_rows:
        qrows = [jnp.concatenate([bc[:, r] if r >= 0 else outside for r in row], axis=-1) for row in pat]
        pats.append(jnp.concatenate(qrows, axis=1))
    return jnp.stack(pats)


def _na_scores(q_pair, lane, h, k_parts):
    in_head = (lane < NA_DH) if h % 2 == 0 else (lane >= NA_DH)
    qm = jnp.where(in_head, q_pair * (NA_DH ** -0.5), 0.0).astype(BF16)
    return [lax.dot_general(qm, k, (((1,), (1,)), ((), ())), preferred_element_type=F32) for k in k_parts]


def _na_softmax(scores, bias):
    if bias is not None:
        scores = [scores[0] + bias] + scores[1:]
    m = scores[0].max(axis=-1, keepdims=True)
    for s in scores[1:]:
        m = jnp.maximum(m, s.max(axis=-1, keepdims=True))
    probs = [jnp.exp(s - m) for s in scores]
    den = probs[0].sum(axis=-1, keepdims=True)
    for p in probs[1:]:
        den = den + p.sum(axis=-1, keepdims=True)
    return [p.astype(BF16) for p in probs], den


def _na_kernel(meta_ref, q_ref, k_ref, v_ref, bias_ref, o_ref, *, n_lat, with_ctx):
    i = pl.program_id(1)
    nq = q_ref.shape[0]
    n_lat_tiles = n_lat // nq
    n_all = k_ref.shape[0]
    lane = lax.broadcasted_iota(jnp.int32, (nq, 128), 1)

    def run(windowed):
        key_rows = [pl.ds(n_lat, n_all - n_lat)]
        if windowed:
            start = pl.multiple_of(meta_ref[1, jnp.minimum(i, n_lat_tiles - 1)] * GRID_W, GRID_W)
            key_rows = [pl.ds(start, NA_SLAB * GRID_W)] + key_rows
        pair_lanes = [slice((h // 2) * 128, (h // 2 + 1) * 128) for h in range(NA_HEADS)]
        scores = [_na_scores(q_ref[:, pair_lanes[h]], lane, h, [k_ref[r, pair_lanes[h]] for r in key_rows])
                  for h in range(NA_HEADS)]
        soft = [_na_softmax(scores[h], bias_ref[h] if windowed else None) for h in range(NA_HEADS)]
        heads = []
        for h in range(NA_HEADS):
            probs, den = soft[h]
            out = None
            for p, r in zip(probs, key_rows):
                po = jnp.dot(p, v_ref[r, pair_lanes[h]], preferred_element_type=F32)
                out = po if out is None else out + po
            heads.append(out / den)
        outs = [jnp.where(lane < NA_DH, heads[2 * pair], heads[2 * pair + 1]) for pair in range(NA_HEADS // 2)]
        o_ref[...] = jnp.concatenate(outs, axis=-1).astype(o_ref.dtype)

    if with_ctx:
        pl.when(i < n_lat_tiles)(lambda: run(True))
        pl.when(i >= n_lat_tiles)(lambda: run(False))
    else:
        run(True)


def _na_attention(p, meta, table, n_lat, with_ctx_queries):
    b, s, _ = p.shape
    nq = NA_QROWS * GRID_W
    nk = NA_SLAB * GRID_W
    assert (s - n_lat) % nq == 0
    nlt = n_lat // nq
    qb = P_NA_QKV // NA_W
    grid_spec = pltpu.PrefetchScalarGridSpec(
        num_scalar_prefetch=1,
        grid=(b, s // nq if with_ctx_queries else nlt),
        in_specs=[pl.BlockSpec((None, nq, NA_W), lambda bi, i, m: (bi, i, qb)),
                  pl.BlockSpec((None, s, NA_W), lambda bi, i, m: (bi, 0, qb + 1)),
                  pl.BlockSpec((None, s, NA_W), lambda bi, i, m: (bi, 0, qb + 2)),
                  pl.BlockSpec((None, NA_HEADS, nq, nk),
                               lambda bi, i, m: (m[0, jnp.minimum(i, nlt - 1)], 0, 0, 0))],
        out_specs=pl.BlockSpec((None, nq, NA_W), lambda bi, i, m: (bi, i, 0)))
    return pl.pallas_call(
        functools.partial(_na_kernel, n_lat=n_lat, with_ctx=with_ctx_queries),
        grid_spec=grid_spec,
        out_shape=jax.ShapeDtypeStruct((b, s, NA_W), BF16),
        compiler_params=_cparams(("parallel", "arbitrary")),
        name="na_attention",
    )(meta, p, p, p, table)


def _tile_conv(x, prev, nxt, w_ref, tile_idx, n_tiles, n_lat_tiles):
    r = x.shape[0]
    width = w_ref.shape[0]
    left = width // 2
    has_prev = jnp.logical_and(tile_idx != 0, tile_idx != n_lat_tiles)
    has_next = jnp.logical_and(tile_idx != n_tiles - 1, tile_idx != n_lat_tiles - 1)
    prev = jnp.where(has_prev, prev, 0.0)
    nxt = jnp.where(has_next, nxt, 0.0)
    xe = jnp.concatenate([prev, x, nxt], axis=0)
    acc = None
    for j in range(width):
        o = HALO - left + j
        term = xe[o:o + r, :] * w_ref[j:j + 1, :]
        acc = term if acc is None else acc + term
    return acc


def _halo_specs(width, col_block, tile_of):
    per = SEQ_TILE // HALO

    def main(bi, s, *_):
        return (bi, tile_of(s), col_block)

    def prev(bi, s, *_):
        return (bi, jnp.maximum(tile_of(s) * per - 1, 0), col_block)

    def make_next(n_tiles):
        def nxt(bi, s, *_):
            return (bi, jnp.minimum((tile_of(s) + 1) * per, n_tiles * per - 1), col_block)
        return nxt

    return main, prev, make_next


GDN_PREP_GROUP = 8


def _gdn_prep_kernel(*refs, n_lat_tiles):
    for e in range(refs[0].shape[0]):
        _gdn_prep_one(*(r.at[e] for r in refs[:4]), *refs[4:9], *(r.at[e] for r in refs[9:]),
                      n_lat_tiles=n_lat_tiles)


def _gdn_prep_one(x_ref, xp_ref, xn_ref, ba_ref, cw_ref, ones_ref, exp_ref, alog_ref, dtb_ref,
                  q_ref, k_ref, v_ref, beta_ref, g_ref, *, n_lat_tiles):
    i = pl.program_id(1)
    y = _tile_conv(x_ref[...], xp_ref[...], xn_ref[...], cw_ref, i, pl.num_programs(1), n_lat_tiles)
    y = y * _sigmoid(y)
    q = y[:, :GDN_W]
    k = y[:, GDN_W:2 * GDN_W]
    v_ref[...] = y[:, 2 * GDN_W:].astype(v_ref.dtype)

    def head_norm(u):
        parts = jnp.concatenate(_split(u * u), axis=-1)
        ss = jnp.dot(parts, ones_ref[...], preferred_element_type=F32)
        return u * lax.rsqrt(ss + RMS_EPS)

    q_ref[...] = (head_norm(q) * (GDN_DK ** -0.5)).astype(q_ref.dtype)
    k_ref[...] = head_norm(k).astype(k_ref.dtype)

    ba = ba_ref[...]
    a = ba + dtb_ref[...]
    softplus = jnp.maximum(a, 0.0) + jnp.log1p(jnp.exp(-jnp.abs(a)))
    lane = lax.broadcasted_iota(jnp.int32, ba.shape, 1)
    compact = jnp.where(lane < 2 * GDN_HEADS, _sigmoid(ba), -jnp.exp(alog_ref[...]) * softplus)
    wide = jnp.dot(jnp.concatenate(_split(compact), axis=-1), exp_ref[...], preferred_element_type=F32)
    half = 2 * GDN_W
    beta_ref[...] = wide[:, :half].astype(beta_ref.dtype)
    g_ref[...] = wide[:, half:]


def _gdn_prep(p, conv_w, a_log, dt_bias, n_lat):
    b, s, _ = p.shape
    n_tiles = s // SEQ_TILE
    main, prev, make_next = _halo_specs(3 * GDN_W, 0, lambda t: t)
    ones_n = jnp.concatenate([_head_block_ones(GDN_HEADS, GDN_DK)] * N_SPLIT, axis=0)
    expand = np.zeros((128, 4 * GDN_W), np.float32)
    for kind in range(2):
        for d in range(2):
            for h in range(GDN_HEADS):
                c0 = kind * 2 * GDN_W + d * GDN_W + h * GDN_DV
                expand[kind * 2 * GDN_HEADS + d * GDN_HEADS + h, c0:c0 + GDN_DV] = 1.0
    expand_n = jnp.asarray(np.concatenate([expand] * N_SPLIT, axis=0), BF16)
    lanes = jnp.zeros((1, 128), F32)
    alog_e = lanes.at[0, 2 * GDN_HEADS:4 * GDN_HEADS].set(a_log.astype(F32).reshape(-1))
    dtb_e = lanes.at[0, 2 * GDN_HEADS:4 * GDN_HEADS].set(dt_bias.astype(F32).reshape(-1))
    grp = GDN_PREP_GROUP if b % GDN_PREP_GROUP == 0 else 1
    tok = lambda w: pl.BlockSpec((grp, SEQ_TILE, w), lambda bi, i: (bi, i, 0))
    return pl.pallas_call(
        functools.partial(_gdn_prep_kernel, n_lat_tiles=n_lat // SEQ_TILE),
        grid=(b // grp, n_tiles),
        in_specs=[pl.BlockSpec((grp, SEQ_TILE, 3 * GDN_W), main),
                  pl.BlockSpec((grp, HALO, 3 * GDN_W), prev),
                  pl.BlockSpec((grp, HALO, 3 * GDN_W), make_next(n_tiles)),
                  pl.BlockSpec((grp, SEQ_TILE, 128), lambda bi, i: (bi, i, P_GDN_BA // 128)),
                  _full(conv_w), _full(ones_n), _full(expand_n), _full(alog_e), _full(dtb_e)],
        out_specs=[tok(GDN_W), tok(GDN_W), tok(GDN_W), tok(2 * GDN_W), tok(2 * GDN_W)],
        out_shape=[jax.ShapeDtypeStruct((b, s, GDN_W), BF16)] * 3 + [jax.ShapeDtypeStruct((b, s, 2 * GDN_W), BF16),
                                                                    jax.ShapeDtypeStruct((b, s, 2 * GDN_W), F32)],
        compiler_params=_cparams(("parallel", "parallel")),
        name="gdn_prep",
    )(p, p, p, p, conv_w, ones_n, expand_n, alog_e, dtb_e)


GDN_PAIR = 2 * GDN_DK


def _gdn_masks():
    c, w = GDN_CHUNK, GDN_W
    r2, c2 = np.arange(GDN_PAIR)[:, None], np.arange(GDN_PAIR)[None, :]
    bd = ((r2 // c) == (c2 // c)).astype(np.float32)
    i = np.arange(c)[:, None]
    j = (np.arange(w) % c)[None, :]
    level = np.zeros((c, w), np.int32)
    for bit in range(6):
        level += ((i ^ j) >= (1 << bit)).astype(np.int32)
    lvl = np.stack([(level == m).astype(np.float32) for m in range(7)])
    dirm = np.stack([np.stack([(j <= i), (j < i)]), np.stack([(j >= i), (j > i)])]).astype(np.float32)
    tj = (np.arange(N_SPLIT * c) % c)[None, :]
    tri = np.stack([(tj <= i), (tj >= i)]).astype(np.float32)
    return jnp.asarray(bd, BF16), jnp.asarray(lvl), jnp.asarray(dirm), jnp.asarray(tri, BF16)


def _heads_mm(x, y, bd, transpose_rhs=False):
    xb = x.astype(BF16)
    yb = y.astype(BF16)
    outs = []
    for pair in range(GDN_W // GDN_PAIR):
        ls = slice(pair * GDN_PAIR, (pair + 1) * GDN_PAIR)
        w = jnp.concatenate([yb[:, ls], yb[:, ls]], axis=0) * bd
        dims = (((1,), (1,)), ((), ())) if transpose_rhs else (((1,), (0,)), ((), ()))
        outs.append(lax.dot_general(xb[:, ls], w, dims, preferred_element_type=F32))
    return jnp.concatenate(outs, axis=1)


def _gdn_intra(probs, bd, lvl_ref, dirm_ref, tri_ref):
    c = GDN_CHUNK
    n = len(probs)
    eye = lvl_ref[0]
    gc, g_last, decay, gram = [], [], [], []
    for q, k, v, beta, g, rev in probs:
        d = 1 if rev else 0
        gcp = jnp.dot(tri_ref[d], jnp.concatenate(_split(g), axis=0), preferred_element_type=F32)
        gc.append(gcp)
        g_last.append(gcp[0:1, :] if rev else gcp[c - 1:c, :])
        gc_row = jnp.sum(gcp * eye, axis=0, keepdims=True)
        decay.append(dirm_ref[d, 0] * jnp.exp(jnp.minimum(gcp - gc_row, 0.0)))
        gram.append(_heads_mm(jnp.concatenate([k, q], axis=0), k, bd, transpose_rhs=True))
    lower = [dirm_ref[1 if p[5] else 0, 1] * p[3] * gram[x][:c] * decay[x] for x, p in enumerate(probs)]
    a_intra = [gram[x][c:] * decay[x] for x in range(n)]
    t = [eye - lower[x] * lvl_ref[1] for x in range(n)]
    for lev in range(2, 7):
        y = [_heads_mm(t[x], lower[x] * lvl_ref[lev], bd) for x in range(n)]
        z = [_heads_mm(y[x], t[x], bd) for x in range(n)]
        t = [t[x] - z[x] for x in range(n)]
    e_gc = [jnp.exp(gc[x]) for x in range(n)]
    u = [_heads_mm(t[x], p[2] * p[3], bd) for x, p in enumerate(probs)]
    w = [_heads_mm(t[x], p[1] * p[3] * e_gc[x], bd) for x, p in enumerate(probs)]
    wq = [jnp.concatenate([w[x], p[0] * e_gc[x]], axis=0).astype(BF16) for x, p in enumerate(probs)]
    k_dec = [(p[1] * jnp.exp(g_last[x] - gc[x])).astype(BF16) for x, p in enumerate(probs)]
    g_tot = [jnp.exp(g_last[x]) for x in range(n)]
    return u, wq, k_dec, a_intra, g_tot


def _gdn_state_steps(items, bd):
    c = GDN_CHUNK
    pairs = [slice(p * GDN_PAIR, (p + 1) * GDN_PAIR) for p in range(GDN_W // GDN_PAIR)]
    bdf = bd.astype(F32)
    ws_qs = [jnp.concatenate([jnp.dot(wq[:, ls], s_ref[p].astype(BF16), preferred_element_type=F32)
                              for p, ls in enumerate(pairs)], axis=1)
             for _, wq, _, _, _, s_ref in items]
    v_new = [it[0] - ws[:c] for it, ws in zip(items, ws_qs)]
    outs = [ws[c:] + _heads_mm(it[3], vn, bd) for it, ws, vn in zip(items, ws_qs, v_new)]
    for (_, _, k_dec, _, g_tot, s_ref), vn in zip(items, v_new):
        vb = vn.astype(BF16)
        for p, ls in enumerate(pairs):
            kv = lax.dot_general(k_dec[:, ls], vb[:, ls], (((0,), (0,)), ((), ())), preferred_element_type=F32)
            s_ref[p] = s_ref[p] * g_tot[:, ls] + kv * bdf
    return outs


def _gdn_scan_body(qf, kf, vf, bf, gf, qb, kb, vb, bb, gb, bd_ref, lvl_ref, dirm_ref, tri_ref,
                   of_ref, ob_ref, sf_ref, sb_ref):
    bd = bd_ref[...]
    n = SEQ_TILE // GDN_CHUNK
    probs, rows = [], []
    for c in range(n):
        rf = slice(c * GDN_CHUNK, (c + 1) * GDN_CHUNK)
        rb = slice((n - 1 - c) * GDN_CHUNK, (n - c) * GDN_CHUNK)
        probs.append(tuple(r[rf, :].astype(F32) for r in (qf, kf, vf, bf, gf)) + (False,))
        probs.append(tuple(r[rb, :].astype(F32) for r in (qb, kb, vb, bb, gb)) + (True,))
        rows += [rf, rb]
    u, wq, k_dec, a_intra, g_tot = _gdn_intra(probs, bd, lvl_ref, dirm_ref, tri_ref)
    for c in range(n):
        xs = (2 * c, 2 * c + 1)
        outs = _gdn_state_steps([(u[x], wq[x], k_dec[x], a_intra[x], g_tot[x], sb_ref if probs[x][5] else sf_ref)
                                 for x in xs], bd)
        for x, o in zip(xs, outs):
            (ob_ref if probs[x][5] else of_ref)[rows[x], :] = o.astype(of_ref.dtype)


def _seq_tile_maps(n_lat_tiles, n_tiles):
    fwd = lambda s: lax.rem(s + n_lat_tiles, n_tiles)
    bwd = lambda s: n_tiles - 1 - s
    return fwd, bwd


def _gdn_scan_specs(q, k, v, beta, g, n_lat):
    b, s, _ = q.shape
    n_tiles = s // SEQ_TILE
    fwd, bwd = _seq_tile_maps(n_lat // SEQ_TILE, n_tiles)
    spec = lambda tile_of, col: pl.BlockSpec((None, SEQ_TILE, GDN_W), lambda bi, t: (bi, tile_of(t), col))
    masks = _gdn_masks()
    in_specs = [spec(fwd, 0)] * 5 + [spec(bwd, 0)] * 3 + [spec(bwd, 1)] * 2 + [_full(m) for m in masks]
    assert len(in_specs) == N_GDN_IN
    return (in_specs, [q, k, v, beta, g, q, k, v, beta, g, *masks], [spec(fwd, 0), spec(bwd, 0)],
            [jax.ShapeDtypeStruct((b, s, GDN_W), BF16)] * 2,
            [pltpu.VMEM((GDN_W // GDN_PAIR, GDN_PAIR, GDN_PAIR), F32)] * 2)


def _lru_tile_prep(x_ref, xp_ref, xn_ref, cw_ref, cb_ref, wg_ref, bg_ref, nla_ref, a_ref, b_ref,
                   tile_idx, n_tiles, n_lat_tiles):
    xb = _tile_conv(x_ref[...], xp_ref[...], xn_ref[...], cw_ref, tile_idx, n_tiles, n_lat_tiles) + cb_ref[...]
    gates = _sigmoid(jnp.dot(xb.astype(BF16), wg_ref[...], preferred_element_type=F32) + bg_ref[...])
    log_a = nla_ref[...] * gates[:, :LRU_W]
    a_ref[...] = jnp.exp(log_a)
    th = jnp.tanh(log_a)
    b_ref[...] = jnp.sqrt(-2.0 * th / (1.0 - th)) * gates[:, LRU_W:] * xb


def _scan_group(a, b, h, row, reverse):
    for d in (1, 2, 4):
        if reverse:
            keep = row < 8 - d
            shift = 8 - d
        else:
            keep = row >= d
            shift = d
        a_s = jnp.where(keep, pltpu.roll(a, shift, 0), 1.0)
        b_s = jnp.where(keep, pltpu.roll(b, shift, 0), 0.0)
        b = a * b_s + b
        a = a * a_s
    return a * h + b


def _lru_scan_body(xf, xfp, xfn, xb, xbp, xbn, cw_ref, cb_ref, wgf, bgf, nlaf, wgb, bgb, nlab,
                   hf_ref, hb_ref, af_ref, bf_ref, ab_ref, bb_ref, cf_ref, cbk_ref, *, n_lat_tiles):
    s = pl.program_id(1)
    n_tiles = pl.num_programs(1)
    t_f, t_b = (m(s) for m in _seq_tile_maps(n_lat_tiles, n_tiles))
    _lru_tile_prep(xf, xfp, xfn, cw_ref, cb_ref, wgf, bgf, nlaf, af_ref, bf_ref, t_f, n_tiles, n_lat_tiles)
    _lru_tile_prep(xb, xbp, xbn, cw_ref, cb_ref, wgb, bgb, nlab, ab_ref, bb_ref, t_b, n_tiles, n_lat_tiles)

    n_groups = SEQ_TILE // 8
    row = lax.broadcasted_iota(jnp.int32, (8, LRU_W), 0)

    h_f, h_b = cf_ref[...], cbk_ref[...]
    pend_f, pend_b = None, None
    for gi in range(n_groups):
        rf = slice(gi * 8, gi * 8 + 8)
        rb = slice((n_groups - 1 - gi) * 8, (n_groups - gi) * 8)
        out_f = _scan_group(af_ref[rf, :], bf_ref[rf, :], h_f, row, False)
        out_b = _scan_group(ab_ref[rb, :], bb_ref[rb, :], h_b, row, True)
        if gi % 2 == 0:
            pend_f, pend_b = out_f, out_b
        else:
            hf_ref[(gi - 1) * 8:(gi + 1) * 8, :] = jnp.concatenate([pend_f, out_f], axis=0).astype(hf_ref.dtype)
            hb_ref[(n_groups - 1 - gi) * 8:(n_groups + 1 - gi) * 8, :] = (
                jnp.concatenate([out_b, pend_b], axis=0).astype(hb_ref.dtype))
        h_f = jnp.broadcast_to(out_f[7:8, :], (8, LRU_W))
        h_b = jnp.broadcast_to(out_b[0:1, :], (8, LRU_W))
    cf_ref[...] = h_f
    cbk_ref[...] = h_b


N_LRU_IN, N_GDN_IN, N_LRU_SCRATCH = 14, 14, 6


def _seq_mixers_kernel(*refs, n_lat_tiles):
    lru_in = refs[:N_LRU_IN]
    gdn_in = refs[N_LRU_IN:N_LRU_IN + N_GDN_IN]
    hf_ref, hb_ref, of_ref, ob_ref = refs[N_LRU_IN + N_GDN_IN:N_LRU_IN + N_GDN_IN + 4]
    scratch = refs[N_LRU_IN + N_GDN_IN + 4:]
    lru_scratch, gdn_scratch = scratch[:N_LRU_SCRATCH], scratch[N_LRU_SCRATCH:]

    @pl.when(pl.program_id(1) == 0)
    def _():
        for ref in lru_scratch[4:] + gdn_scratch:
            ref[...] = jnp.zeros(ref.shape, ref.dtype)

    _lru_scan_body(*lru_in, hf_ref, hb_ref, *lru_scratch, n_lat_tiles=n_lat_tiles)
    _gdn_scan_body(*gdn_in, of_ref, ob_ref, *gdn_scratch)


def _lru_scan_specs(p, conv_w, conv_b, w_r, b_r, w_i, b_i, lam, n_lat):
    b, s, _ = p.shape
    n_tiles = s // SEQ_TILE
    nlt = n_lat // SEQ_TILE
    fwd, bwd = _seq_tile_maps(nlt, n_tiles)
    col = P_LRU_X // LRU_W
    specs = []
    for tile_of in (fwd, bwd):
        main, prev, make_next = _halo_specs(LRU_W, col, tile_of)
        specs += [pl.BlockSpec((None, SEQ_TILE, LRU_W), main), pl.BlockSpec((None, HALO, LRU_W), prev),
                  pl.BlockSpec((None, HALO, LRU_W), make_next(n_tiles))]

    def blockdiag(w):
        return jax.scipy.linalg.block_diag(*[w[n] for n in range(LRU_BLOCKS)])

    dir_args = []
    for d in range(2):
        wg = jnp.concatenate([blockdiag(w_r[d]), blockdiag(w_i[d])], axis=1).astype(BF16)
        bg = jnp.concatenate([b_r[d], b_i[d]]).astype(F32).reshape(1, 2 * LRU_W)
        nla = (-LRU_C * jax.nn.softplus(-lam[d].astype(F32))).reshape(1, LRU_W)
        dir_args += [wg, bg, nla]
    cb2 = conv_b.reshape(1, LRU_W)
    out_spec = lambda tile_of: pl.BlockSpec((None, SEQ_TILE, LRU_W), lambda bi, t: (bi, tile_of(t), 0))
    in_specs = specs + [_full(conv_w), _full(cb2)] + [_full(a) for a in dir_args]
    assert len(in_specs) == N_LRU_IN
    return (in_specs, [p, p, p, p, p, p, conv_w, cb2, *dir_args], [out_spec(fwd), out_spec(bwd)],
            [jax.ShapeDtypeStruct((b, s, LRU_W), BF16)] * 2,
            [pltpu.VMEM((SEQ_TILE, LRU_W), F32)] * 4 + [pltpu.VMEM((8, LRU_W), F32)] * 2)


def _seq_mixers(lru_parts, gdn_parts, bsz, n_tiles, n_lat):
    in_specs, args, out_specs, out_shape, scratch = ([*a, *b] for a, b in zip(lru_parts, gdn_parts))
    return pl.pallas_call(
        functools.partial(_seq_mixers_kernel, n_lat_tiles=n_lat // SEQ_TILE),
        grid=(bsz, n_tiles),
        in_specs=in_specs,
        out_specs=out_specs,
        out_shape=out_shape,
        scratch_shapes=scratch,
        compiler_params=_cparams(("parallel", "arbitrary")),
        name="seq_mixers",
    )(*args)


def _merge_kernel(x_ref, mc_ref, ml_ref, of_ref, ob_ref, z_ref, hf_ref, hb_ref, y_ref, uc_ref, ud_ref,
                  t0_ref, t1_ref, t2_ref, t3_ref, gn_ref, ones_ref, bg_ref, wb_ref, wo_ref, o_ref, *, n_lat):
    project = lambda u, n: jnp.dot(u, wb_ref[n], preferred_element_type=F32)
    o = of_ref[...].astype(F32) + ob_ref[...].astype(F32)
    ms = jnp.dot(jnp.concatenate(_split(o * o), axis=-1), ones_ref[...], preferred_element_type=F32) * (1.0 / GDN_DV)
    proj = {2: project(uc_ref[...], 2), 3: project(ud_ref[...], 3)}
    z = z_ref[...]
    ua = (o * lax.rsqrt(ms + RMS_EPS) * gn_ref[...] * (z * _sigmoid(z))).astype(BF16)
    y = y_ref[...]
    gelu = 0.5 * y * (1.0 + jnp.tanh(0.7978845608028654 * (y + 0.044715 * (y * y * y))))
    ub = ((hf_ref[...].astype(F32) + hb_ref[...].astype(F32)) * gelu).astype(BF16)
    proj[0] = project(ua, 0)
    proj[1] = project(ub, 1)

    merged = None
    for n, t_ref in enumerate((t0_ref, t1_ref, t2_ref, t3_ref)):
        gate = _sigmoid(t_ref[...] + bg_ref[n:n + 1, :].astype(BF16))
        term = gate.astype(F32) * proj[n]
        merged = term if merged is None else merged + term
    out = jnp.dot(merged.astype(BF16), wo_ref[...], preferred_element_type=F32)
    g1 = _row_select(x_ref.shape[0], pl.program_id(1), n_lat, mc_ref[2:3, :], ml_ref[2:3, :])
    o_ref[...] = x_ref[...] + g1 * out


def _merge(xs, mod_c, mod_l, gdn_out, lru_out, uc, ud, p32, p16, gdn_norm_g, b_gate, w_branch, w_out, layer, n_lat,
           with_ctx):
    b, s, d = xs.shape
    rows = s if with_ctx else n_lat
    tm = _pick_tile(rows, 544)
    gate_blk = P_GATES // d
    tok = lambda w: pl.BlockSpec((None, tm, w), lambda bi, i: (bi, i, 0))
    ones_n = jnp.concatenate([_head_block_ones(GDN_HEADS, GDN_DV)] * N_SPLIT, axis=0)
    gn = jnp.tile(gdn_norm_g.astype(F32), GDN_HEADS).reshape(1, GDN_W)
    pcol = lambda c: pl.BlockSpec((None, tm, BRANCH_W), lambda bi, i: (bi, i, c))
    in_specs = [tok(d), pl.BlockSpec((8, d), lambda bi, i: (0, 0)), pl.BlockSpec((None, 8, d), lambda bi, i: (bi, 0, 0))]
    in_specs += [tok(BRANCH_W), tok(BRANCH_W), pcol(P_GDN_Z // GDN_W), tok(BRANCH_W), tok(BRANCH_W),
                 pcol(P_LRU_Y // LRU_W), tok(BRANCH_W), tok(BRANCH_W)]
    in_specs += [pl.BlockSpec((None, tm, d), functools.partial(lambda bi, i, n: (bi, i, gate_blk + n), n=n))
                 for n in range(N_BRANCH)]
    in_specs += [_full(gn), _full(ones_n), _full(b_gate),
                 pl.BlockSpec((None,) + w_branch.shape[1:], lambda bi, i: (layer, 0, 0, 0)),
                 pl.BlockSpec((None,) + w_out.shape[1:], lambda bi, i: (layer, 0, 0))]
    return pl.pallas_call(
        functools.partial(_merge_kernel, n_lat=n_lat),
        grid=(b, rows // tm),
        in_specs=in_specs,
        out_specs=tok(d),
        out_shape=jax.ShapeDtypeStruct((b, rows, d), F32),
        input_output_aliases={0: 0} if with_ctx else {},
        compiler_params=_cparams(("parallel", "parallel")),
        name="merge",
    )(xs, mod_c, mod_l, gdn_out[0], gdn_out[1], p32, lru_out[0], lru_out[1], p32, uc, ud, p16, p16, p16, p16,
      gn, ones_n, b_gate, w_branch, w_out)


def _mlp_kernel(x_ref, mc_ref, ml_ref, gn_ref, gf_ref, w1_ref, w2_ref, o_ref, h_ref, acc_ref, *, n_lat, final_norm):
    f = pl.program_id(2)
    tm = x_ref.shape[0]
    i = pl.program_id(1)

    @pl.when(f == 0)
    def _():
        _norm_modulate(x_ref, h_ref, gn_ref, mc_ref, ml_ref, 3, i, n_lat)
        acc_ref[...] = jnp.zeros(acc_ref.shape, F32)

    a = jnp.maximum(jnp.dot(h_ref[...], w1_ref[...], preferred_element_type=F32), 0.0)
    acc_ref[...] += jnp.dot((a * a).astype(BF16), w2_ref[...], preferred_element_type=F32)

    @pl.when(f == pl.num_programs(2) - 1)
    def _():
        g2 = _row_select(tm, i, n_lat, mc_ref[5:6, :], ml_ref[5:6, :])
        y = x_ref[...] + g2 * acc_ref[...]
        if final_norm:
            y = y * lax.rsqrt(jnp.mean(y * y, axis=-1, keepdims=True) + RMS_EPS) * gf_ref[...]
        o_ref[...] = y


def _mlp(xs, mod_c, mod_l, gain, w1, w2, layer, final_gain, n_lat, final_norm):
    b, rows, d = xs.shape
    dff = w1.shape[-1]
    tm = _pick_tile(rows, 1088)
    tf = 1024
    row = pl.BlockSpec((1, d), lambda bi, i, f: (0, 0))
    return pl.pallas_call(
        functools.partial(_mlp_kernel, n_lat=n_lat, final_norm=final_norm),
        grid=(b, rows // tm, dff // tf),
        in_specs=[pl.BlockSpec((None, tm, d), lambda bi, i, f: (bi, i, 0)),
                  pl.BlockSpec((8, d), lambda bi, i, f: (0, 0)),
                  pl.BlockSpec((None, 8, d), lambda bi, i, f: (bi, 0, 0)),
                  row, row,
                  pl.BlockSpec((None, d, tf), lambda bi, i, f: (layer, 0, f)),
                  pl.BlockSpec((None, tf, d), lambda bi, i, f: (layer, f, 0))],
        out_specs=pl.BlockSpec((None, tm, d), lambda bi, i, f: (bi, i, 0)),
        out_shape=jax.ShapeDtypeStruct((b, rows, d), F32),
        scratch_shapes=[pltpu.VMEM((tm, d), BF16), pltpu.VMEM((tm, d), F32)],
        compiler_params=_cparams(("parallel", "parallel", "arbitrary")),
        name="mlp",
    )(xs, mod_c, mod_l, gain, final_gain, w1, w2)


def kernel(x, c, ctx, c_ctx, mod_w, mod_b, norm1_g, norm2_g, w_in, b_gate, gdn_conv_w, gdn_a_log, gdn_dt_bias,
           gdn_norm_g, lru_conv_w, lru_conv_b, lru_w_r, lru_b_r, lru_w_i, lru_b_i, lru_lambda, mla_q_norm_g,
           mla_w_uq, mla_kv_norm_g, mla_w_ukv, na_rpb, w_branch, w_out, mlp_w1, mlp_w2, final_norm_g):
    bsz, n_tok, d = x.shape
    n_ctx = ctx.shape[1]
    depth = w_in.shape[0]
    assert n_ctx % SEQ_TILE == 0 and n_tok % SEQ_TILE == 0 and n_tok % GRID_W == 0
    na_meta, na_rel_rows, na_cols = _na_geometry(n_tok // GRID_W)
    na_meta = jnp.asarray(na_meta)
    cos, sin = _rope_tables(n_tok, n_ctx)

    n_rows = -(-(bsz + 1) // 8) * 8
    cc = jnp.zeros((n_rows, d), F32).at[:bsz].set(c).at[bsz].set(c_ctx)
    final_gain = final_norm_g.reshape(1, d)

    w_in_all = _arrange_w_in(w_in)
    wb_all = w_branch.astype(BF16)
    wo_all = w_out.astype(BF16)
    w1_all = mlp_w1.astype(BF16)
    w2_all = mlp_w2.astype(BF16)

    xs = jnp.concatenate([x, ctx], axis=1)
    for l in range(depth):
        need_ctx = l < depth - 1
        mod = _modulation(cc, mod_w, mod_b[l], l).reshape(n_rows, N_MOD, d)
        pad = jnp.zeros((8 - N_MOD, d), F32)
        mod_c = jnp.concatenate([mod[bsz], pad], axis=0)
        mod_l = jnp.concatenate([mod[:bsz], jnp.broadcast_to(pad, (bsz, 8 - N_MOD, d))], axis=1)

        wq, wk, wv, place = _arrange_mla(mla_w_uq[l], mla_w_ukv[l])
        gq = mla_q_norm_g[l].reshape(1, -1)
        gkv = mla_kv_norm_g[l].reshape(1, -1)
        g1n = norm1_g[l].reshape(1, d)
        g2n = norm2_g[l].reshape(1, d)

        p32, p16 = _inproj(xs, mod_c, mod_l, g1n, w_in_all, l, n_tok)

        gq_, gk_, gv_, gbeta, gg = _gdn_prep(p32, gdn_conv_w[l], gdn_a_log[l], gdn_dt_bias[l], n_tok)
        h_f, h_b, o_f, o_b = _seq_mixers(
            _lru_scan_specs(p32, lru_conv_w[l], lru_conv_b[l], lru_w_r[l], lru_b_r[l], lru_w_i[l], lru_b_i[l],
                            lru_lambda[l], n_tok),
            _gdn_scan_specs(gq_, gk_, gv_, gbeta, gg, n_tok), bsz, (n_tok + n_ctx) // SEQ_TILE, n_tok)

        mq, mk, mv = _mla_prep(p32, p16, cos, sin, gq, gkv, wq, wk, wv, place)
        uc = _mla_flash(mq, mk, mv, n_tok)
        if need_ctx:
            uc = _mla_flash(mq, mk, mv, n_tok, prev=uc)
        ud = _na_attention(p16, na_meta, _na_bias_table(na_rpb[l], na_rel_rows, na_cols), n_tok, need_ctx)

        xs = _merge(xs, mod_c, mod_l, (o_f, o_b), (h_f, h_b), uc, ud, p32, p16, gdn_norm_g[l], b_gate[l],
                    wb_all, wo_all, l, n_tok, need_ctx)
        xs = _mlp(xs, mod_c, mod_l, g2n, w1_all, w2_all, l, final_gain, n_tok, l == depth - 1)
    return xs
```
